```python
import jax
import jax.numpy as jnp
from jax import lax
import numpy as np

D_MODEL = 2048
BATCH = 8
SEQ = 8192
DEPTH = 1

CTX_LEN = 256
GRID_W = 64
HEAD_DIM = 128
ATTN_Q_HEADS = 8
ATTN_KV_HEADS = 2
ATTN_GROUPS = ATTN_Q_HEADS // ATTN_KV_HEADS
ATTN_WIDTH = ATTN_Q_HEADS * HEAD_DIM
KV_WIDTH = ATTN_KV_HEADS * HEAD_DIM
RET_HEADS = 8
RET_QK_DIM = 128
RET_V_DIM = 128
RET_QK_WIDTH = RET_HEADS * RET_QK_DIM
RET_V_WIDTH = RET_HEADS * RET_V_DIM
D_FF = 5632
Q_BLOCK = 128
RET_CHUNK = 128
ROPE_THETA = 10000.0
NORM_EPS = 1e-6
N_MOD = 9
PROJ_SPLITS = (ATTN_WIDTH, KV_WIDTH, KV_WIDTH, RET_QK_WIDTH, RET_QK_WIDTH, RET_V_WIDTH, RET_V_WIDTH, D_MODEL, D_MODEL)
PROJ_WIDTH = ATTN_WIDTH + 2 * KV_WIDTH + 2 * RET_QK_WIDTH + 2 * RET_V_WIDTH + 2 * D_MODEL

kernel_name = "hybrid_gqa_retention_macaron_dit_block"


def _rms(x):
    xf = x.astype(jnp.float32)
    return (xf * lax.rsqrt(jnp.mean(xf * xf, axis=-1, keepdims=True) + NORM_EPS)).astype(x.dtype)


def _modulate(n, shift, scale):
    return n * (1 + scale) + shift


def _adaln(cond, w_ada, b_ada):
    return jnp.split(jax.nn.silu(cond) @ w_ada + b_ada, N_MOD, axis=-1)


def _swiglu(x, w_in, w_out):
    a, b = jnp.split(x @ w_in, 2, axis=-1)
    return (jax.nn.silu(a) * b) @ w_out


def _split_proj(p):
    cuts = []
    acc = 0
    for w in PROJ_SPLITS[:-1]:
        acc += w
        cuts.append(acc)
    return jnp.split(p, cuts, axis=-1)


def _heads(t, n_heads, d):
    return t.reshape(t.shape[:2] + (n_heads, d))


def _grid_rope(seq_len):
    rows = seq_len // GRID_W
    row = jnp.repeat(jnp.arange(rows, dtype=jnp.float32), GRID_W)
    col = jnp.tile(jnp.arange(GRID_W, dtype=jnp.float32), rows)
    half = HEAD_DIM // 2
    inv_freq = ROPE_THETA ** (-jnp.arange(0, half, 2, dtype=jnp.float32) / half)
    ang = jnp.concatenate([row[:, None] * inv_freq, col[:, None] * inv_freq], axis=-1)
    return jnp.cos(ang), jnp.sin(ang)


def _apply_rope(x, cos, sin):
    xf = x.astype(jnp.float32).reshape(x.shape[:-1] + (HEAD_DIM // 2, 2))
    x0, x1 = xf[..., 0], xf[..., 1]
    c, s = cos[:, None, :], sin[:, None, :]
    out = jnp.stack([x0 * c - x1 * s, x0 * s + x1 * c], axis=-1)
    return out.reshape(x.shape).astype(x.dtype)


def _attend_blocks(q, k, v):
    b, l = q.shape[:2]
    nb = l // Q_BLOCK
    qb = jnp.moveaxis(q.reshape(b, nb, Q_BLOCK, ATTN_KV_HEADS, ATTN_GROUPS, HEAD_DIM), 1, 0)
    scale = HEAD_DIM ** -0.5

    def one_block(qblk):
        s = jnp.einsum("bqkgd,bskd->bkgqs", qblk, k).astype(jnp.float32) * scale
        p = jax.nn.softmax(s, axis=-1)
        return jnp.einsum("bkgqs,bskd->bqkgd", p.astype(v.dtype), v)

    out = lax.map(one_block, qb)
    return jnp.moveaxis(out, 0, 1).reshape(b, l, ATTN_WIDTH)


def _retention_chunkwise(q, k, v, log_gamma, state0):
    b, h, l, _ = q.shape
    n = l // RET_CHUNK
    idx = jnp.arange(RET_CHUNK, dtype=jnp.float32)
    diff = idx[:, None] - idx[None, :]
    lower = diff >= 0
    intra = jnp.where(lower[None], jnp.exp(jnp.where(lower, diff, 0.0)[None] * log_gamma[:, None, None]), 0.0)
    q_dec = jnp.exp((idx + 1.0)[None, :] * log_gamma[:, None])
    k_dec = jnp.exp((RET_CHUNK - 1.0 - idx)[None, :] * log_gamma[:, None])
    chunk_dec = jnp.exp(RET_CHUNK * log_gamma)

    def chunks(t):
        return jnp.moveaxis(t.reshape(b, h, n, RET_CHUNK, t.shape[-1]), 2, 0)

    def step(state, qkv):
        qc, kc, vc = qkv
        inner = jnp.einsum("bhid,bhjd->bhij", qc, kc) * intra
        y = jnp.einsum("bhij,bhje->bhie", inner, vc) + jnp.einsum("bhid,bhde->bhie", qc, state) * q_dec[..., None]
        state = state * chunk_dec[:, None, None] + jnp.einsum("bhjd,bhje->bhde", kc * k_dec[..., None], vc)
        return state, y

    _, ys = lax.scan(step, state0, (chunks(q), chunks(k), chunks(v)))
    return jnp.moveaxis(ys, 0, 2).reshape(b, h, l, v.shape[-1])


def _retention_state(k, v, log_gamma, reverse):
    l = k.shape[2]
    m = jnp.arange(l, dtype=jnp.float32)
    expo = m if reverse else (l - 1.0 - m)
    w = jnp.exp(expo[None, :] * log_gamma[:, None])
    return jnp.einsum("bhld,bhle,hl->bhde", k, v, w)


def _retention_bidir(q, k, v, lg_fwd, lg_bwd, state_fwd, state_bwd):
    flip = lambda t: jnp.flip(t, axis=2)
    y_f = _retention_chunkwise(q, k, v, lg_fwd, state_fwd)
    y_b = flip(_retention_chunkwise(flip(q), flip(k), flip(v), lg_bwd, state_bwd))
    return y_f + y_b


def _ret_heads(t, d):
    return jnp.transpose(_heads(t, RET_HEADS, d), (0, 2, 1, 3)).astype(jnp.float32)


def _retention_out(y, gate):
    b, h, l, d = y.shape
    y = jnp.transpose(_rms(y), (0, 2, 1, 3)).reshape(b, l, h * d).astype(gate.dtype)
    return jax.nn.silu(gate) * y


def _merge(y_attn, y_ret, g_attn, g_ret, w_proj_attn, w_proj_ret, w_out):
    return (jax.nn.sigmoid(g_attn) * (y_attn @ w_proj_attn) + jax.nn.sigmoid(g_ret) * (y_ret @ w_proj_ret)) @ w_out


def _mixer(n_x, n_c, w_in, q_gain, k_gain, decay_logit, w_proj_attn, w_proj_ret, w_out, with_ctx_out):
    seq_len = n_x.shape[1]
    qa_x, ka_x, va_x, qr_x, kr_x, vr_x, gr_x, ga_x, gb_x = _split_proj(n_x @ w_in)
    qa_c, ka_c, va_c, qr_c, kr_c, vr_c, gr_c, ga_c, gb_c = _split_proj(n_c @ w_in)

    cos, sin = _grid_rope(seq_len)
    q_x = _apply_rope(_rms(_heads(qa_x, ATTN_Q_HEADS, HEAD_DIM)) * q_gain, cos, sin)
    k_x = _apply_rope(_rms(_heads(ka_x, ATTN_KV_HEADS, HEAD_DIM)) * k_gain, cos, sin)
    k_c = _rms(_heads(ka_c, ATTN_KV_HEADS, HEAD_DIM)) * k_gain
    v_x = _heads(va_x, ATTN_KV_HEADS, HEAD_DIM)
    v_c = _heads(va_c, ATTN_KV_HEADS, HEAD_DIM)
    k_all = jnp.concatenate([k_c, k_x], axis=1)
    v_all = jnp.concatenate([v_c, v_x], axis=1)
    ya_x = _attend_blocks(q_x, k_all, v_all)

    log_gamma = jax.nn.log_sigmoid(decay_logit.astype(jnp.float32))
    k_scale = RET_QK_DIM ** -0.5
    qr_xh, kr_xh, vr_xh = _ret_heads(qr_x, RET_QK_DIM), _ret_heads(kr_x, RET_QK_DIM) * k_scale, _ret_heads(vr_x, RET_V_DIM)
    kr_ch, vr_ch = _ret_heads(kr_c, RET_QK_DIM) * k_scale, _ret_heads(vr_c, RET_V_DIM)
    state_f = _retention_state(kr_ch, vr_ch, log_gamma[0], False)
    state_b = _retention_state(kr_ch, vr_ch, log_gamma[1], True)
    yr_x = _retention_out(_retention_bidir(qr_xh, kr_xh, vr_xh, log_gamma[0], log_gamma[1], state_f, state_b), gr_x)

    out_x = _merge(ya_x, yr_x, ga_x, gb_x, w_proj_attn, w_proj_ret, w_out)
    if not with_ctx_out:
        return out_x, None

    q_c = _rms(_heads(qa_c, ATTN_Q_HEADS, HEAD_DIM)) * q_gain
    ya_c = _attend_blocks(q_c, k_c, v_c)
    zeros = jnp.zeros_like(state_f)
    qr_ch = _ret_heads(qr_c, RET_QK_DIM)
    yr_c = _retention_out(_retention_bidir(qr_ch, kr_ch, vr_ch, log_gamma[0], log_gamma[1], zeros, zeros), gr_c)
    out_c = _merge(ya_c, yr_c, ga_c, gb_c, w_proj_attn, w_proj_ret, w_out)
    return out_x, out_c


def _fwd_setup_inputs(seed: int = 0) -> dict:
    key = jax.random.key(seed)
    ks = jax.random.split(key, 18)

    def nrm(k, shape, std):
        return jax.random.normal(k, shape, jnp.float32) * std

    heads_idx = jnp.arange(RET_HEADS, dtype=jnp.float32)
    decay_base = jnp.log1p(-(2.0 ** (-(5.0 + heads_idx)))) + (5.0 + heads_idx) * jnp.log(2.0)
    return {
        "x": nrm(ks[0], (BATCH, SEQ, D_MODEL), 1.0),
        "c": nrm(ks[1], (BATCH, D_MODEL), 1.0),
        "ctx": nrm(ks[2], (BATCH, CTX_LEN, D_MODEL), 1.0),
        "c_ctx": nrm(ks[3], (D_MODEL,), 1.0),
        "w_ada": nrm(ks[4], (DEPTH, D_MODEL, N_MOD * D_MODEL), 0.5 * D_MODEL ** -0.5),
        "b_ada": nrm(ks[5], (DEPTH, N_MOD * D_MODEL), 0.01),
        "ffn1_w_in": nrm(ks[6], (DEPTH, D_MODEL, 2 * D_FF), D_MODEL ** -0.5),
        "ffn1_w_out": nrm(ks[7], (DEPTH, D_FF, D_MODEL), D_FF ** -0.5),
        "mix_w_in": nrm(ks[8], (DEPTH, D_MODEL, PROJ_WIDTH), D_MODEL ** -0.5),
        "attn_q_gain": 1.0 + nrm(ks[9], (DEPTH, HEAD_DIM), 0.02),
        "attn_k_gain": 1.0 + nrm(ks[10], (DEPTH, HEAD_DIM), 0.02),
        "ret_decay_logit": decay_base[None, None, :] + nrm(ks[11], (DEPTH, 2, RET_HEADS), 0.1),
        "w_proj_attn": nrm(ks[12], (DEPTH, ATTN_WIDTH, D_MODEL), ATTN_WIDTH ** -0.5),
        "w_proj_ret": nrm(ks[13], (DEPTH, RET_V_WIDTH, D_MODEL), RET_V_WIDTH ** -0.5),
        "mix_w_out": nrm(ks[14], (DEPTH, D_MODEL, D_MODEL), D_MODEL ** -0.5),
        "ffn2_w_in": nrm(ks[15], (DEPTH, D_MODEL, 2 * D_FF), D_MODEL ** -0.5),
        "ffn2_w_out": nrm(ks[16], (DEPTH, D_FF, D_MODEL), D_FF ** -0.5),
        "final_norm": 1.0 + nrm(ks[17], (D_MODEL,), 0.02),
    }


def _fwd_reference(x, c, ctx, c_ctx, w_ada, b_ada, ffn1_w_in, ffn1_w_out, mix_w_in, attn_q_gain, attn_k_gain,
              ret_decay_logit, w_proj_attn, w_proj_ret, mix_w_out, ffn2_w_in, ffn2_w_out, final_norm):
    h_x = x
    h_c = ctx
    for layer in range(DEPTH):
        last = layer == DEPTH - 1
        sh1, sc1, g1, sh2, sc2, g2, sh3, sc3, g3 = [t[:, None, :] for t in _adaln(c, w_ada[layer], b_ada[layer])]
        csh1, csc1, cg1, csh2, csc2, cg2, csh3, csc3, cg3 = _adaln(c_ctx, w_ada[layer], b_ada[layer])

        h_x = h_x + 0.5 * g1 * _swiglu(_modulate(_rms(h_x), sh1, sc1), ffn1_w_in[layer], ffn1_w_out[layer])
        h_c = h_c + 0.5 * cg1 * _swiglu(_modulate(_rms(h_c), csh1, csc1), ffn1_w_in[layer], ffn1_w_out[layer])

        y_x, y_c = _mixer(_modulate(_rms(h_x), sh2, sc2), _modulate(_rms(h_c), csh2, csc2),
                          mix_w_in[layer], attn_q_gain[layer], attn_k_gain[layer], ret_decay_logit[layer],
                          w_proj_attn[layer], w_proj_ret[layer], mix_w_out[layer], not last)
        h_x = h_x + g2 * y_x

        h_x = h_x + 0.5 * g3 * _swiglu(_modulate(_rms(h_x), sh3, sc3), ffn2_w_in[layer], ffn2_w_out[layer])
        if not last:
            h_c = h_c + cg2 * y_c
            h_c = h_c + 0.5 * cg3 * _swiglu(_modulate(_rms(h_c), csh3, csc3), ffn2_w_in[layer], ffn2_w_out[layer])
    return _rms(h_x) * final_norm


import jax as _jax
import jax.numpy as _jnp

TWIN_FORMAT = 'train_step'
FWD_PARAMS = ['x', 'c', 'ctx', 'c_ctx', 'w_ada', 'b_ada', 'ffn1_w_in', 'ffn1_w_out', 'mix_w_in', 'attn_q_gain', 'attn_k_gain', 'ret_decay_logit', 'w_proj_attn', 'w_proj_ret', 'mix_w_out', 'ffn2_w_in', 'ffn2_w_out', 'final_norm']
TWIN_WEIGHTS = ['c_ctx', 'w_ada', 'b_ada', 'ffn1_w_in', 'ffn1_w_out', 'mix_w_in', 'attn_q_gain', 'attn_k_gain', 'ret_decay_logit', 'w_proj_attn', 'w_proj_ret', 'mix_w_out', 'ffn2_w_in', 'ffn2_w_out', 'final_norm']
TWIN_DIFF_INPUT = 'x'
TWIN_INPUTS = ['x', 'c', 'ctx', 'c_ctx', 'w_ada', 'b_ada', 'ffn1_w_in', 'ffn1_w_out', 'mix_w_in', 'attn_q_gain', 'attn_k_gain', 'ret_decay_logit', 'w_proj_attn', 'w_proj_ret', 'mix_w_out', 'ffn2_w_in', 'ffn2_w_out', 'final_norm', 'loss_target', 'm_c_ctx', 'm_w_ada', 'm_b_ada', 'm_ffn1_w_in', 'm_ffn1_w_out', 'm_mix_w_in', 'm_attn_q_gain', 'm_attn_k_gain', 'm_ret_decay_logit', 'm_w_proj_attn', 'm_w_proj_ret', 'm_mix_w_out', 'm_ffn2_w_in', 'm_ffn2_w_out', 'm_final_norm', 'v_c_ctx', 'v_w_ada', 'v_b_ada', 'v_ffn1_w_in', 'v_ffn1_w_out', 'v_mix_w_in', 'v_attn_q_gain', 'v_attn_k_gain', 'v_ret_decay_logit', 'v_w_proj_attn', 'v_w_proj_ret', 'v_mix_w_out', 'v_ffn2_w_in', 'v_ffn2_w_out', 'v_final_norm']
TWIN_OUTPUTS = ['loss', 'grad_x', 'grad_c_ctx', 'grad_w_ada', 'grad_b_ada', 'grad_ffn1_w_in', 'grad_ffn1_w_out', 'grad_mix_w_in', 'grad_attn_q_gain', 'grad_attn_k_gain', 'grad_ret_decay_logit', 'grad_w_proj_attn', 'grad_w_proj_ret', 'grad_mix_w_out', 'grad_ffn2_w_in', 'grad_ffn2_w_out', 'grad_final_norm', 'delta_c_ctx', 'delta_w_ada', 'delta_b_ada', 'delta_ffn1_w_in', 'delta_ffn1_w_out', 'delta_mix_w_in', 'delta_attn_q_gain', 'delta_attn_k_gain', 'delta_ret_decay_logit', 'delta_w_proj_attn', 'delta_w_proj_ret', 'delta_mix_w_out', 'delta_ffn2_w_in', 'delta_ffn2_w_out', 'delta_final_norm', 'new_m_c_ctx', 'new_m_w_ada', 'new_m_b_ada', 'new_m_ffn1_w_in', 'new_m_ffn1_w_out', 'new_m_mix_w_in', 'new_m_attn_q_gain', 'new_m_attn_k_gain', 'new_m_ret_decay_logit', 'new_m_w_proj_attn', 'new_m_w_proj_ret', 'new_m_mix_w_out', 'new_m_ffn2_w_in', 'new_m_ffn2_w_out', 'new_m_final_norm', 'new_v_c_ctx', 'new_v_w_ada', 'new_v_b_ada', 'new_v_ffn1_w_in', 'new_v_ffn1_w_out', 'new_v_mix_w_in', 'new_v_attn_q_gain', 'new_v_attn_k_gain', 'new_v_ret_decay_logit', 'new_v_w_proj_attn', 'new_v_w_proj_ret', 'new_v_mix_w_out', 'new_v_ffn2_w_in', 'new_v_ffn2_w_out', 'new_v_final_norm']
TWIN_LEAF_KINDS = {'loss': 'loss', 'grad_x': 'grad_x', 'grad_c_ctx': 'grad_w', 'grad_w_ada': 'grad_w', 'grad_b_ada': 'grad_w', 'grad_ffn1_w_in': 'grad_w', 'grad_ffn1_w_out': 'grad_w', 'grad_mix_w_in': 'grad_w', 'grad_attn_q_gain': 'grad_w', 'grad_attn_k_gain': 'grad_w', 'grad_ret_decay_logit': 'grad_w', 'grad_w_proj_attn': 'grad_w', 'grad_w_proj_ret': 'grad_w', 'grad_mix_w_out': 'grad_w', 'grad_ffn2_w_in': 'grad_w', 'grad_ffn2_w_out': 'grad_w', 'grad_final_norm': 'grad_w', 'delta_c_ctx': 'delta_w', 'delta_w_ada': 'delta_w', 'delta_b_ada': 'delta_w', 'delta_ffn1_w_in': 'delta_w', 'delta_ffn1_w_out': 'delta_w', 'delta_mix_w_in': 'delta_w', 'delta_attn_q_gain': 'delta_w', 'delta_attn_k_gain': 'delta_w', 'delta_ret_decay_logit': 'delta_w', 'delta_w_proj_attn': 'delta_w', 'delta_w_proj_ret': 'delta_w', 'delta_mix_w_out': 'delta_w', 'delta_ffn2_w_in': 'delta_w', 'delta_ffn2_w_out': 'delta_w', 'delta_final_norm': 'delta_w', 'new_m_c_ctx': 'new_m', 'new_m_w_ada': 'new_m', 'new_m_b_ada': 'new_m', 'new_m_ffn1_w_in': 'new_m', 'new_m_ffn1_w_out': 'new_m', 'new_m_mix_w_in': 'new_m', 'new_m_attn_q_gain': 'new_m', 'new_m_attn_k_gain': 'new_m', 'new_m_ret_decay_logit': 'new_m', 'new_m_w_proj_attn': 'new_m', 'new_m_w_proj_ret': 'new_m', 'new_m_mix_w_out': 'new_m', 'new_m_ffn2_w_in': 'new_m', 'new_m_ffn2_w_out': 'new_m', 'new_m_final_norm': 'new_m', 'new_v_c_ctx': 'new_v', 'new_v_w_ada': 'new_v', 'new_v_b_ada': 'new_v', 'new_v_ffn1_w_in': 'new_v', 'new_v_ffn1_w_out': 'new_v', 'new_v_mix_w_in': 'new_v', 'new_v_attn_q_gain': 'new_v', 'new_v_attn_k_gain': 'new_v', 'new_v_ret_decay_logit': 'new_v', 'new_v_w_proj_attn': 'new_v', 'new_v_w_proj_ret': 'new_v', 'new_v_mix_w_out': 'new_v', 'new_v_ffn2_w_in': 'new_v', 'new_v_ffn2_w_out': 'new_v', 'new_v_final_norm': 'new_v'}


def _forward(args):
    return _fwd_reference(*[args[k] for k in FWD_PARAMS])


def _output_shape():
    def fwd():
        inp = _fwd_setup_inputs(0)
        return _fwd_reference(*[inp[k] for k in FWD_PARAMS])
    out = _jax.eval_shape(fwd)
    return out.shape, out.dtype

N_MICROBATCH = 1
ADAM_LR = 0.001
ADAM_B1 = 0.9
ADAM_B2 = 0.999
ADAM_EPS = 1e-08
ADAM_WD = 0.01
ADAM_STEP = 10
PER_EXAMPLE_BATCH_AXIS = {'x': 0, 'c': 0, 'ctx': 0, 'loss_target': 0}
SHARED_INPUTS = []
_WEIGHT_DTYPES = {'c_ctx': _jnp.float32, 'w_ada': _jnp.float32, 'b_ada': _jnp.float32, 'ffn1_w_in': _jnp.float32, 'ffn1_w_out': _jnp.float32, 'mix_w_in': _jnp.float32, 'attn_q_gain': _jnp.float32, 'attn_k_gain': _jnp.float32, 'ret_decay_logit': _jnp.float32, 'w_proj_attn': _jnp.float32, 'w_proj_ret': _jnp.float32, 'mix_w_out': _jnp.float32, 'ffn2_w_in': _jnp.float32, 'ffn2_w_out': _jnp.float32, 'final_norm': _jnp.float32}
MOMENT_SCALE = {'c_ctx': 1.028327e-02, 'w_ada': 2.283128e-02, 'b_ada': 3.634411e-02, 'ffn1_w_in': 8.260083e-03, 'ffn1_w_out': 1.349439e-02, 'mix_w_in': 1.456358e-02, 'attn_q_gain': 5.860196e-03, 'attn_k_gain': 5.674641e-03, 'ret_decay_logit': 4.821835e-02, 'w_proj_attn': 5.967781e-03, 'w_proj_ret': 1.336719e-02, 'mix_w_out': 1.449321e-02, 'ffn2_w_in': 8.132865e-03, 'ffn2_w_out': 1.325360e-02, 'final_norm': 3.197468e+01}


def _to_microbatches(a, axis):
    t = _jnp.moveaxis(a, axis, 0)
    t = t.reshape((N_MICROBATCH, t.shape[0] // N_MICROBATCH) + t.shape[1:])
    return _jnp.moveaxis(t, 1, axis + 1)


def setup_inputs(seed: int = 0) -> dict:
    inp = _fwd_setup_inputs(seed)
    key = _jax.random.fold_in(_jax.random.key(seed), 7919)
    shape, _ = _output_shape()
    out = dict(inp)
    out["loss_target"] = _jax.random.normal(_jax.random.fold_in(key, 0), shape, _jnp.float32)
    for i, name in enumerate(TWIN_WEIGHTS):
        w = inp[name].astype(_jnp.float32)
        if MOMENT_SCALE is None:
            s = _jnp.sqrt(_jnp.mean(_jnp.square(w)) + 1e-30)
        else:
            s = MOMENT_SCALE[name]
        km, kv = _jax.random.split(_jax.random.fold_in(key, i + 1))
        out[name] = w
        out["m_" + name] = s * _jax.random.normal(km, w.shape, _jnp.float32)
        out["v_" + name] = (s * s) * _jax.random.uniform(kv, w.shape, _jnp.float32, 0.5, 1.5)
    if N_MICROBATCH > 1:
        for name, axis in PER_EXAMPLE_BATCH_AXIS.items():
            out[name] = _to_microbatches(out[name], axis)
    return {'x': out['x'], 'c': out['c'], 'ctx': out['ctx'], 'c_ctx': out['c_ctx'], 'w_ada': out['w_ada'], 'b_ada': out['b_ada'], 'ffn1_w_in': out['ffn1_w_in'], 'ffn1_w_out': out['ffn1_w_out'], 'mix_w_in': out['mix_w_in'], 'attn_q_gain': out['attn_q_gain'], 'attn_k_gain': out['attn_k_gain'], 'ret_decay_logit': out['ret_decay_logit'], 'w_proj_attn': out['w_proj_attn'], 'w_proj_ret': out['w_proj_ret'], 'mix_w_out': out['mix_w_out'], 'ffn2_w_in': out['ffn2_w_in'], 'ffn2_w_out': out['ffn2_w_out'], 'final_norm': out['final_norm'], 'loss_target': out['loss_target'], 'm_c_ctx': out['m_c_ctx'], 'm_w_ada': out['m_w_ada'], 'm_b_ada': out['m_b_ada'], 'm_ffn1_w_in': out['m_ffn1_w_in'], 'm_ffn1_w_out': out['m_ffn1_w_out'], 'm_mix_w_in': out['m_mix_w_in'], 'm_attn_q_gain': out['m_attn_q_gain'], 'm_attn_k_gain': out['m_attn_k_gain'], 'm_ret_decay_logit': out['m_ret_decay_logit'], 'm_w_proj_attn': out['m_w_proj_attn'], 'm_w_proj_ret': out['m_w_proj_ret'], 'm_mix_w_out': out['m_mix_w_out'], 'm_ffn2_w_in': out['m_ffn2_w_in'], 'm_ffn2_w_out': out['m_ffn2_w_out'], 'm_final_norm': out['m_final_norm'], 'v_c_ctx': out['v_c_ctx'], 'v_w_ada': out['v_w_ada'], 'v_b_ada': out['v_b_ada'], 'v_ffn1_w_in': out['v_ffn1_w_in'], 'v_ffn1_w_out': out['v_ffn1_w_out'], 'v_mix_w_in': out['v_mix_w_in'], 'v_attn_q_gain': out['v_attn_q_gain'], 'v_attn_k_gain': out['v_attn_k_gain'], 'v_ret_decay_logit': out['v_ret_decay_logit'], 'v_w_proj_attn': out['v_w_proj_attn'], 'v_w_proj_ret': out['v_w_proj_ret'], 'v_mix_w_out': out['v_mix_w_out'], 'v_ffn2_w_in': out['v_ffn2_w_in'], 'v_ffn2_w_out': out['v_ffn2_w_out'], 'v_final_norm': out['v_final_norm']}


def _loss(weights, diff, rest, loss_target):
    with _jax.named_scope("forward"):
        args = {**rest, TWIN_DIFF_INPUT: diff, **{k: w.astype(_WEIGHT_DTYPES[k]) for k, w in weights.items()}}
        y = _forward(args)
    with _jax.named_scope("loss_head"):
        err = _jnp.square(y.astype(_jnp.float32) - loss_target)
        return 0.5 * _jnp.sum(_jnp.mean(err, axis=-1)) if err.ndim else 0.5 * err


def _adamw(w, g, m, v):
    m = ADAM_B1 * m + (1.0 - ADAM_B1) * g
    v = ADAM_B2 * v + (1.0 - ADAM_B2) * _jnp.square(g)
    m_hat = m / (1.0 - ADAM_B1 ** ADAM_STEP)
    v_hat = v / (1.0 - ADAM_B2 ** ADAM_STEP)
    delta = -ADAM_LR * (m_hat / (_jnp.sqrt(v_hat) + ADAM_EPS) + ADAM_WD * w)
    return delta, m, v


def reference(x, c, ctx, c_ctx, w_ada, b_ada, ffn1_w_in, ffn1_w_out, mix_w_in, attn_q_gain, attn_k_gain, ret_decay_logit, w_proj_attn, w_proj_ret, mix_w_out, ffn2_w_in, ffn2_w_out, final_norm, loss_target, m_c_ctx, m_w_ada, m_b_ada, m_ffn1_w_in, m_ffn1_w_out, m_mix_w_in, m_attn_q_gain, m_attn_k_gain, m_ret_decay_logit, m_w_proj_attn, m_w_proj_ret, m_mix_w_out, m_ffn2_w_in, m_ffn2_w_out, m_final_norm, v_c_ctx, v_w_ada, v_b_ada, v_ffn1_w_in, v_ffn1_w_out, v_mix_w_in, v_attn_q_gain, v_attn_k_gain, v_ret_decay_logit, v_w_proj_attn, v_w_proj_ret, v_mix_w_out, v_ffn2_w_in, v_ffn2_w_out, v_final_norm):
    given = dict(x=x, c=c, ctx=ctx, c_ctx=c_ctx, w_ada=w_ada, b_ada=b_ada, ffn1_w_in=ffn1_w_in, ffn1_w_out=ffn1_w_out, mix_w_in=mix_w_in, attn_q_gain=attn_q_gain, attn_k_gain=attn_k_gain, ret_decay_logit=ret_decay_logit, w_proj_attn=w_proj_attn, w_proj_ret=w_proj_ret, mix_w_out=mix_w_out, ffn2_w_in=ffn2_w_in, ffn2_w_out=ffn2_w_out, final_norm=final_norm, loss_target=loss_target, m_c_ctx=m_c_ctx, m_w_ada=m_w_ada, m_b_ada=m_b_ada, m_ffn1_w_in=m_ffn1_w_in, m_ffn1_w_out=m_ffn1_w_out, m_mix_w_in=m_mix_w_in, m_attn_q_gain=m_attn_q_gain, m_attn_k_gain=m_attn_k_gain, m_ret_decay_logit=m_ret_decay_logit, m_w_proj_attn=m_w_proj_attn, m_w_proj_ret=m_w_proj_ret, m_mix_w_out=m_mix_w_out, m_ffn2_w_in=m_ffn2_w_in, m_ffn2_w_out=m_ffn2_w_out, m_final_norm=m_final_norm, v_c_ctx=v_c_ctx, v_w_ada=v_w_ada, v_b_ada=v_b_ada, v_ffn1_w_in=v_ffn1_w_in, v_ffn1_w_out=v_ffn1_w_out, v_mix_w_in=v_mix_w_in, v_attn_q_gain=v_attn_q_gain, v_attn_k_gain=v_attn_k_gain, v_ret_decay_logit=v_ret_decay_logit, v_w_proj_attn=v_w_proj_attn, v_w_proj_ret=v_w_proj_ret, v_mix_w_out=v_mix_w_out, v_ffn2_w_in=v_ffn2_w_in, v_ffn2_w_out=v_ffn2_w_out, v_final_norm=v_final_norm)
    weights = {n: given[n] for n in TWIN_WEIGHTS}
    shared = {n: given[n] for n in SHARED_INPUTS}
    per_example = {n: given[n] for n in ['x', 'c', 'ctx']}
    grad_fn = _jax.value_and_grad(_loss, argnums=(0, 1))

    def one_microbatch(ex, loss_target):
        ex = dict(ex)
        diff = ex.pop(TWIN_DIFF_INPUT)
        return grad_fn(weights, diff, {**shared, **ex}, loss_target)

    if N_MICROBATCH == 1:
        loss, (grad_w, grad_x) = one_microbatch(per_example, given["loss_target"])
    else:
        def body(carry, xs):
            loss_sum, grad_sum = carry
            l_k, (gw_k, gx_k) = one_microbatch(xs[0], xs[1])
            with _jax.named_scope("update"):
                return (loss_sum + l_k, _jax.tree.map(_jnp.add, grad_sum, gw_k)), gx_k

        init = (_jnp.zeros((), _jnp.float32), _jax.tree.map(_jnp.zeros_like, weights))
        (loss, grad_w), grad_x = _jax.lax.scan(body, init, (per_example, given["loss_target"]))
    with _jax.named_scope("update"):
        delta_w, new_m, new_v = {}, {}, {}
        for n in TWIN_WEIGHTS:
            delta_w[n], new_m[n], new_v[n] = _adamw(weights[n], grad_w[n], given["m_" + n], given["v_" + n])
    return (loss, grad_x, *[grad_w[n] for n in TWIN_WEIGHTS], *[delta_w[n] for n in TWIN_WEIGHTS],
            *[new_m[n] for n in TWIN_WEIGHTS], *[new_v[n] for n in TWIN_WEIGHTS])
```

```python
import functools
import math

import jax
import jax.numpy as jnp
from jax import lax
from jax.experimental import pallas as pl
from jax.experimental.pallas import tpu as pltpu

F32 = jnp.float32
BF16 = jnp.bfloat16

HEAD_DIM = 128
GRID_W = 64
ROPE_THETA = 10000.0
NORM_EPS = 1e-6
N_MOD = 9
RET_CHUNK = 128
ADAM_LR = 0.001
ADAM_B1 = 0.9
ADAM_B2 = 0.999
ADAM_EPS = 1e-08
ADAM_WD = 0.01
ADAM_STEP = 10

N_DEV = 8
N_CHIP = 4
LANES_V7X = 128
VMEM_LIMIT_V7X = 52 * 1024 * 1024

NT_DIMS = (((1,), (1,)), ((), ()))
TN_DIMS = (((0,), (0,)), ((), ()))
NN_DIMS = (((1,), (0,)), ((), ()))


def _tile(n, pref, mult=LANES_V7X):
    if n <= pref:
        return n
    t = (pref // mult) * mult
    while t >= mult:
        if n % t == 0:
            return t
        t -= mult
    return n


def _params(sem):
    return pltpu.CompilerParams(dimension_semantics=sem, vmem_limit_bytes=VMEM_LIMIT_V7X)


def _sigmoid(x):
    return 1.0 / (1.0 + jnp.exp(-x))


def _silu(x):
    return x * _sigmoid(x)


def _rmsn(x):
    return x * lax.rsqrt(jnp.mean(x * x, axis=-1, keepdims=True) + NORM_EPS)


def _mm(name, a, b, mode, out_dtype, b_off=0, n=None):
    if mode == "nn":
        m, k = a.shape
        n = b.shape[1] if n is None else n
        dims = NN_DIMS
    elif mode == "nt":
        m, k = a.shape
        n = b.shape[0]
        dims = NT_DIMS
    else:
        k, m = a.shape
        n = b.shape[1]
        dims = TN_DIMS
    tm = _tile(m, 1024)
    tn = _tile(math.gcd(n, b_off), 1024) if b_off else _tile(n, 1024)
    tk = _tile(k, 2560) if mode != "tn" else _tile(k, 1024)
    nk = k // tk
    joff = b_off // tn

    def body(a_ref, b_ref, o_ref, acc_ref):
        kk = pl.program_id(2)

        @pl.when(kk == 0)
        def _():
            acc_ref[...] = jnp.zeros_like(acc_ref)

        acc_ref[...] += lax.dot_general(a_ref[...], b_ref[...], dims, preferred_element_type=F32)

        @pl.when(kk == nk - 1)
        def _():
            o_ref[...] = acc_ref[...].astype(o_ref.dtype)

    if mode == "nn":
        a_spec = pl.BlockSpec((tm, tk), lambda i, j, kk: (i, kk))
        b_spec = pl.BlockSpec((tk, tn), lambda i, j, kk: (kk, j + joff))
    elif mode == "nt":
        a_spec = pl.BlockSpec((tm, tk), lambda i, j, kk: (i, kk))
        b_spec = pl.BlockSpec((tn, tk), lambda i, j, kk: (j, kk))
    else:
        a_spec = pl.BlockSpec((tk, tm), lambda i, j, kk: (kk, i))
        b_spec = pl.BlockSpec((tk, tn), lambda i, j, kk: (kk, j))
    return pl.pallas_call(
        body, name=name, grid=(m // tm, n // tn, nk),
        in_specs=[a_spec, b_spec],
        out_specs=pl.BlockSpec((tm, tn), lambda i, j, kk: (i, j)),
        out_shape=jax.ShapeDtypeStruct((m, n), out_dtype),
        scratch_shapes=[pltpu.VMEM((tm, tn), F32)],
        compiler_params=_params(("parallel", "parallel", "arbitrary")),
    )(a, b)


def _mm_swiglu(name, a, w):
    m, k = a.shape
    f = w.shape[1] // 2
    tm = _tile(m, 1024)
    tn = _tile(f, 512)
    tk = _tile(k, 2560)
    nk = k // tk
    jf = f // tn

    def body(a_ref, wa_ref, wb_ref, h_ref, ua_ref, ub_ref, acca, accb):
        kk = pl.program_id(2)

        @pl.when(kk == 0)
        def _():
            acca[...] = jnp.zeros_like(acca)
            accb[...] = jnp.zeros_like(accb)

        av = a_ref[...]
        acca[...] += jnp.dot(av, wa_ref[...], preferred_element_type=F32)
        accb[...] += jnp.dot(av, wb_ref[...], preferred_element_type=F32)

        @pl.when(kk == nk - 1)
        def _():
            ua = acca[...]
            ub = accb[...]
            h_ref[...] = (_silu(ua) * ub).astype(BF16)
            ua_ref[...] = ua.astype(BF16)
            ub_ref[...] = ub.astype(BF16)

    o_spec = pl.BlockSpec((tm, tn), lambda i, j, kk: (i, j))
    o_shape = jax.ShapeDtypeStruct((m, f), BF16)
    return pl.pallas_call(
        body, name=name, grid=(m // tm, jf, nk),
        in_specs=[pl.BlockSpec((tm, tk), lambda i, j, kk: (i, kk)),
                  pl.BlockSpec((tk, tn), lambda i, j, kk: (kk, j)),
                  pl.BlockSpec((tk, tn), lambda i, j, kk: (kk, j + jf))],
        out_specs=[o_spec, o_spec, o_spec],
        out_shape=[o_shape, o_shape, o_shape],
        scratch_shapes=[pltpu.VMEM((tm, tn), F32), pltpu.VMEM((tm, tn), F32)],
        compiler_params=_params(("parallel", "parallel", "arbitrary")),
    )(a, w, w)


def _rowwise(name, fn, *, n_tiles, tr, row_ins, row_outs, sel_in=None, sel_off=0, ctx_rows=0,
             full_ins=(), acc_shape=None):
    sr = 32 if tr % 32 == 0 else tr
    n_row, n_full, n_out = len(row_ins), len(full_ins), len(row_outs)
    has_sel = sel_in is not None
    has_acc = acc_shape is not None

    def sel_of(i):
        return jnp.where((i + sel_off) * tr < ctx_rows, 0, 1)

    def body(*refs):
        row_refs = refs[:n_row]
        pos = n_row
        sel_ref = None
        if has_sel:
            sel_ref = refs[pos]
            pos += 1
        full_refs = refs[pos:pos + n_full]
        pos += n_full
        out_refs = refs[pos:pos + n_out]
        pos += n_out
        acc_ref = refs[pos] if has_acc else None
        i = pl.program_id(0)
        if has_acc:
            first = (i == 0) | ((i + sel_off) * tr == ctx_rows)

            @pl.when(first)
            def _():
                acc_ref[...] = jnp.zeros_like(acc_ref)

        sel = (lambda kk: sel_ref[kk:kk + 1, :]) if has_sel else None
        fulls = [r[...] for r in full_refs]

        def slab(r, carry):
            rs = pl.ds(pl.multiple_of(r * sr, sr), sr)
            rows = [ref[rs, :] for ref in row_refs]
            outs, accs = fn(rows, sel, fulls)
            for o_ref, o in zip(out_refs, outs):
                o_ref[rs, :] = o.astype(o_ref.dtype)
            for kk, a in enumerate(accs):
                acc_ref[kk:kk + 1, :a.shape[1]] += a
            return carry

        lax.fori_loop(0, tr // sr, slab, 0)

    in_specs, args = [], []
    for arr, off, blk in row_ins:
        if blk is None:
            in_specs.append(pl.BlockSpec((tr, arr.shape[1]), functools.partial(lambda i, o: (i + o, 0), o=off)))
        else:
            in_specs.append(pl.BlockSpec((tr, blk[0]), functools.partial(lambda i, o, cb: (i + o, cb), o=off, cb=blk[1])))
        args.append(arr)
    if has_sel:
        in_specs.append(pl.BlockSpec((None,) + sel_in.shape[1:], lambda i: (sel_of(i), 0, 0)))
        args.append(sel_in)
    for arr in full_ins:
        in_specs.append(pl.BlockSpec(arr.shape, lambda i: (0, 0)))
        args.append(arr)
    out_specs, out_shape = [], []
    for rows, cols, dt, off in row_outs:
        out_specs.append(pl.BlockSpec((tr, cols), functools.partial(lambda i, o: (i + o, 0), o=off)))
        out_shape.append(jax.ShapeDtypeStruct((rows, cols), dt))
    if has_acc:
        out_specs.append(pl.BlockSpec((None,) + tuple(acc_shape), lambda i: (sel_of(i), 0, 0)))
        out_shape.append(jax.ShapeDtypeStruct((2,) + tuple(acc_shape), F32))
    return pl.pallas_call(
        body, name=name, grid=(n_tiles,), in_specs=in_specs, out_specs=out_specs, out_shape=out_shape,
        compiler_params=_params(("arbitrary",)),
    )(*args)


def _swap_pairs(x):
    lane = lax.broadcasted_iota(jnp.int32, x.shape, 1)
    nxt = pltpu.roll(x, x.shape[1] - 1, 1)
    prv = pltpu.roll(x, 1, 1)
    return jnp.where(lane % 2 == 0, nxt, prv)


def _heads_map(fn, arrs, width):
    outs = None
    for h in range(width // HEAD_DIM):
        sl = slice(h * HEAD_DIM, (h + 1) * HEAD_DIM)
        res = fn(*[a[:, sl] for a in arrs])
        if outs is None:
            outs = [[] for _ in res]
        for lst, r in zip(outs, res):
            lst.append(r)
    return [jnp.concatenate(lst, axis=1) if len(lst) > 1 else lst[0] for lst in outs]


def _flash_fwd(q, k, v, groups):
    t, aw = q.shape
    tk_all = k.shape[0]
    hq = aw // HEAD_DIM
    tq = _tile(t, 512)
    tk = _tile(tk_all, 1024)
    nk = tk_all // tk
    scale = HEAD_DIM ** -0.5

    def body(q_ref, k_ref, v_ref, o_ref, lse_ref, m_sc, l_sc, acc_sc):
        j = pl.program_id(2)

        @pl.when(j == 0)
        def _():
            m_sc[...] = jnp.full_like(m_sc, -jnp.inf)
            l_sc[...] = jnp.zeros_like(l_sc)
            acc_sc[...] = jnp.zeros_like(acc_sc)

        s = lax.dot_general(q_ref[...], k_ref[...], NT_DIMS, preferred_element_type=F32) * scale
        m_prev = m_sc[...]
        m_new = jnp.maximum(m_prev, jnp.max(s, axis=1, keepdims=True))
        p = jnp.exp(s - m_new)
        alpha = jnp.exp(m_prev - m_new)
        l_sc[...] = alpha * l_sc[...] + jnp.sum(p, axis=1, keepdims=True)
        acc_sc[...] = alpha * acc_sc[...] + jnp.dot(p.astype(BF16), v_ref[...], preferred_element_type=F32)
        m_sc[...] = m_new

        @pl.when(j == nk - 1)
        def _():
            o_ref[...] = (acc_sc[...] / l_sc[...]).astype(o_ref.dtype)
            lse_ref[...] = jnp.broadcast_to(m_sc[...] + jnp.log(l_sc[...]), lse_ref.shape)

    qs = pl.BlockSpec((tq, HEAD_DIM), lambda h, i, j: (i, h))
    ks = pl.BlockSpec((tk, HEAD_DIM), lambda h, i, j: (j, h // groups))
    return pl.pallas_call(
        body, name="flash_fwd", grid=(hq, t // tq, nk),
        in_specs=[qs, ks, ks], out_specs=[qs, qs],
        out_shape=[jax.ShapeDtypeStruct((t, aw), BF16), jax.ShapeDtypeStruct((t, aw), F32)],
        scratch_shapes=[pltpu.VMEM((tq, 1), F32), pltpu.VMEM((tq, 1), F32), pltpu.VMEM((tq, HEAD_DIM), F32)],
        compiler_params=_params(("parallel", "parallel", "arbitrary")),
    )(q, k, v)


def _flash_p_ds(q, k, v, o, do, lse, scale):
    s = lax.dot_general(q, k, NT_DIMS, preferred_element_type=F32) * scale
    p = jnp.exp(s - lse[:, :1])
    dp = lax.dot_general(do, v, NT_DIMS, preferred_element_type=F32)
    delta = jnp.sum(do.astype(F32) * o.astype(F32), axis=1, keepdims=True)
    ds = p * (dp - delta) * scale
    return p, ds


def _flash_dq(q, k, v, o, do, lse, groups):
    t, aw = q.shape
    tk_all = k.shape[0]
    hq = aw // HEAD_DIM
    tq = _tile(t, 512)
    tk = _tile(tk_all, 1024)
    nk = tk_all // tk
    scale = HEAD_DIM ** -0.5

    def body(q_ref, k_ref, v_ref, o_ref, do_ref, lse_ref, dq_ref, acc):
        j = pl.program_id(2)

        @pl.when(j == 0)
        def _():
            acc[...] = jnp.zeros_like(acc)

        kv = k_ref[...]
        _, ds = _flash_p_ds(q_ref[...], kv, v_ref[...], o_ref[...], do_ref[...], lse_ref[...], scale)
        acc[...] += jnp.dot(ds.astype(BF16), kv, preferred_element_type=F32)

        @pl.when(j == nk - 1)
        def _():
            dq_ref[...] = acc[...]

    qs = pl.BlockSpec((tq, HEAD_DIM), lambda h, i, j: (i, h))
    ks = pl.BlockSpec((tk, HEAD_DIM), lambda h, i, j: (j, h // groups))
    return pl.pallas_call(
        body, name="flash_dq", grid=(hq, t // tq, nk),
        in_specs=[qs, ks, ks, qs, qs, qs], out_specs=qs,
        out_shape=jax.ShapeDtypeStruct((t, aw), F32),
        scratch_shapes=[pltpu.VMEM((tq, HEAD_DIM), F32)],
        compiler_params=_params(("parallel", "parallel", "arbitrary")),
    )(q, k, v, o, do, lse)


def _flash_dkv(q, k, v, o, do, lse, groups):
    t, aw = q.shape
    tk_all, kvw = k.shape
    kvh = kvw // HEAD_DIM
    tq = _tile(t, 512)
    tk = _tile(tk_all, 1024)
    nq = t // tq
    nin = groups * nq
    scale = HEAD_DIM ** -0.5

    def body(q_ref, k_ref, v_ref, o_ref, do_ref, lse_ref, dk_ref, dv_ref, dk_acc, dv_acc):
        tt = pl.program_id(2)

        @pl.when(tt == 0)
        def _():
            dk_acc[...] = jnp.zeros_like(dk_acc)
            dv_acc[...] = jnp.zeros_like(dv_acc)

        qv = q_ref[...]
        dov = do_ref[...]
        p, ds = _flash_p_ds(qv, k_ref[...], v_ref[...], o_ref[...], dov, lse_ref[...], scale)
        dv_acc[...] += lax.dot_general(p.astype(BF16), dov, TN_DIMS, preferred_element_type=F32)
        dk_acc[...] += lax.dot_general(ds.astype(BF16), qv, TN_DIMS, preferred_element_type=F32)

        @pl.when(tt == nin - 1)
        def _():
            dk_ref[...] = dk_acc[...]
            dv_ref[...] = dv_acc[...]

    qs = pl.BlockSpec((tq, HEAD_DIM), lambda kh, j, tt: (tt % nq, kh * groups + tt // nq))
    ks = pl.BlockSpec((tk, HEAD_DIM), lambda kh, j, tt: (j, kh))
    return pl.pallas_call(
        body, name="flash_dkv", grid=(kvh, tk_all // tk, nin),
        in_specs=[qs, ks, ks, qs, qs, qs], out_specs=[ks, ks],
        out_shape=[jax.ShapeDtypeStruct((tk_all, kvw), F32), jax.ShapeDtypeStruct((tk_all, kvw), F32)],
        scratch_shapes=[pltpu.VMEM((tk, HEAD_DIM), F32), pltpu.VMEM((tk, HEAD_DIM), F32)],
        compiler_params=_params(("parallel", "parallel", "arbitrary")),
    )(q, k, v, o, do, lse)


def _bf_nn(a, b):
    return jnp.dot(a.astype(BF16), b.astype(BF16), preferred_element_type=F32)


def _bf_nt(a, b):
    return lax.dot_general(a.astype(BF16), b.astype(BF16), NT_DIMS, preferred_element_type=F32)


def _bf_tn(a, b):
    return lax.dot_general(a.astype(BF16), b.astype(BF16), TN_DIMS, preferred_element_type=F32)


@jax.custom_vjp
def _d_nn(a, b):
    return _bf_nn(a, b)


@jax.custom_vjp
def _d_nt(a, b):
    return _bf_nt(a, b)


@jax.custom_vjp
def _d_tn(a, b):
    return _bf_tn(a, b)


_d_nn.defvjp(lambda a, b: (_bf_nn(a, b), (a, b)), lambda r, g: (_d_nt(g, r[1]), _d_tn(r[0], g)))
_d_nt.defvjp(lambda a, b: (_bf_nt(a, b), (a, b)), lambda r, g: (_d_nn(g, r[1]), _d_tn(g, r[0])))
_d_tn.defvjp(lambda a, b: (_bf_tn(a, b), (a, b)), lambda r, g: (_d_nt(r[1], g), _d_nn(r[0], g)))


def _ret_chunk(q, k_raw, v, state, lg, rev, dots):
    nn, nt, tn = dots
    c = RET_CHUNK
    tcol = lax.broadcasted_iota(jnp.int32, (c, 1), 0).astype(F32)
    trow = lax.broadcasted_iota(jnp.int32, (1, c), 1).astype(F32)
    ucol = jnp.where(rev, c - 1.0 - tcol, tcol)
    urow = jnp.where(rev, c - 1.0 - trow, trow)
    e = ucol - urow
    low = e >= 0
    intra = jnp.where(low, jnp.exp(jnp.where(low, e, 0.0) * lg), 0.0)
    k = k_raw * (HEAD_DIM ** -0.5)
    inner = nt(q, k) * intra
    y = nn(inner, v) + nn(q, state) * jnp.exp((ucol + 1.0) * lg)
    new_state = state * jnp.exp(c * lg) + tn(k * jnp.exp((c - 1.0 - ucol) * lg), v)
    return y, new_state


def _ret_chunk_index(n_chunks, n_ctx_chunks):
    def idx(d, s):
        rev = jnp.where(s < n_ctx_chunks, n_ctx_chunks - 1 - s, n_chunks - 1 - s + n_ctx_chunks)
        return jnp.where(d == 0, s, rev)
    return idx


def _ret_fwd(pr, lgv, ctx_rows):
    tk_all = pr.shape[0]
    rw = pr.shape[1] // 3
    nh = rw // HEAD_DIM
    nc = tk_all // RET_CHUNK
    cidx = _ret_chunk_index(nc, ctx_rows // RET_CHUNK)

    def body(q_ref, k_ref, v_ref, lg_ref, y_ref, st_ref, s_sc):
        d = pl.program_id(0)
        s = pl.program_id(2)

        @pl.when(s == 0)
        def _():
            s_sc[...] = jnp.zeros_like(s_sc)

        state = s_sc[...]
        st_ref[...] = state
        y, new_state = _ret_chunk(q_ref[...], k_ref[...], v_ref[...], state, lg_ref[:, :1], d == 1,
                                  (_bf_nn, _bf_nt, _bf_tn))
        y_ref[...] = y
        s_sc[...] = new_state

    blk = (RET_CHUNK, HEAD_DIM)
    return pl.pallas_call(
        body, name="ret_fwd", grid=(2, nh, nc),
        in_specs=[pl.BlockSpec(blk, lambda d, h, s: (cidx(d, s), h)),
                  pl.BlockSpec(blk, lambda d, h, s: (cidx(d, s), nh + h)),
                  pl.BlockSpec(blk, lambda d, h, s: (cidx(d, s), 2 * nh + h)),
                  pl.BlockSpec((None, None, 1, HEAD_DIM), lambda d, h, s: (d, h, 0, 0))],
        out_specs=[pl.BlockSpec(blk, lambda d, h, s: (d * nc + cidx(d, s), h)),
                   pl.BlockSpec((None, None, None, HEAD_DIM, HEAD_DIM), lambda d, h, s: (d, h, s, 0, 0))],
        out_shape=[jax.ShapeDtypeStruct((2 * tk_all, rw), F32),
                   jax.ShapeDtypeStruct((2, nh, nc, HEAD_DIM, HEAD_DIM), F32)],
        scratch_shapes=[pltpu.VMEM((HEAD_DIM, HEAD_DIM), F32)],
        compiler_params=_params(("parallel", "parallel", "arbitrary")),
    )(pr, pr, pr, lgv)


def _ret_bwd(pr, states, dy, lgv, ctx_rows):
    tk_all = pr.shape[0]
    rw = pr.shape[1] // 3
    nh = rw // HEAD_DIM
    nc = tk_all // RET_CHUNK
    cidx = _ret_chunk_index(nc, ctx_rows // RET_CHUNK)

    def body(q_ref, k_ref, v_ref, st_ref, dy_ref, lg_ref, dq_ref, dk_ref, dv_ref, dlg_ref, ds_sc):
        d = pl.program_id(0)
        sp = pl.program_id(2)

        @pl.when(sp == 0)
        def _():
            ds_sc[...] = jnp.zeros_like(ds_sc)
            dlg_ref[...] = jnp.zeros_like(dlg_ref)

        def step(q, k, v, state, lg):
            return _ret_chunk(q, k, v, state, lg, d == 1, (_d_nn, _d_nt, _d_tn))

        _, vjp = jax.vjp(step, q_ref[...], k_ref[...], v_ref[...], st_ref[...], lg_ref[:, :1])
        dq, dk, dv, dstate, dlg = vjp((dy_ref[...], ds_sc[...]))
        dq_ref[...] = dq
        dk_ref[...] = dk
        dv_ref[...] = dv
        ds_sc[...] = dstate
        dlg_ref[...] += jnp.broadcast_to(dlg, dlg_ref.shape)

    blk = (RET_CHUNK, HEAD_DIM)

    def at(col):
        return lambda d, h, sp: (cidx(d, nc - 1 - sp), col(h))

    o_spec = pl.BlockSpec(blk, lambda d, h, sp: (d * nc + cidx(d, nc - 1 - sp), h))
    o_shape = jax.ShapeDtypeStruct((2 * tk_all, rw), F32)
    lg_spec = pl.BlockSpec((None, None, 1, HEAD_DIM), lambda d, h, sp: (d, h, 0, 0))
    return pl.pallas_call(
        body, name="ret_bwd", grid=(2, nh, nc),
        in_specs=[pl.BlockSpec(blk, at(lambda h: h)),
                  pl.BlockSpec(blk, at(lambda h: nh + h)),
                  pl.BlockSpec(blk, at(lambda h: 2 * nh + h)),
                  pl.BlockSpec((None, None, None, HEAD_DIM, HEAD_DIM), lambda d, h, sp: (d, h, nc - 1 - sp, 0, 0)),
                  pl.BlockSpec(blk, at(lambda h: h)),
                  lg_spec],
        out_specs=[o_spec, o_spec, o_spec, lg_spec],
        out_shape=[o_shape, o_shape, o_shape, jax.ShapeDtypeStruct(lgv.shape, F32)],
        scratch_shapes=[pltpu.VMEM((HEAD_DIM, HEAD_DIM), F32)],
        compiler_params=_params(("parallel", "parallel", "arbitrary")),
    )(pr, pr, pr, states, dy, lgv)


FLIP_X, FLIP_Y, FLIP_XY, FLIP_C = (1, 0, 0), (0, 1, 0), (1, 1, 0), (0, 0, 1)
CHIP_FLIPS = ((FLIP_X, 2), (FLIP_Y, 1), (FLIP_XY, 3))


def _flip(me, mask):
    return tuple(1 - v if m else v for v, m in zip(me, mask))


def _comm(name, ins, out_shapes, plan, n_remote, n_local):
    n_in, n_out = len(ins), len(out_shapes)

    def body(*refs):
        in_refs = refs[:n_in]
        out_refs = refs[n_in:n_in + n_out]
        send_sems, recv_sems, local_sems = refs[n_in + n_out:]
        me = (lax.axis_index("x"), lax.axis_index("y"), lax.axis_index("c"))
        local, phases = plan(in_refs, out_refs, me)
        local_copies = [pltpu.make_async_copy(s, d, local_sems.at[i]) for i, (s, d) in enumerate(local)]
        for cp in local_copies:
            cp.start()
        sent = []
        kk = 0
        for phase in phases:
            arrivals = []
            for mask, src, dst, landing in phase:
                peer = _flip(me, mask)
                cp = pltpu.make_async_remote_copy(src_ref=src, dst_ref=dst, send_sem=send_sems.at[kk],
                                                  recv_sem=recv_sems.at[kk], device_id=peer,
                                                  device_id_type=pl.DeviceIdType.MESH)
                cp.start()
                sent.append(cp)
                arrivals.append(pltpu.make_async_remote_copy(
                    src_ref=landing, dst_ref=landing, send_sem=send_sems.at[kk], recv_sem=recv_sems.at[kk],
                    device_id=peer, device_id_type=pl.DeviceIdType.MESH))
                kk += 1
            for cp in arrivals:
                cp.wait_recv()
        for cp in sent:
            cp.wait_send()
        for cp in local_copies:
            cp.wait()

    any_spec = pl.BlockSpec(memory_space=pl.ANY)
    return pl.pallas_call(
        body, name=name,
        in_specs=[any_spec] * n_in, out_specs=[any_spec] * n_out, out_shape=list(out_shapes),
        scratch_shapes=[pltpu.SemaphoreType.DMA((n_remote,)), pltpu.SemaphoreType.DMA((n_remote,)),
                        pltpu.SemaphoreType.DMA((max(n_local, 1),))],
    )(*ins)


def _ds(start, size):
    return pl.ds(pl.multiple_of(start * size, 8), size)


def _all_gather8(name, v):
    masks = [(a, b, cc) for a in (0, 1) for b in (0, 1) for cc in (0, 1)][1:]

    def index(p):
        return 4 * p[0] + 2 * p[1] + p[2]

    def plan(in_refs, out_refs, me):
        (src,), (out,) = in_refs, out_refs
        local = [(src, out.at[index(me)])]
        phase = [(m, src, out.at[index(me)], out.at[index(_flip(me, m))]) for m in masks]
        return local, [phase]

    return _comm(name, [v], [jax.ShapeDtypeStruct((N_DEV,) + v.shape, v.dtype)], plan, len(masks), 1)[0]


class _Sharded:
    def __init__(self, kind, rows, cols):
        self.kind, self.rows, self.cols = kind, rows, cols
        self.shard_shape = (rows, cols // N_CHIP) if kind == "col" else (rows // N_CHIP, cols)
        self.half_shape = (rows // 2, cols) if kind == "col" else (rows, cols // 2)
        self.piece_shape = (rows // 2, cols // N_CHIP) if kind == "col" else (rows // N_CHIP, cols // 2)

    def shard_of_full(self, ref, s):
        if self.kind == "col":
            return ref.at[:, _ds(s, self.cols // N_CHIP)]
        return ref.at[_ds(s, self.rows // N_CHIP), :]

    def half_of_full(self, ref, h):
        if self.kind == "col":
            return ref.at[_ds(h, self.rows // 2), :]
        return ref.at[:, _ds(h, self.cols // 2)]

    def piece_of_full(self, ref, s, h):
        if self.kind == "col":
            return ref.at[_ds(h, self.rows // 2), _ds(s, self.cols // N_CHIP)]
        return ref.at[_ds(s, self.rows // N_CHIP), _ds(h, self.cols // 2)]

    def half_of_shard(self, ref, h):
        if self.kind == "col":
            return ref.at[_ds(h, self.rows // 2), :]
        return ref.at[:, _ds(h, self.cols // 2)]

    def shard_of_half(self, ref, s):
        if self.kind == "col":
            return ref.at[:, _ds(s, self.cols // N_CHIP)]
        return ref.at[_ds(s, self.rows // N_CHIP), :]


def _gather_weights(metas, shards):
    nt = len(metas)

    def plan(in_refs, out_refs, me):
        x, y, c = me
        s_me = 2 * x + y
        local, ici, d2d = [], [], []
        for meta, src, full in zip(metas, in_refs, out_refs):
            local.append((src, meta.shard_of_full(full, s_me)))
            for mask, bits in CHIP_FLIPS:
                s_peer = jnp.bitwise_xor(s_me, bits)
                ici.append((mask, meta.half_of_shard(src, c), meta.piece_of_full(full, s_me, c),
                            meta.piece_of_full(full, s_peer, c)))
                d2d.append((FLIP_C, meta.piece_of_full(full, s_peer, c), meta.piece_of_full(full, s_peer, c),
                            meta.piece_of_full(full, s_peer, 1 - c)))
        return local, [ici, d2d]

    outs = [jax.ShapeDtypeStruct((m.rows, m.cols), BF16) for m in metas]
    return _comm("gather_weights", list(shards), outs, plan, 6 * nt, nt)


def _reduce_pair(metas, grads):
    def plan(in_refs, out_refs, me):
        c = me[2]
        phase = [(FLIP_C, m.half_of_full(g, 1 - c), land, land) for m, g, land in zip(metas, in_refs, out_refs)]
        return [], [phase]

    outs = [jax.ShapeDtypeStruct(m.half_shape, BF16) for m in metas]
    return _comm("reduce_pair", list(grads), outs, plan, len(metas), 0)


def _reduce_chips(metas, halves):
    def plan(in_refs, out_refs, me):
        x, y, _ = me
        s_me = 2 * x + y
        phase = []
        for m, p, land in zip(metas, in_refs, out_refs):
            for kk, (mask, bits) in enumerate(CHIP_FLIPS):
                s_peer = jnp.bitwise_xor(s_me, bits)
                phase.append((mask, m.shard_of_half(p, s_peer), land.at[kk], land.at[kk]))
        return [], [phase]

    outs = [jax.ShapeDtypeStruct((3,) + m.piece_shape, BF16) for m in metas]
    return _comm("reduce_chips", list(halves), outs, plan, 3 * len(metas), 0)


def _share_halves(metas, pieces):
    def plan(in_refs, out_refs, me):
        c = me[2]
        local, phase = [], []
        for m, g, full in zip(metas, in_refs, out_refs):
            local.append((g, m.half_of_shard(full, c)))
            phase.append((FLIP_C, g, m.half_of_shard(full, c), m.half_of_shard(full, 1 - c)))
        return local, [phase]

    outs = [jax.ShapeDtypeStruct(m.shard_shape, F32) for m in metas]
    return _comm("share_halves", list(pieces), outs, plan, len(metas), len(metas))


def _pair_sum(meta, grad, landed, c_arr):
    hr, hc = meta.half_shape
    tr = _tile(hr, 256, 16)
    tc = _tile(hc, 2048)
    nr, ncol = hr // tr, hc // tc

    def body(c_ref, g_ref, l_ref, o_ref):
        o_ref[...] = (g_ref[...].astype(F32) + l_ref[...].astype(F32)).astype(BF16)

    if meta.kind == "col":
        g_map = lambda i, j, c_ref: (i + c_ref[0] * nr, j)
    else:
        g_map = lambda i, j, c_ref: (i, j + c_ref[0] * ncol)
    blk = (tr, tc)
    return pl.pallas_call(
        body, name="pair_sum",
        grid_spec=pltpu.PrefetchScalarGridSpec(
            num_scalar_prefetch=1, grid=(nr, ncol),
            in_specs=[pl.BlockSpec(blk, g_map), pl.BlockSpec(blk, lambda i, j, c_ref: (i, j))],
            out_specs=pl.BlockSpec(blk, lambda i, j, c_ref: (i, j))),
        out_shape=jax.ShapeDtypeStruct((hr, hc), BF16),
        compiler_params=_params(("parallel", "parallel")),
    )(c_arr, grad, landed)


def _sum_pieces(meta, half, landed, s_arr):
    pr, pc = meta.piece_shape
    tr = _tile(pr, 256, 16)
    tc = _tile(pc, 2048)
    nr, ncol = pr // tr, pc // tc

    def body(s_ref, p_ref, l_ref, o_ref):
        acc = p_ref[...].astype(F32)
        for kk in range(3):
            acc = acc + l_ref[kk].astype(F32)
        o_ref[...] = acc

    if meta.kind == "col":
        p_map = lambda i, j, s_ref: (i, j + s_ref[0] * ncol)
    else:
        p_map = lambda i, j, s_ref: (i + s_ref[0] * nr, j)
    blk = (tr, tc)
    return pl.pallas_call(
        body, name="sum_pieces",
        grid_spec=pltpu.PrefetchScalarGridSpec(
            num_scalar_prefetch=1, grid=(nr, ncol),
            in_specs=[pl.BlockSpec(blk, p_map), pl.BlockSpec((3,) + blk, lambda i, j, s_ref: (0, i, j))],
            out_specs=pl.BlockSpec(blk, lambda i, j, s_ref: (i, j))),
        out_shape=jax.ShapeDtypeStruct((pr, pc), F32),
        compiler_params=_params(("parallel", "parallel")),
    )(s_arr, half, landed)


def _adam_rows(rows, sel, fulls):
    w, g, m, v = rows
    m2 = ADAM_B1 * m + (1.0 - ADAM_B1) * g
    v2 = ADAM_B2 * v + (1.0 - ADAM_B2) * jnp.square(g)
    m_hat = m2 / (1.0 - ADAM_B1 ** ADAM_STEP)
    v_hat = v2 / (1.0 - ADAM_B2 ** ADAM_STEP)
    delta = -ADAM_LR * (m_hat / (jnp.sqrt(v_hat) + ADAM_EPS) + ADAM_WD * w)
    return [delta, m2, v2], []


def _adamw(w, g, m, v):
    r, c = w.shape
    tr = _tile(r, 128, 8)
    outs = _rowwise("adamw", _adam_rows, n_tiles=r // tr, tr=tr,
                    row_ins=[(w, 0, None), (g, 0, None), (m, 0, None), (v, 0, None)],
                    row_outs=[(r, c, F32, 0)] * 3)
    return outs[0], outs[1], outs[2]


def _ada_fwd(cg, w, b):
    d, n = w.shape
    tn = _tile(n, 512)

    def body(c_ref, w_ref, b_ref, o_ref):
        a = _silu(c_ref[...]).astype(BF16)
        o_ref[...] = jnp.dot(a, w_ref[...].astype(BF16), preferred_element_type=F32) + b_ref[...]

    return pl.pallas_call(
        body, name="ada_fwd", grid=(n // tn,),
        in_specs=[pl.BlockSpec(cg.shape, lambda j: (0, 0)), pl.BlockSpec((d, tn), lambda j: (0, j)),
                  pl.BlockSpec((1, tn), lambda j: (0, j))],
        out_specs=pl.BlockSpec((cg.shape[0], tn), lambda j: (0, j)),
        out_shape=jax.ShapeDtypeStruct((cg.shape[0], n), F32),
        compiler_params=_params(("parallel",)),
    )(cg, w, b)


def _ada_bwd(cg, dm, w):
    d, n = w.shape
    tn = _tile(n, 512)
    nj = n // tn

    def body(c_ref, dm_ref, w_ref, gw_ref, da_ref, acc):
        j = pl.program_id(0)

        @pl.when(j == 0)
        def _():
            acc[...] = jnp.zeros_like(acc)

        a = _silu(c_ref[...]).astype(BF16)
        dmv = dm_ref[...].astype(BF16)
        gw_ref[...] = lax.dot_general(a, dmv, TN_DIMS, preferred_element_type=F32)
        acc[...] += lax.dot_general(dmv, w_ref[...].astype(BF16), NT_DIMS, preferred_element_type=F32)

        @pl.when(j == nj - 1)
        def _():
            da_ref[...] = acc[...]

    return pl.pallas_call(
        body, name="ada_bwd", grid=(nj,),
        in_specs=[pl.BlockSpec(cg.shape, lambda j: (0, 0)), pl.BlockSpec((dm.shape[0], tn), lambda j: (0, j)),
                  pl.BlockSpec((d, tn), lambda j: (0, j))],
        out_specs=[pl.BlockSpec((d, tn), lambda j: (0, j)), pl.BlockSpec(cg.shape, lambda j: (0, 0))],
        out_shape=[jax.ShapeDtypeStruct((d, n), F32), jax.ShapeDtypeStruct(cg.shape, F32)],
        scratch_shapes=[pltpu.VMEM(cg.shape, F32)],
        compiler_params=_params(("arbitrary",)),
    )(cg, dm, w)


def _small_reduce(gathered, logits, n_mod_cols, lg_off, loss_off, loss_cols):
    npk = gathered.shape[1]

    def body(g_ref, lo_ref, tot_ref, gb_ref, gl_ref, loss_ref):
        acc = g_ref[0:1, :]
        for dd in range(1, N_DEV):
            acc = acc + g_ref[dd:dd + 1, :]
        tot_ref[...] = acc
        gb_ref[...] = acc[:, :n_mod_cols] + acc[:, n_mod_cols:2 * n_mod_cols]
        gl_ref[...] = acc[:, lg_off:lg_off + LANES_V7X] * _sigmoid(-lo_ref[...])
        loss = jnp.sum(acc[:, loss_off:loss_off + loss_cols], axis=1, keepdims=True)
        loss_ref[...] = jnp.broadcast_to(loss, loss_ref.shape)

    lane = jax.ShapeDtypeStruct((1, LANES_V7X), F32)
    return pl.pallas_call(
        body, name="small_reduce",
        out_shape=[jax.ShapeDtypeStruct((1, npk), F32), jax.ShapeDtypeStruct((1, n_mod_cols), F32), lane, lane],
    )(gathered, logits)


def _c_ctx_grad(parts, c_ctx):
    def body(p_ref, c_ref, o_ref):
        tot = p_ref[0:1, :] + p_ref[2:3, :] + p_ref[4:5, :] + p_ref[6:7, :]
        _, vjp = jax.vjp(_silu, c_ref[...])
        o_ref[...] = vjp(tot)[0]

    return pl.pallas_call(body, name="c_ctx_grad", out_shape=jax.ShapeDtypeStruct(c_ctx.shape, F32))(parts, c_ctx)


def _rope_tables(seq, ctx_rows):
    rows = seq // GRID_W
    row = jnp.repeat(jnp.arange(rows, dtype=F32), GRID_W)
    col = jnp.tile(jnp.arange(GRID_W, dtype=F32), rows)
    half = HEAD_DIM // 2
    inv_freq = ROPE_THETA ** (-jnp.arange(0, half, 2, dtype=F32) / half)
    ang = jnp.concatenate([row[:, None] * inv_freq, col[:, None] * inv_freq], axis=-1)
    cos, sin = jnp.cos(ang), jnp.sin(ang)
    cos_full = jnp.repeat(cos, 2, axis=1)
    sin_signed = jnp.stack([-sin, sin], axis=-1).reshape(seq, HEAD_DIM)
    cos_full = jnp.concatenate([jnp.ones((ctx_rows, HEAD_DIM), F32), cos_full], axis=0)
    sin_signed = jnp.concatenate([jnp.zeros((ctx_rows, HEAD_DIM), F32), sin_signed], axis=0)
    return cos_full, sin_signed


def _qk_rot(p, gain, cos_full, sin_signed):
    r = _rmsn(p) * gain
    return r * cos_full + _swap_pairs(r) * sin_signed


def _qk_rot_bwd(g, p, gain, cos_full, sin_signed):
    g1 = g * cos_full + _swap_pairs(g * sin_signed)
    _, vjp = jax.vjp(lambda pp, gn: _rmsn(pp) * gn, p, gain)
    return vjp(g1)


def kernel(x, c, ctx, c_ctx, w_ada, b_ada, ffn1_w_in, ffn1_w_out, mix_w_in, attn_q_gain, attn_k_gain, ret_decay_logit, w_proj_attn, w_proj_ret, mix_w_out, ffn2_w_in, ffn2_w_out, final_norm, loss_target, m_c_ctx, m_w_ada, m_b_ada, m_ffn1_w_in, m_ffn1_w_out, m_mix_w_in, m_attn_q_gain, m_attn_k_gain, m_ret_decay_logit, m_w_proj_attn, m_w_proj_ret, m_mix_w_out, m_ffn2_w_in, m_ffn2_w_out, m_final_norm, v_c_ctx, v_w_ada, v_b_ada, v_ffn1_w_in, v_ffn1_w_out, v_mix_w_in, v_attn_q_gain, v_attn_k_gain, v_ret_decay_logit, v_w_proj_attn, v_w_proj_ret, v_mix_w_out, v_ffn2_w_in, v_ffn2_w_out, v_final_norm):
    xi, yi, ci = lax.axis_index("x"), lax.axis_index("y"), lax.axis_index("c")
    dev = 4 * xi + 2 * yi + ci
    s_me = 2 * xi + yi
    c_arr = jnp.reshape(ci, (1,)).astype(jnp.int32)
    s_arr = jnp.reshape(s_me, (1,)).astype(jnp.int32)

    t, d = x.shape[1], x.shape[2]
    tc = ctx.shape[1]
    tk = tc + t
    ff = ffn1_w_out.shape[1] * N_CHIP
    aw = w_proj_attn.shape[1]
    rw = w_proj_ret.shape[1]
    pw = mix_w_in.shape[2] * N_CHIP
    kvw = (pw - aw - 4 * rw - 2 * d) // 2
    groups = aw // kvw
    n_ret_heads = rw // HEAD_DIM
    mod_cols = N_MOD * d
    tr = _tile(tc, 256, 32)
    nt_all, nt_x, ctx_tiles = tk // tr, t // tr, tc // tr

    c_rows = _all_gather8("gather_c", jnp.pad(c, ((0, 7), (0, 0))))[:, 0, :]
    cg = jnp.concatenate([c_rows, c_ctx[None, :], jnp.zeros((7, d), F32)], axis=0)
    w_ada_l = w_ada[0]
    ada_cols = w_ada_l.shape[1]
    b_ada_l = lax.dynamic_slice_in_dim(b_ada, s_me * ada_cols, ada_cols, axis=1)
    mod_shard = _ada_fwd(cg, w_ada_l, b_ada_l)
    mod_g = _all_gather8("gather_mod", mod_shard)
    mod_full = jnp.concatenate([mod_g[0], mod_g[2], mod_g[4], mod_g[6]], axis=1)
    mod_x = lax.dynamic_slice_in_dim(mod_full, dev, 1, axis=0).reshape(N_MOD, d)
    mod_c = mod_full[8].reshape(N_MOD, d)
    mods = jnp.stack([mod_c, mod_x])

    big = [("col", ffn1_w_in), ("row", ffn1_w_out), ("col", mix_w_in), ("col", w_proj_attn), ("col", w_proj_ret),
           ("row", mix_w_out), ("col", ffn2_w_in), ("row", ffn2_w_out)]
    metas = []
    for kind, w in big:
        r_l, c_l = w.shape[1:]
        metas.append(_Sharded(kind, r_l, c_l * N_CHIP) if kind == "col" else _Sharded(kind, r_l * N_CHIP, c_l))
    w1i, w1o, wmi, wpa, wpr, wmo, w2i, w2o = _gather_weights(metas, [w[0].astype(BF16) for _, w in big])

    cos_full, sin_signed = _rope_tables(t, tc)
    q_gain, k_gain = attn_q_gain, attn_k_gain
    log_gamma = jax.nn.log_sigmoid(ret_decay_logit[0])
    lgv = jnp.broadcast_to(log_gamma[:, :, None, None], (2, n_ret_heads, 1, HEAD_DIM))

    def norm_mod(name, h, n_tiles, off, i_shift, i_scale):
        def fn(rows, sel, fulls):
            return [_rmsn(rows[0]) * (1.0 + sel(i_scale)) + sel(i_shift)], []
        return _rowwise(name, fn, n_tiles=n_tiles, tr=tr, row_ins=[(h, 0, None)],
                        row_outs=[(h.shape[0], d, BF16, 0)], sel_in=mods, sel_off=off, ctx_rows=tc)[0]

    def resid(name, h, h_off, f, n_tiles, off, i_gate, coef):
        def fn(rows, sel, fulls):
            return [rows[0] + coef * sel(i_gate) * rows[1]], []
        return _rowwise(name, fn, n_tiles=n_tiles, tr=tr, row_ins=[(h, h_off, None), (f, 0, None)],
                        row_outs=[(f.shape[0], d, F32, 0)], sel_in=mods, sel_off=off, ctx_rows=tc)[0]

    h0 = jnp.concatenate([ctx[0], x[0]], axis=0)
    n1 = norm_mod("norm_mod1", h0, nt_all, 0, 0, 1)
    hm1, ua1, ub1 = _mm_swiglu("ffn1_in", n1, w1i)
    f1 = _mm("ffn1_out", hm1, w1o, "nn", F32)
    h1 = resid("resid1", h0, 0, f1, nt_all, 0, 2, 0.5)

    n2 = norm_mod("norm_mod2", h1, nt_all, 0, 3, 4)
    p_q = _mm("mix_in_q", n2, wmi, "nn", F32, 0, aw)
    p_kv = _mm("mix_in_kv", n2, wmi, "nn", F32, aw, 2 * kvw)
    p_r = _mm("mix_in_ret", n2, wmi, "nn", F32, aw + 2 * kvw, 3 * rw)
    p_gr = _mm("mix_in_gr", n2, wmi, "nn", F32, aw + 2 * kvw + 3 * rw, rw)
    p_gab = _mm("mix_in_gab", n2, wmi, "nn", F32, aw + 2 * kvw + 4 * rw, 2 * d)

    def q_prep(rows, sel, fulls):
        p, cf, ss = rows
        return _heads_map(lambda ph: [_qk_rot(ph, fulls[0], cf, ss)], [p], aw), []

    q_rot = _rowwise("q_prep", q_prep, n_tiles=nt_x, tr=tr,
                     row_ins=[(p_q, ctx_tiles, None), (cos_full, ctx_tiles, None), (sin_signed, ctx_tiles, None)],
                     row_outs=[(t, aw, BF16, 0)], full_ins=[q_gain])[0]

    def kv_prep(rows, sel, fulls):
        p, cf, ss = rows
        k_rot = _heads_map(lambda ph: [_qk_rot(ph, fulls[0], cf, ss)], [p[:, :kvw]], kvw)[0]
        return [k_rot, p[:, kvw:]], []

    k_rot, v_att = _rowwise("kv_prep", kv_prep, n_tiles=nt_all, tr=tr,
                            row_ins=[(p_kv, 0, None), (cos_full, 0, None), (sin_signed, 0, None)],
                            row_outs=[(tk, kvw, BF16, 0), (tk, kvw, BF16, 0)], full_ins=[k_gain])

    ya, lse = _flash_fwd(q_rot, k_rot, v_att, groups)
    y_ret, states = _ret_fwd(p_r, lgv, tc)

    def ret_out_fn(yf, yb, gr):
        return [_silu(gr) * _rmsn(yf + yb)]

    def ret_out(rows, sel, fulls):
        return _heads_map(ret_out_fn, rows, rw), []

    y_rows = [(y_ret, ctx_tiles, None), (y_ret, nt_all + ctx_tiles, None), (p_gr, ctx_tiles, None)]
    yr = _rowwise("ret_out", ret_out, n_tiles=nt_x, tr=tr, row_ins=y_rows, row_outs=[(t, rw, BF16, 0)])[0]

    pa = _mm("proj_attn", ya, wpa, "nn", F32)
    prj = _mm("proj_ret", yr, wpr, "nn", F32)

    def merge_fn(a, r, ga, gb):
        return _sigmoid(ga) * a + _sigmoid(gb) * r

    gate_rows = [(p_gab, ctx_tiles, (d, 0)), (p_gab, ctx_tiles, (d, 1))]
    z = _rowwise("merge", lambda rows, sel, fulls: ([merge_fn(*rows)], []), n_tiles=nt_x, tr=tr,
                 row_ins=[(pa, 0, None), (prj, 0, None)] + gate_rows, row_outs=[(t, d, BF16, 0)])[0]
    fo = _mm("mix_out", z, wmo, "nn", F32)
    h2 = resid("resid2", h1, ctx_tiles, fo, nt_x, ctx_tiles, 5, 1.0)

    n3 = norm_mod("norm_mod3", h2, nt_x, ctx_tiles, 6, 7)
    hm2, ua2, ub2 = _mm_swiglu("ffn2_in", n3, w2i)
    f2 = _mm("ffn2_out", hm2, w2o, "nn", F32)
    h3 = resid("resid3", h2, 0, f2, nt_x, ctx_tiles, 8, 0.5)

    def loss_fn(rows, sel, fulls):
        h, tgt = rows
        y, vjp = jax.vjp(lambda hh, ww: _rmsn(hh) * ww, h, fulls[0])
        err = y - tgt
        dh, dw = vjp(err / d)
        return [dh], [0.5 / d * jnp.sum(err * err, axis=0, keepdims=True), dw]

    dh3, loss_acc = _rowwise("loss_head", loss_fn, n_tiles=nt_x, tr=tr,
                             row_ins=[(h3, 0, None), (loss_target[0], 0, None)], row_outs=[(t, d, F32, 0)],
                             full_ins=[final_norm[None, :]], acc_shape=(8, d), sel_off=ctx_tiles, ctx_rows=tc)
    loss_cols, g_final = loss_acc[1, 0:1], loss_acc[1, 1:2]

    def gate_bwd(name, dh, f, n_tiles, off, i_gate, coef):
        def fn(rows, sel, fulls):
            dhh, fv = rows
            return [coef * sel(i_gate) * dhh], [jnp.sum(coef * dhh * fv, axis=0, keepdims=True)]
        return _rowwise(name, fn, n_tiles=n_tiles, tr=tr, row_ins=[(dh, 0, None), (f, 0, None)],
                        row_outs=[(dh.shape[0], d, BF16, 0)], sel_in=mods, sel_off=off, ctx_rows=tc,
                        acc_shape=(8, d))

    def swiglu_bwd(name, dhm, ua, ub):
        rows_n = dhm.shape[0]
        tr_w = _tile(tr, 128, 32)

        def fn(rows, sel, fulls):
            g, a, b = rows
            _, vjp = jax.vjp(lambda aa, bb: _silu(aa) * bb, a.astype(F32), b.astype(F32))
            da, db = vjp(g)
            return [jnp.concatenate([da, db], axis=1)], []
        return _rowwise(name, fn, n_tiles=rows_n // tr_w, tr=tr_w,
                        row_ins=[(dhm, 0, None), (ua, 0, None), (ub, 0, None)],
                        row_outs=[(rows_n, 2 * ff, BF16, 0)])[0]

    def norm_mod_bwd(name, dn, h, dres, n_tiles, off, i_shift, i_scale):
        def fn(rows, sel, fulls):
            g, hh, dr = rows
            _, vjp = jax.vjp(lambda a, sh, sc: _rmsn(a) * (1.0 + sc) + sh, hh,
                             sel(i_shift), sel(i_scale))
            dh, dsh, dsc = vjp(g)
            return [dr + dh], [dsh, dsc]
        return _rowwise(name, fn, n_tiles=n_tiles, tr=tr, row_ins=[(dn, 0, None), (h, 0, None), (dres, 0, None)],
                        row_outs=[(dn.shape[0], d, F32, 0)], sel_in=mods, sel_off=off, ctx_rows=tc,
                        acc_shape=(8, d))

    df2, acc_g3 = gate_bwd("gate_bwd3", dh3, f2, nt_x, ctx_tiles, 8, 0.5)
    g_w2o = _mm("ffn2_out_dw", hm2, df2, "tn", BF16)
    dhm2 = _mm("ffn2_out_dx", df2, w2o, "nt", F32)
    du2 = swiglu_bwd("swiglu_bwd2", dhm2, ua2, ub2)
    g_w2i = _mm("ffn2_in_dw", n3, du2, "tn", BF16)
    dn3 = _mm("ffn2_in_dx", du2, w2i, "nt", F32)
    dh2, acc_n3 = norm_mod_bwd("norm_mod_bwd3", dn3, h2, dh3, nt_x, ctx_tiles, 6, 7)

    dfo, acc_g2 = gate_bwd("gate_bwd2", dh2, fo, nt_x, ctx_tiles, 5, 1.0)
    g_wmo = _mm("mix_out_dw", z, dfo, "tn", BF16)
    dz = _mm("mix_out_dx", dfo, wmo, "nt", F32)

    def merge_bwd(rows, sel, fulls):
        g, a, r, ga, gb = rows
        _, vjp = jax.vjp(merge_fn, a, r, ga, gb)
        da, dr, dga, dgb = vjp(g)
        return [da, dr, jnp.concatenate([dga, dgb], axis=1)], []

    dpa, dpr, dgab = _rowwise("merge_bwd", merge_bwd, n_tiles=nt_x, tr=tr,
                              row_ins=[(dz, 0, None), (pa, 0, None), (prj, 0, None)] + gate_rows,
                              row_outs=[(t, d, BF16, 0), (t, d, BF16, 0), (t, 2 * d, BF16, 0)])
    g_wpa = _mm("proj_attn_dw", ya, dpa, "tn", BF16)
    dya = _mm("proj_attn_dx", dpa, wpa, "nt", BF16)
    g_wpr = _mm("proj_ret_dw", yr, dpr, "tn", BF16)
    dyr = _mm("proj_ret_dx", dpr, wpr, "nt", F32)

    def ret_out_bwd(rows, sel, fulls):
        def per_head(g, yf, yb, gr):
            _, vjp = jax.vjp(lambda yy, gg: ret_out_fn(yy, 0.0, gg)[0], yf + yb, gr)
            return list(vjp(g))
        dy, dgr = _heads_map(per_head, rows, rw)
        return [dy, dgr], []

    dy_ret, dgr = _rowwise("ret_out_bwd", ret_out_bwd, n_tiles=nt_x, tr=tr, row_ins=[(dyr, 0, None)] + y_rows,
                           row_outs=[(t, rw, F32, 0), (t, rw, BF16, 0)])
    dy_all = jnp.concatenate([jnp.zeros((tc, rw), F32), dy_ret], axis=0)
    dq_r, dk_r, dv_r, dlg = _ret_bwd(p_r, states, dy_all, lgv, tc)

    def ret_sum(rows, sel, fulls):
        return [jnp.concatenate([rows[0] + rows[1], rows[2] + rows[3], rows[4] + rows[5]], axis=1)], []

    dp_r = _rowwise("ret_bwd_sum", ret_sum, n_tiles=nt_all, tr=tr,
                    row_ins=[(dq_r, 0, None), (dq_r, nt_all, None), (dk_r, 0, None), (dk_r, nt_all, None),
                             (dv_r, 0, None), (dv_r, nt_all, None)],
                    row_outs=[(tk, 3 * rw, BF16, 0)])[0]

    dq_rot = _flash_dq(q_rot, k_rot, v_att, ya, dya, lse, groups)
    dk_rot, dv_att = _flash_dkv(q_rot, k_rot, v_att, ya, dya, lse, groups)

    def q_prep_bwd(rows, sel, fulls):
        g, p, cf, ss = rows
        gain_acc = []

        def per_head(gh, ph):
            dp, dgain = _qk_rot_bwd(gh, ph, fulls[0], cf, ss)
            gain_acc.append(dgain)
            return [dp]
        dp = _heads_map(per_head, [g, p], aw)[0]
        return [dp], [functools.reduce(lambda a, b: a + b, gain_acc)]

    dp_q, acc_gq = _rowwise("q_prep_bwd", q_prep_bwd, n_tiles=nt_x, tr=tr,
                            row_ins=[(dq_rot, 0, None), (p_q, ctx_tiles, None), (cos_full, ctx_tiles, None),
                                     (sin_signed, ctx_tiles, None)],
                            row_outs=[(t, aw, BF16, 0)], full_ins=[q_gain], acc_shape=(8, HEAD_DIM),
                            sel_off=ctx_tiles, ctx_rows=tc)

    def kv_prep_bwd(rows, sel, fulls):
        gk, gv, p, cf, ss = rows
        gain_acc = []

        def per_head(gh, ph):
            dp, dgain = _qk_rot_bwd(gh, ph, fulls[0], cf, ss)
            gain_acc.append(dgain)
            return [dp]
        dpk = _heads_map(per_head, [gk, p], kvw)[0]
        return [jnp.concatenate([dpk, gv], axis=1)], [functools.reduce(lambda a, b: a + b, gain_acc)]

    dp_kv, acc_gk = _rowwise("kv_prep_bwd", kv_prep_bwd, n_tiles=nt_all, tr=tr,
                             row_ins=[(dk_rot, 0, None), (dv_att, 0, None), (p_kv, 0, (kvw, 0)), (cos_full, 0, None),
                                      (sin_signed, 0, None)],
                             row_outs=[(tk, 2 * kvw, BF16, 0)], full_ins=[k_gain], acc_shape=(8, HEAD_DIM),
                             sel_off=0, ctx_rows=tc)

    def with_ctx_zeros(a):
        return jnp.concatenate([jnp.zeros((tc, a.shape[1]), a.dtype), a], axis=0)

    dp = jnp.concatenate([with_ctx_zeros(dp_q), dp_kv, dp_r, with_ctx_zeros(dgr), with_ctx_zeros(dgab)], axis=1)
    g_wmi = _mm("mix_in_dw", n2, dp, "tn", BF16)
    dn2 = _mm("mix_in_dx", dp, wmi, "nt", F32)
    dh1, acc_n2 = norm_mod_bwd("norm_mod_bwd2", dn2, h1, with_ctx_zeros(dh2), nt_all, 0, 3, 4)

    df1, acc_g1 = gate_bwd("gate_bwd1", dh1, f1, nt_all, 0, 2, 0.5)
    g_w1o = _mm("ffn1_out_dw", hm1, df1, "tn", BF16)
    dhm1 = _mm("ffn1_out_dx", df1, w1o, "nt", F32)
    du1 = swiglu_bwd("swiglu_bwd1", dhm1, ua1, ub1)
    g_w1i = _mm("ffn1_in_dw", n1, du1, "tn", BF16)
    dn1 = _mm("ffn1_in_dx", du1, w1i, "nt", F32)
    dh0, acc_n1 = norm_mod_bwd("norm_mod_bwd1", dn1, h0, dh1, nt_all, 0, 0, 1)
    grad_x = dh0[tc:][None]

    grads_full = [g_w1i, g_w1o, g_wmi, g_wpa, g_wpr, g_wmo, g_w2i, g_w2o]
    landed = _reduce_pair(metas, grads_full)
    halves = [_pair_sum(m, g, l, c_arr) for m, g, l in zip(metas, grads_full, landed)]
    landed3 = _reduce_chips(metas, halves)
    pieces = [_sum_pieces(m, p, l, s_arr) for m, p, l in zip(metas, halves, landed3)]
    grads_big = _share_halves(metas, pieces)

    zero_row = jnp.zeros((1, d), F32)
    dmod_x = jnp.concatenate([acc_n1[1, 0:1], acc_n1[1, 1:2], acc_g1[1, 0:1], acc_n2[1, 0:1], acc_n2[1, 1:2],
                              acc_g2[1, 0:1], acc_n3[1, 0:1], acc_n3[1, 1:2], acc_g3[1, 0:1]], axis=1)
    dmod_c = jnp.concatenate([acc_n1[0, 0:1], acc_n1[0, 1:2], acc_g1[0, 0:1], acc_n2[0, 0:1], acc_n2[0, 1:2]]
                             + [zero_row] * 4, axis=1)
    dlg_row = jnp.pad(dlg[:, :, 0, 0].reshape(1, 2 * n_ret_heads), ((0, 0), (0, LANES_V7X - 2 * n_ret_heads)))
    packed = jnp.concatenate([dmod_x, dmod_c, acc_gq[1, 0:1], acc_gk[0, 0:1] + acc_gk[1, 0:1], dlg_row,
                              g_final, loss_cols], axis=1)
    off_gq = 2 * mod_cols
    off_gk = off_gq + LANES_V7X
    off_lg = off_gk + LANES_V7X
    off_fn = off_lg + LANES_V7X
    off_loss = off_fn + d
    gathered = _all_gather8("gather_small", jnp.pad(packed, ((0, 7), (0, 0))))[:, 0, :]
    logits_row = jnp.pad(ret_decay_logit.reshape(1, 2 * n_ret_heads), ((0, 0), (0, LANES_V7X - 2 * n_ret_heads)))
    totals, g_b_ada, g_decay, loss_row = _small_reduce(gathered, logits_row, mod_cols, off_lg, off_loss, d)
    loss = loss_row[0, 0]

    dm = jnp.concatenate([gathered[:, :mod_cols], totals[:, mod_cols:2 * mod_cols],
                          jnp.zeros((7, mod_cols), F32)], axis=0)
    dm_l = lax.dynamic_slice_in_dim(dm, s_me * ada_cols, ada_cols, axis=1)
    g_w_ada, da_part = _ada_bwd(cg, dm_l, w_ada_l)
    da_rows = _all_gather8("gather_dc", jnp.pad(da_part[8:9], ((0, 7), (0, 0))))[:, 0, :]
    g_c_ctx = _c_ctx_grad(da_rows, c_ctx[None, :])

    def as2d(a):
        return a.reshape(-1, a.shape[-1])

    grads = {
        "c_ctx": g_c_ctx, "w_ada": g_w_ada, "b_ada": g_b_ada,
        "ffn1_w_in": grads_big[0], "ffn1_w_out": grads_big[1], "mix_w_in": grads_big[2],
        "attn_q_gain": totals[:, off_gq:off_gq + HEAD_DIM], "attn_k_gain": totals[:, off_gk:off_gk + HEAD_DIM],
        "ret_decay_logit": g_decay[:, :2 * n_ret_heads],
        "w_proj_attn": grads_big[3], "w_proj_ret": grads_big[4], "mix_w_out": grads_big[5],
        "ffn2_w_in": grads_big[6], "ffn2_w_out": grads_big[7], "final_norm": totals[:, off_fn:off_fn + d],
    }
    weights = {"c_ctx": (c_ctx, m_c_ctx, v_c_ctx), "w_ada": (w_ada, m_w_ada, v_w_ada),
               "b_ada": (b_ada, m_b_ada, v_b_ada), "ffn1_w_in": (ffn1_w_in, m_ffn1_w_in, v_ffn1_w_in),
               "ffn1_w_out": (ffn1_w_out, m_ffn1_w_out, v_ffn1_w_out), "mix_w_in": (mix_w_in, m_mix_w_in, v_mix_w_in),
               "attn_q_gain": (attn_q_gain, m_attn_q_gain, v_attn_q_gain),
               "attn_k_gain": (attn_k_gain, m_attn_k_gain, v_attn_k_gain),
               "ret_decay_logit": (ret_decay_logit, m_ret_decay_logit, v_ret_decay_logit),
               "w_proj_attn": (w_proj_attn, m_w_proj_attn, v_w_proj_attn),
               "w_proj_ret": (w_proj_ret, m_w_proj_ret, v_w_proj_ret), "mix_w_out": (mix_w_out, m_mix_w_out, v_mix_w_out),
               "ffn2_w_in": (ffn2_w_in, m_ffn2_w_in, v_ffn2_w_in), "ffn2_w_out": (ffn2_w_out, m_ffn2_w_out, v_ffn2_w_out),
               "final_norm": (final_norm, m_final_norm, v_final_norm)}
    out_g, out_d, out_m, out_v = [], [], [], []
    for name, (w, m, v) in weights.items():
        shape = w.shape
        if name == "ret_decay_logit":
            w2, m2, v2 = (a.reshape(1, -1) for a in (w, m, v))
        else:
            w2, m2, v2 = as2d(w), as2d(m), as2d(v)
        g2 = grads[name].reshape(w2.shape)
        delta, new_m, new_v = _adamw(w2, g2, m2, v2)
        out_g.append(g2.reshape(shape))
        out_d.append(delta.reshape(shape))
        out_m.append(new_m.reshape(shape))
        out_v.append(new_v.reshape(shape))
    return (loss, grad_x, *out_g, *out_d, *out_m, *out_v)
```

```python
import functools
import math

import jax
import jax.numpy as jnp
from jax import lax
from jax.experimental import pallas as pl
from jax.experimental.pallas import tpu as pltpu

F32 = jnp.float32
BF16 = jnp.bfloat16

HEAD_DIM = 128
GRID_W = 64
ROPE_THETA = 10000.0
NORM_EPS = 1e-6
N_MOD = 9
RET_CHUNK = 128
ADAM_LR = 0.001
ADAM_B1 = 0.9
ADAM_B2 = 0.999
ADAM_EPS = 1e-08
ADAM_WD = 0.01
ADAM_STEP = 10

N_DEV = 8
N_CHIP = 4
LANES_V7X = 128
VMEM_LIMIT_V7X = 52 * 1024 * 1024

NT_DIMS = (((1,), (1,)), ((), ()))
TN_DIMS = (((0,), (0,)), ((), ()))
NN_DIMS = (((1,), (0,)), ((), ()))


def _tile(n, pref, mult=LANES_V7X):
    if n <= pref:
        return n
    t = (pref // mult) * mult
    while t >= mult:
        if n % t == 0:
            return t
        t -= mult
    return n


def _params(sem):
    return pltpu.CompilerParams(dimension_semantics=sem, vmem_limit_bytes=VMEM_LIMIT_V7X)


def _sigmoid(x):
    return 1.0 / (1.0 + jnp.exp(-x))


def _silu(x):
    return x * _sigmoid(x)


def _rmsn(x):
    return x * lax.rsqrt(jnp.mean(x * x, axis=-1, keepdims=True) + NORM_EPS)


def _mm(name, a, b, mode, out_dtype, b_off=0, n=None):
    if mode == "nn":
        m, k = a.shape
        n = b.shape[1] if n is None else n
        dims = NN_DIMS
    elif mode == "nt":
        m, k = a.shape
        n = b.shape[0]
        dims = NT_DIMS
    else:
        k, m = a.shape
        n = b.shape[1]
        dims = TN_DIMS
    tm = _tile(m, 1024)
    tn = _tile(math.gcd(n, b_off), 1024) if b_off else _tile(n, 1024)
    tk = _tile(k, 2560) if mode != "tn" else _tile(k, 1024)
    nk = k // tk
    joff = b_off // tn

    def body(a_ref, b_ref, o_ref, acc_ref):
        kk = pl.program_id(2)

        @pl.when(kk == 0)
        def _():
            acc_ref[...] = jnp.zeros_like(acc_ref)

        acc_ref[...] += lax.dot_general(a_ref[...], b_ref[...], dims, preferred_element_type=F32)

        @pl.when(kk == nk - 1)
        def _():
            o_ref[...] = acc_ref[...].astype(o_ref.dtype)

    if mode == "nn":
        a_spec = pl.BlockSpec((tm, tk), lambda i, j, kk: (i, kk))
        b_spec = pl.BlockSpec((tk, tn), lambda i, j, kk: (kk, j + joff))
    elif mode == "nt":
        a_spec = pl.BlockSpec((tm, tk), lambda i, j, kk: (i, kk))
        b_spec = pl.BlockSpec((tn, tk), lambda i, j, kk: (j, kk))
    else:
        a_spec = pl.BlockSpec((tk, tm), lambda i, j, kk: (kk, i))
        b_spec = pl.BlockSpec((tk, tn), lambda i, j, kk: (kk, j))
    return pl.pallas_call(
        body, name=name, grid=(m // tm, n // tn, nk),
        in_specs=[a_spec, b_spec],
        out_specs=pl.BlockSpec((tm, tn), lambda i, j, kk: (i, j)),
        out_shape=jax.ShapeDtypeStruct((m, n), out_dtype),
        scratch_shapes=[pltpu.VMEM((tm, tn), F32)],
        compiler_params=_params(("parallel", "parallel", "arbitrary")),
    )(a, b)


def _mm_swiglu(name, a, w):
    m, k = a.shape
    f = w.shape[1] // 2
    tm = _tile(m, 1024)
    tn = _tile(f, 512)
    tk = _tile(k, 2560)
    nk = k // tk
    jf = f // tn

    def body(a_ref, wa_ref, wb_ref, h_ref, ua_ref, ub_ref, acca, accb):
        kk = pl.program_id(2)

        @pl.when(kk == 0)
        def _():
            acca[...] = jnp.zeros_like(acca)
            accb[...] = jnp.zeros_like(accb)

        av = a_ref[...]
        acca[...] += jnp.dot(av, wa_ref[...], preferred_element_type=F32)
        accb[...] += jnp.dot(av, wb_ref[...], preferred_element_type=F32)

        @pl.when(kk == nk - 1)
        def _():
            ua = acca[...]
            ub = accb[...]
            h_ref[...] = (_silu(ua) * ub).astype(BF16)
            ua_ref[...] = ua.astype(BF16)
            ub_ref[...] = ub.astype(BF16)

    o_spec = pl.BlockSpec((tm, tn), lambda i, j, kk: (i, j))
    o_shape = jax.ShapeDtypeStruct((m, f), BF16)
    return pl.pallas_call(
        body, name=name, grid=(m // tm, jf, nk),
        in_specs=[pl.BlockSpec((tm, tk), lambda i, j, kk: (i, kk)),
                  pl.BlockSpec((tk, tn), lambda i, j, kk: (kk, j)),
                  pl.BlockSpec((tk, tn), lambda i, j, kk: (kk, j + jf))],
        out_specs=[o_spec, o_spec, o_spec],
        out_shape=[o_shape, o_shape, o_shape],
        scratch_shapes=[pltpu.VMEM((tm, tn), F32), pltpu.VMEM((tm, tn), F32)],
        compiler_params=_params(("parallel", "parallel", "arbitrary")),
    )(a, w, w)


def _rowwise(name, fn, *, n_tiles, tr, row_ins, row_outs, sel_in=None, sel_off=0, ctx_rows=0,
             full_ins=(), acc_shape=None):
    sr = 32 if tr % 32 == 0 else tr
    n_row, n_full, n_out = len(row_ins), len(full_ins), len(row_outs)
    has_sel = sel_in is not None
    has_acc = acc_shape is not None

    def sel_of(i):
        return jnp.where((i + sel_off) * tr < ctx_rows, 0, 1)

    def body(*refs):
        row_refs = refs[:n_row]
        pos = n_row
        sel_ref = None
        if has_sel:
            sel_ref = refs[pos]
            pos += 1
        full_refs = refs[pos:pos + n_full]
        pos += n_full
        out_refs = refs[pos:pos + n_out]
        pos += n_out
        acc_ref = refs[pos] if has_acc else None
        i = pl.program_id(0)
        if has_acc:
            first = (i == 0) | ((i + sel_off) * tr == ctx_rows)

            @pl.when(first)
            def _():
                acc_ref[...] = jnp.zeros_like(acc_ref)

        sel = (lambda kk: sel_ref[kk:kk + 1, :]) if has_sel else None
        fulls = [r[...] for r in full_refs]

        def slab(r, carry):
            rs = pl.ds(pl.multiple_of(r * sr, sr), sr)
            rows = [ref[rs, :] for ref in row_refs]
            outs, accs = fn(rows, sel, fulls)
            for o_ref, o in zip(out_refs, outs):
                o_ref[rs, :] = o.astype(o_ref.dtype)
            for kk, a in enumerate(accs):
                acc_ref[kk:kk + 1, :a.shape[1]] += a
            return carry

        lax.fori_loop(0, tr // sr, slab, 0)

    in_specs, args = [], []
    for arr, off, blk in row_ins:
        if blk is None:
            in_specs.append(pl.BlockSpec((tr, arr.shape[1]), functools.partial(lambda i, o: (i + o, 0), o=off)))
        else:
            in_specs.append(pl.BlockSpec((tr, blk[0]), functools.partial(lambda i, o, cb: (i + o, cb), o=off, cb=blk[1])))
        args.append(arr)
    if has_sel:
        in_specs.append(pl.BlockSpec((None,) + sel_in.shape[1:], lambda i: (sel_of(i), 0, 0)))
        args.append(sel_in)
    for arr in full_ins:
        in_specs.append(pl.BlockSpec(arr.shape, lambda i: (0, 0)))
        args.append(arr)
    out_specs, out_shape = [], []
    for rows, cols, dt, off in row_outs:
        out_specs.append(pl.BlockSpec((tr, cols), functools.partial(lambda i, o: (i + o, 0), o=off)))
        out_shape.append(jax.ShapeDtypeStruct((rows, cols), dt))
    if has_acc:
        out_specs.append(pl.BlockSpec((None,) + tuple(acc_shape), lambda i: (sel_of(i), 0, 0)))
        out_shape.append(jax.ShapeDtypeStruct((2,) + tuple(acc_shape), F32))
    return pl.pallas_call(
        body, name=name, grid=(n_tiles,), in_specs=in_specs, out_specs=out_specs, out_shape=out_shape,
        compiler_params=_params(("arbitrary",)),
    )(*args)


def _swap_pairs(x):
    lane = lax.broadcasted_iota(jnp.int32, x.shape, 1)
    nxt = pltpu.roll(x, x.shape[1] - 1, 1)
    prv = pltpu.roll(x, 1, 1)
    return jnp.where(lane % 2 == 0, nxt, prv)


def _heads_map(fn, arrs, width):
    outs = None
    for h in range(width // HEAD_DIM):
        sl = slice(h * HEAD_DIM, (h + 1) * HEAD_DIM)
        res = fn(*[a[:, sl] for a in arrs])
        if outs is None:
            outs = [[] for _ in res]
        for lst, r in zip(outs, res):
            lst.append(r)
    return [jnp.concatenate(lst, axis=1) if len(lst) > 1 else lst[0] for lst in outs]


QSCALE = HEAD_DIM ** -0.5 * math.log2(math.e)
LN2 = math.log(2.0)


def _lane_chunks(a):
    return [a[:, cc * LANES_V7X:(cc + 1) * LANES_V7X] for cc in range(a.shape[1] // LANES_V7X)]


def _row_bcast(col, like):
    return jnp.broadcast_to(col, like.shape)


def _flash_tiles(t, tk_all):
    return _tile(t, 256), _tile(tk_all, 1024)


def _flash_fwd(q, k, vx, groups):
    t, aw = q.shape
    tk_all, kvw = k.shape
    kvh = kvw // HEAD_DIM
    gw = groups * HEAD_DIM
    tq, tk = _flash_tiles(t, tk_all)
    nk = tk_all // tk

    def body(q_ref, k_ref, v_ref, o_ref, lse_ref, m_sc, l_sc, acc_sc):
        j = pl.program_id(2)

        @pl.when(j == 0)
        def _():
            m_sc[...] = jnp.full_like(m_sc, -jnp.inf)
            l_sc[...] = jnp.zeros_like(l_sc)
            acc_sc[...] = jnp.zeros_like(acc_sc)

        kt = k_ref[...]
        vt = v_ref[...]
        for g in range(groups):
            sl = slice(g * HEAD_DIM, (g + 1) * HEAD_DIM)
            s = _lane_chunks(lax.dot_general(q_ref[:, sl], kt, NT_DIMS, preferred_element_type=F32))
            mx = functools.reduce(jnp.maximum, s)
            m_prev = m_sc[g]
            m_new = jnp.maximum(m_prev, _row_bcast(jnp.max(mx, axis=1, keepdims=True), mx))
            p = jnp.concatenate([jnp.exp2(sc - m_new).astype(BF16) for sc in s], axis=1)
            alpha = jnp.exp2(m_prev - m_new)
            pv = jnp.dot(p, vt, preferred_element_type=F32)
            acc_sc[g] = alpha * acc_sc[g] + pv[:, :HEAD_DIM]
            l_sc[g] = alpha * l_sc[g] + pv[:, HEAD_DIM:]
            m_sc[g] = m_new

        @pl.when(j == nk - 1)
        def _():
            for g in range(groups):
                sl = slice(g * HEAD_DIM, (g + 1) * HEAD_DIM)
                o_ref[:, sl] = (acc_sc[g] / l_sc[g]).astype(o_ref.dtype)
                lse_ref[:, sl] = m_sc[g] + jnp.log2(l_sc[g])

    qs = pl.BlockSpec((tq, gw), lambda kh, i, j: (i, kh))
    sc = pltpu.VMEM((groups, tq, HEAD_DIM), F32)
    return pl.pallas_call(
        body, name="flash_fwd", grid=(kvh, t // tq, nk),
        in_specs=[qs, pl.BlockSpec((tk, HEAD_DIM), lambda kh, i, j: (j, kh)),
                  pl.BlockSpec((tk, 2 * HEAD_DIM), lambda kh, i, j: (j, kh))],
        out_specs=[qs, qs],
        out_shape=[jax.ShapeDtypeStruct((t, aw), BF16), jax.ShapeDtypeStruct((t, aw), F32)],
        scratch_shapes=[sc, sc, sc],
        compiler_params=_params(("parallel", "parallel", "arbitrary")),
    )(q, k, vx)


def _flash_p_ds(q, kt, vt, do, lse, delta):
    s = _lane_chunks(lax.dot_general(q, kt, NT_DIMS, preferred_element_type=F32))
    dp = _lane_chunks(lax.dot_general(do, vt, NT_DIMS, preferred_element_type=F32))
    p = [jnp.exp2(sc - lse) for sc in s]
    ds = jnp.concatenate([(pc * (dc - delta)).astype(BF16) for pc, dc in zip(p, dp)], axis=1)
    return jnp.concatenate([pc.astype(BF16) for pc in p], axis=1), ds


def _flash_delta(do, o):
    prod = do.astype(F32) * o.astype(F32)
    return _row_bcast(jnp.sum(prod, axis=1, keepdims=True), prod)


def _flash_dq(q, k, vx, o, do, lse, groups):
    t, aw = q.shape
    tk_all, kvw = k.shape
    kvh = kvw // HEAD_DIM
    gw = groups * HEAD_DIM
    tq, tk = _flash_tiles(t, tk_all)
    nk = tk_all // tk

    def body(q_ref, k_ref, v_ref, o_ref, do_ref, lse_ref, dq_ref, acc_sc, dl_sc):
        j = pl.program_id(2)

        @pl.when(j == 0)
        def _():
            acc_sc[...] = jnp.zeros_like(acc_sc)
            for g in range(groups):
                sl = slice(g * HEAD_DIM, (g + 1) * HEAD_DIM)
                dl_sc[g] = _flash_delta(do_ref[:, sl], o_ref[:, sl])

        kt = k_ref[...]
        vt = v_ref[:, :HEAD_DIM]
        for g in range(groups):
            sl = slice(g * HEAD_DIM, (g + 1) * HEAD_DIM)
            _, ds = _flash_p_ds(q_ref[:, sl], kt, vt, do_ref[:, sl], lse_ref[:, sl], dl_sc[g])
            acc_sc[g] += jnp.dot(ds, kt, preferred_element_type=F32)

        @pl.when(j == nk - 1)
        def _():
            for g in range(groups):
                dq_ref[:, g * HEAD_DIM:(g + 1) * HEAD_DIM] = acc_sc[g]

    qs = pl.BlockSpec((tq, gw), lambda kh, i, j: (i, kh))
    sc = pltpu.VMEM((groups, tq, HEAD_DIM), F32)
    return pl.pallas_call(
        body, name="flash_dq", grid=(kvh, t // tq, nk),
        in_specs=[qs, pl.BlockSpec((tk, HEAD_DIM), lambda kh, i, j: (j, kh)),
                  pl.BlockSpec((tk, 2 * HEAD_DIM), lambda kh, i, j: (j, kh)), qs, qs, qs],
        out_specs=qs,
        out_shape=jax.ShapeDtypeStruct((t, aw), F32),
        scratch_shapes=[sc, sc],
        compiler_params=_params(("parallel", "parallel", "arbitrary")),
    )(q, k, vx, o, do, lse)


def _flash_dkv(q, k, vx, o, do, lse, groups):
    t, aw = q.shape
    tk_all, kvw = k.shape
    kvh = kvw // HEAD_DIM
    gw = groups * HEAD_DIM
    tq, tk = _flash_tiles(t, tk_all)
    nq = t // tq

    def body(q_ref, k_ref, v_ref, o_ref, do_ref, lse_ref, dk_ref, dv_ref, dk_acc, dv_acc):
        i = pl.program_id(2)

        @pl.when(i == 0)
        def _():
            dk_acc[...] = jnp.zeros_like(dk_acc)
            dv_acc[...] = jnp.zeros_like(dv_acc)

        kt = k_ref[...]
        vt = v_ref[:, :HEAD_DIM]
        for g in range(groups):
            sl = slice(g * HEAD_DIM, (g + 1) * HEAD_DIM)
            qv = q_ref[:, sl]
            dov = do_ref[:, sl]
            p, ds = _flash_p_ds(qv, kt, vt, dov, lse_ref[:, sl], _flash_delta(dov, o_ref[:, sl]))
            dv_acc[...] += lax.dot_general(p, dov, TN_DIMS, preferred_element_type=F32)
            dk_acc[...] += lax.dot_general(ds, qv, TN_DIMS, preferred_element_type=F32)

        @pl.when(i == nq - 1)
        def _():
            dk_ref[...] = dk_acc[...] * LN2
            dv_ref[...] = dv_acc[...]

    qs = pl.BlockSpec((tq, gw), lambda kh, j, i: (i, kh))
    ks = pl.BlockSpec((tk, HEAD_DIM), lambda kh, j, i: (j, kh))
    return pl.pallas_call(
        body, name="flash_dkv", grid=(kvh, tk_all // tk, nq),
        in_specs=[qs, ks, pl.BlockSpec((tk, 2 * HEAD_DIM), lambda kh, j, i: (j, kh)), qs, qs, qs],
        out_specs=[ks, ks],
        out_shape=[jax.ShapeDtypeStruct((tk_all, kvw), F32), jax.ShapeDtypeStruct((tk_all, kvw), F32)],
        scratch_shapes=[pltpu.VMEM((tk, HEAD_DIM), F32), pltpu.VMEM((tk, HEAD_DIM), F32)],
        compiler_params=_params(("parallel", "parallel", "arbitrary")),
    )(q, k, vx, o, do, lse)


def _bf_nn(a, b):
    return jnp.dot(a.astype(BF16), b.astype(BF16), preferred_element_type=F32)


def _bf_nt(a, b):
    return lax.dot_general(a.astype(BF16), b.astype(BF16), NT_DIMS, preferred_element_type=F32)


def _bf_tn(a, b):
    return lax.dot_general(a.astype(BF16), b.astype(BF16), TN_DIMS, preferred_element_type=F32)


@jax.custom_vjp
def _d_nn(a, b):
    return _bf_nn(a, b)


@jax.custom_vjp
def _d_nt(a, b):
    return _bf_nt(a, b)


@jax.custom_vjp
def _d_tn(a, b):
    return _bf_tn(a, b)


_d_nn.defvjp(lambda a, b: (_bf_nn(a, b), (a, b)), lambda r, g: (_d_nt(g, r[1]), _d_tn(r[0], g)))
_d_nt.defvjp(lambda a, b: (_bf_nt(a, b), (a, b)), lambda r, g: (_d_nn(g, r[1]), _d_tn(g, r[0])))
_d_tn.defvjp(lambda a, b: (_bf_tn(a, b), (a, b)), lambda r, g: (_d_nt(r[1], g), _d_nn(r[0], g)))


def _ret_chunk(q, k_raw, v, state, lg, rev, dots):
    nn, nt, tn = dots
    c = RET_CHUNK
    tcol = lax.broadcasted_iota(jnp.int32, (c, 1), 0).astype(F32)
    trow = lax.broadcasted_iota(jnp.int32, (1, c), 1).astype(F32)
    ucol = jnp.where(rev, c - 1.0 - tcol, tcol)
    urow = jnp.where(rev, c - 1.0 - trow, trow)
    e = ucol - urow
    low = e >= 0
    intra = jnp.where(low, jnp.exp(jnp.where(low, e, 0.0) * lg), 0.0)
    k = k_raw * (HEAD_DIM ** -0.5)
    inner = nt(q, k) * intra
    y = nn(inner, v) + nn(q, state) * jnp.exp((ucol + 1.0) * lg)
    new_state = state * jnp.exp(c * lg) + tn(k * jnp.exp((c - 1.0 - ucol) * lg), v)
    return y, new_state


def _ret_chunk_index(n_chunks, n_ctx_chunks):
    def idx(d, s):
        rev = jnp.where(s < n_ctx_chunks, n_ctx_chunks - 1 - s, n_chunks - 1 - s + n_ctx_chunks)
        return jnp.where(d == 0, s, rev)
    return idx


def _ret_fwd(pr, lgv, ctx_rows):
    tk_all = pr.shape[0]
    rw = pr.shape[1] // 3
    nh = rw // HEAD_DIM
    nc = tk_all // RET_CHUNK
    cidx = _ret_chunk_index(nc, ctx_rows // RET_CHUNK)

    def body(q_ref, k_ref, v_ref, lg_ref, y_ref, st_ref, s_sc):
        d = pl.program_id(0)
        s = pl.program_id(2)

        @pl.when(s == 0)
        def _():
            s_sc[...] = jnp.zeros_like(s_sc)

        state = s_sc[...]
        st_ref[...] = state
        y, new_state = _ret_chunk(q_ref[...], k_ref[...], v_ref[...], state, lg_ref[:, :1], d == 1,
                                  (_bf_nn, _bf_nt, _bf_tn))
        y_ref[...] = y
        s_sc[...] = new_state

    blk = (RET_CHUNK, HEAD_DIM)
    return pl.pallas_call(
        body, name="ret_fwd", grid=(2, nh, nc),
        in_specs=[pl.BlockSpec(blk, lambda d, h, s: (cidx(d, s), h)),
                  pl.BlockSpec(blk, lambda d, h, s: (cidx(d, s), nh + h)),
                  pl.BlockSpec(blk, lambda d, h, s: (cidx(d, s), 2 * nh + h)),
                  pl.BlockSpec((None, None, 1, HEAD_DIM), lambda d, h, s: (d, h, 0, 0))],
        out_specs=[pl.BlockSpec(blk, lambda d, h, s: (d * nc + cidx(d, s), h)),
                   pl.BlockSpec((None, None, None, HEAD_DIM, HEAD_DIM), lambda d, h, s: (d, h, s, 0, 0))],
        out_shape=[jax.ShapeDtypeStruct((2 * tk_all, rw), F32),
                   jax.ShapeDtypeStruct((2, nh, nc, HEAD_DIM, HEAD_DIM), F32)],
        scratch_shapes=[pltpu.VMEM((HEAD_DIM, HEAD_DIM), F32)],
        compiler_params=_params(("parallel", "parallel", "arbitrary")),
    )(pr, pr, pr, lgv)


def _ret_bwd(pr, states, dy, lgv, ctx_rows):
    tk_all = pr.shape[0]
    rw = pr.shape[1] // 3
    nh = rw // HEAD_DIM
    nc = tk_all // RET_CHUNK
    cidx = _ret_chunk_index(nc, ctx_rows // RET_CHUNK)

    def body(q_ref, k_ref, v_ref, st_ref, dy_ref, lg_ref, dq_ref, dk_ref, dv_ref, dlg_ref, ds_sc):
        d = pl.program_id(0)
        sp = pl.program_id(2)

        @pl.when(sp == 0)
        def _():
            ds_sc[...] = jnp.zeros_like(ds_sc)
            dlg_ref[...] = jnp.zeros_like(dlg_ref)

        def step(q, k, v, state, lg):
            return _ret_chunk(q, k, v, state, lg, d == 1, (_d_nn, _d_nt, _d_tn))

        _, vjp = jax.vjp(step, q_ref[...], k_ref[...], v_ref[...], st_ref[...], lg_ref[:, :1])
        dq, dk, dv, dstate, dlg = vjp((dy_ref[...], ds_sc[...]))
        dq_ref[...] = dq
        dk_ref[...] = dk
        dv_ref[...] = dv
        ds_sc[...] = dstate
        dlg_ref[...] += jnp.broadcast_to(dlg, dlg_ref.shape)

    blk = (RET_CHUNK, HEAD_DIM)

    def at(col):
        return lambda d, h, sp: (cidx(d, nc - 1 - sp), col(h))

    o_spec = pl.BlockSpec(blk, lambda d, h, sp: (d * nc + cidx(d, nc - 1 - sp), h))
    o_shape = jax.ShapeDtypeStruct((2 * tk_all, rw), F32)
    lg_spec = pl.BlockSpec((None, None, 1, HEAD_DIM), lambda d, h, sp: (d, h, 0, 0))
    return pl.pallas_call(
        body, name="ret_bwd", grid=(2, nh, nc),
        in_specs=[pl.BlockSpec(blk, at(lambda h: h)),
                  pl.BlockSpec(blk, at(lambda h: nh + h)),
                  pl.BlockSpec(blk, at(lambda h: 2 * nh + h)),
                  pl.BlockSpec((None, None, None, HEAD_DIM, HEAD_DIM), lambda d, h, sp: (d, h, nc - 1 - sp, 0, 0)),
                  pl.BlockSpec(blk, at(lambda h: h)),
                  lg_spec],
        out_specs=[o_spec, o_spec, o_spec, lg_spec],
        out_shape=[o_shape, o_shape, o_shape, jax.ShapeDtypeStruct(lgv.shape, F32)],
        scratch_shapes=[pltpu.VMEM((HEAD_DIM, HEAD_DIM), F32)],
        compiler_params=_params(("parallel", "parallel", "arbitrary")),
    )(pr, pr, pr, states, dy, lgv)


FLIP_X, FLIP_Y, FLIP_XY, FLIP_C = (1, 0, 0), (0, 1, 0), (1, 1, 0), (0, 0, 1)
CHIP_FLIPS = ((FLIP_X, 2), (FLIP_Y, 1), (FLIP_XY, 3))


def _flip(me, mask):
    return tuple(1 - v if m else v for v, m in zip(me, mask))


def _comm(name, ins, out_shapes, plan, n_remote, n_local, aliases=None):
    n_in, n_out = len(ins), len(out_shapes)

    def body(*refs):
        in_refs = refs[:n_in]
        out_refs = refs[n_in:n_in + n_out]
        send_sems, recv_sems, local_sems = refs[n_in + n_out:]
        me = (lax.axis_index("x"), lax.axis_index("y"), lax.axis_index("c"))
        local, phases = plan(in_refs, out_refs, me)
        local_copies = [pltpu.make_async_copy(s, d, local_sems.at[i]) for i, (s, d) in enumerate(local)]
        for cp in local_copies:
            cp.start()
        sent = []
        kk = 0
        for phase in phases:
            arrivals = []
            for mask, src, dst, landing in phase:
                peer = _flip(me, mask)
                cp = pltpu.make_async_remote_copy(src_ref=src, dst_ref=dst, send_sem=send_sems.at[kk],
                                                  recv_sem=recv_sems.at[kk], device_id=peer,
                                                  device_id_type=pl.DeviceIdType.MESH)
                cp.start()
                sent.append(cp)
                arrivals.append(pltpu.make_async_remote_copy(
                    src_ref=landing, dst_ref=landing, send_sem=send_sems.at[kk], recv_sem=recv_sems.at[kk],
                    device_id=peer, device_id_type=pl.DeviceIdType.MESH))
                kk += 1
            for cp in arrivals:
                cp.wait_recv()
        for cp in sent:
            cp.wait_send()
        for cp in local_copies:
            cp.wait()

    any_spec = pl.BlockSpec(memory_space=pl.ANY)
    return pl.pallas_call(
        body, name=name,
        in_specs=[any_spec] * n_in, out_specs=[any_spec] * n_out, out_shape=list(out_shapes),
        scratch_shapes=[pltpu.SemaphoreType.DMA((n_remote,)), pltpu.SemaphoreType.DMA((n_remote,)),
                        pltpu.SemaphoreType.DMA((max(n_local, 1),))],
        input_output_aliases=aliases or {},
    )(*ins)


def _ds(start, size):
    return pl.ds(pl.multiple_of(start * size, 8), size)


def _all_gather8(name, v):
    masks = [(a, b, cc) for a in (0, 1) for b in (0, 1) for cc in (0, 1)][1:]

    def index(p):
        return 4 * p[0] + 2 * p[1] + p[2]

    def plan(in_refs, out_refs, me):
        (src,), (out,) = in_refs, out_refs
        local = [(src, out.at[index(me)])]
        phase = [(m, src, out.at[index(me)], out.at[index(_flip(me, m))]) for m in masks]
        return local, [phase]

    return _comm(name, [v], [jax.ShapeDtypeStruct((N_DEV,) + v.shape, v.dtype)], plan, len(masks), 1)[0]


def _gather_row(name, row):
    n = row.shape[1]
    n_pad = -(-n // (8 * LANES_V7X)) * (8 * LANES_V7X)
    v = jnp.pad(row, ((0, 0), (0, n_pad - n))).reshape(8, n_pad // 8)
    return _all_gather8(name, v).reshape(N_DEV, n_pad)[:, :n]


class _Sharded:
    def __init__(self, kind, rows, cols):
        self.kind, self.rows, self.cols = kind, rows, cols
        self.shard_shape = (rows, cols // N_CHIP) if kind == "col" else (rows // N_CHIP, cols)
        self.half_shape = (rows // 2, cols) if kind == "col" else (rows, cols // 2)
        self.piece_shape = (rows // 2, cols // N_CHIP) if kind == "col" else (rows // N_CHIP, cols // 2)

    def shard_of_full(self, ref, s):
        if self.kind == "col":
            return ref.at[:, _ds(s, self.cols // N_CHIP)]
        return ref.at[_ds(s, self.rows // N_CHIP), :]

    def half_of_full(self, ref, h):
        if self.kind == "col":
            return ref.at[_ds(h, self.rows // 2), :]
        return ref.at[:, _ds(h, self.cols // 2)]

    def piece_of_full(self, ref, s, h):
        if self.kind == "col":
            return ref.at[_ds(h, self.rows // 2), _ds(s, self.cols // N_CHIP)]
        return ref.at[_ds(s, self.rows // N_CHIP), _ds(h, self.cols // 2)]

    def half_of_shard(self, ref, h):
        if self.kind == "col":
            return ref.at[_ds(h, self.rows // 2), :]
        return ref.at[:, _ds(h, self.cols // 2)]

    def shard_of_half(self, ref, s):
        if self.kind == "col":
            return ref.at[:, _ds(s, self.cols // N_CHIP)]
        return ref.at[_ds(s, self.rows // N_CHIP), :]


def _place_shard(meta, w, s_arr):
    r, cols = w.shape
    tr = _tile(r, 256, 16)
    nr = r // tr

    def body(s_ref, w_ref, o_ref):
        o_ref[...] = w_ref[...].astype(BF16)

    if meta.kind == "col":
        o_map = lambda i, s_ref: (i, s_ref[0])
    else:
        o_map = lambda i, s_ref: (i + s_ref[0] * nr, 0)
    return pl.pallas_call(
        body, name="place_shard",
        grid_spec=pltpu.PrefetchScalarGridSpec(
            num_scalar_prefetch=1, grid=(nr,),
            in_specs=[pl.BlockSpec((tr, cols), lambda i, s_ref: (i, 0))],
            out_specs=pl.BlockSpec((tr, cols), o_map)),
        out_shape=jax.ShapeDtypeStruct((meta.rows, meta.cols), BF16),
        compiler_params=_params(("parallel",)),
    )(s_arr, w)


def _gather_weights(metas, fulls):
    nt = len(metas)

    def plan(in_refs, out_refs, me):
        x, y, c = me
        s_me = 2 * x + y
        ici, d2d = [], []
        for meta, full in zip(metas, out_refs):
            for mask, bits in CHIP_FLIPS:
                s_peer = jnp.bitwise_xor(s_me, bits)
                ici.append((mask, meta.piece_of_full(full, s_me, c), meta.piece_of_full(full, s_me, c),
                            meta.piece_of_full(full, s_peer, c)))
                d2d.append((FLIP_C, meta.piece_of_full(full, s_peer, c), meta.piece_of_full(full, s_peer, c),
                            meta.piece_of_full(full, s_peer, 1 - c)))
        return [], [ici, d2d]

    outs = [jax.ShapeDtypeStruct((m.rows, m.cols), BF16) for m in metas]
    return _comm("gather_weights", list(fulls), outs, plan, 6 * nt, 0, aliases={i: i for i in range(nt)})


def _reduce_pair(metas, grads):
    def plan(in_refs, out_refs, me):
        c = me[2]
        phase = [(FLIP_C, m.half_of_full(g, 1 - c), land, land) for m, g, land in zip(metas, in_refs, out_refs)]
        return [], [phase]

    outs = [jax.ShapeDtypeStruct(m.half_shape, BF16) for m in metas]
    return _comm("reduce_pair", list(grads), outs, plan, len(metas), 0)


def _reduce_chips(metas, halves):
    def plan(in_refs, out_refs, me):
        x, y, _ = me
        s_me = 2 * x + y
        phase = []
        for m, p, land in zip(metas, in_refs, out_refs):
            for kk, (mask, bits) in enumerate(CHIP_FLIPS):
                s_peer = jnp.bitwise_xor(s_me, bits)
                phase.append((mask, m.shard_of_half(p, s_peer), land.at[kk], land.at[kk]))
        return [], [phase]

    outs = [jax.ShapeDtypeStruct((3,) + m.piece_shape, BF16) for m in metas]
    return _comm("reduce_chips", list(halves), outs, plan, 3 * len(metas), 0)


def _share_halves(metas, shards):
    def plan(in_refs, out_refs, me):
        c = me[2]
        phase = [(FLIP_C, m.half_of_shard(g, c), m.half_of_shard(g, c), m.half_of_shard(g, 1 - c))
                 for m, g in zip(metas, out_refs)]
        return [], [phase]

    outs = [jax.ShapeDtypeStruct(m.shard_shape, F32) for m in metas]
    return _comm("share_halves", list(shards), outs, plan, len(metas), 0,
                 aliases={i: i for i in range(len(metas))})


def _pair_sum(meta, grad, landed, c_arr):
    hr, hc = meta.half_shape
    tr = _tile(hr, 256, 16)
    tc = _tile(hc, 2048)
    nr, ncol = hr // tr, hc // tc

    def body(c_ref, g_ref, l_ref, o_ref):
        o_ref[...] = (g_ref[...].astype(F32) + l_ref[...].astype(F32)).astype(BF16)

    if meta.kind == "col":
        g_map = lambda i, j, c_ref: (i + c_ref[0] * nr, j)
    else:
        g_map = lambda i, j, c_ref: (i, j + c_ref[0] * ncol)
    blk = (tr, tc)
    return pl.pallas_call(
        body, name="pair_sum",
        grid_spec=pltpu.PrefetchScalarGridSpec(
            num_scalar_prefetch=1, grid=(nr, ncol),
            in_specs=[pl.BlockSpec(blk, g_map), pl.BlockSpec(blk, lambda i, j, c_ref: (i, j))],
            out_specs=pl.BlockSpec(blk, lambda i, j, c_ref: (i, j))),
        out_shape=jax.ShapeDtypeStruct((hr, hc), BF16),
        compiler_params=_params(("parallel", "parallel")),
    )(c_arr, grad, landed)


def _sum_pieces(meta, half, landed, s_arr, c_arr):
    pr, pc = meta.piece_shape
    tr = _tile(pr, 256, 16)
    tc = _tile(pc, 2048)
    nr, ncol = pr // tr, pc // tc

    def body(s_ref, c_ref, p_ref, l_ref, o_ref):
        acc = p_ref[...].astype(F32)
        for kk in range(3):
            acc = acc + l_ref[kk].astype(F32)
        o_ref[...] = acc

    if meta.kind == "col":
        p_map = lambda i, j, s_ref, c_ref: (i, j + s_ref[0] * ncol)
        o_map = lambda i, j, s_ref, c_ref: (i + c_ref[0] * nr, j)
    else:
        p_map = lambda i, j, s_ref, c_ref: (i + s_ref[0] * nr, j)
        o_map = lambda i, j, s_ref, c_ref: (i, j + c_ref[0] * ncol)
    blk = (tr, tc)
    return pl.pallas_call(
        body, name="sum_pieces",
        grid_spec=pltpu.PrefetchScalarGridSpec(
            num_scalar_prefetch=2, grid=(nr, ncol),
            in_specs=[pl.BlockSpec(blk, p_map), pl.BlockSpec((3,) + blk, lambda i, j, s_ref, c_ref: (0, i, j))],
            out_specs=pl.BlockSpec(blk, o_map)),
        out_shape=jax.ShapeDtypeStruct(meta.shard_shape, F32),
        compiler_params=_params(("parallel", "parallel")),
    )(s_arr, c_arr, half, landed)


def _adam_rows(rows, sel, fulls):
    w, g, m, v = rows
    m2 = ADAM_B1 * m + (1.0 - ADAM_B1) * g
    v2 = ADAM_B2 * v + (1.0 - ADAM_B2) * jnp.square(g)
    m_hat = m2 / (1.0 - ADAM_B1 ** ADAM_STEP)
    v_hat = v2 / (1.0 - ADAM_B2 ** ADAM_STEP)
    delta = -ADAM_LR * (m_hat / (jnp.sqrt(v_hat) + ADAM_EPS) + ADAM_WD * w)
    return [delta, m2, v2], []


def _adamw(w, g, m, v):
    r, c = w.shape
    tr = _tile(r, 128, 8)
    outs = _rowwise("adamw", _adam_rows, n_tiles=r // tr, tr=tr,
                    row_ins=[(w, 0, None), (g, 0, None), (m, 0, None), (v, 0, None)],
                    row_outs=[(r, c, F32, 0)] * 3)
    return outs[0], outs[1], outs[2]


def _ada_fwd(cg, w, b):
    d, n = w.shape
    tn = _tile(n, 512)

    def body(c_ref, w_ref, b_ref, o_ref):
        a = _silu(c_ref[...]).astype(BF16)
        o_ref[...] = jnp.dot(a, w_ref[...].astype(BF16), preferred_element_type=F32) + b_ref[...]

    return pl.pallas_call(
        body, name="ada_fwd", grid=(n // tn,),
        in_specs=[pl.BlockSpec(cg.shape, lambda j: (0, 0)), pl.BlockSpec((d, tn), lambda j: (0, j)),
                  pl.BlockSpec((1, tn), lambda j: (0, j))],
        out_specs=pl.BlockSpec((cg.shape[0], tn), lambda j: (0, j)),
        out_shape=jax.ShapeDtypeStruct((cg.shape[0], n), F32),
        compiler_params=_params(("parallel",)),
    )(cg, w, b)


def _ada_bwd(cg, dm, w):
    d, n = w.shape
    tn = _tile(n, 512)
    nj = n // tn

    def body(c_ref, dm_ref, w_ref, gw_ref, da_ref, acc):
        j = pl.program_id(0)

        @pl.when(j == 0)
        def _():
            acc[...] = jnp.zeros_like(acc)

        a = _silu(c_ref[...]).astype(BF16)
        dmv = dm_ref[...].astype(BF16)
        gw_ref[...] = lax.dot_general(a, dmv, TN_DIMS, preferred_element_type=F32)
        acc[...] += lax.dot_general(dmv, w_ref[...].astype(BF16), NT_DIMS, preferred_element_type=F32)

        @pl.when(j == nj - 1)
        def _():
            da_ref[...] = acc[...]

    return pl.pallas_call(
        body, name="ada_bwd", grid=(nj,),
        in_specs=[pl.BlockSpec(cg.shape, lambda j: (0, 0)), pl.BlockSpec((dm.shape[0], tn), lambda j: (0, j)),
                  pl.BlockSpec((d, tn), lambda j: (0, j))],
        out_specs=[pl.BlockSpec((d, tn), lambda j: (0, j)), pl.BlockSpec(cg.shape, lambda j: (0, 0))],
        out_shape=[jax.ShapeDtypeStruct((d, n), F32), jax.ShapeDtypeStruct(cg.shape, F32)],
        scratch_shapes=[pltpu.VMEM(cg.shape, F32)],
        compiler_params=_params(("arbitrary",)),
    )(cg, dm, w)


def _small_reduce(gathered, logits, n_mod_cols, lg_off, loss_off, loss_cols):
    npk = gathered.shape[1]

    def body(g_ref, lo_ref, tot_ref, gb_ref, gl_ref, loss_ref):
        acc = g_ref[0:1, :]
        for dd in range(1, N_DEV):
            acc = acc + g_ref[dd:dd + 1, :]
        tot_ref[...] = acc
        gb_ref[...] = acc[:, :n_mod_cols] + acc[:, n_mod_cols:2 * n_mod_cols]
        gl_ref[...] = acc[:, lg_off:lg_off + LANES_V7X] * _sigmoid(-lo_ref[...])
        loss = jnp.sum(acc[:, loss_off:loss_off + loss_cols], axis=1, keepdims=True)
        loss_ref[...] = jnp.broadcast_to(loss, loss_ref.shape)

    lane = jax.ShapeDtypeStruct((1, LANES_V7X), F32)
    return pl.pallas_call(
        body, name="small_reduce",
        out_shape=[jax.ShapeDtypeStruct((1, npk), F32), jax.ShapeDtypeStruct((1, n_mod_cols), F32), lane, lane],
    )(gathered, logits)


def _c_ctx_grad(parts, c_ctx):
    def body(p_ref, c_ref, o_ref):
        tot = p_ref[0:1, :] + p_ref[2:3, :] + p_ref[4:5, :] + p_ref[6:7, :]
        _, vjp = jax.vjp(_silu, c_ref[...])
        o_ref[...] = vjp(tot)[0]

    return pl.pallas_call(body, name="c_ctx_grad", out_shape=jax.ShapeDtypeStruct(c_ctx.shape, F32))(parts, c_ctx)


def _rope_tables(seq, ctx_rows):
    rows = seq // GRID_W
    row = jnp.repeat(jnp.arange(rows, dtype=F32), GRID_W)
    col = jnp.tile(jnp.arange(GRID_W, dtype=F32), rows)
    half = HEAD_DIM // 2
    inv_freq = ROPE_THETA ** (-jnp.arange(0, half, 2, dtype=F32) / half)
    ang = jnp.concatenate([row[:, None] * inv_freq, col[:, None] * inv_freq], axis=-1)
    cos, sin = jnp.cos(ang), jnp.sin(ang)
    cos_full = jnp.repeat(cos, 2, axis=1)
    sin_signed = jnp.stack([-sin, sin], axis=-1).reshape(seq, HEAD_DIM)
    cos_full = jnp.concatenate([jnp.ones((ctx_rows, HEAD_DIM), F32), cos_full], axis=0)
    sin_signed = jnp.concatenate([jnp.zeros((ctx_rows, HEAD_DIM), F32), sin_signed], axis=0)
    return cos_full, sin_signed


def _qk_rot(p, gain, cos_full, sin_signed):
    r = _rmsn(p) * gain
    return r * cos_full + _swap_pairs(r) * sin_signed


def _qk_rot_bwd(g, p, gain, cos_full, sin_signed):
    g1 = g * cos_full + _swap_pairs(g * sin_signed)
    _, vjp = jax.vjp(lambda pp, gn: _rmsn(pp) * gn, p, gain)
    return vjp(g1)


def kernel(x, c, ctx, c_ctx, w_ada, b_ada, ffn1_w_in, ffn1_w_out, mix_w_in, attn_q_gain, attn_k_gain, ret_decay_logit, w_proj_attn, w_proj_ret, mix_w_out, ffn2_w_in, ffn2_w_out, final_norm, loss_target, m_c_ctx, m_w_ada, m_b_ada, m_ffn1_w_in, m_ffn1_w_out, m_mix_w_in, m_attn_q_gain, m_attn_k_gain, m_ret_decay_logit, m_w_proj_attn, m_w_proj_ret, m_mix_w_out, m_ffn2_w_in, m_ffn2_w_out, m_final_norm, v_c_ctx, v_w_ada, v_b_ada, v_ffn1_w_in, v_ffn1_w_out, v_mix_w_in, v_attn_q_gain, v_attn_k_gain, v_ret_decay_logit, v_w_proj_attn, v_w_proj_ret, v_mix_w_out, v_ffn2_w_in, v_ffn2_w_out, v_final_norm):
    xi, yi, ci = lax.axis_index("x"), lax.axis_index("y"), lax.axis_index("c")
    dev = 4 * xi + 2 * yi + ci
    s_me = 2 * xi + yi
    c_arr = jnp.reshape(ci, (1,)).astype(jnp.int32)
    s_arr = jnp.reshape(s_me, (1,)).astype(jnp.int32)

    t, d = x.shape[1], x.shape[2]
    tc = ctx.shape[1]
    tk = tc + t
    ff = ffn1_w_out.shape[1] * N_CHIP
    aw = w_proj_attn.shape[1]
    rw = w_proj_ret.shape[1]
    pw = mix_w_in.shape[2] * N_CHIP
    kvw = (pw - aw - 4 * rw - 2 * d) // 2
    groups = aw // kvw
    n_ret_heads = rw // HEAD_DIM
    mod_cols = N_MOD * d
    tr = _tile(tc, 256, 32)
    nt_all, nt_x, ctx_tiles = tk // tr, t // tr, tc // tr

    c_rows = _gather_row("gather_c", c)
    cg = jnp.concatenate([c_rows, c_ctx[None, :], jnp.zeros((7, d), F32)], axis=0)
    w_ada_l = w_ada[0]
    ada_cols = w_ada_l.shape[1]
    b_ada_l = lax.dynamic_slice_in_dim(b_ada, s_me * ada_cols, ada_cols, axis=1)
    mod_shard = _ada_fwd(cg, w_ada_l, b_ada_l)
    mod_g = _all_gather8("gather_mod", mod_shard)
    mod_full = jnp.concatenate([mod_g[0], mod_g[2], mod_g[4], mod_g[6]], axis=1)
    mod_x = lax.dynamic_slice_in_dim(mod_full, dev, 1, axis=0).reshape(N_MOD, d)
    mod_c = mod_full[8].reshape(N_MOD, d)
    mods = jnp.stack([mod_c, mod_x])

    big = [("col", ffn1_w_in), ("row", ffn1_w_out), ("col", mix_w_in), ("col", w_proj_attn), ("col", w_proj_ret),
           ("row", mix_w_out), ("col", ffn2_w_in), ("row", ffn2_w_out)]
    metas = []
    for kind, w in big:
        r_l, c_l = w.shape[1:]
        metas.append(_Sharded(kind, r_l, c_l * N_CHIP) if kind == "col" else _Sharded(kind, r_l * N_CHIP, c_l))
    placed = [_place_shard(m, w[0], s_arr) for m, (_, w) in zip(metas, big)]
    w1i, w1o, wmi, wpa, wpr, wmo, w2i, w2o = _gather_weights(metas, placed)

    cos_full, sin_signed = _rope_tables(t, tc)
    q_gain, k_gain = attn_q_gain, attn_k_gain
    log_gamma = jax.nn.log_sigmoid(ret_decay_logit[0])
    lgv = jnp.broadcast_to(log_gamma[:, :, None, None], (2, n_ret_heads, 1, HEAD_DIM))

    def norm_mod(name, h, n_tiles, off, i_shift, i_scale):
        def fn(rows, sel, fulls):
            return [_rmsn(rows[0]) * (1.0 + sel(i_scale)) + sel(i_shift)], []
        return _rowwise(name, fn, n_tiles=n_tiles, tr=tr, row_ins=[(h, 0, None)],
                        row_outs=[(h.shape[0], d, BF16, 0)], sel_in=mods, sel_off=off, ctx_rows=tc)[0]

    def resid(name, h, h_off, f, n_tiles, off, i_gate, coef):
        def fn(rows, sel, fulls):
            return [rows[0] + coef * sel(i_gate) * rows[1]], []
        return _rowwise(name, fn, n_tiles=n_tiles, tr=tr, row_ins=[(h, h_off, None), (f, 0, None)],
                        row_outs=[(f.shape[0], d, F32, 0)], sel_in=mods, sel_off=off, ctx_rows=tc)[0]

    h0 = jnp.concatenate([ctx[0], x[0]], axis=0)
    n1 = norm_mod("norm_mod1", h0, nt_all, 0, 0, 1)
    hm1, ua1, ub1 = _mm_swiglu("ffn1_in", n1, w1i)
    f1 = _mm("ffn1_out", hm1, w1o, "nn", F32)
    h1 = resid("resid1", h0, 0, f1, nt_all, 0, 2, 0.5)

    n2 = norm_mod("norm_mod2", h1, nt_all, 0, 3, 4)
    p_q = _mm("mix_in_q", n2, wmi, "nn", F32, 0, aw)
    p_kv = _mm("mix_in_kv", n2, wmi, "nn", F32, aw, 2 * kvw)
    p_r = _mm("mix_in_ret", n2, wmi, "nn", F32, aw + 2 * kvw, 3 * rw)
    p_gr = _mm("mix_in_gr", n2, wmi, "nn", F32, aw + 2 * kvw + 3 * rw, rw)
    p_gab = _mm("mix_in_gab", n2, wmi, "nn", F32, aw + 2 * kvw + 4 * rw, 2 * d)

    def q_prep(rows, sel, fulls):
        p, cf, ss = rows
        return _heads_map(lambda ph: [_qk_rot(ph, fulls[0], cf, ss) * QSCALE], [p], aw), []

    q_rot = _rowwise("q_prep", q_prep, n_tiles=nt_x, tr=tr,
                     row_ins=[(p_q, ctx_tiles, None), (cos_full, ctx_tiles, None), (sin_signed, ctx_tiles, None)],
                     row_outs=[(t, aw, BF16, 0)], full_ins=[q_gain])[0]

    def kv_prep(rows, sel, fulls):
        p, cf, ss = rows
        k_rot = _heads_map(lambda ph: [_qk_rot(ph, fulls[0], cf, ss)], [p[:, :kvw]], kvw)[0]
        v_ones = _heads_map(lambda vh: [jnp.concatenate([vh, jnp.ones_like(vh)], axis=1)], [p[:, kvw:]], kvw)[0]
        return [k_rot, v_ones], []

    k_rot, v_att = _rowwise("kv_prep", kv_prep, n_tiles=nt_all, tr=tr,
                            row_ins=[(p_kv, 0, None), (cos_full, 0, None), (sin_signed, 0, None)],
                            row_outs=[(tk, kvw, BF16, 0), (tk, 2 * kvw, BF16, 0)], full_ins=[k_gain])

    ya, lse = _flash_fwd(q_rot, k_rot, v_att, groups)
    y_ret, states = _ret_fwd(p_r, lgv, tc)

    def ret_out_fn(yf, yb, gr):
        return [_silu(gr) * _rmsn(yf + yb)]

    def ret_out(rows, sel, fulls):
        return _heads_map(ret_out_fn, rows, rw), []

    y_rows = [(y_ret, ctx_tiles, None), (y_ret, nt_all + ctx_tiles, None), (p_gr, ctx_tiles, None)]
    yr = _rowwise("ret_out", ret_out, n_tiles=nt_x, tr=tr, row_ins=y_rows, row_outs=[(t, rw, BF16, 0)])[0]

    pa = _mm("proj_attn", ya, wpa, "nn", F32)
    prj = _mm("proj_ret", yr, wpr, "nn", F32)

    def merge_fn(a, r, ga, gb):
        return _sigmoid(ga) * a + _sigmoid(gb) * r

    gate_rows = [(p_gab, ctx_tiles, (d, 0)), (p_gab, ctx_tiles, (d, 1))]
    z = _rowwise("merge", lambda rows, sel, fulls: ([merge_fn(*rows)], []), n_tiles=nt_x, tr=tr,
                 row_ins=[(pa, 0, None), (prj, 0, None)] + gate_rows, row_outs=[(t, d, BF16, 0)])[0]
    fo = _mm("mix_out", z, wmo, "nn", F32)
    h2 = resid("resid2", h1, ctx_tiles, fo, nt_x, ctx_tiles, 5, 1.0)

    n3 = norm_mod("norm_mod3", h2, nt_x, ctx_tiles, 6, 7)
    hm2, ua2, ub2 = _mm_swiglu("ffn2_in", n3, w2i)
    f2 = _mm("ffn2_out", hm2, w2o, "nn", F32)
    h3 = resid("resid3", h2, 0, f2, nt_x, ctx_tiles, 8, 0.5)

    def loss_fn(rows, sel, fulls):
        h, tgt = rows
        y, vjp = jax.vjp(lambda hh, ww: _rmsn(hh) * ww, h, fulls[0])
        err = y - tgt
        dh, dw = vjp(err / d)
        return [dh], [0.5 / d * jnp.sum(err * err, axis=0, keepdims=True), dw]

    dh3, loss_acc = _rowwise("loss_head", loss_fn, n_tiles=nt_x, tr=tr,
                             row_ins=[(h3, 0, None), (loss_target[0], 0, None)], row_outs=[(t, d, F32, 0)],
                             full_ins=[final_norm[None, :]], acc_shape=(8, d), sel_off=ctx_tiles, ctx_rows=tc)
    loss_cols, g_final = loss_acc[1, 0:1], loss_acc[1, 1:2]

    def gate_bwd(name, dh, f, n_tiles, off, i_gate, coef):
        def fn(rows, sel, fulls):
            dhh, fv = rows
            return [coef * sel(i_gate) * dhh], [jnp.sum(coef * dhh * fv, axis=0, keepdims=True)]
        return _rowwise(name, fn, n_tiles=n_tiles, tr=tr, row_ins=[(dh, 0, None), (f, 0, None)],
                        row_outs=[(dh.shape[0], d, BF16, 0)], sel_in=mods, sel_off=off, ctx_rows=tc,
                        acc_shape=(8, d))

    def swiglu_bwd(name, dhm, ua, ub):
        rows_n = dhm.shape[0]
        tr_w = _tile(tr, 128, 32)

        def fn(rows, sel, fulls):
            g, a, b = rows
            _, vjp = jax.vjp(lambda aa, bb: _silu(aa) * bb, a.astype(F32), b.astype(F32))
            da, db = vjp(g)
            return [jnp.concatenate([da, db], axis=1)], []
        return _rowwise(name, fn, n_tiles=rows_n // tr_w, tr=tr_w,
                        row_ins=[(dhm, 0, None), (ua, 0, None), (ub, 0, None)],
                        row_outs=[(rows_n, 2 * ff, BF16, 0)])[0]

    def norm_mod_bwd(name, dn, h, dres, n_tiles, off, i_shift, i_scale):
        def fn(rows, sel, fulls):
            g, hh, dr = rows
            _, vjp = jax.vjp(lambda a, sh, sc: _rmsn(a) * (1.0 + sc) + sh, hh,
                             sel(i_shift), sel(i_scale))
            dh, dsh, dsc = vjp(g)
            return [dr + dh], [dsh, dsc]
        return _rowwise(name, fn, n_tiles=n_tiles, tr=tr, row_ins=[(dn, 0, None), (h, 0, None), (dres, 0, None)],
                        row_outs=[(dn.shape[0], d, F32, 0)], sel_in=mods, sel_off=off, ctx_rows=tc,
                        acc_shape=(8, d))

    df2, acc_g3 = gate_bwd("gate_bwd3", dh3, f2, nt_x, ctx_tiles, 8, 0.5)
    g_w2o = _mm("ffn2_out_dw", hm2, df2, "tn", BF16)
    dhm2 = _mm("ffn2_out_dx", df2, w2o, "nt", F32)
    du2 = swiglu_bwd("swiglu_bwd2", dhm2, ua2, ub2)
    g_w2i = _mm("ffn2_in_dw", n3, du2, "tn", BF16)
    dn3 = _mm("ffn2_in_dx", du2, w2i, "nt", F32)
    dh2, acc_n3 = norm_mod_bwd("norm_mod_bwd3", dn3, h2, dh3, nt_x, ctx_tiles, 6, 7)

    dfo, acc_g2 = gate_bwd("gate_bwd2", dh2, fo, nt_x, ctx_tiles, 5, 1.0)
    g_wmo = _mm("mix_out_dw", z, dfo, "tn", BF16)
    dz = _mm("mix_out_dx", dfo, wmo, "nt", F32)

    def merge_bwd(rows, sel, fulls):
        g, a, r, ga, gb = rows
        _, vjp = jax.vjp(merge_fn, a, r, ga, gb)
        da, dr, dga, dgb = vjp(g)
        return [da, dr, jnp.concatenate([dga, dgb], axis=1)], []

    dpa, dpr, dgab = _rowwise("merge_bwd", merge_bwd, n_tiles=nt_x, tr=tr,
                              row_ins=[(dz, 0, None), (pa, 0, None), (prj, 0, None)] + gate_rows,
                              row_outs=[(t, d, BF16, 0), (t, d, BF16, 0), (t, 2 * d, BF16, 0)])
    g_wpa = _mm("proj_attn_dw", ya, dpa, "tn", BF16)
    dya = _mm("proj_attn_dx", dpa, wpa, "nt", BF16)
    g_wpr = _mm("proj_ret_dw", yr, dpr, "tn", BF16)
    dyr = _mm("proj_ret_dx", dpr, wpr, "nt", F32)

    def ret_out_bwd(rows, sel, fulls):
        def per_head(g, yf, yb, gr):
            _, vjp = jax.vjp(lambda yy, gg: ret_out_fn(yy, 0.0, gg)[0], yf + yb, gr)
            return list(vjp(g))
        dy, dgr = _heads_map(per_head, rows, rw)
        return [dy, dgr], []

    dy_ret, dgr = _rowwise("ret_out_bwd", ret_out_bwd, n_tiles=nt_x, tr=tr, row_ins=[(dyr, 0, None)] + y_rows,
                           row_outs=[(t, rw, F32, 0), (t, rw, BF16, 0)])
    dy_all = jnp.concatenate([jnp.zeros((tc, rw), F32), dy_ret], axis=0)
    dq_r, dk_r, dv_r, dlg = _ret_bwd(p_r, states, dy_all, lgv, tc)

    def ret_sum(rows, sel, fulls):
        return [jnp.concatenate([rows[0] + rows[1], rows[2] + rows[3], rows[4] + rows[5]], axis=1)], []

    dp_r = _rowwise("ret_bwd_sum", ret_sum, n_tiles=nt_all, tr=tr,
                    row_ins=[(dq_r, 0, None), (dq_r, nt_all, None), (dk_r, 0, None), (dk_r, nt_all, None),
                             (dv_r, 0, None), (dv_r, nt_all, None)],
                    row_outs=[(tk, 3 * rw, BF16, 0)])[0]

    dq_rot = _flash_dq(q_rot, k_rot, v_att, ya, dya, lse, groups)
    dk_rot, dv_att = _flash_dkv(q_rot, k_rot, v_att, ya, dya, lse, groups)

    def q_prep_bwd(rows, sel, fulls):
        g, p, cf, ss = rows
        gain_acc = []

        def per_head(gh, ph):
            dp, dgain = _qk_rot_bwd(gh * HEAD_DIM ** -0.5, ph, fulls[0], cf, ss)
            gain_acc.append(dgain)
            return [dp]
        dp = _heads_map(per_head, [g, p], aw)[0]
        return [dp], [functools.reduce(lambda a, b: a + b, gain_acc)]

    dp_q, acc_gq = _rowwise("q_prep_bwd", q_prep_bwd, n_tiles=nt_x, tr=tr,
                            row_ins=[(dq_rot, 0, None), (p_q, ctx_tiles, None), (cos_full, ctx_tiles, None),
                                     (sin_signed, ctx_tiles, None)],
                            row_outs=[(t, aw, BF16, 0)], full_ins=[q_gain], acc_shape=(8, HEAD_DIM),
                            sel_off=ctx_tiles, ctx_rows=tc)

    def kv_prep_bwd(rows, sel, fulls):
        gk, gv, p, cf, ss = rows
        gain_acc = []

        def per_head(gh, ph):
            dp, dgain = _qk_rot_bwd(gh, ph, fulls[0], cf, ss)
            gain_acc.append(dgain)
            return [dp]
        dpk = _heads_map(per_head, [gk, p], kvw)[0]
        return [jnp.concatenate([dpk, gv], axis=1)], [functools.reduce(lambda a, b: a + b, gain_acc)]

    dp_kv, acc_gk = _rowwise("kv_prep_bwd", kv_prep_bwd, n_tiles=nt_all, tr=tr,
                             row_ins=[(dk_rot, 0, None), (dv_att, 0, None), (p_kv, 0, (kvw, 0)), (cos_full, 0, None),
                                      (sin_signed, 0, None)],
                             row_outs=[(tk, 2 * kvw, BF16, 0)], full_ins=[k_gain], acc_shape=(8, HEAD_DIM),
                             sel_off=0, ctx_rows=tc)

    def with_ctx_zeros(a):
        return jnp.concatenate([jnp.zeros((tc, a.shape[1]), a.dtype), a], axis=0)

    dp = jnp.concatenate([with_ctx_zeros(dp_q), dp_kv, dp_r, with_ctx_zeros(dgr), with_ctx_zeros(dgab)], axis=1)
    g_wmi = _mm("mix_in_dw", n2, dp, "tn", BF16)
    dn2 = _mm("mix_in_dx", dp, wmi, "nt", F32)
    dh1, acc_n2 = norm_mod_bwd("norm_mod_bwd2", dn2, h1, with_ctx_zeros(dh2), nt_all, 0, 3, 4)

    df1, acc_g1 = gate_bwd("gate_bwd1", dh1, f1, nt_all, 0, 2, 0.5)
    g_w1o = _mm("ffn1_out_dw", hm1, df1, "tn", BF16)
    dhm1 = _mm("ffn1_out_dx", df1, w1o, "nt", F32)
    du1 = swiglu_bwd("swiglu_bwd1", dhm1, ua1, ub1)
    g_w1i = _mm("ffn1_in_dw", n1, du1, "tn", BF16)
    dn1 = _mm("ffn1_in_dx", du1, w1i, "nt", F32)
    dh0, acc_n1 = norm_mod_bwd("norm_mod_bwd1", dn1, h0, dh1, nt_all, 0, 0, 1)
    grad_x = dh0[tc:][None]

    grads_full = [g_w1i, g_w1o, g_wmi, g_wpa, g_wpr, g_wmo, g_w2i, g_w2o]
    landed = _reduce_pair(metas, grads_full)
    halves = [_pair_sum(m, g, l, c_arr) for m, g, l in zip(metas, grads_full, landed)]
    landed3 = _reduce_chips(metas, halves)
    pieces = [_sum_pieces(m, p, l, s_arr, c_arr) for m, p, l in zip(metas, halves, landed3)]
    grads_big = _share_halves(metas, pieces)

    zero_row = jnp.zeros((1, d), F32)
    dmod_x = jnp.concatenate([acc_n1[1, 0:1], acc_n1[1, 1:2], acc_g1[1, 0:1], acc_n2[1, 0:1], acc_n2[1, 1:2],
                              acc_g2[1, 0:1], acc_n3[1, 0:1], acc_n3[1, 1:2], acc_g3[1, 0:1]], axis=1)
    dmod_c = jnp.concatenate([acc_n1[0, 0:1], acc_n1[0, 1:2], acc_g1[0, 0:1], acc_n2[0, 0:1], acc_n2[0, 1:2]]
                             + [zero_row] * 4, axis=1)
    dlg_row = jnp.pad(dlg[:, :, 0, 0].reshape(1, 2 * n_ret_heads), ((0, 0), (0, LANES_V7X - 2 * n_ret_heads)))
    packed = jnp.concatenate([dmod_x, dmod_c, acc_gq[1, 0:1], acc_gk[0, 0:1] + acc_gk[1, 0:1], dlg_row,
                              g_final, loss_cols], axis=1)
    off_gq = 2 * mod_cols
    off_gk = off_gq + LANES_V7X
    off_lg = off_gk + LANES_V7X
    off_fn = off_lg + LANES_V7X
    off_loss = off_fn + d
    gathered = _gather_row("gather_small", packed)
    logits_row = jnp.pad(ret_decay_logit.reshape(1, 2 * n_ret_heads), ((0, 0), (0, LANES_V7X - 2 * n_ret_heads)))
    totals, g_b_ada, g_decay, loss_row = _small_reduce(gathered, logits_row, mod_cols, off_lg, off_loss, d)
    loss = loss_row[0, 0]

    dm = jnp.concatenate([gathered[:, :mod_cols], totals[:, mod_cols:2 * mod_cols],
                          jnp.zeros((7, mod_cols), F32)], axis=0)
    dm_l = lax.dynamic_slice_in_dim(dm, s_me * ada_cols, ada_cols, axis=1)
    g_w_ada, da_part = _ada_bwd(cg, dm_l, w_ada_l)
    da_rows = _gather_row("gather_dc", da_part[8:9])
    g_c_ctx = _c_ctx_grad(da_rows, c_ctx[None, :])

    def as2d(a):
        return a.reshape(-1, a.shape[-1])

    grads = {
        "c_ctx": g_c_ctx, "w_ada": g_w_ada, "b_ada": g_b_ada,
        "ffn1_w_in": grads_big[0], "ffn1_w_out": grads_big[1], "mix_w_in": grads_big[2],
        "attn_q_gain": totals[:, off_gq:off_gq + HEAD_DIM], "attn_k_gain": totals[:, off_gk:off_gk + HEAD_DIM],
        "ret_decay_logit": g_decay[:, :2 * n_ret_heads],
        "w_proj_attn": grads_big[3], "w_proj_ret": grads_big[4], "mix_w_out": grads_big[5],
        "ffn2_w_in": grads_big[6], "ffn2_w_out": grads_big[7], "final_norm": totals[:, off_fn:off_fn + d],
    }
    weights = {"c_ctx": (c_ctx, m_c_ctx, v_c_ctx), "w_ada": (w_ada, m_w_ada, v_w_ada),
               "b_ada": (b_ada, m_b_ada, v_b_ada), "ffn1_w_in": (ffn1_w_in, m_ffn1_w_in, v_ffn1_w_in),
               "ffn1_w_out": (ffn1_w_out, m_ffn1_w_out, v_ffn1_w_out), "mix_w_in": (mix_w_in, m_mix_w_in, v_mix_w_in),
               "attn_q_gain": (attn_q_gain, m_attn_q_gain, v_attn_q_gain),
               "attn_k_gain": (attn_k_gain, m_attn_k_gain, v_attn_k_gain),
               "ret_decay_logit": (ret_decay_logit, m_ret_decay_logit, v_ret_decay_logit),
               "w_proj_attn": (w_proj_attn, m_w_proj_attn, v_w_proj_attn),
               "w_proj_ret": (w_proj_ret, m_w_proj_ret, v_w_proj_ret), "mix_w_out": (mix_w_out, m_mix_w_out, v_mix_w_out),
               "ffn2_w_in": (ffn2_w_in, m_ffn2_w_in, v_ffn2_w_in), "ffn2_w_out": (ffn2_w_out, m_ffn2_w_out, v_ffn2_w_out),
               "final_norm": (final_norm, m_final_norm, v_final_norm)}
    out_g, out_d, out_m, out_v = [], [], [], []
    for name, (w, m, v) in weights.items():
        shape = w.shape
        if name == "ret_decay_logit":
            w2, m2, v2 = (a.reshape(1, -1) for a in (w, m, v))
        else:
            w2, m2, v2 = as2d(w), as2d(m), as2d(v)
        g2 = grads[name].reshape(w2.shape)
        delta, new_m, new_v = _adamw(w2, g2, m2, v2)
        out_g.append(g2.reshape(shape))
        out_d.append(delta.reshape(shape))
        out_m.append(new_m.reshape(shape))
        out_v.append(new_v.reshape(shape))
    return (loss, grad_x, *out_g, *out_d, *out_m, *out_v)
```

```python
import functools
import math

import jax
import jax.numpy as jnp
from jax import lax
from jax.experimental import pallas as pl
from jax.experimental.pallas import tpu as pltpu

F32 = jnp.float32
BF16 = jnp.bfloat16

HEAD_DIM = 128
GRID_W = 64
ROPE_THETA = 10000.0
NORM_EPS = 1e-6
N_MOD = 9
RET_CHUNK = 128
ADAM_LR = 0.001
ADAM_B1 = 0.9
ADAM_B2 = 0.999
ADAM_EPS = 1e-08
ADAM_WD = 0.01
ADAM_STEP = 10

N_DEV = 8
N_CHIP = 4
LANES_V7X = 128
VMEM_LIMIT_V7X = 52 * 1024 * 1024

NT_DIMS = (((1,), (1,)), ((), ()))
TN_DIMS = (((0,), (0,)), ((), ()))
NN_DIMS = (((1,), (0,)), ((), ()))


def _tile(n, pref, mult=LANES_V7X):
    if n <= pref:
        return n
    t = (pref // mult) * mult
    while t >= mult:
        if n % t == 0:
            return t
        t -= mult
    return n


def _params(sem):
    return pltpu.CompilerParams(dimension_semantics=sem, vmem_limit_bytes=VMEM_LIMIT_V7X)


def _sigmoid(x):
    return 1.0 / (1.0 + jnp.exp(-x))


def _silu(x):
    return x * _sigmoid(x)


def _rmsn(x):
    return x * lax.rsqrt(jnp.mean(x * x, axis=-1, keepdims=True) + NORM_EPS)


MM_VMEM_BUDGET = 36 * 1024 * 1024


def _divisor_tiles(n, cap):
    ts = [t for t in range(LANES_V7X, min(n, cap) + 1, LANES_V7X) if n % t == 0]
    return ts or [n]


def _mm_tiles(m, n, tk, out_bytes, has_acc):
    best = None
    for tm in _divisor_tiles(m, 1536):
        for tn in _divisor_tiles(n, 2560):
            need = 4 * tk * (tm + tn) + 2 * tm * tn * out_bytes + 4 * tm * tn
            if need > MM_VMEM_BUDGET:
                continue
            score = tm * tn / (tm + tn)
            if best is None or score > best[0]:
                best = (score, tm, tn)
    return best[1], best[2]


def _mm(name, a, b, mode, out_dtype, b_off=0, n=None):
    if mode == "nn":
        m, k = a.shape
        n = b.shape[1] if n is None else n
        dims = NN_DIMS
    elif mode == "nt":
        m, k = a.shape
        n = b.shape[0]
        dims = NT_DIMS
    else:
        k, m = a.shape
        n = b.shape[1]
        dims = TN_DIMS
    tk = _tile(k, 2560) if mode != "tn" else _tile(k, 1024)
    nk = k // tk
    tm, tn = _mm_tiles(m, math.gcd(n, b_off) if b_off else n, tk, jnp.dtype(out_dtype).itemsize, nk > 1)
    joff = b_off // tn

    def body(a_ref, b_ref, o_ref, *acc):
        prod = lax.dot_general(a_ref[...], b_ref[...], dims, preferred_element_type=F32)
        if nk == 1:
            o_ref[...] = prod.astype(o_ref.dtype)
            return
        acc_ref, = acc
        kk = pl.program_id(2)

        @pl.when(kk == 0)
        def _():
            acc_ref[...] = prod

        @pl.when(kk > 0)
        def _():
            acc_ref[...] += prod

        @pl.when(kk == nk - 1)
        def _():
            o_ref[...] = acc_ref[...].astype(o_ref.dtype)

    if mode == "nn":
        a_spec = pl.BlockSpec((tm, tk), lambda i, j, kk: (i, kk))
        b_spec = pl.BlockSpec((tk, tn), lambda i, j, kk: (kk, j + joff))
    elif mode == "nt":
        a_spec = pl.BlockSpec((tm, tk), lambda i, j, kk: (i, kk))
        b_spec = pl.BlockSpec((tn, tk), lambda i, j, kk: (j, kk))
    else:
        a_spec = pl.BlockSpec((tk, tm), lambda i, j, kk: (kk, i))
        b_spec = pl.BlockSpec((tk, tn), lambda i, j, kk: (kk, j))
    return pl.pallas_call(
        body, name=name, grid=(m // tm, n // tn, nk),
        in_specs=[a_spec, b_spec],
        out_specs=pl.BlockSpec((tm, tn), lambda i, j, kk: (i, j)),
        out_shape=jax.ShapeDtypeStruct((m, n), out_dtype),
        scratch_shapes=[pltpu.VMEM((tm, tn), F32)] if nk > 1 else [],
        compiler_params=_params(("parallel", "parallel", "arbitrary")),
    )(a, b)


def _mm_swiglu(name, a, w):
    m, k = a.shape
    f = w.shape[1] // 2
    tm = _tile(m, 1024)
    tn = _tile(f, 512)
    tk = _tile(k, 2560)
    nk = k // tk
    jf = f // tn

    def body(a_ref, wa_ref, wb_ref, h_ref, ua_ref, ub_ref, acca, accb):
        kk = pl.program_id(2)

        @pl.when(kk == 0)
        def _():
            acca[...] = jnp.zeros_like(acca)
            accb[...] = jnp.zeros_like(accb)

        av = a_ref[...]
        acca[...] += jnp.dot(av, wa_ref[...], preferred_element_type=F32)
        accb[...] += jnp.dot(av, wb_ref[...], preferred_element_type=F32)

        @pl.when(kk == nk - 1)
        def _():
            ua = acca[...]
            ub = accb[...]
            h_ref[...] = (_silu(ua) * ub).astype(BF16)
            ua_ref[...] = ua.astype(BF16)
            ub_ref[...] = ub.astype(BF16)

    o_spec = pl.BlockSpec((tm, tn), lambda i, j, kk: (i, j))
    o_shape = jax.ShapeDtypeStruct((m, f), BF16)
    return pl.pallas_call(
        body, name=name, grid=(m // tm, jf, nk),
        in_specs=[pl.BlockSpec((tm, tk), lambda i, j, kk: (i, kk)),
                  pl.BlockSpec((tk, tn), lambda i, j, kk: (kk, j)),
                  pl.BlockSpec((tk, tn), lambda i, j, kk: (kk, j + jf))],
        out_specs=[o_spec, o_spec, o_spec],
        out_shape=[o_shape, o_shape, o_shape],
        scratch_shapes=[pltpu.VMEM((tm, tn), F32), pltpu.VMEM((tm, tn), F32)],
        compiler_params=_params(("parallel", "parallel", "arbitrary")),
    )(a, w, w)


def _rowwise(name, fn, *, n_tiles, tr, row_ins, row_outs, sel_in=None, sel_off=0, ctx_rows=0,
             full_ins=(), acc_shape=None):
    sr = 32 if tr % 32 == 0 else tr
    n_row, n_full, n_out = len(row_ins), len(full_ins), len(row_outs)
    has_sel = sel_in is not None
    has_acc = acc_shape is not None

    def sel_of(i):
        return jnp.where((i + sel_off) * tr < ctx_rows, 0, 1)

    def body(*refs):
        row_refs = refs[:n_row]
        pos = n_row
        sel_ref = None
        if has_sel:
            sel_ref = refs[pos]
            pos += 1
        full_refs = refs[pos:pos + n_full]
        pos += n_full
        out_refs = refs[pos:pos + n_out]
        pos += n_out
        acc_ref = refs[pos] if has_acc else None
        i = pl.program_id(0)
        if has_acc:
            first = (i == 0) | ((i + sel_off) * tr == ctx_rows)

            @pl.when(first)
            def _():
                acc_ref[...] = jnp.zeros_like(acc_ref)

        sel = (lambda kk: sel_ref[kk:kk + 1, :]) if has_sel else None
        fulls = [r[...] for r in full_refs]

        def slab(r, carry):
            rs = pl.ds(pl.multiple_of(r * sr, sr), sr)
            rows = [ref[rs, :] for ref in row_refs]
            outs, accs = fn(rows, sel, fulls)
            for o_ref, o in zip(out_refs, outs):
                o_ref[rs, :] = o.astype(o_ref.dtype)
            for kk, a in enumerate(accs):
                acc_ref[kk:kk + 1, :a.shape[1]] += a
            return carry

        lax.fori_loop(0, tr // sr, slab, 0)

    in_specs, args = [], []
    for arr, off, blk in row_ins:
        if blk is None:
            in_specs.append(pl.BlockSpec((tr, arr.shape[1]), functools.partial(lambda i, o: (i + o, 0), o=off)))
        else:
            in_specs.append(pl.BlockSpec((tr, blk[0]), functools.partial(lambda i, o, cb: (i + o, cb), o=off, cb=blk[1])))
        args.append(arr)
    if has_sel:
        in_specs.append(pl.BlockSpec((None,) + sel_in.shape[1:], lambda i: (sel_of(i), 0, 0)))
        args.append(sel_in)
    for arr in full_ins:
        in_specs.append(pl.BlockSpec(arr.shape, lambda i: (0, 0)))
        args.append(arr)
    out_specs, out_shape = [], []
    for rows, cols, dt, off in row_outs:
        out_specs.append(pl.BlockSpec((tr, cols), functools.partial(lambda i, o: (i + o, 0), o=off)))
        out_shape.append(jax.ShapeDtypeStruct((rows, cols), dt))
    if has_acc:
        out_specs.append(pl.BlockSpec((None,) + tuple(acc_shape), lambda i: (sel_of(i), 0, 0)))
        out_shape.append(jax.ShapeDtypeStruct((2,) + tuple(acc_shape), F32))
    return pl.pallas_call(
        body, name=name, grid=(n_tiles,), in_specs=in_specs, out_specs=out_specs, out_shape=out_shape,
        compiler_params=_params(("arbitrary",)),
    )(*args)


def _swap_pairs(x):
    lane = lax.broadcasted_iota(jnp.int32, x.shape, 1)
    nxt = pltpu.roll(x, x.shape[1] - 1, 1)
    prv = pltpu.roll(x, 1, 1)
    return jnp.where(lane % 2 == 0, nxt, prv)


def _heads_map(fn, arrs, width):
    outs = None
    for h in range(width // HEAD_DIM):
        sl = slice(h * HEAD_DIM, (h + 1) * HEAD_DIM)
        res = fn(*[a[:, sl] for a in arrs])
        if outs is None:
            outs = [[] for _ in res]
        for lst, r in zip(outs, res):
            lst.append(r)
    return [jnp.concatenate(lst, axis=1) if len(lst) > 1 else lst[0] for lst in outs]


QSCALE = HEAD_DIM ** -0.5 * math.log2(math.e)
LN2 = math.log(2.0)


def _lane_chunks(a):
    return [a[:, cc * LANES_V7X:(cc + 1) * LANES_V7X] for cc in range(a.shape[1] // LANES_V7X)]


def _row_bcast(col, like):
    return jnp.broadcast_to(col, like.shape)


def _flash_tiles(t, tk_all):
    return _tile(t, 256), _tile(tk_all, 1024)


def _flash_fwd(q, k, vx, groups):
    t, aw = q.shape
    tk_all, kvw = k.shape
    kvh = kvw // HEAD_DIM
    gw = groups * HEAD_DIM
    tq, tk = _flash_tiles(t, tk_all)
    nk = tk_all // tk

    def body(q_ref, k_ref, v_ref, o_ref, lse_ref, m_sc, l_sc, acc_sc):
        j = pl.program_id(2)

        @pl.when(j == 0)
        def _():
            m_sc[...] = jnp.full_like(m_sc, -jnp.inf)
            l_sc[...] = jnp.zeros_like(l_sc)
            acc_sc[...] = jnp.zeros_like(acc_sc)

        kt = k_ref[...]
        vt = v_ref[...]
        for g in range(groups):
            sl = slice(g * HEAD_DIM, (g + 1) * HEAD_DIM)
            s = _lane_chunks(lax.dot_general(q_ref[:, sl], kt, NT_DIMS, preferred_element_type=F32))
            mx = functools.reduce(jnp.maximum, s)
            m_prev = m_sc[g]
            m_new = jnp.maximum(m_prev, _row_bcast(jnp.max(mx, axis=1, keepdims=True), mx))
            p = jnp.concatenate([jnp.exp2(sc - m_new).astype(BF16) for sc in s], axis=1)
            alpha = jnp.exp2(m_prev - m_new)
            pv = jnp.dot(p, vt, preferred_element_type=F32)
            acc_sc[g] = alpha * acc_sc[g] + pv[:, :HEAD_DIM]
            l_sc[g] = alpha * l_sc[g] + pv[:, HEAD_DIM:]
            m_sc[g] = m_new

        @pl.when(j == nk - 1)
        def _():
            for g in range(groups):
                sl = slice(g * HEAD_DIM, (g + 1) * HEAD_DIM)
                o_ref[:, sl] = (acc_sc[g] / l_sc[g]).astype(o_ref.dtype)
                lse_ref[:, sl] = m_sc[g] + jnp.log2(l_sc[g])

    qs = pl.BlockSpec((tq, gw), lambda kh, i, j: (i, kh))
    sc = pltpu.VMEM((groups, tq, HEAD_DIM), F32)
    return pl.pallas_call(
        body, name="flash_fwd", grid=(kvh, t // tq, nk),
        in_specs=[qs, pl.BlockSpec((tk, HEAD_DIM), lambda kh, i, j: (j, kh)),
                  pl.BlockSpec((tk, 2 * HEAD_DIM), lambda kh, i, j: (j, kh))],
        out_specs=[qs, qs],
        out_shape=[jax.ShapeDtypeStruct((t, aw), BF16), jax.ShapeDtypeStruct((t, aw), F32)],
        scratch_shapes=[sc, sc, sc],
        compiler_params=_params(("parallel", "parallel", "arbitrary")),
    )(q, k, vx)


def _flash_p_ds(q, kt, vt, do, lse, delta):
    s = _lane_chunks(lax.dot_general(q, kt, NT_DIMS, preferred_element_type=F32))
    dp = _lane_chunks(lax.dot_general(do, vt, NT_DIMS, preferred_element_type=F32))
    p = [jnp.exp2(sc - lse) for sc in s]
    ds = jnp.concatenate([(pc * (dc - delta)).astype(BF16) for pc, dc in zip(p, dp)], axis=1)
    return jnp.concatenate([pc.astype(BF16) for pc in p], axis=1), ds


def _flash_delta(do, o):
    prod = do.astype(F32) * o.astype(F32)
    return _row_bcast(jnp.sum(prod, axis=1, keepdims=True), prod)


def _flash_bwd(q, k, vx, o, do, lse, groups):
    t, aw = q.shape
    tk_all, kvw = k.shape
    kvh = kvw // HEAD_DIM
    gw = groups * HEAD_DIM
    tq, tk = _flash_tiles(t, tk_all)
    nq, nk = t // tq, tk_all // tk

    def body(q_ref, k_ref, v_ref, o_ref, do_ref, lse_ref, dq_ref, dk_ref, dv_ref, dq_sc, dk_acc, dv_acc):
        j = pl.program_id(1)
        i = pl.program_id(2)

        @pl.when(i == 0)
        def _():
            dk_acc[...] = jnp.zeros_like(dk_acc)
            dv_acc[...] = jnp.zeros_like(dv_acc)

        @pl.when(j == 0)
        def _():
            dq_sc[i] = jnp.zeros((groups, tq, HEAD_DIM), F32)

        kt = k_ref[...]
        vt = v_ref[:, :HEAD_DIM]
        for g in range(groups):
            sl = slice(g * HEAD_DIM, (g + 1) * HEAD_DIM)
            qv = q_ref[:, sl]
            dov = do_ref[:, sl]
            p, ds = _flash_p_ds(qv, kt, vt, dov, lse_ref[:, sl], _flash_delta(dov, o_ref[:, sl]))
            dv_acc[...] += lax.dot_general(p, dov, TN_DIMS, preferred_element_type=F32)
            dk_acc[...] += lax.dot_general(ds, qv, TN_DIMS, preferred_element_type=F32)
            dq_sc[i, g] += jnp.dot(ds, kt, preferred_element_type=F32)

        @pl.when(i == nq - 1)
        def _():
            dk_ref[...] = dk_acc[...] * LN2
            dv_ref[...] = dv_acc[...]

        @pl.when(j == nk - 1)
        def _():
            for g in range(groups):
                dq_ref[:, g * HEAD_DIM:(g + 1) * HEAD_DIM] = dq_sc[i, g]

    qs = pl.BlockSpec((tq, gw), lambda kh, j, i: (i, kh))
    ks = pl.BlockSpec((tk, HEAD_DIM), lambda kh, j, i: (j, kh))
    dq_spec = pl.BlockSpec((tq, gw), lambda kh, j, i: (jnp.where(j == nk - 1, i, 0), kh))
    return pl.pallas_call(
        body, name="flash_bwd", grid=(kvh, nk, nq),
        in_specs=[qs, ks, pl.BlockSpec((tk, 2 * HEAD_DIM), lambda kh, j, i: (j, kh)), qs, qs, qs],
        out_specs=[dq_spec, ks, ks],
        out_shape=[jax.ShapeDtypeStruct((t, aw), F32), jax.ShapeDtypeStruct((tk_all, kvw), F32),
                   jax.ShapeDtypeStruct((tk_all, kvw), F32)],
        scratch_shapes=[pltpu.VMEM((nq, groups, tq, HEAD_DIM), F32), pltpu.VMEM((tk, HEAD_DIM), F32),
                        pltpu.VMEM((tk, HEAD_DIM), F32)],
        compiler_params=_params(("parallel", "arbitrary", "arbitrary")),
    )(q, k, vx, o, do, lse)


def _bf_nn(a, b):
    return jnp.dot(a.astype(BF16), b.astype(BF16), preferred_element_type=F32)


def _bf_nt(a, b):
    return lax.dot_general(a.astype(BF16), b.astype(BF16), NT_DIMS, preferred_element_type=F32)


def _bf_tn(a, b):
    return lax.dot_general(a.astype(BF16), b.astype(BF16), TN_DIMS, preferred_element_type=F32)


@jax.custom_vjp
def _d_nn(a, b):
    return _bf_nn(a, b)


@jax.custom_vjp
def _d_nt(a, b):
    return _bf_nt(a, b)


@jax.custom_vjp
def _d_tn(a, b):
    return _bf_tn(a, b)


_d_nn.defvjp(lambda a, b: (_bf_nn(a, b), (a, b)), lambda r, g: (_d_nt(g, r[1]), _d_tn(r[0], g)))
_d_nt.defvjp(lambda a, b: (_bf_nt(a, b), (a, b)), lambda r, g: (_d_nn(g, r[1]), _d_tn(g, r[0])))
_d_tn.defvjp(lambda a, b: (_bf_tn(a, b), (a, b)), lambda r, g: (_d_nt(r[1], g), _d_nn(r[0], g)))


def _ret_chunk(q, k_raw, v, state, lg, rev, dots):
    nn, nt, tn = dots
    c = RET_CHUNK
    tcol = lax.broadcasted_iota(jnp.int32, (c, 1), 0).astype(F32)
    trow = lax.broadcasted_iota(jnp.int32, (1, c), 1).astype(F32)
    ucol = jnp.where(rev, c - 1.0 - tcol, tcol)
    urow = jnp.where(rev, c - 1.0 - trow, trow)
    e = ucol - urow
    low = e >= 0
    intra = jnp.where(low, jnp.exp(jnp.where(low, e, 0.0) * lg), 0.0)
    k = k_raw * (HEAD_DIM ** -0.5)
    inner = nt(q, k) * intra
    y = nn(inner, v) + nn(q, state) * jnp.exp((ucol + 1.0) * lg)
    new_state = state * jnp.exp(c * lg) + tn(k * jnp.exp((c - 1.0 - ucol) * lg), v)
    return y, new_state


def _ret_chunk_index(n_chunks, n_ctx_chunks):
    def idx(d, s):
        if d == 0:
            return s
        return jnp.where(s < n_ctx_chunks, n_ctx_chunks - 1 - s, n_chunks - 1 - s + n_ctx_chunks)
    return idx


def _ret_fwd(pr, lgv, ctx_rows):
    tk_all = pr.shape[0]
    rw = pr.shape[1] // 3
    nh = rw // HEAD_DIM
    nc = tk_all // RET_CHUNK
    cidx = _ret_chunk_index(nc, ctx_rows // RET_CHUNK)

    def body(pf_ref, pb_ref, lg_ref, yf_ref, yb_ref, st_ref, s_sc):
        s = pl.program_id(0)

        @pl.when(s == 0)
        def _():
            s_sc[...] = jnp.zeros_like(s_sc)

        for d, (p_ref, y_ref) in enumerate(((pf_ref, yf_ref), (pb_ref, yb_ref))):
            for h in range(nh):
                cols = [slice((part * nh + h) * HEAD_DIM, (part * nh + h + 1) * HEAD_DIM) for part in range(3)]
                state = s_sc[d, h]
                st_ref[d, h] = state
                y, new_state = _ret_chunk(p_ref[:, cols[0]], p_ref[:, cols[1]], p_ref[:, cols[2]], state,
                                          lg_ref[d, h][:, :1], d == 1, (_bf_nn, _bf_nt, _bf_tn))
                y_ref[:, h * HEAD_DIM:(h + 1) * HEAD_DIM] = y
                s_sc[d, h] = new_state

    y_shape = jax.ShapeDtypeStruct((tk_all, rw), F32)
    return pl.pallas_call(
        body, name="ret_fwd", grid=(nc,),
        in_specs=[pl.BlockSpec((RET_CHUNK, 3 * rw), lambda s: (cidx(0, s), 0)),
                  pl.BlockSpec((RET_CHUNK, 3 * rw), lambda s: (cidx(1, s), 0)),
                  pl.BlockSpec(lgv.shape, lambda s: (0, 0, 0, 0))],
        out_specs=[pl.BlockSpec((RET_CHUNK, rw), lambda s: (cidx(0, s), 0)),
                   pl.BlockSpec((RET_CHUNK, rw), lambda s: (cidx(1, s), 0)),
                   pl.BlockSpec((2, nh, None, HEAD_DIM, HEAD_DIM), lambda s: (0, 0, s, 0, 0))],
        out_shape=[y_shape, y_shape, jax.ShapeDtypeStruct((2, nh, nc, HEAD_DIM, HEAD_DIM), F32)],
        scratch_shapes=[pltpu.VMEM((2, nh, HEAD_DIM, HEAD_DIM), F32)],
        compiler_params=_params(("arbitrary",)),
    )(pr, pr, lgv)


def _ret_bwd(pr, states, dy, lgv, ctx_rows):
    tk_all = pr.shape[0]
    rw = pr.shape[1] // 3
    nh = rw // HEAD_DIM
    nc = tk_all // RET_CHUNK
    cidx = _ret_chunk_index(nc, ctx_rows // RET_CHUNK)

    def body(pf_ref, pb_ref, st_ref, dyf_ref, dyb_ref, lg_ref, dpf_ref, dpb_ref, dlg_ref, ds_sc):
        sp = pl.program_id(0)

        @pl.when(sp == 0)
        def _():
            ds_sc[...] = jnp.zeros_like(ds_sc)
            dlg_ref[...] = jnp.zeros_like(dlg_ref)

        for d, (p_ref, dy_ref, dp_ref) in enumerate(((pf_ref, dyf_ref, dpf_ref), (pb_ref, dyb_ref, dpb_ref))):
            for h in range(nh):
                cols = [slice((part * nh + h) * HEAD_DIM, (part * nh + h + 1) * HEAD_DIM) for part in range(3)]

                def step(q, k, v, state, lg, rev=(d == 1)):
                    return _ret_chunk(q, k, v, state, lg, rev, (_d_nn, _d_nt, _d_tn))

                _, vjp = jax.vjp(step, p_ref[:, cols[0]], p_ref[:, cols[1]], p_ref[:, cols[2]], st_ref[d, h],
                                 lg_ref[d, h][:, :1])
                grads = vjp((dy_ref[:, h * HEAD_DIM:(h + 1) * HEAD_DIM], ds_sc[d, h]))
                for part in range(3):
                    dp_ref[:, cols[part]] = grads[part]
                ds_sc[d, h] = grads[3]
                dlg_ref[d, h] += jnp.broadcast_to(grads[4], (1, HEAD_DIM))

    def at(d):
        return lambda sp: (cidx(d, nc - 1 - sp), 0)

    dp_shape = jax.ShapeDtypeStruct((tk_all, 3 * rw), F32)
    lg_spec = pl.BlockSpec(lgv.shape, lambda sp: (0, 0, 0, 0))
    return pl.pallas_call(
        body, name="ret_bwd", grid=(nc,),
        in_specs=[pl.BlockSpec((RET_CHUNK, 3 * rw), at(0)), pl.BlockSpec((RET_CHUNK, 3 * rw), at(1)),
                  pl.BlockSpec((2, nh, None, HEAD_DIM, HEAD_DIM), lambda sp: (0, 0, nc - 1 - sp, 0, 0)),
                  pl.BlockSpec((RET_CHUNK, rw), at(0)), pl.BlockSpec((RET_CHUNK, rw), at(1)), lg_spec],
        out_specs=[pl.BlockSpec((RET_CHUNK, 3 * rw), at(0)), pl.BlockSpec((RET_CHUNK, 3 * rw), at(1)), lg_spec],
        out_shape=[dp_shape, dp_shape, jax.ShapeDtypeStruct(lgv.shape, F32)],
        scratch_shapes=[pltpu.VMEM((2, nh, HEAD_DIM, HEAD_DIM), F32)],
        compiler_params=_params(("arbitrary",)),
    )(pr, pr, states, dy, dy, lgv)


FLIP_X, FLIP_Y, FLIP_XY, FLIP_C = (1, 0, 0), (0, 1, 0), (1, 1, 0), (0, 0, 1)
CHIP_FLIPS = ((FLIP_X, 2), (FLIP_Y, 1), (FLIP_XY, 3))


def _flip(me, mask):
    return tuple(1 - v if m else v for v, m in zip(me, mask))


def _comm(name, ins, out_shapes, plan, n_remote, n_local, aliases=None):
    n_in, n_out = len(ins), len(out_shapes)

    def body(*refs):
        in_refs = refs[:n_in]
        out_refs = refs[n_in:n_in + n_out]
        send_sems, recv_sems, local_sems = refs[n_in + n_out:]
        me = (lax.axis_index("x"), lax.axis_index("y"), lax.axis_index("c"))
        local, phases = plan(in_refs, out_refs, me)
        local_copies = [pltpu.make_async_copy(s, d, local_sems.at[i]) for i, (s, d) in enumerate(local)]
        for cp in local_copies:
            cp.start()
        sent = []
        kk = 0
        for phase in phases:
            arrivals = []
            for mask, src, dst, landing in phase:
                peer = _flip(me, mask)
                cp = pltpu.make_async_remote_copy(src_ref=src, dst_ref=dst, send_sem=send_sems.at[kk],
                                                  recv_sem=recv_sems.at[kk], device_id=peer,
                                                  device_id_type=pl.DeviceIdType.MESH)
                cp.start()
                sent.append(cp)
                arrivals.append(pltpu.make_async_remote_copy(
                    src_ref=landing, dst_ref=landing, send_sem=send_sems.at[kk], recv_sem=recv_sems.at[kk],
                    device_id=peer, device_id_type=pl.DeviceIdType.MESH))
                kk += 1
            for cp in arrivals:
                cp.wait_recv()
        for cp in sent:
            cp.wait_send()
        for cp in local_copies:
            cp.wait()

    any_spec = pl.BlockSpec(memory_space=pl.ANY)
    return pl.pallas_call(
        body, name=name,
        in_specs=[any_spec] * n_in, out_specs=[any_spec] * n_out, out_shape=list(out_shapes),
        scratch_shapes=[pltpu.SemaphoreType.DMA((n_remote,)), pltpu.SemaphoreType.DMA((n_remote,)),
                        pltpu.SemaphoreType.DMA((max(n_local, 1),))],
        input_output_aliases=aliases or {},
    )(*ins)


def _ds(start, size):
    return pl.ds(pl.multiple_of(start * size, 8), size)


def _all_gather8(name, v):
    masks = [(a, b, cc) for a in (0, 1) for b in (0, 1) for cc in (0, 1)][1:]

    def index(p):
        return 4 * p[0] + 2 * p[1] + p[2]

    def plan(in_refs, out_refs, me):
        (src,), (out,) = in_refs, out_refs
        local = [(src, out.at[index(me)])]
        phase = [(m, src, out.at[index(me)], out.at[index(_flip(me, m))]) for m in masks]
        return local, [phase]

    return _comm(name, [v], [jax.ShapeDtypeStruct((N_DEV,) + v.shape, v.dtype)], plan, len(masks), 1)[0]


def _gather_row(name, row):
    n = row.shape[1]
    n_pad = -(-n // (8 * LANES_V7X)) * (8 * LANES_V7X)
    v = jnp.pad(row, ((0, 0), (0, n_pad - n))).reshape(8, n_pad // 8)
    return _all_gather8(name, v).reshape(N_DEV, n_pad)[:, :n]


class _Sharded:
    def __init__(self, kind, rows, cols):
        self.kind, self.rows, self.cols = kind, rows, cols
        self.shard_shape = (rows, cols // N_CHIP) if kind == "col" else (rows // N_CHIP, cols)
        self.half_shape = (rows // 2, cols) if kind == "col" else (rows, cols // 2)
        self.piece_shape = (rows // 2, cols // N_CHIP) if kind == "col" else (rows // N_CHIP, cols // 2)

    def shard_of_full(self, ref, s):
        if self.kind == "col":
            return ref.at[:, _ds(s, self.cols // N_CHIP)]
        return ref.at[_ds(s, self.rows // N_CHIP), :]

    def half_of_full(self, ref, h):
        if self.kind == "col":
            return ref.at[_ds(h, self.rows // 2), :]
        return ref.at[:, _ds(h, self.cols // 2)]

    def piece_of_full(self, ref, s, h):
        if self.kind == "col":
            return ref.at[_ds(h, self.rows // 2), _ds(s, self.cols // N_CHIP)]
        return ref.at[_ds(s, self.rows // N_CHIP), _ds(h, self.cols // 2)]

    def half_of_shard(self, ref, h):
        if self.kind == "col":
            return ref.at[_ds(h, self.rows // 2), :]
        return ref.at[:, _ds(h, self.cols // 2)]

    def shard_of_half(self, ref, s):
        if self.kind == "col":
            return ref.at[:, _ds(s, self.cols // N_CHIP)]
        return ref.at[_ds(s, self.rows // N_CHIP), :]


def _place_shard(meta, w, s_arr):
    r, cols = w.shape
    tr = _tile(r, 256, 16)
    nr = r // tr

    def body(s_ref, w_ref, o_ref):
        o_ref[...] = w_ref[...].astype(BF16)

    if meta.kind == "col":
        o_map = lambda i, s_ref: (i, s_ref[0])
    else:
        o_map = lambda i, s_ref: (i + s_ref[0] * nr, 0)
    return pl.pallas_call(
        body, name="place_shard",
        grid_spec=pltpu.PrefetchScalarGridSpec(
            num_scalar_prefetch=1, grid=(nr,),
            in_specs=[pl.BlockSpec((tr, cols), lambda i, s_ref: (i, 0))],
            out_specs=pl.BlockSpec((tr, cols), o_map)),
        out_shape=jax.ShapeDtypeStruct((meta.rows, meta.cols), BF16),
        compiler_params=_params(("parallel",)),
    )(s_arr, w)


def _gather_weights(metas, fulls):
    nt = len(metas)

    def plan(in_refs, out_refs, me):
        x, y, c = me
        s_me = 2 * x + y
        ici, d2d = [], []
        for meta, full in zip(metas, out_refs):
            for mask, bits in CHIP_FLIPS:
                s_peer = jnp.bitwise_xor(s_me, bits)
                ici.append((mask, meta.piece_of_full(full, s_me, c), meta.piece_of_full(full, s_me, c),
                            meta.piece_of_full(full, s_peer, c)))
                d2d.append((FLIP_C, meta.piece_of_full(full, s_peer, c), meta.piece_of_full(full, s_peer, c),
                            meta.piece_of_full(full, s_peer, 1 - c)))
        return [], [ici, d2d]

    outs = [jax.ShapeDtypeStruct((m.rows, m.cols), BF16) for m in metas]
    return _comm("gather_weights", list(fulls), outs, plan, 6 * nt, 0, aliases={i: i for i in range(nt)})


def _reduce_pair(metas, grads):
    def plan(in_refs, out_refs, me):
        c = me[2]
        phase = [(FLIP_C, m.half_of_full(g, 1 - c), land, land) for m, g, land in zip(metas, in_refs, out_refs)]
        return [], [phase]

    outs = [jax.ShapeDtypeStruct(m.half_shape, BF16) for m in metas]
    return _comm("reduce_pair", list(grads), outs, plan, len(metas), 0)


def _reduce_chips(metas, halves):
    def plan(in_refs, out_refs, me):
        x, y, _ = me
        s_me = 2 * x + y
        phase = []
        for m, p, land in zip(metas, in_refs, out_refs):
            for kk, (mask, bits) in enumerate(CHIP_FLIPS):
                s_peer = jnp.bitwise_xor(s_me, bits)
                phase.append((mask, m.shard_of_half(p, s_peer), land.at[kk], land.at[kk]))
        return [], [phase]

    outs = [jax.ShapeDtypeStruct((3,) + m.piece_shape, BF16) for m in metas]
    return _comm("reduce_chips", list(halves), outs, plan, 3 * len(metas), 0)


def _share_halves(metas, shards):
    def plan(in_refs, out_refs, me):
        c = me[2]
        phase = [(FLIP_C, m.half_of_shard(g, c), m.half_of_shard(g, c), m.half_of_shard(g, 1 - c))
                 for m, g in zip(metas, out_refs)]
        return [], [phase]

    outs = [jax.ShapeDtypeStruct(m.shard_shape, F32) for m in metas]
    return _comm("share_halves", list(shards), outs, plan, len(metas), 0,
                 aliases={i: i for i in range(len(metas))})


def _pair_sum(meta, grad, landed, c_arr):
    hr, hc = meta.half_shape
    tr = _tile(hr, 256, 16)
    tc = _tile(hc, 2048)
    nr, ncol = hr // tr, hc // tc

    def body(c_ref, g_ref, l_ref, o_ref):
        o_ref[...] = (g_ref[...].astype(F32) + l_ref[...].astype(F32)).astype(BF16)

    if meta.kind == "col":
        g_map = lambda i, j, c_ref: (i + c_ref[0] * nr, j)
    else:
        g_map = lambda i, j, c_ref: (i, j + c_ref[0] * ncol)
    blk = (tr, tc)
    return pl.pallas_call(
        body, name="pair_sum",
        grid_spec=pltpu.PrefetchScalarGridSpec(
            num_scalar_prefetch=1, grid=(nr, ncol),
            in_specs=[pl.BlockSpec(blk, g_map), pl.BlockSpec(blk, lambda i, j, c_ref: (i, j))],
            out_specs=pl.BlockSpec(blk, lambda i, j, c_ref: (i, j))),
        out_shape=jax.ShapeDtypeStruct((hr, hc), BF16),
        compiler_params=_params(("parallel", "parallel")),
    )(c_arr, grad, landed)


def _sum_pieces(meta, half, landed, s_arr, c_arr):
    pr, pc = meta.piece_shape
    tr = _tile(pr, 256, 16)
    tc = _tile(pc, 2048)
    nr, ncol = pr // tr, pc // tc

    def body(s_ref, c_ref, p_ref, l_ref, o_ref):
        acc = p_ref[...].astype(F32)
        for kk in range(3):
            acc = acc + l_ref[kk].astype(F32)
        o_ref[...] = acc

    if meta.kind == "col":
        p_map = lambda i, j, s_ref, c_ref: (i, j + s_ref[0] * ncol)
        o_map = lambda i, j, s_ref, c_ref: (i + c_ref[0] * nr, j)
    else:
        p_map = lambda i, j, s_ref, c_ref: (i + s_ref[0] * nr, j)
        o_map = lambda i, j, s_ref, c_ref: (i, j + c_ref[0] * ncol)
    blk = (tr, tc)
    return pl.pallas_call(
        body, name="sum_pieces",
        grid_spec=pltpu.PrefetchScalarGridSpec(
            num_scalar_prefetch=2, grid=(nr, ncol),
            in_specs=[pl.BlockSpec(blk, p_map), pl.BlockSpec((3,) + blk, lambda i, j, s_ref, c_ref: (0, i, j))],
            out_specs=pl.BlockSpec(blk, o_map)),
        out_shape=jax.ShapeDtypeStruct(meta.shard_shape, F32),
        compiler_params=_params(("parallel", "parallel")),
    )(s_arr, c_arr, half, landed)


def _adam_rows(rows, sel, fulls):
    w, g, m, v = rows
    m2 = ADAM_B1 * m + (1.0 - ADAM_B1) * g
    v2 = ADAM_B2 * v + (1.0 - ADAM_B2) * jnp.square(g)
    m_hat = m2 / (1.0 - ADAM_B1 ** ADAM_STEP)
    v_hat = v2 / (1.0 - ADAM_B2 ** ADAM_STEP)
    delta = -ADAM_LR * (m_hat / (jnp.sqrt(v_hat) + ADAM_EPS) + ADAM_WD * w)
    return [delta, m2, v2], []


def _adamw(w, g, m, v):
    r, c = w.shape
    tr = _tile(r, 128, 8)
    outs = _rowwise("adamw", _adam_rows, n_tiles=r // tr, tr=tr,
                    row_ins=[(w, 0, None), (g, 0, None), (m, 0, None), (v, 0, None)],
                    row_outs=[(r, c, F32, 0)] * 3)
    return outs[0], outs[1], outs[2]


def _ada_fwd(cg, w, b):
    d, n = w.shape
    tn = _tile(n, 512)

    def body(c_ref, w_ref, b_ref, o_ref):
        a = _silu(c_ref[...]).astype(BF16)
        o_ref[...] = jnp.dot(a, w_ref[...].astype(BF16), preferred_element_type=F32) + b_ref[...]

    return pl.pallas_call(
        body, name="ada_fwd", grid=(n // tn,),
        in_specs=[pl.BlockSpec(cg.shape, lambda j: (0, 0)), pl.BlockSpec((d, tn), lambda j: (0, j)),
                  pl.BlockSpec((1, tn), lambda j: (0, j))],
        out_specs=pl.BlockSpec((cg.shape[0], tn), lambda j: (0, j)),
        out_shape=jax.ShapeDtypeStruct((cg.shape[0], n), F32),
        compiler_params=_params(("parallel",)),
    )(cg, w, b)


def _ada_bwd(cg, dm, w):
    d, n = w.shape
    tn = _tile(n, 512)
    nj = n // tn

    def body(c_ref, dm_ref, w_ref, gw_ref, da_ref, acc):
        j = pl.program_id(0)

        @pl.when(j == 0)
        def _():
            acc[...] = jnp.zeros_like(acc)

        a = _silu(c_ref[...]).astype(BF16)
        dmv = dm_ref[...].astype(BF16)
        gw_ref[...] = lax.dot_general(a, dmv, TN_DIMS, preferred_element_type=F32)
        acc[...] += lax.dot_general(dmv, w_ref[...].astype(BF16), NT_DIMS, preferred_element_type=F32)

        @pl.when(j == nj - 1)
        def _():
            da_ref[...] = acc[...]

    return pl.pallas_call(
        body, name="ada_bwd", grid=(nj,),
        in_specs=[pl.BlockSpec(cg.shape, lambda j: (0, 0)), pl.BlockSpec((dm.shape[0], tn), lambda j: (0, j)),
                  pl.BlockSpec((d, tn), lambda j: (0, j))],
        out_specs=[pl.BlockSpec((d, tn), lambda j: (0, j)), pl.BlockSpec(cg.shape, lambda j: (0, 0))],
        out_shape=[jax.ShapeDtypeStruct((d, n), F32), jax.ShapeDtypeStruct(cg.shape, F32)],
        scratch_shapes=[pltpu.VMEM(cg.shape, F32)],
        compiler_params=_params(("arbitrary",)),
    )(cg, dm, w)


def _small_reduce(gathered, logits, n_mod_cols, lg_off, loss_off, loss_cols):
    npk = gathered.shape[1]

    def body(g_ref, lo_ref, tot_ref, gb_ref, gl_ref, loss_ref):
        acc = g_ref[0:1, :]
        for dd in range(1, N_DEV):
            acc = acc + g_ref[dd:dd + 1, :]
        tot_ref[...] = acc
        gb_ref[...] = acc[:, :n_mod_cols] + acc[:, n_mod_cols:2 * n_mod_cols]
        gl_ref[...] = acc[:, lg_off:lg_off + LANES_V7X] * _sigmoid(-lo_ref[...])
        loss = jnp.sum(acc[:, loss_off:loss_off + loss_cols], axis=1, keepdims=True)
        loss_ref[...] = jnp.broadcast_to(loss, loss_ref.shape)

    lane = jax.ShapeDtypeStruct((1, LANES_V7X), F32)
    return pl.pallas_call(
        body, name="small_reduce",
        out_shape=[jax.ShapeDtypeStruct((1, npk), F32), jax.ShapeDtypeStruct((1, n_mod_cols), F32), lane, lane],
    )(gathered, logits)


def _c_ctx_grad(parts, c_ctx):
    def body(p_ref, c_ref, o_ref):
        tot = p_ref[0:1, :] + p_ref[2:3, :] + p_ref[4:5, :] + p_ref[6:7, :]
        _, vjp = jax.vjp(_silu, c_ref[...])
        o_ref[...] = vjp(tot)[0]

    return pl.pallas_call(body, name="c_ctx_grad", out_shape=jax.ShapeDtypeStruct(c_ctx.shape, F32))(parts, c_ctx)


def _rope_tables(seq, ctx_rows):
    rows = seq // GRID_W
    row = jnp.repeat(jnp.arange(rows, dtype=F32), GRID_W)
    col = jnp.tile(jnp.arange(GRID_W, dtype=F32), rows)
    half = HEAD_DIM // 2
    inv_freq = ROPE_THETA ** (-jnp.arange(0, half, 2, dtype=F32) / half)
    ang = jnp.concatenate([row[:, None] * inv_freq, col[:, None] * inv_freq], axis=-1)
    cos, sin = jnp.cos(ang), jnp.sin(ang)
    cos_full = jnp.repeat(cos, 2, axis=1)
    sin_signed = jnp.stack([-sin, sin], axis=-1).reshape(seq, HEAD_DIM)
    cos_full = jnp.concatenate([jnp.ones((ctx_rows, HEAD_DIM), F32), cos_full], axis=0)
    sin_signed = jnp.concatenate([jnp.zeros((ctx_rows, HEAD_DIM), F32), sin_signed], axis=0)
    return cos_full, sin_signed


def _qk_rot(p, gain, cos_full, sin_signed):
    r = _rmsn(p) * gain
    return r * cos_full + _swap_pairs(r) * sin_signed


def _qk_rot_bwd(g, p, gain, cos_full, sin_signed):
    g1 = g * cos_full + _swap_pairs(g * sin_signed)
    _, vjp = jax.vjp(lambda pp, gn: _rmsn(pp) * gn, p, gain)
    return vjp(g1)


def kernel(x, c, ctx, c_ctx, w_ada, b_ada, ffn1_w_in, ffn1_w_out, mix_w_in, attn_q_gain, attn_k_gain, ret_decay_logit, w_proj_attn, w_proj_ret, mix_w_out, ffn2_w_in, ffn2_w_out, final_norm, loss_target, m_c_ctx, m_w_ada, m_b_ada, m_ffn1_w_in, m_ffn1_w_out, m_mix_w_in, m_attn_q_gain, m_attn_k_gain, m_ret_decay_logit, m_w_proj_attn, m_w_proj_ret, m_mix_w_out, m_ffn2_w_in, m_ffn2_w_out, m_final_norm, v_c_ctx, v_w_ada, v_b_ada, v_ffn1_w_in, v_ffn1_w_out, v_mix_w_in, v_attn_q_gain, v_attn_k_gain, v_ret_decay_logit, v_w_proj_attn, v_w_proj_ret, v_mix_w_out, v_ffn2_w_in, v_ffn2_w_out, v_final_norm):
    xi, yi, ci = lax.axis_index("x"), lax.axis_index("y"), lax.axis_index("c")
    dev = 4 * xi + 2 * yi + ci
    s_me = 2 * xi + yi
    c_arr = jnp.reshape(ci, (1,)).astype(jnp.int32)
    s_arr = jnp.reshape(s_me, (1,)).astype(jnp.int32)

    t, d = x.shape[1], x.shape[2]
    tc = ctx.shape[1]
    tk = tc + t
    ff = ffn1_w_out.shape[1] * N_CHIP
    aw = w_proj_attn.shape[1]
    rw = w_proj_ret.shape[1]
    pw = mix_w_in.shape[2] * N_CHIP
    kvw = (pw - aw - 4 * rw - 2 * d) // 2
    groups = aw // kvw
    n_ret_heads = rw // HEAD_DIM
    mod_cols = N_MOD * d
    tr = _tile(tc, 256, 32)
    nt_all, nt_x, ctx_tiles = tk // tr, t // tr, tc // tr

    c_rows = _gather_row("gather_c", c)
    cg = jnp.concatenate([c_rows, c_ctx[None, :], jnp.zeros((7, d), F32)], axis=0)
    w_ada_l = w_ada[0]
    ada_cols = w_ada_l.shape[1]
    b_ada_l = lax.dynamic_slice_in_dim(b_ada, s_me * ada_cols, ada_cols, axis=1)
    mod_shard = _ada_fwd(cg, w_ada_l, b_ada_l)
    mod_g = _all_gather8("gather_mod", mod_shard)
    mod_full = jnp.concatenate([mod_g[0], mod_g[2], mod_g[4], mod_g[6]], axis=1)
    mod_x = lax.dynamic_slice_in_dim(mod_full, dev, 1, axis=0).reshape(N_MOD, d)
    mod_c = mod_full[8].reshape(N_MOD, d)
    mods = jnp.stack([mod_c, mod_x])

    big = [("col", ffn1_w_in), ("row", ffn1_w_out), ("col", mix_w_in), ("col", w_proj_attn), ("col", w_proj_ret),
           ("row", mix_w_out), ("col", ffn2_w_in), ("row", ffn2_w_out)]
    metas = []
    for kind, w in big:
        r_l, c_l = w.shape[1:]
        metas.append(_Sharded(kind, r_l, c_l * N_CHIP) if kind == "col" else _Sharded(kind, r_l * N_CHIP, c_l))
    placed = [_place_shard(m, w[0], s_arr) for m, (_, w) in zip(metas, big)]
    w1i, w1o, wmi, wpa, wpr, wmo, w2i, w2o = _gather_weights(metas, placed)

    cos_full, sin_signed = _rope_tables(t, tc)
    q_gain, k_gain = attn_q_gain, attn_k_gain
    log_gamma = jax.nn.log_sigmoid(ret_decay_logit[0])
    lgv = jnp.broadcast_to(log_gamma[:, :, None, None], (2, n_ret_heads, 1, HEAD_DIM))

    def norm_mod(name, h, n_tiles, off, i_shift, i_scale):
        def fn(rows, sel, fulls):
            return [_rmsn(rows[0]) * (1.0 + sel(i_scale)) + sel(i_shift)], []
        return _rowwise(name, fn, n_tiles=n_tiles, tr=tr, row_ins=[(h, 0, None)],
                        row_outs=[(h.shape[0], d, BF16, 0)], sel_in=mods, sel_off=off, ctx_rows=tc)[0]

    def resid(name, h, h_off, f, n_tiles, off, i_gate, coef):
        def fn(rows, sel, fulls):
            return [rows[0] + coef * sel(i_gate) * rows[1]], []
        return _rowwise(name, fn, n_tiles=n_tiles, tr=tr, row_ins=[(h, h_off, None), (f, 0, None)],
                        row_outs=[(f.shape[0], d, F32, 0)], sel_in=mods, sel_off=off, ctx_rows=tc)[0]

    h0 = jnp.concatenate([ctx[0], x[0]], axis=0)
    n1 = norm_mod("norm_mod1", h0, nt_all, 0, 0, 1)
    hm1, ua1, ub1 = _mm_swiglu("ffn1_in", n1, w1i)
    f1 = _mm("ffn1_out", hm1, w1o, "nn", F32)
    h1 = resid("resid1", h0, 0, f1, nt_all, 0, 2, 0.5)

    n2 = norm_mod("norm_mod2", h1, nt_all, 0, 3, 4)
    p_q = _mm("mix_in_q", n2, wmi, "nn", F32, 0, aw)
    p_kv = _mm("mix_in_kv", n2, wmi, "nn", F32, aw, 2 * kvw)
    p_r = _mm("mix_in_ret", n2, wmi, "nn", F32, aw + 2 * kvw, 3 * rw)
    p_gr = _mm("mix_in_gr", n2, wmi, "nn", F32, aw + 2 * kvw + 3 * rw, rw)
    p_gab = _mm("mix_in_gab", n2, wmi, "nn", F32, aw + 2 * kvw + 4 * rw, 2 * d)

    def q_prep(rows, sel, fulls):
        p, cf, ss = rows
        return _heads_map(lambda ph: [_qk_rot(ph, fulls[0], cf, ss) * QSCALE], [p], aw), []

    q_rot = _rowwise("q_prep", q_prep, n_tiles=nt_x, tr=tr,
                     row_ins=[(p_q, ctx_tiles, None), (cos_full, ctx_tiles, None), (sin_signed, ctx_tiles, None)],
                     row_outs=[(t, aw, BF16, 0)], full_ins=[q_gain])[0]

    def kv_prep(rows, sel, fulls):
        p, cf, ss = rows
        k_rot = _heads_map(lambda ph: [_qk_rot(ph, fulls[0], cf, ss)], [p[:, :kvw]], kvw)[0]
        v_ones = _heads_map(lambda vh: [jnp.concatenate([vh, jnp.ones_like(vh)], axis=1)], [p[:, kvw:]], kvw)[0]
        return [k_rot, v_ones], []

    k_rot, v_att = _rowwise("kv_prep", kv_prep, n_tiles=nt_all, tr=tr,
                            row_ins=[(p_kv, 0, None), (cos_full, 0, None), (sin_signed, 0, None)],
                            row_outs=[(tk, kvw, BF16, 0), (tk, 2 * kvw, BF16, 0)], full_ins=[k_gain])

    ya, lse = _flash_fwd(q_rot, k_rot, v_att, groups)
    y_fwd, y_bwd, states = _ret_fwd(p_r, lgv, tc)

    def ret_out_fn(yf, yb, gr):
        return [_silu(gr) * _rmsn(yf + yb)]

    def ret_out(rows, sel, fulls):
        return _heads_map(ret_out_fn, rows, rw), []

    y_rows = [(y_fwd, ctx_tiles, None), (y_bwd, ctx_tiles, None), (p_gr, ctx_tiles, None)]
    yr = _rowwise("ret_out", ret_out, n_tiles=nt_x, tr=tr, row_ins=y_rows, row_outs=[(t, rw, BF16, 0)])[0]

    pa = _mm("proj_attn", ya, wpa, "nn", F32)
    prj = _mm("proj_ret", yr, wpr, "nn", F32)

    def merge_fn(a, r, ga, gb):
        return _sigmoid(ga) * a + _sigmoid(gb) * r

    gate_rows = [(p_gab, ctx_tiles, (d, 0)), (p_gab, ctx_tiles, (d, 1))]
    z = _rowwise("merge", lambda rows, sel, fulls: ([merge_fn(*rows)], []), n_tiles=nt_x, tr=tr,
                 row_ins=[(pa, 0, None), (prj, 0, None)] + gate_rows, row_outs=[(t, d, BF16, 0)])[0]
    fo = _mm("mix_out", z, wmo, "nn", F32)
    h2 = resid("resid2", h1, ctx_tiles, fo, nt_x, ctx_tiles, 5, 1.0)

    n3 = norm_mod("norm_mod3", h2, nt_x, ctx_tiles, 6, 7)
    hm2, ua2, ub2 = _mm_swiglu("ffn2_in", n3, w2i)
    f2 = _mm("ffn2_out", hm2, w2o, "nn", F32)
    h3 = resid("resid3", h2, 0, f2, nt_x, ctx_tiles, 8, 0.5)

    def loss_fn(rows, sel, fulls):
        h, tgt = rows
        y, vjp = jax.vjp(lambda hh, ww: _rmsn(hh) * ww, h, fulls[0])
        err = y - tgt
        dh, dw = vjp(err / d)
        return [dh], [0.5 / d * jnp.sum(err * err, axis=0, keepdims=True), dw]

    dh3, loss_acc = _rowwise("loss_head", loss_fn, n_tiles=nt_x, tr=tr,
                             row_ins=[(h3, 0, None), (loss_target[0], 0, None)], row_outs=[(t, d, F32, 0)],
                             full_ins=[final_norm[None, :]], acc_shape=(8, d), sel_off=ctx_tiles, ctx_rows=tc)
    loss_cols, g_final = loss_acc[1, 0:1], loss_acc[1, 1:2]

    def gate_bwd(name, dh, f, n_tiles, off, i_gate, coef):
        def fn(rows, sel, fulls):
            dhh, fv = rows
            return [coef * sel(i_gate) * dhh], [jnp.sum(coef * dhh * fv, axis=0, keepdims=True)]
        return _rowwise(name, fn, n_tiles=n_tiles, tr=tr, row_ins=[(dh, 0, None), (f, 0, None)],
                        row_outs=[(dh.shape[0], d, BF16, 0)], sel_in=mods, sel_off=off, ctx_rows=tc,
                        acc_shape=(8, d))

    def swiglu_bwd(name, dhm, ua, ub):
        rows_n = dhm.shape[0]
        tr_w = _tile(tr, 128, 32)

        def fn(rows, sel, fulls):
            g, a, b = rows
            _, vjp = jax.vjp(lambda aa, bb: _silu(aa) * bb, a.astype(F32), b.astype(F32))
            da, db = vjp(g)
            return [jnp.concatenate([da, db], axis=1)], []
        return _rowwise(name, fn, n_tiles=rows_n // tr_w, tr=tr_w,
                        row_ins=[(dhm, 0, None), (ua, 0, None), (ub, 0, None)],
                        row_outs=[(rows_n, 2 * ff, BF16, 0)])[0]

    def norm_mod_bwd(name, dn, h, dres, n_tiles, off, i_shift, i_scale):
        def fn(rows, sel, fulls):
            g, hh, dr = rows
            _, vjp = jax.vjp(lambda a, sh, sc: _rmsn(a) * (1.0 + sc) + sh, hh,
                             sel(i_shift), sel(i_scale))
            dh, dsh, dsc = vjp(g)
            return [dr + dh], [dsh, dsc]
        return _rowwise(name, fn, n_tiles=n_tiles, tr=tr, row_ins=[(dn, 0, None), (h, 0, None), (dres, 0, None)],
                        row_outs=[(dn.shape[0], d, F32, 0)], sel_in=mods, sel_off=off, ctx_rows=tc,
                        acc_shape=(8, d))

    df2, acc_g3 = gate_bwd("gate_bwd3", dh3, f2, nt_x, ctx_tiles, 8, 0.5)
    g_w2o = _mm("ffn2_out_dw", hm2, df2, "tn", BF16)
    dhm2 = _mm("ffn2_out_dx", df2, w2o, "nt", F32)
    du2 = swiglu_bwd("swiglu_bwd2", dhm2, ua2, ub2)
    g_w2i = _mm("ffn2_in_dw", n3, du2, "tn", BF16)
    dn3 = _mm("ffn2_in_dx", du2, w2i, "nt", F32)
    dh2, acc_n3 = norm_mod_bwd("norm_mod_bwd3", dn3, h2, dh3, nt_x, ctx_tiles, 6, 7)

    dfo, acc_g2 = gate_bwd("gate_bwd2", dh2, fo, nt_x, ctx_tiles, 5, 1.0)
    g_wmo = _mm("mix_out_dw", z, dfo, "tn", BF16)
    dz = _mm("mix_out_dx", dfo, wmo, "nt", F32)

    def merge_bwd(rows, sel, fulls):
        g, a, r, ga, gb = rows
        _, vjp = jax.vjp(merge_fn, a, r, ga, gb)
        da, dr, dga, dgb = vjp(g)
        return [da, dr, jnp.concatenate([dga, dgb], axis=1)], []

    dpa, dpr, dgab = _rowwise("merge_bwd", merge_bwd, n_tiles=nt_x, tr=tr,
                              row_ins=[(dz, 0, None), (pa, 0, None), (prj, 0, None)] + gate_rows,
                              row_outs=[(t, d, BF16, 0), (t, d, BF16, 0), (t, 2 * d, BF16, 0)])
    g_wpa = _mm("proj_attn_dw", ya, dpa, "tn", BF16)
    dya = _mm("proj_attn_dx", dpa, wpa, "nt", BF16)
    g_wpr = _mm("proj_ret_dw", yr, dpr, "tn", BF16)
    dyr = _mm("proj_ret_dx", dpr, wpr, "nt", F32)

    def ret_out_bwd(rows, sel, fulls):
        def per_head(g, yf, yb, gr):
            _, vjp = jax.vjp(lambda yy, gg: ret_out_fn(yy, 0.0, gg)[0], yf + yb, gr)
            return list(vjp(g))
        dy, dgr = _heads_map(per_head, rows, rw)
        return [dy, dgr], []

    dy_ret, dgr = _rowwise("ret_out_bwd", ret_out_bwd, n_tiles=nt_x, tr=tr, row_ins=[(dyr, 0, None)] + y_rows,
                           row_outs=[(t, rw, F32, 0), (t, rw, BF16, 0)])
    dy_all = jnp.concatenate([jnp.zeros((tc, rw), F32), dy_ret], axis=0)
    dp_rf, dp_rb, dlg = _ret_bwd(p_r, states, dy_all, lgv, tc)
    dp_r = _rowwise("ret_bwd_sum", lambda rows, sel, fulls: ([rows[0] + rows[1]], []), n_tiles=nt_all, tr=tr,
                    row_ins=[(dp_rf, 0, None), (dp_rb, 0, None)], row_outs=[(tk, 3 * rw, BF16, 0)])[0]

    dq_rot, dk_rot, dv_att = _flash_bwd(q_rot, k_rot, v_att, ya, dya, lse, groups)

    def q_prep_bwd(rows, sel, fulls):
        g, p, cf, ss = rows
        gain_acc = []

        def per_head(gh, ph):
            dp, dgain = _qk_rot_bwd(gh * HEAD_DIM ** -0.5, ph, fulls[0], cf, ss)
            gain_acc.append(dgain)
            return [dp]
        dp = _heads_map(per_head, [g, p], aw)[0]
        return [dp], [functools.reduce(lambda a, b: a + b, gain_acc)]

    dp_q, acc_gq = _rowwise("q_prep_bwd", q_prep_bwd, n_tiles=nt_x, tr=tr,
                            row_ins=[(dq_rot, 0, None), (p_q, ctx_tiles, None), (cos_full, ctx_tiles, None),
                                     (sin_signed, ctx_tiles, None)],
                            row_outs=[(t, aw, BF16, 0)], full_ins=[q_gain], acc_shape=(8, HEAD_DIM),
                            sel_off=ctx_tiles, ctx_rows=tc)

    def kv_prep_bwd(rows, sel, fulls):
        gk, gv, p, cf, ss = rows
        gain_acc = []

        def per_head(gh, ph):
            dp, dgain = _qk_rot_bwd(gh, ph, fulls[0], cf, ss)
            gain_acc.append(dgain)
            return [dp]
        dpk = _heads_map(per_head, [gk, p], kvw)[0]
        return [jnp.concatenate([dpk, gv], axis=1)], [functools.reduce(lambda a, b: a + b, gain_acc)]

    dp_kv, acc_gk = _rowwise("kv_prep_bwd", kv_prep_bwd, n_tiles=nt_all, tr=tr,
                             row_ins=[(dk_rot, 0, None), (dv_att, 0, None), (p_kv, 0, (kvw, 0)), (cos_full, 0, None),
                                      (sin_signed, 0, None)],
                             row_outs=[(tk, 2 * kvw, BF16, 0)], full_ins=[k_gain], acc_shape=(8, HEAD_DIM),
                             sel_off=0, ctx_rows=tc)

    def with_ctx_zeros(a):
        return jnp.concatenate([jnp.zeros((tc, a.shape[1]), a.dtype), a], axis=0)

    dp = jnp.concatenate([with_ctx_zeros(dp_q), dp_kv, dp_r, with_ctx_zeros(dgr), with_ctx_zeros(dgab)], axis=1)
    g_wmi = _mm("mix_in_dw", n2, dp, "tn", BF16)
    dn2 = _mm("mix_in_dx", dp, wmi, "nt", F32)
    dh1, acc_n2 = norm_mod_bwd("norm_mod_bwd2", dn2, h1, with_ctx_zeros(dh2), nt_all, 0, 3, 4)

    df1, acc_g1 = gate_bwd("gate_bwd1", dh1, f1, nt_all, 0, 2, 0.5)
    g_w1o = _mm("ffn1_out_dw", hm1, df1, "tn", BF16)
    dhm1 = _mm("ffn1_out_dx", df1, w1o, "nt", F32)
    du1 = swiglu_bwd("swiglu_bwd1", dhm1, ua1, ub1)
    g_w1i = _mm("ffn1_in_dw", n1, du1, "tn", BF16)
    dn1 = _mm("ffn1_in_dx", du1, w1i, "nt", F32)
    dh0, acc_n1 = norm_mod_bwd("norm_mod_bwd1", dn1, h0, dh1, nt_all, 0, 0, 1)
    grad_x = dh0[tc:][None]

    grads_full = [g_w1i, g_w1o, g_wmi, g_wpa, g_wpr, g_wmo, g_w2i, g_w2o]
    landed = _reduce_pair(metas, grads_full)
    halves = [_pair_sum(m, g, l, c_arr) for m, g, l in zip(metas, grads_full, landed)]
    landed3 = _reduce_chips(metas, halves)
    pieces = [_sum_pieces(m, p, l, s_arr, c_arr) for m, p, l in zip(metas, halves, landed3)]
    grads_big = _share_halves(metas, pieces)

    zero_row = jnp.zeros((1, d), F32)
    dmod_x = jnp.concatenate([acc_n1[1, 0:1], acc_n1[1, 1:2], acc_g1[1, 0:1], acc_n2[1, 0:1], acc_n2[1, 1:2],
                              acc_g2[1, 0:1], acc_n3[1, 0:1], acc_n3[1, 1:2], acc_g3[1, 0:1]], axis=1)
    dmod_c = jnp.concatenate([acc_n1[0, 0:1], acc_n1[0, 1:2], acc_g1[0, 0:1], acc_n2[0, 0:1], acc_n2[0, 1:2]]
                             + [zero_row] * 4, axis=1)
    dlg_row = jnp.pad(dlg[:, :, 0, 0].reshape(1, 2 * n_ret_heads), ((0, 0), (0, LANES_V7X - 2 * n_ret_heads)))
    packed = jnp.concatenate([dmod_x, dmod_c, acc_gq[1, 0:1], acc_gk[0, 0:1] + acc_gk[1, 0:1], dlg_row,
                              g_final, loss_cols], axis=1)
    off_gq = 2 * mod_cols
    off_gk = off_gq + LANES_V7X
    off_lg = off_gk + LANES_V7X
    off_fn = off_lg + LANES_V7X
    off_loss = off_fn + d
    gathered = _gather_row("gather_small", packed)
    logits_row = jnp.pad(ret_decay_logit.reshape(1, 2 * n_ret_heads), ((0, 0), (0, LANES_V7X - 2 * n_ret_heads)))
    totals, g_b_ada, g_decay, loss_row = _small_reduce(gathered, logits_row, mod_cols, off_lg, off_loss, d)
    loss = loss_row[0, 0]

    dm = jnp.concatenate([gathered[:, :mod_cols], totals[:, mod_cols:2 * mod_cols],
                          jnp.zeros((7, mod_cols), F32)], axis=0)
    dm_l = lax.dynamic_slice_in_dim(dm, s_me * ada_cols, ada_cols, axis=1)
    g_w_ada, da_part = _ada_bwd(cg, dm_l, w_ada_l)
    da_rows = _gather_row("gather_dc", da_part[8:9])
    g_c_ctx = _c_ctx_grad(da_rows, c_ctx[None, :])

    def as2d(a):
        return a.reshape(-1, a.shape[-1])

    grads = {
        "c_ctx": g_c_ctx, "w_ada": g_w_ada, "b_ada": g_b_ada,
        "ffn1_w_in": grads_big[0], "ffn1_w_out": grads_big[1], "mix_w_in": grads_big[2],
        "attn_q_gain": totals[:, off_gq:off_gq + HEAD_DIM], "attn_k_gain": totals[:, off_gk:off_gk + HEAD_DIM],
        "ret_decay_logit": g_decay[:, :2 * n_ret_heads],
        "w_proj_attn": grads_big[3], "w_proj_ret": grads_big[4], "mix_w_out": grads_big[5],
        "ffn2_w_in": grads_big[6], "ffn2_w_out": grads_big[7], "final_norm": totals[:, off_fn:off_fn + d],
    }
    weights = {"c_ctx": (c_ctx, m_c_ctx, v_c_ctx), "w_ada": (w_ada, m_w_ada, v_w_ada),
               "b_ada": (b_ada, m_b_ada, v_b_ada), "ffn1_w_in": (ffn1_w_in, m_ffn1_w_in, v_ffn1_w_in),
               "ffn1_w_out": (ffn1_w_out, m_ffn1_w_out, v_ffn1_w_out), "mix_w_in": (mix_w_in, m_mix_w_in, v_mix_w_in),
               "attn_q_gain": (attn_q_gain, m_attn_q_gain, v_attn_q_gain),
               "attn_k_gain": (attn_k_gain, m_attn_k_gain, v_attn_k_gain),
               "ret_decay_logit": (ret_decay_logit, m_ret_decay_logit, v_ret_decay_logit),
               "w_proj_attn": (w_proj_attn, m_w_proj_attn, v_w_proj_attn),
               "w_proj_ret": (w_proj_ret, m_w_proj_ret, v_w_proj_ret), "mix_w_out": (mix_w_out, m_mix_w_out, v_mix_w_out),
               "ffn2_w_in": (ffn2_w_in, m_ffn2_w_in, v_ffn2_w_in), "ffn2_w_out": (ffn2_w_out, m_ffn2_w_out, v_ffn2_w_out),
               "final_norm": (final_norm, m_final_norm, v_final_norm)}
    out_g, out_d, out_m, out_v = [], [], [], []
    for name, (w, m, v) in weights.items():
        shape = w.shape
        if name == "ret_decay_logit":
            w2, m2, v2 = (a.reshape(1, -1) for a in (w, m, v))
        else:
            w2, m2, v2 = as2d(w), as2d(m), as2d(v)
        g2 = grads[name].reshape(w2.shape)
        delta, new_m, new_v = _adamw(w2, g2, m2, v2)
        out_g.append(g2.reshape(shape))
        out_d.append(delta.reshape(shape))
        out_m.append(new_m.reshape(shape))
        out_v.append(new_v.reshape(shape))
    return (loss, grad_x, *out_g, *out_d, *out_m, *out_v)
```

```python
import functools
import math

import jax
import jax.numpy as jnp
from jax import lax
from jax.experimental import pallas as pl
from jax.experimental.pallas import tpu as pltpu

F32 = jnp.float32
BF16 = jnp.bfloat16

HEAD_DIM = 128
GRID_W = 64
ROPE_THETA = 10000.0
NORM_EPS = 1e-6
N_MOD = 9
RET_CHUNK = 128
ADAM_LR = 0.001
ADAM_B1 = 0.9
ADAM_B2 = 0.999
ADAM_EPS = 1e-08
ADAM_WD = 0.01
ADAM_STEP = 10

N_DEV = 8
N_CHIP = 4
LANES_V7X = 128
MXU_WIDTH_V7X = 256
VMEM_LIMIT_V7X = 52 * 1024 * 1024

NT_DIMS = (((1,), (1,)), ((), ()))
TN_DIMS = (((0,), (0,)), ((), ()))
NN_DIMS = (((1,), (0,)), ((), ()))


def _tile(n, pref, mult=LANES_V7X):
    if n <= pref:
        return n
    t = (pref // mult) * mult
    while t >= mult:
        if n % t == 0:
            return t
        t -= mult
    return n


def _params(sem):
    return pltpu.CompilerParams(dimension_semantics=sem, vmem_limit_bytes=VMEM_LIMIT_V7X)


def _sigmoid(x):
    return 1.0 / (1.0 + jnp.exp(-x))


def _silu(x):
    return x * _sigmoid(x)


def _rmsn(x):
    return x * lax.rsqrt(jnp.mean(x * x, axis=-1, keepdims=True) + NORM_EPS)


MM_VMEM_BUDGET = 36 * 1024 * 1024


def _divisor_tiles(n, cap):
    ts = [t for t in range(LANES_V7X, min(n, cap) + 1, LANES_V7X) if n % t == 0]
    return ts or [n]


def _mm_tiles(m, n, tk, out_bytes, has_acc):
    best = None
    for tm in _divisor_tiles(m, 1536):
        for tn in _divisor_tiles(n, 2560):
            need = 4 * tk * (tm + tn) + 2 * tm * tn * out_bytes + 4 * tm * tn
            if need > MM_VMEM_BUDGET:
                continue
            score = tm * tn / (tm + tn)
            for tdim in (tm, tn):
                if tdim % MXU_WIDTH_V7X:
                    score *= 0.85
            if best is None or score > best[0]:
                best = (score, tm, tn)
    return best[1], best[2]


def _mm(name, a, b, mode, out_dtype, b_off=0, n=None):
    if mode == "nn":
        m, k = a.shape
        n = b.shape[1] if n is None else n
        dims = NN_DIMS
    elif mode == "nt":
        m, k = a.shape
        n = b.shape[0]
        dims = NT_DIMS
    else:
        k, m = a.shape
        n = b.shape[1]
        dims = TN_DIMS
    tk = _tile(k, 2560) if mode != "tn" else _tile(k, 1024)
    nk = k // tk
    tm, tn = _mm_tiles(m, math.gcd(n, b_off) if b_off else n, tk, jnp.dtype(out_dtype).itemsize, nk > 1)
    joff = b_off // tn

    def body(a_ref, b_ref, o_ref, *acc):
        prod = lax.dot_general(a_ref[...], b_ref[...], dims, preferred_element_type=F32)
        if nk == 1:
            o_ref[...] = prod.astype(o_ref.dtype)
            return
        acc_ref, = acc
        kk = pl.program_id(2)

        @pl.when(kk == 0)
        def _():
            acc_ref[...] = jnp.zeros_like(acc_ref)

        acc_ref[...] += prod

        @pl.when(kk == nk - 1)
        def _():
            o_ref[...] = acc_ref[...].astype(o_ref.dtype)

    if mode == "nn":
        a_spec = pl.BlockSpec((tm, tk), lambda i, j, kk: (i, kk))
        b_spec = pl.BlockSpec((tk, tn), lambda i, j, kk: (kk, j + joff))
    elif mode == "nt":
        a_spec = pl.BlockSpec((tm, tk), lambda i, j, kk: (i, kk))
        b_spec = pl.BlockSpec((tn, tk), lambda i, j, kk: (j, kk))
    else:
        a_spec = pl.BlockSpec((tk, tm), lambda i, j, kk: (kk, i))
        b_spec = pl.BlockSpec((tk, tn), lambda i, j, kk: (kk, j))
    return pl.pallas_call(
        body, name=name, grid=(m // tm, n // tn, nk),
        in_specs=[a_spec, b_spec],
        out_specs=pl.BlockSpec((tm, tn), lambda i, j, kk: (i, j)),
        out_shape=jax.ShapeDtypeStruct((m, n), out_dtype),
        scratch_shapes=[pltpu.VMEM((tm, tn), F32)] if nk > 1 else [],
        compiler_params=_params(("parallel", "parallel", "arbitrary")),
    )(a, b)


def _mm_swiglu(name, a, w):
    m, k = a.shape
    f = w.shape[1] // 2
    tm = _tile(m, 1024)
    tn = _tile(f, 512)
    tk = _tile(k, 2560)
    nk = k // tk
    jf = f // tn

    def body(a_ref, wa_ref, wb_ref, h_ref, ua_ref, ub_ref, acca, accb):
        kk = pl.program_id(2)

        @pl.when(kk == 0)
        def _():
            acca[...] = jnp.zeros_like(acca)
            accb[...] = jnp.zeros_like(accb)

        av = a_ref[...]
        acca[...] += jnp.dot(av, wa_ref[...], preferred_element_type=F32)
        accb[...] += jnp.dot(av, wb_ref[...], preferred_element_type=F32)

        @pl.when(kk == nk - 1)
        def _():
            ua = acca[...]
            ub = accb[...]
            h_ref[...] = (_silu(ua) * ub).astype(BF16)
            ua_ref[...] = ua.astype(BF16)
            ub_ref[...] = ub.astype(BF16)

    o_spec = pl.BlockSpec((tm, tn), lambda i, j, kk: (i, j))
    o_shape = jax.ShapeDtypeStruct((m, f), BF16)
    return pl.pallas_call(
        body, name=name, grid=(m // tm, jf, nk),
        in_specs=[pl.BlockSpec((tm, tk), lambda i, j, kk: (i, kk)),
                  pl.BlockSpec((tk, tn), lambda i, j, kk: (kk, j)),
                  pl.BlockSpec((tk, tn), lambda i, j, kk: (kk, j + jf))],
        out_specs=[o_spec, o_spec, o_spec],
        out_shape=[o_shape, o_shape, o_shape],
        scratch_shapes=[pltpu.VMEM((tm, tn), F32), pltpu.VMEM((tm, tn), F32)],
        compiler_params=_params(("parallel", "parallel", "arbitrary")),
    )(a, w, w)


def _rowwise(name, fn, *, n_tiles, tr, row_ins, row_outs, sel_in=None, sel_off=0, ctx_rows=0,
             full_ins=(), acc_shape=None):
    sr = 32 if tr % 32 == 0 else tr
    n_row, n_full, n_out = len(row_ins), len(full_ins), len(row_outs)
    has_sel = sel_in is not None
    has_acc = acc_shape is not None

    def sel_of(i):
        return jnp.where((i + sel_off) * tr < ctx_rows, 0, 1)

    def body(*refs):
        row_refs = refs[:n_row]
        pos = n_row
        sel_ref = None
        if has_sel:
            sel_ref = refs[pos]
            pos += 1
        full_refs = refs[pos:pos + n_full]
        pos += n_full
        out_refs = refs[pos:pos + n_out]
        pos += n_out
        acc_ref = refs[pos] if has_acc else None
        i = pl.program_id(0)
        if has_acc:
            first = (i == 0) | ((i + sel_off) * tr == ctx_rows)

            @pl.when(first)
            def _():
                acc_ref[...] = jnp.zeros_like(acc_ref)

        sel = (lambda kk: sel_ref[kk:kk + 1, :]) if has_sel else None
        fulls = [r[...] for r in full_refs]

        def slab(r, carry):
            rs = pl.ds(pl.multiple_of(r * sr, sr), sr)
            rows = [ref[rs, :] for ref in row_refs]
            outs, accs = fn(rows, sel, fulls)
            for o_ref, o in zip(out_refs, outs):
                o_ref[rs, :] = o.astype(o_ref.dtype)
            for kk, a in enumerate(accs):
                acc_ref[kk:kk + 1, :a.shape[1]] += a
            return carry

        lax.fori_loop(0, tr // sr, slab, 0)

    in_specs, args = [], []
    for arr, off, blk in row_ins:
        if blk is None:
            in_specs.append(pl.BlockSpec((tr, arr.shape[1]), functools.partial(lambda i, o: (i + o, 0), o=off)))
        else:
            in_specs.append(pl.BlockSpec((tr, blk[0]), functools.partial(lambda i, o, cb: (i + o, cb), o=off, cb=blk[1])))
        args.append(arr)
    if has_sel:
        in_specs.append(pl.BlockSpec((None,) + sel_in.shape[1:], lambda i: (sel_of(i), 0, 0)))
        args.append(sel_in)
    for arr in full_ins:
        in_specs.append(pl.BlockSpec(arr.shape, lambda i: (0, 0)))
        args.append(arr)
    out_specs, out_shape = [], []
    for rows, cols, dt, off in row_outs:
        out_specs.append(pl.BlockSpec((tr, cols), functools.partial(lambda i, o: (i + o, 0), o=off)))
        out_shape.append(jax.ShapeDtypeStruct((rows, cols), dt))
    if has_acc:
        out_specs.append(pl.BlockSpec((None,) + tuple(acc_shape), lambda i: (sel_of(i), 0, 0)))
        out_shape.append(jax.ShapeDtypeStruct((2,) + tuple(acc_shape), F32))
    return pl.pallas_call(
        body, name=name, grid=(n_tiles,), in_specs=in_specs, out_specs=out_specs, out_shape=out_shape,
        compiler_params=_params(("arbitrary",)),
    )(*args)


def _swap_pairs(x):
    lane = lax.broadcasted_iota(jnp.int32, x.shape, 1)
    nxt = pltpu.roll(x, x.shape[1] - 1, 1)
    prv = pltpu.roll(x, 1, 1)
    return jnp.where(lane % 2 == 0, nxt, prv)


def _heads_map(fn, arrs, width):
    outs = None
    for h in range(width // HEAD_DIM):
        sl = slice(h * HEAD_DIM, (h + 1) * HEAD_DIM)
        res = fn(*[a[:, sl] for a in arrs])
        if outs is None:
            outs = [[] for _ in res]
        for lst, r in zip(outs, res):
            lst.append(r)
    return [jnp.concatenate(lst, axis=1) if len(lst) > 1 else lst[0] for lst in outs]


QSCALE = HEAD_DIM ** -0.5 * math.log2(math.e)
LN2 = math.log(2.0)


def _lane_chunks(a):
    return [a[:, cc * LANES_V7X:(cc + 1) * LANES_V7X] for cc in range(a.shape[1] // LANES_V7X)]


def _row_bcast(col, like):
    return jnp.broadcast_to(col, like.shape)


def _flash_tiles(t, tk_all):
    return _tile(t, 256), _tile(tk_all, 1024)


def _flash_fwd(q, k, vx, groups):
    t, aw = q.shape
    tk_all, kvw = k.shape
    kvh = kvw // HEAD_DIM
    gw = groups * HEAD_DIM
    tq, tk = _flash_tiles(t, tk_all)
    nk = tk_all // tk

    def body(q_ref, k_ref, v_ref, o_ref, lse_ref, m_sc, l_sc, acc_sc):
        j = pl.program_id(2)

        @pl.when(j == 0)
        def _():
            m_sc[...] = jnp.full_like(m_sc, -jnp.inf)
            l_sc[...] = jnp.zeros_like(l_sc)
            acc_sc[...] = jnp.zeros_like(acc_sc)

        kt = k_ref[...]
        vt = v_ref[...]
        for g in range(groups):
            sl = slice(g * HEAD_DIM, (g + 1) * HEAD_DIM)
            s = _lane_chunks(lax.dot_general(q_ref[:, sl], kt, NT_DIMS, preferred_element_type=F32))
            mx = functools.reduce(jnp.maximum, s)
            m_prev = m_sc[g]
            m_new = jnp.maximum(m_prev, _row_bcast(jnp.max(mx, axis=1, keepdims=True), mx))
            p = jnp.concatenate([jnp.exp2(sc - m_new).astype(BF16) for sc in s], axis=1)
            alpha = jnp.exp2(m_prev - m_new)
            pv = jnp.dot(p, vt, preferred_element_type=F32)
            acc_sc[g] = alpha * acc_sc[g] + pv[:, :HEAD_DIM]
            l_sc[g] = alpha * l_sc[g] + pv[:, HEAD_DIM:]
            m_sc[g] = m_new

        @pl.when(j == nk - 1)
        def _():
            for g in range(groups):
                sl = slice(g * HEAD_DIM, (g + 1) * HEAD_DIM)
                o_ref[:, sl] = (acc_sc[g] / l_sc[g]).astype(o_ref.dtype)
                lse_ref[:, sl] = m_sc[g] + jnp.log2(l_sc[g])

    qs = pl.BlockSpec((tq, gw), lambda kh, i, j: (i, kh))
    sc = pltpu.VMEM((groups, tq, HEAD_DIM), F32)
    return pl.pallas_call(
        body, name="flash_fwd", grid=(kvh, t // tq, nk),
        in_specs=[qs, pl.BlockSpec((tk, HEAD_DIM), lambda kh, i, j: (j, kh)),
                  pl.BlockSpec((tk, 2 * HEAD_DIM), lambda kh, i, j: (j, kh))],
        out_specs=[qs, qs],
        out_shape=[jax.ShapeDtypeStruct((t, aw), BF16), jax.ShapeDtypeStruct((t, aw), F32)],
        scratch_shapes=[sc, sc, sc],
        compiler_params=_params(("parallel", "parallel", "arbitrary")),
    )(q, k, vx)


def _flash_p_ds(q, kt, vt, do, lse, delta):
    s = _lane_chunks(lax.dot_general(q, kt, NT_DIMS, preferred_element_type=F32))
    dp = _lane_chunks(lax.dot_general(do, vt, NT_DIMS, preferred_element_type=F32))
    p = [jnp.exp2(sc - lse) for sc in s]
    ds = jnp.concatenate([(pc * (dc - delta)).astype(BF16) for pc, dc in zip(p, dp)], axis=1)
    return jnp.concatenate([pc.astype(BF16) for pc in p], axis=1), ds


def _flash_delta(do, o):
    prod = do.astype(F32) * o.astype(F32)
    return _row_bcast(jnp.sum(prod, axis=1, keepdims=True), prod)


def _flash_bwd(q, k, vx, o, do, lse, groups):
    t, aw = q.shape
    tk_all, kvw = k.shape
    kvh = kvw // HEAD_DIM
    gw = groups * HEAD_DIM
    tq, tk = _flash_tiles(t, tk_all)
    nq, nk = t // tq, tk_all // tk

    def body(q_ref, k_ref, v_ref, o_ref, do_ref, lse_ref, dq_ref, dk_ref, dv_ref, dq_sc, dk_acc, dv_acc):
        j = pl.program_id(1)
        i = pl.program_id(2)

        @pl.when(i == 0)
        def _():
            dk_acc[...] = jnp.zeros_like(dk_acc)
            dv_acc[...] = jnp.zeros_like(dv_acc)

        @pl.when(j == 0)
        def _():
            dq_sc[i] = jnp.zeros((groups, tq, HEAD_DIM), F32)

        kt = k_ref[...]
        vt = v_ref[:, :HEAD_DIM]
        for g in range(groups):
            sl = slice(g * HEAD_DIM, (g + 1) * HEAD_DIM)
            qv = q_ref[:, sl]
            dov = do_ref[:, sl]
            p, ds = _flash_p_ds(qv, kt, vt, dov, lse_ref[:, sl], _flash_delta(dov, o_ref[:, sl]))
            dv_acc[...] += lax.dot_general(p, dov, TN_DIMS, preferred_element_type=F32)
            dk_acc[...] += lax.dot_general(ds, qv, TN_DIMS, preferred_element_type=F32)
            dq_sc[i, g] += jnp.dot(ds, kt, preferred_element_type=F32)

        @pl.when(i == nq - 1)
        def _():
            dk_ref[...] = dk_acc[...] * LN2
            dv_ref[...] = dv_acc[...]

        @pl.when(j == nk - 1)
        def _():
            for g in range(groups):
                dq_ref[:, g * HEAD_DIM:(g + 1) * HEAD_DIM] = dq_sc[i, g]

    qs = pl.BlockSpec((tq, gw), lambda kh, j, i: (i, kh))
    ks = pl.BlockSpec((tk, HEAD_DIM), lambda kh, j, i: (j, kh))
    dq_spec = pl.BlockSpec((tq, gw), lambda kh, j, i: (jnp.where(j == nk - 1, i, 0), kh))
    return pl.pallas_call(
        body, name="flash_bwd", grid=(kvh, nk, nq),
        in_specs=[qs, ks, pl.BlockSpec((tk, 2 * HEAD_DIM), lambda kh, j, i: (j, kh)), qs, qs, qs],
        out_specs=[dq_spec, ks, ks],
        out_shape=[jax.ShapeDtypeStruct((t, aw), F32), jax.ShapeDtypeStruct((tk_all, kvw), F32),
                   jax.ShapeDtypeStruct((tk_all, kvw), F32)],
        scratch_shapes=[pltpu.VMEM((nq, groups, tq, HEAD_DIM), F32), pltpu.VMEM((tk, HEAD_DIM), F32),
                        pltpu.VMEM((tk, HEAD_DIM), F32)],
        compiler_params=_params(("parallel", "arbitrary", "arbitrary")),
    )(q, k, vx, o, do, lse)


def _bf_nn(a, b):
    return jnp.dot(a.astype(BF16), b.astype(BF16), preferred_element_type=F32)


def _bf_nt(a, b):
    return lax.dot_general(a.astype(BF16), b.astype(BF16), NT_DIMS, preferred_element_type=F32)


def _bf_tn(a, b):
    return lax.dot_general(a.astype(BF16), b.astype(BF16), TN_DIMS, preferred_element_type=F32)


@jax.custom_vjp
def _d_nn(a, b):
    return _bf_nn(a, b)


@jax.custom_vjp
def _d_nt(a, b):
    return _bf_nt(a, b)


@jax.custom_vjp
def _d_tn(a, b):
    return _bf_tn(a, b)


_d_nn.defvjp(lambda a, b: (_bf_nn(a, b), (a, b)), lambda r, g: (_d_nt(g, r[1]), _d_tn(r[0], g)))
_d_nt.defvjp(lambda a, b: (_bf_nt(a, b), (a, b)), lambda r, g: (_d_nn(g, r[1]), _d_tn(g, r[0])))
_d_tn.defvjp(lambda a, b: (_bf_tn(a, b), (a, b)), lambda r, g: (_d_nt(r[1], g), _d_nn(r[0], g)))


def _ret_chunk(q, k_raw, v, state, lg, rev, dots):
    nn, nt, tn = dots
    c = RET_CHUNK
    tcol = lax.broadcasted_iota(jnp.int32, (c, 1), 0).astype(F32)
    trow = lax.broadcasted_iota(jnp.int32, (1, c), 1).astype(F32)
    ucol = jnp.where(rev, c - 1.0 - tcol, tcol)
    urow = jnp.where(rev, c - 1.0 - trow, trow)
    e = ucol - urow
    low = e >= 0
    intra = jnp.where(low, jnp.exp(jnp.where(low, e, 0.0) * lg), 0.0)
    k = k_raw * (HEAD_DIM ** -0.5)
    inner = nt(q, k) * intra
    y = nn(inner, v) + nn(q, state) * jnp.exp((ucol + 1.0) * lg)
    new_state = state * jnp.exp(c * lg) + tn(k * jnp.exp((c - 1.0 - ucol) * lg), v)
    return y, new_state


def _ret_chunk_index(n_chunks, n_ctx_chunks):
    def idx(d, s):
        if d == 0:
            return s
        return jnp.where(s < n_ctx_chunks, n_ctx_chunks - 1 - s, n_chunks - 1 - s + n_ctx_chunks)
    return idx


def _ret_fwd(pr, lgv, ctx_rows):
    tk_all = pr.shape[0]
    rw = pr.shape[1] // 3
    nh = rw // HEAD_DIM
    nc = tk_all // RET_CHUNK
    cidx = _ret_chunk_index(nc, ctx_rows // RET_CHUNK)

    def body(pf_ref, pb_ref, lg_ref, yf_ref, yb_ref, st_ref, s_sc):
        s = pl.program_id(0)

        @pl.when(s == 0)
        def _():
            s_sc[...] = jnp.zeros_like(s_sc)

        for d, (p_ref, y_ref) in enumerate(((pf_ref, yf_ref), (pb_ref, yb_ref))):
            for h in range(nh):
                cols = [slice((part * nh + h) * HEAD_DIM, (part * nh + h + 1) * HEAD_DIM) for part in range(3)]
                state = s_sc[d, h]
                st_ref[d, h] = state
                y, new_state = _ret_chunk(p_ref[:, cols[0]], p_ref[:, cols[1]], p_ref[:, cols[2]], state,
                                          lg_ref[d, h][:, :1], d == 1, (_bf_nn, _bf_nt, _bf_tn))
                y_ref[:, h * HEAD_DIM:(h + 1) * HEAD_DIM] = y
                s_sc[d, h] = new_state

    y_shape = jax.ShapeDtypeStruct((tk_all, rw), F32)
    return pl.pallas_call(
        body, name="ret_fwd", grid=(nc,),
        in_specs=[pl.BlockSpec((RET_CHUNK, 3 * rw), lambda s: (cidx(0, s), 0)),
                  pl.BlockSpec((RET_CHUNK, 3 * rw), lambda s: (cidx(1, s), 0)),
                  pl.BlockSpec(lgv.shape, lambda s: (0, 0, 0, 0))],
        out_specs=[pl.BlockSpec((RET_CHUNK, rw), lambda s: (cidx(0, s), 0)),
                   pl.BlockSpec((RET_CHUNK, rw), lambda s: (cidx(1, s), 0)),
                   pl.BlockSpec((2, nh, None, HEAD_DIM, HEAD_DIM), lambda s: (0, 0, s, 0, 0))],
        out_shape=[y_shape, y_shape, jax.ShapeDtypeStruct((2, nh, nc, HEAD_DIM, HEAD_DIM), F32)],
        scratch_shapes=[pltpu.VMEM((2, nh, HEAD_DIM, HEAD_DIM), F32)],
        compiler_params=_params(("arbitrary",)),
    )(pr, pr, lgv)


def _ret_bwd(pr, states, dy, lgv, ctx_rows):
    tk_all = pr.shape[0]
    rw = pr.shape[1] // 3
    nh = rw // HEAD_DIM
    nc = tk_all // RET_CHUNK
    cidx = _ret_chunk_index(nc, ctx_rows // RET_CHUNK)

    def body(pf_ref, pb_ref, st_ref, dyf_ref, dyb_ref, lg_ref, dpf_ref, dpb_ref, dlg_ref, ds_sc):
        sp = pl.program_id(0)

        @pl.when(sp == 0)
        def _():
            ds_sc[...] = jnp.zeros_like(ds_sc)
            dlg_ref[...] = jnp.zeros_like(dlg_ref)

        for d, (p_ref, dy_ref, dp_ref) in enumerate(((pf_ref, dyf_ref, dpf_ref), (pb_ref, dyb_ref, dpb_ref))):
            for h in range(nh):
                cols = [slice((part * nh + h) * HEAD_DIM, (part * nh + h + 1) * HEAD_DIM) for part in range(3)]

                def step(q, k, v, state, lg, rev=(d == 1)):
                    return _ret_chunk(q, k, v, state, lg, rev, (_d_nn, _d_nt, _d_tn))

                _, vjp = jax.vjp(step, p_ref[:, cols[0]], p_ref[:, cols[1]], p_ref[:, cols[2]], st_ref[d, h],
                                 lg_ref[d, h][:, :1])
                grads = vjp((dy_ref[:, h * HEAD_DIM:(h + 1) * HEAD_DIM], ds_sc[d, h]))
                for part in range(3):
                    dp_ref[:, cols[part]] = grads[part]
                ds_sc[d, h] = grads[3]
                dlg_ref[d, h] += jnp.broadcast_to(grads[4], (1, HEAD_DIM))

    def at(d):
        return lambda sp: (cidx(d, nc - 1 - sp), 0)

    dp_shape = jax.ShapeDtypeStruct((tk_all, 3 * rw), F32)
    lg_spec = pl.BlockSpec(lgv.shape, lambda sp: (0, 0, 0, 0))
    return pl.pallas_call(
        body, name="ret_bwd", grid=(nc,),
        in_specs=[pl.BlockSpec((RET_CHUNK, 3 * rw), at(0)), pl.BlockSpec((RET_CHUNK, 3 * rw), at(1)),
                  pl.BlockSpec((2, nh, None, HEAD_DIM, HEAD_DIM), lambda sp: (0, 0, nc - 1 - sp, 0, 0)),
                  pl.BlockSpec((RET_CHUNK, rw), at(0)), pl.BlockSpec((RET_CHUNK, rw), at(1)), lg_spec],
        out_specs=[pl.BlockSpec((RET_CHUNK, 3 * rw), at(0)), pl.BlockSpec((RET_CHUNK, 3 * rw), at(1)), lg_spec],
        out_shape=[dp_shape, dp_shape, jax.ShapeDtypeStruct(lgv.shape, F32)],
        scratch_shapes=[pltpu.VMEM((2, nh, HEAD_DIM, HEAD_DIM), F32)],
        compiler_params=_params(("arbitrary",)),
    )(pr, pr, states, dy, dy, lgv)


FLIP_X, FLIP_Y, FLIP_XY, FLIP_C = (1, 0, 0), (0, 1, 0), (1, 1, 0), (0, 0, 1)
CHIP_FLIPS = ((FLIP_X, 2), (FLIP_Y, 1), (FLIP_XY, 3))


def _flip(me, mask):
    return tuple(1 - v if m else v for v, m in zip(me, mask))


def _comm(name, ins, out_shapes, plan, n_remote, n_local, aliases=None):
    n_in, n_out = len(ins), len(out_shapes)

    def body(*refs):
        in_refs = refs[:n_in]
        out_refs = refs[n_in:n_in + n_out]
        send_sems, recv_sems, local_sems = refs[n_in + n_out:]
        me = (lax.axis_index("x"), lax.axis_index("y"), lax.axis_index("c"))
        local, phases = plan(in_refs, out_refs, me)
        local_copies = [pltpu.make_async_copy(s, d, local_sems.at[i]) for i, (s, d) in enumerate(local)]
        for cp in local_copies:
            cp.start()
        sent = []
        kk = 0
        for phase in phases:
            arrivals = []
            for mask, src, dst, landing in phase:
                peer = _flip(me, mask)
                cp = pltpu.make_async_remote_copy(src_ref=src, dst_ref=dst, send_sem=send_sems.at[kk],
                                                  recv_sem=recv_sems.at[kk], device_id=peer,
                                                  device_id_type=pl.DeviceIdType.MESH)
                cp.start()
                sent.append(cp)
                arrivals.append(pltpu.make_async_remote_copy(
                    src_ref=landing, dst_ref=landing, send_sem=send_sems.at[kk], recv_sem=recv_sems.at[kk],
                    device_id=peer, device_id_type=pl.DeviceIdType.MESH))
                kk += 1
            for cp in arrivals:
                cp.wait_recv()
        for cp in sent:
            cp.wait_send()
        for cp in local_copies:
            cp.wait()

    any_spec = pl.BlockSpec(memory_space=pl.ANY)
    return pl.pallas_call(
        body, name=name,
        in_specs=[any_spec] * n_in, out_specs=[any_spec] * n_out, out_shape=list(out_shapes),
        scratch_shapes=[pltpu.SemaphoreType.DMA((n_remote,)), pltpu.SemaphoreType.DMA((n_remote,)),
                        pltpu.SemaphoreType.DMA((max(n_local, 1),))],
        input_output_aliases=aliases or {},
    )(*ins)


def _ds(start, size):
    return pl.ds(pl.multiple_of(start * size, 8), size)


def _all_gather8(name, v):
    masks = [(a, b, cc) for a in (0, 1) for b in (0, 1) for cc in (0, 1)][1:]

    def index(p):
        return 4 * p[0] + 2 * p[1] + p[2]

    def plan(in_refs, out_refs, me):
        (src,), (out,) = in_refs, out_refs
        local = [(src, out.at[index(me)])]
        phase = [(m, src, out.at[index(me)], out.at[index(_flip(me, m))]) for m in masks]
        return local, [phase]

    return _comm(name, [v], [jax.ShapeDtypeStruct((N_DEV,) + v.shape, v.dtype)], plan, len(masks), 1)[0]


def _gather_row(name, row):
    n = row.shape[1]
    n_pad = -(-n // (8 * LANES_V7X)) * (8 * LANES_V7X)
    v = jnp.pad(row, ((0, 0), (0, n_pad - n))).reshape(8, n_pad // 8)
    return _all_gather8(name, v).reshape(N_DEV, n_pad)[:, :n]


class _Sharded:
    def __init__(self, kind, rows, cols):
        self.kind, self.rows, self.cols = kind, rows, cols
        self.shard_shape = (rows, cols // N_CHIP) if kind == "col" else (rows // N_CHIP, cols)
        self.half_shape = (rows // 2, cols) if kind == "col" else (rows, cols // 2)
        self.piece_shape = (rows // 2, cols // N_CHIP) if kind == "col" else (rows // N_CHIP, cols // 2)

    def shard_of_full(self, ref, s):
        if self.kind == "col":
            return ref.at[:, _ds(s, self.cols // N_CHIP)]
        return ref.at[_ds(s, self.rows // N_CHIP), :]

    def half_of_full(self, ref, h):
        if self.kind == "col":
            return ref.at[_ds(h, self.rows // 2), :]
        return ref.at[:, _ds(h, self.cols // 2)]

    def piece_of_full(self, ref, s, h):
        if self.kind == "col":
            return ref.at[_ds(h, self.rows // 2), _ds(s, self.cols // N_CHIP)]
        return ref.at[_ds(s, self.rows // N_CHIP), _ds(h, self.cols // 2)]

    def half_of_shard(self, ref, h):
        if self.kind == "col":
            return ref.at[_ds(h, self.rows // 2), :]
        return ref.at[:, _ds(h, self.cols // 2)]

    def shard_of_half(self, ref, s):
        if self.kind == "col":
            return ref.at[:, _ds(s, self.cols // N_CHIP)]
        return ref.at[_ds(s, self.rows // N_CHIP), :]


def _place_shard(meta, w, s_arr):
    r, cols = w.shape
    tr = _tile(r, 256, 16)
    nr = r // tr

    def body(s_ref, w_ref, o_ref):
        o_ref[...] = w_ref[...].astype(BF16)

    if meta.kind == "col":
        o_map = lambda i, s_ref: (i, s_ref[0])
    else:
        o_map = lambda i, s_ref: (i + s_ref[0] * nr, 0)
    return pl.pallas_call(
        body, name="place_shard",
        grid_spec=pltpu.PrefetchScalarGridSpec(
            num_scalar_prefetch=1, grid=(nr,),
            in_specs=[pl.BlockSpec((tr, cols), lambda i, s_ref: (i, 0))],
            out_specs=pl.BlockSpec((tr, cols), o_map)),
        out_shape=jax.ShapeDtypeStruct((meta.rows, meta.cols), BF16),
        compiler_params=_params(("parallel",)),
    )(s_arr, w)


def _gather_copies(metas, over_ici):
    def copies(fulls, me):
        x, y, c = me
        s_me = 2 * x + y
        out = []
        for meta, full in zip(metas, fulls):
            for mask, bits in CHIP_FLIPS:
                s_peer = jnp.bitwise_xor(s_me, bits)
                if over_ici:
                    out.append((mask, meta.piece_of_full(full, s_me, c), meta.piece_of_full(full, s_me, c),
                                meta.piece_of_full(full, s_peer, c)))
                else:
                    out.append((FLIP_C, meta.piece_of_full(full, s_peer, c), meta.piece_of_full(full, s_peer, c),
                                meta.piece_of_full(full, s_peer, 1 - c)))
        return out
    return copies


def _gather_forward(name, metas, fulls):
    nt = len(metas)
    copies = _gather_copies(metas, False)
    outs = [jax.ShapeDtypeStruct((m.rows, m.cols), BF16) for m in metas]
    return _comm(name, list(fulls), outs, lambda ins, outs_, me: ([], [copies(outs_, me)]), 3 * nt, 0,
                 aliases={i: i for i in range(nt)})


HBM_SPEC = pl.BlockSpec(memory_space=pltpu.HBM)
SEM_SPEC = pl.BlockSpec(memory_space=pltpu.SEMAPHORE)
SPLIT_EFFECT = pltpu.SideEffectType.DATAFLOW_SIDE_EFFECTING


def _split_start(name, bufs, groups):
    nb, ng = len(bufs), len(groups)

    def body(*refs):
        buf_refs = refs[:nb]
        sem_refs = refs[nb:nb + 2 * ng]
        token = refs[-1]
        me = (lax.axis_index("x"), lax.axis_index("y"), lax.axis_index("c"))
        for gi, (lo, n_bufs, copies, _) in enumerate(groups):
            for kk, (mask, src, dst, _) in enumerate(copies(buf_refs[lo:lo + n_bufs], me)):
                pltpu.make_async_remote_copy(src_ref=src, dst_ref=dst, send_sem=sem_refs[2 * gi].at[kk],
                                             recv_sem=sem_refs[2 * gi + 1].at[kk], device_id=_flip(me, mask),
                                             device_id_type=pl.DeviceIdType.MESH).start()
        token[...] = jnp.zeros_like(token)

    out_shape = []
    for _, _, _, n in groups:
        out_shape += [pltpu.SemaphoreType.DMA((n,)), pltpu.SemaphoreType.DMA((n,))]
    out_shape += [pltpu.HBM(b.shape, b.dtype) for b in bufs] + [jax.ShapeDtypeStruct((8, LANES_V7X), F32)]
    res = pl.pallas_call(
        body, name=name, out_shape=tuple(out_shape),
        in_specs=(HBM_SPEC,) * nb,
        out_specs=(SEM_SPEC,) * (2 * ng) + (HBM_SPEC,) * nb + (pl.BlockSpec(memory_space=pltpu.VMEM),),
        input_output_aliases={i: 2 * ng + i for i in range(nb)},
        compiler_params=pltpu.CompilerParams(has_side_effects=SPLIT_EFFECT),
    )(*[pltpu.with_memory_space_constraint(b, pltpu.HBM) for b in bufs])
    sems = [(res[2 * gi], res[2 * gi + 1]) for gi in range(ng)]
    return sems, list(res[2 * ng:2 * ng + nb]), res[-1]


def _split_wait(name, sems, bufs, copies, after):
    nb = len(bufs)

    def body(*refs):
        buf_refs = refs[:nb]
        send_sems, recv_sems = refs[nb], refs[nb + 1]
        me = (lax.axis_index("x"), lax.axis_index("y"), lax.axis_index("c"))
        for kk, (mask, _, _, landing) in enumerate(copies(buf_refs, me)):
            cp = pltpu.make_async_remote_copy(src_ref=landing, dst_ref=landing, send_sem=send_sems.at[kk],
                                              recv_sem=recv_sems.at[kk], device_id=_flip(me, mask),
                                              device_id_type=pl.DeviceIdType.MESH)
            cp.wait_send()
            cp.wait_recv()

    return list(pl.pallas_call(
        body, name=name, out_shape=tuple(pltpu.HBM(b.shape, b.dtype) for b in bufs),
        in_specs=(HBM_SPEC,) * nb + (SEM_SPEC, SEM_SPEC, pl.BlockSpec(memory_space=pl.ANY)),
        out_specs=(HBM_SPEC,) * nb,
        input_output_aliases={i: i for i in range(nb)},
        compiler_params=pltpu.CompilerParams(has_side_effects=SPLIT_EFFECT),
    )(*bufs, sems[0], sems[1], after))


def _reduce_pair(metas, grads):
    def plan(in_refs, out_refs, me):
        c = me[2]
        phase = [(FLIP_C, m.half_of_full(g, 1 - c), land, land) for m, g, land in zip(metas, in_refs, out_refs)]
        return [], [phase]

    outs = [jax.ShapeDtypeStruct(m.half_shape, BF16) for m in metas]
    return _comm("reduce_pair", list(grads), outs, plan, len(metas), 0)


def _reduce_chip_copies(metas):
    def copies(refs, me):
        x, y, _ = me
        s_me = 2 * x + y
        out = []
        for m, p, land in zip(metas, refs[:len(metas)], refs[len(metas):]):
            for kk, (mask, bits) in enumerate(CHIP_FLIPS):
                s_peer = jnp.bitwise_xor(s_me, bits)
                out.append((mask, m.shard_of_half(p, s_peer), land.at[kk], land.at[kk]))
        return out
    return copies


def _reduce_chips(metas, halves):
    copies = _reduce_chip_copies(metas)
    outs = [jax.ShapeDtypeStruct((3,) + m.piece_shape, BF16) for m in metas]
    return _comm("reduce_chips", list(halves), outs,
                 lambda ins, outs_, me: ([], [copies(list(ins) + list(outs_), me)]), 3 * len(metas), 0)


def _reduce_chips_start(name, metas, halves):
    lands = [lax.empty((3,) + m.piece_shape, BF16) for m in metas]
    bufs = list(halves) + lands
    sems, thru, token = _split_start(name, bufs, [(0, len(bufs), _reduce_chip_copies(metas), 3 * len(metas))])
    return sems[0], thru, token


def _reduce_chips_wait(name, metas, sems, thru, after):
    done = _split_wait(name, sems, thru, _reduce_chip_copies(metas), after)
    return done[:len(metas)], done[len(metas):]


def _share_halves(metas, shards):
    def plan(in_refs, out_refs, me):
        c = me[2]
        phase = [(FLIP_C, m.half_of_shard(g, c), m.half_of_shard(g, c), m.half_of_shard(g, 1 - c))
                 for m, g in zip(metas, out_refs)]
        return [], [phase]

    outs = [jax.ShapeDtypeStruct(m.shard_shape, F32) for m in metas]
    return _comm("share_halves", list(shards), outs, plan, len(metas), 0,
                 aliases={i: i for i in range(len(metas))})


def _pair_sum(meta, grad, landed, c_arr):
    hr, hc = meta.half_shape
    tr = _tile(hr, 256, 16)
    tc = _tile(hc, 2048)
    nr, ncol = hr // tr, hc // tc

    def body(c_ref, g_ref, l_ref, o_ref):
        o_ref[...] = (g_ref[...].astype(F32) + l_ref[...].astype(F32)).astype(BF16)

    if meta.kind == "col":
        g_map = lambda i, j, c_ref: (i + c_ref[0] * nr, j)
    else:
        g_map = lambda i, j, c_ref: (i, j + c_ref[0] * ncol)
    blk = (tr, tc)
    return pl.pallas_call(
        body, name="pair_sum",
        grid_spec=pltpu.PrefetchScalarGridSpec(
            num_scalar_prefetch=1, grid=(nr, ncol),
            in_specs=[pl.BlockSpec(blk, g_map), pl.BlockSpec(blk, lambda i, j, c_ref: (i, j))],
            out_specs=pl.BlockSpec(blk, lambda i, j, c_ref: (i, j))),
        out_shape=jax.ShapeDtypeStruct((hr, hc), BF16),
        compiler_params=_params(("parallel", "parallel")),
    )(c_arr, grad, landed)


def _sum_pieces(meta, half, landed, s_arr, c_arr):
    pr, pc = meta.piece_shape
    tr = _tile(pr, 256, 16)
    tc = _tile(pc, 2048)
    nr, ncol = pr // tr, pc // tc

    def body(s_ref, c_ref, p_ref, l_ref, o_ref):
        acc = p_ref[...].astype(F32)
        for kk in range(3):
            acc = acc + l_ref[kk].astype(F32)
        o_ref[...] = acc

    if meta.kind == "col":
        p_map = lambda i, j, s_ref, c_ref: (i, j + s_ref[0] * ncol)
        o_map = lambda i, j, s_ref, c_ref: (i + c_ref[0] * nr, j)
    else:
        p_map = lambda i, j, s_ref, c_ref: (i + s_ref[0] * nr, j)
        o_map = lambda i, j, s_ref, c_ref: (i, j + c_ref[0] * ncol)
    blk = (tr, tc)
    return pl.pallas_call(
        body, name="sum_pieces",
        grid_spec=pltpu.PrefetchScalarGridSpec(
            num_scalar_prefetch=2, grid=(nr, ncol),
            in_specs=[pl.BlockSpec(blk, p_map), pl.BlockSpec((3,) + blk, lambda i, j, s_ref, c_ref: (0, i, j))],
            out_specs=pl.BlockSpec(blk, o_map)),
        out_shape=jax.ShapeDtypeStruct(meta.shard_shape, F32),
        compiler_params=_params(("parallel", "parallel")),
    )(s_arr, c_arr, half, landed)


def _adam_rows(rows, sel, fulls):
    w, g, m, v = rows
    m2 = ADAM_B1 * m + (1.0 - ADAM_B1) * g
    v2 = ADAM_B2 * v + (1.0 - ADAM_B2) * jnp.square(g)
    m_hat = m2 / (1.0 - ADAM_B1 ** ADAM_STEP)
    v_hat = v2 / (1.0 - ADAM_B2 ** ADAM_STEP)
    delta = -ADAM_LR * (m_hat / (jnp.sqrt(v_hat) + ADAM_EPS) + ADAM_WD * w)
    return [delta, m2, v2], []


def _adamw(w, g, m, v):
    r, c = w.shape
    tr = _tile(r, 128, 8)
    outs = _rowwise("adamw", _adam_rows, n_tiles=r // tr, tr=tr,
                    row_ins=[(w, 0, None), (g, 0, None), (m, 0, None), (v, 0, None)],
                    row_outs=[(r, c, F32, 0)] * 3)
    return outs[0], outs[1], outs[2]


def _ada_fwd(cg, w, b):
    d, n = w.shape
    tn = _tile(n, 512)

    def body(c_ref, w_ref, b_ref, o_ref):
        a = _silu(c_ref[...]).astype(BF16)
        o_ref[...] = jnp.dot(a, w_ref[...].astype(BF16), preferred_element_type=F32) + b_ref[...]

    return pl.pallas_call(
        body, name="ada_fwd", grid=(n // tn,),
        in_specs=[pl.BlockSpec(cg.shape, lambda j: (0, 0)), pl.BlockSpec((d, tn), lambda j: (0, j)),
                  pl.BlockSpec((1, tn), lambda j: (0, j))],
        out_specs=pl.BlockSpec((cg.shape[0], tn), lambda j: (0, j)),
        out_shape=jax.ShapeDtypeStruct((cg.shape[0], n), F32),
        compiler_params=_params(("parallel",)),
    )(cg, w, b)


def _ada_bwd(cg, dm, w):
    d, n = w.shape
    tn = _tile(n, 512)
    nj = n // tn

    def body(c_ref, dm_ref, w_ref, gw_ref, da_ref, acc):
        j = pl.program_id(0)

        @pl.when(j == 0)
        def _():
            acc[...] = jnp.zeros_like(acc)

        a = _silu(c_ref[...]).astype(BF16)
        dmv = dm_ref[...].astype(BF16)
        gw_ref[...] = lax.dot_general(a, dmv, TN_DIMS, preferred_element_type=F32)
        acc[...] += lax.dot_general(dmv, w_ref[...].astype(BF16), NT_DIMS, preferred_element_type=F32)

        @pl.when(j == nj - 1)
        def _():
            da_ref[...] = acc[...]

    return pl.pallas_call(
        body, name="ada_bwd", grid=(nj,),
        in_specs=[pl.BlockSpec(cg.shape, lambda j: (0, 0)), pl.BlockSpec((dm.shape[0], tn), lambda j: (0, j)),
                  pl.BlockSpec((d, tn), lambda j: (0, j))],
        out_specs=[pl.BlockSpec((d, tn), lambda j: (0, j)), pl.BlockSpec(cg.shape, lambda j: (0, 0))],
        out_shape=[jax.ShapeDtypeStruct((d, n), F32), jax.ShapeDtypeStruct(cg.shape, F32)],
        scratch_shapes=[pltpu.VMEM(cg.shape, F32)],
        compiler_params=_params(("arbitrary",)),
    )(cg, dm, w)


def _small_reduce(gathered, logits, n_mod_cols, lg_off, loss_off, loss_cols):
    npk = gathered.shape[1]

    def body(g_ref, lo_ref, tot_ref, gb_ref, gl_ref, loss_ref):
        acc = g_ref[0:1, :]
        for dd in range(1, N_DEV):
            acc = acc + g_ref[dd:dd + 1, :]
        tot_ref[...] = acc
        gb_ref[...] = acc[:, :n_mod_cols] + acc[:, n_mod_cols:2 * n_mod_cols]
        gl_ref[...] = acc[:, lg_off:lg_off + LANES_V7X] * _sigmoid(-lo_ref[...])
        loss = jnp.sum(acc[:, loss_off:loss_off + loss_cols], axis=1, keepdims=True)
        loss_ref[...] = jnp.broadcast_to(loss, loss_ref.shape)

    lane = jax.ShapeDtypeStruct((1, LANES_V7X), F32)
    return pl.pallas_call(
        body, name="small_reduce",
        out_shape=[jax.ShapeDtypeStruct((1, npk), F32), jax.ShapeDtypeStruct((1, n_mod_cols), F32), lane, lane],
    )(gathered, logits)


def _c_ctx_grad(parts, c_ctx):
    def body(p_ref, c_ref, o_ref):
        tot = p_ref[0:1, :] + p_ref[2:3, :] + p_ref[4:5, :] + p_ref[6:7, :]
        _, vjp = jax.vjp(_silu, c_ref[...])
        o_ref[...] = vjp(tot)[0]

    return pl.pallas_call(body, name="c_ctx_grad", out_shape=jax.ShapeDtypeStruct(c_ctx.shape, F32))(parts, c_ctx)


def _rope_tables(seq, ctx_rows):
    rows = seq // GRID_W
    row = jnp.repeat(jnp.arange(rows, dtype=F32), GRID_W)
    col = jnp.tile(jnp.arange(GRID_W, dtype=F32), rows)
    half = HEAD_DIM // 2
    inv_freq = ROPE_THETA ** (-jnp.arange(0, half, 2, dtype=F32) / half)
    ang = jnp.concatenate([row[:, None] * inv_freq, col[:, None] * inv_freq], axis=-1)
    cos, sin = jnp.cos(ang), jnp.sin(ang)
    cos_full = jnp.repeat(cos, 2, axis=1)
    sin_signed = jnp.stack([-sin, sin], axis=-1).reshape(seq, HEAD_DIM)
    cos_full = jnp.concatenate([jnp.ones((ctx_rows, HEAD_DIM), F32), cos_full], axis=0)
    sin_signed = jnp.concatenate([jnp.zeros((ctx_rows, HEAD_DIM), F32), sin_signed], axis=0)
    return cos_full, sin_signed


def _qk_rot(p, gain, cos_full, sin_signed):
    r = _rmsn(p) * gain
    return r * cos_full + _swap_pairs(r) * sin_signed


def _qk_rot_bwd(g, p, gain, cos_full, sin_signed):
    g1 = g * cos_full + _swap_pairs(g * sin_signed)
    _, vjp = jax.vjp(lambda pp, gn: _rmsn(pp) * gn, p, gain)
    return vjp(g1)


def kernel(x, c, ctx, c_ctx, w_ada, b_ada, ffn1_w_in, ffn1_w_out, mix_w_in, attn_q_gain, attn_k_gain, ret_decay_logit, w_proj_attn, w_proj_ret, mix_w_out, ffn2_w_in, ffn2_w_out, final_norm, loss_target, m_c_ctx, m_w_ada, m_b_ada, m_ffn1_w_in, m_ffn1_w_out, m_mix_w_in, m_attn_q_gain, m_attn_k_gain, m_ret_decay_logit, m_w_proj_attn, m_w_proj_ret, m_mix_w_out, m_ffn2_w_in, m_ffn2_w_out, m_final_norm, v_c_ctx, v_w_ada, v_b_ada, v_ffn1_w_in, v_ffn1_w_out, v_mix_w_in, v_attn_q_gain, v_attn_k_gain, v_ret_decay_logit, v_w_proj_attn, v_w_proj_ret, v_mix_w_out, v_ffn2_w_in, v_ffn2_w_out, v_final_norm):
    xi, yi, ci = lax.axis_index("x"), lax.axis_index("y"), lax.axis_index("c")
    dev = 4 * xi + 2 * yi + ci
    s_me = 2 * xi + yi
    c_arr = jnp.reshape(ci, (1,)).astype(jnp.int32)
    s_arr = jnp.reshape(s_me, (1,)).astype(jnp.int32)

    t, d = x.shape[1], x.shape[2]
    tc = ctx.shape[1]
    tk = tc + t
    ff = ffn1_w_out.shape[1] * N_CHIP
    aw = w_proj_attn.shape[1]
    rw = w_proj_ret.shape[1]
    pw = mix_w_in.shape[2] * N_CHIP
    kvw = (pw - aw - 4 * rw - 2 * d) // 2
    groups = aw // kvw
    n_ret_heads = rw // HEAD_DIM
    mod_cols = N_MOD * d
    tr = _tile(tc, 256, 32)
    nt_all, nt_x, ctx_tiles = tk // tr, t // tr, tc // tr

    big = [("col", ffn1_w_in), ("row", ffn1_w_out), ("col", mix_w_in), ("col", w_proj_attn), ("col", w_proj_ret),
           ("row", mix_w_out), ("col", ffn2_w_in), ("row", ffn2_w_out)]
    metas = []
    for kind, w in big:
        r_l, c_l = w.shape[1:]
        metas.append(_Sharded(kind, r_l, c_l * N_CHIP) if kind == "col" else _Sharded(kind, r_l * N_CHIP, c_l))
    placed = [_place_shard(m, w[0], s_arr) for m, (_, w) in zip(metas, big)]
    layer_groups = ((0, 2), (2, 6), (6, 8))
    gather_sems, placed, gather_token = _split_start(
        "gather_start", placed,
        [(lo, hi - lo, _gather_copies(metas[lo:hi], True), 3 * (hi - lo)) for lo, hi in layer_groups])
    c = c + gather_token[0, 0]

    def weights_of(gi, after):
        lo, hi = layer_groups[gi]
        arrived = _split_wait("gather_wait_%d" % gi, gather_sems[gi], placed[lo:hi],
                              _gather_copies(metas[lo:hi], True), after)
        return _gather_forward("gather_forward_%d" % gi, metas[lo:hi], arrived)

    c_rows = _gather_row("gather_c", c)
    cg = jnp.concatenate([c_rows, c_ctx[None, :], jnp.zeros((7, d), F32)], axis=0)
    w_ada_l = w_ada[0]
    ada_cols = w_ada_l.shape[1]
    b_ada_l = lax.dynamic_slice_in_dim(b_ada, s_me * ada_cols, ada_cols, axis=1)
    mod_shard = _ada_fwd(cg, w_ada_l, b_ada_l)
    mod_g = _all_gather8("gather_mod", mod_shard)
    mod_full = jnp.concatenate([mod_g[0], mod_g[2], mod_g[4], mod_g[6]], axis=1)
    mod_x = lax.dynamic_slice_in_dim(mod_full, dev, 1, axis=0).reshape(N_MOD, d)
    mod_c = mod_full[8].reshape(N_MOD, d)
    mods = jnp.stack([mod_c, mod_x])

    cos_full, sin_signed = _rope_tables(t, tc)
    q_gain, k_gain = attn_q_gain, attn_k_gain
    log_gamma = jax.nn.log_sigmoid(ret_decay_logit[0])
    lgv = jnp.broadcast_to(log_gamma[:, :, None, None], (2, n_ret_heads, 1, HEAD_DIM))

    def norm_mod(name, h, n_tiles, off, i_shift, i_scale):
        def fn(rows, sel, fulls):
            return [_rmsn(rows[0]) * (1.0 + sel(i_scale)) + sel(i_shift)], []
        return _rowwise(name, fn, n_tiles=n_tiles, tr=tr, row_ins=[(h, 0, None)],
                        row_outs=[(h.shape[0], d, BF16, 0)], sel_in=mods, sel_off=off, ctx_rows=tc)[0]

    def resid(name, h, h_off, f, n_tiles, off, i_gate, coef):
        def fn(rows, sel, fulls):
            return [rows[0] + coef * sel(i_gate) * rows[1]], []
        return _rowwise(name, fn, n_tiles=n_tiles, tr=tr, row_ins=[(h, h_off, None), (f, 0, None)],
                        row_outs=[(f.shape[0], d, F32, 0)], sel_in=mods, sel_off=off, ctx_rows=tc)[0]

    h0 = jnp.concatenate([ctx[0], x[0]], axis=0)
    n1 = norm_mod("norm_mod1", h0, nt_all, 0, 0, 1)
    w1i, w1o = weights_of(0, n1)
    hm1, ua1, ub1 = _mm_swiglu("ffn1_in", n1, w1i)
    f1 = _mm("ffn1_out", hm1, w1o, "nn", F32)
    h1 = resid("resid1", h0, 0, f1, nt_all, 0, 2, 0.5)

    n2 = norm_mod("norm_mod2", h1, nt_all, 0, 3, 4)
    wmi, wpa, wpr, wmo = weights_of(1, n2)
    p_q = _mm("mix_in_q", n2, wmi, "nn", F32, 0, aw)
    p_kv = _mm("mix_in_kv", n2, wmi, "nn", F32, aw, 2 * kvw)
    p_r = _mm("mix_in_ret", n2, wmi, "nn", F32, aw + 2 * kvw, 3 * rw)
    p_gr = _mm("mix_in_gr", n2, wmi, "nn", F32, aw + 2 * kvw + 3 * rw, rw)
    p_gab = _mm("mix_in_gab", n2, wmi, "nn", F32, aw + 2 * kvw + 4 * rw, 2 * d)

    def q_prep(rows, sel, fulls):
        p, cf, ss = rows
        return _heads_map(lambda ph: [_qk_rot(ph, fulls[0], cf, ss) * QSCALE], [p], aw), []

    q_rot = _rowwise("q_prep", q_prep, n_tiles=nt_x, tr=tr,
                     row_ins=[(p_q, ctx_tiles, None), (cos_full, ctx_tiles, None), (sin_signed, ctx_tiles, None)],
                     row_outs=[(t, aw, BF16, 0)], full_ins=[q_gain])[0]

    def kv_prep(rows, sel, fulls):
        p, cf, ss = rows
        k_rot = _heads_map(lambda ph: [_qk_rot(ph, fulls[0], cf, ss)], [p[:, :kvw]], kvw)[0]
        v_ones = _heads_map(lambda vh: [jnp.concatenate([vh, jnp.ones_like(vh)], axis=1)], [p[:, kvw:]], kvw)[0]
        return [k_rot, v_ones], []

    k_rot, v_att = _rowwise("kv_prep", kv_prep, n_tiles=nt_all, tr=tr,
                            row_ins=[(p_kv, 0, None), (cos_full, 0, None), (sin_signed, 0, None)],
                            row_outs=[(tk, kvw, BF16, 0), (tk, 2 * kvw, BF16, 0)], full_ins=[k_gain])

    ya, lse = _flash_fwd(q_rot, k_rot, v_att, groups)
    y_fwd, y_bwd, states = _ret_fwd(p_r, lgv, tc)

    def ret_out_fn(yf, yb, gr):
        return [_silu(gr) * _rmsn(yf + yb)]

    def ret_out(rows, sel, fulls):
        return _heads_map(ret_out_fn, rows, rw), []

    y_rows = [(y_fwd, ctx_tiles, None), (y_bwd, ctx_tiles, None), (p_gr, ctx_tiles, None)]
    yr = _rowwise("ret_out", ret_out, n_tiles=nt_x, tr=tr, row_ins=y_rows, row_outs=[(t, rw, BF16, 0)])[0]

    pa = _mm("proj_attn", ya, wpa, "nn", F32)
    prj = _mm("proj_ret", yr, wpr, "nn", F32)

    def merge_fn(a, r, ga, gb):
        return _sigmoid(ga) * a + _sigmoid(gb) * r

    gate_rows = [(p_gab, ctx_tiles, (d, 0)), (p_gab, ctx_tiles, (d, 1))]
    z = _rowwise("merge", lambda rows, sel, fulls: ([merge_fn(*rows)], []), n_tiles=nt_x, tr=tr,
                 row_ins=[(pa, 0, None), (prj, 0, None)] + gate_rows, row_outs=[(t, d, BF16, 0)])[0]
    fo = _mm("mix_out", z, wmo, "nn", F32)
    h2 = resid("resid2", h1, ctx_tiles, fo, nt_x, ctx_tiles, 5, 1.0)

    n3 = norm_mod("norm_mod3", h2, nt_x, ctx_tiles, 6, 7)
    w2i, w2o = weights_of(2, n3)
    hm2, ua2, ub2 = _mm_swiglu("ffn2_in", n3, w2i)
    f2 = _mm("ffn2_out", hm2, w2o, "nn", F32)
    h3 = resid("resid3", h2, 0, f2, nt_x, ctx_tiles, 8, 0.5)

    def loss_fn(rows, sel, fulls):
        h, tgt = rows
        y, vjp = jax.vjp(lambda hh, ww: _rmsn(hh) * ww, h, fulls[0])
        err = y - tgt
        dh, dw = vjp(err / d)
        return [dh], [0.5 / d * jnp.sum(err * err, axis=0, keepdims=True), dw]

    dh3, loss_acc = _rowwise("loss_head", loss_fn, n_tiles=nt_x, tr=tr,
                             row_ins=[(h3, 0, None), (loss_target[0], 0, None)], row_outs=[(t, d, F32, 0)],
                             full_ins=[final_norm[None, :]], acc_shape=(8, d), sel_off=ctx_tiles, ctx_rows=tc)
    loss_cols, g_final = loss_acc[1, 0:1], loss_acc[1, 1:2]

    def gate_bwd(name, dh, f, n_tiles, off, i_gate, coef):
        def fn(rows, sel, fulls):
            dhh, fv = rows
            return [coef * sel(i_gate) * dhh], [jnp.sum(coef * dhh * fv, axis=0, keepdims=True)]
        return _rowwise(name, fn, n_tiles=n_tiles, tr=tr, row_ins=[(dh, 0, None), (f, 0, None)],
                        row_outs=[(dh.shape[0], d, BF16, 0)], sel_in=mods, sel_off=off, ctx_rows=tc,
                        acc_shape=(8, d))

    def swiglu_bwd(name, dhm, ua, ub):
        rows_n = dhm.shape[0]
        tr_w = _tile(tr, 128, 32)

        def fn(rows, sel, fulls):
            g, a, b = rows
            _, vjp = jax.vjp(lambda aa, bb: _silu(aa) * bb, a.astype(F32), b.astype(F32))
            da, db = vjp(g)
            return [jnp.concatenate([da, db], axis=1)], []
        return _rowwise(name, fn, n_tiles=rows_n // tr_w, tr=tr_w,
                        row_ins=[(dhm, 0, None), (ua, 0, None), (ub, 0, None)],
                        row_outs=[(rows_n, 2 * ff, BF16, 0)])[0]

    def norm_mod_bwd(name, dn, h, dres, n_tiles, off, i_shift, i_scale):
        def fn(rows, sel, fulls):
            g, hh, dr = rows
            _, vjp = jax.vjp(lambda a, sh, sc: _rmsn(a) * (1.0 + sc) + sh, hh,
                             sel(i_shift), sel(i_scale))
            dh, dsh, dsc = vjp(g)
            return [dr + dh], [dsh, dsc]
        return _rowwise(name, fn, n_tiles=n_tiles, tr=tr, row_ins=[(dn, 0, None), (h, 0, None), (dres, 0, None)],
                        row_outs=[(dn.shape[0], d, F32, 0)], sel_in=mods, sel_off=off, ctx_rows=tc,
                        acc_shape=(8, d))

    df2, acc_g3 = gate_bwd("gate_bwd3", dh3, f2, nt_x, ctx_tiles, 8, 0.5)
    g_w2o = _mm("ffn2_out_dw", hm2, df2, "tn", BF16)
    dhm2 = _mm("ffn2_out_dx", df2, w2o, "nt", F32)
    du2 = swiglu_bwd("swiglu_bwd2", dhm2, ua2, ub2)
    g_w2i = _mm("ffn2_in_dw", n3, du2, "tn", BF16)
    dn3 = _mm("ffn2_in_dx", du2, w2i, "nt", F32)
    dh2, acc_n3 = norm_mod_bwd("norm_mod_bwd3", dn3, h2, dh3, nt_x, ctx_tiles, 6, 7)

    def chip_halves(lo, hi, grads_l):
        landed_l = _reduce_pair(metas[lo:hi], grads_l)
        return [_pair_sum(m, g, l, c_arr) for m, g, l in zip(metas[lo:hi], grads_l, landed_l)]

    sems_ffn2, thru_ffn2, token = _reduce_chips_start("reduce_start_ffn2", metas[6:8],
                                                      chip_halves(6, 8, [g_w2i, g_w2o]))
    mods = mods + token[0, 0]

    dfo, acc_g2 = gate_bwd("gate_bwd2", dh2, fo, nt_x, ctx_tiles, 5, 1.0)
    g_wmo = _mm("mix_out_dw", z, dfo, "tn", BF16)
    dz = _mm("mix_out_dx", dfo, wmo, "nt", F32)

    def merge_bwd(rows, sel, fulls):
        g, a, r, ga, gb = rows
        _, vjp = jax.vjp(merge_fn, a, r, ga, gb)
        da, dr, dga, dgb = vjp(g)
        return [da, dr, jnp.concatenate([dga, dgb], axis=1)], []

    dpa, dpr, dgab = _rowwise("merge_bwd", merge_bwd, n_tiles=nt_x, tr=tr,
                              row_ins=[(dz, 0, None), (pa, 0, None), (prj, 0, None)] + gate_rows,
                              row_outs=[(t, d, BF16, 0), (t, d, BF16, 0), (t, 2 * d, BF16, 0)])
    g_wpa = _mm("proj_attn_dw", ya, dpa, "tn", BF16)
    dya = _mm("proj_attn_dx", dpa, wpa, "nt", BF16)
    g_wpr = _mm("proj_ret_dw", yr, dpr, "tn", BF16)
    dyr = _mm("proj_ret_dx", dpr, wpr, "nt", F32)

    def ret_out_bwd(rows, sel, fulls):
        def per_head(g, yf, yb, gr):
            _, vjp = jax.vjp(lambda yy, gg: ret_out_fn(yy, 0.0, gg)[0], yf + yb, gr)
            return list(vjp(g))
        dy, dgr = _heads_map(per_head, rows, rw)
        return [dy, dgr], []

    dy_ret, dgr = _rowwise("ret_out_bwd", ret_out_bwd, n_tiles=nt_x, tr=tr, row_ins=[(dyr, 0, None)] + y_rows,
                           row_outs=[(t, rw, F32, 0), (t, rw, BF16, 0)])
    dy_all = jnp.concatenate([jnp.zeros((tc, rw), F32), dy_ret], axis=0)
    dp_rf, dp_rb, dlg = _ret_bwd(p_r, states, dy_all, lgv, tc)
    dp_r = _rowwise("ret_bwd_sum", lambda rows, sel, fulls: ([rows[0] + rows[1]], []), n_tiles=nt_all, tr=tr,
                    row_ins=[(dp_rf, 0, None), (dp_rb, 0, None)], row_outs=[(tk, 3 * rw, BF16, 0)])[0]

    dq_rot, dk_rot, dv_att = _flash_bwd(q_rot, k_rot, v_att, ya, dya, lse, groups)

    def q_prep_bwd(rows, sel, fulls):
        g, p, cf, ss = rows
        gain_acc = []

        def per_head(gh, ph):
            dp, dgain = _qk_rot_bwd(gh * HEAD_DIM ** -0.5, ph, fulls[0], cf, ss)
            gain_acc.append(dgain)
            return [dp]
        dp = _heads_map(per_head, [g, p], aw)[0]
        return [dp], [functools.reduce(lambda a, b: a + b, gain_acc)]

    dp_q, acc_gq = _rowwise("q_prep_bwd", q_prep_bwd, n_tiles=nt_x, tr=tr,
                            row_ins=[(dq_rot, 0, None), (p_q, ctx_tiles, None), (cos_full, ctx_tiles, None),
                                     (sin_signed, ctx_tiles, None)],
                            row_outs=[(t, aw, BF16, 0)], full_ins=[q_gain], acc_shape=(8, HEAD_DIM),
                            sel_off=ctx_tiles, ctx_rows=tc)

    def kv_prep_bwd(rows, sel, fulls):
        gk, gv, p, cf, ss = rows
        gain_acc = []

        def per_head(gh, ph):
            dp, dgain = _qk_rot_bwd(gh, ph, fulls[0], cf, ss)
            gain_acc.append(dgain)
            return [dp]
        dpk = _heads_map(per_head, [gk, p], kvw)[0]
        return [jnp.concatenate([dpk, gv], axis=1)], [functools.reduce(lambda a, b: a + b, gain_acc)]

    dp_kv, acc_gk = _rowwise("kv_prep_bwd", kv_prep_bwd, n_tiles=nt_all, tr=tr,
                             row_ins=[(dk_rot, 0, None), (dv_att, 0, None), (p_kv, 0, (kvw, 0)), (cos_full, 0, None),
                                      (sin_signed, 0, None)],
                             row_outs=[(tk, 2 * kvw, BF16, 0)], full_ins=[k_gain], acc_shape=(8, HEAD_DIM),
                             sel_off=0, ctx_rows=tc)

    def with_ctx_zeros(a):
        return jnp.concatenate([jnp.zeros((tc, a.shape[1]), a.dtype), a], axis=0)

    dp = jnp.concatenate([with_ctx_zeros(dp_q), dp_kv, dp_r, with_ctx_zeros(dgr), with_ctx_zeros(dgab)], axis=1)
    g_wmi = _mm("mix_in_dw", n2, dp, "tn", BF16)
    dn2 = _mm("mix_in_dx", dp, wmi, "nt", F32)
    dh1, acc_n2 = norm_mod_bwd("norm_mod_bwd2", dn2, h1, with_ctx_zeros(dh2), nt_all, 0, 3, 4)
    sems_mix, thru_mix, token = _reduce_chips_start("reduce_start_mix", metas[2:6],
                                                    chip_halves(2, 6, [g_wmi, g_wpa, g_wpr, g_wmo]))
    mods = mods + token[0, 0]

    df1, acc_g1 = gate_bwd("gate_bwd1", dh1, f1, nt_all, 0, 2, 0.5)
    g_w1o = _mm("ffn1_out_dw", hm1, df1, "tn", BF16)
    dhm1 = _mm("ffn1_out_dx", df1, w1o, "nt", F32)
    du1 = swiglu_bwd("swiglu_bwd1", dhm1, ua1, ub1)
    g_w1i = _mm("ffn1_in_dw", n1, du1, "tn", BF16)
    dn1 = _mm("ffn1_in_dx", du1, w1i, "nt", F32)
    dh0, acc_n1 = norm_mod_bwd("norm_mod_bwd1", dn1, h0, dh1, nt_all, 0, 0, 1)
    grad_x = dh0[tc:][None]

    halves_ffn1 = chip_halves(0, 2, [g_w1i, g_w1o])
    landed_ffn1 = _reduce_chips(metas[0:2], halves_ffn1)
    halves_mix, landed_mix = _reduce_chips_wait("reduce_wait_mix", metas[2:6], sems_mix, thru_mix, dh0)
    halves_ffn2, landed_ffn2 = _reduce_chips_wait("reduce_wait_ffn2", metas[6:8], sems_ffn2, thru_ffn2, dh0)
    halves = halves_ffn1 + halves_mix + halves_ffn2
    landed3 = list(landed_ffn1) + landed_mix + landed_ffn2
    pieces = [_sum_pieces(m, p, l, s_arr, c_arr) for m, p, l in zip(metas, halves, landed3)]
    grads_big = _share_halves(metas, pieces)

    zero_row = jnp.zeros((1, d), F32)
    dmod_x = jnp.concatenate([acc_n1[1, 0:1], acc_n1[1, 1:2], acc_g1[1, 0:1], acc_n2[1, 0:1], acc_n2[1, 1:2],
                              acc_g2[1, 0:1], acc_n3[1, 0:1], acc_n3[1, 1:2], acc_g3[1, 0:1]], axis=1)
    dmod_c = jnp.concatenate([acc_n1[0, 0:1], acc_n1[0, 1:2], acc_g1[0, 0:1], acc_n2[0, 0:1], acc_n2[0, 1:2]]
                             + [zero_row] * 4, axis=1)
    dlg_row = jnp.pad(dlg[:, :, 0, 0].reshape(1, 2 * n_ret_heads), ((0, 0), (0, LANES_V7X - 2 * n_ret_heads)))
    packed = jnp.concatenate([dmod_x, dmod_c, acc_gq[1, 0:1], acc_gk[0, 0:1] + acc_gk[1, 0:1], dlg_row,
                              g_final, loss_cols], axis=1)
    off_gq = 2 * mod_cols
    off_gk = off_gq + LANES_V7X
    off_lg = off_gk + LANES_V7X
    off_fn = off_lg + LANES_V7X
    off_loss = off_fn + d
    gathered = _gather_row("gather_small", packed)
    logits_row = jnp.pad(ret_decay_logit.reshape(1, 2 * n_ret_heads), ((0, 0), (0, LANES_V7X - 2 * n_ret_heads)))
    totals, g_b_ada, g_decay, loss_row = _small_reduce(gathered, logits_row, mod_cols, off_lg, off_loss, d)
    loss = loss_row[0, 0]

    dm = jnp.concatenate([gathered[:, :mod_cols], totals[:, mod_cols:2 * mod_cols],
                          jnp.zeros((7, mod_cols), F32)], axis=0)
    dm_l = lax.dynamic_slice_in_dim(dm, s_me * ada_cols, ada_cols, axis=1)
    g_w_ada, da_part = _ada_bwd(cg, dm_l, w_ada_l)
    da_rows = _gather_row("gather_dc", da_part[8:9])
    g_c_ctx = _c_ctx_grad(da_rows, c_ctx[None, :])

    def as2d(a):
        return a.reshape(-1, a.shape[-1])

    grads = {
        "c_ctx": g_c_ctx, "w_ada": g_w_ada, "b_ada": g_b_ada,
        "ffn1_w_in": grads_big[0], "ffn1_w_out": grads_big[1], "mix_w_in": grads_big[2],
        "attn_q_gain": totals[:, off_gq:off_gq + HEAD_DIM], "attn_k_gain": totals[:, off_gk:off_gk + HEAD_DIM],
        "ret_decay_logit": g_decay[:, :2 * n_ret_heads],
        "w_proj_attn": grads_big[3], "w_proj_ret": grads_big[4], "mix_w_out": grads_big[5],
        "ffn2_w_in": grads_big[6], "ffn2_w_out": grads_big[7], "final_norm": totals[:, off_fn:off_fn + d],
    }
    weights = {"c_ctx": (c_ctx, m_c_ctx, v_c_ctx), "w_ada": (w_ada, m_w_ada, v_w_ada),
               "b_ada": (b_ada, m_b_ada, v_b_ada), "ffn1_w_in": (ffn1_w_in, m_ffn1_w_in, v_ffn1_w_in),
               "ffn1_w_out": (ffn1_w_out, m_ffn1_w_out, v_ffn1_w_out), "mix_w_in": (mix_w_in, m_mix_w_in, v_mix_w_in),
               "attn_q_gain": (attn_q_gain, m_attn_q_gain, v_attn_q_gain),
               "attn_k_gain": (attn_k_gain, m_attn_k_gain, v_attn_k_gain),
               "ret_decay_logit": (ret_decay_logit, m_ret_decay_logit, v_ret_decay_logit),
               "w_proj_attn": (w_proj_attn, m_w_proj_attn, v_w_proj_attn),
               "w_proj_ret": (w_proj_ret, m_w_proj_ret, v_w_proj_ret), "mix_w_out": (mix_w_out, m_mix_w_out, v_mix_w_out),
               "ffn2_w_in": (ffn2_w_in, m_ffn2_w_in, v_ffn2_w_in), "ffn2_w_out": (ffn2_w_out, m_ffn2_w_out, v_ffn2_w_out),
               "final_norm": (final_norm, m_final_norm, v_final_norm)}
    out_g, out_d, out_m, out_v = [], [], [], []
    for name, (w, m, v) in weights.items():
        shape = w.shape
        if name == "ret_decay_logit":
            w2, m2, v2 = (a.reshape(1, -1) for a in (w, m, v))
        else:
            w2, m2, v2 = as2d(w), as2d(m), as2d(v)
        g2 = grads[name].reshape(w2.shape)
        delta, new_m, new_v = _adamw(w2, g2, m2, v2)
        out_g.append(g2.reshape(shape))
        out_d.append(delta.reshape(shape))
        out_m.append(new_m.reshape(shape))
        out_v.append(new_v.reshape(shape))
    return (loss, grad_x, *out_g, *out_d, *out_m, *out_v)
```

```python
import functools
import math

import jax
import jax.numpy as jnp
from jax import lax
from jax.experimental import pallas as pl
from jax.experimental.pallas import tpu as pltpu

F32 = jnp.float32
BF16 = jnp.bfloat16

HEAD_DIM = 128
GRID_W = 64
ROPE_THETA = 10000.0
NORM_EPS = 1e-6
N_MOD = 9
RET_CHUNK = 128
ADAM_LR = 0.001
ADAM_B1 = 0.9
ADAM_B2 = 0.999
ADAM_EPS = 1e-08
ADAM_WD = 0.01
ADAM_STEP = 10

N_DEV = 8
N_CHIP = 4
LANES_V7X = 128
MXU_WIDTH_V7X = 256
VMEM_LIMIT_V7X = 52 * 1024 * 1024

NT_DIMS = (((1,), (1,)), ((), ()))
TN_DIMS = (((0,), (0,)), ((), ()))
NN_DIMS = (((1,), (0,)), ((), ()))


def _tile(n, pref, mult=LANES_V7X):
    if n <= pref:
        return n
    t = (pref // mult) * mult
    while t >= mult:
        if n % t == 0:
            return t
        t -= mult
    return n


def _params(sem):
    return pltpu.CompilerParams(dimension_semantics=sem, vmem_limit_bytes=VMEM_LIMIT_V7X)


def _sigmoid(x):
    return 1.0 / (1.0 + jnp.exp(-x))


def _silu(x):
    return x * _sigmoid(x)


def _rmsn(x):
    return x * lax.rsqrt(jnp.mean(x * x, axis=-1, keepdims=True) + NORM_EPS)


MM_VMEM_BUDGET = 36 * 1024 * 1024


def _divisor_tiles(n, cap):
    ts = [t for t in range(LANES_V7X, min(n, cap) + 1, LANES_V7X) if n % t == 0]
    return ts or [n]


def _mm_tiles(m, n, tk, out_bytes, has_acc):
    best = None
    for tm in _divisor_tiles(m, 1536):
        for tn in _divisor_tiles(n, 2560):
            need = 4 * tk * (tm + tn) + 2 * tm * tn * out_bytes + 4 * tm * tn
            if need > MM_VMEM_BUDGET:
                continue
            score = tm * tn / (tm + tn)
            for tdim in (tm, tn):
                if tdim % MXU_WIDTH_V7X:
                    score *= 0.85
            if best is None or score > best[0]:
                best = (score, tm, tn)
    return best[1], best[2]


def _mm(name, a, b, mode, out_dtype, b_off=0, n=None, after=None):
    if mode == "nn":
        m, k = a.shape
        n = b.shape[1] if n is None else n
        dims = NN_DIMS
    elif mode == "nt":
        m, k = a.shape
        n = b.shape[0]
        dims = NT_DIMS
    else:
        k, m = a.shape
        n = b.shape[1]
        dims = TN_DIMS
    tk = _tile(k, 2560) if mode != "tn" else _tile(k, 1024)
    nk = k // tk
    tm, tn = _mm_tiles(m, math.gcd(n, b_off) if b_off else n, tk, jnp.dtype(out_dtype).itemsize, nk > 1)
    joff = b_off // tn

    def body(a_ref, b_ref, *rest):
        o_ref = rest[0 if after is None else 1]
        prod = lax.dot_general(a_ref[...], b_ref[...], dims, preferred_element_type=F32)
        if nk == 1:
            o_ref[...] = prod.astype(o_ref.dtype)
            return
        acc_ref = rest[-1]
        kk = pl.program_id(2)

        @pl.when(kk == 0)
        def _():
            acc_ref[...] = jnp.zeros_like(acc_ref)

        acc_ref[...] += prod

        @pl.when(kk == nk - 1)
        def _():
            o_ref[...] = acc_ref[...].astype(o_ref.dtype)

    if mode == "nn":
        a_spec = pl.BlockSpec((tm, tk), lambda i, j, kk: (i, kk))
        b_spec = pl.BlockSpec((tk, tn), lambda i, j, kk: (kk, j + joff))
    elif mode == "nt":
        a_spec = pl.BlockSpec((tm, tk), lambda i, j, kk: (i, kk))
        b_spec = pl.BlockSpec((tn, tk), lambda i, j, kk: (j, kk))
    else:
        a_spec = pl.BlockSpec((tk, tm), lambda i, j, kk: (kk, i))
        b_spec = pl.BlockSpec((tk, tn), lambda i, j, kk: (kk, j))
    return pl.pallas_call(
        body, name=name, grid=(m // tm, n // tn, nk),
        in_specs=[a_spec, b_spec] + ([] if after is None else [pl.BlockSpec(memory_space=pl.ANY)]),
        out_specs=pl.BlockSpec((tm, tn), lambda i, j, kk: (i, j)),
        out_shape=jax.ShapeDtypeStruct((m, n), out_dtype),
        scratch_shapes=[pltpu.VMEM((tm, tn), F32)] if nk > 1 else [],
        compiler_params=_params(("parallel", "parallel", "arbitrary")),
    )(*((a, b) if after is None else (a, b, after)))


def _mm_swiglu(name, a, w):
    m, k = a.shape
    f = w.shape[1] // 2
    tm = _tile(m, 1024)
    tn = _tile(f, 512)
    tk = _tile(k, 2560)
    nk = k // tk
    jf = f // tn

    def body(a_ref, wa_ref, wb_ref, h_ref, ua_ref, ub_ref, acca, accb):
        kk = pl.program_id(2)

        @pl.when(kk == 0)
        def _():
            acca[...] = jnp.zeros_like(acca)
            accb[...] = jnp.zeros_like(accb)

        av = a_ref[...]
        acca[...] += jnp.dot(av, wa_ref[...], preferred_element_type=F32)
        accb[...] += jnp.dot(av, wb_ref[...], preferred_element_type=F32)

        @pl.when(kk == nk - 1)
        def _():
            ua = acca[...]
            ub = accb[...]
            h_ref[...] = (_silu(ua) * ub).astype(BF16)
            ua_ref[...] = ua.astype(BF16)
            ub_ref[...] = ub.astype(BF16)

    o_spec = pl.BlockSpec((tm, tn), lambda i, j, kk: (i, j))
    o_shape = jax.ShapeDtypeStruct((m, f), BF16)
    return pl.pallas_call(
        body, name=name, grid=(m // tm, jf, nk),
        in_specs=[pl.BlockSpec((tm, tk), lambda i, j, kk: (i, kk)),
                  pl.BlockSpec((tk, tn), lambda i, j, kk: (kk, j)),
                  pl.BlockSpec((tk, tn), lambda i, j, kk: (kk, j + jf))],
        out_specs=[o_spec, o_spec, o_spec],
        out_shape=[o_shape, o_shape, o_shape],
        scratch_shapes=[pltpu.VMEM((tm, tn), F32), pltpu.VMEM((tm, tn), F32)],
        compiler_params=_params(("parallel", "parallel", "arbitrary")),
    )(a, w, w)


def _rowwise(name, fn, *, n_tiles, tr, row_ins, row_outs, sel_in=None, sel_off=0, ctx_rows=0,
             full_ins=(), acc_shape=None):
    sr = 32 if tr % 32 == 0 else tr
    n_row, n_full, n_out = len(row_ins), len(full_ins), len(row_outs)
    has_sel = sel_in is not None
    has_acc = acc_shape is not None

    def sel_of(i):
        return jnp.where((i + sel_off) * tr < ctx_rows, 0, 1)

    def body(*refs):
        row_refs = refs[:n_row]
        pos = n_row
        sel_ref = None
        if has_sel:
            sel_ref = refs[pos]
            pos += 1
        full_refs = refs[pos:pos + n_full]
        pos += n_full
        out_refs = refs[pos:pos + n_out]
        pos += n_out
        acc_ref = refs[pos] if has_acc else None
        i = pl.program_id(0)
        if has_acc:
            first = (i == 0) | ((i + sel_off) * tr == ctx_rows)

            @pl.when(first)
            def _():
                acc_ref[...] = jnp.zeros_like(acc_ref)

        sel = (lambda kk: sel_ref[kk:kk + 1, :]) if has_sel else None
        fulls = [r[...] for r in full_refs]

        def slab(r, carry):
            rs = pl.ds(pl.multiple_of(r * sr, sr), sr)
            rows = [ref[rs, :] for ref in row_refs]
            outs, accs = fn(rows, sel, fulls)
            for o_ref, o in zip(out_refs, outs):
                o_ref[rs, :] = o.astype(o_ref.dtype)
            for kk, a in enumerate(accs):
                acc_ref[kk:kk + 1, :a.shape[1]] += a
            return carry

        lax.fori_loop(0, tr // sr, slab, 0)

    in_specs, args = [], []
    for arr, off, blk in row_ins:
        if blk is None:
            in_specs.append(pl.BlockSpec((tr, arr.shape[1]), functools.partial(lambda i, o: (i + o, 0), o=off)))
        else:
            in_specs.append(pl.BlockSpec((tr, blk[0]), functools.partial(lambda i, o, cb: (i + o, cb), o=off, cb=blk[1])))
        args.append(arr)
    if has_sel:
        in_specs.append(pl.BlockSpec((None,) + sel_in.shape[1:], lambda i: (sel_of(i), 0, 0)))
        args.append(sel_in)
    for arr in full_ins:
        in_specs.append(pl.BlockSpec(arr.shape, lambda i: (0, 0)))
        args.append(arr)
    out_specs, out_shape = [], []
    for rows, cols, dt, off in row_outs:
        out_specs.append(pl.BlockSpec((tr, cols), functools.partial(lambda i, o: (i + o, 0), o=off)))
        out_shape.append(jax.ShapeDtypeStruct((rows, cols), dt))
    if has_acc:
        out_specs.append(pl.BlockSpec((None,) + tuple(acc_shape), lambda i: (sel_of(i), 0, 0)))
        out_shape.append(jax.ShapeDtypeStruct((2,) + tuple(acc_shape), F32))
    return pl.pallas_call(
        body, name=name, grid=(n_tiles,), in_specs=in_specs, out_specs=out_specs, out_shape=out_shape,
        compiler_params=_params(("arbitrary",)),
    )(*args)


def _swap_pairs(x):
    lane = lax.broadcasted_iota(jnp.int32, x.shape, 1)
    nxt = pltpu.roll(x, x.shape[1] - 1, 1)
    prv = pltpu.roll(x, 1, 1)
    return jnp.where(lane % 2 == 0, nxt, prv)


def _heads_map(fn, arrs, width):
    outs = None
    for h in range(width // HEAD_DIM):
        sl = slice(h * HEAD_DIM, (h + 1) * HEAD_DIM)
        res = fn(*[a[:, sl] for a in arrs])
        if outs is None:
            outs = [[] for _ in res]
        for lst, r in zip(outs, res):
            lst.append(r)
    return [jnp.concatenate(lst, axis=1) if len(lst) > 1 else lst[0] for lst in outs]


QSCALE = HEAD_DIM ** -0.5 * math.log2(math.e)
LN2 = math.log(2.0)


def _lane_chunks(a):
    return [a[:, cc * LANES_V7X:(cc + 1) * LANES_V7X] for cc in range(a.shape[1] // LANES_V7X)]


def _row_bcast(col, like):
    return jnp.broadcast_to(col, like.shape)


def _flash_tiles(t, tk_all):
    return _tile(t, 256), _tile(tk_all, 1024)


def _flash_fwd(q, k, vx, groups):
    t, aw = q.shape
    tk_all, kvw = k.shape
    kvh = kvw // HEAD_DIM
    gw = groups * HEAD_DIM
    tq, tk = _flash_tiles(t, tk_all)
    nk = tk_all // tk

    def body(q_ref, k_ref, v_ref, o_ref, lse_ref, m_sc, l_sc, acc_sc):
        j = pl.program_id(2)

        @pl.when(j == 0)
        def _():
            m_sc[...] = jnp.full_like(m_sc, -jnp.inf)
            l_sc[...] = jnp.zeros_like(l_sc)
            acc_sc[...] = jnp.zeros_like(acc_sc)

        kt = k_ref[...]
        vt = v_ref[...]
        for g in range(groups):
            sl = slice(g * HEAD_DIM, (g + 1) * HEAD_DIM)
            s = _lane_chunks(lax.dot_general(q_ref[:, sl], kt, NT_DIMS, preferred_element_type=F32))
            mx = functools.reduce(jnp.maximum, s)
            m_prev = m_sc[g]
            m_new = jnp.maximum(m_prev, _row_bcast(jnp.max(mx, axis=1, keepdims=True), mx))
            p = jnp.concatenate([jnp.exp2(sc - m_new).astype(BF16) for sc in s], axis=1)
            alpha = jnp.exp2(m_prev - m_new)
            pv = jnp.dot(p, vt, preferred_element_type=F32)
            acc_sc[g] = alpha * acc_sc[g] + pv[:, :HEAD_DIM]
            l_sc[g] = alpha * l_sc[g] + pv[:, HEAD_DIM:]
            m_sc[g] = m_new

        @pl.when(j == nk - 1)
        def _():
            for g in range(groups):
                sl = slice(g * HEAD_DIM, (g + 1) * HEAD_DIM)
                o_ref[:, sl] = (acc_sc[g] / l_sc[g]).astype(o_ref.dtype)
                lse_ref[:, sl] = m_sc[g] + jnp.log2(l_sc[g])

    qs = pl.BlockSpec((tq, gw), lambda kh, i, j: (i, kh))
    sc = pltpu.VMEM((groups, tq, HEAD_DIM), F32)
    return pl.pallas_call(
        body, name="flash_fwd", grid=(kvh, t // tq, nk),
        in_specs=[qs, pl.BlockSpec((tk, HEAD_DIM), lambda kh, i, j: (j, kh)),
                  pl.BlockSpec((tk, 2 * HEAD_DIM), lambda kh, i, j: (j, kh))],
        out_specs=[qs, qs],
        out_shape=[jax.ShapeDtypeStruct((t, aw), BF16), jax.ShapeDtypeStruct((t, aw), F32)],
        scratch_shapes=[sc, sc, sc],
        compiler_params=_params(("parallel", "parallel", "arbitrary")),
    )(q, k, vx)


def _flash_p_ds(q, kt, vt, do, lse, delta):
    s = _lane_chunks(lax.dot_general(q, kt, NT_DIMS, preferred_element_type=F32))
    dp = _lane_chunks(lax.dot_general(do, vt, NT_DIMS, preferred_element_type=F32))
    p = [jnp.exp2(sc - lse) for sc in s]
    ds = jnp.concatenate([(pc * (dc - delta)).astype(BF16) for pc, dc in zip(p, dp)], axis=1)
    return jnp.concatenate([pc.astype(BF16) for pc in p], axis=1), ds


def _flash_delta(do, o):
    prod = do.astype(F32) * o.astype(F32)
    return _row_bcast(jnp.sum(prod, axis=1, keepdims=True), prod)


def _flash_bwd(q, k, vx, o, do, lse, groups):
    t, aw = q.shape
    tk_all, kvw = k.shape
    kvh = kvw // HEAD_DIM
    gw = groups * HEAD_DIM
    tq, tk = _flash_tiles(t, tk_all)
    nq, nk = t // tq, tk_all // tk

    def body(q_ref, k_ref, v_ref, o_ref, do_ref, lse_ref, dq_ref, dk_ref, dv_ref, dq_sc, dk_acc, dv_acc):
        j = pl.program_id(1)
        i = pl.program_id(2)

        @pl.when(i == 0)
        def _():
            dk_acc[...] = jnp.zeros_like(dk_acc)
            dv_acc[...] = jnp.zeros_like(dv_acc)

        @pl.when(j == 0)
        def _():
            dq_sc[i] = jnp.zeros((groups, tq, HEAD_DIM), F32)

        kt = k_ref[...]
        vt = v_ref[:, :HEAD_DIM]
        for g in range(groups):
            sl = slice(g * HEAD_DIM, (g + 1) * HEAD_DIM)
            qv = q_ref[:, sl]
            dov = do_ref[:, sl]
            p, ds = _flash_p_ds(qv, kt, vt, dov, lse_ref[:, sl], _flash_delta(dov, o_ref[:, sl]))
            dv_acc[...] += lax.dot_general(p, dov, TN_DIMS, preferred_element_type=F32)
            dk_acc[...] += lax.dot_general(ds, qv, TN_DIMS, preferred_element_type=F32)
            dq_sc[i, g] += jnp.dot(ds, kt, preferred_element_type=F32)

        @pl.when(i == nq - 1)
        def _():
            dk_ref[...] = dk_acc[...] * LN2
            dv_ref[...] = dv_acc[...]

        @pl.when(j == nk - 1)
        def _():
            for g in range(groups):
                dq_ref[:, g * HEAD_DIM:(g + 1) * HEAD_DIM] = dq_sc[i, g]

    qs = pl.BlockSpec((tq, gw), lambda kh, j, i: (i, kh))
    ks = pl.BlockSpec((tk, HEAD_DIM), lambda kh, j, i: (j, kh))
    dq_spec = pl.BlockSpec((tq, gw), lambda kh, j, i: (jnp.where(j == nk - 1, i, 0), kh))
    return pl.pallas_call(
        body, name="flash_bwd", grid=(kvh, nk, nq),
        in_specs=[qs, ks, pl.BlockSpec((tk, 2 * HEAD_DIM), lambda kh, j, i: (j, kh)), qs, qs, qs],
        out_specs=[dq_spec, ks, ks],
        out_shape=[jax.ShapeDtypeStruct((t, aw), F32), jax.ShapeDtypeStruct((tk_all, kvw), F32),
                   jax.ShapeDtypeStruct((tk_all, kvw), F32)],
        scratch_shapes=[pltpu.VMEM((nq, groups, tq, HEAD_DIM), F32), pltpu.VMEM((tk, HEAD_DIM), F32),
                        pltpu.VMEM((tk, HEAD_DIM), F32)],
        compiler_params=_params(("parallel", "arbitrary", "arbitrary")),
    )(q, k, vx, o, do, lse)


def _bf_nn(a, b):
    return jnp.dot(a.astype(BF16), b.astype(BF16), preferred_element_type=F32)


def _bf_nt(a, b):
    return lax.dot_general(a.astype(BF16), b.astype(BF16), NT_DIMS, preferred_element_type=F32)


def _bf_tn(a, b):
    return lax.dot_general(a.astype(BF16), b.astype(BF16), TN_DIMS, preferred_element_type=F32)


@jax.custom_vjp
def _d_nn(a, b):
    return _bf_nn(a, b)


@jax.custom_vjp
def _d_nt(a, b):
    return _bf_nt(a, b)


@jax.custom_vjp
def _d_tn(a, b):
    return _bf_tn(a, b)


_d_nn.defvjp(lambda a, b: (_bf_nn(a, b), (a, b)), lambda r, g: (_d_nt(g, r[1]), _d_tn(r[0], g)))
_d_nt.defvjp(lambda a, b: (_bf_nt(a, b), (a, b)), lambda r, g: (_d_nn(g, r[1]), _d_tn(g, r[0])))
_d_tn.defvjp(lambda a, b: (_bf_tn(a, b), (a, b)), lambda r, g: (_d_nt(r[1], g), _d_nn(r[0], g)))


def _ret_chunk(q, k_raw, v, state, lg, rev, dots):
    nn, nt, tn = dots
    c = RET_CHUNK
    tcol = lax.broadcasted_iota(jnp.int32, (c, 1), 0).astype(F32)
    trow = lax.broadcasted_iota(jnp.int32, (1, c), 1).astype(F32)
    ucol = jnp.where(rev, c - 1.0 - tcol, tcol)
    urow = jnp.where(rev, c - 1.0 - trow, trow)
    e = ucol - urow
    low = e >= 0
    intra = jnp.where(low, jnp.exp(jnp.where(low, e, 0.0) * lg), 0.0)
    k = k_raw * (HEAD_DIM ** -0.5)
    inner = nt(q, k) * intra
    y = nn(inner, v) + nn(q, state) * jnp.exp((ucol + 1.0) * lg)
    new_state = state * jnp.exp(c * lg) + tn(k * jnp.exp((c - 1.0 - ucol) * lg), v)
    return y, new_state


def _ret_chunk_index(n_chunks, n_ctx_chunks):
    def idx(d, s):
        if d == 0:
            return s
        return jnp.where(s < n_ctx_chunks, n_ctx_chunks - 1 - s, n_chunks - 1 - s + n_ctx_chunks)
    return idx


def _ret_fwd(pr, lgv, ctx_rows):
    tk_all = pr.shape[0]
    rw = pr.shape[1] // 3
    nh = rw // HEAD_DIM
    nc = tk_all // RET_CHUNK
    cidx = _ret_chunk_index(nc, ctx_rows // RET_CHUNK)

    def body(pf_ref, pb_ref, lg_ref, yf_ref, yb_ref, st_ref, s_sc):
        s = pl.program_id(0)

        @pl.when(s == 0)
        def _():
            s_sc[...] = jnp.zeros_like(s_sc)

        for d, (p_ref, y_ref) in enumerate(((pf_ref, yf_ref), (pb_ref, yb_ref))):
            for h in range(nh):
                cols = [slice((part * nh + h) * HEAD_DIM, (part * nh + h + 1) * HEAD_DIM) for part in range(3)]
                state = s_sc[d, h]
                st_ref[d, h] = state
                y, new_state = _ret_chunk(p_ref[:, cols[0]], p_ref[:, cols[1]], p_ref[:, cols[2]], state,
                                          lg_ref[d, h][:, :1], d == 1, (_bf_nn, _bf_nt, _bf_tn))
                y_ref[:, h * HEAD_DIM:(h + 1) * HEAD_DIM] = y
                s_sc[d, h] = new_state

    y_shape = jax.ShapeDtypeStruct((tk_all, rw), F32)
    return pl.pallas_call(
        body, name="ret_fwd", grid=(nc,),
        in_specs=[pl.BlockSpec((RET_CHUNK, 3 * rw), lambda s: (cidx(0, s), 0)),
                  pl.BlockSpec((RET_CHUNK, 3 * rw), lambda s: (cidx(1, s), 0)),
                  pl.BlockSpec(lgv.shape, lambda s: (0, 0, 0, 0))],
        out_specs=[pl.BlockSpec((RET_CHUNK, rw), lambda s: (cidx(0, s), 0)),
                   pl.BlockSpec((RET_CHUNK, rw), lambda s: (cidx(1, s), 0)),
                   pl.BlockSpec((2, nh, None, HEAD_DIM, HEAD_DIM), lambda s: (0, 0, s, 0, 0))],
        out_shape=[y_shape, y_shape, jax.ShapeDtypeStruct((2, nh, nc, HEAD_DIM, HEAD_DIM), F32)],
        scratch_shapes=[pltpu.VMEM((2, nh, HEAD_DIM, HEAD_DIM), F32)],
        compiler_params=_params(("arbitrary",)),
    )(pr, pr, lgv)


def _ret_bwd(pr, states, dy, lgv, ctx_rows):
    tk_all = pr.shape[0]
    rw = pr.shape[1] // 3
    nh = rw // HEAD_DIM
    nc = tk_all // RET_CHUNK
    cidx = _ret_chunk_index(nc, ctx_rows // RET_CHUNK)

    def body(pf_ref, pb_ref, st_ref, dyf_ref, dyb_ref, lg_ref, dpf_ref, dpb_ref, dlg_ref, ds_sc):
        sp = pl.program_id(0)

        @pl.when(sp == 0)
        def _():
            ds_sc[...] = jnp.zeros_like(ds_sc)
            dlg_ref[...] = jnp.zeros_like(dlg_ref)

        for d, (p_ref, dy_ref, dp_ref) in enumerate(((pf_ref, dyf_ref, dpf_ref), (pb_ref, dyb_ref, dpb_ref))):
            for h in range(nh):
                cols = [slice((part * nh + h) * HEAD_DIM, (part * nh + h + 1) * HEAD_DIM) for part in range(3)]

                def step(q, k, v, state, lg, rev=(d == 1)):
                    return _ret_chunk(q, k, v, state, lg, rev, (_d_nn, _d_nt, _d_tn))

                _, vjp = jax.vjp(step, p_ref[:, cols[0]], p_ref[:, cols[1]], p_ref[:, cols[2]], st_ref[d, h],
                                 lg_ref[d, h][:, :1])
                grads = vjp((dy_ref[:, h * HEAD_DIM:(h + 1) * HEAD_DIM], ds_sc[d, h]))
                for part in range(3):
                    dp_ref[:, cols[part]] = grads[part]
                ds_sc[d, h] = grads[3]
                dlg_ref[d, h] += jnp.broadcast_to(grads[4], (1, HEAD_DIM))

    def at(d):
        return lambda sp: (cidx(d, nc - 1 - sp), 0)

    dp_shape = jax.ShapeDtypeStruct((tk_all, 3 * rw), F32)
    lg_spec = pl.BlockSpec(lgv.shape, lambda sp: (0, 0, 0, 0))
    return pl.pallas_call(
        body, name="ret_bwd", grid=(nc,),
        in_specs=[pl.BlockSpec((RET_CHUNK, 3 * rw), at(0)), pl.BlockSpec((RET_CHUNK, 3 * rw), at(1)),
                  pl.BlockSpec((2, nh, None, HEAD_DIM, HEAD_DIM), lambda sp: (0, 0, nc - 1 - sp, 0, 0)),
                  pl.BlockSpec((RET_CHUNK, rw), at(0)), pl.BlockSpec((RET_CHUNK, rw), at(1)), lg_spec],
        out_specs=[pl.BlockSpec((RET_CHUNK, 3 * rw), at(0)), pl.BlockSpec((RET_CHUNK, 3 * rw), at(1)), lg_spec],
        out_shape=[dp_shape, dp_shape, jax.ShapeDtypeStruct(lgv.shape, F32)],
        scratch_shapes=[pltpu.VMEM((2, nh, HEAD_DIM, HEAD_DIM), F32)],
        compiler_params=_params(("arbitrary",)),
    )(pr, pr, states, dy, dy, lgv)


FLIP_X, FLIP_Y, FLIP_XY, FLIP_C = (1, 0, 0), (0, 1, 0), (1, 1, 0), (0, 0, 1)
CHIP_FLIPS = ((FLIP_X, 2), (FLIP_Y, 1), (FLIP_XY, 3))


def _flip(me, mask):
    return tuple(1 - v if m else v for v, m in zip(me, mask))


def _comm(name, ins, out_shapes, plan, n_remote, n_local, aliases=None):
    n_in, n_out = len(ins), len(out_shapes)

    def body(*refs):
        in_refs = refs[:n_in]
        out_refs = refs[n_in:n_in + n_out]
        send_sems, recv_sems, local_sems = refs[n_in + n_out:]
        me = (lax.axis_index("x"), lax.axis_index("y"), lax.axis_index("c"))
        local, phases = plan(in_refs, out_refs, me)
        local_copies = [pltpu.make_async_copy(s, d, local_sems.at[i]) for i, (s, d) in enumerate(local)]
        for cp in local_copies:
            cp.start()
        sent = []
        kk = 0
        for phase in phases:
            arrivals = []
            for mask, src, dst, landing in phase:
                peer = _flip(me, mask)
                cp = pltpu.make_async_remote_copy(src_ref=src, dst_ref=dst, send_sem=send_sems.at[kk],
                                                  recv_sem=recv_sems.at[kk], device_id=peer,
                                                  device_id_type=pl.DeviceIdType.MESH)
                cp.start()
                sent.append(cp)
                arrivals.append(pltpu.make_async_remote_copy(
                    src_ref=landing, dst_ref=landing, send_sem=send_sems.at[kk], recv_sem=recv_sems.at[kk],
                    device_id=peer, device_id_type=pl.DeviceIdType.MESH))
                kk += 1
            for cp in arrivals:
                cp.wait_recv()
        for cp in sent:
            cp.wait_send()
        for cp in local_copies:
            cp.wait()

    any_spec = pl.BlockSpec(memory_space=pl.ANY)
    return pl.pallas_call(
        body, name=name,
        in_specs=[any_spec] * n_in, out_specs=[any_spec] * n_out, out_shape=list(out_shapes),
        scratch_shapes=[pltpu.SemaphoreType.DMA((n_remote,)), pltpu.SemaphoreType.DMA((n_remote,)),
                        pltpu.SemaphoreType.DMA((max(n_local, 1),))],
        input_output_aliases=aliases or {},
    )(*ins)


def _ds(start, size):
    return pl.ds(pl.multiple_of(start * size, 8), size)


def _all_gather8(name, v):
    masks = [(a, b, cc) for a in (0, 1) for b in (0, 1) for cc in (0, 1)][1:]

    def index(p):
        return 4 * p[0] + 2 * p[1] + p[2]

    def plan(in_refs, out_refs, me):
        (src,), (out,) = in_refs, out_refs
        local = [(src, out.at[index(me)])]
        phase = [(m, src, out.at[index(me)], out.at[index(_flip(me, m))]) for m in masks]
        return local, [phase]

    return _comm(name, [v], [jax.ShapeDtypeStruct((N_DEV,) + v.shape, v.dtype)], plan, len(masks), 1)[0]


def _gather_row(name, row):
    n = row.shape[1]
    n_pad = -(-n // (8 * LANES_V7X)) * (8 * LANES_V7X)
    v = jnp.pad(row, ((0, 0), (0, n_pad - n))).reshape(8, n_pad // 8)
    return _all_gather8(name, v).reshape(N_DEV, n_pad)[:, :n]


class _Sharded:
    def __init__(self, kind, rows, cols):
        self.kind, self.rows, self.cols = kind, rows, cols
        self.shard_shape = (rows, cols // N_CHIP) if kind == "col" else (rows // N_CHIP, cols)
        self.half_shape = (rows // 2, cols) if kind == "col" else (rows, cols // 2)
        self.piece_shape = (rows // 2, cols // N_CHIP) if kind == "col" else (rows // N_CHIP, cols // 2)

    def shard_of_full(self, ref, s):
        if self.kind == "col":
            return ref.at[:, _ds(s, self.cols // N_CHIP)]
        return ref.at[_ds(s, self.rows // N_CHIP), :]

    def half_of_full(self, ref, h):
        if self.kind == "col":
            return ref.at[_ds(h, self.rows // 2), :]
        return ref.at[:, _ds(h, self.cols // 2)]

    def piece_of_full(self, ref, s, h):
        if self.kind == "col":
            return ref.at[_ds(h, self.rows // 2), _ds(s, self.cols // N_CHIP)]
        return ref.at[_ds(s, self.rows // N_CHIP), _ds(h, self.cols // 2)]

    def half_of_shard(self, ref, h):
        if self.kind == "col":
            return ref.at[_ds(h, self.rows // 2), :]
        return ref.at[:, _ds(h, self.cols // 2)]

    def shard_of_half(self, ref, s):
        if self.kind == "col":
            return ref.at[:, _ds(s, self.cols // N_CHIP)]
        return ref.at[_ds(s, self.rows // N_CHIP), :]


def _place_shard(meta, w, s_arr):
    r, cols = w.shape
    tr = _tile(r, 256, 16)
    nr = r // tr

    def body(s_ref, w_ref, o_ref):
        o_ref[...] = w_ref[...].astype(BF16)

    if meta.kind == "col":
        o_map = lambda i, s_ref: (i, s_ref[0])
    else:
        o_map = lambda i, s_ref: (i + s_ref[0] * nr, 0)
    return pl.pallas_call(
        body, name="place_shard",
        grid_spec=pltpu.PrefetchScalarGridSpec(
            num_scalar_prefetch=1, grid=(nr,),
            in_specs=[pl.BlockSpec((tr, cols), lambda i, s_ref: (i, 0))],
            out_specs=pl.BlockSpec((tr, cols), o_map)),
        out_shape=jax.ShapeDtypeStruct((meta.rows, meta.cols), BF16),
        compiler_params=_params(("parallel",)),
    )(s_arr, w)


def _gather_copies(metas, over_ici):
    def copies(fulls, me):
        x, y, c = me
        s_me = 2 * x + y
        out = []
        for meta, full in zip(metas, fulls):
            for mask, bits in CHIP_FLIPS:
                s_peer = jnp.bitwise_xor(s_me, bits)
                if over_ici:
                    out.append((mask, meta.piece_of_full(full, s_me, c), meta.piece_of_full(full, s_me, c),
                                meta.piece_of_full(full, s_peer, c)))
                else:
                    out.append((FLIP_C, meta.piece_of_full(full, s_peer, c), meta.piece_of_full(full, s_peer, c),
                                meta.piece_of_full(full, s_peer, 1 - c)))
        return out
    return copies


def _gather_forward(name, metas, fulls):
    nt = len(metas)
    copies = _gather_copies(metas, False)
    outs = [jax.ShapeDtypeStruct((m.rows, m.cols), BF16) for m in metas]
    return _comm(name, list(fulls), outs, lambda ins, outs_, me: ([], [copies(outs_, me)]), 3 * nt, 0,
                 aliases={i: i for i in range(nt)})


HBM_SPEC = pl.BlockSpec(memory_space=pltpu.HBM)
SEM_SPEC = pl.BlockSpec(memory_space=pltpu.SEMAPHORE)
SPLIT_EFFECT = pltpu.SideEffectType.DATAFLOW_SIDE_EFFECTING


def _split_start(name, bufs, groups, after):
    nb, ng = len(bufs), len(groups)

    def body(*refs):
        buf_refs = refs[:nb]
        sem_refs = refs[nb + 1:nb + 1 + 2 * ng]
        token = refs[-1]
        me = (lax.axis_index("x"), lax.axis_index("y"), lax.axis_index("c"))
        for gi, (lo, n_bufs, copies, _) in enumerate(groups):
            for kk, (mask, src, dst, _) in enumerate(copies(buf_refs[lo:lo + n_bufs], me)):
                pltpu.make_async_remote_copy(src_ref=src, dst_ref=dst, send_sem=sem_refs[2 * gi].at[kk],
                                             recv_sem=sem_refs[2 * gi + 1].at[kk], device_id=_flip(me, mask),
                                             device_id_type=pl.DeviceIdType.MESH).start()
        token[...] = jnp.zeros_like(token)

    out_shape = []
    for _, _, _, n in groups:
        out_shape += [pltpu.SemaphoreType.DMA((n,)), pltpu.SemaphoreType.DMA((n,))]
    out_shape += [pltpu.HBM(b.shape, b.dtype) for b in bufs] + [jax.ShapeDtypeStruct((8, LANES_V7X), F32)]
    res = pl.pallas_call(
        body, name=name, out_shape=tuple(out_shape),
        in_specs=(HBM_SPEC,) * nb + (pl.BlockSpec(memory_space=pl.ANY),),
        out_specs=(SEM_SPEC,) * (2 * ng) + (HBM_SPEC,) * nb + (pl.BlockSpec(memory_space=pltpu.VMEM),),
        input_output_aliases={i: 2 * ng + i for i in range(nb)},
        compiler_params=pltpu.CompilerParams(has_side_effects=SPLIT_EFFECT),
    )(*[pltpu.with_memory_space_constraint(b, pltpu.HBM) for b in bufs], after)
    sems = [(res[2 * gi], res[2 * gi + 1]) for gi in range(ng)]
    return sems, list(res[2 * ng:2 * ng + nb]), res[-1]


def _split_wait(name, sems, bufs, copies, after):
    nb = len(bufs)

    def body(*refs):
        buf_refs = refs[:nb]
        send_sems, recv_sems = refs[nb], refs[nb + 1]
        me = (lax.axis_index("x"), lax.axis_index("y"), lax.axis_index("c"))
        for kk, (mask, _, _, landing) in enumerate(copies(buf_refs, me)):
            cp = pltpu.make_async_remote_copy(src_ref=landing, dst_ref=landing, send_sem=send_sems.at[kk],
                                              recv_sem=recv_sems.at[kk], device_id=_flip(me, mask),
                                              device_id_type=pl.DeviceIdType.MESH)
            cp.wait_send()
            cp.wait_recv()

    return list(pl.pallas_call(
        body, name=name, out_shape=tuple(pltpu.HBM(b.shape, b.dtype) for b in bufs),
        in_specs=(HBM_SPEC,) * nb + (SEM_SPEC, SEM_SPEC, pl.BlockSpec(memory_space=pl.ANY)),
        out_specs=(HBM_SPEC,) * nb,
        input_output_aliases={i: i for i in range(nb)},
        compiler_params=pltpu.CompilerParams(has_side_effects=SPLIT_EFFECT),
    )(*bufs, sems[0], sems[1], after))


def _reduce_pair(metas, grads):
    def plan(in_refs, out_refs, me):
        c = me[2]
        phase = [(FLIP_C, m.half_of_full(g, 1 - c), land, land) for m, g, land in zip(metas, in_refs, out_refs)]
        return [], [phase]

    outs = [jax.ShapeDtypeStruct(m.half_shape, BF16) for m in metas]
    return _comm("reduce_pair", list(grads), outs, plan, len(metas), 0)


def _reduce_chip_copies(metas):
    def copies(refs, me):
        x, y, _ = me
        s_me = 2 * x + y
        out = []
        for m, p, land in zip(metas, refs[:len(metas)], refs[len(metas):]):
            for kk, (mask, bits) in enumerate(CHIP_FLIPS):
                s_peer = jnp.bitwise_xor(s_me, bits)
                out.append((mask, m.shard_of_half(p, s_peer), land.at[kk], land.at[kk]))
        return out
    return copies


def _reduce_chips(metas, halves):
    copies = _reduce_chip_copies(metas)
    outs = [jax.ShapeDtypeStruct((3,) + m.piece_shape, BF16) for m in metas]
    return _comm("reduce_chips", list(halves), outs,
                 lambda ins, outs_, me: ([], [copies(list(ins) + list(outs_), me)]), 3 * len(metas), 0)


def _reduce_chips_start(name, metas, halves):
    lands = [lax.empty((3,) + m.piece_shape, BF16) for m in metas]
    bufs = list(halves) + lands
    sems, thru, token = _split_start(name, bufs, [(0, len(bufs), _reduce_chip_copies(metas), 3 * len(metas))],
                                     halves[0])
    return sems[0], thru, token


def _reduce_chips_wait(name, metas, sems, thru, after):
    done = _split_wait(name, sems, thru, _reduce_chip_copies(metas), after)
    return done[:len(metas)], done[len(metas):]


def _share_halves(metas, shards):
    def plan(in_refs, out_refs, me):
        c = me[2]
        phase = [(FLIP_C, m.half_of_shard(g, c), m.half_of_shard(g, c), m.half_of_shard(g, 1 - c))
                 for m, g in zip(metas, out_refs)]
        return [], [phase]

    outs = [jax.ShapeDtypeStruct(m.shard_shape, F32) for m in metas]
    return _comm("share_halves", list(shards), outs, plan, len(metas), 0,
                 aliases={i: i for i in range(len(metas))})


def _pair_sum(meta, grad, landed, c_arr):
    hr, hc = meta.half_shape
    tr = _tile(hr, 256, 16)
    tc = _tile(hc, 2048)
    nr, ncol = hr // tr, hc // tc

    def body(c_ref, g_ref, l_ref, o_ref):
        o_ref[...] = (g_ref[...].astype(F32) + l_ref[...].astype(F32)).astype(BF16)

    if meta.kind == "col":
        g_map = lambda i, j, c_ref: (i + c_ref[0] * nr, j)
    else:
        g_map = lambda i, j, c_ref: (i, j + c_ref[0] * ncol)
    blk = (tr, tc)
    return pl.pallas_call(
        body, name="pair_sum",
        grid_spec=pltpu.PrefetchScalarGridSpec(
            num_scalar_prefetch=1, grid=(nr, ncol),
            in_specs=[pl.BlockSpec(blk, g_map), pl.BlockSpec(blk, lambda i, j, c_ref: (i, j))],
            out_specs=pl.BlockSpec(blk, lambda i, j, c_ref: (i, j))),
        out_shape=jax.ShapeDtypeStruct((hr, hc), BF16),
        compiler_params=_params(("parallel", "parallel")),
    )(c_arr, grad, landed)


def _sum_pieces(meta, half, landed, s_arr, c_arr):
    pr, pc = meta.piece_shape
    tr = _tile(pr, 256, 16)
    tc = _tile(pc, 2048)
    nr, ncol = pr // tr, pc // tc

    def body(s_ref, c_ref, p_ref, l_ref, o_ref):
        acc = p_ref[...].astype(F32)
        for kk in range(3):
            acc = acc + l_ref[kk].astype(F32)
        o_ref[...] = acc

    if meta.kind == "col":
        p_map = lambda i, j, s_ref, c_ref: (i, j + s_ref[0] * ncol)
        o_map = lambda i, j, s_ref, c_ref: (i + c_ref[0] * nr, j)
    else:
        p_map = lambda i, j, s_ref, c_ref: (i + s_ref[0] * nr, j)
        o_map = lambda i, j, s_ref, c_ref: (i, j + c_ref[0] * ncol)
    blk = (tr, tc)
    return pl.pallas_call(
        body, name="sum_pieces",
        grid_spec=pltpu.PrefetchScalarGridSpec(
            num_scalar_prefetch=2, grid=(nr, ncol),
            in_specs=[pl.BlockSpec(blk, p_map), pl.BlockSpec((3,) + blk, lambda i, j, s_ref, c_ref: (0, i, j))],
            out_specs=pl.BlockSpec(blk, o_map)),
        out_shape=jax.ShapeDtypeStruct(meta.shard_shape, F32),
        compiler_params=_params(("parallel", "parallel")),
    )(s_arr, c_arr, half, landed)


def _adam_rows(rows, sel, fulls):
    w, g, m, v = rows
    m2 = ADAM_B1 * m + (1.0 - ADAM_B1) * g
    v2 = ADAM_B2 * v + (1.0 - ADAM_B2) * jnp.square(g)
    m_hat = m2 / (1.0 - ADAM_B1 ** ADAM_STEP)
    v_hat = v2 / (1.0 - ADAM_B2 ** ADAM_STEP)
    delta = -ADAM_LR * (m_hat / (jnp.sqrt(v_hat) + ADAM_EPS) + ADAM_WD * w)
    return [delta, m2, v2], []


def _adamw(w, g, m, v):
    r, c = w.shape
    tr = _tile(r, 128, 8)
    outs = _rowwise("adamw", _adam_rows, n_tiles=r // tr, tr=tr,
                    row_ins=[(w, 0, None), (g, 0, None), (m, 0, None), (v, 0, None)],
                    row_outs=[(r, c, F32, 0)] * 3)
    return outs[0], outs[1], outs[2]


def _ada_fwd(cg, w, b):
    d, n = w.shape
    tn = _tile(n, 512)

    def body(c_ref, w_ref, b_ref, o_ref):
        a = _silu(c_ref[...]).astype(BF16)
        o_ref[...] = jnp.dot(a, w_ref[...].astype(BF16), preferred_element_type=F32) + b_ref[...]

    return pl.pallas_call(
        body, name="ada_fwd", grid=(n // tn,),
        in_specs=[pl.BlockSpec(cg.shape, lambda j: (0, 0)), pl.BlockSpec((d, tn), lambda j: (0, j)),
                  pl.BlockSpec((1, tn), lambda j: (0, j))],
        out_specs=pl.BlockSpec((cg.shape[0], tn), lambda j: (0, j)),
        out_shape=jax.ShapeDtypeStruct((cg.shape[0], n), F32),
        compiler_params=_params(("parallel",)),
    )(cg, w, b)


def _ada_bwd(cg, dm, w):
    d, n = w.shape
    tn = _tile(n, 512)
    nj = n // tn

    def body(c_ref, dm_ref, w_ref, gw_ref, da_ref, acc):
        j = pl.program_id(0)

        @pl.when(j == 0)
        def _():
            acc[...] = jnp.zeros_like(acc)

        a = _silu(c_ref[...]).astype(BF16)
        dmv = dm_ref[...].astype(BF16)
        gw_ref[...] = lax.dot_general(a, dmv, TN_DIMS, preferred_element_type=F32)
        acc[...] += lax.dot_general(dmv, w_ref[...].astype(BF16), NT_DIMS, preferred_element_type=F32)

        @pl.when(j == nj - 1)
        def _():
            da_ref[...] = acc[...]

    return pl.pallas_call(
        body, name="ada_bwd", grid=(nj,),
        in_specs=[pl.BlockSpec(cg.shape, lambda j: (0, 0)), pl.BlockSpec((dm.shape[0], tn), lambda j: (0, j)),
                  pl.BlockSpec((d, tn), lambda j: (0, j))],
        out_specs=[pl.BlockSpec((d, tn), lambda j: (0, j)), pl.BlockSpec(cg.shape, lambda j: (0, 0))],
        out_shape=[jax.ShapeDtypeStruct((d, n), F32), jax.ShapeDtypeStruct(cg.shape, F32)],
        scratch_shapes=[pltpu.VMEM(cg.shape, F32)],
        compiler_params=_params(("arbitrary",)),
    )(cg, dm, w)


def _small_reduce(gathered, logits, n_mod_cols, lg_off, loss_off, loss_cols):
    npk = gathered.shape[1]

    def body(g_ref, lo_ref, tot_ref, gb_ref, gl_ref, loss_ref):
        acc = g_ref[0:1, :]
        for dd in range(1, N_DEV):
            acc = acc + g_ref[dd:dd + 1, :]
        tot_ref[...] = acc
        gb_ref[...] = acc[:, :n_mod_cols] + acc[:, n_mod_cols:2 * n_mod_cols]
        gl_ref[...] = acc[:, lg_off:lg_off + LANES_V7X] * _sigmoid(-lo_ref[...])
        loss = jnp.sum(acc[:, loss_off:loss_off + loss_cols], axis=1, keepdims=True)
        loss_ref[...] = jnp.broadcast_to(loss, loss_ref.shape)

    lane = jax.ShapeDtypeStruct((1, LANES_V7X), F32)
    return pl.pallas_call(
        body, name="small_reduce",
        out_shape=[jax.ShapeDtypeStruct((1, npk), F32), jax.ShapeDtypeStruct((1, n_mod_cols), F32), lane, lane],
    )(gathered, logits)


def _c_ctx_grad(parts, c_ctx):
    def body(p_ref, c_ref, o_ref):
        tot = p_ref[0:1, :] + p_ref[2:3, :] + p_ref[4:5, :] + p_ref[6:7, :]
        _, vjp = jax.vjp(_silu, c_ref[...])
        o_ref[...] = vjp(tot)[0]

    return pl.pallas_call(body, name="c_ctx_grad", out_shape=jax.ShapeDtypeStruct(c_ctx.shape, F32))(parts, c_ctx)


def _rope_tables(seq, ctx_rows):
    rows = seq // GRID_W
    row = jnp.repeat(jnp.arange(rows, dtype=F32), GRID_W)
    col = jnp.tile(jnp.arange(GRID_W, dtype=F32), rows)
    half = HEAD_DIM // 2
    inv_freq = ROPE_THETA ** (-jnp.arange(0, half, 2, dtype=F32) / half)
    ang = jnp.concatenate([row[:, None] * inv_freq, col[:, None] * inv_freq], axis=-1)
    cos, sin = jnp.cos(ang), jnp.sin(ang)
    cos_full = jnp.repeat(cos, 2, axis=1)
    sin_signed = jnp.stack([-sin, sin], axis=-1).reshape(seq, HEAD_DIM)
    cos_full = jnp.concatenate([jnp.ones((ctx_rows, HEAD_DIM), F32), cos_full], axis=0)
    sin_signed = jnp.concatenate([jnp.zeros((ctx_rows, HEAD_DIM), F32), sin_signed], axis=0)
    return cos_full, sin_signed


def _qk_rot(p, gain, cos_full, sin_signed):
    r = _rmsn(p) * gain
    return r * cos_full + _swap_pairs(r) * sin_signed


def _qk_rot_bwd(g, p, gain, cos_full, sin_signed):
    g1 = g * cos_full + _swap_pairs(g * sin_signed)
    _, vjp = jax.vjp(lambda pp, gn: _rmsn(pp) * gn, p, gain)
    return vjp(g1)


def kernel(x, c, ctx, c_ctx, w_ada, b_ada, ffn1_w_in, ffn1_w_out, mix_w_in, attn_q_gain, attn_k_gain, ret_decay_logit, w_proj_attn, w_proj_ret, mix_w_out, ffn2_w_in, ffn2_w_out, final_norm, loss_target, m_c_ctx, m_w_ada, m_b_ada, m_ffn1_w_in, m_ffn1_w_out, m_mix_w_in, m_attn_q_gain, m_attn_k_gain, m_ret_decay_logit, m_w_proj_attn, m_w_proj_ret, m_mix_w_out, m_ffn2_w_in, m_ffn2_w_out, m_final_norm, v_c_ctx, v_w_ada, v_b_ada, v_ffn1_w_in, v_ffn1_w_out, v_mix_w_in, v_attn_q_gain, v_attn_k_gain, v_ret_decay_logit, v_w_proj_attn, v_w_proj_ret, v_mix_w_out, v_ffn2_w_in, v_ffn2_w_out, v_final_norm):
    xi, yi, ci = lax.axis_index("x"), lax.axis_index("y"), lax.axis_index("c")
    dev = 4 * xi + 2 * yi + ci
    s_me = 2 * xi + yi
    c_arr = jnp.reshape(ci, (1,)).astype(jnp.int32)
    s_arr = jnp.reshape(s_me, (1,)).astype(jnp.int32)

    t, d = x.shape[1], x.shape[2]
    tc = ctx.shape[1]
    tk = tc + t
    ff = ffn1_w_out.shape[1] * N_CHIP
    aw = w_proj_attn.shape[1]
    rw = w_proj_ret.shape[1]
    pw = mix_w_in.shape[2] * N_CHIP
    kvw = (pw - aw - 4 * rw - 2 * d) // 2
    groups = aw // kvw
    n_ret_heads = rw // HEAD_DIM
    mod_cols = N_MOD * d
    tr = _tile(tc, 256, 32)
    nt_all, nt_x, ctx_tiles = tk // tr, t // tr, tc // tr

    c_rows = _gather_row("gather_c", c)
    cg = jnp.concatenate([c_rows, c_ctx[None, :], jnp.zeros((7, d), F32)], axis=0)
    w_ada_l = w_ada[0]
    ada_cols = w_ada_l.shape[1]
    b_ada_l = lax.dynamic_slice_in_dim(b_ada, s_me * ada_cols, ada_cols, axis=1)
    mod_shard = _ada_fwd(cg, w_ada_l, b_ada_l)
    mod_g = _all_gather8("gather_mod", mod_shard)
    mod_full = jnp.concatenate([mod_g[0], mod_g[2], mod_g[4], mod_g[6]], axis=1)
    mod_x = lax.dynamic_slice_in_dim(mod_full, dev, 1, axis=0).reshape(N_MOD, d)
    mod_c = mod_full[8].reshape(N_MOD, d)
    mods = jnp.stack([mod_c, mod_x])

    big = [("col", ffn1_w_in), ("row", ffn1_w_out), ("col", mix_w_in), ("col", w_proj_attn), ("col", w_proj_ret),
           ("row", mix_w_out), ("col", ffn2_w_in), ("row", ffn2_w_out)]
    metas = []
    for kind, w in big:
        r_l, c_l = w.shape[1:]
        metas.append(_Sharded(kind, r_l, c_l * N_CHIP) if kind == "col" else _Sharded(kind, r_l * N_CHIP, c_l))
    placed = [_place_shard(m, w[0], s_arr) for m, (_, w) in zip(metas, big)]
    layer_groups = ((0, 1), (1, 2), (2, 6), (6, 8))
    gather_sems, placed, token = _split_start(
        "gather_start", placed,
        [(lo, hi - lo, _gather_copies(metas[lo:hi], True), 3 * (hi - lo)) for lo, hi in layer_groups], mods)
    mods = mods + token[0, 0]

    def weights_of(gi, after):
        lo, hi = layer_groups[gi]
        arrived = _split_wait("gather_wait_%d" % gi, gather_sems[gi], placed[lo:hi],
                              _gather_copies(metas[lo:hi], True), after)
        return _gather_forward("gather_forward_%d" % gi, metas[lo:hi], arrived)

    cos_full, sin_signed = _rope_tables(t, tc)
    q_gain, k_gain = attn_q_gain, attn_k_gain
    log_gamma = jax.nn.log_sigmoid(ret_decay_logit[0])
    lgv = jnp.broadcast_to(log_gamma[:, :, None, None], (2, n_ret_heads, 1, HEAD_DIM))

    def norm_mod(name, h, n_tiles, off, i_shift, i_scale):
        def fn(rows, sel, fulls):
            return [_rmsn(rows[0]) * (1.0 + sel(i_scale)) + sel(i_shift)], []
        return _rowwise(name, fn, n_tiles=n_tiles, tr=tr, row_ins=[(h, 0, None)],
                        row_outs=[(h.shape[0], d, BF16, 0)], sel_in=mods, sel_off=off, ctx_rows=tc)[0]

    def resid(name, h, h_off, f, n_tiles, off, i_gate, coef):
        def fn(rows, sel, fulls):
            return [rows[0] + coef * sel(i_gate) * rows[1]], []
        return _rowwise(name, fn, n_tiles=n_tiles, tr=tr, row_ins=[(h, h_off, None), (f, 0, None)],
                        row_outs=[(f.shape[0], d, F32, 0)], sel_in=mods, sel_off=off, ctx_rows=tc)[0]

    h0 = jnp.concatenate([ctx[0], x[0]], axis=0)
    n1 = norm_mod("norm_mod1", h0, nt_all, 0, 0, 1)
    w1i, = weights_of(0, n1)
    hm1, ua1, ub1 = _mm_swiglu("ffn1_in", n1, w1i)
    w1o, = weights_of(1, hm1)
    f1 = _mm("ffn1_out", hm1, w1o, "nn", F32)
    h1 = resid("resid1", h0, 0, f1, nt_all, 0, 2, 0.5)

    n2 = norm_mod("norm_mod2", h1, nt_all, 0, 3, 4)
    wmi, wpa, wpr, wmo = weights_of(2, n2)
    p_q = _mm("mix_in_q", n2, wmi, "nn", F32, 0, aw)
    p_kv = _mm("mix_in_kv", n2, wmi, "nn", F32, aw, 2 * kvw)
    p_r = _mm("mix_in_ret", n2, wmi, "nn", F32, aw + 2 * kvw, 3 * rw)
    p_gr = _mm("mix_in_gr", n2, wmi, "nn", F32, aw + 2 * kvw + 3 * rw, rw)
    p_gab = _mm("mix_in_gab", n2, wmi, "nn", F32, aw + 2 * kvw + 4 * rw, 2 * d)

    def q_prep(rows, sel, fulls):
        p, cf, ss = rows
        return _heads_map(lambda ph: [_qk_rot(ph, fulls[0], cf, ss) * QSCALE], [p], aw), []

    q_rot = _rowwise("q_prep", q_prep, n_tiles=nt_x, tr=tr,
                     row_ins=[(p_q, ctx_tiles, None), (cos_full, ctx_tiles, None), (sin_signed, ctx_tiles, None)],
                     row_outs=[(t, aw, BF16, 0)], full_ins=[q_gain])[0]

    def kv_prep(rows, sel, fulls):
        p, cf, ss = rows
        k_rot = _heads_map(lambda ph: [_qk_rot(ph, fulls[0], cf, ss)], [p[:, :kvw]], kvw)[0]
        v_ones = _heads_map(lambda vh: [jnp.concatenate([vh, jnp.ones_like(vh)], axis=1)], [p[:, kvw:]], kvw)[0]
        return [k_rot, v_ones], []

    k_rot, v_att = _rowwise("kv_prep", kv_prep, n_tiles=nt_all, tr=tr,
                            row_ins=[(p_kv, 0, None), (cos_full, 0, None), (sin_signed, 0, None)],
                            row_outs=[(tk, kvw, BF16, 0), (tk, 2 * kvw, BF16, 0)], full_ins=[k_gain])

    ya, lse = _flash_fwd(q_rot, k_rot, v_att, groups)
    y_fwd, y_bwd, states = _ret_fwd(p_r, lgv, tc)

    def ret_out_fn(yf, yb, gr):
        return [_silu(gr) * _rmsn(yf + yb)]

    def ret_out(rows, sel, fulls):
        return _heads_map(ret_out_fn, rows, rw), []

    y_rows = [(y_fwd, ctx_tiles, None), (y_bwd, ctx_tiles, None), (p_gr, ctx_tiles, None)]
    yr = _rowwise("ret_out", ret_out, n_tiles=nt_x, tr=tr, row_ins=y_rows, row_outs=[(t, rw, BF16, 0)])[0]

    pa = _mm("proj_attn", ya, wpa, "nn", F32)
    prj = _mm("proj_ret", yr, wpr, "nn", F32)

    def merge_fn(a, r, ga, gb):
        return _sigmoid(ga) * a + _sigmoid(gb) * r

    gate_rows = [(p_gab, ctx_tiles, (d, 0)), (p_gab, ctx_tiles, (d, 1))]
    z = _rowwise("merge", lambda rows, sel, fulls: ([merge_fn(*rows)], []), n_tiles=nt_x, tr=tr,
                 row_ins=[(pa, 0, None), (prj, 0, None)] + gate_rows, row_outs=[(t, d, BF16, 0)])[0]
    fo = _mm("mix_out", z, wmo, "nn", F32)
    h2 = resid("resid2", h1, ctx_tiles, fo, nt_x, ctx_tiles, 5, 1.0)

    n3 = norm_mod("norm_mod3", h2, nt_x, ctx_tiles, 6, 7)
    w2i, w2o = weights_of(3, n3)
    hm2, ua2, ub2 = _mm_swiglu("ffn2_in", n3, w2i)
    f2 = _mm("ffn2_out", hm2, w2o, "nn", F32)
    h3 = resid("resid3", h2, 0, f2, nt_x, ctx_tiles, 8, 0.5)

    def loss_fn(rows, sel, fulls):
        h, tgt = rows
        y, vjp = jax.vjp(lambda hh, ww: _rmsn(hh) * ww, h, fulls[0])
        err = y - tgt
        dh, dw = vjp(err / d)
        return [dh], [0.5 / d * jnp.sum(err * err, axis=0, keepdims=True), dw]

    dh3, loss_acc = _rowwise("loss_head", loss_fn, n_tiles=nt_x, tr=tr,
                             row_ins=[(h3, 0, None), (loss_target[0], 0, None)], row_outs=[(t, d, F32, 0)],
                             full_ins=[final_norm[None, :]], acc_shape=(8, d), sel_off=ctx_tiles, ctx_rows=tc)
    loss_cols, g_final = loss_acc[1, 0:1], loss_acc[1, 1:2]

    def gate_bwd(name, dh, f, n_tiles, off, i_gate, coef):
        def fn(rows, sel, fulls):
            dhh, fv = rows
            return [coef * sel(i_gate) * dhh], [jnp.sum(coef * dhh * fv, axis=0, keepdims=True)]
        return _rowwise(name, fn, n_tiles=n_tiles, tr=tr, row_ins=[(dh, 0, None), (f, 0, None)],
                        row_outs=[(dh.shape[0], d, BF16, 0)], sel_in=mods, sel_off=off, ctx_rows=tc,
                        acc_shape=(8, d))

    def swiglu_bwd(name, dhm, ua, ub):
        rows_n = dhm.shape[0]
        tr_w = _tile(tr, 128, 32)

        def fn(rows, sel, fulls):
            g, a, b = rows
            _, vjp = jax.vjp(lambda aa, bb: _silu(aa) * bb, a.astype(F32), b.astype(F32))
            da, db = vjp(g)
            return [jnp.concatenate([da, db], axis=1)], []
        return _rowwise(name, fn, n_tiles=rows_n // tr_w, tr=tr_w,
                        row_ins=[(dhm, 0, None), (ua, 0, None), (ub, 0, None)],
                        row_outs=[(rows_n, 2 * ff, BF16, 0)])[0]

    def norm_mod_bwd(name, dn, h, dres, n_tiles, off, i_shift, i_scale):
        def fn(rows, sel, fulls):
            g, hh, dr = rows
            _, vjp = jax.vjp(lambda a, sh, sc: _rmsn(a) * (1.0 + sc) + sh, hh,
                             sel(i_shift), sel(i_scale))
            dh, dsh, dsc = vjp(g)
            return [dr + dh], [dsh, dsc]
        return _rowwise(name, fn, n_tiles=n_tiles, tr=tr, row_ins=[(dn, 0, None), (h, 0, None), (dres, 0, None)],
                        row_outs=[(dn.shape[0], d, F32, 0)], sel_in=mods, sel_off=off, ctx_rows=tc,
                        acc_shape=(8, d))

    df2, acc_g3 = gate_bwd("gate_bwd3", dh3, f2, nt_x, ctx_tiles, 8, 0.5)
    g_w2o = _mm("ffn2_out_dw", hm2, df2, "tn", BF16)
    dhm2 = _mm("ffn2_out_dx", df2, w2o, "nt", F32)
    du2 = swiglu_bwd("swiglu_bwd2", dhm2, ua2, ub2)
    g_w2i = _mm("ffn2_in_dw", n3, du2, "tn", BF16)
    dn3 = _mm("ffn2_in_dx", du2, w2i, "nt", F32)
    dh2, acc_n3 = norm_mod_bwd("norm_mod_bwd3", dn3, h2, dh3, nt_x, ctx_tiles, 6, 7)

    def chip_halves(lo, hi, grads_l):
        landed_l = _reduce_pair(metas[lo:hi], grads_l)
        return [_pair_sum(m, g, l, c_arr) for m, g, l in zip(metas[lo:hi], grads_l, landed_l)]

    sems_ffn2, thru_ffn2, token = _reduce_chips_start("reduce_start_ffn2", metas[6:8],
                                                      chip_halves(6, 8, [g_w2i, g_w2o]))
    mods = mods + token[0, 0]

    dfo, acc_g2 = gate_bwd("gate_bwd2", dh2, fo, nt_x, ctx_tiles, 5, 1.0)
    g_wmo = _mm("mix_out_dw", z, dfo, "tn", BF16)
    dz = _mm("mix_out_dx", dfo, wmo, "nt", F32)

    def merge_bwd(rows, sel, fulls):
        g, a, r, ga, gb = rows
        _, vjp = jax.vjp(merge_fn, a, r, ga, gb)
        da, dr, dga, dgb = vjp(g)
        return [da, dr, jnp.concatenate([dga, dgb], axis=1)], []

    dpa, dpr, dgab = _rowwise("merge_bwd", merge_bwd, n_tiles=nt_x, tr=tr,
                              row_ins=[(dz, 0, None), (pa, 0, None), (prj, 0, None)] + gate_rows,
                              row_outs=[(t, d, BF16, 0), (t, d, BF16, 0), (t, 2 * d, BF16, 0)])
    g_wpa = _mm("proj_attn_dw", ya, dpa, "tn", BF16)
    dya = _mm("proj_attn_dx", dpa, wpa, "nt", BF16)
    g_wpr = _mm("proj_ret_dw", yr, dpr, "tn", BF16)
    dyr = _mm("proj_ret_dx", dpr, wpr, "nt", F32)

    def ret_out_bwd(rows, sel, fulls):
        def per_head(g, yf, yb, gr):
            _, vjp = jax.vjp(lambda yy, gg: ret_out_fn(yy, 0.0, gg)[0], yf + yb, gr)
            return list(vjp(g))
        dy, dgr = _heads_map(per_head, rows, rw)
        return [dy, dgr], []

    dy_ret, dgr = _rowwise("ret_out_bwd", ret_out_bwd, n_tiles=nt_x, tr=tr, row_ins=[(dyr, 0, None)] + y_rows,
                           row_outs=[(t, rw, F32, 0), (t, rw, BF16, 0)])
    dy_all = jnp.concatenate([jnp.zeros((tc, rw), F32), dy_ret], axis=0)
    dp_rf, dp_rb, dlg = _ret_bwd(p_r, states, dy_all, lgv, tc)
    dp_r = _rowwise("ret_bwd_sum", lambda rows, sel, fulls: ([rows[0] + rows[1]], []), n_tiles=nt_all, tr=tr,
                    row_ins=[(dp_rf, 0, None), (dp_rb, 0, None)], row_outs=[(tk, 3 * rw, BF16, 0)])[0]

    dq_rot, dk_rot, dv_att = _flash_bwd(q_rot, k_rot, v_att, ya, dya, lse, groups)

    def q_prep_bwd(rows, sel, fulls):
        g, p, cf, ss = rows
        gain_acc = []

        def per_head(gh, ph):
            dp, dgain = _qk_rot_bwd(gh * HEAD_DIM ** -0.5, ph, fulls[0], cf, ss)
            gain_acc.append(dgain)
            return [dp]
        dp = _heads_map(per_head, [g, p], aw)[0]
        return [dp], [functools.reduce(lambda a, b: a + b, gain_acc)]

    dp_q, acc_gq = _rowwise("q_prep_bwd", q_prep_bwd, n_tiles=nt_x, tr=tr,
                            row_ins=[(dq_rot, 0, None), (p_q, ctx_tiles, None), (cos_full, ctx_tiles, None),
                                     (sin_signed, ctx_tiles, None)],
                            row_outs=[(t, aw, BF16, 0)], full_ins=[q_gain], acc_shape=(8, HEAD_DIM),
                            sel_off=ctx_tiles, ctx_rows=tc)

    def kv_prep_bwd(rows, sel, fulls):
        gk, gv, p, cf, ss = rows
        gain_acc = []

        def per_head(gh, ph):
            dp, dgain = _qk_rot_bwd(gh, ph, fulls[0], cf, ss)
            gain_acc.append(dgain)
            return [dp]
        dpk = _heads_map(per_head, [gk, p], kvw)[0]
        return [jnp.concatenate([dpk, gv], axis=1)], [functools.reduce(lambda a, b: a + b, gain_acc)]

    dp_kv, acc_gk = _rowwise("kv_prep_bwd", kv_prep_bwd, n_tiles=nt_all, tr=tr,
                             row_ins=[(dk_rot, 0, None), (dv_att, 0, None), (p_kv, 0, (kvw, 0)), (cos_full, 0, None),
                                      (sin_signed, 0, None)],
                             row_outs=[(tk, 2 * kvw, BF16, 0)], full_ins=[k_gain], acc_shape=(8, HEAD_DIM),
                             sel_off=0, ctx_rows=tc)

    def with_ctx_zeros(a):
        return jnp.concatenate([jnp.zeros((tc, a.shape[1]), a.dtype), a], axis=0)

    dp = jnp.concatenate([with_ctx_zeros(dp_q), dp_kv, dp_r, with_ctx_zeros(dgr), with_ctx_zeros(dgab)], axis=1)
    g_wmi = _mm("mix_in_dw", n2, dp, "tn", BF16)
    dn2 = _mm("mix_in_dx", dp, wmi, "nt", F32)
    dh1, acc_n2 = norm_mod_bwd("norm_mod_bwd2", dn2, h1, with_ctx_zeros(dh2), nt_all, 0, 3, 4)
    sems_mix, thru_mix, token = _reduce_chips_start("reduce_start_mix", metas[2:6],
                                                    chip_halves(2, 6, [g_wmi, g_wpa, g_wpr, g_wmo]))
    mods = mods + token[0, 0]

    df1, acc_g1 = gate_bwd("gate_bwd1", dh1, f1, nt_all, 0, 2, 0.5)
    g_w1o = _mm("ffn1_out_dw", hm1, df1, "tn", BF16)
    sems_w1o, thru_w1o, token = _reduce_chips_start("reduce_start_ffn1_out", metas[1:2], chip_halves(1, 2, [g_w1o]))
    dhm1 = _mm("ffn1_out_dx", df1, w1o, "nt", F32, after=token)
    du1 = swiglu_bwd("swiglu_bwd1", dhm1, ua1, ub1)
    g_w1i = _mm("ffn1_in_dw", n1, du1, "tn", BF16)
    sems_w1i, thru_w1i, token = _reduce_chips_start("reduce_start_ffn1_in", metas[0:1], chip_halves(0, 1, [g_w1i]))
    dn1 = _mm("ffn1_in_dx", du1, w1i, "nt", F32, after=token)
    dh0, acc_n1 = norm_mod_bwd("norm_mod_bwd1", dn1, h0, dh1, nt_all, 0, 0, 1)
    grad_x = dh0[tc:][None]

    halves, landed3 = [], []
    for name, lo, hi, sems_l, thru_l in (("reduce_wait_ffn1_in", 0, 1, sems_w1i, thru_w1i),
                                         ("reduce_wait_ffn1_out", 1, 2, sems_w1o, thru_w1o),
                                         ("reduce_wait_mix", 2, 6, sems_mix, thru_mix),
                                         ("reduce_wait_ffn2", 6, 8, sems_ffn2, thru_ffn2)):
        halves_l, landed_l = _reduce_chips_wait(name, metas[lo:hi], sems_l, thru_l, dh0)
        halves += halves_l
        landed3 += landed_l
    pieces = [_sum_pieces(m, p, l, s_arr, c_arr) for m, p, l in zip(metas, halves, landed3)]
    grads_big = _share_halves(metas, pieces)

    zero_row = jnp.zeros((1, d), F32)
    dmod_x = jnp.concatenate([acc_n1[1, 0:1], acc_n1[1, 1:2], acc_g1[1, 0:1], acc_n2[1, 0:1], acc_n2[1, 1:2],
                              acc_g2[1, 0:1], acc_n3[1, 0:1], acc_n3[1, 1:2], acc_g3[1, 0:1]], axis=1)
    dmod_c = jnp.concatenate([acc_n1[0, 0:1], acc_n1[0, 1:2], acc_g1[0, 0:1], acc_n2[0, 0:1], acc_n2[0, 1:2]]
                             + [zero_row] * 4, axis=1)
    dlg_row = jnp.pad(dlg[:, :, 0, 0].reshape(1, 2 * n_ret_heads), ((0, 0), (0, LANES_V7X - 2 * n_ret_heads)))
    packed = jnp.concatenate([dmod_x, dmod_c, acc_gq[1, 0:1], acc_gk[0, 0:1] + acc_gk[1, 0:1], dlg_row,
                              g_final, loss_cols], axis=1)
    off_gq = 2 * mod_cols
    off_gk = off_gq + LANES_V7X
    off_lg = off_gk + LANES_V7X
    off_fn = off_lg + LANES_V7X
    off_loss = off_fn + d
    gathered = _gather_row("gather_small", packed)
    logits_row = jnp.pad(ret_decay_logit.reshape(1, 2 * n_ret_heads), ((0, 0), (0, LANES_V7X - 2 * n_ret_heads)))
    totals, g_b_ada, g_decay, loss_row = _small_reduce(gathered, logits_row, mod_cols, off_lg, off_loss, d)
    loss = loss_row[0, 0]

    dm = jnp.concatenate([gathered[:, :mod_cols], totals[:, mod_cols:2 * mod_cols],
                          jnp.zeros((7, mod_cols), F32)], axis=0)
    dm_l = lax.dynamic_slice_in_dim(dm, s_me * ada_cols, ada_cols, axis=1)
    g_w_ada, da_part = _ada_bwd(cg, dm_l, w_ada_l)
    da_rows = _gather_row("gather_dc", da_part[8:9])
    g_c_ctx = _c_ctx_grad(da_rows, c_ctx[None, :])

    def as2d(a):
        return a.reshape(-1, a.shape[-1])

    grads = {
        "c_ctx": g_c_ctx, "w_ada": g_w_ada, "b_ada": g_b_ada,
        "ffn1_w_in": grads_big[0], "ffn1_w_out": grads_big[1], "mix_w_in": grads_big[2],
        "attn_q_gain": totals[:, off_gq:off_gq + HEAD_DIM], "attn_k_gain": totals[:, off_gk:off_gk + HEAD_DIM],
        "ret_decay_logit": g_decay[:, :2 * n_ret_heads],
        "w_proj_attn": grads_big[3], "w_proj_ret": grads_big[4], "mix_w_out": grads_big[5],
        "ffn2_w_in": grads_big[6], "ffn2_w_out": grads_big[7], "final_norm": totals[:, off_fn:off_fn + d],
    }
    weights = {"c_ctx": (c_ctx, m_c_ctx, v_c_ctx), "w_ada": (w_ada, m_w_ada, v_w_ada),
               "b_ada": (b_ada, m_b_ada, v_b_ada), "ffn1_w_in": (ffn1_w_in, m_ffn1_w_in, v_ffn1_w_in),
               "ffn1_w_out": (ffn1_w_out, m_ffn1_w_out, v_ffn1_w_out), "mix_w_in": (mix_w_in, m_mix_w_in, v_mix_w_in),
               "attn_q_gain": (attn_q_gain, m_attn_q_gain, v_attn_q_gain),
               "attn_k_gain": (attn_k_gain, m_attn_k_gain, v_attn_k_gain),
               "ret_decay_logit": (ret_decay_logit, m_ret_decay_logit, v_ret_decay_logit),
               "w_proj_attn": (w_proj_attn, m_w_proj_attn, v_w_proj_attn),
               "w_proj_ret": (w_proj_ret, m_w_proj_ret, v_w_proj_ret), "mix_w_out": (mix_w_out, m_mix_w_out, v_mix_w_out),
               "ffn2_w_in": (ffn2_w_in, m_ffn2_w_in, v_ffn2_w_in), "ffn2_w_out": (ffn2_w_out, m_ffn2_w_out, v_ffn2_w_out),
               "final_norm": (final_norm, m_final_norm, v_final_norm)}
    out_g, out_d, out_m, out_v = [], [], [], []
    for name, (w, m, v) in weights.items():
        shape = w.shape
        if name == "ret_decay_logit":
            w2, m2, v2 = (a.reshape(1, -1) for a in (w, m, v))
        else:
            w2, m2, v2 = as2d(w), as2d(m), as2d(v)
        g2 = grads[name].reshape(w2.shape)
        delta, new_m, new_v = _adamw(w2, g2, m2, v2)
        out_g.append(g2.reshape(shape))
        out_d.append(delta.reshape(shape))
        out_m.append(new_m.reshape(shape))
        out_v.append(new_v.reshape(shape))
    return (loss, grad_x, *out_g, *out_d, *out_m, *out_v)
```

```python
import functools
import math

import jax
import jax.numpy as jnp
from jax import lax
from jax.experimental import pallas as pl
from jax.experimental.pallas import tpu as pltpu

F32 = jnp.float32
BF16 = jnp.bfloat16

HEAD_DIM = 128
GRID_W = 64
ROPE_THETA = 10000.0
NORM_EPS = 1e-6
N_MOD = 9
RET_CHUNK = 128
ADAM_LR = 0.001
ADAM_B1 = 0.9
ADAM_B2 = 0.999
ADAM_EPS = 1e-08
ADAM_WD = 0.01
ADAM_STEP = 10

N_DEV = 8
N_CHIP = 4
LANES_V7X = 128
MXU_WIDTH_V7X = 256
VMEM_LIMIT_V7X = 52 * 1024 * 1024

NT_DIMS = (((1,), (1,)), ((), ()))
TN_DIMS = (((0,), (0,)), ((), ()))
NN_DIMS = (((1,), (0,)), ((), ()))


def _tile(n, pref, mult=LANES_V7X):
    if n <= pref:
        return n
    t = (pref // mult) * mult
    while t >= mult:
        if n % t == 0:
            return t
        t -= mult
    return n


def _params(sem):
    return pltpu.CompilerParams(dimension_semantics=sem, vmem_limit_bytes=VMEM_LIMIT_V7X)


def _sigmoid(x):
    return 1.0 / (1.0 + jnp.exp(-x))


def _silu(x):
    return x * _sigmoid(x)


def _rmsn(x):
    return x * lax.rsqrt(jnp.mean(x * x, axis=-1, keepdims=True) + NORM_EPS)


MM_VMEM_BUDGET = 36 * 1024 * 1024


def _divisor_tiles(n, cap):
    ts = [t for t in range(LANES_V7X, min(n, cap) + 1, LANES_V7X) if n % t == 0]
    return ts or [n]


def _mm_tiles(m, n, tk, out_bytes, has_acc):
    best = None
    for tm in _divisor_tiles(m, 1536):
        for tn in _divisor_tiles(n, 2560):
            need = 4 * tk * (tm + tn) + 2 * tm * tn * out_bytes + 4 * tm * tn
            if need > MM_VMEM_BUDGET:
                continue
            score = tm * tn / (tm + tn)
            for tdim in (tm, tn):
                if tdim % MXU_WIDTH_V7X:
                    score *= 0.85
            if best is None or score > best[0]:
                best = (score, tm, tn)
    return best[1], best[2]


def _mm(name, a, b, mode, out_dtype, b_off=0, n=None, after=None):
    if mode == "nn":
        m, k = a.shape
        n = b.shape[1] if n is None else n
        dims = NN_DIMS
    elif mode == "nt":
        m, k = a.shape
        n = b.shape[0]
        dims = NT_DIMS
    else:
        k, m = a.shape
        n = b.shape[1]
        dims = TN_DIMS
    tk = _tile(k, 2560) if mode != "tn" else _tile(k, 1024)
    nk = k // tk
    tm, tn = _mm_tiles(m, math.gcd(n, b_off) if b_off else n, tk, jnp.dtype(out_dtype).itemsize, nk > 1)
    joff = b_off // tn

    def body(a_ref, b_ref, *rest):
        o_ref = rest[0 if after is None else 1]
        if nk == 1:
            o_ref[...] = lax.dot_general(a_ref[...], b_ref[...], dims,
                                         preferred_element_type=F32).astype(o_ref.dtype)
            return
        acc_ref = rest[-1]
        kk = pl.program_id(2)

        @pl.when(kk == 0)
        def _():
            acc_ref[...] = jnp.zeros_like(acc_ref)

        acc_ref[...] += lax.dot_general(a_ref[...], b_ref[...], dims, preferred_element_type=F32)

        @pl.when(kk == nk - 1)
        def _():
            o_ref[...] = acc_ref[...].astype(o_ref.dtype)

    if mode == "nn":
        a_spec = pl.BlockSpec((tm, tk), lambda i, j, kk: (i, kk))
        b_spec = pl.BlockSpec((tk, tn), lambda i, j, kk: (kk, j + joff))
    elif mode == "nt":
        a_spec = pl.BlockSpec((tm, tk), lambda i, j, kk: (i, kk))
        b_spec = pl.BlockSpec((tn, tk), lambda i, j, kk: (j, kk))
    else:
        a_spec = pl.BlockSpec((tk, tm), lambda i, j, kk: (kk, i))
        b_spec = pl.BlockSpec((tk, tn), lambda i, j, kk: (kk, j))
    return pl.pallas_call(
        body, name=name, grid=(m // tm, n // tn, nk),
        in_specs=[a_spec, b_spec] + ([] if after is None else [pl.BlockSpec(memory_space=pl.ANY)]),
        out_specs=pl.BlockSpec((tm, tn), lambda i, j, kk: (i, j)),
        out_shape=jax.ShapeDtypeStruct((m, n), out_dtype),
        scratch_shapes=[pltpu.VMEM((tm, tn), F32)] if nk > 1 else [],
        compiler_params=_params(("parallel", "parallel", "arbitrary")),
    )(*((a, b) if after is None else (a, b, after)))


def _mm_swiglu(name, a, w):
    m, k = a.shape
    f = w.shape[1] // 2
    tm = _tile(m, 1024)
    tn = _tile(f, 512)
    tk = _tile(k, 2560)
    nk = k // tk
    jf = f // tn

    def body(a_ref, wa_ref, wb_ref, h_ref, ua_ref, ub_ref, acca, accb):
        kk = pl.program_id(2)

        @pl.when(kk == 0)
        def _():
            acca[...] = jnp.zeros_like(acca)
            accb[...] = jnp.zeros_like(accb)

        av = a_ref[...]
        acca[...] += jnp.dot(av, wa_ref[...], preferred_element_type=F32)
        accb[...] += jnp.dot(av, wb_ref[...], preferred_element_type=F32)

        @pl.when(kk == nk - 1)
        def _():
            ua = acca[...]
            ub = accb[...]
            h_ref[...] = (_silu(ua) * ub).astype(BF16)
            ua_ref[...] = ua.astype(BF16)
            ub_ref[...] = ub.astype(BF16)

    o_spec = pl.BlockSpec((tm, tn), lambda i, j, kk: (i, j))
    o_shape = jax.ShapeDtypeStruct((m, f), BF16)
    return pl.pallas_call(
        body, name=name, grid=(m // tm, jf, nk),
        in_specs=[pl.BlockSpec((tm, tk), lambda i, j, kk: (i, kk)),
                  pl.BlockSpec((tk, tn), lambda i, j, kk: (kk, j)),
                  pl.BlockSpec((tk, tn), lambda i, j, kk: (kk, j + jf))],
        out_specs=[o_spec, o_spec, o_spec],
        out_shape=[o_shape, o_shape, o_shape],
        scratch_shapes=[pltpu.VMEM((tm, tn), F32), pltpu.VMEM((tm, tn), F32)],
        compiler_params=_params(("parallel", "parallel", "arbitrary")),
    )(a, w, w)


def _rowwise(name, fn, *, n_tiles, tr, row_ins, row_outs, sel_in=None, sel_off=0, ctx_rows=0,
             full_ins=(), acc_shape=None):
    sr = 128 if tr % 128 == 0 else (32 if tr % 32 == 0 else tr)
    n_row, n_full, n_out = len(row_ins), len(full_ins), len(row_outs)
    has_sel = sel_in is not None
    has_acc = acc_shape is not None

    def sel_of(i):
        return jnp.where((i + sel_off) * tr < ctx_rows, 0, 1)

    def body(*refs):
        row_refs = refs[:n_row]
        pos = n_row
        sel_ref = None
        if has_sel:
            sel_ref = refs[pos]
            pos += 1
        full_refs = refs[pos:pos + n_full]
        pos += n_full
        out_refs = refs[pos:pos + n_out]
        pos += n_out
        acc_ref = refs[pos] if has_acc else None
        i = pl.program_id(0)
        if has_acc:
            first = (i == 0) | ((i + sel_off) * tr == ctx_rows)

            @pl.when(first)
            def _():
                acc_ref[...] = jnp.zeros_like(acc_ref)

        sel = (lambda kk: sel_ref[kk:kk + 1, :]) if has_sel else None
        fulls = [r[...] for r in full_refs]

        def slab(r, carry):
            rs = pl.ds(pl.multiple_of(r * sr, sr), sr)
            rows = [ref[rs, :] for ref in row_refs]
            outs, accs = fn(rows, sel, fulls)
            for o_ref, o in zip(out_refs, outs):
                o_ref[rs, :] = o.astype(o_ref.dtype)
            for kk, a in enumerate(accs):
                acc_ref[kk:kk + 1, :a.shape[1]] += a
            return carry

        lax.fori_loop(0, tr // sr, slab, 0)

    in_specs, args = [], []
    for arr, off, blk in row_ins:
        if blk is None:
            in_specs.append(pl.BlockSpec((tr, arr.shape[1]), functools.partial(lambda i, o: (i + o, 0), o=off)))
        else:
            in_specs.append(pl.BlockSpec((tr, blk[0]), functools.partial(lambda i, o, cb: (i + o, cb), o=off, cb=blk[1])))
        args.append(arr)
    if has_sel:
        in_specs.append(pl.BlockSpec((None,) + sel_in.shape[1:], lambda i: (sel_of(i), 0, 0)))
        args.append(sel_in)
    for arr in full_ins:
        in_specs.append(pl.BlockSpec(arr.shape, lambda i: (0, 0)))
        args.append(arr)
    out_specs, out_shape = [], []
    for rows, cols, dt, off in row_outs:
        out_specs.append(pl.BlockSpec((tr, cols), functools.partial(lambda i, o: (i + o, 0), o=off)))
        out_shape.append(jax.ShapeDtypeStruct((rows, cols), dt))
    if has_acc:
        out_specs.append(pl.BlockSpec((None,) + tuple(acc_shape), lambda i: (sel_of(i), 0, 0)))
        out_shape.append(jax.ShapeDtypeStruct((2,) + tuple(acc_shape), F32))
    return pl.pallas_call(
        body, name=name, grid=(n_tiles,), in_specs=in_specs, out_specs=out_specs, out_shape=out_shape,
        compiler_params=_params(("arbitrary",)),
    )(*args)


def _swap_pairs(x):
    lane = lax.broadcasted_iota(jnp.int32, x.shape, 1)
    nxt = pltpu.roll(x, x.shape[1] - 1, 1)
    prv = pltpu.roll(x, 1, 1)
    return jnp.where(lane % 2 == 0, nxt, prv)


def _heads_map(fn, arrs, width):
    outs = None
    for h in range(width // HEAD_DIM):
        sl = slice(h * HEAD_DIM, (h + 1) * HEAD_DIM)
        res = fn(*[a[:, sl] for a in arrs])
        if outs is None:
            outs = [[] for _ in res]
        for lst, r in zip(outs, res):
            lst.append(r)
    return [jnp.concatenate(lst, axis=1) if len(lst) > 1 else lst[0] for lst in outs]


QSCALE = HEAD_DIM ** -0.5 * math.log2(math.e)
LN2 = math.log(2.0)


def _lane_chunks(a):
    return [a[:, cc * LANES_V7X:(cc + 1) * LANES_V7X] for cc in range(a.shape[1] // LANES_V7X)]


def _row_bcast(col, like):
    return jnp.broadcast_to(col, like.shape)


def _flash_tiles(t, tk_all):
    return _tile(t, 1024), _tile(tk_all, 1024)


def _flash_fwd(q, k, vx, groups):
    t, aw = q.shape
    tk_all, kvw = k.shape
    kvh = kvw // HEAD_DIM
    gw = groups * HEAD_DIM
    tq, tk = _flash_tiles(t, tk_all)
    nk = tk_all // tk

    def body(q_ref, k_ref, v_ref, o_ref, lse_ref, m_sc, l_sc, acc_sc):
        j = pl.program_id(2)

        @pl.when(j == 0)
        def _():
            m_sc[...] = jnp.full_like(m_sc, -jnp.inf)
            l_sc[...] = jnp.zeros_like(l_sc)
            acc_sc[...] = jnp.zeros_like(acc_sc)

        kt = k_ref[...]
        vt = v_ref[...]
        for g in range(groups):
            sl = slice(g * HEAD_DIM, (g + 1) * HEAD_DIM)
            s = _lane_chunks(lax.dot_general(q_ref[:, sl], kt, NT_DIMS, preferred_element_type=F32))
            mx = functools.reduce(jnp.maximum, s)
            m_prev = m_sc[g]
            m_new = jnp.maximum(m_prev, _row_bcast(jnp.max(mx, axis=1, keepdims=True), mx))
            p = jnp.concatenate([jnp.exp2(sc - m_new).astype(BF16) for sc in s], axis=1)
            alpha = jnp.exp2(m_prev - m_new)
            pv = jnp.dot(p, vt, preferred_element_type=F32)
            acc_sc[g] = alpha * acc_sc[g] + pv[:, :HEAD_DIM]
            l_sc[g] = alpha * l_sc[g] + pv[:, HEAD_DIM:]
            m_sc[g] = m_new

        @pl.when(j == nk - 1)
        def _():
            for g in range(groups):
                sl = slice(g * HEAD_DIM, (g + 1) * HEAD_DIM)
                o_ref[:, sl] = (acc_sc[g] / l_sc[g]).astype(o_ref.dtype)
                lse_ref[:, sl] = m_sc[g] + jnp.log2(l_sc[g])

    qs = pl.BlockSpec((tq, gw), lambda kh, i, j: (i, kh))
    sc = pltpu.VMEM((groups, tq, HEAD_DIM), F32)
    return pl.pallas_call(
        body, name="flash_fwd", grid=(kvh, t // tq, nk),
        in_specs=[qs, pl.BlockSpec((tk, HEAD_DIM), lambda kh, i, j: (j, kh)),
                  pl.BlockSpec((tk, 2 * HEAD_DIM), lambda kh, i, j: (j, kh))],
        out_specs=[qs, qs],
        out_shape=[jax.ShapeDtypeStruct((t, aw), BF16), jax.ShapeDtypeStruct((t, aw), F32)],
        scratch_shapes=[sc, sc, sc],
        compiler_params=_params(("parallel", "parallel", "arbitrary")),
    )(q, k, vx)


def _flash_p_ds(q, kt, vt, do, lse, delta):
    s = _lane_chunks(lax.dot_general(q, kt, NT_DIMS, preferred_element_type=F32))
    dp = _lane_chunks(lax.dot_general(do, vt, NT_DIMS, preferred_element_type=F32))
    p = [jnp.exp2(sc - lse) for sc in s]
    ds = jnp.concatenate([(pc * (dc - delta)).astype(BF16) for pc, dc in zip(p, dp)], axis=1)
    return jnp.concatenate([pc.astype(BF16) for pc in p], axis=1), ds


def _flash_delta(do, o):
    prod = do.astype(F32) * o.astype(F32)
    return _row_bcast(jnp.sum(prod, axis=1, keepdims=True), prod)


def _flash_bwd(q, k, vx, o, do, lse, groups):
    t, aw = q.shape
    tk_all, kvw = k.shape
    kvh = kvw // HEAD_DIM
    gw = groups * HEAD_DIM
    tq, tk = _flash_tiles(t, tk_all)
    nq, nk = t // tq, tk_all // tk

    def body(q_ref, k_ref, v_ref, o_ref, do_ref, lse_ref, dq_ref, dk_ref, dv_ref, dq_sc, dk_acc, dv_acc):
        j = pl.program_id(1)
        i = pl.program_id(2)

        @pl.when(i == 0)
        def _():
            dk_acc[...] = jnp.zeros_like(dk_acc)
            dv_acc[...] = jnp.zeros_like(dv_acc)

        @pl.when(j == 0)
        def _():
            dq_sc[i] = jnp.zeros((groups, tq, HEAD_DIM), F32)

        kt = k_ref[...]
        vt = v_ref[:, :HEAD_DIM]
        for g in range(groups):
            sl = slice(g * HEAD_DIM, (g + 1) * HEAD_DIM)
            qv = q_ref[:, sl]
            dov = do_ref[:, sl]
            p, ds = _flash_p_ds(qv, kt, vt, dov, lse_ref[:, sl], _flash_delta(dov, o_ref[:, sl]))
            dv_acc[...] += lax.dot_general(p, dov, TN_DIMS, preferred_element_type=F32)
            dk_acc[...] += lax.dot_general(ds, qv, TN_DIMS, preferred_element_type=F32)
            dq_sc[i, g] += jnp.dot(ds, kt, preferred_element_type=F32)

        @pl.when(i == nq - 1)
        def _():
            dk_ref[...] = dk_acc[...] * LN2
            dv_ref[...] = dv_acc[...]

        @pl.when(j == nk - 1)
        def _():
            for g in range(groups):
                dq_ref[:, g * HEAD_DIM:(g + 1) * HEAD_DIM] = dq_sc[i, g]

    qs = pl.BlockSpec((tq, gw), lambda kh, j, i: (i, kh))
    ks = pl.BlockSpec((tk, HEAD_DIM), lambda kh, j, i: (j, kh))
    dq_spec = pl.BlockSpec((tq, gw), lambda kh, j, i: (jnp.where(j == nk - 1, i, 0), kh))
    return pl.pallas_call(
        body, name="flash_bwd", grid=(kvh, nk, nq),
        in_specs=[qs, ks, pl.BlockSpec((tk, 2 * HEAD_DIM), lambda kh, j, i: (j, kh)), qs, qs, qs],
        out_specs=[dq_spec, ks, ks],
        out_shape=[jax.ShapeDtypeStruct((t, aw), F32), jax.ShapeDtypeStruct((tk_all, kvw), F32),
                   jax.ShapeDtypeStruct((tk_all, kvw), F32)],
        scratch_shapes=[pltpu.VMEM((nq, groups, tq, HEAD_DIM), F32), pltpu.VMEM((tk, HEAD_DIM), F32),
                        pltpu.VMEM((tk, HEAD_DIM), F32)],
        compiler_params=_params(("parallel", "arbitrary", "arbitrary")),
    )(q, k, vx, o, do, lse)


def _bf_nn(a, b):
    return jnp.dot(a.astype(BF16), b.astype(BF16), preferred_element_type=F32)


def _bf_nt(a, b):
    return lax.dot_general(a.astype(BF16), b.astype(BF16), NT_DIMS, preferred_element_type=F32)


def _bf_tn(a, b):
    return lax.dot_general(a.astype(BF16), b.astype(BF16), TN_DIMS, preferred_element_type=F32)


@jax.custom_vjp
def _d_nn(a, b):
    return _bf_nn(a, b)


@jax.custom_vjp
def _d_nt(a, b):
    return _bf_nt(a, b)


@jax.custom_vjp
def _d_tn(a, b):
    return _bf_tn(a, b)


_d_nn.defvjp(lambda a, b: (_bf_nn(a, b), (a, b)), lambda r, g: (_d_nt(g, r[1]), _d_tn(r[0], g)))
_d_nt.defvjp(lambda a, b: (_bf_nt(a, b), (a, b)), lambda r, g: (_d_nn(g, r[1]), _d_tn(g, r[0])))
_d_tn.defvjp(lambda a, b: (_bf_tn(a, b), (a, b)), lambda r, g: (_d_nt(r[1], g), _d_nn(r[0], g)))


def _ret_chunk(q, k_raw, v, state, lg, rev, dots):
    nn, nt, tn = dots
    c = RET_CHUNK
    tcol = lax.broadcasted_iota(jnp.int32, (c, 1), 0).astype(F32)
    trow = lax.broadcasted_iota(jnp.int32, (1, c), 1).astype(F32)
    ucol = jnp.where(rev, c - 1.0 - tcol, tcol)
    urow = jnp.where(rev, c - 1.0 - trow, trow)
    e = ucol - urow
    low = e >= 0
    intra = jnp.where(low, jnp.exp(jnp.where(low, e, 0.0) * lg), 0.0)
    k = k_raw * (HEAD_DIM ** -0.5)
    inner = nt(q, k) * intra
    y = nn(inner, v) + nn(q, state) * jnp.exp((ucol + 1.0) * lg)
    new_state = state * jnp.exp(c * lg) + tn(k * jnp.exp((c - 1.0 - ucol) * lg), v)
    return y, new_state


def _ret_chunk_index(n_chunks, n_ctx_chunks):
    def idx(d, s):
        if d == 0:
            return s
        return jnp.where(s < n_ctx_chunks, n_ctx_chunks - 1 - s, n_chunks - 1 - s + n_ctx_chunks)
    return idx


def _ret_fwd(pr, lgv, ctx_rows):
    tk_all = pr.shape[0]
    rw = pr.shape[1] // 3
    nh = rw // HEAD_DIM
    nc = tk_all // RET_CHUNK
    cidx = _ret_chunk_index(nc, ctx_rows // RET_CHUNK)

    def body(pf_ref, pb_ref, lg_ref, yf_ref, yb_ref, st_ref, s_sc):
        s = pl.program_id(0)

        @pl.when(s == 0)
        def _():
            s_sc[...] = jnp.zeros_like(s_sc)

        for d, (p_ref, y_ref) in enumerate(((pf_ref, yf_ref), (pb_ref, yb_ref))):
            for h in range(nh):
                cols = [slice((part * nh + h) * HEAD_DIM, (part * nh + h + 1) * HEAD_DIM) for part in range(3)]
                state = s_sc[d, h]
                st_ref[d, h] = state
                y, new_state = _ret_chunk(p_ref[:, cols[0]], p_ref[:, cols[1]], p_ref[:, cols[2]], state,
                                          lg_ref[d, h][:, :1], d == 1, (_bf_nn, _bf_nt, _bf_tn))
                y_ref[:, h * HEAD_DIM:(h + 1) * HEAD_DIM] = y
                s_sc[d, h] = new_state

    y_shape = jax.ShapeDtypeStruct((tk_all, rw), F32)
    return pl.pallas_call(
        body, name="ret_fwd", grid=(nc,),
        in_specs=[pl.BlockSpec((RET_CHUNK, 3 * rw), lambda s: (cidx(0, s), 0)),
                  pl.BlockSpec((RET_CHUNK, 3 * rw), lambda s: (cidx(1, s), 0)),
                  pl.BlockSpec(lgv.shape, lambda s: (0, 0, 0, 0))],
        out_specs=[pl.BlockSpec((RET_CHUNK, rw), lambda s: (cidx(0, s), 0)),
                   pl.BlockSpec((RET_CHUNK, rw), lambda s: (cidx(1, s), 0)),
                   pl.BlockSpec((2, nh, None, HEAD_DIM, HEAD_DIM), lambda s: (0, 0, s, 0, 0))],
        out_shape=[y_shape, y_shape, jax.ShapeDtypeStruct((2, nh, nc, HEAD_DIM, HEAD_DIM), F32)],
        scratch_shapes=[pltpu.VMEM((2, nh, HEAD_DIM, HEAD_DIM), F32)],
        compiler_params=_params(("arbitrary",)),
    )(pr, pr, lgv)


def _ret_bwd(pr, states, dy, lgv, ctx_rows):
    tk_all = pr.shape[0]
    rw = pr.shape[1] // 3
    nh = rw // HEAD_DIM
    nc = tk_all // RET_CHUNK
    cidx = _ret_chunk_index(nc, ctx_rows // RET_CHUNK)

    def body(pf_ref, pb_ref, st_ref, dyf_ref, dyb_ref, lg_ref, dpf_ref, dpb_ref, dlg_ref, ds_sc):
        sp = pl.program_id(0)

        @pl.when(sp == 0)
        def _():
            ds_sc[...] = jnp.zeros_like(ds_sc)
            dlg_ref[...] = jnp.zeros_like(dlg_ref)

        for d, (p_ref, dy_ref, dp_ref) in enumerate(((pf_ref, dyf_ref, dpf_ref), (pb_ref, dyb_ref, dpb_ref))):
            for h in range(nh):
                cols = [slice((part * nh + h) * HEAD_DIM, (part * nh + h + 1) * HEAD_DIM) for part in range(3)]

                def step(q, k, v, state, lg, rev=(d == 1)):
                    return _ret_chunk(q, k, v, state, lg, rev, (_d_nn, _d_nt, _d_tn))

                _, vjp = jax.vjp(step, p_ref[:, cols[0]], p_ref[:, cols[1]], p_ref[:, cols[2]], st_ref[d, h],
                                 lg_ref[d, h][:, :1])
                grads = vjp((dy_ref[:, h * HEAD_DIM:(h + 1) * HEAD_DIM], ds_sc[d, h]))
                for part in range(3):
                    dp_ref[:, cols[part]] = grads[part]
                ds_sc[d, h] = grads[3]
                dlg_ref[d, h] += jnp.broadcast_to(grads[4], (1, HEAD_DIM))

    def at(d):
        return lambda sp: (cidx(d, nc - 1 - sp), 0)

    dp_shape = jax.ShapeDtypeStruct((tk_all, 3 * rw), F32)
    lg_spec = pl.BlockSpec(lgv.shape, lambda sp: (0, 0, 0, 0))
    return pl.pallas_call(
        body, name="ret_bwd", grid=(nc,),
        in_specs=[pl.BlockSpec((RET_CHUNK, 3 * rw), at(0)), pl.BlockSpec((RET_CHUNK, 3 * rw), at(1)),
                  pl.BlockSpec((2, nh, None, HEAD_DIM, HEAD_DIM), lambda sp: (0, 0, nc - 1 - sp, 0, 0)),
                  pl.BlockSpec((RET_CHUNK, rw), at(0)), pl.BlockSpec((RET_CHUNK, rw), at(1)), lg_spec],
        out_specs=[pl.BlockSpec((RET_CHUNK, 3 * rw), at(0)), pl.BlockSpec((RET_CHUNK, 3 * rw), at(1)), lg_spec],
        out_shape=[dp_shape, dp_shape, jax.ShapeDtypeStruct(lgv.shape, F32)],
        scratch_shapes=[pltpu.VMEM((2, nh, HEAD_DIM, HEAD_DIM), F32)],
        compiler_params=_params(("arbitrary",)),
    )(pr, pr, states, dy, dy, lgv)


FLIP_X, FLIP_Y, FLIP_XY, FLIP_C = (1, 0, 0), (0, 1, 0), (1, 1, 0), (0, 0, 1)
CHIP_FLIPS = ((FLIP_X, 2), (FLIP_Y, 1), (FLIP_XY, 3))


def _flip(me, mask):
    return tuple(1 - v if m else v for v, m in zip(me, mask))


def _comm(name, ins, out_shapes, plan, n_remote, n_local, aliases=None):
    n_in, n_out = len(ins), len(out_shapes)

    def body(*refs):
        in_refs = refs[:n_in]
        out_refs = refs[n_in:n_in + n_out]
        send_sems, recv_sems, local_sems = refs[n_in + n_out:]
        me = (lax.axis_index("x"), lax.axis_index("y"), lax.axis_index("c"))
        local, phases = plan(in_refs, out_refs, me)
        local_copies = [pltpu.make_async_copy(s, d, local_sems.at[i]) for i, (s, d) in enumerate(local)]
        for cp in local_copies:
            cp.start()
        sent = []
        kk = 0
        for phase in phases:
            arrivals = []
            for mask, src, dst, landing in phase:
                peer = _flip(me, mask)
                cp = pltpu.make_async_remote_copy(src_ref=src, dst_ref=dst, send_sem=send_sems.at[kk],
                                                  recv_sem=recv_sems.at[kk], device_id=peer,
                                                  device_id_type=pl.DeviceIdType.MESH)
                cp.start()
                sent.append(cp)
                arrivals.append(pltpu.make_async_remote_copy(
                    src_ref=landing, dst_ref=landing, send_sem=send_sems.at[kk], recv_sem=recv_sems.at[kk],
                    device_id=peer, device_id_type=pl.DeviceIdType.MESH))
                kk += 1
            for cp in arrivals:
                cp.wait_recv()
        for cp in sent:
            cp.wait_send()
        for cp in local_copies:
            cp.wait()

    any_spec = pl.BlockSpec(memory_space=pl.ANY)
    return pl.pallas_call(
        body, name=name,
        in_specs=[any_spec] * n_in, out_specs=[any_spec] * n_out, out_shape=list(out_shapes),
        scratch_shapes=[pltpu.SemaphoreType.DMA((n_remote,)), pltpu.SemaphoreType.DMA((n_remote,)),
                        pltpu.SemaphoreType.DMA((max(n_local, 1),))],
        input_output_aliases=aliases or {},
    )(*ins)


def _ds(start, size):
    return pl.ds(pl.multiple_of(start * size, 8), size)


def _all_gather8(name, v):
    masks = [(a, b, cc) for a in (0, 1) for b in (0, 1) for cc in (0, 1)][1:]

    def index(p):
        return 4 * p[0] + 2 * p[1] + p[2]

    def plan(in_refs, out_refs, me):
        (src,), (out,) = in_refs, out_refs
        local = [(src, out.at[index(me)])]
        phase = [(m, src, out.at[index(me)], out.at[index(_flip(me, m))]) for m in masks]
        return local, [phase]

    return _comm(name, [v], [jax.ShapeDtypeStruct((N_DEV,) + v.shape, v.dtype)], plan, len(masks), 1)[0]


def _gather_row(name, row):
    n = row.shape[1]
    n_pad = -(-n // (8 * LANES_V7X)) * (8 * LANES_V7X)
    v = jnp.pad(row, ((0, 0), (0, n_pad - n))).reshape(8, n_pad // 8)
    return _all_gather8(name, v).reshape(N_DEV, n_pad)[:, :n]


class _Sharded:
    def __init__(self, kind, rows, cols):
        self.kind, self.rows, self.cols = kind, rows, cols
        self.shard_shape = (rows, cols // N_CHIP) if kind == "col" else (rows // N_CHIP, cols)
        self.half_shape = (rows // 2, cols) if kind == "col" else (rows, cols // 2)
        self.piece_shape = (rows // 2, cols // N_CHIP) if kind == "col" else (rows // N_CHIP, cols // 2)

    def shard_of_full(self, ref, s):
        if self.kind == "col":
            return ref.at[:, _ds(s, self.cols // N_CHIP)]
        return ref.at[_ds(s, self.rows // N_CHIP), :]

    def half_of_full(self, ref, h):
        if self.kind == "col":
            return ref.at[_ds(h, self.rows // 2), :]
        return ref.at[:, _ds(h, self.cols // 2)]

    def piece_of_full(self, ref, s, h):
        if self.kind == "col":
            return ref.at[_ds(h, self.rows // 2), _ds(s, self.cols // N_CHIP)]
        return ref.at[_ds(s, self.rows // N_CHIP), _ds(h, self.cols // 2)]

    def half_of_shard(self, ref, h):
        if self.kind == "col":
            return ref.at[_ds(h, self.rows // 2), :]
        return ref.at[:, _ds(h, self.cols // 2)]

    def shard_of_half(self, ref, s):
        if self.kind == "col":
            return ref.at[:, _ds(s, self.cols // N_CHIP)]
        return ref.at[_ds(s, self.rows // N_CHIP), :]


def _place_shard(meta, w, s_arr):
    r, cols = w.shape
    tr = _tile(r, 256, 16)
    nr = r // tr

    def body(s_ref, w_ref, o_ref):
        o_ref[...] = w_ref[...].astype(BF16)

    if meta.kind == "col":
        o_map = lambda i, s_ref: (i, s_ref[0])
    else:
        o_map = lambda i, s_ref: (i + s_ref[0] * nr, 0)
    return pl.pallas_call(
        body, name="place_shard",
        grid_spec=pltpu.PrefetchScalarGridSpec(
            num_scalar_prefetch=1, grid=(nr,),
            in_specs=[pl.BlockSpec((tr, cols), lambda i, s_ref: (i, 0))],
            out_specs=pl.BlockSpec((tr, cols), o_map)),
        out_shape=jax.ShapeDtypeStruct((meta.rows, meta.cols), BF16),
        compiler_params=_params(("parallel",)),
    )(s_arr, w)


def _gather_copies(metas, over_ici):
    def copies(fulls, me):
        x, y, c = me
        s_me = 2 * x + y
        out = []
        for meta, full in zip(metas, fulls):
            for mask, bits in CHIP_FLIPS:
                s_peer = jnp.bitwise_xor(s_me, bits)
                if over_ici:
                    out.append((mask, meta.piece_of_full(full, s_me, c), meta.piece_of_full(full, s_me, c),
                                meta.piece_of_full(full, s_peer, c)))
                else:
                    out.append((FLIP_C, meta.piece_of_full(full, s_peer, c), meta.piece_of_full(full, s_peer, c),
                                meta.piece_of_full(full, s_peer, 1 - c)))
        return out
    return copies


def _gather_forward(name, metas, fulls):
    nt = len(metas)
    copies = _gather_copies(metas, False)
    outs = [jax.ShapeDtypeStruct((m.rows, m.cols), BF16) for m in metas]
    return _comm(name, list(fulls), outs, lambda ins, outs_, me: ([], [copies(outs_, me)]), 3 * nt, 0,
                 aliases={i: i for i in range(nt)})


HBM_SPEC = pl.BlockSpec(memory_space=pltpu.HBM)
SEM_SPEC = pl.BlockSpec(memory_space=pltpu.SEMAPHORE)
SPLIT_EFFECT = pltpu.SideEffectType.DATAFLOW_SIDE_EFFECTING


def _split_start(name, bufs, groups, after):
    nb, ng = len(bufs), len(groups)

    def body(*refs):
        buf_refs = refs[:nb]
        sem_refs = refs[nb + 1:nb + 1 + 2 * ng]
        token = refs[-1]
        me = (lax.axis_index("x"), lax.axis_index("y"), lax.axis_index("c"))
        for gi, (lo, n_bufs, copies, _) in enumerate(groups):
            for kk, (mask, src, dst, _) in enumerate(copies(buf_refs[lo:lo + n_bufs], me)):
                pltpu.make_async_remote_copy(src_ref=src, dst_ref=dst, send_sem=sem_refs[2 * gi].at[kk],
                                             recv_sem=sem_refs[2 * gi + 1].at[kk], device_id=_flip(me, mask),
                                             device_id_type=pl.DeviceIdType.MESH).start()
        token[...] = jnp.zeros_like(token)

    out_shape = []
    for _, _, _, n in groups:
        out_shape += [pltpu.SemaphoreType.DMA((n,)), pltpu.SemaphoreType.DMA((n,))]
    out_shape += [pltpu.HBM(b.shape, b.dtype) for b in bufs] + [jax.ShapeDtypeStruct((8, LANES_V7X), F32)]
    res = pl.pallas_call(
        body, name=name, out_shape=tuple(out_shape),
        in_specs=(HBM_SPEC,) * nb + (pl.BlockSpec(memory_space=pl.ANY),),
        out_specs=(SEM_SPEC,) * (2 * ng) + (HBM_SPEC,) * nb + (pl.BlockSpec(memory_space=pltpu.VMEM),),
        input_output_aliases={i: 2 * ng + i for i in range(nb)},
        compiler_params=pltpu.CompilerParams(has_side_effects=SPLIT_EFFECT),
    )(*[pltpu.with_memory_space_constraint(b, pltpu.HBM) for b in bufs], after)
    sems = [(res[2 * gi], res[2 * gi + 1]) for gi in range(ng)]
    return sems, list(res[2 * ng:2 * ng + nb]), res[-1]


def _split_wait(name, sems, bufs, copies, after):
    nb = len(bufs)

    def body(*refs):
        buf_refs = refs[:nb]
        send_sems, recv_sems = refs[nb], refs[nb + 1]
        me = (lax.axis_index("x"), lax.axis_index("y"), lax.axis_index("c"))
        for kk, (mask, _, _, landing) in enumerate(copies(buf_refs, me)):
            cp = pltpu.make_async_remote_copy(src_ref=landing, dst_ref=landing, send_sem=send_sems.at[kk],
                                              recv_sem=recv_sems.at[kk], device_id=_flip(me, mask),
                                              device_id_type=pl.DeviceIdType.MESH)
            cp.wait_send()
            cp.wait_recv()

    return list(pl.pallas_call(
        body, name=name, out_shape=tuple(pltpu.HBM(b.shape, b.dtype) for b in bufs),
        in_specs=(HBM_SPEC,) * nb + (SEM_SPEC, SEM_SPEC, pl.BlockSpec(memory_space=pl.ANY)),
        out_specs=(HBM_SPEC,) * nb,
        input_output_aliases={i: i for i in range(nb)},
        compiler_params=pltpu.CompilerParams(has_side_effects=SPLIT_EFFECT),
    )(*bufs, sems[0], sems[1], after))


def _reduce_pair(metas, grads):
    def plan(in_refs, out_refs, me):
        c = me[2]
        phase = [(FLIP_C, m.half_of_full(g, 1 - c), land, land) for m, g, land in zip(metas, in_refs, out_refs)]
        return [], [phase]

    outs = [jax.ShapeDtypeStruct(m.half_shape, BF16) for m in metas]
    return _comm("reduce_pair", list(grads), outs, plan, len(metas), 0)


def _reduce_chip_copies(metas):
    def copies(refs, me):
        x, y, _ = me
        s_me = 2 * x + y
        out = []
        for m, p, land in zip(metas, refs[:len(metas)], refs[len(metas):]):
            for kk, (mask, bits) in enumerate(CHIP_FLIPS):
                s_peer = jnp.bitwise_xor(s_me, bits)
                out.append((mask, m.shard_of_half(p, s_peer), land.at[kk], land.at[kk]))
        return out
    return copies


def _reduce_chips(metas, halves):
    copies = _reduce_chip_copies(metas)
    outs = [jax.ShapeDtypeStruct((3,) + m.piece_shape, BF16) for m in metas]
    return _comm("reduce_chips", list(halves), outs,
                 lambda ins, outs_, me: ([], [copies(list(ins) + list(outs_), me)]), 3 * len(metas), 0)


def _reduce_chips_start(name, metas, halves):
    lands = [lax.empty((3,) + m.piece_shape, BF16) for m in metas]
    bufs = list(halves) + lands
    sems, thru, token = _split_start(name, bufs, [(0, len(bufs), _reduce_chip_copies(metas), 3 * len(metas))],
                                     halves[0])
    return sems[0], thru, token


def _reduce_chips_wait(name, metas, sems, thru, after):
    done = _split_wait(name, sems, thru, _reduce_chip_copies(metas), after)
    return done[:len(metas)], done[len(metas):]


def _share_halves(metas, shards):
    def plan(in_refs, out_refs, me):
        c = me[2]
        phase = [(FLIP_C, m.half_of_shard(g, c), m.half_of_shard(g, c), m.half_of_shard(g, 1 - c))
                 for m, g in zip(metas, out_refs)]
        return [], [phase]

    outs = [jax.ShapeDtypeStruct(m.shard_shape, F32) for m in metas]
    return _comm("share_halves", list(shards), outs, plan, len(metas), 0,
                 aliases={i: i for i in range(len(metas))})


def _pair_sum(meta, grad, landed, c_arr):
    hr, hc = meta.half_shape
    tr = _tile(hr, 256, 16)
    tc = _tile(hc, 2048)
    nr, ncol = hr // tr, hc // tc

    def body(c_ref, g_ref, l_ref, o_ref):
        o_ref[...] = (g_ref[...].astype(F32) + l_ref[...].astype(F32)).astype(BF16)

    if meta.kind == "col":
        g_map = lambda i, j, c_ref: (i + c_ref[0] * nr, j)
    else:
        g_map = lambda i, j, c_ref: (i, j + c_ref[0] * ncol)
    blk = (tr, tc)
    return pl.pallas_call(
        body, name="pair_sum",
        grid_spec=pltpu.PrefetchScalarGridSpec(
            num_scalar_prefetch=1, grid=(nr, ncol),
            in_specs=[pl.BlockSpec(blk, g_map), pl.BlockSpec(blk, lambda i, j, c_ref: (i, j))],
            out_specs=pl.BlockSpec(blk, lambda i, j, c_ref: (i, j))),
        out_shape=jax.ShapeDtypeStruct((hr, hc), BF16),
        compiler_params=_params(("parallel", "parallel")),
    )(c_arr, grad, landed)


def _sum_pieces(meta, half, landed, s_arr, c_arr):
    pr, pc = meta.piece_shape
    tr = _tile(pr, 256, 16)
    tc = _tile(pc, 2048)
    nr, ncol = pr // tr, pc // tc

    def body(s_ref, c_ref, p_ref, l_ref, o_ref):
        acc = p_ref[...].astype(F32)
        for kk in range(3):
            acc = acc + l_ref[kk].astype(F32)
        o_ref[...] = acc

    if meta.kind == "col":
        p_map = lambda i, j, s_ref, c_ref: (i, j + s_ref[0] * ncol)
        o_map = lambda i, j, s_ref, c_ref: (i + c_ref[0] * nr, j)
    else:
        p_map = lambda i, j, s_ref, c_ref: (i + s_ref[0] * nr, j)
        o_map = lambda i, j, s_ref, c_ref: (i, j + c_ref[0] * ncol)
    blk = (tr, tc)
    return pl.pallas_call(
        body, name="sum_pieces",
        grid_spec=pltpu.PrefetchScalarGridSpec(
            num_scalar_prefetch=2, grid=(nr, ncol),
            in_specs=[pl.BlockSpec(blk, p_map), pl.BlockSpec((3,) + blk, lambda i, j, s_ref, c_ref: (0, i, j))],
            out_specs=pl.BlockSpec(blk, o_map)),
        out_shape=jax.ShapeDtypeStruct(meta.shard_shape, F32),
        compiler_params=_params(("parallel", "parallel")),
    )(s_arr, c_arr, half, landed)


def _adam_rows(rows, sel, fulls):
    w, g, m, v = rows
    m2 = ADAM_B1 * m + (1.0 - ADAM_B1) * g
    v2 = ADAM_B2 * v + (1.0 - ADAM_B2) * jnp.square(g)
    m_hat = m2 / (1.0 - ADAM_B1 ** ADAM_STEP)
    v_hat = v2 / (1.0 - ADAM_B2 ** ADAM_STEP)
    delta = -ADAM_LR * (m_hat / (jnp.sqrt(v_hat) + ADAM_EPS) + ADAM_WD * w)
    return [delta, m2, v2], []


def _adamw(w, g, m, v):
    r, c = w.shape
    tr = _tile(r, 128, 8)
    outs = _rowwise("adamw", _adam_rows, n_tiles=r // tr, tr=tr,
                    row_ins=[(w, 0, None), (g, 0, None), (m, 0, None), (v, 0, None)],
                    row_outs=[(r, c, F32, 0)] * 3)
    return outs[0], outs[1], outs[2]


def _ada_fwd(cg, w, b):
    d, n = w.shape
    tn = _tile(n, 512)

    def body(c_ref, w_ref, b_ref, o_ref):
        a = _silu(c_ref[...]).astype(BF16)
        o_ref[...] = jnp.dot(a, w_ref[...].astype(BF16), preferred_element_type=F32) + b_ref[...]

    return pl.pallas_call(
        body, name="ada_fwd", grid=(n // tn,),
        in_specs=[pl.BlockSpec(cg.shape, lambda j: (0, 0)), pl.BlockSpec((d, tn), lambda j: (0, j)),
                  pl.BlockSpec((1, tn), lambda j: (0, j))],
        out_specs=pl.BlockSpec((cg.shape[0], tn), lambda j: (0, j)),
        out_shape=jax.ShapeDtypeStruct((cg.shape[0], n), F32),
        compiler_params=_params(("parallel",)),
    )(cg, w, b)


def _ada_bwd(cg, dm, w):
    d, n = w.shape
    tn = _tile(n, 512)
    nj = n // tn

    def body(c_ref, dm_ref, w_ref, gw_ref, da_ref, acc):
        j = pl.program_id(0)

        @pl.when(j == 0)
        def _():
            acc[...] = jnp.zeros_like(acc)

        a = _silu(c_ref[...]).astype(BF16)
        dmv = dm_ref[...].astype(BF16)
        gw_ref[...] = lax.dot_general(a, dmv, TN_DIMS, preferred_element_type=F32)
        acc[...] += lax.dot_general(dmv, w_ref[...].astype(BF16), NT_DIMS, preferred_element_type=F32)

        @pl.when(j == nj - 1)
        def _():
            da_ref[...] = acc[...]

    return pl.pallas_call(
        body, name="ada_bwd", grid=(nj,),
        in_specs=[pl.BlockSpec(cg.shape, lambda j: (0, 0)), pl.BlockSpec((dm.shape[0], tn), lambda j: (0, j)),
                  pl.BlockSpec((d, tn), lambda j: (0, j))],
        out_specs=[pl.BlockSpec((d, tn), lambda j: (0, j)), pl.BlockSpec(cg.shape, lambda j: (0, 0))],
        out_shape=[jax.ShapeDtypeStruct((d, n), F32), jax.ShapeDtypeStruct(cg.shape, F32)],
        scratch_shapes=[pltpu.VMEM(cg.shape, F32)],
        compiler_params=_params(("arbitrary",)),
    )(cg, dm, w)


def _small_reduce(gathered, logits, n_mod_cols, lg_off, loss_off, loss_cols):
    npk = gathered.shape[1]

    def body(g_ref, lo_ref, tot_ref, gb_ref, gl_ref, loss_ref):
        acc = g_ref[0:1, :]
        for dd in range(1, N_DEV):
            acc = acc + g_ref[dd:dd + 1, :]
        tot_ref[...] = acc
        gb_ref[...] = acc[:, :n_mod_cols] + acc[:, n_mod_cols:2 * n_mod_cols]
        gl_ref[...] = acc[:, lg_off:lg_off + LANES_V7X] * _sigmoid(-lo_ref[...])
        loss = jnp.sum(acc[:, loss_off:loss_off + loss_cols], axis=1, keepdims=True)
        loss_ref[...] = jnp.broadcast_to(loss, loss_ref.shape)

    lane = jax.ShapeDtypeStruct((1, LANES_V7X), F32)
    return pl.pallas_call(
        body, name="small_reduce",
        out_shape=[jax.ShapeDtypeStruct((1, npk), F32), jax.ShapeDtypeStruct((1, n_mod_cols), F32), lane, lane],
    )(gathered, logits)


def _c_ctx_grad(parts, c_ctx):
    def body(p_ref, c_ref, o_ref):
        tot = p_ref[0:1, :] + p_ref[2:3, :] + p_ref[4:5, :] + p_ref[6:7, :]
        _, vjp = jax.vjp(_silu, c_ref[...])
        o_ref[...] = vjp(tot)[0]

    return pl.pallas_call(body, name="c_ctx_grad", out_shape=jax.ShapeDtypeStruct(c_ctx.shape, F32))(parts, c_ctx)


def _rope_tables(seq, ctx_rows):
    rows = seq // GRID_W
    row = jnp.repeat(jnp.arange(rows, dtype=F32), GRID_W)
    col = jnp.tile(jnp.arange(GRID_W, dtype=F32), rows)
    half = HEAD_DIM // 2
    inv_freq = ROPE_THETA ** (-jnp.arange(0, half, 2, dtype=F32) / half)
    ang = jnp.concatenate([row[:, None] * inv_freq, col[:, None] * inv_freq], axis=-1)
    cos, sin = jnp.cos(ang), jnp.sin(ang)
    cos_full = jnp.repeat(cos, 2, axis=1)
    sin_signed = jnp.stack([-sin, sin], axis=-1).reshape(seq, HEAD_DIM)
    cos_full = jnp.concatenate([jnp.ones((ctx_rows, HEAD_DIM), F32), cos_full], axis=0)
    sin_signed = jnp.concatenate([jnp.zeros((ctx_rows, HEAD_DIM), F32), sin_signed], axis=0)
    return cos_full, sin_signed


def _qk_rot(p, gain, cos_full, sin_signed):
    r = _rmsn(p) * gain
    return r * cos_full + _swap_pairs(r) * sin_signed


def _qk_rot_bwd(g, p, gain, cos_full, sin_signed):
    g1 = g * cos_full + _swap_pairs(g * sin_signed)
    _, vjp = jax.vjp(lambda pp, gn: _rmsn(pp) * gn, p, gain)
    return vjp(g1)


def kernel(x, c, ctx, c_ctx, w_ada, b_ada, ffn1_w_in, ffn1_w_out, mix_w_in, attn_q_gain, attn_k_gain, ret_decay_logit, w_proj_attn, w_proj_ret, mix_w_out, ffn2_w_in, ffn2_w_out, final_norm, loss_target, m_c_ctx, m_w_ada, m_b_ada, m_ffn1_w_in, m_ffn1_w_out, m_mix_w_in, m_attn_q_gain, m_attn_k_gain, m_ret_decay_logit, m_w_proj_attn, m_w_proj_ret, m_mix_w_out, m_ffn2_w_in, m_ffn2_w_out, m_final_norm, v_c_ctx, v_w_ada, v_b_ada, v_ffn1_w_in, v_ffn1_w_out, v_mix_w_in, v_attn_q_gain, v_attn_k_gain, v_ret_decay_logit, v_w_proj_attn, v_w_proj_ret, v_mix_w_out, v_ffn2_w_in, v_ffn2_w_out, v_final_norm):
    xi, yi, ci = lax.axis_index("x"), lax.axis_index("y"), lax.axis_index("c")
    dev = 4 * xi + 2 * yi + ci
    s_me = 2 * xi + yi
    c_arr = jnp.reshape(ci, (1,)).astype(jnp.int32)
    s_arr = jnp.reshape(s_me, (1,)).astype(jnp.int32)

    t, d = x.shape[1], x.shape[2]
    tc = ctx.shape[1]
    tk = tc + t
    ff = ffn1_w_out.shape[1] * N_CHIP
    aw = w_proj_attn.shape[1]
    rw = w_proj_ret.shape[1]
    pw = mix_w_in.shape[2] * N_CHIP
    kvw = (pw - aw - 4 * rw - 2 * d) // 2
    groups = aw // kvw
    n_ret_heads = rw // HEAD_DIM
    mod_cols = N_MOD * d
    tr = _tile(tc, 256, 32)
    nt_all, nt_x, ctx_tiles = tk // tr, t // tr, tc // tr

    c_rows = _gather_row("gather_c", c)
    cg = jnp.concatenate([c_rows, c_ctx[None, :], jnp.zeros((7, d), F32)], axis=0)
    w_ada_l = w_ada[0]
    ada_cols = w_ada_l.shape[1]
    b_ada_l = lax.dynamic_slice_in_dim(b_ada, s_me * ada_cols, ada_cols, axis=1)
    mod_shard = _ada_fwd(cg, w_ada_l, b_ada_l)
    mod_g = _all_gather8("gather_mod", mod_shard)
    mod_full = jnp.concatenate([mod_g[0], mod_g[2], mod_g[4], mod_g[6]], axis=1)
    mod_x = lax.dynamic_slice_in_dim(mod_full, dev, 1, axis=0).reshape(N_MOD, d)
    mod_c = mod_full[8].reshape(N_MOD, d)
    mods = jnp.stack([mod_c, mod_x])

    big = [("col", ffn1_w_in), ("row", ffn1_w_out), ("col", mix_w_in), ("col", w_proj_attn), ("col", w_proj_ret),
           ("row", mix_w_out), ("col", ffn2_w_in), ("row", ffn2_w_out)]
    metas = []
    for kind, w in big:
        r_l, c_l = w.shape[1:]
        metas.append(_Sharded(kind, r_l, c_l * N_CHIP) if kind == "col" else _Sharded(kind, r_l * N_CHIP, c_l))
    placed = [_place_shard(m, w[0], s_arr) for m, (_, w) in zip(metas, big)]
    layer_groups = ((0, 1), (1, 2), (2, 6), (6, 8))
    gather_sems, placed, token = _split_start(
        "gather_start", placed,
        [(lo, hi - lo, _gather_copies(metas[lo:hi], True), 3 * (hi - lo)) for lo, hi in layer_groups], mods)
    mods = mods + token[0, 0]

    def weights_of(gi, after):
        lo, hi = layer_groups[gi]
        arrived = _split_wait("gather_wait_%d" % gi, gather_sems[gi], placed[lo:hi],
                              _gather_copies(metas[lo:hi], True), after)
        return _gather_forward("gather_forward_%d" % gi, metas[lo:hi], arrived)

    cos_full, sin_signed = _rope_tables(t, tc)
    q_gain, k_gain = attn_q_gain, attn_k_gain
    log_gamma = jax.nn.log_sigmoid(ret_decay_logit[0])
    lgv = jnp.broadcast_to(log_gamma[:, :, None, None], (2, n_ret_heads, 1, HEAD_DIM))

    def norm_mod(name, h, n_tiles, off, i_shift, i_scale):
        def fn(rows, sel, fulls):
            return [_rmsn(rows[0]) * (1.0 + sel(i_scale)) + sel(i_shift)], []
        return _rowwise(name, fn, n_tiles=n_tiles, tr=tr, row_ins=[(h, 0, None)],
                        row_outs=[(h.shape[0], d, BF16, 0)], sel_in=mods, sel_off=off, ctx_rows=tc)[0]

    def resid(name, h, h_off, f, n_tiles, off, i_gate, coef):
        def fn(rows, sel, fulls):
            return [rows[0] + coef * sel(i_gate) * rows[1]], []
        return _rowwise(name, fn, n_tiles=n_tiles, tr=tr, row_ins=[(h, h_off, None), (f, 0, None)],
                        row_outs=[(f.shape[0], d, F32, 0)], sel_in=mods, sel_off=off, ctx_rows=tc)[0]

    h0 = jnp.concatenate([ctx[0], x[0]], axis=0)
    n1 = norm_mod("norm_mod1", h0, nt_all, 0, 0, 1)
    w1i, = weights_of(0, n1)
    hm1, ua1, ub1 = _mm_swiglu("ffn1_in", n1, w1i)
    w1o, = weights_of(1, hm1)
    f1 = _mm("ffn1_out", hm1, w1o, "nn", F32)
    h1 = resid("resid1", h0, 0, f1, nt_all, 0, 2, 0.5)

    n2 = norm_mod("norm_mod2", h1, nt_all, 0, 3, 4)
    wmi, wpa, wpr, wmo = weights_of(2, n2)
    p_q = _mm("mix_in_q", n2, wmi, "nn", F32, 0, aw)
    p_kv = _mm("mix_in_kv", n2, wmi, "nn", F32, aw, 2 * kvw)
    p_r = _mm("mix_in_ret", n2, wmi, "nn", F32, aw + 2 * kvw, 3 * rw)
    p_gr = _mm("mix_in_gr", n2, wmi, "nn", F32, aw + 2 * kvw + 3 * rw, rw)
    p_gab = _mm("mix_in_gab", n2, wmi, "nn", F32, aw + 2 * kvw + 4 * rw, 2 * d)

    def q_prep(rows, sel, fulls):
        p, cf, ss = rows
        return _heads_map(lambda ph: [_qk_rot(ph, fulls[0], cf, ss) * QSCALE], [p], aw), []

    q_rot = _rowwise("q_prep", q_prep, n_tiles=nt_x, tr=tr,
                     row_ins=[(p_q, ctx_tiles, None), (cos_full, ctx_tiles, None), (sin_signed, ctx_tiles, None)],
                     row_outs=[(t, aw, BF16, 0)], full_ins=[q_gain])[0]

    def kv_prep(rows, sel, fulls):
        p, cf, ss = rows
        k_rot = _heads_map(lambda ph: [_qk_rot(ph, fulls[0], cf, ss)], [p[:, :kvw]], kvw)[0]
        v_ones = _heads_map(lambda vh: [jnp.concatenate([vh, jnp.ones_like(vh)], axis=1)], [p[:, kvw:]], kvw)[0]
        return [k_rot, v_ones], []

    k_rot, v_att = _rowwise("kv_prep", kv_prep, n_tiles=nt_all, tr=tr,
                            row_ins=[(p_kv, 0, None), (cos_full, 0, None), (sin_signed, 0, None)],
                            row_outs=[(tk, kvw, BF16, 0), (tk, 2 * kvw, BF16, 0)], full_ins=[k_gain])

    ya, lse = _flash_fwd(q_rot, k_rot, v_att, groups)
    y_fwd, y_bwd, states = _ret_fwd(p_r, lgv, tc)

    def ret_out_fn(yf, yb, gr):
        return [_silu(gr) * _rmsn(yf + yb)]

    def ret_out(rows, sel, fulls):
        return _heads_map(ret_out_fn, rows, rw), []

    y_rows = [(y_fwd, ctx_tiles, None), (y_bwd, ctx_tiles, None), (p_gr, ctx_tiles, None)]
    yr = _rowwise("ret_out", ret_out, n_tiles=nt_x, tr=tr, row_ins=y_rows, row_outs=[(t, rw, BF16, 0)])[0]

    pa = _mm("proj_attn", ya, wpa, "nn", F32)
    prj = _mm("proj_ret", yr, wpr, "nn", F32)

    def merge_fn(a, r, ga, gb):
        return _sigmoid(ga) * a + _sigmoid(gb) * r

    gate_rows = [(p_gab, ctx_tiles, (d, 0)), (p_gab, ctx_tiles, (d, 1))]
    z = _rowwise("merge", lambda rows, sel, fulls: ([merge_fn(*rows)], []), n_tiles=nt_x, tr=tr,
                 row_ins=[(pa, 0, None), (prj, 0, None)] + gate_rows, row_outs=[(t, d, BF16, 0)])[0]
    fo = _mm("mix_out", z, wmo, "nn", F32)
    h2 = resid("resid2", h1, ctx_tiles, fo, nt_x, ctx_tiles, 5, 1.0)

    n3 = norm_mod("norm_mod3", h2, nt_x, ctx_tiles, 6, 7)
    w2i, w2o = weights_of(3, n3)
    hm2, ua2, ub2 = _mm_swiglu("ffn2_in", n3, w2i)
    f2 = _mm("ffn2_out", hm2, w2o, "nn", F32)
    h3 = resid("resid3", h2, 0, f2, nt_x, ctx_tiles, 8, 0.5)

    def loss_fn(rows, sel, fulls):
        h, tgt = rows
        y, vjp = jax.vjp(lambda hh, ww: _rmsn(hh) * ww, h, fulls[0])
        err = y - tgt
        dh, dw = vjp(err / d)
        return [dh], [0.5 / d * jnp.sum(err * err, axis=0, keepdims=True), dw]

    dh3, loss_acc = _rowwise("loss_head", loss_fn, n_tiles=nt_x, tr=tr,
                             row_ins=[(h3, 0, None), (loss_target[0], 0, None)], row_outs=[(t, d, F32, 0)],
                             full_ins=[final_norm[None, :]], acc_shape=(8, d), sel_off=ctx_tiles, ctx_rows=tc)
    loss_cols, g_final = loss_acc[1, 0:1], loss_acc[1, 1:2]

    def gate_bwd(name, dh, f, n_tiles, off, i_gate, coef):
        def fn(rows, sel, fulls):
            dhh, fv = rows
            return [coef * sel(i_gate) * dhh], [jnp.sum(coef * dhh * fv, axis=0, keepdims=True)]
        return _rowwise(name, fn, n_tiles=n_tiles, tr=tr, row_ins=[(dh, 0, None), (f, 0, None)],
                        row_outs=[(dh.shape[0], d, BF16, 0)], sel_in=mods, sel_off=off, ctx_rows=tc,
                        acc_shape=(8, d))

    def swiglu_bwd(name, dhm, ua, ub):
        rows_n = dhm.shape[0]
        tr_w = _tile(tr, 128, 32)

        def fn(rows, sel, fulls):
            g, a, b = rows
            _, vjp = jax.vjp(lambda aa, bb: _silu(aa) * bb, a.astype(F32), b.astype(F32))
            da, db = vjp(g)
            return [jnp.concatenate([da, db], axis=1)], []
        return _rowwise(name, fn, n_tiles=rows_n // tr_w, tr=tr_w,
                        row_ins=[(dhm, 0, None), (ua, 0, None), (ub, 0, None)],
                        row_outs=[(rows_n, 2 * ff, BF16, 0)])[0]

    def norm_mod_bwd(name, dn, h, dres, n_tiles, off, i_shift, i_scale):
        def fn(rows, sel, fulls):
            g, hh, dr = rows
            _, vjp = jax.vjp(lambda a, sh, sc: _rmsn(a) * (1.0 + sc) + sh, hh,
                             sel(i_shift), sel(i_scale))
            dh, dsh, dsc = vjp(g)
            return [dr + dh], [dsh, dsc]
        return _rowwise(name, fn, n_tiles=n_tiles, tr=tr, row_ins=[(dn, 0, None), (h, 0, None), (dres, 0, None)],
                        row_outs=[(dn.shape[0], d, F32, 0)], sel_in=mods, sel_off=off, ctx_rows=tc,
                        acc_shape=(8, d))

    df2, acc_g3 = gate_bwd("gate_bwd3", dh3, f2, nt_x, ctx_tiles, 8, 0.5)
    g_w2o = _mm("ffn2_out_dw", hm2, df2, "tn", BF16)
    dhm2 = _mm("ffn2_out_dx", df2, w2o, "nt", F32)
    du2 = swiglu_bwd("swiglu_bwd2", dhm2, ua2, ub2)
    g_w2i = _mm("ffn2_in_dw", n3, du2, "tn", BF16)
    dn3 = _mm("ffn2_in_dx", du2, w2i, "nt", F32)
    dh2, acc_n3 = norm_mod_bwd("norm_mod_bwd3", dn3, h2, dh3, nt_x, ctx_tiles, 6, 7)

    def chip_halves(lo, hi, grads_l):
        landed_l = _reduce_pair(metas[lo:hi], grads_l)
        return [_pair_sum(m, g, l, c_arr) for m, g, l in zip(metas[lo:hi], grads_l, landed_l)]

    sems_ffn2, thru_ffn2, token = _reduce_chips_start("reduce_start_ffn2", metas[6:8],
                                                      chip_halves(6, 8, [g_w2i, g_w2o]))
    mods = mods + token[0, 0]

    dfo, acc_g2 = gate_bwd("gate_bwd2", dh2, fo, nt_x, ctx_tiles, 5, 1.0)
    g_wmo = _mm("mix_out_dw", z, dfo, "tn", BF16)
    dz = _mm("mix_out_dx", dfo, wmo, "nt", F32)

    def merge_bwd(rows, sel, fulls):
        g, a, r, ga, gb = rows
        _, vjp = jax.vjp(merge_fn, a, r, ga, gb)
        da, dr, dga, dgb = vjp(g)
        return [da, dr, jnp.concatenate([dga, dgb], axis=1)], []

    dpa, dpr, dgab = _rowwise("merge_bwd", merge_bwd, n_tiles=nt_x, tr=tr,
                              row_ins=[(dz, 0, None), (pa, 0, None), (prj, 0, None)] + gate_rows,
                              row_outs=[(t, d, BF16, 0), (t, d, BF16, 0), (t, 2 * d, BF16, 0)])
    g_wpa = _mm("proj_attn_dw", ya, dpa, "tn", BF16)
    dya = _mm("proj_attn_dx", dpa, wpa, "nt", BF16)
    g_wpr = _mm("proj_ret_dw", yr, dpr, "tn", BF16)
    dyr = _mm("proj_ret_dx", dpr, wpr, "nt", F32)

    def ret_out_bwd(rows, sel, fulls):
        def per_head(g, yf, yb, gr):
            _, vjp = jax.vjp(lambda yy, gg: ret_out_fn(yy, 0.0, gg)[0], yf + yb, gr)
            return list(vjp(g))
        dy, dgr = _heads_map(per_head, rows, rw)
        return [dy, dgr], []

    dy_ret, dgr = _rowwise("ret_out_bwd", ret_out_bwd, n_tiles=nt_x, tr=tr, row_ins=[(dyr, 0, None)] + y_rows,
                           row_outs=[(t, rw, F32, 0), (t, rw, BF16, 0)])
    dy_all = jnp.concatenate([jnp.zeros((tc, rw), F32), dy_ret], axis=0)
    dp_rf, dp_rb, dlg = _ret_bwd(p_r, states, dy_all, lgv, tc)
    dp_r = _rowwise("ret_bwd_sum", lambda rows, sel, fulls: ([rows[0] + rows[1]], []), n_tiles=nt_all, tr=tr,
                    row_ins=[(dp_rf, 0, None), (dp_rb, 0, None)], row_outs=[(tk, 3 * rw, BF16, 0)])[0]

    dq_rot, dk_rot, dv_att = _flash_bwd(q_rot, k_rot, v_att, ya, dya, lse, groups)

    def q_prep_bwd(rows, sel, fulls):
        g, p, cf, ss = rows
        gain_acc = []

        def per_head(gh, ph):
            dp, dgain = _qk_rot_bwd(gh * HEAD_DIM ** -0.5, ph, fulls[0], cf, ss)
            gain_acc.append(dgain)
            return [dp]
        dp = _heads_map(per_head, [g, p], aw)[0]
        return [dp], [functools.reduce(lambda a, b: a + b, gain_acc)]

    dp_q, acc_gq = _rowwise("q_prep_bwd", q_prep_bwd, n_tiles=nt_x, tr=tr,
                            row_ins=[(dq_rot, 0, None), (p_q, ctx_tiles, None), (cos_full, ctx_tiles, None),
                                     (sin_signed, ctx_tiles, None)],
                            row_outs=[(t, aw, BF16, 0)], full_ins=[q_gain], acc_shape=(8, HEAD_DIM),
                            sel_off=ctx_tiles, ctx_rows=tc)

    def kv_prep_bwd(rows, sel, fulls):
        gk, gv, p, cf, ss = rows
        gain_acc = []

        def per_head(gh, ph):
            dp, dgain = _qk_rot_bwd(gh, ph, fulls[0], cf, ss)
            gain_acc.append(dgain)
            return [dp]
        dpk = _heads_map(per_head, [gk, p], kvw)[0]
        return [jnp.concatenate([dpk, gv], axis=1)], [functools.reduce(lambda a, b: a + b, gain_acc)]

    dp_kv, acc_gk = _rowwise("kv_prep_bwd", kv_prep_bwd, n_tiles=nt_all, tr=tr,
                             row_ins=[(dk_rot, 0, None), (dv_att, 0, None), (p_kv, 0, (kvw, 0)), (cos_full, 0, None),
                                      (sin_signed, 0, None)],
                             row_outs=[(tk, 2 * kvw, BF16, 0)], full_ins=[k_gain], acc_shape=(8, HEAD_DIM),
                             sel_off=0, ctx_rows=tc)

    def with_ctx_zeros(a):
        return jnp.concatenate([jnp.zeros((tc, a.shape[1]), a.dtype), a], axis=0)

    dp = jnp.concatenate([with_ctx_zeros(dp_q), dp_kv, dp_r, with_ctx_zeros(dgr), with_ctx_zeros(dgab)], axis=1)
    g_wmi = _mm("mix_in_dw", n2, dp, "tn", BF16)
    dn2 = _mm("mix_in_dx", dp, wmi, "nt", F32)
    dh1, acc_n2 = norm_mod_bwd("norm_mod_bwd2", dn2, h1, with_ctx_zeros(dh2), nt_all, 0, 3, 4)
    sems_mix, thru_mix, token = _reduce_chips_start("reduce_start_mix", metas[2:6],
                                                    chip_halves(2, 6, [g_wmi, g_wpa, g_wpr, g_wmo]))
    mods = mods + token[0, 0]

    df1, acc_g1 = gate_bwd("gate_bwd1", dh1, f1, nt_all, 0, 2, 0.5)
    g_w1o = _mm("ffn1_out_dw", hm1, df1, "tn", BF16)
    sems_w1o, thru_w1o, token = _reduce_chips_start("reduce_start_ffn1_out", metas[1:2], chip_halves(1, 2, [g_w1o]))
    dhm1 = _mm("ffn1_out_dx", df1, w1o, "nt", F32, after=token)
    du1 = swiglu_bwd("swiglu_bwd1", dhm1, ua1, ub1)
    g_w1i = _mm("ffn1_in_dw", n1, du1, "tn", BF16)
    sems_w1i, thru_w1i, token = _reduce_chips_start("reduce_start_ffn1_in", metas[0:1], chip_halves(0, 1, [g_w1i]))
    dn1 = _mm("ffn1_in_dx", du1, w1i, "nt", F32, after=token)
    dh0, acc_n1 = norm_mod_bwd("norm_mod_bwd1", dn1, h0, dh1, nt_all, 0, 0, 1)
    grad_x = dh0[tc:][None]

    halves, landed3 = [], []
    for name, lo, hi, sems_l, thru_l in (("reduce_wait_ffn1_in", 0, 1, sems_w1i, thru_w1i),
                                         ("reduce_wait_ffn1_out", 1, 2, sems_w1o, thru_w1o),
                                         ("reduce_wait_mix", 2, 6, sems_mix, thru_mix),
                                         ("reduce_wait_ffn2", 6, 8, sems_ffn2, thru_ffn2)):
        halves_l, landed_l = _reduce_chips_wait(name, metas[lo:hi], sems_l, thru_l, dh0)
        halves += halves_l
        landed3 += landed_l
    pieces = [_sum_pieces(m, p, l, s_arr, c_arr) for m, p, l in zip(metas, halves, landed3)]
    grads_big = _share_halves(metas, pieces)

    zero_row = jnp.zeros((1, d), F32)
    dmod_x = jnp.concatenate([acc_n1[1, 0:1], acc_n1[1, 1:2], acc_g1[1, 0:1], acc_n2[1, 0:1], acc_n2[1, 1:2],
                              acc_g2[1, 0:1], acc_n3[1, 0:1], acc_n3[1, 1:2], acc_g3[1, 0:1]], axis=1)
    dmod_c = jnp.concatenate([acc_n1[0, 0:1], acc_n1[0, 1:2], acc_g1[0, 0:1], acc_n2[0, 0:1], acc_n2[0, 1:2]]
                             + [zero_row] * 4, axis=1)
    dlg_row = jnp.pad(dlg[:, :, 0, 0].reshape(1, 2 * n_ret_heads), ((0, 0), (0, LANES_V7X - 2 * n_ret_heads)))
    packed = jnp.concatenate([dmod_x, dmod_c, acc_gq[1, 0:1], acc_gk[0, 0:1] + acc_gk[1, 0:1], dlg_row,
                              g_final, loss_cols], axis=1)
    off_gq = 2 * mod_cols
    off_gk = off_gq + LANES_V7X
    off_lg = off_gk + LANES_V7X
    off_fn = off_lg + LANES_V7X
    off_loss = off_fn + d
    gathered = _gather_row("gather_small", packed)
    logits_row = jnp.pad(ret_decay_logit.reshape(1, 2 * n_ret_heads), ((0, 0), (0, LANES_V7X - 2 * n_ret_heads)))
    totals, g_b_ada, g_decay, loss_row = _small_reduce(gathered, logits_row, mod_cols, off_lg, off_loss, d)
    loss = loss_row[0, 0]

    dm = jnp.concatenate([gathered[:, :mod_cols], totals[:, mod_cols:2 * mod_cols],
                          jnp.zeros((7, mod_cols), F32)], axis=0)
    dm_l = lax.dynamic_slice_in_dim(dm, s_me * ada_cols, ada_cols, axis=1)
    g_w_ada, da_part = _ada_bwd(cg, dm_l, w_ada_l)
    da_rows = _gather_row("gather_dc", da_part[8:9])
    g_c_ctx = _c_ctx_grad(da_rows, c_ctx[None, :])

    def as2d(a):
        return a.reshape(-1, a.shape[-1])

    grads = {
        "c_ctx": g_c_ctx, "w_ada": g_w_ada, "b_ada": g_b_ada,
        "ffn1_w_in": grads_big[0], "ffn1_w_out": grads_big[1], "mix_w_in": grads_big[2],
        "attn_q_gain": totals[:, off_gq:off_gq + HEAD_DIM], "attn_k_gain": totals[:, off_gk:off_gk + HEAD_DIM],
        "ret_decay_logit": g_decay[:, :2 * n_ret_heads],
        "w_proj_attn": grads_big[3], "w_proj_ret": grads_big[4], "mix_w_out": grads_big[5],
        "ffn2_w_in": grads_big[6], "ffn2_w_out": grads_big[7], "final_norm": totals[:, off_fn:off_fn + d],
    }
    weights = {"c_ctx": (c_ctx, m_c_ctx, v_c_ctx), "w_ada": (w_ada, m_w_ada, v_w_ada),
               "b_ada": (b_ada, m_b_ada, v_b_ada), "ffn1_w_in": (ffn1_w_in, m_ffn1_w_in, v_ffn1_w_in),
               "ffn1_w_out": (ffn1_w_out, m_ffn1_w_out, v_ffn1_w_out), "mix_w_in": (mix_w_in, m_mix_w_in, v_mix_w_in),
               "attn_q_gain": (attn_q_gain, m_attn_q_gain, v_attn_q_gain),
               "attn_k_gain": (attn_k_gain, m_attn_k_gain, v_attn_k_gain),
               "ret_decay_logit": (ret_decay_logit, m_ret_decay_logit, v_ret_decay_logit),
               "w_proj_attn": (w_proj_attn, m_w_proj_attn, v_w_proj_attn),
               "w_proj_ret": (w_proj_ret, m_w_proj_ret, v_w_proj_ret), "mix_w_out": (mix_w_out, m_mix_w_out, v_mix_w_out),
               "ffn2_w_in": (ffn2_w_in, m_ffn2_w_in, v_ffn2_w_in), "ffn2_w_out": (ffn2_w_out, m_ffn2_w_out, v_ffn2_w_out),
               "final_norm": (final_norm, m_final_norm, v_final_norm)}
    out_g, out_d, out_m, out_v = [], [], [], []
    for name, (w, m, v) in weights.items():
        shape = w.shape
        if name == "ret_decay_logit":
            w2, m2, v2 = (a.reshape(1, -1) for a in (w, m, v))
        else:
            w2, m2, v2 = as2d(w), as2d(m), as2d(v)
        g2 = grads[name].reshape(w2.shape)
        delta, new_m, new_v = _adamw(w2, g2, m2, v2)
        out_g.append(g2.reshape(shape))
        out_d.append(delta.reshape(shape))
        out_m.append(new_m.reshape(shape))
        out_v.append(new_v.reshape(shape))
    return (loss, grad_x, *out_g, *out_d, *out_m, *out_v)
```

```python
import functools
import math

import jax
import jax.numpy as jnp
from jax import lax
from jax.experimental import pallas as pl
from jax.experimental.pallas import tpu as pltpu

F32 = jnp.float32
BF16 = jnp.bfloat16

HEAD_DIM = 128
GRID_W = 64
ROPE_THETA = 10000.0
NORM_EPS = 1e-6
N_MOD = 9
RET_CHUNK = 128
ADAM_LR = 0.001
ADAM_B1 = 0.9
ADAM_B2 = 0.999
ADAM_EPS = 1e-08
ADAM_WD = 0.01
ADAM_STEP = 10

N_DEV = 8
N_CHIP = 4
LANES_V7X = 128
MXU_WIDTH_V7X = 256
VMEM_LIMIT_V7X = 52 * 1024 * 1024

NT_DIMS = (((1,), (1,)), ((), ()))
TN_DIMS = (((0,), (0,)), ((), ()))
NN_DIMS = (((1,), (0,)), ((), ()))


def _tile(n, pref, mult=LANES_V7X):
    if n <= pref:
        return n
    t = (pref // mult) * mult
    while t >= mult:
        if n % t == 0:
            return t
        t -= mult
    return n


def _params(sem):
    return pltpu.CompilerParams(dimension_semantics=sem, vmem_limit_bytes=VMEM_LIMIT_V7X)


def _sigmoid(x):
    return 1.0 / (1.0 + jnp.exp(-x))


def _silu(x):
    return x * _sigmoid(x)


def _rmsn(x):
    return x * lax.rsqrt(jnp.mean(x * x, axis=-1, keepdims=True) + NORM_EPS)


MM_VMEM_BUDGET = 36 * 1024 * 1024


def _divisor_tiles(n, cap):
    ts = [t for t in range(LANES_V7X, min(n, cap) + 1, LANES_V7X) if n % t == 0]
    return ts or [n]


def _mm_tiles(m, n, tk, out_bytes, has_acc):
    best = None
    for tm in _divisor_tiles(m, 1536):
        for tn in _divisor_tiles(n, 2560):
            need = 4 * tk * (tm + tn) + 2 * tm * tn * out_bytes + 4 * tm * tn
            if need > MM_VMEM_BUDGET:
                continue
            score = tm * tn / (tm + tn)
            for tdim in (tm, tn):
                if tdim % MXU_WIDTH_V7X:
                    score *= 0.85
            if best is None or score > best[0]:
                best = (score, tm, tn)
    return best[1], best[2]


def _mm(name, a, b, mode, out_dtype, b_off=0, n=None, after=None):
    if mode == "nn":
        m, k = a.shape
        n = b.shape[1] if n is None else n
        dims = NN_DIMS
    elif mode == "nt":
        m, k = a.shape
        n = b.shape[0]
        dims = NT_DIMS
    else:
        k, m = a.shape
        n = b.shape[1]
        dims = TN_DIMS
    tk = _tile(k, 2560) if mode != "tn" else _tile(k, 1024)
    nk = k // tk
    tm, tn = _mm_tiles(m, math.gcd(n, b_off) if b_off else n, tk, jnp.dtype(out_dtype).itemsize, nk > 1)
    joff = b_off // tn

    def body(a_ref, b_ref, *rest):
        o_ref = rest[0 if after is None else 1]
        if nk == 1:
            o_ref[...] = lax.dot_general(a_ref[...], b_ref[...], dims,
                                         preferred_element_type=F32).astype(o_ref.dtype)
            return
        acc_ref = rest[-1]
        kk = pl.program_id(2)

        @pl.when(kk == 0)
        def _():
            acc_ref[...] = jnp.zeros_like(acc_ref)

        acc_ref[...] += lax.dot_general(a_ref[...], b_ref[...], dims, preferred_element_type=F32)

        @pl.when(kk == nk - 1)
        def _():
            o_ref[...] = acc_ref[...].astype(o_ref.dtype)

    if mode == "nn":
        a_spec = pl.BlockSpec((tm, tk), lambda i, j, kk: (i, kk))
        b_spec = pl.BlockSpec((tk, tn), lambda i, j, kk: (kk, j + joff))
    elif mode == "nt":
        a_spec = pl.BlockSpec((tm, tk), lambda i, j, kk: (i, kk))
        b_spec = pl.BlockSpec((tn, tk), lambda i, j, kk: (j, kk))
    else:
        a_spec = pl.BlockSpec((tk, tm), lambda i, j, kk: (kk, i))
        b_spec = pl.BlockSpec((tk, tn), lambda i, j, kk: (kk, j))
    return pl.pallas_call(
        body, name=name, grid=(m // tm, n // tn, nk),
        in_specs=[a_spec, b_spec] + ([] if after is None else [pl.BlockSpec(memory_space=pl.ANY)]),
        out_specs=pl.BlockSpec((tm, tn), lambda i, j, kk: (i, j)),
        out_shape=jax.ShapeDtypeStruct((m, n), out_dtype),
        scratch_shapes=[pltpu.VMEM((tm, tn), F32)] if nk > 1 else [],
        compiler_params=_params(("parallel", "parallel", "arbitrary")),
    )(*((a, b) if after is None else (a, b, after)))


def _mm_swiglu(name, a, w):
    m, k = a.shape
    f = w.shape[1] // 2
    tm = _tile(m, 1024)
    tn = _tile(f, 512)
    tk = _tile(k, 2560)
    nk = k // tk
    jf = f // tn

    def body(a_ref, wa_ref, wb_ref, h_ref, ua_ref, ub_ref, acca, accb):
        kk = pl.program_id(2)

        @pl.when(kk == 0)
        def _():
            acca[...] = jnp.zeros_like(acca)
            accb[...] = jnp.zeros_like(accb)

        av = a_ref[...]
        acca[...] += jnp.dot(av, wa_ref[...], preferred_element_type=F32)
        accb[...] += jnp.dot(av, wb_ref[...], preferred_element_type=F32)

        @pl.when(kk == nk - 1)
        def _():
            ua = acca[...]
            ub = accb[...]
            h_ref[...] = (_silu(ua) * ub).astype(BF16)
            ua_ref[...] = ua.astype(BF16)
            ub_ref[...] = ub.astype(BF16)

    o_spec = pl.BlockSpec((tm, tn), lambda i, j, kk: (i, j))
    o_shape = jax.ShapeDtypeStruct((m, f), BF16)
    return pl.pallas_call(
        body, name=name, grid=(m // tm, jf, nk),
        in_specs=[pl.BlockSpec((tm, tk), lambda i, j, kk: (i, kk)),
                  pl.BlockSpec((tk, tn), lambda i, j, kk: (kk, j)),
                  pl.BlockSpec((tk, tn), lambda i, j, kk: (kk, j + jf))],
        out_specs=[o_spec, o_spec, o_spec],
        out_shape=[o_shape, o_shape, o_shape],
        scratch_shapes=[pltpu.VMEM((tm, tn), F32), pltpu.VMEM((tm, tn), F32)],
        compiler_params=_params(("parallel", "parallel", "arbitrary")),
    )(a, w, w)


def _rowwise(name, fn, *, n_tiles, tr, row_ins, row_outs, sel_in=None, sel_off=0, ctx_rows=0,
             full_ins=(), acc_shape=None):
    sr = 128 if tr % 128 == 0 else (32 if tr % 32 == 0 else tr)
    n_row, n_full, n_out = len(row_ins), len(full_ins), len(row_outs)
    has_sel = sel_in is not None
    has_acc = acc_shape is not None

    def sel_of(i):
        return jnp.where((i + sel_off) * tr < ctx_rows, 0, 1)

    def body(*refs):
        row_refs = refs[:n_row]
        pos = n_row
        sel_ref = None
        if has_sel:
            sel_ref = refs[pos]
            pos += 1
        full_refs = refs[pos:pos + n_full]
        pos += n_full
        out_refs = refs[pos:pos + n_out]
        pos += n_out
        acc_ref = refs[pos] if has_acc else None
        i = pl.program_id(0)
        if has_acc:
            first = (i == 0) | ((i + sel_off) * tr == ctx_rows)

            @pl.when(first)
            def _():
                acc_ref[...] = jnp.zeros_like(acc_ref)

        sel = (lambda kk: sel_ref[kk:kk + 1, :]) if has_sel else None
        fulls = [r[...] for r in full_refs] + [(i + sel_off) * tr < ctx_rows]

        def slab(r, carry):
            rs = pl.ds(pl.multiple_of(r * sr, sr), sr)
            rows = [ref[rs, :] for ref in row_refs]
            outs, accs = fn(rows, sel, fulls)
            for o_ref, o in zip(out_refs, outs):
                o_ref[rs, :] = o.astype(o_ref.dtype)
            for kk, a in enumerate(accs):
                acc_ref[kk:kk + 1, :a.shape[1]] += a
            return carry

        lax.fori_loop(0, tr // sr, slab, 0)

    def row_map(off, col=0):
        if off < 0:
            return lambda i: (jnp.maximum(i + off, 0), col)
        return lambda i: (i + off, col)

    in_specs, args = [], []
    for arr, off, blk in row_ins:
        if blk is None:
            in_specs.append(pl.BlockSpec((tr, arr.shape[1]), row_map(off)))
        else:
            in_specs.append(pl.BlockSpec((tr, blk[0]), row_map(off, blk[1])))
        args.append(arr)
    if has_sel:
        in_specs.append(pl.BlockSpec((None,) + sel_in.shape[1:], lambda i: (sel_of(i), 0, 0)))
        args.append(sel_in)
    for arr in full_ins:
        in_specs.append(pl.BlockSpec(arr.shape, lambda i: (0, 0)))
        args.append(arr)
    out_specs, out_shape = [], []
    for rows, cols, dt, off in row_outs:
        out_specs.append(pl.BlockSpec((tr, cols), row_map(off)))
        out_shape.append(jax.ShapeDtypeStruct((rows, cols), dt))
    if has_acc:
        out_specs.append(pl.BlockSpec((None,) + tuple(acc_shape), lambda i: (sel_of(i), 0, 0)))
        out_shape.append(jax.ShapeDtypeStruct((2,) + tuple(acc_shape), F32))
    return pl.pallas_call(
        body, name=name, grid=(n_tiles,), in_specs=in_specs, out_specs=out_specs, out_shape=out_shape,
        compiler_params=_params(("arbitrary",)),
    )(*args)


def _swap_pairs(x):
    lane = lax.broadcasted_iota(jnp.int32, x.shape, 1)
    nxt = pltpu.roll(x, x.shape[1] - 1, 1)
    prv = pltpu.roll(x, 1, 1)
    return jnp.where(lane % 2 == 0, nxt, prv)


def _heads_map(fn, arrs, width):
    outs = None
    for h in range(width // HEAD_DIM):
        sl = slice(h * HEAD_DIM, (h + 1) * HEAD_DIM)
        res = fn(*[a[:, sl] for a in arrs])
        if outs is None:
            outs = [[] for _ in res]
        for lst, r in zip(outs, res):
            lst.append(r)
    return [jnp.concatenate(lst, axis=1) if len(lst) > 1 else lst[0] for lst in outs]


QSCALE = HEAD_DIM ** -0.5 * math.log2(math.e)
LN2 = math.log(2.0)


def _lane_chunks(a):
    return [a[:, cc * LANES_V7X:(cc + 1) * LANES_V7X] for cc in range(a.shape[1] // LANES_V7X)]


def _row_bcast(col, like):
    return jnp.broadcast_to(col, like.shape)


def _flash_tiles(t, tk_all):
    return _tile(t, 1024), _tile(tk_all, 1024)


def _flash_fwd(q, k, vx, groups):
    t, aw = q.shape
    tk_all, kvw = k.shape
    kvh = kvw // HEAD_DIM
    gw = groups * HEAD_DIM
    tq, tk = _flash_tiles(t, tk_all)
    nk = tk_all // tk

    def body(q_ref, k_ref, v_ref, o_ref, lse_ref, m_sc, l_sc, acc_sc):
        j = pl.program_id(2)

        @pl.when(j == 0)
        def _():
            m_sc[...] = jnp.full_like(m_sc, -jnp.inf)
            l_sc[...] = jnp.zeros_like(l_sc)
            acc_sc[...] = jnp.zeros_like(acc_sc)

        kt = k_ref[...]
        vt = v_ref[...]
        for g in range(groups):
            sl = slice(g * HEAD_DIM, (g + 1) * HEAD_DIM)
            s = _lane_chunks(lax.dot_general(q_ref[:, sl], kt, NT_DIMS, preferred_element_type=F32))
            mx = functools.reduce(jnp.maximum, s)
            m_prev = m_sc[g]
            m_new = jnp.maximum(m_prev, _row_bcast(jnp.max(mx, axis=1, keepdims=True), mx))
            p = jnp.concatenate([jnp.exp2(sc - m_new).astype(BF16) for sc in s], axis=1)
            alpha = jnp.exp2(m_prev - m_new)
            pv = jnp.dot(p, vt, preferred_element_type=F32)
            acc_sc[g] = alpha * acc_sc[g] + pv[:, :HEAD_DIM]
            l_sc[g] = alpha * l_sc[g] + pv[:, HEAD_DIM:]
            m_sc[g] = m_new

        @pl.when(j == nk - 1)
        def _():
            for g in range(groups):
                sl = slice(g * HEAD_DIM, (g + 1) * HEAD_DIM)
                o_ref[:, sl] = (acc_sc[g] / l_sc[g]).astype(o_ref.dtype)
                lse_ref[:, sl] = m_sc[g] + jnp.log2(l_sc[g])

    qs = pl.BlockSpec((tq, gw), lambda kh, i, j: (i, kh))
    sc = pltpu.VMEM((groups, tq, HEAD_DIM), F32)
    return pl.pallas_call(
        body, name="flash_fwd", grid=(kvh, t // tq, nk),
        in_specs=[qs, pl.BlockSpec((tk, HEAD_DIM), lambda kh, i, j: (j, kh)),
                  pl.BlockSpec((tk, 2 * HEAD_DIM), lambda kh, i, j: (j, kh))],
        out_specs=[qs, qs],
        out_shape=[jax.ShapeDtypeStruct((t, aw), BF16), jax.ShapeDtypeStruct((t, aw), F32)],
        scratch_shapes=[sc, sc, sc],
        compiler_params=_params(("parallel", "parallel", "arbitrary")),
    )(q, k, vx)


def _flash_p_ds(q, kt, vt, do, lse, delta):
    s = _lane_chunks(lax.dot_general(q, kt, NT_DIMS, preferred_element_type=F32))
    dp = _lane_chunks(lax.dot_general(do, vt, NT_DIMS, preferred_element_type=F32))
    p = [jnp.exp2(sc - lse) for sc in s]
    ds = jnp.concatenate([(pc * (dc - delta)).astype(BF16) for pc, dc in zip(p, dp)], axis=1)
    return jnp.concatenate([pc.astype(BF16) for pc in p], axis=1), ds


def _flash_delta(do, o):
    prod = do.astype(F32) * o.astype(F32)
    return _row_bcast(jnp.sum(prod, axis=1, keepdims=True), prod)


def _flash_bwd(q, k, vx, o, do, lse, groups):
    t, aw = q.shape
    tk_all, kvw = k.shape
    kvh = kvw // HEAD_DIM
    gw = groups * HEAD_DIM
    tq, tk = _flash_tiles(t, tk_all)
    nq, nk = t // tq, tk_all // tk

    def body(q_ref, k_ref, v_ref, o_ref, do_ref, lse_ref, dq_ref, dk_ref, dv_ref, dq_sc, dk_acc, dv_acc):
        j = pl.program_id(1)
        i = pl.program_id(2)

        @pl.when(i == 0)
        def _():
            dk_acc[...] = jnp.zeros_like(dk_acc)
            dv_acc[...] = jnp.zeros_like(dv_acc)

        @pl.when(j == 0)
        def _():
            dq_sc[i] = jnp.zeros((groups, tq, HEAD_DIM), F32)

        kt = k_ref[...]
        vt = v_ref[:, :HEAD_DIM]
        for g in range(groups):
            sl = slice(g * HEAD_DIM, (g + 1) * HEAD_DIM)
            qv = q_ref[:, sl]
            dov = do_ref[:, sl]
            p, ds = _flash_p_ds(qv, kt, vt, dov, lse_ref[:, sl], _flash_delta(dov, o_ref[:, sl]))
            dv_acc[...] += lax.dot_general(p, dov, TN_DIMS, preferred_element_type=F32)
            dk_acc[...] += lax.dot_general(ds, qv, TN_DIMS, preferred_element_type=F32)
            dq_sc[i, g] += jnp.dot(ds, kt, preferred_element_type=F32)

        @pl.when(i == nq - 1)
        def _():
            dk_ref[...] = dk_acc[...] * LN2
            dv_ref[...] = dv_acc[...]

        @pl.when(j == nk - 1)
        def _():
            for g in range(groups):
                dq_ref[:, g * HEAD_DIM:(g + 1) * HEAD_DIM] = dq_sc[i, g]

    qs = pl.BlockSpec((tq, gw), lambda kh, j, i: (i, kh))
    ks = pl.BlockSpec((tk, HEAD_DIM), lambda kh, j, i: (j, kh))
    dq_spec = pl.BlockSpec((tq, gw), lambda kh, j, i: (jnp.where(j == nk - 1, i, 0), kh))
    return pl.pallas_call(
        body, name="flash_bwd", grid=(kvh, nk, nq),
        in_specs=[qs, ks, pl.BlockSpec((tk, 2 * HEAD_DIM), lambda kh, j, i: (j, kh)), qs, qs, qs],
        out_specs=[dq_spec, ks, ks],
        out_shape=[jax.ShapeDtypeStruct((t, aw), F32), jax.ShapeDtypeStruct((tk_all, kvw), F32),
                   jax.ShapeDtypeStruct((tk_all, kvw), F32)],
        scratch_shapes=[pltpu.VMEM((nq, groups, tq, HEAD_DIM), F32), pltpu.VMEM((tk, HEAD_DIM), F32),
                        pltpu.VMEM((tk, HEAD_DIM), F32)],
        compiler_params=_params(("parallel", "arbitrary", "arbitrary")),
    )(q, k, vx, o, do, lse)


def _bf_nn(a, b):
    return jnp.dot(a.astype(BF16), b.astype(BF16), preferred_element_type=F32)


def _bf_nt(a, b):
    return lax.dot_general(a.astype(BF16), b.astype(BF16), NT_DIMS, preferred_element_type=F32)


def _bf_tn(a, b):
    return lax.dot_general(a.astype(BF16), b.astype(BF16), TN_DIMS, preferred_element_type=F32)


@jax.custom_vjp
def _d_nn(a, b):
    return _bf_nn(a, b)


@jax.custom_vjp
def _d_nt(a, b):
    return _bf_nt(a, b)


@jax.custom_vjp
def _d_tn(a, b):
    return _bf_tn(a, b)


_d_nn.defvjp(lambda a, b: (_bf_nn(a, b), (a, b)), lambda r, g: (_d_nt(g, r[1]), _d_tn(r[0], g)))
_d_nt.defvjp(lambda a, b: (_bf_nt(a, b), (a, b)), lambda r, g: (_d_nn(g, r[1]), _d_tn(g, r[0])))
_d_tn.defvjp(lambda a, b: (_bf_tn(a, b), (a, b)), lambda r, g: (_d_nt(r[1], g), _d_nn(r[0], g)))


def _ret_chunk(q, k_raw, v, state, lg, rev, dots):
    nn, nt, tn = dots
    c = RET_CHUNK
    tcol = lax.broadcasted_iota(jnp.int32, (c, 1), 0).astype(F32)
    trow = lax.broadcasted_iota(jnp.int32, (1, c), 1).astype(F32)
    ucol = jnp.where(rev, c - 1.0 - tcol, tcol)
    urow = jnp.where(rev, c - 1.0 - trow, trow)
    e = ucol - urow
    low = e >= 0
    intra = jnp.where(low, jnp.exp(jnp.where(low, e, 0.0) * lg), 0.0)
    k = k_raw * (HEAD_DIM ** -0.5)
    inner = nt(q, k) * intra
    y = nn(inner, v) + nn(q, state) * jnp.exp((ucol + 1.0) * lg)
    new_state = state * jnp.exp(c * lg) + tn(k * jnp.exp((c - 1.0 - ucol) * lg), v)
    return y, new_state


def _ret_chunk_index(n_chunks, n_ctx_chunks):
    def idx(d, s):
        if d == 0:
            return s
        return jnp.where(s < n_ctx_chunks, n_ctx_chunks - 1 - s, n_chunks - 1 - s + n_ctx_chunks)
    return idx


def _ret_fwd(pr, lgv, ctx_rows):
    tk_all = pr.shape[0]
    rw = pr.shape[1] // 3
    nh = rw // HEAD_DIM
    nc = tk_all // RET_CHUNK
    cidx = _ret_chunk_index(nc, ctx_rows // RET_CHUNK)

    def body(pf_ref, pb_ref, lg_ref, yf_ref, yb_ref, st_ref, s_sc):
        s = pl.program_id(0)

        @pl.when(s == 0)
        def _():
            s_sc[...] = jnp.zeros_like(s_sc)

        for d, (p_ref, y_ref) in enumerate(((pf_ref, yf_ref), (pb_ref, yb_ref))):
            for h in range(nh):
                cols = [slice((part * nh + h) * HEAD_DIM, (part * nh + h + 1) * HEAD_DIM) for part in range(3)]
                state = s_sc[d, h]
                st_ref[d, h] = state
                y, new_state = _ret_chunk(p_ref[:, cols[0]], p_ref[:, cols[1]], p_ref[:, cols[2]], state,
                                          lg_ref[d, h][:, :1], d == 1, (_bf_nn, _bf_nt, _bf_tn))
                y_ref[:, h * HEAD_DIM:(h + 1) * HEAD_DIM] = y
                s_sc[d, h] = new_state

    y_shape = jax.ShapeDtypeStruct((tk_all, rw), F32)
    return pl.pallas_call(
        body, name="ret_fwd", grid=(nc,),
        in_specs=[pl.BlockSpec((RET_CHUNK, 3 * rw), lambda s: (cidx(0, s), 0)),
                  pl.BlockSpec((RET_CHUNK, 3 * rw), lambda s: (cidx(1, s), 0)),
                  pl.BlockSpec(lgv.shape, lambda s: (0, 0, 0, 0))],
        out_specs=[pl.BlockSpec((RET_CHUNK, rw), lambda s: (cidx(0, s), 0)),
                   pl.BlockSpec((RET_CHUNK, rw), lambda s: (cidx(1, s), 0)),
                   pl.BlockSpec((2, nh, None, HEAD_DIM, HEAD_DIM), lambda s: (0, 0, s, 0, 0))],
        out_shape=[y_shape, y_shape, jax.ShapeDtypeStruct((2, nh, nc, HEAD_DIM, HEAD_DIM), F32)],
        scratch_shapes=[pltpu.VMEM((2, nh, HEAD_DIM, HEAD_DIM), F32)],
        compiler_params=_params(("arbitrary",)),
    )(pr, pr, lgv)


def _ret_bwd(pr, states, dy, lgv, ctx_rows):
    tk_all = pr.shape[0]
    rw = pr.shape[1] // 3
    nh = rw // HEAD_DIM
    nc = tk_all // RET_CHUNK
    n_ctx = ctx_rows // RET_CHUNK
    cidx = _ret_chunk_index(nc, n_ctx)

    def body(pf_ref, pb_ref, st_ref, dyf_ref, dyb_ref, lg_ref, dpf_ref, dpb_ref, dlg_ref, ds_sc):
        sp = pl.program_id(0)
        on_ctx = [cidx(dd, nc - 1 - sp) < n_ctx for dd in (0, 1)]

        @pl.when(sp == 0)
        def _():
            ds_sc[...] = jnp.zeros_like(ds_sc)
            dlg_ref[...] = jnp.zeros_like(dlg_ref)

        for d, (p_ref, dy_ref, dp_ref) in enumerate(((pf_ref, dyf_ref, dpf_ref), (pb_ref, dyb_ref, dpb_ref))):
            for h in range(nh):
                cols = [slice((part * nh + h) * HEAD_DIM, (part * nh + h + 1) * HEAD_DIM) for part in range(3)]

                def step(q, k, v, state, lg, rev=(d == 1)):
                    return _ret_chunk(q, k, v, state, lg, rev, (_d_nn, _d_nt, _d_tn))

                _, vjp = jax.vjp(step, p_ref[:, cols[0]], p_ref[:, cols[1]], p_ref[:, cols[2]], st_ref[d, h],
                                 lg_ref[d, h][:, :1])
                dy_h = jnp.where(on_ctx[d], 0.0, dy_ref[:, h * HEAD_DIM:(h + 1) * HEAD_DIM])
                grads = vjp((dy_h, ds_sc[d, h]))
                for part in range(3):
                    dp_ref[:, cols[part]] = grads[part]
                ds_sc[d, h] = grads[3]
                dlg_ref[d, h] += jnp.broadcast_to(grads[4], (1, HEAD_DIM))

    def at(d):
        return lambda sp: (cidx(d, nc - 1 - sp), 0)

    def dy_at(d):
        return lambda sp: (jnp.maximum(cidx(d, nc - 1 - sp) - n_ctx, 0), 0)

    dp_shape = jax.ShapeDtypeStruct((tk_all, 3 * rw), F32)
    lg_spec = pl.BlockSpec(lgv.shape, lambda sp: (0, 0, 0, 0))
    return pl.pallas_call(
        body, name="ret_bwd", grid=(nc,),
        in_specs=[pl.BlockSpec((RET_CHUNK, 3 * rw), at(0)), pl.BlockSpec((RET_CHUNK, 3 * rw), at(1)),
                  pl.BlockSpec((2, nh, None, HEAD_DIM, HEAD_DIM), lambda sp: (0, 0, nc - 1 - sp, 0, 0)),
                  pl.BlockSpec((RET_CHUNK, rw), dy_at(0)), pl.BlockSpec((RET_CHUNK, rw), dy_at(1)), lg_spec],
        out_specs=[pl.BlockSpec((RET_CHUNK, 3 * rw), at(0)), pl.BlockSpec((RET_CHUNK, 3 * rw), at(1)), lg_spec],
        out_shape=[dp_shape, dp_shape, jax.ShapeDtypeStruct(lgv.shape, F32)],
        scratch_shapes=[pltpu.VMEM((2, nh, HEAD_DIM, HEAD_DIM), F32)],
        compiler_params=_params(("arbitrary",)),
    )(pr, pr, states, dy, dy, lgv)


FLIP_X, FLIP_Y, FLIP_XY, FLIP_C = (1, 0, 0), (0, 1, 0), (1, 1, 0), (0, 0, 1)
CHIP_FLIPS = ((FLIP_X, 2), (FLIP_Y, 1), (FLIP_XY, 3))


def _flip(me, mask):
    return tuple(1 - v if m else v for v, m in zip(me, mask))


def _comm(name, ins, out_shapes, plan, n_remote, n_local, aliases=None):
    n_in, n_out = len(ins), len(out_shapes)

    def body(*refs):
        in_refs = refs[:n_in]
        out_refs = refs[n_in:n_in + n_out]
        send_sems, recv_sems, local_sems = refs[n_in + n_out:]
        me = (lax.axis_index("x"), lax.axis_index("y"), lax.axis_index("c"))
        local, phases = plan(in_refs, out_refs, me)
        local_copies = [pltpu.make_async_copy(s, d, local_sems.at[i]) for i, (s, d) in enumerate(local)]
        for cp in local_copies:
            cp.start()
        sent = []
        kk = 0
        for phase in phases:
            arrivals = []
            for mask, src, dst, landing in phase:
                peer = _flip(me, mask)
                cp = pltpu.make_async_remote_copy(src_ref=src, dst_ref=dst, send_sem=send_sems.at[kk],
                                                  recv_sem=recv_sems.at[kk], device_id=peer,
                                                  device_id_type=pl.DeviceIdType.MESH)
                cp.start()
                sent.append(cp)
                arrivals.append(pltpu.make_async_remote_copy(
                    src_ref=landing, dst_ref=landing, send_sem=send_sems.at[kk], recv_sem=recv_sems.at[kk],
                    device_id=peer, device_id_type=pl.DeviceIdType.MESH))
                kk += 1
            for cp in arrivals:
                cp.wait_recv()
        for cp in sent:
            cp.wait_send()
        for cp in local_copies:
            cp.wait()

    any_spec = pl.BlockSpec(memory_space=pl.ANY)
    return pl.pallas_call(
        body, name=name,
        in_specs=[any_spec] * n_in, out_specs=[any_spec] * n_out, out_shape=list(out_shapes),
        scratch_shapes=[pltpu.SemaphoreType.DMA((n_remote,)), pltpu.SemaphoreType.DMA((n_remote,)),
                        pltpu.SemaphoreType.DMA((max(n_local, 1),))],
        input_output_aliases=aliases or {},
    )(*ins)


def _ds(start, size):
    return pl.ds(pl.multiple_of(start * size, 8), size)


def _all_gather8(name, v):
    masks = [(a, b, cc) for a in (0, 1) for b in (0, 1) for cc in (0, 1)][1:]

    def index(p):
        return 4 * p[0] + 2 * p[1] + p[2]

    def plan(in_refs, out_refs, me):
        (src,), (out,) = in_refs, out_refs
        local = [(src, out.at[index(me)])]
        phase = [(m, src, out.at[index(me)], out.at[index(_flip(me, m))]) for m in masks]
        return local, [phase]

    return _comm(name, [v], [jax.ShapeDtypeStruct((N_DEV,) + v.shape, v.dtype)], plan, len(masks), 1)[0]


def _gather_row(name, row):
    n = row.shape[1]
    n_pad = -(-n // (8 * LANES_V7X)) * (8 * LANES_V7X)
    v = jnp.pad(row, ((0, 0), (0, n_pad - n))).reshape(8, n_pad // 8)
    return _all_gather8(name, v).reshape(N_DEV, n_pad)[:, :n]


class _Sharded:
    def __init__(self, kind, rows, cols):
        self.kind, self.rows, self.cols = kind, rows, cols
        self.shard_shape = (rows, cols // N_CHIP) if kind == "col" else (rows // N_CHIP, cols)
        self.half_shape = (rows // 2, cols) if kind == "col" else (rows, cols // 2)
        self.piece_shape = (rows // 2, cols // N_CHIP) if kind == "col" else (rows // N_CHIP, cols // 2)

    def shard_of_full(self, ref, s):
        if self.kind == "col":
            return ref.at[:, _ds(s, self.cols // N_CHIP)]
        return ref.at[_ds(s, self.rows // N_CHIP), :]

    def half_of_full(self, ref, h):
        if self.kind == "col":
            return ref.at[_ds(h, self.rows // 2), :]
        return ref.at[:, _ds(h, self.cols // 2)]

    def piece_of_full(self, ref, s, h):
        if self.kind == "col":
            return ref.at[_ds(h, self.rows // 2), _ds(s, self.cols // N_CHIP)]
        return ref.at[_ds(s, self.rows // N_CHIP), _ds(h, self.cols // 2)]

    def half_of_shard(self, ref, h):
        if self.kind == "col":
            return ref.at[_ds(h, self.rows // 2), :]
        return ref.at[:, _ds(h, self.cols // 2)]

    def shard_of_half(self, ref, s):
        if self.kind == "col":
            return ref.at[:, _ds(s, self.cols // N_CHIP)]
        return ref.at[_ds(s, self.rows // N_CHIP), :]


def _place_shard(meta, w, s_arr):
    r, cols = w.shape
    tr = _tile(r, 256, 16)
    nr = r // tr

    def body(s_ref, w_ref, o_ref):
        o_ref[...] = w_ref[...].astype(BF16)

    if meta.kind == "col":
        o_map = lambda i, s_ref: (i, s_ref[0])
    else:
        o_map = lambda i, s_ref: (i + s_ref[0] * nr, 0)
    return pl.pallas_call(
        body, name="place_shard",
        grid_spec=pltpu.PrefetchScalarGridSpec(
            num_scalar_prefetch=1, grid=(nr,),
            in_specs=[pl.BlockSpec((tr, cols), lambda i, s_ref: (i, 0))],
            out_specs=pl.BlockSpec((tr, cols), o_map)),
        out_shape=jax.ShapeDtypeStruct((meta.rows, meta.cols), BF16),
        compiler_params=_params(("parallel",)),
    )(s_arr, w)


def _gather_copies(metas, over_ici):
    def copies(fulls, me):
        x, y, c = me
        s_me = 2 * x + y
        out = []
        for meta, full in zip(metas, fulls):
            for mask, bits in CHIP_FLIPS:
                s_peer = jnp.bitwise_xor(s_me, bits)
                if over_ici:
                    out.append((mask, meta.piece_of_full(full, s_me, c), meta.piece_of_full(full, s_me, c),
                                meta.piece_of_full(full, s_peer, c)))
                else:
                    out.append((FLIP_C, meta.piece_of_full(full, s_peer, c), meta.piece_of_full(full, s_peer, c),
                                meta.piece_of_full(full, s_peer, 1 - c)))
        return out
    return copies


def _gather_forward(name, metas, fulls):
    nt = len(metas)
    copies = _gather_copies(metas, False)
    outs = [jax.ShapeDtypeStruct((m.rows, m.cols), BF16) for m in metas]
    return _comm(name, list(fulls), outs, lambda ins, outs_, me: ([], [copies(outs_, me)]), 3 * nt, 0,
                 aliases={i: i for i in range(nt)})


HBM_SPEC = pl.BlockSpec(memory_space=pltpu.HBM)
SEM_SPEC = pl.BlockSpec(memory_space=pltpu.SEMAPHORE)
SPLIT_EFFECT = pltpu.SideEffectType.DATAFLOW_SIDE_EFFECTING


def _split_start(name, bufs, groups, after):
    nb, ng = len(bufs), len(groups)

    def body(*refs):
        buf_refs = refs[:nb]
        sem_refs = refs[nb + 1:nb + 1 + 2 * ng]
        token = refs[-1]
        me = (lax.axis_index("x"), lax.axis_index("y"), lax.axis_index("c"))
        for gi, (lo, n_bufs, copies, _) in enumerate(groups):
            for kk, (mask, src, dst, _) in enumerate(copies(buf_refs[lo:lo + n_bufs], me)):
                pltpu.make_async_remote_copy(src_ref=src, dst_ref=dst, send_sem=sem_refs[2 * gi].at[kk],
                                             recv_sem=sem_refs[2 * gi + 1].at[kk], device_id=_flip(me, mask),
                                             device_id_type=pl.DeviceIdType.MESH).start()
        token[...] = jnp.zeros_like(token)

    out_shape = []
    for _, _, _, n in groups:
        out_shape += [pltpu.SemaphoreType.DMA((n,)), pltpu.SemaphoreType.DMA((n,))]
    out_shape += [pltpu.HBM(b.shape, b.dtype) for b in bufs] + [jax.ShapeDtypeStruct((8, LANES_V7X), F32)]
    res = pl.pallas_call(
        body, name=name, out_shape=tuple(out_shape),
        in_specs=(HBM_SPEC,) * nb + (pl.BlockSpec(memory_space=pl.ANY),),
        out_specs=(SEM_SPEC,) * (2 * ng) + (HBM_SPEC,) * nb + (pl.BlockSpec(memory_space=pltpu.VMEM),),
        input_output_aliases={i: 2 * ng + i for i in range(nb)},
        compiler_params=pltpu.CompilerParams(has_side_effects=SPLIT_EFFECT),
    )(*[pltpu.with_memory_space_constraint(b, pltpu.HBM) for b in bufs], after)
    sems = [(res[2 * gi], res[2 * gi + 1]) for gi in range(ng)]
    return sems, list(res[2 * ng:2 * ng + nb]), res[-1]


def _split_wait(name, sems, bufs, copies, after):
    nb = len(bufs)

    def body(*refs):
        buf_refs = refs[:nb]
        send_sems, recv_sems = refs[nb], refs[nb + 1]
        me = (lax.axis_index("x"), lax.axis_index("y"), lax.axis_index("c"))
        for kk, (mask, _, _, landing) in enumerate(copies(buf_refs, me)):
            cp = pltpu.make_async_remote_copy(src_ref=landing, dst_ref=landing, send_sem=send_sems.at[kk],
                                              recv_sem=recv_sems.at[kk], device_id=_flip(me, mask),
                                              device_id_type=pl.DeviceIdType.MESH)
            cp.wait_send()
            cp.wait_recv()

    return list(pl.pallas_call(
        body, name=name, out_shape=tuple(pltpu.HBM(b.shape, b.dtype) for b in bufs),
        in_specs=(HBM_SPEC,) * nb + (SEM_SPEC, SEM_SPEC, pl.BlockSpec(memory_space=pl.ANY)),
        out_specs=(HBM_SPEC,) * nb,
        input_output_aliases={i: i for i in range(nb)},
        compiler_params=pltpu.CompilerParams(has_side_effects=SPLIT_EFFECT),
    )(*bufs, sems[0], sems[1], after))


N_REDUCE_PIECES = 7


def _reduce_copies(metas):
    def copies(refs, me):
        x, y, c = me
        s_me = 2 * x + y
        out = []
        for m, g, land in zip(metas, refs[:len(metas)], refs[len(metas):]):
            for kk, (mask, bits) in enumerate(CHIP_FLIPS):
                s_peer = jnp.bitwise_xor(s_me, bits)
                out.append((mask, m.piece_of_full(g, s_peer, c), land.at[kk], land.at[kk]))
                out.append((mask[:2] + (1,), m.piece_of_full(g, s_peer, 1 - c), land.at[3 + kk], land.at[3 + kk]))
            out.append((FLIP_C, m.piece_of_full(g, s_me, 1 - c), land.at[6], land.at[6]))
        return out
    return copies


def _reduce_start(name, metas, grads):
    lands = [lax.empty((N_REDUCE_PIECES,) + m.piece_shape, BF16) for m in metas]
    bufs = list(grads) + lands
    sems, thru, token = _split_start(name, bufs, [(0, len(bufs), _reduce_copies(metas),
                                                   N_REDUCE_PIECES * len(metas))], grads[0])
    return sems[0], thru, token


def _reduce_wait(name, metas, sems, thru, after):
    done = _split_wait(name, sems, thru, _reduce_copies(metas), after)
    return done[:len(metas)], done[len(metas):]


def _share_halves(metas, shards):
    def plan(in_refs, out_refs, me):
        c = me[2]
        phase = [(FLIP_C, m.half_of_shard(g, c), m.half_of_shard(g, c), m.half_of_shard(g, 1 - c))
                 for m, g in zip(metas, out_refs)]
        return [], [phase]

    outs = [jax.ShapeDtypeStruct(m.shard_shape, F32) for m in metas]
    return _comm("share_halves", list(shards), outs, plan, len(metas), 0,
                 aliases={i: i for i in range(len(metas))})


def _sum_pieces(meta, grad, landed, s_arr, c_arr):
    pr, pc = meta.piece_shape
    tr = _tile(pr, 256, 16)
    tc = _tile(pc, 2048)
    nr, ncol = pr // tr, pc // tc

    def body(s_ref, c_ref, p_ref, l_ref, o_ref):
        acc = p_ref[...].astype(F32)
        for kk in range(N_REDUCE_PIECES):
            acc = acc + l_ref[kk].astype(F32)
        o_ref[...] = acc

    if meta.kind == "col":
        p_map = lambda i, j, s_ref, c_ref: (i + c_ref[0] * nr, j + s_ref[0] * ncol)
        o_map = lambda i, j, s_ref, c_ref: (i + c_ref[0] * nr, j)
    else:
        p_map = lambda i, j, s_ref, c_ref: (i + s_ref[0] * nr, j + c_ref[0] * ncol)
        o_map = lambda i, j, s_ref, c_ref: (i, j + c_ref[0] * ncol)
    blk = (tr, tc)
    return pl.pallas_call(
        body, name="sum_pieces",
        grid_spec=pltpu.PrefetchScalarGridSpec(
            num_scalar_prefetch=2, grid=(nr, ncol),
            in_specs=[pl.BlockSpec(blk, p_map),
                      pl.BlockSpec((N_REDUCE_PIECES,) + blk, lambda i, j, s_ref, c_ref: (0, i, j))],
            out_specs=pl.BlockSpec(blk, o_map)),
        out_shape=jax.ShapeDtypeStruct(meta.shard_shape, F32),
        compiler_params=_params(("parallel", "parallel")),
    )(s_arr, c_arr, grad, landed)


def _adam_rows(rows, sel, fulls):
    w, g, m, v = rows
    m2 = ADAM_B1 * m + (1.0 - ADAM_B1) * g
    v2 = ADAM_B2 * v + (1.0 - ADAM_B2) * jnp.square(g)
    m_hat = m2 / (1.0 - ADAM_B1 ** ADAM_STEP)
    v_hat = v2 / (1.0 - ADAM_B2 ** ADAM_STEP)
    delta = -ADAM_LR * (m_hat / (jnp.sqrt(v_hat) + ADAM_EPS) + ADAM_WD * w)
    return [delta, m2, v2], []


def _adamw(w, g, m, v):
    r, c = w.shape
    tr = _tile(r, 128, 8)
    outs = _rowwise("adamw", _adam_rows, n_tiles=r // tr, tr=tr,
                    row_ins=[(w, 0, None), (g, 0, None), (m, 0, None), (v, 0, None)],
                    row_outs=[(r, c, F32, 0)] * 3)
    return outs[0], outs[1], outs[2]


def _ada_fwd(cg, w, b):
    d, n = w.shape
    tn = _tile(n, 512)

    def body(c_ref, w_ref, b_ref, o_ref):
        a = _silu(c_ref[...]).astype(BF16)
        o_ref[...] = jnp.dot(a, w_ref[...].astype(BF16), preferred_element_type=F32) + b_ref[...]

    return pl.pallas_call(
        body, name="ada_fwd", grid=(n // tn,),
        in_specs=[pl.BlockSpec(cg.shape, lambda j: (0, 0)), pl.BlockSpec((d, tn), lambda j: (0, j)),
                  pl.BlockSpec((1, tn), lambda j: (0, j))],
        out_specs=pl.BlockSpec((cg.shape[0], tn), lambda j: (0, j)),
        out_shape=jax.ShapeDtypeStruct((cg.shape[0], n), F32),
        compiler_params=_params(("parallel",)),
    )(cg, w, b)


def _ada_bwd(cg, dm, w):
    d, n = w.shape
    tn = _tile(n, 512)
    nj = n // tn

    def body(c_ref, dm_ref, w_ref, gw_ref, da_ref, acc):
        j = pl.program_id(0)

        @pl.when(j == 0)
        def _():
            acc[...] = jnp.zeros_like(acc)

        a = _silu(c_ref[...]).astype(BF16)
        dmv = dm_ref[...].astype(BF16)
        gw_ref[...] = lax.dot_general(a, dmv, TN_DIMS, preferred_element_type=F32)
        acc[...] += lax.dot_general(dmv, w_ref[...].astype(BF16), NT_DIMS, preferred_element_type=F32)

        @pl.when(j == nj - 1)
        def _():
            da_ref[...] = acc[...]

    return pl.pallas_call(
        body, name="ada_bwd", grid=(nj,),
        in_specs=[pl.BlockSpec(cg.shape, lambda j: (0, 0)), pl.BlockSpec((dm.shape[0], tn), lambda j: (0, j)),
                  pl.BlockSpec((d, tn), lambda j: (0, j))],
        out_specs=[pl.BlockSpec((d, tn), lambda j: (0, j)), pl.BlockSpec(cg.shape, lambda j: (0, 0))],
        out_shape=[jax.ShapeDtypeStruct((d, n), F32), jax.ShapeDtypeStruct(cg.shape, F32)],
        scratch_shapes=[pltpu.VMEM(cg.shape, F32)],
        compiler_params=_params(("arbitrary",)),
    )(cg, dm, w)


def _small_reduce(gathered, logits, n_mod_cols, lg_off, loss_off, loss_cols):
    npk = gathered.shape[1]

    def body(g_ref, lo_ref, tot_ref, gb_ref, gl_ref, loss_ref):
        acc = g_ref[0:1, :]
        for dd in range(1, N_DEV):
            acc = acc + g_ref[dd:dd + 1, :]
        tot_ref[...] = acc
        gb_ref[...] = acc[:, :n_mod_cols] + acc[:, n_mod_cols:2 * n_mod_cols]
        gl_ref[...] = acc[:, lg_off:lg_off + LANES_V7X] * _sigmoid(-lo_ref[...])
        loss = jnp.sum(acc[:, loss_off:loss_off + loss_cols], axis=1, keepdims=True)
        loss_ref[...] = jnp.broadcast_to(loss, loss_ref.shape)

    lane = jax.ShapeDtypeStruct((1, LANES_V7X), F32)
    return pl.pallas_call(
        body, name="small_reduce",
        out_shape=[jax.ShapeDtypeStruct((1, npk), F32), jax.ShapeDtypeStruct((1, n_mod_cols), F32), lane, lane],
    )(gathered, logits)


def _c_ctx_grad(parts, c_ctx):
    def body(p_ref, c_ref, o_ref):
        tot = p_ref[0:1, :] + p_ref[2:3, :] + p_ref[4:5, :] + p_ref[6:7, :]
        _, vjp = jax.vjp(_silu, c_ref[...])
        o_ref[...] = vjp(tot)[0]

    return pl.pallas_call(body, name="c_ctx_grad", out_shape=jax.ShapeDtypeStruct(c_ctx.shape, F32))(parts, c_ctx)


def _rope_tables(seq, ctx_rows):
    rows = seq // GRID_W
    row = jnp.repeat(jnp.arange(rows, dtype=F32), GRID_W)
    col = jnp.tile(jnp.arange(GRID_W, dtype=F32), rows)
    half = HEAD_DIM // 2
    inv_freq = ROPE_THETA ** (-jnp.arange(0, half, 2, dtype=F32) / half)
    ang = jnp.concatenate([row[:, None] * inv_freq, col[:, None] * inv_freq], axis=-1)
    cos, sin = jnp.cos(ang), jnp.sin(ang)
    cos_full = jnp.repeat(cos, 2, axis=1)
    sin_signed = jnp.stack([-sin, sin], axis=-1).reshape(seq, HEAD_DIM)
    cos_full = jnp.concatenate([jnp.ones((ctx_rows, HEAD_DIM), F32), cos_full], axis=0)
    sin_signed = jnp.concatenate([jnp.zeros((ctx_rows, HEAD_DIM), F32), sin_signed], axis=0)
    return cos_full, sin_signed


def _qk_rot(p, gain, cos_full, sin_signed):
    r = _rmsn(p) * gain
    return r * cos_full + _swap_pairs(r) * sin_signed


def _qk_rot_bwd(g, p, gain, cos_full, sin_signed):
    g1 = g * cos_full + _swap_pairs(g * sin_signed)
    _, vjp = jax.vjp(lambda pp, gn: _rmsn(pp) * gn, p, gain)
    return vjp(g1)


def kernel(x, c, ctx, c_ctx, w_ada, b_ada, ffn1_w_in, ffn1_w_out, mix_w_in, attn_q_gain, attn_k_gain, ret_decay_logit, w_proj_attn, w_proj_ret, mix_w_out, ffn2_w_in, ffn2_w_out, final_norm, loss_target, m_c_ctx, m_w_ada, m_b_ada, m_ffn1_w_in, m_ffn1_w_out, m_mix_w_in, m_attn_q_gain, m_attn_k_gain, m_ret_decay_logit, m_w_proj_attn, m_w_proj_ret, m_mix_w_out, m_ffn2_w_in, m_ffn2_w_out, m_final_norm, v_c_ctx, v_w_ada, v_b_ada, v_ffn1_w_in, v_ffn1_w_out, v_mix_w_in, v_attn_q_gain, v_attn_k_gain, v_ret_decay_logit, v_w_proj_attn, v_w_proj_ret, v_mix_w_out, v_ffn2_w_in, v_ffn2_w_out, v_final_norm):
    xi, yi, ci = lax.axis_index("x"), lax.axis_index("y"), lax.axis_index("c")
    dev = 4 * xi + 2 * yi + ci
    s_me = 2 * xi + yi
    c_arr = jnp.reshape(ci, (1,)).astype(jnp.int32)
    s_arr = jnp.reshape(s_me, (1,)).astype(jnp.int32)

    t, d = x.shape[1], x.shape[2]
    tc = ctx.shape[1]
    tk = tc + t
    ff = ffn1_w_out.shape[1] * N_CHIP
    aw = w_proj_attn.shape[1]
    rw = w_proj_ret.shape[1]
    pw = mix_w_in.shape[2] * N_CHIP
    kvw = (pw - aw - 4 * rw - 2 * d) // 2
    groups = aw // kvw
    n_ret_heads = rw // HEAD_DIM
    mod_cols = N_MOD * d
    tr = _tile(tc, 256, 32)
    nt_all, nt_x, ctx_tiles = tk // tr, t // tr, tc // tr

    c_rows = _gather_row("gather_c", c)
    cg = jnp.concatenate([c_rows, c_ctx[None, :], jnp.zeros((7, d), F32)], axis=0)
    w_ada_l = w_ada[0]
    ada_cols = w_ada_l.shape[1]
    b_ada_l = lax.dynamic_slice_in_dim(b_ada, s_me * ada_cols, ada_cols, axis=1)
    mod_shard = _ada_fwd(cg, w_ada_l, b_ada_l)
    mod_g = _all_gather8("gather_mod", mod_shard)
    mod_full = jnp.concatenate([mod_g[0], mod_g[2], mod_g[4], mod_g[6]], axis=1)
    mod_x = lax.dynamic_slice_in_dim(mod_full, dev, 1, axis=0).reshape(N_MOD, d)
    mod_c = mod_full[8].reshape(N_MOD, d)
    mods = jnp.stack([mod_c, mod_x])

    big = [("col", ffn1_w_in), ("row", ffn1_w_out), ("col", mix_w_in), ("col", w_proj_attn), ("col", w_proj_ret),
           ("row", mix_w_out), ("col", ffn2_w_in), ("row", ffn2_w_out)]
    metas = []
    for kind, w in big:
        r_l, c_l = w.shape[1:]
        metas.append(_Sharded(kind, r_l, c_l * N_CHIP) if kind == "col" else _Sharded(kind, r_l * N_CHIP, c_l))
    placed = [_place_shard(m, w[0], s_arr) for m, (_, w) in zip(metas, big)]
    layer_groups = ((0, 1), (1, 2), (2, 6), (6, 8))
    gather_sems, placed, token = _split_start(
        "gather_start", placed,
        [(lo, hi - lo, _gather_copies(metas[lo:hi], True), 3 * (hi - lo)) for lo, hi in layer_groups], mods)
    mods = mods + token[0, 0]

    def weights_of(gi, after):
        lo, hi = layer_groups[gi]
        arrived = _split_wait("gather_wait_%d" % gi, gather_sems[gi], placed[lo:hi],
                              _gather_copies(metas[lo:hi], True), after)
        return _gather_forward("gather_forward_%d" % gi, metas[lo:hi], arrived)

    cos_full, sin_signed = _rope_tables(t, tc)
    q_gain, k_gain = attn_q_gain, attn_k_gain
    log_gamma = jax.nn.log_sigmoid(ret_decay_logit[0])
    lgv = jnp.broadcast_to(log_gamma[:, :, None, None], (2, n_ret_heads, 1, HEAD_DIM))

    def norm_mod(name, h, n_tiles, off, i_shift, i_scale):
        def fn(rows, sel, fulls):
            return [_rmsn(rows[0]) * (1.0 + sel(i_scale)) + sel(i_shift)], []
        return _rowwise(name, fn, n_tiles=n_tiles, tr=tr, row_ins=[(h, 0, None)],
                        row_outs=[(h.shape[0], d, BF16, 0)], sel_in=mods, sel_off=off, ctx_rows=tc)[0]

    def resid_norm(name, h, h_off, f, n_tiles, off, i_gate, coef, i_shift, i_scale):
        def fn(rows, sel, fulls):
            hn = rows[0] + coef * sel(i_gate) * rows[1]
            return [hn, _rmsn(hn) * (1.0 + sel(i_scale)) + sel(i_shift)], []
        return _rowwise(name, fn, n_tiles=n_tiles, tr=tr, row_ins=[(h, h_off, None), (f, 0, None)],
                        row_outs=[(f.shape[0], d, F32, 0), (f.shape[0], d, BF16, 0)], sel_in=mods, sel_off=off,
                        ctx_rows=tc)

    h0 = jnp.concatenate([ctx[0], x[0]], axis=0)
    n1 = norm_mod("norm_mod1", h0, nt_all, 0, 0, 1)
    w1i, = weights_of(0, n1)
    hm1, ua1, ub1 = _mm_swiglu("ffn1_in", n1, w1i)
    w1o, = weights_of(1, hm1)
    f1 = _mm("ffn1_out", hm1, w1o, "nn", F32)
    h1, n2 = resid_norm("resid_norm1", h0, 0, f1, nt_all, 0, 2, 0.5, 3, 4)
    wmi, wpa, wpr, wmo = weights_of(2, n2)
    p_q = _mm("mix_in_q", n2, wmi, "nn", F32, 0, aw)
    p_kv = _mm("mix_in_kv", n2, wmi, "nn", F32, aw, 2 * kvw)
    p_r = _mm("mix_in_ret", n2, wmi, "nn", F32, aw + 2 * kvw, 3 * rw)
    p_gr = _mm("mix_in_gr", n2, wmi, "nn", F32, aw + 2 * kvw + 3 * rw, rw)
    p_gab = _mm("mix_in_gab", n2, wmi, "nn", F32, aw + 2 * kvw + 4 * rw, 2 * d)

    def q_prep(rows, sel, fulls):
        p, cf, ss = rows
        return _heads_map(lambda ph: [_qk_rot(ph, fulls[0], cf, ss) * QSCALE], [p], aw), []

    q_rot = _rowwise("q_prep", q_prep, n_tiles=nt_x, tr=tr,
                     row_ins=[(p_q, ctx_tiles, None), (cos_full, ctx_tiles, None), (sin_signed, ctx_tiles, None)],
                     row_outs=[(t, aw, BF16, 0)], full_ins=[q_gain])[0]

    def kv_prep(rows, sel, fulls):
        p, cf, ss = rows
        k_rot = _heads_map(lambda ph: [_qk_rot(ph, fulls[0], cf, ss)], [p[:, :kvw]], kvw)[0]
        v_ones = _heads_map(lambda vh: [jnp.concatenate([vh, jnp.ones_like(vh)], axis=1)], [p[:, kvw:]], kvw)[0]
        return [k_rot, v_ones], []

    k_rot, v_att = _rowwise("kv_prep", kv_prep, n_tiles=nt_all, tr=tr,
                            row_ins=[(p_kv, 0, None), (cos_full, 0, None), (sin_signed, 0, None)],
                            row_outs=[(tk, kvw, BF16, 0), (tk, 2 * kvw, BF16, 0)], full_ins=[k_gain])

    ya, lse = _flash_fwd(q_rot, k_rot, v_att, groups)
    y_fwd, y_bwd, states = _ret_fwd(p_r, lgv, tc)

    def ret_out_fn(yf, yb, gr):
        return [_silu(gr) * _rmsn(yf + yb)]

    def ret_out(rows, sel, fulls):
        return _heads_map(ret_out_fn, rows, rw), []

    y_rows = [(y_fwd, ctx_tiles, None), (y_bwd, ctx_tiles, None), (p_gr, ctx_tiles, None)]
    yr = _rowwise("ret_out", ret_out, n_tiles=nt_x, tr=tr, row_ins=y_rows, row_outs=[(t, rw, BF16, 0)])[0]

    pa = _mm("proj_attn", ya, wpa, "nn", F32)
    prj = _mm("proj_ret", yr, wpr, "nn", F32)

    def merge_fn(a, r, ga, gb):
        return _sigmoid(ga) * a + _sigmoid(gb) * r

    gate_rows = [(p_gab, ctx_tiles, (d, 0)), (p_gab, ctx_tiles, (d, 1))]
    z = _rowwise("merge", lambda rows, sel, fulls: ([merge_fn(*rows)], []), n_tiles=nt_x, tr=tr,
                 row_ins=[(pa, 0, None), (prj, 0, None)] + gate_rows, row_outs=[(t, d, BF16, 0)])[0]
    fo = _mm("mix_out", z, wmo, "nn", F32)
    h2, n3 = resid_norm("resid_norm2", h1, ctx_tiles, fo, nt_x, ctx_tiles, 5, 1.0, 6, 7)
    w2i, w2o = weights_of(3, n3)
    hm2, ua2, ub2 = _mm_swiglu("ffn2_in", n3, w2i)
    f2 = _mm("ffn2_out", hm2, w2o, "nn", F32)

    def loss_fn(rows, sel, fulls):
        h2v, f2v, tgt = rows
        g3 = 0.5 * sel(8)
        y, vjp = jax.vjp(lambda hh, ww: _rmsn(hh) * ww, h2v + g3 * f2v, fulls[0])
        err = y - tgt
        dh, dw = vjp(err / d)
        return [dh, g3 * dh], [0.5 / d * jnp.sum(err * err, axis=0, keepdims=True), dw,
                               jnp.sum(0.5 * dh * f2v, axis=0, keepdims=True)]

    dh3, df2, loss_acc = _rowwise("loss_head", loss_fn, n_tiles=nt_x, tr=tr,
                                  row_ins=[(h2, 0, None), (f2, 0, None), (loss_target[0], 0, None)],
                                  row_outs=[(t, d, F32, 0), (t, d, BF16, 0)], sel_in=mods, sel_off=ctx_tiles,
                                  ctx_rows=tc, full_ins=[final_norm[None, :]], acc_shape=(8, d))
    loss_cols, g_final, dg3 = loss_acc[1, 0:1], loss_acc[1, 1:2], loss_acc[1, 2:3]

    def swiglu_bwd(name, dhm, ua, ub):
        rows_n = dhm.shape[0]
        tr_w = _tile(tr, 128, 32)

        def fn(rows, sel, fulls):
            g, a, b = rows
            _, vjp = jax.vjp(lambda aa, bb: _silu(aa) * bb, a.astype(F32), b.astype(F32))
            da, db = vjp(g)
            return [jnp.concatenate([da, db], axis=1)], []
        return _rowwise(name, fn, n_tiles=rows_n // tr_w, tr=tr_w,
                        row_ins=[(dhm, 0, None), (ua, 0, None), (ub, 0, None)],
                        row_outs=[(rows_n, 2 * ff, BF16, 0)])[0]

    def norm_mod_bwd(name, dn, h, dres, dres_off, n_tiles, off, i_shift, i_scale, gate=None, out_off=0):
        def fn(rows, sel, fulls):
            g, hh, dr = rows[:3]
            if dres_off < 0:
                dr = jnp.where(fulls[-1], 0.0, dr)
            _, vjp = jax.vjp(lambda a, sh, sc: _rmsn(a) * (1.0 + sc) + sh, hh,
                             sel(i_shift), sel(i_scale))
            dhh, dsh, dsc = vjp(g)
            dh = dr + dhh
            if gate is None:
                return [dh], [dsh, dsc]
            return [dh, gate[2] * sel(gate[1]) * dh], [dsh, dsc, jnp.sum(gate[2] * dh * rows[3], axis=0, keepdims=True)]
        n_rows = dn.shape[0] + out_off * tr
        row_ins = [(dn, 0, None), (h, 0, None), (dres, dres_off, None)]
        row_outs = [(n_rows, d, F32, out_off)]
        if gate is not None:
            row_ins.append((gate[0], 0, None))
            row_outs.append((n_rows, d, BF16, out_off))
        return _rowwise(name, fn, n_tiles=n_tiles, tr=tr, row_ins=row_ins, row_outs=row_outs, sel_in=mods,
                        sel_off=off, ctx_rows=tc, acc_shape=(8, d))

    g_w2o = _mm("ffn2_out_dw", hm2, df2, "tn", BF16)
    dhm2 = _mm("ffn2_out_dx", df2, w2o, "nt", F32)
    du2 = swiglu_bwd("swiglu_bwd2", dhm2, ua2, ub2)
    g_w2i = _mm("ffn2_in_dw", n3, du2, "tn", BF16)
    dn3 = _mm("ffn2_in_dx", du2, w2i, "nt", F32)
    dh2, dfo, acc_n3 = norm_mod_bwd("norm_mod_bwd3", dn3, h2, dh3, 0, nt_x, ctx_tiles, 6, 7, gate=(fo, 5, 1.0))

    sems_ffn2, thru_ffn2, token = _reduce_start("reduce_start_ffn2", metas[6:8], [g_w2i, g_w2o])

    g_wmo = _mm("mix_out_dw", z, dfo, "tn", BF16, after=token)
    dz = _mm("mix_out_dx", dfo, wmo, "nt", F32)

    def merge_bwd(rows, sel, fulls):
        g, a, r, ga, gb = rows
        _, vjp = jax.vjp(merge_fn, a, r, ga, gb)
        da, dr, dga, dgb = vjp(g)
        return [da, dr, jnp.concatenate([dga, dgb], axis=1)], []

    dpa, dpr, dgab = _rowwise("merge_bwd", merge_bwd, n_tiles=nt_x, tr=tr,
                              row_ins=[(dz, 0, None), (pa, 0, None), (prj, 0, None)] + gate_rows,
                              row_outs=[(t, d, BF16, 0), (t, d, BF16, 0), (t, 2 * d, BF16, 0)])
    g_wpa = _mm("proj_attn_dw", ya, dpa, "tn", BF16)
    dya = _mm("proj_attn_dx", dpa, wpa, "nt", BF16)
    g_wpr = _mm("proj_ret_dw", yr, dpr, "tn", BF16)
    dyr = _mm("proj_ret_dx", dpr, wpr, "nt", F32)

    def ret_out_bwd(rows, sel, fulls):
        def per_head(g, yf, yb, gr):
            _, vjp = jax.vjp(lambda yy, gg: ret_out_fn(yy, 0.0, gg)[0], yf + yb, gr)
            return list(vjp(g))
        dy, dgr = _heads_map(per_head, rows, rw)
        return [dy, dgr], []

    dy_ret, dgr = _rowwise("ret_out_bwd", ret_out_bwd, n_tiles=nt_x, tr=tr, row_ins=[(dyr, 0, None)] + y_rows,
                           row_outs=[(t, rw, F32, 0), (t, rw, BF16, 0)])
    dp_rf, dp_rb, dlg = _ret_bwd(p_r, states, dy_ret, lgv, tc)
    dp_r = _rowwise("ret_bwd_sum", lambda rows, sel, fulls: ([rows[0] + rows[1]], []), n_tiles=nt_all, tr=tr,
                    row_ins=[(dp_rf, 0, None), (dp_rb, 0, None)], row_outs=[(tk, 3 * rw, BF16, 0)])[0]

    dq_rot, dk_rot, dv_att = _flash_bwd(q_rot, k_rot, v_att, ya, dya, lse, groups)

    def q_prep_bwd(rows, sel, fulls):
        g, p, cf, ss = rows
        gain_acc = []

        def per_head(gh, ph):
            dp, dgain = _qk_rot_bwd(gh * HEAD_DIM ** -0.5, ph, fulls[0], cf, ss)
            gain_acc.append(dgain)
            return [dp]
        dp = _heads_map(per_head, [g, p], aw)[0]
        return [dp], [functools.reduce(lambda a, b: a + b, gain_acc)]

    dp_q, acc_gq = _rowwise("q_prep_bwd", q_prep_bwd, n_tiles=nt_x, tr=tr,
                            row_ins=[(dq_rot, 0, None), (p_q, ctx_tiles, None), (cos_full, ctx_tiles, None),
                                     (sin_signed, ctx_tiles, None)],
                            row_outs=[(t, aw, BF16, 0)], full_ins=[q_gain], acc_shape=(8, HEAD_DIM),
                            sel_off=ctx_tiles, ctx_rows=tc)

    def kv_prep_bwd(rows, sel, fulls):
        gk, gv, p, cf, ss = rows
        gain_acc = []

        def per_head(gh, ph):
            dp, dgain = _qk_rot_bwd(gh, ph, fulls[0], cf, ss)
            gain_acc.append(dgain)
            return [dp]
        dpk = _heads_map(per_head, [gk, p], kvw)[0]
        return [jnp.concatenate([dpk, gv], axis=1)], [functools.reduce(lambda a, b: a + b, gain_acc)]

    dp_kv, acc_gk = _rowwise("kv_prep_bwd", kv_prep_bwd, n_tiles=nt_all, tr=tr,
                             row_ins=[(dk_rot, 0, None), (dv_att, 0, None), (p_kv, 0, (kvw, 0)), (cos_full, 0, None),
                                      (sin_signed, 0, None)],
                             row_outs=[(tk, 2 * kvw, BF16, 0)], full_ins=[k_gain], acc_shape=(8, HEAD_DIM),
                             sel_off=0, ctx_rows=tc)

    def with_ctx_zeros(a):
        return jnp.concatenate([jnp.zeros((tc, a.shape[1]), a.dtype), a], axis=0)

    dp = jnp.concatenate([with_ctx_zeros(dp_q), dp_kv, dp_r, with_ctx_zeros(dgr), with_ctx_zeros(dgab)], axis=1)
    g_wmi = _mm("mix_in_dw", n2, dp, "tn", BF16)
    dn2 = _mm("mix_in_dx", dp, wmi, "nt", F32)
    dh1, df1, acc_n2 = norm_mod_bwd("norm_mod_bwd2", dn2, h1, dh2, -ctx_tiles, nt_all, 0, 3, 4, gate=(f1, 2, 0.5))
    sems_mix, thru_mix, token = _reduce_start("reduce_start_mix", metas[2:6], [g_wmi, g_wpa, g_wpr, g_wmo])

    g_w1o = _mm("ffn1_out_dw", hm1, df1, "tn", BF16, after=token)
    sems_w1o, thru_w1o, token = _reduce_start("reduce_start_ffn1_out", metas[1:2], [g_w1o])
    dhm1 = _mm("ffn1_out_dx", df1, w1o, "nt", F32, after=token)
    du1 = swiglu_bwd("swiglu_bwd1", dhm1, ua1, ub1)
    g_w1i = _mm("ffn1_in_dw", n1, du1, "tn", BF16)
    sems_w1i, thru_w1i, token = _reduce_start("reduce_start_ffn1_in", metas[0:1], [g_w1i])
    dn1 = _mm("ffn1_in_dx", du1, w1i, "nt", F32, after=token)
    dh0, acc_n1 = norm_mod_bwd("norm_mod_bwd1", dn1, h0, dh1, 0, nt_all, 0, 0, 1, out_off=-ctx_tiles)
    grad_x = dh0[None]

    grads_own, landed = [], []
    for name, lo, hi, sems_l, thru_l in (("reduce_wait_ffn1_in", 0, 1, sems_w1i, thru_w1i),
                                         ("reduce_wait_ffn1_out", 1, 2, sems_w1o, thru_w1o),
                                         ("reduce_wait_mix", 2, 6, sems_mix, thru_mix),
                                         ("reduce_wait_ffn2", 6, 8, sems_ffn2, thru_ffn2)):
        grads_l, landed_l = _reduce_wait(name, metas[lo:hi], sems_l, thru_l, dh0)
        grads_own += grads_l
        landed += landed_l
    pieces = [_sum_pieces(m, g, l, s_arr, c_arr) for m, g, l in zip(metas, grads_own, landed)]
    grads_big = _share_halves(metas, pieces)

    zero_row = jnp.zeros((1, d), F32)
    dmod_x = jnp.concatenate([acc_n1[1, 0:1], acc_n1[1, 1:2], acc_n2[1, 2:3], acc_n2[1, 0:1], acc_n2[1, 1:2],
                              acc_n3[1, 2:3], acc_n3[1, 0:1], acc_n3[1, 1:2], dg3], axis=1)
    dmod_c = jnp.concatenate([acc_n1[0, 0:1], acc_n1[0, 1:2], acc_n2[0, 2:3], acc_n2[0, 0:1], acc_n2[0, 1:2]]
                             + [zero_row] * 4, axis=1)
    dlg_row = jnp.pad(dlg[:, :, 0, 0].reshape(1, 2 * n_ret_heads), ((0, 0), (0, LANES_V7X - 2 * n_ret_heads)))
    packed = jnp.concatenate([dmod_x, dmod_c, acc_gq[1, 0:1], acc_gk[0, 0:1] + acc_gk[1, 0:1], dlg_row,
                              g_final, loss_cols], axis=1)
    off_gq = 2 * mod_cols
    off_gk = off_gq + LANES_V7X
    off_lg = off_gk + LANES_V7X
    off_fn = off_lg + LANES_V7X
    off_loss = off_fn + d
    gathered = _gather_row("gather_small", packed)
    logits_row = jnp.pad(ret_decay_logit.reshape(1, 2 * n_ret_heads), ((0, 0), (0, LANES_V7X - 2 * n_ret_heads)))
    totals, g_b_ada, g_decay, loss_row = _small_reduce(gathered, logits_row, mod_cols, off_lg, off_loss, d)
    loss = loss_row[0, 0]

    dm = jnp.concatenate([gathered[:, :mod_cols], totals[:, mod_cols:2 * mod_cols],
                          jnp.zeros((7, mod_cols), F32)], axis=0)
    dm_l = lax.dynamic_slice_in_dim(dm, s_me * ada_cols, ada_cols, axis=1)
    g_w_ada, da_part = _ada_bwd(cg, dm_l, w_ada_l)
    da_rows = _gather_row("gather_dc", da_part[8:9])
    g_c_ctx = _c_ctx_grad(da_rows, c_ctx[None, :])

    def as2d(a):
        return a.reshape(-1, a.shape[-1])

    grads = {
        "c_ctx": g_c_ctx, "w_ada": g_w_ada, "b_ada": g_b_ada,
        "ffn1_w_in": grads_big[0], "ffn1_w_out": grads_big[1], "mix_w_in": grads_big[2],
        "attn_q_gain": totals[:, off_gq:off_gq + HEAD_DIM], "attn_k_gain": totals[:, off_gk:off_gk + HEAD_DIM],
        "ret_decay_logit": g_decay[:, :2 * n_ret_heads],
        "w_proj_attn": grads_big[3], "w_proj_ret": grads_big[4], "mix_w_out": grads_big[5],
        "ffn2_w_in": grads_big[6], "ffn2_w_out": grads_big[7], "final_norm": totals[:, off_fn:off_fn + d],
    }
    weights = {"c_ctx": (c_ctx, m_c_ctx, v_c_ctx), "w_ada": (w_ada, m_w_ada, v_w_ada),
               "b_ada": (b_ada, m_b_ada, v_b_ada), "ffn1_w_in": (ffn1_w_in, m_ffn1_w_in, v_ffn1_w_in),
               "ffn1_w_out": (ffn1_w_out, m_ffn1_w_out, v_ffn1_w_out), "mix_w_in": (mix_w_in, m_mix_w_in, v_mix_w_in),
               "attn_q_gain": (attn_q_gain, m_attn_q_gain, v_attn_q_gain),
               "attn_k_gain": (attn_k_gain, m_attn_k_gain, v_attn_k_gain),
               "ret_decay_logit": (ret_decay_logit, m_ret_decay_logit, v_ret_decay_logit),
               "w_proj_attn": (w_proj_attn, m_w_proj_attn, v_w_proj_attn),
               "w_proj_ret": (w_proj_ret, m_w_proj_ret, v_w_proj_ret), "mix_w_out": (mix_w_out, m_mix_w_out, v_mix_w_out),
               "ffn2_w_in": (ffn2_w_in, m_ffn2_w_in, v_ffn2_w_in), "ffn2_w_out": (ffn2_w_out, m_ffn2_w_out, v_ffn2_w_out),
               "final_norm": (final_norm, m_final_norm, v_final_norm)}
    out_g, out_d, out_m, out_v = [], [], [], []
    for name, (w, m, v) in weights.items():
        shape = w.shape
        if name == "ret_decay_logit":
            w2, m2, v2 = (a.reshape(1, -1) for a in (w, m, v))
        else:
            w2, m2, v2 = as2d(w), as2d(m), as2d(v)
        g2 = grads[name].reshape(w2.shape)
        delta, new_m, new_v = _adamw(w2, g2, m2, v2)
        out_g.append(g2.reshape(shape))
        out_d.append(delta.reshape(shape))
        out_m.append(new_m.reshape(shape))
        out_v.append(new_v.reshape(shape))
    return (loss, grad_x, *out_g, *out_d, *out_m, *out_v)
```

```python
import functools
import math

import jax
import jax.numpy as jnp
from jax import lax
from jax.experimental import pallas as pl
from jax.experimental.pallas import tpu as pltpu

F32 = jnp.float32
BF16 = jnp.bfloat16

HEAD_DIM = 128
GRID_W = 64
ROPE_THETA = 10000.0
NORM_EPS = 1e-6
N_MOD = 9
RET_CHUNK = 128
ADAM_LR = 0.001
ADAM_B1 = 0.9
ADAM_B2 = 0.999
ADAM_EPS = 1e-08
ADAM_WD = 0.01
ADAM_STEP = 10

N_DEV = 8
N_CHIP = 4
LANES_V7X = 128
MXU_WIDTH_V7X = 256
VMEM_LIMIT_V7X = 52 * 1024 * 1024

NT_DIMS = (((1,), (1,)), ((), ()))
TN_DIMS = (((0,), (0,)), ((), ()))
NN_DIMS = (((1,), (0,)), ((), ()))


def _tile(n, pref, mult=LANES_V7X):
    if n <= pref:
        return n
    t = (pref // mult) * mult
    while t >= mult:
        if n % t == 0:
            return t
        t -= mult
    return n


def _params(sem):
    return pltpu.CompilerParams(dimension_semantics=sem, vmem_limit_bytes=VMEM_LIMIT_V7X)


def _sigmoid(x):
    return 1.0 / (1.0 + jnp.exp(-x))


def _silu(x):
    return x * _sigmoid(x)


def _rmsn(x):
    return x * lax.rsqrt(jnp.mean(x * x, axis=-1, keepdims=True) + NORM_EPS)


MM_VMEM_BUDGET = 44 * 1024 * 1024


def _divisor_tiles(n, cap):
    ts = [t for t in range(LANES_V7X, min(n, cap) + 1, LANES_V7X) if n % t == 0]
    return ts or [n]


def _mm_tiles(m, n, tk, out_bytes, has_acc):
    best = None
    for tm in _divisor_tiles(m, 1536):
        for tn in _divisor_tiles(n, 2560):
            need = 4 * tk * (tm + tn) + 2 * tm * tn * out_bytes + 4 * tm * tn
            if need > MM_VMEM_BUDGET:
                continue
            score = tm * tn / (tm + tn)
            for tdim in (tm, tn):
                if tdim % MXU_WIDTH_V7X:
                    score *= 0.85
            if best is None or score > best[0]:
                best = (score, tm, tn)
    return best[1], best[2]


def _mm(name, a, b, mode, out_dtype, b_off=0, n=None, after=None):
    if mode == "nn":
        m, k = a.shape
        n = b.shape[1] if n is None else n
        dims = NN_DIMS
    elif mode == "nt":
        m, k = a.shape
        n = b.shape[0]
        dims = NT_DIMS
    else:
        k, m = a.shape
        n = b.shape[1]
        dims = TN_DIMS
    tk = _tile(k, 2816)
    nk = k // tk
    tm, tn = _mm_tiles(m, math.gcd(n, b_off) if b_off else n, tk, jnp.dtype(out_dtype).itemsize, nk > 1)
    joff = b_off // tn

    def body(a_ref, b_ref, *rest):
        o_ref = rest[0 if after is None else 1]
        if nk == 1:
            o_ref[...] = lax.dot_general(a_ref[...], b_ref[...], dims,
                                         preferred_element_type=F32).astype(o_ref.dtype)
            return
        acc_ref = rest[-1]
        kk = pl.program_id(2)

        @pl.when(kk == 0)
        def _():
            acc_ref[...] = jnp.zeros_like(acc_ref)

        acc_ref[...] += lax.dot_general(a_ref[...], b_ref[...], dims, preferred_element_type=F32)

        @pl.when(kk == nk - 1)
        def _():
            o_ref[...] = acc_ref[...].astype(o_ref.dtype)

    if mode == "nn":
        a_spec = pl.BlockSpec((tm, tk), lambda i, j, kk: (i, kk))
        b_spec = pl.BlockSpec((tk, tn), lambda i, j, kk: (kk, j + joff))
    elif mode == "nt":
        a_spec = pl.BlockSpec((tm, tk), lambda i, j, kk: (i, kk))
        b_spec = pl.BlockSpec((tn, tk), lambda i, j, kk: (j, kk))
    else:
        a_spec = pl.BlockSpec((tk, tm), lambda i, j, kk: (kk, i))
        b_spec = pl.BlockSpec((tk, tn), lambda i, j, kk: (kk, j))
    return pl.pallas_call(
        body, name=name, grid=(m // tm, n // tn, nk),
        in_specs=[a_spec, b_spec] + ([] if after is None else [pl.BlockSpec(memory_space=pl.ANY)]),
        out_specs=pl.BlockSpec((tm, tn), lambda i, j, kk: (i, j)),
        out_shape=jax.ShapeDtypeStruct((m, n), out_dtype),
        scratch_shapes=[pltpu.VMEM((tm, tn), F32)] if nk > 1 else [],
        compiler_params=_params(("parallel", "parallel", "arbitrary")),
    )(*((a, b) if after is None else (a, b, after)))


def _mm_swiglu(name, a, w):
    m, k = a.shape
    f = w.shape[1] // 2
    tm = _tile(m, 1024)
    tn = _tile(f, 512)
    tk = _tile(k, 2560)
    nk = k // tk
    jf = f // tn

    def body(a_ref, wa_ref, wb_ref, h_ref, ua_ref, ub_ref, acca, accb):
        kk = pl.program_id(2)

        @pl.when(kk == 0)
        def _():
            acca[...] = jnp.zeros_like(acca)
            accb[...] = jnp.zeros_like(accb)

        av = a_ref[...]
        acca[...] += jnp.dot(av, wa_ref[...], preferred_element_type=F32)
        accb[...] += jnp.dot(av, wb_ref[...], preferred_element_type=F32)

        @pl.when(kk == nk - 1)
        def _():
            ua = acca[...]
            ub = accb[...]
            h_ref[...] = (_silu(ua) * ub).astype(BF16)
            ua_ref[...] = ua.astype(BF16)
            ub_ref[...] = ub.astype(BF16)

    o_spec = pl.BlockSpec((tm, tn), lambda i, j, kk: (i, j))
    o_shape = jax.ShapeDtypeStruct((m, f), BF16)
    return pl.pallas_call(
        body, name=name, grid=(m // tm, jf, nk),
        in_specs=[pl.BlockSpec((tm, tk), lambda i, j, kk: (i, kk)),
                  pl.BlockSpec((tk, tn), lambda i, j, kk: (kk, j)),
                  pl.BlockSpec((tk, tn), lambda i, j, kk: (kk, j + jf))],
        out_specs=[o_spec, o_spec, o_spec],
        out_shape=[o_shape, o_shape, o_shape],
        scratch_shapes=[pltpu.VMEM((tm, tn), F32), pltpu.VMEM((tm, tn), F32)],
        compiler_params=_params(("parallel", "parallel", "arbitrary")),
    )(a, w, w)


def _rowwise(name, fn, *, n_tiles, tr, row_ins, row_outs, sel_in=None, sel_off=0, ctx_rows=0,
             full_ins=(), acc_shape=None):
    sr = 128 if tr % 128 == 0 else (32 if tr % 32 == 0 else tr)
    n_row, n_full, n_out = len(row_ins), len(full_ins), len(row_outs)
    has_sel = sel_in is not None
    has_acc = acc_shape is not None

    def sel_of(i):
        return jnp.where((i + sel_off) * tr < ctx_rows, 0, 1)

    def body(*refs):
        row_refs = refs[:n_row]
        pos = n_row
        sel_ref = None
        if has_sel:
            sel_ref = refs[pos]
            pos += 1
        full_refs = refs[pos:pos + n_full]
        pos += n_full
        out_refs = refs[pos:pos + n_out]
        pos += n_out
        acc_ref = refs[pos] if has_acc else None
        i = pl.program_id(0)
        if has_acc:
            first = (i == 0) | ((i + sel_off) * tr == ctx_rows)

            @pl.when(first)
            def _():
                acc_ref[...] = jnp.zeros_like(acc_ref)

        sel = (lambda kk: sel_ref[kk:kk + 1, :]) if has_sel else None
        fulls = [r[...] for r in full_refs] + [(i + sel_off) * tr < ctx_rows]

        def slab(r, carry):
            rs = pl.ds(pl.multiple_of(r * sr, sr), sr)
            rows = [ref[rs, :].astype(F32) for ref in row_refs]
            outs, accs = fn(rows, sel, fulls)
            for o_ref, o in zip(out_refs, outs):
                o_ref[rs, :] = o.astype(o_ref.dtype)
            for kk, a in enumerate(accs):
                acc_ref[kk:kk + 1, :a.shape[1]] += a
            return carry

        lax.fori_loop(0, tr // sr, slab, 0)

    def row_map(off, col=0):
        if off < 0:
            return lambda i: (jnp.maximum(i + off, 0), col)
        return lambda i: (i + off, col)

    in_specs, args = [], []
    for arr, off, blk in row_ins:
        if blk is None:
            in_specs.append(pl.BlockSpec((tr, arr.shape[1]), row_map(off)))
        else:
            in_specs.append(pl.BlockSpec((tr, blk[0]), row_map(off, blk[1])))
        args.append(arr)
    if has_sel:
        in_specs.append(pl.BlockSpec((None,) + sel_in.shape[1:], lambda i: (sel_of(i), 0, 0)))
        args.append(sel_in)
    for arr in full_ins:
        in_specs.append(pl.BlockSpec(arr.shape, lambda i: (0, 0)))
        args.append(arr)
    out_specs, out_shape = [], []
    for rows, cols, dt, off in row_outs:
        out_specs.append(pl.BlockSpec((tr, cols), row_map(off)))
        out_shape.append(jax.ShapeDtypeStruct((rows, cols), dt))
    if has_acc:
        out_specs.append(pl.BlockSpec((None,) + tuple(acc_shape), lambda i: (sel_of(i), 0, 0)))
        out_shape.append(jax.ShapeDtypeStruct((2,) + tuple(acc_shape), F32))
    return pl.pallas_call(
        body, name=name, grid=(n_tiles,), in_specs=in_specs, out_specs=out_specs, out_shape=out_shape,
        compiler_params=_params(("arbitrary",)),
    )(*args)


def _swap_pairs(x):
    lane = lax.broadcasted_iota(jnp.int32, x.shape, 1)
    nxt = pltpu.roll(x, x.shape[1] - 1, 1)
    prv = pltpu.roll(x, 1, 1)
    return jnp.where(lane % 2 == 0, nxt, prv)


def _heads_map(fn, arrs, width):
    outs = None
    for h in range(width // HEAD_DIM):
        sl = slice(h * HEAD_DIM, (h + 1) * HEAD_DIM)
        res = fn(*[a[:, sl] for a in arrs])
        if outs is None:
            outs = [[] for _ in res]
        for lst, r in zip(outs, res):
            lst.append(r)
    return [jnp.concatenate(lst, axis=1) if len(lst) > 1 else lst[0] for lst in outs]


QSCALE = HEAD_DIM ** -0.5 * math.log2(math.e)
LN2 = math.log(2.0)


def _lane_chunks(a):
    return [a[:, cc * LANES_V7X:(cc + 1) * LANES_V7X] for cc in range(a.shape[1] // LANES_V7X)]


def _row_bcast(col, like):
    return jnp.broadcast_to(col, like.shape)


def _flash_tiles(t, tk_all):
    return _tile(t, 1024), _tile(tk_all, 1024)


def _flash_fwd(q, k, vx, groups):
    t, aw = q.shape
    tk_all, kvw = k.shape
    kvh = kvw // HEAD_DIM
    gw = groups * HEAD_DIM
    tq, tk = _flash_tiles(t, tk_all)
    nk = tk_all // tk

    def body(q_ref, k_ref, v_ref, o_ref, lse_ref, m_sc, l_sc, acc_sc):
        j = pl.program_id(2)

        @pl.when(j == 0)
        def _():
            m_sc[...] = jnp.full_like(m_sc, -jnp.inf)
            l_sc[...] = jnp.zeros_like(l_sc)
            acc_sc[...] = jnp.zeros_like(acc_sc)

        kt = k_ref[...]
        vt = v_ref[...]
        for g in range(groups):
            sl = slice(g * HEAD_DIM, (g + 1) * HEAD_DIM)
            s = _lane_chunks(lax.dot_general(q_ref[:, sl], kt, NT_DIMS, preferred_element_type=F32))
            mx = functools.reduce(jnp.maximum, s)
            m_prev = m_sc[g]
            m_new = jnp.maximum(m_prev, _row_bcast(jnp.max(mx, axis=1, keepdims=True), mx))
            p = jnp.concatenate([jnp.exp2(sc - m_new).astype(BF16) for sc in s], axis=1)
            alpha = jnp.exp2(m_prev - m_new)
            pv = jnp.dot(p, vt, preferred_element_type=F32)
            acc_sc[g] = alpha * acc_sc[g] + pv[:, :HEAD_DIM]
            l_sc[g] = alpha * l_sc[g] + pv[:, HEAD_DIM:]
            m_sc[g] = m_new

        @pl.when(j == nk - 1)
        def _():
            for g in range(groups):
                sl = slice(g * HEAD_DIM, (g + 1) * HEAD_DIM)
                o_ref[:, sl] = (acc_sc[g] / l_sc[g]).astype(o_ref.dtype)
                lse_ref[:, sl] = m_sc[g] + jnp.log2(l_sc[g])

    qs = pl.BlockSpec((tq, gw), lambda kh, i, j: (i, kh))
    sc = pltpu.VMEM((groups, tq, HEAD_DIM), F32)
    return pl.pallas_call(
        body, name="flash_fwd", grid=(kvh, t // tq, nk),
        in_specs=[qs, pl.BlockSpec((tk, HEAD_DIM), lambda kh, i, j: (j, kh)),
                  pl.BlockSpec((tk, 2 * HEAD_DIM), lambda kh, i, j: (j, kh))],
        out_specs=[qs, qs],
        out_shape=[jax.ShapeDtypeStruct((t, aw), BF16), jax.ShapeDtypeStruct((t, aw), F32)],
        scratch_shapes=[sc, sc, sc],
        compiler_params=_params(("parallel", "parallel", "arbitrary")),
    )(q, k, vx)


def _flash_p_ds(q, kt, vt, do, lse, delta):
    s = _lane_chunks(lax.dot_general(q, kt, NT_DIMS, preferred_element_type=F32))
    dp = _lane_chunks(lax.dot_general(do, vt, NT_DIMS, preferred_element_type=F32))
    p = [jnp.exp2(sc - lse) for sc in s]
    ds = jnp.concatenate([(pc * (dc - delta)).astype(BF16) for pc, dc in zip(p, dp)], axis=1)
    return jnp.concatenate([pc.astype(BF16) for pc in p], axis=1), ds


def _flash_delta(do, o):
    prod = do.astype(F32) * o.astype(F32)
    return _row_bcast(jnp.sum(prod, axis=1, keepdims=True), prod)


def _flash_bwd(q, k, vx, o, do, lse, groups):
    t, aw = q.shape
    tk_all, kvw = k.shape
    kvh = kvw // HEAD_DIM
    gw = groups * HEAD_DIM
    tq, tk = _flash_tiles(t, tk_all)
    nq, nk = t // tq, tk_all // tk

    def body(q_ref, k_ref, v_ref, o_ref, do_ref, lse_ref, dq_ref, dk_ref, dv_ref, dq_sc, dk_acc, dv_acc):
        j = pl.program_id(1)
        i = pl.program_id(2)

        @pl.when(i == 0)
        def _():
            dk_acc[...] = jnp.zeros_like(dk_acc)
            dv_acc[...] = jnp.zeros_like(dv_acc)

        @pl.when(j == 0)
        def _():
            dq_sc[i] = jnp.zeros((groups, tq, HEAD_DIM), F32)

        kt = k_ref[...]
        vt = v_ref[:, :HEAD_DIM]
        for g in range(groups):
            sl = slice(g * HEAD_DIM, (g + 1) * HEAD_DIM)
            qv = q_ref[:, sl]
            dov = do_ref[:, sl]
            p, ds = _flash_p_ds(qv, kt, vt, dov, lse_ref[:, sl], _flash_delta(dov, o_ref[:, sl]))
            dv_acc[...] += lax.dot_general(p, dov, TN_DIMS, preferred_element_type=F32)
            dk_acc[...] += lax.dot_general(ds, qv, TN_DIMS, preferred_element_type=F32)
            dq_sc[i, g] += jnp.dot(ds, kt, preferred_element_type=F32)

        @pl.when(i == nq - 1)
        def _():
            dk_ref[...] = dk_acc[...] * LN2
            dv_ref[...] = dv_acc[...]

        @pl.when(j == nk - 1)
        def _():
            for g in range(groups):
                dq_ref[:, g * HEAD_DIM:(g + 1) * HEAD_DIM] = dq_sc[i, g]

    qs = pl.BlockSpec((tq, gw), lambda kh, j, i: (i, kh))
    ks = pl.BlockSpec((tk, HEAD_DIM), lambda kh, j, i: (j, kh))
    dq_spec = pl.BlockSpec((tq, gw), lambda kh, j, i: (jnp.where(j == nk - 1, i, 0), kh))
    return pl.pallas_call(
        body, name="flash_bwd", grid=(kvh, nk, nq),
        in_specs=[qs, ks, pl.BlockSpec((tk, 2 * HEAD_DIM), lambda kh, j, i: (j, kh)), qs, qs, qs],
        out_specs=[dq_spec, ks, ks],
        out_shape=[jax.ShapeDtypeStruct((t, aw), F32), jax.ShapeDtypeStruct((tk_all, kvw), F32),
                   jax.ShapeDtypeStruct((tk_all, kvw), F32)],
        scratch_shapes=[pltpu.VMEM((nq, groups, tq, HEAD_DIM), F32), pltpu.VMEM((tk, HEAD_DIM), F32),
                        pltpu.VMEM((tk, HEAD_DIM), F32)],
        compiler_params=_params(("parallel", "arbitrary", "arbitrary")),
    )(q, k, vx, o, do, lse)


def _bf_nn(a, b):
    return jnp.dot(a.astype(BF16), b.astype(BF16), preferred_element_type=F32)


def _bf_nt(a, b):
    return lax.dot_general(a.astype(BF16), b.astype(BF16), NT_DIMS, preferred_element_type=F32)


def _bf_tn(a, b):
    return lax.dot_general(a.astype(BF16), b.astype(BF16), TN_DIMS, preferred_element_type=F32)


@jax.custom_vjp
def _d_nn(a, b):
    return _bf_nn(a, b)


@jax.custom_vjp
def _d_nt(a, b):
    return _bf_nt(a, b)


@jax.custom_vjp
def _d_tn(a, b):
    return _bf_tn(a, b)


_d_nn.defvjp(lambda a, b: (_bf_nn(a, b), (a, b)), lambda r, g: (_d_nt(g, r[1]), _d_tn(r[0], g)))
_d_nt.defvjp(lambda a, b: (_bf_nt(a, b), (a, b)), lambda r, g: (_d_nn(g, r[1]), _d_tn(g, r[0])))
_d_tn.defvjp(lambda a, b: (_bf_tn(a, b), (a, b)), lambda r, g: (_d_nt(r[1], g), _d_nn(r[0], g)))


def _ret_chunk(q, k_raw, v, state, lg, rev, dots):
    nn, nt, tn = dots
    c = RET_CHUNK
    tcol = lax.broadcasted_iota(jnp.int32, (c, 1), 0).astype(F32)
    trow = lax.broadcasted_iota(jnp.int32, (1, c), 1).astype(F32)
    ucol = jnp.where(rev, c - 1.0 - tcol, tcol)
    urow = jnp.where(rev, c - 1.0 - trow, trow)
    e = ucol - urow
    low = e >= 0
    intra = jnp.where(low, jnp.exp(jnp.where(low, e, 0.0) * lg), 0.0)
    k = k_raw * (HEAD_DIM ** -0.5)
    inner = nt(q, k) * intra
    y = nn(inner, v) + nn(q, state) * jnp.exp((ucol + 1.0) * lg)
    new_state = state * jnp.exp(c * lg) + tn(k * jnp.exp((c - 1.0 - ucol) * lg), v)
    return y, new_state


def _ret_chunk_index(n_chunks, n_ctx_chunks):
    def idx(d, s):
        if d == 0:
            return s
        return jnp.where(s < n_ctx_chunks, n_ctx_chunks - 1 - s, n_chunks - 1 - s + n_ctx_chunks)
    return idx


def _ret_fwd(pr, lgv, ctx_rows):
    tk_all = pr.shape[0]
    rw = pr.shape[1] // 3
    nh = rw // HEAD_DIM
    nc = tk_all // RET_CHUNK
    cidx = _ret_chunk_index(nc, ctx_rows // RET_CHUNK)

    def body(pf_ref, pb_ref, lg_ref, yf_ref, yb_ref, st_ref, s_sc):
        s = pl.program_id(0)

        @pl.when(s == 0)
        def _():
            s_sc[...] = jnp.zeros_like(s_sc)

        for d, (p_ref, y_ref) in enumerate(((pf_ref, yf_ref), (pb_ref, yb_ref))):
            for h in range(nh):
                cols = [slice((part * nh + h) * HEAD_DIM, (part * nh + h + 1) * HEAD_DIM) for part in range(3)]
                state = s_sc[d, h]
                st_ref[d, h] = state
                y, new_state = _ret_chunk(p_ref[:, cols[0]], p_ref[:, cols[1]], p_ref[:, cols[2]], state,
                                          lg_ref[d, h][:, :1], d == 1, (_bf_nn, _bf_nt, _bf_tn))
                y_ref[:, h * HEAD_DIM:(h + 1) * HEAD_DIM] = y
                s_sc[d, h] = new_state

    y_shape = jax.ShapeDtypeStruct((tk_all, rw), F32)
    return pl.pallas_call(
        body, name="ret_fwd", grid=(nc,),
        in_specs=[pl.BlockSpec((RET_CHUNK, 3 * rw), lambda s: (cidx(0, s), 0)),
                  pl.BlockSpec((RET_CHUNK, 3 * rw), lambda s: (cidx(1, s), 0)),
                  pl.BlockSpec(lgv.shape, lambda s: (0, 0, 0, 0))],
        out_specs=[pl.BlockSpec((RET_CHUNK, rw), lambda s: (cidx(0, s), 0)),
                   pl.BlockSpec((RET_CHUNK, rw), lambda s: (cidx(1, s), 0)),
                   pl.BlockSpec((2, nh, None, HEAD_DIM, HEAD_DIM), lambda s: (0, 0, s, 0, 0))],
        out_shape=[y_shape, y_shape, jax.ShapeDtypeStruct((2, nh, nc, HEAD_DIM, HEAD_DIM), F32)],
        scratch_shapes=[pltpu.VMEM((2, nh, HEAD_DIM, HEAD_DIM), F32)],
        compiler_params=_params(("arbitrary",)),
    )(pr, pr, lgv)


def _ret_bwd(pr, states, dy, lgv, ctx_rows):
    tk_all = pr.shape[0]
    rw = pr.shape[1] // 3
    nh = rw // HEAD_DIM
    nc = tk_all // RET_CHUNK
    n_ctx = ctx_rows // RET_CHUNK
    cidx = _ret_chunk_index(nc, n_ctx)

    def body(pf_ref, pb_ref, st_ref, dyf_ref, dyb_ref, lg_ref, dpf_ref, dpb_ref, dlg_ref, ds_sc):
        sp = pl.program_id(0)
        on_ctx = [cidx(dd, nc - 1 - sp) < n_ctx for dd in (0, 1)]

        @pl.when(sp == 0)
        def _():
            ds_sc[...] = jnp.zeros_like(ds_sc)
            dlg_ref[...] = jnp.zeros_like(dlg_ref)

        for d, (p_ref, dy_ref, dp_ref) in enumerate(((pf_ref, dyf_ref, dpf_ref), (pb_ref, dyb_ref, dpb_ref))):
            for h in range(nh):
                cols = [slice((part * nh + h) * HEAD_DIM, (part * nh + h + 1) * HEAD_DIM) for part in range(3)]

                def step(q, k, v, state, lg, rev=(d == 1)):
                    return _ret_chunk(q, k, v, state, lg, rev, (_d_nn, _d_nt, _d_tn))

                _, vjp = jax.vjp(step, p_ref[:, cols[0]], p_ref[:, cols[1]], p_ref[:, cols[2]], st_ref[d, h],
                                 lg_ref[d, h][:, :1])
                dy_h = jnp.where(on_ctx[d], 0.0, dy_ref[:, h * HEAD_DIM:(h + 1) * HEAD_DIM])
                grads = vjp((dy_h, ds_sc[d, h]))
                for part in range(3):
                    dp_ref[:, cols[part]] = grads[part]
                ds_sc[d, h] = grads[3]
                dlg_ref[d, h] += jnp.broadcast_to(grads[4], (1, HEAD_DIM))

    def at(d):
        return lambda sp: (cidx(d, nc - 1 - sp), 0)

    def dy_at(d):
        return lambda sp: (jnp.maximum(cidx(d, nc - 1 - sp) - n_ctx, 0), 0)

    dp_shape = jax.ShapeDtypeStruct((tk_all, 3 * rw), F32)
    lg_spec = pl.BlockSpec(lgv.shape, lambda sp: (0, 0, 0, 0))
    return pl.pallas_call(
        body, name="ret_bwd", grid=(nc,),
        in_specs=[pl.BlockSpec((RET_CHUNK, 3 * rw), at(0)), pl.BlockSpec((RET_CHUNK, 3 * rw), at(1)),
                  pl.BlockSpec((2, nh, None, HEAD_DIM, HEAD_DIM), lambda sp: (0, 0, nc - 1 - sp, 0, 0)),
                  pl.BlockSpec((RET_CHUNK, rw), dy_at(0)), pl.BlockSpec((RET_CHUNK, rw), dy_at(1)), lg_spec],
        out_specs=[pl.BlockSpec((RET_CHUNK, 3 * rw), at(0)), pl.BlockSpec((RET_CHUNK, 3 * rw), at(1)), lg_spec],
        out_shape=[dp_shape, dp_shape, jax.ShapeDtypeStruct(lgv.shape, F32)],
        scratch_shapes=[pltpu.VMEM((2, nh, HEAD_DIM, HEAD_DIM), F32)],
        compiler_params=_params(("arbitrary",)),
    )(pr, pr, states, dy, dy, lgv)


FLIP_X, FLIP_Y, FLIP_XY, FLIP_C = (1, 0, 0), (0, 1, 0), (1, 1, 0), (0, 0, 1)
CHIP_FLIPS = ((FLIP_X, 2), (FLIP_Y, 1), (FLIP_XY, 3))


def _flip(me, mask):
    return tuple(1 - v if m else v for v, m in zip(me, mask))


def _comm(name, ins, out_shapes, plan, n_remote, n_local, aliases=None):
    n_in, n_out = len(ins), len(out_shapes)

    def body(*refs):
        in_refs = refs[:n_in]
        out_refs = refs[n_in:n_in + n_out]
        send_sems, recv_sems, local_sems = refs[n_in + n_out:]
        me = (lax.axis_index("x"), lax.axis_index("y"), lax.axis_index("c"))
        local, phases = plan(in_refs, out_refs, me)
        local_copies = [pltpu.make_async_copy(s, d, local_sems.at[i]) for i, (s, d) in enumerate(local)]
        for cp in local_copies:
            cp.start()
        sent = []
        kk = 0
        for phase in phases:
            arrivals = []
            for mask, src, dst, landing in phase:
                peer = _flip(me, mask)
                cp = pltpu.make_async_remote_copy(src_ref=src, dst_ref=dst, send_sem=send_sems.at[kk],
                                                  recv_sem=recv_sems.at[kk], device_id=peer,
                                                  device_id_type=pl.DeviceIdType.MESH)
                cp.start()
                sent.append(cp)
                arrivals.append(pltpu.make_async_remote_copy(
                    src_ref=landing, dst_ref=landing, send_sem=send_sems.at[kk], recv_sem=recv_sems.at[kk],
                    device_id=peer, device_id_type=pl.DeviceIdType.MESH))
                kk += 1
            for cp in arrivals:
                cp.wait_recv()
        for cp in sent:
            cp.wait_send()
        for cp in local_copies:
            cp.wait()

    any_spec = pl.BlockSpec(memory_space=pl.ANY)
    return pl.pallas_call(
        body, name=name,
        in_specs=[any_spec] * n_in, out_specs=[any_spec] * n_out, out_shape=list(out_shapes),
        scratch_shapes=[pltpu.SemaphoreType.DMA((n_remote,)), pltpu.SemaphoreType.DMA((n_remote,)),
                        pltpu.SemaphoreType.DMA((max(n_local, 1),))],
        input_output_aliases=aliases or {},
    )(*ins)


def _ds(start, size):
    return pl.ds(pl.multiple_of(start * size, 8), size)


def _all_gather8(name, v):
    masks = [(a, b, cc) for a in (0, 1) for b in (0, 1) for cc in (0, 1)][1:]

    def index(p):
        return 4 * p[0] + 2 * p[1] + p[2]

    def plan(in_refs, out_refs, me):
        (src,), (out,) = in_refs, out_refs
        local = [(src, out.at[index(me)])]
        phase = [(m, src, out.at[index(me)], out.at[index(_flip(me, m))]) for m in masks]
        return local, [phase]

    return _comm(name, [v], [jax.ShapeDtypeStruct((N_DEV,) + v.shape, v.dtype)], plan, len(masks), 1)[0]


def _gather_row(name, row):
    n = row.shape[1]
    n_pad = -(-n // (8 * LANES_V7X)) * (8 * LANES_V7X)
    v = jnp.pad(row, ((0, 0), (0, n_pad - n))).reshape(8, n_pad // 8)
    return _all_gather8(name, v).reshape(N_DEV, n_pad)[:, :n]


class _Sharded:
    def __init__(self, kind, rows, cols):
        self.kind, self.rows, self.cols = kind, rows, cols
        self.shard_shape = (rows, cols // N_CHIP) if kind == "col" else (rows // N_CHIP, cols)
        self.half_shape = (rows // 2, cols) if kind == "col" else (rows, cols // 2)
        self.piece_shape = (rows // 2, cols // N_CHIP) if kind == "col" else (rows // N_CHIP, cols // 2)

    def shard_of_full(self, ref, s):
        if self.kind == "col":
            return ref.at[:, _ds(s, self.cols // N_CHIP)]
        return ref.at[_ds(s, self.rows // N_CHIP), :]

    def half_of_full(self, ref, h):
        if self.kind == "col":
            return ref.at[_ds(h, self.rows // 2), :]
        return ref.at[:, _ds(h, self.cols // 2)]

    def piece_of_full(self, ref, s, h):
        if self.kind == "col":
            return ref.at[_ds(h, self.rows // 2), _ds(s, self.cols // N_CHIP)]
        return ref.at[_ds(s, self.rows // N_CHIP), _ds(h, self.cols // 2)]

    def half_of_shard(self, ref, h):
        if self.kind == "col":
            return ref.at[_ds(h, self.rows // 2), :]
        return ref.at[:, _ds(h, self.cols // 2)]

    def shard_of_half(self, ref, s):
        if self.kind == "col":
            return ref.at[:, _ds(s, self.cols // N_CHIP)]
        return ref.at[_ds(s, self.rows // N_CHIP), :]


def _place_shard(meta, w, s_arr):
    r, cols = w.shape
    tr = _tile(r, 256, 16)
    nr = r // tr

    def body(s_ref, w_ref, o_ref):
        o_ref[...] = w_ref[...].astype(BF16)

    if meta.kind == "col":
        o_map = lambda i, s_ref: (i, s_ref[0])
    else:
        o_map = lambda i, s_ref: (i + s_ref[0] * nr, 0)
    return pl.pallas_call(
        body, name="place_shard",
        grid_spec=pltpu.PrefetchScalarGridSpec(
            num_scalar_prefetch=1, grid=(nr,),
            in_specs=[pl.BlockSpec((tr, cols), lambda i, s_ref: (i, 0))],
            out_specs=pl.BlockSpec((tr, cols), o_map)),
        out_shape=jax.ShapeDtypeStruct((meta.rows, meta.cols), BF16),
        compiler_params=_params(("parallel",)),
    )(s_arr, w)


def _gather_copies(metas, over_ici):
    def copies(fulls, me):
        x, y, c = me
        s_me = 2 * x + y
        out = []
        for meta, full in zip(metas, fulls):
            for mask, bits in CHIP_FLIPS:
                s_peer = jnp.bitwise_xor(s_me, bits)
                if over_ici:
                    out.append((mask, meta.piece_of_full(full, s_me, c), meta.piece_of_full(full, s_me, c),
                                meta.piece_of_full(full, s_peer, c)))
                else:
                    out.append((FLIP_C, meta.piece_of_full(full, s_peer, c), meta.piece_of_full(full, s_peer, c),
                                meta.piece_of_full(full, s_peer, 1 - c)))
        return out
    return copies


def _gather_forward(name, metas, fulls):
    nt = len(metas)
    copies = _gather_copies(metas, False)
    outs = [jax.ShapeDtypeStruct((m.rows, m.cols), BF16) for m in metas]
    return _comm(name, list(fulls), outs, lambda ins, outs_, me: ([], [copies(outs_, me)]), 3 * nt, 0,
                 aliases={i: i for i in range(nt)})


HBM_SPEC = pl.BlockSpec(memory_space=pltpu.HBM)
SEM_SPEC = pl.BlockSpec(memory_space=pltpu.SEMAPHORE)
SPLIT_EFFECT = pltpu.SideEffectType.DATAFLOW_SIDE_EFFECTING


def _split_start(name, bufs, groups, after):
    nb, ng = len(bufs), len(groups)
    n_in = nb + (0 if after is None else 1)

    def body(*refs):
        buf_refs = refs[:nb]
        sem_refs = refs[n_in:n_in + 2 * ng]
        token = refs[-1]
        me = (lax.axis_index("x"), lax.axis_index("y"), lax.axis_index("c"))
        for gi, (lo, n_bufs, copies, _) in enumerate(groups):
            for kk, (mask, src, dst, _) in enumerate(copies(buf_refs[lo:lo + n_bufs], me)):
                pltpu.make_async_remote_copy(src_ref=src, dst_ref=dst, send_sem=sem_refs[2 * gi].at[kk],
                                             recv_sem=sem_refs[2 * gi + 1].at[kk], device_id=_flip(me, mask),
                                             device_id_type=pl.DeviceIdType.MESH).start()
        token[...] = jnp.zeros_like(token)

    out_shape = []
    for _, _, _, n in groups:
        out_shape += [pltpu.SemaphoreType.DMA((n,)), pltpu.SemaphoreType.DMA((n,))]
    out_shape += [pltpu.HBM(b.shape, b.dtype) for b in bufs] + [jax.ShapeDtypeStruct((8, LANES_V7X), F32)]
    res = pl.pallas_call(
        body, name=name, out_shape=tuple(out_shape),
        in_specs=(HBM_SPEC,) * nb + (pl.BlockSpec(memory_space=pl.ANY),) * (n_in - nb),
        out_specs=(SEM_SPEC,) * (2 * ng) + (HBM_SPEC,) * nb + (pl.BlockSpec(memory_space=pltpu.VMEM),),
        input_output_aliases={i: 2 * ng + i for i in range(nb)},
        compiler_params=pltpu.CompilerParams(has_side_effects=SPLIT_EFFECT),
    )(*[pltpu.with_memory_space_constraint(b, pltpu.HBM) for b in bufs], *([] if after is None else [after]))
    sems = [(res[2 * gi], res[2 * gi + 1]) for gi in range(ng)]
    return sems, list(res[2 * ng:2 * ng + nb]), res[-1]


def _split_wait(name, sems, bufs, copies, after):
    nb = len(bufs)

    def body(*refs):
        buf_refs = refs[:nb]
        send_sems, recv_sems = refs[nb], refs[nb + 1]
        me = (lax.axis_index("x"), lax.axis_index("y"), lax.axis_index("c"))
        for kk, (mask, _, _, landing) in enumerate(copies(buf_refs, me)):
            cp = pltpu.make_async_remote_copy(src_ref=landing, dst_ref=landing, send_sem=send_sems.at[kk],
                                              recv_sem=recv_sems.at[kk], device_id=_flip(me, mask),
                                              device_id_type=pl.DeviceIdType.MESH)
            cp.wait_send()
            cp.wait_recv()

    return list(pl.pallas_call(
        body, name=name, out_shape=tuple(pltpu.HBM(b.shape, b.dtype) for b in bufs),
        in_specs=(HBM_SPEC,) * nb + (SEM_SPEC, SEM_SPEC, pl.BlockSpec(memory_space=pl.ANY)),
        out_specs=(HBM_SPEC,) * nb,
        input_output_aliases={i: i for i in range(nb)},
        compiler_params=pltpu.CompilerParams(has_side_effects=SPLIT_EFFECT),
    )(*bufs, sems[0], sems[1], after))


N_REDUCE_PIECES = 7


def _reduce_copies(metas):
    def copies(refs, me):
        x, y, c = me
        s_me = 2 * x + y
        out = []
        for m, g, land in zip(metas, refs[:len(metas)], refs[len(metas):]):
            for kk, (mask, bits) in enumerate(CHIP_FLIPS):
                s_peer = jnp.bitwise_xor(s_me, bits)
                out.append((mask, m.piece_of_full(g, s_peer, c), land.at[kk], land.at[kk]))
                out.append((mask[:2] + (1,), m.piece_of_full(g, s_peer, 1 - c), land.at[3 + kk], land.at[3 + kk]))
            out.append((FLIP_C, m.piece_of_full(g, s_me, 1 - c), land.at[6], land.at[6]))
        return out
    return copies


def _reduce_start(name, metas, grads):
    lands = [lax.empty((N_REDUCE_PIECES,) + m.piece_shape, BF16) for m in metas]
    bufs = list(grads) + lands
    sems, thru, token = _split_start(name, bufs, [(0, len(bufs), _reduce_copies(metas),
                                                   N_REDUCE_PIECES * len(metas))], None)
    return sems[0], thru, token


def _reduce_wait(name, metas, sems, thru, after):
    done = _split_wait(name, sems, thru, _reduce_copies(metas), after)
    return done[:len(metas)], done[len(metas):]


def _share_halves(metas, shards):
    def plan(in_refs, out_refs, me):
        c = me[2]
        phase = [(FLIP_C, m.half_of_shard(g, c), m.half_of_shard(g, c), m.half_of_shard(g, 1 - c))
                 for m, g in zip(metas, out_refs)]
        return [], [phase]

    outs = [jax.ShapeDtypeStruct(m.shard_shape, F32) for m in metas]
    return _comm("share_halves", list(shards), outs, plan, len(metas), 0,
                 aliases={i: i for i in range(len(metas))})


def _sum_pieces(meta, grad, landed, s_arr, c_arr):
    pr, pc = meta.piece_shape
    tr = _tile(pr, 256, 16)
    tc = _tile(pc, 2048)
    nr, ncol = pr // tr, pc // tc

    def body(s_ref, c_ref, p_ref, l_ref, o_ref):
        acc = p_ref[...].astype(F32)
        for kk in range(N_REDUCE_PIECES):
            acc = acc + l_ref[kk].astype(F32)
        o_ref[...] = acc

    if meta.kind == "col":
        p_map = lambda i, j, s_ref, c_ref: (i + c_ref[0] * nr, j + s_ref[0] * ncol)
        o_map = lambda i, j, s_ref, c_ref: (i + c_ref[0] * nr, j)
    else:
        p_map = lambda i, j, s_ref, c_ref: (i + s_ref[0] * nr, j + c_ref[0] * ncol)
        o_map = lambda i, j, s_ref, c_ref: (i, j + c_ref[0] * ncol)
    blk = (tr, tc)
    return pl.pallas_call(
        body, name="sum_pieces",
        grid_spec=pltpu.PrefetchScalarGridSpec(
            num_scalar_prefetch=2, grid=(nr, ncol),
            in_specs=[pl.BlockSpec(blk, p_map),
                      pl.BlockSpec((N_REDUCE_PIECES,) + blk, lambda i, j, s_ref, c_ref: (0, i, j))],
            out_specs=pl.BlockSpec(blk, o_map)),
        out_shape=jax.ShapeDtypeStruct(meta.shard_shape, F32),
        compiler_params=_params(("parallel", "parallel")),
    )(s_arr, c_arr, grad, landed)


def _adam_rows(rows, sel, fulls):
    w, g, m, v = rows
    m2 = ADAM_B1 * m + (1.0 - ADAM_B1) * g
    v2 = ADAM_B2 * v + (1.0 - ADAM_B2) * jnp.square(g)
    m_hat = m2 / (1.0 - ADAM_B1 ** ADAM_STEP)
    v_hat = v2 / (1.0 - ADAM_B2 ** ADAM_STEP)
    delta = -ADAM_LR * (m_hat / (jnp.sqrt(v_hat) + ADAM_EPS) + ADAM_WD * w)
    return [delta, m2, v2], []


def _adamw(w, g, m, v):
    r, c = w.shape
    tr = _tile(r, 128, 8)
    outs = _rowwise("adamw", _adam_rows, n_tiles=r // tr, tr=tr,
                    row_ins=[(w, 0, None), (g, 0, None), (m, 0, None), (v, 0, None)],
                    row_outs=[(r, c, F32, 0)] * 3)
    return outs[0], outs[1], outs[2]


def _ada_fwd(cg, w, b):
    d, n = w.shape
    tn = _tile(n, 512)

    def body(c_ref, w_ref, b_ref, o_ref):
        a = _silu(c_ref[...]).astype(BF16)
        o_ref[...] = jnp.dot(a, w_ref[...].astype(BF16), preferred_element_type=F32) + b_ref[...]

    return pl.pallas_call(
        body, name="ada_fwd", grid=(n // tn,),
        in_specs=[pl.BlockSpec(cg.shape, lambda j: (0, 0)), pl.BlockSpec((d, tn), lambda j: (0, j)),
                  pl.BlockSpec((1, tn), lambda j: (0, j))],
        out_specs=pl.BlockSpec((cg.shape[0], tn), lambda j: (0, j)),
        out_shape=jax.ShapeDtypeStruct((cg.shape[0], n), F32),
        compiler_params=_params(("parallel",)),
    )(cg, w, b)


def _ada_bwd(cg, dm, w):
    d, n = w.shape
    tn = _tile(n, 512)
    nj = n // tn

    def body(c_ref, dm_ref, w_ref, gw_ref, da_ref, acc):
        j = pl.program_id(0)

        @pl.when(j == 0)
        def _():
            acc[...] = jnp.zeros_like(acc)

        a = _silu(c_ref[...]).astype(BF16)
        dmv = dm_ref[...].astype(BF16)
        gw_ref[...] = lax.dot_general(a, dmv, TN_DIMS, preferred_element_type=F32)
        acc[...] += lax.dot_general(dmv, w_ref[...].astype(BF16), NT_DIMS, preferred_element_type=F32)

        @pl.when(j == nj - 1)
        def _():
            da_ref[...] = acc[...]

    return pl.pallas_call(
        body, name="ada_bwd", grid=(nj,),
        in_specs=[pl.BlockSpec(cg.shape, lambda j: (0, 0)), pl.BlockSpec((dm.shape[0], tn), lambda j: (0, j)),
                  pl.BlockSpec((d, tn), lambda j: (0, j))],
        out_specs=[pl.BlockSpec((d, tn), lambda j: (0, j)), pl.BlockSpec(cg.shape, lambda j: (0, 0))],
        out_shape=[jax.ShapeDtypeStruct((d, n), F32), jax.ShapeDtypeStruct(cg.shape, F32)],
        scratch_shapes=[pltpu.VMEM(cg.shape, F32)],
        compiler_params=_params(("arbitrary",)),
    )(cg, dm, w)


def _small_reduce(gathered, logits, n_mod_cols, lg_off, loss_off, loss_cols):
    npk = gathered.shape[1]

    def body(g_ref, lo_ref, tot_ref, gb_ref, gl_ref, loss_ref):
        acc = g_ref[0:1, :]
        for dd in range(1, N_DEV):
            acc = acc + g_ref[dd:dd + 1, :]
        tot_ref[...] = acc
        gb_ref[...] = acc[:, :n_mod_cols] + acc[:, n_mod_cols:2 * n_mod_cols]
        gl_ref[...] = acc[:, lg_off:lg_off + LANES_V7X] * _sigmoid(-lo_ref[...])
        loss = jnp.sum(acc[:, loss_off:loss_off + loss_cols], axis=1, keepdims=True)
        loss_ref[...] = jnp.broadcast_to(loss, loss_ref.shape)

    lane = jax.ShapeDtypeStruct((1, LANES_V7X), F32)
    return pl.pallas_call(
        body, name="small_reduce",
        out_shape=[jax.ShapeDtypeStruct((1, npk), F32), jax.ShapeDtypeStruct((1, n_mod_cols), F32), lane, lane],
    )(gathered, logits)


def _c_ctx_grad(parts, c_ctx):
    def body(p_ref, c_ref, o_ref):
        tot = p_ref[0:1, :] + p_ref[2:3, :] + p_ref[4:5, :] + p_ref[6:7, :]
        _, vjp = jax.vjp(_silu, c_ref[...])
        o_ref[...] = vjp(tot)[0]

    return pl.pallas_call(body, name="c_ctx_grad", out_shape=jax.ShapeDtypeStruct(c_ctx.shape, F32))(parts, c_ctx)


def _rope_tables(seq, ctx_rows):
    rows = seq // GRID_W
    half = HEAD_DIM // 2
    inv_freq = ROPE_THETA ** (-jnp.arange(0, half, 2, dtype=F32) / half)
    ang_row = jnp.arange(rows, dtype=F32)[:, None] * inv_freq
    ang_col = jnp.arange(GRID_W, dtype=F32)[:, None] * inv_freq

    def spread(fn):
        return jnp.concatenate([jnp.repeat(fn(ang_row), GRID_W, axis=0), jnp.tile(fn(ang_col), (rows, 1))], axis=-1)

    cos, sin = spread(jnp.cos), spread(jnp.sin)
    cos_full = jnp.repeat(cos, 2, axis=1)
    sin_signed = jnp.stack([-sin, sin], axis=-1).reshape(seq, HEAD_DIM)
    cos_full = jnp.concatenate([jnp.ones((ctx_rows, HEAD_DIM), F32), cos_full], axis=0)
    sin_signed = jnp.concatenate([jnp.zeros((ctx_rows, HEAD_DIM), F32), sin_signed], axis=0)
    return cos_full, sin_signed


def _qk_rot(p, gain, cos_full, sin_signed):
    r = _rmsn(p) * gain
    return r * cos_full + _swap_pairs(r) * sin_signed


def _qk_rot_bwd(g, p, gain, cos_full, sin_signed):
    g1 = g * cos_full + _swap_pairs(g * sin_signed)
    _, vjp = jax.vjp(lambda pp, gn: _rmsn(pp) * gn, p, gain)
    return vjp(g1)


def kernel(x, c, ctx, c_ctx, w_ada, b_ada, ffn1_w_in, ffn1_w_out, mix_w_in, attn_q_gain, attn_k_gain, ret_decay_logit, w_proj_attn, w_proj_ret, mix_w_out, ffn2_w_in, ffn2_w_out, final_norm, loss_target, m_c_ctx, m_w_ada, m_b_ada, m_ffn1_w_in, m_ffn1_w_out, m_mix_w_in, m_attn_q_gain, m_attn_k_gain, m_ret_decay_logit, m_w_proj_attn, m_w_proj_ret, m_mix_w_out, m_ffn2_w_in, m_ffn2_w_out, m_final_norm, v_c_ctx, v_w_ada, v_b_ada, v_ffn1_w_in, v_ffn1_w_out, v_mix_w_in, v_attn_q_gain, v_attn_k_gain, v_ret_decay_logit, v_w_proj_attn, v_w_proj_ret, v_mix_w_out, v_ffn2_w_in, v_ffn2_w_out, v_final_norm):
    xi, yi, ci = lax.axis_index("x"), lax.axis_index("y"), lax.axis_index("c")
    dev = 4 * xi + 2 * yi + ci
    s_me = 2 * xi + yi
    c_arr = jnp.reshape(ci, (1,)).astype(jnp.int32)
    s_arr = jnp.reshape(s_me, (1,)).astype(jnp.int32)

    t, d = x.shape[1], x.shape[2]
    tc = ctx.shape[1]
    tk = tc + t
    ff = ffn1_w_out.shape[1] * N_CHIP
    aw = w_proj_attn.shape[1]
    rw = w_proj_ret.shape[1]
    pw = mix_w_in.shape[2] * N_CHIP
    kvw = (pw - aw - 4 * rw - 2 * d) // 2
    groups = aw // kvw
    n_ret_heads = rw // HEAD_DIM
    mod_cols = N_MOD * d
    tr = _tile(tc, 256, 32)
    nt_all, nt_x, ctx_tiles = tk // tr, t // tr, tc // tr

    c_rows = _gather_row("gather_c", c)
    cg = jnp.concatenate([c_rows, c_ctx[None, :], jnp.zeros((7, d), F32)], axis=0)
    w_ada_l = w_ada[0]
    ada_cols = w_ada_l.shape[1]
    b_ada_l = lax.dynamic_slice_in_dim(b_ada, s_me * ada_cols, ada_cols, axis=1)
    mod_shard = _ada_fwd(cg, w_ada_l, b_ada_l)
    mod_g = _all_gather8("gather_mod", mod_shard)
    mod_full = jnp.concatenate([mod_g[0], mod_g[2], mod_g[4], mod_g[6]], axis=1)
    mod_x = lax.dynamic_slice_in_dim(mod_full, dev, 1, axis=0).reshape(N_MOD, d)
    mod_c = mod_full[8].reshape(N_MOD, d)
    mods = jnp.stack([mod_c, mod_x])

    big = [("col", ffn1_w_in), ("row", ffn1_w_out), ("col", mix_w_in), ("col", w_proj_attn), ("col", w_proj_ret),
           ("row", mix_w_out), ("col", ffn2_w_in), ("row", ffn2_w_out)]
    metas = []
    for kind, w in big:
        r_l, c_l = w.shape[1:]
        metas.append(_Sharded(kind, r_l, c_l * N_CHIP) if kind == "col" else _Sharded(kind, r_l * N_CHIP, c_l))
    placed = [_place_shard(m, w[0], s_arr) for m, (_, w) in zip(metas, big)]
    layer_groups = ((0, 1), (1, 2), (2, 6), (6, 8))
    gather_sems, placed, token = _split_start(
        "gather_start", placed,
        [(lo, hi - lo, _gather_copies(metas[lo:hi], True), 3 * (hi - lo)) for lo, hi in layer_groups], mods)
    mods = mods + token[0, 0]

    def weights_of(gi, after):
        lo, hi = layer_groups[gi]
        arrived = _split_wait("gather_wait_%d" % gi, gather_sems[gi], placed[lo:hi],
                              _gather_copies(metas[lo:hi], True), after)
        return _gather_forward("gather_forward_%d" % gi, metas[lo:hi], arrived)

    cos_full, sin_signed = _rope_tables(t, tc)
    q_gain, k_gain = attn_q_gain, attn_k_gain
    log_gamma = jax.nn.log_sigmoid(ret_decay_logit[0])
    lgv = jnp.broadcast_to(log_gamma[:, :, None, None], (2, n_ret_heads, 1, HEAD_DIM))

    def norm_mod(name, h, n_tiles, off, i_shift, i_scale):
        def fn(rows, sel, fulls):
            return [_rmsn(rows[0]) * (1.0 + sel(i_scale)) + sel(i_shift)], []
        return _rowwise(name, fn, n_tiles=n_tiles, tr=tr, row_ins=[(h, 0, None)],
                        row_outs=[(h.shape[0], d, BF16, 0)], sel_in=mods, sel_off=off, ctx_rows=tc)[0]

    def resid_norm(name, h, h_off, f, n_tiles, off, i_gate, coef, i_shift, i_scale):
        def fn(rows, sel, fulls):
            hn = rows[0] + coef * sel(i_gate) * rows[1]
            return [hn, _rmsn(hn) * (1.0 + sel(i_scale)) + sel(i_shift)], []
        return _rowwise(name, fn, n_tiles=n_tiles, tr=tr, row_ins=[(h, h_off, None), (f, 0, None)],
                        row_outs=[(f.shape[0], d, F32, 0), (f.shape[0], d, BF16, 0)], sel_in=mods, sel_off=off,
                        ctx_rows=tc)

    h0 = jnp.concatenate([ctx[0], x[0]], axis=0)
    n1 = norm_mod("norm_mod1", h0, nt_all, 0, 0, 1)
    w1i, = weights_of(0, n1)
    hm1, ua1, ub1 = _mm_swiglu("ffn1_in", n1, w1i)
    w1o, = weights_of(1, hm1)
    f1 = _mm("ffn1_out", hm1, w1o, "nn", BF16)
    h1, n2 = resid_norm("resid_norm1", h0, 0, f1, nt_all, 0, 2, 0.5, 3, 4)
    wmi, wpa, wpr, wmo = weights_of(2, n2)
    p_q = _mm("mix_in_q", n2, wmi, "nn", F32, 0, aw)
    p_kv = _mm("mix_in_kv", n2, wmi, "nn", F32, aw, 2 * kvw)
    p_r = _mm("mix_in_ret", n2, wmi, "nn", F32, aw + 2 * kvw, 3 * rw)
    p_gr = _mm("mix_in_gr", n2, wmi, "nn", BF16, aw + 2 * kvw + 3 * rw, rw)
    p_gab = _mm("mix_in_gab", n2, wmi, "nn", BF16, aw + 2 * kvw + 4 * rw, 2 * d)

    def q_prep(rows, sel, fulls):
        p, cf, ss = rows
        return _heads_map(lambda ph: [_qk_rot(ph, fulls[0], cf, ss) * QSCALE], [p], aw), []

    q_rot = _rowwise("q_prep", q_prep, n_tiles=nt_x, tr=tr,
                     row_ins=[(p_q, ctx_tiles, None), (cos_full, ctx_tiles, None), (sin_signed, ctx_tiles, None)],
                     row_outs=[(t, aw, BF16, 0)], full_ins=[q_gain])[0]

    def kv_prep(rows, sel, fulls):
        p, cf, ss = rows
        k_rot = _heads_map(lambda ph: [_qk_rot(ph, fulls[0], cf, ss)], [p[:, :kvw]], kvw)[0]
        v_ones = _heads_map(lambda vh: [jnp.concatenate([vh, jnp.ones_like(vh)], axis=1)], [p[:, kvw:]], kvw)[0]
        return [k_rot, v_ones], []

    k_rot, v_att = _rowwise("kv_prep", kv_prep, n_tiles=nt_all, tr=tr,
                            row_ins=[(p_kv, 0, None), (cos_full, 0, None), (sin_signed, 0, None)],
                            row_outs=[(tk, kvw, BF16, 0), (tk, 2 * kvw, BF16, 0)], full_ins=[k_gain])

    ya, lse = _flash_fwd(q_rot, k_rot, v_att, groups)
    y_fwd, y_bwd, states = _ret_fwd(p_r, lgv, tc)

    def ret_out_fn(yf, yb, gr):
        return [_silu(gr) * _rmsn(yf + yb)]

    def ret_out(rows, sel, fulls):
        return _heads_map(ret_out_fn, rows, rw), []

    y_rows = [(y_fwd, ctx_tiles, None), (y_bwd, ctx_tiles, None), (p_gr, ctx_tiles, None)]
    yr = _rowwise("ret_out", ret_out, n_tiles=nt_x, tr=tr, row_ins=y_rows, row_outs=[(t, rw, BF16, 0)])[0]

    pa = _mm("proj_attn", ya, wpa, "nn", BF16)
    prj = _mm("proj_ret", yr, wpr, "nn", BF16)

    def merge_fn(a, r, ga, gb):
        return _sigmoid(ga) * a + _sigmoid(gb) * r

    gate_rows = [(p_gab, ctx_tiles, (d, 0)), (p_gab, ctx_tiles, (d, 1))]
    z = _rowwise("merge", lambda rows, sel, fulls: ([merge_fn(*rows)], []), n_tiles=nt_x, tr=tr,
                 row_ins=[(pa, 0, None), (prj, 0, None)] + gate_rows, row_outs=[(t, d, BF16, 0)])[0]
    fo = _mm("mix_out", z, wmo, "nn", BF16)
    h2, n3 = resid_norm("resid_norm2", h1, ctx_tiles, fo, nt_x, ctx_tiles, 5, 1.0, 6, 7)
    w2i, w2o = weights_of(3, n3)
    hm2, ua2, ub2 = _mm_swiglu("ffn2_in", n3, w2i)
    f2 = _mm("ffn2_out", hm2, w2o, "nn", BF16)

    def loss_fn(rows, sel, fulls):
        h2v, f2v, tgt = rows
        g3 = 0.5 * sel(8)
        y, vjp = jax.vjp(lambda hh, ww: _rmsn(hh) * ww, h2v + g3 * f2v, fulls[0])
        err = y - tgt
        dh, dw = vjp(err / d)
        return [dh, g3 * dh], [0.5 / d * jnp.sum(err * err, axis=0, keepdims=True), dw,
                               jnp.sum(0.5 * dh * f2v, axis=0, keepdims=True)]

    dh3, df2, loss_acc = _rowwise("loss_head", loss_fn, n_tiles=nt_x, tr=tr,
                                  row_ins=[(h2, 0, None), (f2, 0, None), (loss_target[0], 0, None)],
                                  row_outs=[(t, d, F32, 0), (t, d, BF16, 0)], sel_in=mods, sel_off=ctx_tiles,
                                  ctx_rows=tc, full_ins=[final_norm[None, :]], acc_shape=(8, d))
    loss_cols, g_final, dg3 = loss_acc[1, 0:1], loss_acc[1, 1:2], loss_acc[1, 2:3]

    def swiglu_bwd(name, dhm, ua, ub):
        rows_n = dhm.shape[0]
        tr_w = _tile(tr, 128, 32)

        def fn(rows, sel, fulls):
            g, a, b = rows
            _, vjp = jax.vjp(lambda aa, bb: _silu(aa) * bb, a.astype(F32), b.astype(F32))
            da, db = vjp(g)
            return [jnp.concatenate([da, db], axis=1)], []
        return _rowwise(name, fn, n_tiles=rows_n // tr_w, tr=tr_w,
                        row_ins=[(dhm, 0, None), (ua, 0, None), (ub, 0, None)],
                        row_outs=[(rows_n, 2 * ff, BF16, 0)])[0]

    def norm_mod_bwd(name, dn, h, dres, dres_off, n_tiles, off, i_shift, i_scale, gate=None, out_off=0):
        def fn(rows, sel, fulls):
            g, hh, dr = rows[:3]
            if dres_off < 0:
                dr = jnp.where(fulls[-1], 0.0, dr)
            _, vjp = jax.vjp(lambda a, sh, sc: _rmsn(a) * (1.0 + sc) + sh, hh,
                             sel(i_shift), sel(i_scale))
            dhh, dsh, dsc = vjp(g)
            dh = dr + dhh
            if gate is None:
                return [dh], [dsh, dsc]
            return [dh, gate[2] * sel(gate[1]) * dh], [dsh, dsc, jnp.sum(gate[2] * dh * rows[3], axis=0, keepdims=True)]
        n_rows = dn.shape[0] + out_off * tr
        row_ins = [(dn, 0, None), (h, 0, None), (dres, dres_off, None)]
        row_outs = [(n_rows, d, F32, out_off)]
        if gate is not None:
            row_ins.append((gate[0], 0, None))
            row_outs.append((n_rows, d, BF16, out_off))
        return _rowwise(name, fn, n_tiles=n_tiles, tr=tr, row_ins=row_ins, row_outs=row_outs, sel_in=mods,
                        sel_off=off, ctx_rows=tc, acc_shape=(8, d))

    g_w2o = _mm("ffn2_out_dw", hm2, df2, "tn", BF16)
    dhm2 = _mm("ffn2_out_dx", df2, w2o, "nt", F32)
    du2 = swiglu_bwd("swiglu_bwd2", dhm2, ua2, ub2)
    g_w2i = _mm("ffn2_in_dw", n3, du2, "tn", BF16)
    dn3 = _mm("ffn2_in_dx", du2, w2i, "nt", F32)
    dh2, dfo, acc_n3 = norm_mod_bwd("norm_mod_bwd3", dn3, h2, dh3, 0, nt_x, ctx_tiles, 6, 7, gate=(fo, 5, 1.0))

    sems_ffn2, thru_ffn2, token = _reduce_start("reduce_start_ffn2", metas[6:8], [g_w2i, g_w2o])

    g_wmo = _mm("mix_out_dw", z, dfo, "tn", BF16, after=token)
    dz = _mm("mix_out_dx", dfo, wmo, "nt", F32)

    def merge_bwd(rows, sel, fulls):
        g, a, r, ga, gb = rows
        _, vjp = jax.vjp(merge_fn, a, r, ga, gb)
        da, dr, dga, dgb = vjp(g)
        return [da, dr, jnp.concatenate([dga, dgb], axis=1)], []

    dpa, dpr, dgab = _rowwise("merge_bwd", merge_bwd, n_tiles=nt_x, tr=tr,
                              row_ins=[(dz, 0, None), (pa, 0, None), (prj, 0, None)] + gate_rows,
                              row_outs=[(t, d, BF16, 0), (t, d, BF16, 0), (t, 2 * d, BF16, 0)])
    g_wpa = _mm("proj_attn_dw", ya, dpa, "tn", BF16)
    dya = _mm("proj_attn_dx", dpa, wpa, "nt", BF16)
    g_wpr = _mm("proj_ret_dw", yr, dpr, "tn", BF16)
    dyr = _mm("proj_ret_dx", dpr, wpr, "nt", F32)

    def ret_out_bwd(rows, sel, fulls):
        def per_head(g, yf, yb, gr):
            _, vjp = jax.vjp(lambda yy, gg: ret_out_fn(yy, 0.0, gg)[0], yf + yb, gr)
            return list(vjp(g))
        dy, dgr = _heads_map(per_head, rows, rw)
        return [dy, dgr], []

    dy_ret, dgr = _rowwise("ret_out_bwd", ret_out_bwd, n_tiles=nt_x, tr=tr, row_ins=[(dyr, 0, None)] + y_rows,
                           row_outs=[(t, rw, F32, 0), (t, rw, BF16, 0)])
    dp_rf, dp_rb, dlg = _ret_bwd(p_r, states, dy_ret, lgv, tc)
    dp_r = _rowwise("ret_bwd_sum", lambda rows, sel, fulls: ([rows[0] + rows[1]], []), n_tiles=nt_all, tr=tr,
                    row_ins=[(dp_rf, 0, None), (dp_rb, 0, None)], row_outs=[(tk, 3 * rw, BF16, 0)])[0]

    dq_rot, dk_rot, dv_att = _flash_bwd(q_rot, k_rot, v_att, ya, dya, lse, groups)

    def q_prep_bwd(rows, sel, fulls):
        g, p, cf, ss = rows
        gain_acc = []

        def per_head(gh, ph):
            dp, dgain = _qk_rot_bwd(gh * HEAD_DIM ** -0.5, ph, fulls[0], cf, ss)
            gain_acc.append(dgain)
            return [dp]
        dp = _heads_map(per_head, [g, p], aw)[0]
        return [dp], [functools.reduce(lambda a, b: a + b, gain_acc)]

    dp_q, acc_gq = _rowwise("q_prep_bwd", q_prep_bwd, n_tiles=nt_x, tr=tr,
                            row_ins=[(dq_rot, 0, None), (p_q, ctx_tiles, None), (cos_full, ctx_tiles, None),
                                     (sin_signed, ctx_tiles, None)],
                            row_outs=[(t, aw, BF16, 0)], full_ins=[q_gain], acc_shape=(8, HEAD_DIM),
                            sel_off=ctx_tiles, ctx_rows=tc)

    def kv_prep_bwd(rows, sel, fulls):
        gk, gv, p, cf, ss = rows
        gain_acc = []

        def per_head(gh, ph):
            dp, dgain = _qk_rot_bwd(gh, ph, fulls[0], cf, ss)
            gain_acc.append(dgain)
            return [dp]
        dpk = _heads_map(per_head, [gk, p], kvw)[0]
        return [jnp.concatenate([dpk, gv], axis=1)], [functools.reduce(lambda a, b: a + b, gain_acc)]

    dp_kv, acc_gk = _rowwise("kv_prep_bwd", kv_prep_bwd, n_tiles=nt_all, tr=tr,
                             row_ins=[(dk_rot, 0, None), (dv_att, 0, None), (p_kv, 0, (kvw, 0)), (cos_full, 0, None),
                                      (sin_signed, 0, None)],
                             row_outs=[(tk, 2 * kvw, BF16, 0)], full_ins=[k_gain], acc_shape=(8, HEAD_DIM),
                             sel_off=0, ctx_rows=tc)

    def with_ctx_zeros(a):
        return jnp.concatenate([jnp.zeros((tc, a.shape[1]), a.dtype), a], axis=0)

    dp = jnp.concatenate([with_ctx_zeros(dp_q), dp_kv, dp_r, with_ctx_zeros(dgr), with_ctx_zeros(dgab)], axis=1)
    g_wmi = _mm("mix_in_dw", n2, dp, "tn", BF16)
    dn2 = _mm("mix_in_dx", dp, wmi, "nt", F32)
    dh1, df1, acc_n2 = norm_mod_bwd("norm_mod_bwd2", dn2, h1, dh2, -ctx_tiles, nt_all, 0, 3, 4, gate=(f1, 2, 0.5))
    sems_mix, thru_mix, token = _reduce_start("reduce_start_mix", metas[2:6], [g_wmi, g_wpa, g_wpr, g_wmo])

    g_w1o = _mm("ffn1_out_dw", hm1, df1, "tn", BF16, after=token)
    sems_w1o, thru_w1o, token = _reduce_start("reduce_start_ffn1_out", metas[1:2], [g_w1o])
    dhm1 = _mm("ffn1_out_dx", df1, w1o, "nt", F32, after=token)
    du1 = swiglu_bwd("swiglu_bwd1", dhm1, ua1, ub1)
    g_w1i = _mm("ffn1_in_dw", n1, du1, "tn", BF16)
    sems_w1i, thru_w1i, token = _reduce_start("reduce_start_ffn1_in", metas[0:1], [g_w1i])
    dn1 = _mm("ffn1_in_dx", du1, w1i, "nt", F32, after=token)
    dh0, acc_n1 = norm_mod_bwd("norm_mod_bwd1", dn1, h0, dh1, 0, nt_all, 0, 0, 1, out_off=-ctx_tiles)
    grad_x = dh0[None]

    grads_own, landed = [], []
    for name, lo, hi, sems_l, thru_l in (("reduce_wait_ffn1_in", 0, 1, sems_w1i, thru_w1i),
                                         ("reduce_wait_ffn1_out", 1, 2, sems_w1o, thru_w1o),
                                         ("reduce_wait_mix", 2, 6, sems_mix, thru_mix),
                                         ("reduce_wait_ffn2", 6, 8, sems_ffn2, thru_ffn2)):
        grads_l, landed_l = _reduce_wait(name, metas[lo:hi], sems_l, thru_l, dh0)
        grads_own += grads_l
        landed += landed_l
    pieces = [_sum_pieces(m, g, l, s_arr, c_arr) for m, g, l in zip(metas, grads_own, landed)]
    grads_big = _share_halves(metas, pieces)

    zero_row = jnp.zeros((1, d), F32)
    dmod_x = jnp.concatenate([acc_n1[1, 0:1], acc_n1[1, 1:2], acc_n2[1, 2:3], acc_n2[1, 0:1], acc_n2[1, 1:2],
                              acc_n3[1, 2:3], acc_n3[1, 0:1], acc_n3[1, 1:2], dg3], axis=1)
    dmod_c = jnp.concatenate([acc_n1[0, 0:1], acc_n1[0, 1:2], acc_n2[0, 2:3], acc_n2[0, 0:1], acc_n2[0, 1:2]]
                             + [zero_row] * 4, axis=1)
    dlg_row = jnp.pad(dlg[:, :, 0, 0].reshape(1, 2 * n_ret_heads), ((0, 0), (0, LANES_V7X - 2 * n_ret_heads)))
    packed = jnp.concatenate([dmod_x, dmod_c, acc_gq[1, 0:1], acc_gk[0, 0:1] + acc_gk[1, 0:1], dlg_row,
                              g_final, loss_cols], axis=1)
    off_gq = 2 * mod_cols
    off_gk = off_gq + LANES_V7X
    off_lg = off_gk + LANES_V7X
    off_fn = off_lg + LANES_V7X
    off_loss = off_fn + d
    gathered = _gather_row("gather_small", packed)
    logits_row = jnp.pad(ret_decay_logit.reshape(1, 2 * n_ret_heads), ((0, 0), (0, LANES_V7X - 2 * n_ret_heads)))
    totals, g_b_ada, g_decay, loss_row = _small_reduce(gathered, logits_row, mod_cols, off_lg, off_loss, d)
    loss = loss_row[0, 0]

    dm = jnp.concatenate([gathered[:, :mod_cols], totals[:, mod_cols:2 * mod_cols],
                          jnp.zeros((7, mod_cols), F32)], axis=0)
    dm_l = lax.dynamic_slice_in_dim(dm, s_me * ada_cols, ada_cols, axis=1)
    g_w_ada, da_part = _ada_bwd(cg, dm_l, w_ada_l)
    da_rows = _gather_row("gather_dc", da_part[8:9])
    g_c_ctx = _c_ctx_grad(da_rows, c_ctx[None, :])

    def as2d(a):
        return a.reshape(-1, a.shape[-1])

    grads = {
        "c_ctx": g_c_ctx, "w_ada": g_w_ada, "b_ada": g_b_ada,
        "ffn1_w_in": grads_big[0], "ffn1_w_out": grads_big[1], "mix_w_in": grads_big[2],
        "attn_q_gain": totals[:, off_gq:off_gq + HEAD_DIM], "attn_k_gain": totals[:, off_gk:off_gk + HEAD_DIM],
        "ret_decay_logit": g_decay[:, :2 * n_ret_heads],
        "w_proj_attn": grads_big[3], "w_proj_ret": grads_big[4], "mix_w_out": grads_big[5],
        "ffn2_w_in": grads_big[6], "ffn2_w_out": grads_big[7], "final_norm": totals[:, off_fn:off_fn + d],
    }
    weights = {"c_ctx": (c_ctx, m_c_ctx, v_c_ctx), "w_ada": (w_ada, m_w_ada, v_w_ada),
               "b_ada": (b_ada, m_b_ada, v_b_ada), "ffn1_w_in": (ffn1_w_in, m_ffn1_w_in, v_ffn1_w_in),
               "ffn1_w_out": (ffn1_w_out, m_ffn1_w_out, v_ffn1_w_out), "mix_w_in": (mix_w_in, m_mix_w_in, v_mix_w_in),
               "attn_q_gain": (attn_q_gain, m_attn_q_gain, v_attn_q_gain),
               "attn_k_gain": (attn_k_gain, m_attn_k_gain, v_attn_k_gain),
               "ret_decay_logit": (ret_decay_logit, m_ret_decay_logit, v_ret_decay_logit),
               "w_proj_attn": (w_proj_attn, m_w_proj_attn, v_w_proj_attn),
               "w_proj_ret": (w_proj_ret, m_w_proj_ret, v_w_proj_ret), "mix_w_out": (mix_w_out, m_mix_w_out, v_mix_w_out),
               "ffn2_w_in": (ffn2_w_in, m_ffn2_w_in, v_ffn2_w_in), "ffn2_w_out": (ffn2_w_out, m_ffn2_w_out, v_ffn2_w_out),
               "final_norm": (final_norm, m_final_norm, v_final_norm)}
    out_g, out_d, out_m, out_v = [], [], [], []
    for name, (w, m, v) in weights.items():
        shape = w.shape
        if name == "ret_decay_logit":
            w2, m2, v2 = (a.reshape(1, -1) for a in (w, m, v))
        else:
            w2, m2, v2 = as2d(w), as2d(m), as2d(v)
        g2 = grads[name].reshape(w2.shape)
        delta, new_m, new_v = _adamw(w2, g2, m2, v2)
        out_g.append(g2.reshape(shape))
        out_d.append(delta.reshape(shape))
        out_m.append(new_m.reshape(shape))
        out_v.append(new_v.reshape(shape))
    return (loss, grad_x, *out_g, *out_d, *out_m, *out_v)
```

```python
import functools
import math

import jax
import jax.numpy as jnp
from jax import lax
from jax.experimental import pallas as pl
from jax.experimental.pallas import tpu as pltpu

F32 = jnp.float32
BF16 = jnp.bfloat16

HEAD_DIM = 128
GRID_W = 64
ROPE_THETA = 10000.0
NORM_EPS = 1e-6
N_MOD = 9
RET_CHUNK = 128
ADAM_LR = 0.001
ADAM_B1 = 0.9
ADAM_B2 = 0.999
ADAM_EPS = 1e-08
ADAM_WD = 0.01
ADAM_STEP = 10

N_DEV = 8
N_CHIP = 4
LANES_V7X = 128
MXU_WIDTH_V7X = 256
VMEM_LIMIT_V7X = 52 * 1024 * 1024

NT_DIMS = (((1,), (1,)), ((), ()))
TN_DIMS = (((0,), (0,)), ((), ()))
NN_DIMS = (((1,), (0,)), ((), ()))


def _tile(n, pref, mult=LANES_V7X):
    if n <= pref:
        return n
    t = (pref // mult) * mult
    while t >= mult:
        if n % t == 0:
            return t
        t -= mult
    return n


def _params(sem):
    return pltpu.CompilerParams(dimension_semantics=sem, vmem_limit_bytes=VMEM_LIMIT_V7X)


def _sigmoid(x):
    return 1.0 / (1.0 + jnp.exp(-x))


def _silu(x):
    return x * _sigmoid(x)


def _rmsn(x):
    return x * lax.rsqrt(jnp.mean(x * x, axis=-1, keepdims=True) + NORM_EPS)


MM_VMEM_BUDGET = 44 * 1024 * 1024


def _divisor_tiles(n, cap):
    ts = [t for t in range(LANES_V7X, min(n, cap) + 1, LANES_V7X) if n % t == 0]
    return ts or [n]


def _mm_tiles(m, n, tk, out_bytes, has_acc):
    best = None
    for tm in _divisor_tiles(m, 1536):
        for tn in _divisor_tiles(n, 2560):
            need = 4 * tk * (tm + tn) + 2 * tm * tn * out_bytes + 4 * tm * tn
            if need > MM_VMEM_BUDGET:
                continue
            score = tm * tn / (tm + tn)
            for tdim in (tm, tn):
                if tdim % MXU_WIDTH_V7X:
                    score *= 0.85
            if best is None or score > best[0]:
                best = (score, tm, tn)
    return best[1], best[2]


def _mm(name, a, b, mode, out_dtype, b_off=0, n=None, after=None):
    if mode == "nn":
        m, k = a.shape
        n = b.shape[1] if n is None else n
        dims = NN_DIMS
    elif mode == "nt":
        m, k = a.shape
        n = b.shape[0]
        dims = NT_DIMS
    else:
        k, m = a.shape
        n = b.shape[1]
        dims = TN_DIMS
    tk = _tile(k, 2816)
    nk = k // tk
    tm, tn = _mm_tiles(m, math.gcd(n, b_off) if b_off else n, tk, jnp.dtype(out_dtype).itemsize, nk > 1)
    joff = b_off // tn

    def body(a_ref, b_ref, *rest):
        o_ref = rest[0 if after is None else 1]
        if nk == 1:
            o_ref[...] = lax.dot_general(a_ref[...], b_ref[...], dims,
                                         preferred_element_type=F32).astype(o_ref.dtype)
            return
        acc_ref = rest[-1]
        kk = pl.program_id(2)

        @pl.when(kk == 0)
        def _():
            acc_ref[...] = jnp.zeros_like(acc_ref)

        acc_ref[...] += lax.dot_general(a_ref[...], b_ref[...], dims, preferred_element_type=F32)

        @pl.when(kk == nk - 1)
        def _():
            o_ref[...] = acc_ref[...].astype(o_ref.dtype)

    if mode == "nn":
        a_spec = pl.BlockSpec((tm, tk), lambda i, j, kk: (i, kk))
        b_spec = pl.BlockSpec((tk, tn), lambda i, j, kk: (kk, j + joff))
    elif mode == "nt":
        a_spec = pl.BlockSpec((tm, tk), lambda i, j, kk: (i, kk))
        b_spec = pl.BlockSpec((tn, tk), lambda i, j, kk: (j, kk))
    else:
        a_spec = pl.BlockSpec((tk, tm), lambda i, j, kk: (kk, i))
        b_spec = pl.BlockSpec((tk, tn), lambda i, j, kk: (kk, j))
    return pl.pallas_call(
        body, name=name, grid=(m // tm, n // tn, nk),
        in_specs=[a_spec, b_spec] + ([] if after is None else [pl.BlockSpec(memory_space=pl.ANY)]),
        out_specs=pl.BlockSpec((tm, tn), lambda i, j, kk: (i, j)),
        out_shape=jax.ShapeDtypeStruct((m, n), out_dtype),
        scratch_shapes=[pltpu.VMEM((tm, tn), F32)] if nk > 1 else [],
        compiler_params=_params(("parallel", "parallel", "arbitrary")),
    )(*((a, b) if after is None else (a, b, after)))


def _mm_swiglu(name, a, w):
    m, k = a.shape
    f = w.shape[1] // 2
    tm = _tile(m, 1024)
    tn = _tile(f, 512)
    tk = _tile(k, 2560)
    nk = k // tk
    jf = f // tn

    def body(a_ref, wa_ref, wb_ref, h_ref, ua_ref, ub_ref, acca, accb):
        kk = pl.program_id(2)

        @pl.when(kk == 0)
        def _():
            acca[...] = jnp.zeros_like(acca)
            accb[...] = jnp.zeros_like(accb)

        av = a_ref[...]
        acca[...] += jnp.dot(av, wa_ref[...], preferred_element_type=F32)
        accb[...] += jnp.dot(av, wb_ref[...], preferred_element_type=F32)

        @pl.when(kk == nk - 1)
        def _():
            ua = acca[...]
            ub = accb[...]
            h_ref[...] = (_silu(ua) * ub).astype(BF16)
            ua_ref[...] = ua.astype(BF16)
            ub_ref[...] = ub.astype(BF16)

    o_spec = pl.BlockSpec((tm, tn), lambda i, j, kk: (i, j))
    o_shape = jax.ShapeDtypeStruct((m, f), BF16)
    return pl.pallas_call(
        body, name=name, grid=(m // tm, jf, nk),
        in_specs=[pl.BlockSpec((tm, tk), lambda i, j, kk: (i, kk)),
                  pl.BlockSpec((tk, tn), lambda i, j, kk: (kk, j)),
                  pl.BlockSpec((tk, tn), lambda i, j, kk: (kk, j + jf))],
        out_specs=[o_spec, o_spec, o_spec],
        out_shape=[o_shape, o_shape, o_shape],
        scratch_shapes=[pltpu.VMEM((tm, tn), F32), pltpu.VMEM((tm, tn), F32)],
        compiler_params=_params(("parallel", "parallel", "arbitrary")),
    )(a, w, w)


def _rowwise(name, fn, *, n_tiles, tr, row_ins, row_outs, sel_in=None, sel_off=0, ctx_rows=0,
             full_ins=(), acc_shape=None):
    sr = 128 if tr % 128 == 0 else (32 if tr % 32 == 0 else tr)
    n_row, n_full, n_out = len(row_ins), len(full_ins), len(row_outs)
    has_sel = sel_in is not None
    has_acc = acc_shape is not None

    def sel_of(i):
        return jnp.where((i + sel_off) * tr < ctx_rows, 0, 1)

    def body(*refs):
        row_refs = refs[:n_row]
        pos = n_row
        sel_ref = None
        if has_sel:
            sel_ref = refs[pos]
            pos += 1
        full_refs = refs[pos:pos + n_full]
        pos += n_full
        out_refs = refs[pos:pos + n_out]
        pos += n_out
        acc_ref = refs[pos] if has_acc else None
        i = pl.program_id(0)
        if has_acc:
            first = (i == 0) | ((i + sel_off) * tr == ctx_rows)

            @pl.when(first)
            def _():
                acc_ref[...] = jnp.zeros_like(acc_ref)

        sel = (lambda kk: sel_ref[kk:kk + 1, :]) if has_sel else None
        fulls = [r[...] for r in full_refs] + [(i + sel_off) * tr < ctx_rows]

        def slab(r, carry):
            rs = pl.ds(pl.multiple_of(r * sr, sr), sr)
            rows = [ref[rs, :].astype(F32) for ref in row_refs]
            outs, accs = fn(rows, sel, fulls)
            for o_ref, o in zip(out_refs, outs):
                o_ref[rs, :] = o.astype(o_ref.dtype)
            for kk, a in enumerate(accs):
                acc_ref[kk:kk + 1, :a.shape[1]] += a
            return carry

        lax.fori_loop(0, tr // sr, slab, 0)

    def row_map(off, col=0):
        if off == "ctx":
            return lambda i: (jnp.minimum(i, ctx_rows // tr - 1), col)
        if off < 0:
            return lambda i: (jnp.maximum(i + off, 0), col)
        return lambda i: (i + off, col)

    in_specs, args = [], []
    for arr, off, blk in row_ins:
        if blk is None:
            in_specs.append(pl.BlockSpec((tr, arr.shape[1]), row_map(off)))
        else:
            in_specs.append(pl.BlockSpec((tr, blk[0]), row_map(off, blk[1])))
        args.append(arr)
    if has_sel:
        in_specs.append(pl.BlockSpec((None,) + sel_in.shape[1:], lambda i: (sel_of(i), 0, 0)))
        args.append(sel_in)
    for arr in full_ins:
        in_specs.append(pl.BlockSpec(arr.shape, lambda i: (0, 0)))
        args.append(arr)
    out_specs, out_shape = [], []
    for rows, cols, dt, off in row_outs:
        out_specs.append(pl.BlockSpec((tr, cols), row_map(off)))
        out_shape.append(jax.ShapeDtypeStruct((rows, cols), dt))
    if has_acc:
        out_specs.append(pl.BlockSpec((None,) + tuple(acc_shape), lambda i: (sel_of(i), 0, 0)))
        out_shape.append(jax.ShapeDtypeStruct((2,) + tuple(acc_shape), F32))
    return pl.pallas_call(
        body, name=name, grid=(n_tiles,), in_specs=in_specs, out_specs=out_specs, out_shape=out_shape,
        compiler_params=_params(("arbitrary",)),
    )(*args)


def _swap_pairs(x):
    lane = lax.broadcasted_iota(jnp.int32, x.shape, 1)
    nxt = pltpu.roll(x, x.shape[1] - 1, 1)
    prv = pltpu.roll(x, 1, 1)
    return jnp.where(lane % 2 == 0, nxt, prv)


def _heads_map(fn, arrs, width):
    outs = None
    for h in range(width // HEAD_DIM):
        sl = slice(h * HEAD_DIM, (h + 1) * HEAD_DIM)
        res = fn(*[a[:, sl] for a in arrs])
        if outs is None:
            outs = [[] for _ in res]
        for lst, r in zip(outs, res):
            lst.append(r)
    return [jnp.concatenate(lst, axis=1) if len(lst) > 1 else lst[0] for lst in outs]


QSCALE = HEAD_DIM ** -0.5 * math.log2(math.e)
LN2 = math.log(2.0)


def _lane_chunks(a):
    return [a[:, cc * LANES_V7X:(cc + 1) * LANES_V7X] for cc in range(a.shape[1] // LANES_V7X)]


def _row_bcast(col, like):
    return jnp.broadcast_to(col, like.shape)


def _flash_tiles(t, tk_all, key_pref):
    return _tile(t, 1024), _tile(tk_all, key_pref)


def _flash_fwd(q, k, vx, groups):
    t, aw = q.shape
    tk_all, kvw = k.shape
    kvh = kvw // HEAD_DIM
    gw = groups * HEAD_DIM
    tq, tk = _flash_tiles(t, tk_all, 1536)
    nk = tk_all // tk

    def body(q_ref, k_ref, v_ref, o_ref, lse_ref, m_sc, l_sc, acc_sc):
        j = pl.program_id(2)

        @pl.when(j == 0)
        def _():
            m_sc[...] = jnp.full_like(m_sc, -jnp.inf)
            l_sc[...] = jnp.zeros_like(l_sc)
            acc_sc[...] = jnp.zeros_like(acc_sc)

        kt = k_ref[...]
        vt = v_ref[...]
        for g in range(groups):
            sl = slice(g * HEAD_DIM, (g + 1) * HEAD_DIM)
            s = _lane_chunks(lax.dot_general(q_ref[:, sl], kt, NT_DIMS, preferred_element_type=F32))
            mx = functools.reduce(jnp.maximum, s)
            m_prev = m_sc[g]
            m_new = jnp.maximum(m_prev, _row_bcast(jnp.max(mx, axis=1, keepdims=True), mx))
            p = jnp.concatenate([jnp.exp2(sc - m_new).astype(BF16) for sc in s], axis=1)
            alpha = jnp.exp2(m_prev - m_new)
            pv = jnp.dot(p, vt, preferred_element_type=F32)
            acc_sc[g] = alpha * acc_sc[g] + pv[:, :HEAD_DIM]
            l_sc[g] = alpha * l_sc[g] + pv[:, HEAD_DIM:]
            m_sc[g] = m_new

        @pl.when(j == nk - 1)
        def _():
            for g in range(groups):
                sl = slice(g * HEAD_DIM, (g + 1) * HEAD_DIM)
                o_ref[:, sl] = (acc_sc[g] / l_sc[g]).astype(o_ref.dtype)
                lse_ref[:, sl] = m_sc[g] + jnp.log2(l_sc[g])

    qs = pl.BlockSpec((tq, gw), lambda kh, i, j: (i, kh))
    sc = pltpu.VMEM((groups, tq, HEAD_DIM), F32)
    return pl.pallas_call(
        body, name="flash_fwd", grid=(kvh, t // tq, nk),
        in_specs=[qs, pl.BlockSpec((tk, HEAD_DIM), lambda kh, i, j: (j, kh)),
                  pl.BlockSpec((tk, 2 * HEAD_DIM), lambda kh, i, j: (j, kh))],
        out_specs=[qs, qs],
        out_shape=[jax.ShapeDtypeStruct((t, aw), BF16), jax.ShapeDtypeStruct((t, aw), F32)],
        scratch_shapes=[sc, sc, sc],
        compiler_params=_params(("parallel", "parallel", "arbitrary")),
    )(q, k, vx)


def _flash_p_ds(q, kt, vt, do, lse, delta):
    s = _lane_chunks(lax.dot_general(q, kt, NT_DIMS, preferred_element_type=F32))
    dp = _lane_chunks(lax.dot_general(do, vt, NT_DIMS, preferred_element_type=F32))
    p = [jnp.exp2(sc - lse) for sc in s]
    ds = jnp.concatenate([(pc * (dc - delta)).astype(BF16) for pc, dc in zip(p, dp)], axis=1)
    return jnp.concatenate([pc.astype(BF16) for pc in p], axis=1), ds


def _flash_delta(do, o):
    prod = do.astype(F32) * o.astype(F32)
    return _row_bcast(jnp.sum(prod, axis=1, keepdims=True), prod)


def _flash_bwd(q, k, vx, o, do, lse, groups):
    t, aw = q.shape
    tk_all, kvw = k.shape
    kvh = kvw // HEAD_DIM
    gw = groups * HEAD_DIM
    tq, tk = _flash_tiles(t, tk_all, 1024)
    nq, nk = t // tq, tk_all // tk

    def body(q_ref, k_ref, v_ref, o_ref, do_ref, lse_ref, dq_ref, dk_ref, dv_ref, dq_sc, dk_acc, dv_acc):
        j = pl.program_id(1)
        i = pl.program_id(2)

        @pl.when(i == 0)
        def _():
            dk_acc[...] = jnp.zeros_like(dk_acc)
            dv_acc[...] = jnp.zeros_like(dv_acc)

        @pl.when(j == 0)
        def _():
            dq_sc[i] = jnp.zeros((groups, tq, HEAD_DIM), F32)

        kt = k_ref[...]
        vt = v_ref[:, :HEAD_DIM]
        for g in range(groups):
            sl = slice(g * HEAD_DIM, (g + 1) * HEAD_DIM)
            qv = q_ref[:, sl]
            dov = do_ref[:, sl]
            p, ds = _flash_p_ds(qv, kt, vt, dov, lse_ref[:, sl], _flash_delta(dov, o_ref[:, sl]))
            dv_acc[...] += lax.dot_general(p, dov, TN_DIMS, preferred_element_type=F32)
            dk_acc[...] += lax.dot_general(ds, qv, TN_DIMS, preferred_element_type=F32)
            dq_sc[i, g] += jnp.dot(ds, kt, preferred_element_type=F32)

        @pl.when(i == nq - 1)
        def _():
            dk_ref[...] = dk_acc[...] * LN2
            dv_ref[...] = dv_acc[...]

        @pl.when(j == nk - 1)
        def _():
            for g in range(groups):
                dq_ref[:, g * HEAD_DIM:(g + 1) * HEAD_DIM] = dq_sc[i, g]

    qs = pl.BlockSpec((tq, gw), lambda kh, j, i: (i, kh))
    ks = pl.BlockSpec((tk, HEAD_DIM), lambda kh, j, i: (j, kh))
    dq_spec = pl.BlockSpec((tq, gw), lambda kh, j, i: (jnp.where(j == nk - 1, i, 0), kh))
    return pl.pallas_call(
        body, name="flash_bwd", grid=(kvh, nk, nq),
        in_specs=[qs, ks, pl.BlockSpec((tk, 2 * HEAD_DIM), lambda kh, j, i: (j, kh)), qs, qs, qs],
        out_specs=[dq_spec, ks, ks],
        out_shape=[jax.ShapeDtypeStruct((t, aw), F32), jax.ShapeDtypeStruct((tk_all, kvw), F32),
                   jax.ShapeDtypeStruct((tk_all, kvw), F32)],
        scratch_shapes=[pltpu.VMEM((nq, groups, tq, HEAD_DIM), F32), pltpu.VMEM((tk, HEAD_DIM), F32),
                        pltpu.VMEM((tk, HEAD_DIM), F32)],
        compiler_params=_params(("parallel", "arbitrary", "arbitrary")),
    )(q, k, vx, o, do, lse)


def _bf_nn(a, b):
    return jnp.dot(a.astype(BF16), b.astype(BF16), preferred_element_type=F32)


def _bf_nt(a, b):
    return lax.dot_general(a.astype(BF16), b.astype(BF16), NT_DIMS, preferred_element_type=F32)


def _bf_tn(a, b):
    return lax.dot_general(a.astype(BF16), b.astype(BF16), TN_DIMS, preferred_element_type=F32)


@jax.custom_vjp
def _d_nn(a, b):
    return _bf_nn(a, b)


@jax.custom_vjp
def _d_nt(a, b):
    return _bf_nt(a, b)


@jax.custom_vjp
def _d_tn(a, b):
    return _bf_tn(a, b)


_d_nn.defvjp(lambda a, b: (_bf_nn(a, b), (a, b)), lambda r, g: (_d_nt(g, r[1]), _d_tn(r[0], g)))
_d_nt.defvjp(lambda a, b: (_bf_nt(a, b), (a, b)), lambda r, g: (_d_nn(g, r[1]), _d_tn(g, r[0])))
_d_tn.defvjp(lambda a, b: (_bf_tn(a, b), (a, b)), lambda r, g: (_d_nt(r[1], g), _d_nn(r[0], g)))


def _ret_chunk(q, k_raw, v, state, lg, rev, dots):
    nn, nt, tn = dots
    c = RET_CHUNK
    tcol = lax.broadcasted_iota(jnp.int32, (c, 1), 0).astype(F32)
    trow = lax.broadcasted_iota(jnp.int32, (1, c), 1).astype(F32)
    ucol = jnp.where(rev, c - 1.0 - tcol, tcol)
    urow = jnp.where(rev, c - 1.0 - trow, trow)
    e = ucol - urow
    low = e >= 0
    intra = jnp.where(low, jnp.exp(jnp.where(low, e, 0.0) * lg), 0.0)
    k = k_raw * (HEAD_DIM ** -0.5)
    inner = nt(q, k) * intra
    y = nn(inner, v) + nn(q, state) * jnp.exp((ucol + 1.0) * lg)
    new_state = state * jnp.exp(c * lg) + tn(k * jnp.exp((c - 1.0 - ucol) * lg), v)
    return y, new_state


def _ret_chunk_index(n_chunks, n_ctx_chunks):
    def idx(d, s):
        if d == 0:
            return s
        return jnp.where(s < n_ctx_chunks, n_ctx_chunks - 1 - s, n_chunks - 1 - s + n_ctx_chunks)
    return idx


def _ret_fwd(pr, lgv, ctx_rows):
    tk_all = pr.shape[0]
    rw = pr.shape[1] // 3
    nh = rw // HEAD_DIM
    nc = tk_all // RET_CHUNK
    cidx = _ret_chunk_index(nc, ctx_rows // RET_CHUNK)

    def body(pf_ref, pb_ref, lg_ref, yf_ref, yb_ref, st_ref, s_sc):
        s = pl.program_id(0)

        @pl.when(s == 0)
        def _():
            s_sc[...] = jnp.zeros_like(s_sc)

        for d, (p_ref, y_ref) in enumerate(((pf_ref, yf_ref), (pb_ref, yb_ref))):
            for h in range(nh):
                cols = [slice((part * nh + h) * HEAD_DIM, (part * nh + h + 1) * HEAD_DIM) for part in range(3)]
                state = s_sc[d, h]
                st_ref[d, h] = state
                y, new_state = _ret_chunk(p_ref[:, cols[0]], p_ref[:, cols[1]], p_ref[:, cols[2]], state,
                                          lg_ref[d, h][:, :1], d == 1, (_bf_nn, _bf_nt, _bf_tn))
                y_ref[:, h * HEAD_DIM:(h + 1) * HEAD_DIM] = y
                s_sc[d, h] = new_state

    y_shape = jax.ShapeDtypeStruct((tk_all, rw), F32)
    return pl.pallas_call(
        body, name="ret_fwd", grid=(nc,),
        in_specs=[pl.BlockSpec((RET_CHUNK, 3 * rw), lambda s: (cidx(0, s), 0)),
                  pl.BlockSpec((RET_CHUNK, 3 * rw), lambda s: (cidx(1, s), 0)),
                  pl.BlockSpec(lgv.shape, lambda s: (0, 0, 0, 0))],
        out_specs=[pl.BlockSpec((RET_CHUNK, rw), lambda s: (cidx(0, s), 0)),
                   pl.BlockSpec((RET_CHUNK, rw), lambda s: (cidx(1, s), 0)),
                   pl.BlockSpec((2, nh, None, HEAD_DIM, HEAD_DIM), lambda s: (0, 0, s, 0, 0))],
        out_shape=[y_shape, y_shape, jax.ShapeDtypeStruct((2, nh, nc, HEAD_DIM, HEAD_DIM), F32)],
        scratch_shapes=[pltpu.VMEM((2, nh, HEAD_DIM, HEAD_DIM), F32)],
        compiler_params=_params(("arbitrary",)),
    )(pr, pr, lgv)


def _ret_bwd(pr, states, dy, lgv, ctx_rows):
    tk_all = pr.shape[0]
    rw = pr.shape[1] // 3
    nh = rw // HEAD_DIM
    nc = tk_all // RET_CHUNK
    n_ctx = ctx_rows // RET_CHUNK
    cidx = _ret_chunk_index(nc, n_ctx)

    def body(pf_ref, pb_ref, st_ref, dyf_ref, dyb_ref, lg_ref, dpf_ref, dpb_ref, dlg_ref, ds_sc):
        sp = pl.program_id(0)
        on_ctx = [cidx(dd, nc - 1 - sp) < n_ctx for dd in (0, 1)]

        @pl.when(sp == 0)
        def _():
            ds_sc[...] = jnp.zeros_like(ds_sc)
            dlg_ref[...] = jnp.zeros_like(dlg_ref)

        for d, (p_ref, dy_ref, dp_ref) in enumerate(((pf_ref, dyf_ref, dpf_ref), (pb_ref, dyb_ref, dpb_ref))):
            for h in range(nh):
                cols = [slice((part * nh + h) * HEAD_DIM, (part * nh + h + 1) * HEAD_DIM) for part in range(3)]

                def step(q, k, v, state, lg, rev=(d == 1)):
                    return _ret_chunk(q, k, v, state, lg, rev, (_d_nn, _d_nt, _d_tn))

                _, vjp = jax.vjp(step, p_ref[:, cols[0]], p_ref[:, cols[1]], p_ref[:, cols[2]], st_ref[d, h],
                                 lg_ref[d, h][:, :1])
                dy_h = jnp.where(on_ctx[d], 0.0, dy_ref[:, h * HEAD_DIM:(h + 1) * HEAD_DIM])
                grads = vjp((dy_h, ds_sc[d, h]))
                for part in range(3):
                    dp_ref[:, cols[part]] = grads[part]
                ds_sc[d, h] = grads[3]
                dlg_ref[d, h] += jnp.broadcast_to(grads[4], (1, HEAD_DIM))

    def at(d):
        return lambda sp: (cidx(d, nc - 1 - sp), 0)

    def dy_at(d):
        return lambda sp: (jnp.maximum(cidx(d, nc - 1 - sp) - n_ctx, 0), 0)

    dp_shape = jax.ShapeDtypeStruct((tk_all, 3 * rw), F32)
    lg_spec = pl.BlockSpec(lgv.shape, lambda sp: (0, 0, 0, 0))
    return pl.pallas_call(
        body, name="ret_bwd", grid=(nc,),
        in_specs=[pl.BlockSpec((RET_CHUNK, 3 * rw), at(0)), pl.BlockSpec((RET_CHUNK, 3 * rw), at(1)),
                  pl.BlockSpec((2, nh, None, HEAD_DIM, HEAD_DIM), lambda sp: (0, 0, nc - 1 - sp, 0, 0)),
                  pl.BlockSpec((RET_CHUNK, rw), dy_at(0)), pl.BlockSpec((RET_CHUNK, rw), dy_at(1)), lg_spec],
        out_specs=[pl.BlockSpec((RET_CHUNK, 3 * rw), at(0)), pl.BlockSpec((RET_CHUNK, 3 * rw), at(1)), lg_spec],
        out_shape=[dp_shape, dp_shape, jax.ShapeDtypeStruct(lgv.shape, F32)],
        scratch_shapes=[pltpu.VMEM((2, nh, HEAD_DIM, HEAD_DIM), F32)],
        compiler_params=_params(("arbitrary",)),
    )(pr, pr, states, dy, dy, lgv)


FLIP_X, FLIP_Y, FLIP_XY, FLIP_C = (1, 0, 0), (0, 1, 0), (1, 1, 0), (0, 0, 1)
CHIP_FLIPS = ((FLIP_X, 2), (FLIP_Y, 1), (FLIP_XY, 3))


def _flip(me, mask):
    return tuple(1 - v if m else v for v, m in zip(me, mask))


def _comm(name, ins, out_shapes, plan, n_remote, n_local, aliases=None):
    n_in, n_out = len(ins), len(out_shapes)

    def body(*refs):
        in_refs = refs[:n_in]
        out_refs = refs[n_in:n_in + n_out]
        send_sems, recv_sems, local_sems = refs[n_in + n_out:]
        me = (lax.axis_index("x"), lax.axis_index("y"), lax.axis_index("c"))
        local, phases = plan(in_refs, out_refs, me)
        local_copies = [pltpu.make_async_copy(s, d, local_sems.at[i]) for i, (s, d) in enumerate(local)]
        for cp in local_copies:
            cp.start()
        sent = []
        kk = 0
        for phase in phases:
            arrivals = []
            for mask, src, dst, landing in phase:
                peer = _flip(me, mask)
                cp = pltpu.make_async_remote_copy(src_ref=src, dst_ref=dst, send_sem=send_sems.at[kk],
                                                  recv_sem=recv_sems.at[kk], device_id=peer,
                                                  device_id_type=pl.DeviceIdType.MESH)
                cp.start()
                sent.append(cp)
                arrivals.append(pltpu.make_async_remote_copy(
                    src_ref=landing, dst_ref=landing, send_sem=send_sems.at[kk], recv_sem=recv_sems.at[kk],
                    device_id=peer, device_id_type=pl.DeviceIdType.MESH))
                kk += 1
            for cp in arrivals:
                cp.wait_recv()
        for cp in sent:
            cp.wait_send()
        for cp in local_copies:
            cp.wait()

    any_spec = pl.BlockSpec(memory_space=pl.ANY)
    return pl.pallas_call(
        body, name=name,
        in_specs=[any_spec] * n_in, out_specs=[any_spec] * n_out, out_shape=list(out_shapes),
        scratch_shapes=[pltpu.SemaphoreType.DMA((n_remote,)), pltpu.SemaphoreType.DMA((n_remote,)),
                        pltpu.SemaphoreType.DMA((max(n_local, 1),))],
        input_output_aliases=aliases or {},
    )(*ins)


def _ds(start, size):
    return pl.ds(pl.multiple_of(start * size, 8), size)


def _all_gather8(name, v):
    masks = [(a, b, cc) for a in (0, 1) for b in (0, 1) for cc in (0, 1)][1:]

    def index(p):
        return 4 * p[0] + 2 * p[1] + p[2]

    def plan(in_refs, out_refs, me):
        (src,), (out,) = in_refs, out_refs
        local = [(src, out.at[index(me)])]
        phase = [(m, src, out.at[index(me)], out.at[index(_flip(me, m))]) for m in masks]
        return local, [phase]

    return _comm(name, [v], [jax.ShapeDtypeStruct((N_DEV,) + v.shape, v.dtype)], plan, len(masks), 1)[0]


def _gather_row(name, row):
    n = row.shape[1]
    n_pad = -(-n // (8 * LANES_V7X)) * (8 * LANES_V7X)
    v = jnp.pad(row, ((0, 0), (0, n_pad - n))).reshape(8, n_pad // 8)
    return _all_gather8(name, v).reshape(N_DEV, n_pad)[:, :n]


class _Sharded:
    def __init__(self, kind, rows, cols):
        self.kind, self.rows, self.cols = kind, rows, cols
        self.shard_shape = (rows, cols // N_CHIP) if kind == "col" else (rows // N_CHIP, cols)
        self.half_shape = (rows // 2, cols) if kind == "col" else (rows, cols // 2)
        self.piece_shape = (rows // 2, cols // N_CHIP) if kind == "col" else (rows // N_CHIP, cols // 2)

    def shard_of_full(self, ref, s):
        if self.kind == "col":
            return ref.at[:, _ds(s, self.cols // N_CHIP)]
        return ref.at[_ds(s, self.rows // N_CHIP), :]

    def half_of_full(self, ref, h):
        if self.kind == "col":
            return ref.at[_ds(h, self.rows // 2), :]
        return ref.at[:, _ds(h, self.cols // 2)]

    def piece_of_full(self, ref, s, h):
        if self.kind == "col":
            return ref.at[_ds(h, self.rows // 2), _ds(s, self.cols // N_CHIP)]
        return ref.at[_ds(s, self.rows // N_CHIP), _ds(h, self.cols // 2)]

    def half_of_shard(self, ref, h):
        if self.kind == "col":
            return ref.at[_ds(h, self.rows // 2), :]
        return ref.at[:, _ds(h, self.cols // 2)]

    def shard_of_half(self, ref, s):
        if self.kind == "col":
            return ref.at[:, _ds(s, self.cols // N_CHIP)]
        return ref.at[_ds(s, self.rows // N_CHIP), :]


def _place_shard(meta, w, s_arr, after=None):
    r, cols = w.shape
    tr = _tile(r, 256, 16)
    nr = r // tr

    def body(s_ref, w_ref, *rest):
        rest[-1][...] = w_ref[...].astype(BF16)

    if meta.kind == "col":
        o_map = lambda i, s_ref: (i, s_ref[0])
    else:
        o_map = lambda i, s_ref: (i + s_ref[0] * nr, 0)
    return pl.pallas_call(
        body, name="place_shard",
        grid_spec=pltpu.PrefetchScalarGridSpec(
            num_scalar_prefetch=1, grid=(nr,),
            in_specs=[pl.BlockSpec((tr, cols), lambda i, s_ref: (i, 0))]
            + ([] if after is None else [pl.BlockSpec(memory_space=pl.ANY)]),
            out_specs=pl.BlockSpec((tr, cols), o_map)),
        out_shape=jax.ShapeDtypeStruct((meta.rows, meta.cols), BF16),
        compiler_params=_params(("parallel",)),
    )(*((s_arr, w) if after is None else (s_arr, w, after)))


def _gather_copies(metas, over_ici):
    def copies(fulls, me):
        x, y, c = me
        s_me = 2 * x + y
        out = []
        for meta, full in zip(metas, fulls):
            for mask, bits in CHIP_FLIPS:
                s_peer = jnp.bitwise_xor(s_me, bits)
                if over_ici:
                    out.append((mask, meta.piece_of_full(full, s_me, c), meta.piece_of_full(full, s_me, c),
                                meta.piece_of_full(full, s_peer, c)))
                else:
                    out.append((FLIP_C, meta.piece_of_full(full, s_peer, c), meta.piece_of_full(full, s_peer, c),
                                meta.piece_of_full(full, s_peer, 1 - c)))
        return out
    return copies


def _gather_forward(name, metas, fulls):
    nt = len(metas)
    copies = _gather_copies(metas, False)
    outs = [jax.ShapeDtypeStruct((m.rows, m.cols), BF16) for m in metas]
    return _comm(name, list(fulls), outs, lambda ins, outs_, me: ([], [copies(outs_, me)]), 3 * nt, 0,
                 aliases={i: i for i in range(nt)})


HBM_SPEC = pl.BlockSpec(memory_space=pltpu.HBM)
SEM_SPEC = pl.BlockSpec(memory_space=pltpu.SEMAPHORE)
SPLIT_EFFECT = pltpu.SideEffectType.DATAFLOW_SIDE_EFFECTING


def _split_start(name, bufs, groups, after):
    nb, ng = len(bufs), len(groups)
    n_in = nb + (0 if after is None else 1)

    def body(*refs):
        buf_refs = refs[:nb]
        sem_refs = refs[n_in:n_in + 2 * ng]
        token = refs[-1]
        me = (lax.axis_index("x"), lax.axis_index("y"), lax.axis_index("c"))
        for gi, (lo, n_bufs, copies, _) in enumerate(groups):
            for kk, (mask, src, dst, _) in enumerate(copies(buf_refs[lo:lo + n_bufs], me)):
                pltpu.make_async_remote_copy(src_ref=src, dst_ref=dst, send_sem=sem_refs[2 * gi].at[kk],
                                             recv_sem=sem_refs[2 * gi + 1].at[kk], device_id=_flip(me, mask),
                                             device_id_type=pl.DeviceIdType.MESH).start()
        token[...] = jnp.zeros_like(token)

    out_shape = []
    for _, _, _, n in groups:
        out_shape += [pltpu.SemaphoreType.DMA((n,)), pltpu.SemaphoreType.DMA((n,))]
    out_shape += [pltpu.HBM(b.shape, b.dtype) for b in bufs] + [jax.ShapeDtypeStruct((8, LANES_V7X), F32)]
    res = pl.pallas_call(
        body, name=name, out_shape=tuple(out_shape),
        in_specs=(HBM_SPEC,) * nb + (pl.BlockSpec(memory_space=pl.ANY),) * (n_in - nb),
        out_specs=(SEM_SPEC,) * (2 * ng) + (HBM_SPEC,) * nb + (pl.BlockSpec(memory_space=pltpu.VMEM),),
        input_output_aliases={i: 2 * ng + i for i in range(nb)},
        compiler_params=pltpu.CompilerParams(has_side_effects=SPLIT_EFFECT),
    )(*[pltpu.with_memory_space_constraint(b, pltpu.HBM) for b in bufs], *([] if after is None else [after]))
    sems = [(res[2 * gi], res[2 * gi + 1]) for gi in range(ng)]
    return sems, list(res[2 * ng:2 * ng + nb]), res[-1]


def _split_wait(name, sems, bufs, copies, after):
    nb = len(bufs)

    def body(*refs):
        buf_refs = refs[:nb]
        send_sems, recv_sems = refs[nb], refs[nb + 1]
        me = (lax.axis_index("x"), lax.axis_index("y"), lax.axis_index("c"))
        for kk, (mask, _, _, landing) in enumerate(copies(buf_refs, me)):
            cp = pltpu.make_async_remote_copy(src_ref=landing, dst_ref=landing, send_sem=send_sems.at[kk],
                                              recv_sem=recv_sems.at[kk], device_id=_flip(me, mask),
                                              device_id_type=pl.DeviceIdType.MESH)
            cp.wait_send()
            cp.wait_recv()

    return list(pl.pallas_call(
        body, name=name, out_shape=tuple(pltpu.HBM(b.shape, b.dtype) for b in bufs),
        in_specs=(HBM_SPEC,) * nb + (SEM_SPEC, SEM_SPEC, pl.BlockSpec(memory_space=pl.ANY)),
        out_specs=(HBM_SPEC,) * nb,
        input_output_aliases={i: i for i in range(nb)},
        compiler_params=pltpu.CompilerParams(has_side_effects=SPLIT_EFFECT),
    )(*bufs, sems[0], sems[1], after))


N_REDUCE_PIECES = 7


def _reduce_copies(metas):
    def copies(refs, me):
        x, y, c = me
        s_me = 2 * x + y
        out = []
        for m, g, land in zip(metas, refs[:len(metas)], refs[len(metas):]):
            for kk, (mask, bits) in enumerate(CHIP_FLIPS):
                s_peer = jnp.bitwise_xor(s_me, bits)
                out.append((mask, m.piece_of_full(g, s_peer, c), land.at[kk], land.at[kk]))
                out.append((mask[:2] + (1,), m.piece_of_full(g, s_peer, 1 - c), land.at[3 + kk], land.at[3 + kk]))
            out.append((FLIP_C, m.piece_of_full(g, s_me, 1 - c), land.at[6], land.at[6]))
        return out
    return copies


def _reduce_start(name, metas, grads):
    lands = [lax.empty((N_REDUCE_PIECES,) + m.piece_shape, BF16) for m in metas]
    bufs = list(grads) + lands
    sems, thru, token = _split_start(name, bufs, [(0, len(bufs), _reduce_copies(metas),
                                                   N_REDUCE_PIECES * len(metas))], None)
    return sems[0], thru, token


def _reduce_wait(name, metas, sems, thru, after):
    done = _split_wait(name, sems, thru, _reduce_copies(metas), after)
    return done[:len(metas)], done[len(metas):]


def _share_halves(metas, shards):
    def plan(in_refs, out_refs, me):
        c = me[2]
        phase = [(FLIP_C, m.half_of_shard(g, c), m.half_of_shard(g, c), m.half_of_shard(g, 1 - c))
                 for m, g in zip(metas, out_refs)]
        return [], [phase]

    outs = [jax.ShapeDtypeStruct(m.shard_shape, F32) for m in metas]
    return _comm("share_halves", list(shards), outs, plan, len(metas), 0,
                 aliases={i: i for i in range(len(metas))})


def _sum_pieces(meta, grad, landed, s_arr, c_arr):
    pr, pc = meta.piece_shape
    tr = _tile(pr, 256, 16)
    tc = _tile(pc, 2048)
    nr, ncol = pr // tr, pc // tc

    def body(s_ref, c_ref, p_ref, l_ref, o_ref):
        acc = p_ref[...].astype(F32)
        for kk in range(N_REDUCE_PIECES):
            acc = acc + l_ref[kk].astype(F32)
        o_ref[...] = acc

    if meta.kind == "col":
        p_map = lambda i, j, s_ref, c_ref: (i + c_ref[0] * nr, j + s_ref[0] * ncol)
        o_map = lambda i, j, s_ref, c_ref: (i + c_ref[0] * nr, j)
    else:
        p_map = lambda i, j, s_ref, c_ref: (i + s_ref[0] * nr, j + c_ref[0] * ncol)
        o_map = lambda i, j, s_ref, c_ref: (i, j + c_ref[0] * ncol)
    blk = (tr, tc)
    return pl.pallas_call(
        body, name="sum_pieces",
        grid_spec=pltpu.PrefetchScalarGridSpec(
            num_scalar_prefetch=2, grid=(nr, ncol),
            in_specs=[pl.BlockSpec(blk, p_map),
                      pl.BlockSpec((N_REDUCE_PIECES,) + blk, lambda i, j, s_ref, c_ref: (0, i, j))],
            out_specs=pl.BlockSpec(blk, o_map)),
        out_shape=jax.ShapeDtypeStruct(meta.shard_shape, F32),
        compiler_params=_params(("parallel", "parallel")),
    )(s_arr, c_arr, grad, landed)


def _adam_rows(rows, sel, fulls):
    w, g, m, v = rows
    m2 = ADAM_B1 * m + (1.0 - ADAM_B1) * g
    v2 = ADAM_B2 * v + (1.0 - ADAM_B2) * jnp.square(g)
    m_hat = m2 / (1.0 - ADAM_B1 ** ADAM_STEP)
    v_hat = v2 / (1.0 - ADAM_B2 ** ADAM_STEP)
    delta = -ADAM_LR * (m_hat / (jnp.sqrt(v_hat) + ADAM_EPS) + ADAM_WD * w)
    return [delta, m2, v2], []


def _adamw(w, g, m, v):
    r, c = w.shape
    tr = _tile(r, 128, 8)
    outs = _rowwise("adamw", _adam_rows, n_tiles=r // tr, tr=tr,
                    row_ins=[(w, 0, None), (g, 0, None), (m, 0, None), (v, 0, None)],
                    row_outs=[(r, c, F32, 0)] * 3)
    return outs[0], outs[1], outs[2]


def _ada_fwd(cg, w, b):
    d, n = w.shape
    tn = _tile(n, 512)

    def body(c_ref, w_ref, b_ref, o_ref):
        a = _silu(c_ref[...]).astype(BF16)
        o_ref[...] = jnp.dot(a, w_ref[...].astype(BF16), preferred_element_type=F32) + b_ref[...]

    return pl.pallas_call(
        body, name="ada_fwd", grid=(n // tn,),
        in_specs=[pl.BlockSpec(cg.shape, lambda j: (0, 0)), pl.BlockSpec((d, tn), lambda j: (0, j)),
                  pl.BlockSpec((1, tn), lambda j: (0, j))],
        out_specs=pl.BlockSpec((cg.shape[0], tn), lambda j: (0, j)),
        out_shape=jax.ShapeDtypeStruct((cg.shape[0], n), F32),
        compiler_params=_params(("parallel",)),
    )(cg, w, b)


def _ada_bwd(cg, dm, w):
    d, n = w.shape
    tn = _tile(n, 512)
    nj = n // tn

    def body(c_ref, dm_ref, w_ref, gw_ref, da_ref, acc):
        j = pl.program_id(0)

        @pl.when(j == 0)
        def _():
            acc[...] = jnp.zeros_like(acc)

        a = _silu(c_ref[...]).astype(BF16)
        dmv = dm_ref[...].astype(BF16)
        gw_ref[...] = lax.dot_general(a, dmv, TN_DIMS, preferred_element_type=F32)
        acc[...] += lax.dot_general(dmv, w_ref[...].astype(BF16), NT_DIMS, preferred_element_type=F32)

        @pl.when(j == nj - 1)
        def _():
            da_ref[...] = acc[...]

    return pl.pallas_call(
        body, name="ada_bwd", grid=(nj,),
        in_specs=[pl.BlockSpec(cg.shape, lambda j: (0, 0)), pl.BlockSpec((dm.shape[0], tn), lambda j: (0, j)),
                  pl.BlockSpec((d, tn), lambda j: (0, j))],
        out_specs=[pl.BlockSpec((d, tn), lambda j: (0, j)), pl.BlockSpec(cg.shape, lambda j: (0, 0))],
        out_shape=[jax.ShapeDtypeStruct((d, n), F32), jax.ShapeDtypeStruct(cg.shape, F32)],
        scratch_shapes=[pltpu.VMEM(cg.shape, F32)],
        compiler_params=_params(("arbitrary",)),
    )(cg, dm, w)


def _small_reduce(gathered, logits, n_mod_cols, lg_off, loss_off, loss_cols):
    npk = gathered.shape[1]

    def body(g_ref, lo_ref, tot_ref, gb_ref, gl_ref, loss_ref):
        acc = g_ref[0:1, :]
        for dd in range(1, N_DEV):
            acc = acc + g_ref[dd:dd + 1, :]
        tot_ref[...] = acc
        gb_ref[...] = acc[:, :n_mod_cols] + acc[:, n_mod_cols:2 * n_mod_cols]
        gl_ref[...] = acc[:, lg_off:lg_off + LANES_V7X] * _sigmoid(-lo_ref[...])
        loss = jnp.sum(acc[:, loss_off:loss_off + loss_cols], axis=1, keepdims=True)
        loss_ref[...] = jnp.broadcast_to(loss, loss_ref.shape)

    lane = jax.ShapeDtypeStruct((1, LANES_V7X), F32)
    return pl.pallas_call(
        body, name="small_reduce",
        out_shape=[jax.ShapeDtypeStruct((1, npk), F32), jax.ShapeDtypeStruct((1, n_mod_cols), F32), lane, lane],
    )(gathered, logits)


def _c_ctx_grad(parts, c_ctx):
    def body(p_ref, c_ref, o_ref):
        tot = p_ref[0:1, :] + p_ref[2:3, :] + p_ref[4:5, :] + p_ref[6:7, :]
        _, vjp = jax.vjp(_silu, c_ref[...])
        o_ref[...] = vjp(tot)[0]

    return pl.pallas_call(body, name="c_ctx_grad", out_shape=jax.ShapeDtypeStruct(c_ctx.shape, F32))(parts, c_ctx)


def _rope_tables(seq, ctx_rows):
    rows = seq // GRID_W
    half = HEAD_DIM // 2
    inv_freq = ROPE_THETA ** (-jnp.arange(0, half, 2, dtype=F32) / half)
    ang_row = jnp.arange(rows, dtype=F32)[:, None] * inv_freq
    ang_col = jnp.arange(GRID_W, dtype=F32)[:, None] * inv_freq

    def spread(fn):
        return jnp.concatenate([jnp.repeat(fn(ang_row), GRID_W, axis=0), jnp.tile(fn(ang_col), (rows, 1))], axis=-1)

    cos, sin = spread(jnp.cos), spread(jnp.sin)
    cos_full = jnp.repeat(cos, 2, axis=1)
    sin_signed = jnp.stack([-sin, sin], axis=-1).reshape(seq, HEAD_DIM)
    cos_full = jnp.concatenate([jnp.ones((ctx_rows, HEAD_DIM), F32), cos_full], axis=0)
    sin_signed = jnp.concatenate([jnp.zeros((ctx_rows, HEAD_DIM), F32), sin_signed], axis=0)
    return cos_full, sin_signed


def _qk_rot(p, gain, cos_full, sin_signed):
    r = _rmsn(p) * gain
    return r * cos_full + _swap_pairs(r) * sin_signed


def _qk_rot_bwd(g, p, gain, cos_full, sin_signed):
    g1 = g * cos_full + _swap_pairs(g * sin_signed)
    _, vjp = jax.vjp(lambda pp, gn: _rmsn(pp) * gn, p, gain)
    return vjp(g1)


def kernel(x, c, ctx, c_ctx, w_ada, b_ada, ffn1_w_in, ffn1_w_out, mix_w_in, attn_q_gain, attn_k_gain, ret_decay_logit, w_proj_attn, w_proj_ret, mix_w_out, ffn2_w_in, ffn2_w_out, final_norm, loss_target, m_c_ctx, m_w_ada, m_b_ada, m_ffn1_w_in, m_ffn1_w_out, m_mix_w_in, m_attn_q_gain, m_attn_k_gain, m_ret_decay_logit, m_w_proj_attn, m_w_proj_ret, m_mix_w_out, m_ffn2_w_in, m_ffn2_w_out, m_final_norm, v_c_ctx, v_w_ada, v_b_ada, v_ffn1_w_in, v_ffn1_w_out, v_mix_w_in, v_attn_q_gain, v_attn_k_gain, v_ret_decay_logit, v_w_proj_attn, v_w_proj_ret, v_mix_w_out, v_ffn2_w_in, v_ffn2_w_out, v_final_norm):
    xi, yi, ci = lax.axis_index("x"), lax.axis_index("y"), lax.axis_index("c")
    dev = 4 * xi + 2 * yi + ci
    s_me = 2 * xi + yi
    c_arr = jnp.reshape(ci, (1,)).astype(jnp.int32)
    s_arr = jnp.reshape(s_me, (1,)).astype(jnp.int32)

    t, d = x.shape[1], x.shape[2]
    tc = ctx.shape[1]
    tk = tc + t
    ff = ffn1_w_out.shape[1] * N_CHIP
    aw = w_proj_attn.shape[1]
    rw = w_proj_ret.shape[1]
    pw = mix_w_in.shape[2] * N_CHIP
    kvw = (pw - aw - 4 * rw - 2 * d) // 2
    groups = aw // kvw
    n_ret_heads = rw // HEAD_DIM
    mod_cols = N_MOD * d
    tr = _tile(tc, 256, 32)
    nt_all, nt_x, ctx_tiles = tk // tr, t // tr, tc // tr

    c_rows = _gather_row("gather_c", c)
    cg = jnp.concatenate([c_rows, c_ctx[None, :], jnp.zeros((7, d), F32)], axis=0)
    w_ada_l = w_ada[0]
    ada_cols = w_ada_l.shape[1]
    b_ada_l = lax.dynamic_slice_in_dim(b_ada, s_me * ada_cols, ada_cols, axis=1)
    mod_shard = _ada_fwd(cg, w_ada_l, b_ada_l)
    mod_g = _all_gather8("gather_mod", mod_shard)
    mod_full = jnp.concatenate([mod_g[0], mod_g[2], mod_g[4], mod_g[6]], axis=1)
    mod_x = lax.dynamic_slice_in_dim(mod_full, dev, 1, axis=0).reshape(N_MOD, d)
    mod_c = mod_full[8].reshape(N_MOD, d)
    mods = jnp.stack([mod_c, mod_x])

    big = [("col", ffn1_w_in), ("row", ffn1_w_out), ("col", mix_w_in), ("col", w_proj_attn), ("col", w_proj_ret),
           ("row", mix_w_out), ("col", ffn2_w_in), ("row", ffn2_w_out)]
    metas = []
    for kind, w in big:
        r_l, c_l = w.shape[1:]
        metas.append(_Sharded(kind, r_l, c_l * N_CHIP) if kind == "col" else _Sharded(kind, r_l * N_CHIP, c_l))
    layer_groups = ((0, 1), (1, 2), (2, 6), (6, 8))
    sems_first, placed_first, token = _split_start(
        "gather_start_first", [_place_shard(metas[0], big[0][1][0], s_arr)],
        [(0, 1, _gather_copies(metas[0:1], True), 3)], mods)
    placed_rest = [_place_shard(m, w[0], s_arr, after=token) for m, (_, w) in zip(metas[1:], big[1:])]
    sems_rest, placed_rest, token = _split_start(
        "gather_start_rest", placed_rest,
        [(lo - 1, hi - lo, _gather_copies(metas[lo:hi], True), 3 * (hi - lo)) for lo, hi in layer_groups[1:]], None)
    gather_sems, placed = sems_first + sems_rest, placed_first + placed_rest
    mods = mods + token[0, 0]

    def weights_of(gi, after):
        lo, hi = layer_groups[gi]
        arrived = _split_wait("gather_wait_%d" % gi, gather_sems[gi], placed[lo:hi],
                              _gather_copies(metas[lo:hi], True), after)
        return _gather_forward("gather_forward_%d" % gi, metas[lo:hi], arrived)

    cos_full, sin_signed = _rope_tables(t, tc)
    q_gain, k_gain = attn_q_gain, attn_k_gain
    log_gamma = jax.nn.log_sigmoid(ret_decay_logit[0])
    lgv = jnp.broadcast_to(log_gamma[:, :, None, None], (2, n_ret_heads, 1, HEAD_DIM))

    def stream_rows(h, off):
        if isinstance(h, tuple):
            return [(h[0], "ctx", None), (h[1], -ctx_tiles, None)]
        return [(h, off, None)]

    def stream_value(h, rows, fulls):
        if isinstance(h, tuple):
            return jnp.where(fulls[-1], rows[0], rows[1]), rows[2:]
        return rows[0], rows[1:]

    def norm_mod(name, h, n_tiles, off, i_shift, i_scale):
        def fn(rows, sel, fulls):
            hv, _ = stream_value(h, rows, fulls)
            return [_rmsn(hv) * (1.0 + sel(i_scale)) + sel(i_shift)], []
        return _rowwise(name, fn, n_tiles=n_tiles, tr=tr, row_ins=stream_rows(h, 0),
                        row_outs=[(n_tiles * tr, d, BF16, 0)], sel_in=mods, sel_off=off, ctx_rows=tc)[0]

    def resid_norm(name, h, h_off, f, n_tiles, off, i_gate, coef, i_shift, i_scale):
        def fn(rows, sel, fulls):
            hv, rest = stream_value(h, rows, fulls)
            hn = hv + coef * sel(i_gate) * rest[0]
            return [hn, _rmsn(hn) * (1.0 + sel(i_scale)) + sel(i_shift)], []
        return _rowwise(name, fn, n_tiles=n_tiles, tr=tr, row_ins=stream_rows(h, h_off) + [(f, 0, None)],
                        row_outs=[(f.shape[0], d, F32, 0), (f.shape[0], d, BF16, 0)], sel_in=mods, sel_off=off,
                        ctx_rows=tc)

    h0 = (ctx[0], x[0])
    n1 = norm_mod("norm_mod1", h0, nt_all, 0, 0, 1)
    w1i, = weights_of(0, n1)
    hm1, ua1, ub1 = _mm_swiglu("ffn1_in", n1, w1i)
    w1o, = weights_of(1, hm1)
    f1 = _mm("ffn1_out", hm1, w1o, "nn", BF16)
    h1, n2 = resid_norm("resid_norm1", h0, 0, f1, nt_all, 0, 2, 0.5, 3, 4)
    wmi, wpa, wpr, wmo = weights_of(2, n2)
    p_q = _mm("mix_in_q", n2, wmi, "nn", F32, 0, aw)
    p_kv = _mm("mix_in_kv", n2, wmi, "nn", F32, aw, 2 * kvw)
    p_r = _mm("mix_in_ret", n2, wmi, "nn", F32, aw + 2 * kvw, 3 * rw)
    p_gr = _mm("mix_in_gr", n2, wmi, "nn", BF16, aw + 2 * kvw + 3 * rw, rw)
    p_gab = _mm("mix_in_gab", n2, wmi, "nn", BF16, aw + 2 * kvw + 4 * rw, 2 * d)

    def q_prep(rows, sel, fulls):
        p, cf, ss = rows
        return _heads_map(lambda ph: [_qk_rot(ph, fulls[0], cf, ss) * QSCALE], [p], aw), []

    q_rot = _rowwise("q_prep", q_prep, n_tiles=nt_x, tr=tr,
                     row_ins=[(p_q, ctx_tiles, None), (cos_full, ctx_tiles, None), (sin_signed, ctx_tiles, None)],
                     row_outs=[(t, aw, BF16, 0)], full_ins=[q_gain])[0]

    def kv_prep(rows, sel, fulls):
        p, cf, ss = rows
        k_rot = _heads_map(lambda ph: [_qk_rot(ph, fulls[0], cf, ss)], [p[:, :kvw]], kvw)[0]
        v_ones = _heads_map(lambda vh: [jnp.concatenate([vh, jnp.ones_like(vh)], axis=1)], [p[:, kvw:]], kvw)[0]
        return [k_rot, v_ones], []

    k_rot, v_att = _rowwise("kv_prep", kv_prep, n_tiles=nt_all, tr=tr,
                            row_ins=[(p_kv, 0, None), (cos_full, 0, None), (sin_signed, 0, None)],
                            row_outs=[(tk, kvw, BF16, 0), (tk, 2 * kvw, BF16, 0)], full_ins=[k_gain])

    ya, lse = _flash_fwd(q_rot, k_rot, v_att, groups)
    y_fwd, y_bwd, states = _ret_fwd(p_r, lgv, tc)

    def ret_out_fn(yf, yb, gr):
        return [_silu(gr) * _rmsn(yf + yb)]

    def ret_out(rows, sel, fulls):
        return _heads_map(ret_out_fn, rows, rw), []

    y_rows = [(y_fwd, ctx_tiles, None), (y_bwd, ctx_tiles, None), (p_gr, ctx_tiles, None)]
    yr = _rowwise("ret_out", ret_out, n_tiles=nt_x, tr=tr, row_ins=y_rows, row_outs=[(t, rw, BF16, 0)])[0]

    pa = _mm("proj_attn", ya, wpa, "nn", BF16)
    prj = _mm("proj_ret", yr, wpr, "nn", BF16)

    def merge_fn(a, r, ga, gb):
        return _sigmoid(ga) * a + _sigmoid(gb) * r

    gate_rows = [(p_gab, ctx_tiles, (d, 0)), (p_gab, ctx_tiles, (d, 1))]
    z = _rowwise("merge", lambda rows, sel, fulls: ([merge_fn(*rows)], []), n_tiles=nt_x, tr=tr,
                 row_ins=[(pa, 0, None), (prj, 0, None)] + gate_rows, row_outs=[(t, d, BF16, 0)])[0]
    fo = _mm("mix_out", z, wmo, "nn", BF16)
    h2, n3 = resid_norm("resid_norm2", h1, ctx_tiles, fo, nt_x, ctx_tiles, 5, 1.0, 6, 7)
    w2i, w2o = weights_of(3, n3)
    hm2, ua2, ub2 = _mm_swiglu("ffn2_in", n3, w2i)
    f2 = _mm("ffn2_out", hm2, w2o, "nn", BF16)

    def loss_fn(rows, sel, fulls):
        h2v, f2v, tgt = rows
        g3 = 0.5 * sel(8)
        y, vjp = jax.vjp(lambda hh, ww: _rmsn(hh) * ww, h2v + g3 * f2v, fulls[0])
        err = y - tgt
        dh, dw = vjp(err / d)
        return [dh, g3 * dh], [0.5 / d * jnp.sum(err * err, axis=0, keepdims=True), dw,
                               jnp.sum(0.5 * dh * f2v, axis=0, keepdims=True)]

    dh3, df2, loss_acc = _rowwise("loss_head", loss_fn, n_tiles=nt_x, tr=tr,
                                  row_ins=[(h2, 0, None), (f2, 0, None), (loss_target[0], 0, None)],
                                  row_outs=[(t, d, F32, 0), (t, d, BF16, 0)], sel_in=mods, sel_off=ctx_tiles,
                                  ctx_rows=tc, full_ins=[final_norm[None, :]], acc_shape=(8, d))
    loss_cols, g_final, dg3 = loss_acc[1, 0:1], loss_acc[1, 1:2], loss_acc[1, 2:3]

    def swiglu_bwd(name, dhm, ua, ub):
        rows_n = dhm.shape[0]
        tr_w = _tile(tr, 128, 32)

        def fn(rows, sel, fulls):
            g, a, b = rows
            _, vjp = jax.vjp(lambda aa, bb: _silu(aa) * bb, a.astype(F32), b.astype(F32))
            da, db = vjp(g)
            return [jnp.concatenate([da, db], axis=1)], []
        return _rowwise(name, fn, n_tiles=rows_n // tr_w, tr=tr_w,
                        row_ins=[(dhm, 0, None), (ua, 0, None), (ub, 0, None)],
                        row_outs=[(rows_n, 2 * ff, BF16, 0)])[0]

    def norm_mod_bwd(name, dn, h, dres, dres_off, n_tiles, off, i_shift, i_scale, gate=None, out_off=0):
        def fn(rows, sel, fulls):
            hh, rows = stream_value(h, rows, fulls)
            g, dr = rows[:2]
            if dres_off < 0:
                dr = jnp.where(fulls[-1], 0.0, dr)
            _, vjp = jax.vjp(lambda a, sh, sc: _rmsn(a) * (1.0 + sc) + sh, hh,
                             sel(i_shift), sel(i_scale))
            dhh, dsh, dsc = vjp(g)
            dh = dr + dhh
            if gate is None:
                return [dh], [dsh, dsc]
            return [dh, gate[2] * sel(gate[1]) * dh], [dsh, dsc, jnp.sum(gate[2] * dh * rows[2], axis=0, keepdims=True)]
        n_rows = dn.shape[0] + out_off * tr
        row_ins = stream_rows(h, 0) + [(dn, 0, None), (dres, dres_off, None)]
        row_outs = [(n_rows, d, F32, out_off)]
        if gate is not None:
            row_ins.append((gate[0], 0, None))
            row_outs.append((n_rows, d, BF16, out_off))
        return _rowwise(name, fn, n_tiles=n_tiles, tr=tr, row_ins=row_ins, row_outs=row_outs, sel_in=mods,
                        sel_off=off, ctx_rows=tc, acc_shape=(8, d))

    g_w2o = _mm("ffn2_out_dw", hm2, df2, "tn", BF16)
    dhm2 = _mm("ffn2_out_dx", df2, w2o, "nt", F32)
    du2 = swiglu_bwd("swiglu_bwd2", dhm2, ua2, ub2)
    g_w2i = _mm("ffn2_in_dw", n3, du2, "tn", BF16)
    dn3 = _mm("ffn2_in_dx", du2, w2i, "nt", F32)
    dh2, dfo, acc_n3 = norm_mod_bwd("norm_mod_bwd3", dn3, h2, dh3, 0, nt_x, ctx_tiles, 6, 7, gate=(fo, 5, 1.0))

    sems_ffn2, thru_ffn2, token = _reduce_start("reduce_start_ffn2", metas[6:8], [g_w2i, g_w2o])

    g_wmo = _mm("mix_out_dw", z, dfo, "tn", BF16, after=token)
    dz = _mm("mix_out_dx", dfo, wmo, "nt", F32)

    def merge_bwd(rows, sel, fulls):
        g, a, r, ga, gb = rows
        _, vjp = jax.vjp(merge_fn, a, r, ga, gb)
        da, dr, dga, dgb = vjp(g)
        return [da, dr, jnp.concatenate([dga, dgb], axis=1)], []

    dpa, dpr, dgab = _rowwise("merge_bwd", merge_bwd, n_tiles=nt_x, tr=tr,
                              row_ins=[(dz, 0, None), (pa, 0, None), (prj, 0, None)] + gate_rows,
                              row_outs=[(t, d, BF16, 0), (t, d, BF16, 0), (t, 2 * d, BF16, 0)])
    g_wpa = _mm("proj_attn_dw", ya, dpa, "tn", BF16)
    dya = _mm("proj_attn_dx", dpa, wpa, "nt", BF16)
    g_wpr = _mm("proj_ret_dw", yr, dpr, "tn", BF16)
    dyr = _mm("proj_ret_dx", dpr, wpr, "nt", F32)

    def ret_out_bwd(rows, sel, fulls):
        def per_head(g, yf, yb, gr):
            _, vjp = jax.vjp(lambda yy, gg: ret_out_fn(yy, 0.0, gg)[0], yf + yb, gr)
            return list(vjp(g))
        dy, dgr = _heads_map(per_head, rows, rw)
        return [dy, dgr], []

    dy_ret, dgr = _rowwise("ret_out_bwd", ret_out_bwd, n_tiles=nt_x, tr=tr, row_ins=[(dyr, 0, None)] + y_rows,
                           row_outs=[(t, rw, F32, 0), (t, rw, BF16, 0)])
    dp_rf, dp_rb, dlg = _ret_bwd(p_r, states, dy_ret, lgv, tc)
    dp_r = _rowwise("ret_bwd_sum", lambda rows, sel, fulls: ([rows[0] + rows[1]], []), n_tiles=nt_all, tr=tr,
                    row_ins=[(dp_rf, 0, None), (dp_rb, 0, None)], row_outs=[(tk, 3 * rw, BF16, 0)])[0]

    dq_rot, dk_rot, dv_att = _flash_bwd(q_rot, k_rot, v_att, ya, dya, lse, groups)

    def q_prep_bwd(rows, sel, fulls):
        g, p, cf, ss = rows
        gain_acc = []

        def per_head(gh, ph):
            dp, dgain = _qk_rot_bwd(gh * HEAD_DIM ** -0.5, ph, fulls[0], cf, ss)
            gain_acc.append(dgain)
            return [dp]
        dp = _heads_map(per_head, [g, p], aw)[0]
        return [dp], [functools.reduce(lambda a, b: a + b, gain_acc)]

    dp_q, acc_gq = _rowwise("q_prep_bwd", q_prep_bwd, n_tiles=nt_x, tr=tr,
                            row_ins=[(dq_rot, 0, None), (p_q, ctx_tiles, None), (cos_full, ctx_tiles, None),
                                     (sin_signed, ctx_tiles, None)],
                            row_outs=[(t, aw, BF16, 0)], full_ins=[q_gain], acc_shape=(8, HEAD_DIM),
                            sel_off=ctx_tiles, ctx_rows=tc)

    def kv_prep_bwd(rows, sel, fulls):
        gk, gv, p, cf, ss = rows
        gain_acc = []

        def per_head(gh, ph):
            dp, dgain = _qk_rot_bwd(gh, ph, fulls[0], cf, ss)
            gain_acc.append(dgain)
            return [dp]
        dpk = _heads_map(per_head, [gk, p], kvw)[0]
        return [jnp.concatenate([dpk, gv], axis=1)], [functools.reduce(lambda a, b: a + b, gain_acc)]

    dp_kv, acc_gk = _rowwise("kv_prep_bwd", kv_prep_bwd, n_tiles=nt_all, tr=tr,
                             row_ins=[(dk_rot, 0, None), (dv_att, 0, None), (p_kv, 0, (kvw, 0)), (cos_full, 0, None),
                                      (sin_signed, 0, None)],
                             row_outs=[(tk, 2 * kvw, BF16, 0)], full_ins=[k_gain], acc_shape=(8, HEAD_DIM),
                             sel_off=0, ctx_rows=tc)

    def with_ctx_zeros(a):
        return jnp.concatenate([jnp.zeros((tc, a.shape[1]), a.dtype), a], axis=0)

    dp = jnp.concatenate([with_ctx_zeros(dp_q), dp_kv, dp_r, with_ctx_zeros(dgr), with_ctx_zeros(dgab)], axis=1)
    g_wmi = _mm("mix_in_dw", n2, dp, "tn", BF16)
    dn2 = _mm("mix_in_dx", dp, wmi, "nt", F32)
    dh1, df1, acc_n2 = norm_mod_bwd("norm_mod_bwd2", dn2, h1, dh2, -ctx_tiles, nt_all, 0, 3, 4, gate=(f1, 2, 0.5))
    sems_mix, thru_mix, token = _reduce_start("reduce_start_mix", metas[2:6], [g_wmi, g_wpa, g_wpr, g_wmo])

    g_w1o = _mm("ffn1_out_dw", hm1, df1, "tn", BF16, after=token)
    sems_w1o, thru_w1o, token = _reduce_start("reduce_start_ffn1_out", metas[1:2], [g_w1o])
    dhm1 = _mm("ffn1_out_dx", df1, w1o, "nt", F32, after=token)
    du1 = swiglu_bwd("swiglu_bwd1", dhm1, ua1, ub1)
    g_w1i = _mm("ffn1_in_dw", n1, du1, "tn", BF16)
    sems_w1i, thru_w1i, token = _reduce_start("reduce_start_ffn1_in", metas[0:1], [g_w1i])
    dn1 = _mm("ffn1_in_dx", du1, w1i, "nt", F32, after=token)
    dh0, acc_n1 = norm_mod_bwd("norm_mod_bwd1", dn1, h0, dh1, 0, nt_all, 0, 0, 1, out_off=-ctx_tiles)
    grad_x = dh0[None]

    grads_own, landed = [], []
    for name, lo, hi, sems_l, thru_l in (("reduce_wait_ffn1_in", 0, 1, sems_w1i, thru_w1i),
                                         ("reduce_wait_ffn1_out", 1, 2, sems_w1o, thru_w1o),
                                         ("reduce_wait_mix", 2, 6, sems_mix, thru_mix),
                                         ("reduce_wait_ffn2", 6, 8, sems_ffn2, thru_ffn2)):
        grads_l, landed_l = _reduce_wait(name, metas[lo:hi], sems_l, thru_l, dh0)
        grads_own += grads_l
        landed += landed_l
    pieces = [_sum_pieces(m, g, l, s_arr, c_arr) for m, g, l in zip(metas, grads_own, landed)]
    grads_big = _share_halves(metas, pieces)

    zero_row = jnp.zeros((1, d), F32)
    dmod_x = jnp.concatenate([acc_n1[1, 0:1], acc_n1[1, 1:2], acc_n2[1, 2:3], acc_n2[1, 0:1], acc_n2[1, 1:2],
                              acc_n3[1, 2:3], acc_n3[1, 0:1], acc_n3[1, 1:2], dg3], axis=1)
    dmod_c = jnp.concatenate([acc_n1[0, 0:1], acc_n1[0, 1:2], acc_n2[0, 2:3], acc_n2[0, 0:1], acc_n2[0, 1:2]]
                             + [zero_row] * 4, axis=1)
    dlg_row = jnp.pad(dlg[:, :, 0, 0].reshape(1, 2 * n_ret_heads), ((0, 0), (0, LANES_V7X - 2 * n_ret_heads)))
    packed = jnp.concatenate([dmod_x, dmod_c, acc_gq[1, 0:1], acc_gk[0, 0:1] + acc_gk[1, 0:1], dlg_row,
                              g_final, loss_cols], axis=1)
    off_gq = 2 * mod_cols
    off_gk = off_gq + LANES_V7X
    off_lg = off_gk + LANES_V7X
    off_fn = off_lg + LANES_V7X
    off_loss = off_fn + d
    gathered = _gather_row("gather_small", packed)
    logits_row = jnp.pad(ret_decay_logit.reshape(1, 2 * n_ret_heads), ((0, 0), (0, LANES_V7X - 2 * n_ret_heads)))
    totals, g_b_ada, g_decay, loss_row = _small_reduce(gathered, logits_row, mod_cols, off_lg, off_loss, d)
    loss = loss_row[0, 0]

    dm = jnp.concatenate([gathered[:, :mod_cols], totals[:, mod_cols:2 * mod_cols],
                          jnp.zeros((7, mod_cols), F32)], axis=0)
    dm_l = lax.dynamic_slice_in_dim(dm, s_me * ada_cols, ada_cols, axis=1)
    g_w_ada, da_part = _ada_bwd(cg, dm_l, w_ada_l)
    da_rows = _gather_row("gather_dc", da_part[8:9])
    g_c_ctx = _c_ctx_grad(da_rows, c_ctx[None, :])

    def as2d(a):
        return a.reshape(-1, a.shape[-1])

    grads = {
        "c_ctx": g_c_ctx, "w_ada": g_w_ada, "b_ada": g_b_ada,
        "ffn1_w_in": grads_big[0], "ffn1_w_out": grads_big[1], "mix_w_in": grads_big[2],
        "attn_q_gain": totals[:, off_gq:off_gq + HEAD_DIM], "attn_k_gain": totals[:, off_gk:off_gk + HEAD_DIM],
        "ret_decay_logit": g_decay[:, :2 * n_ret_heads],
        "w_proj_attn": grads_big[3], "w_proj_ret": grads_big[4], "mix_w_out": grads_big[5],
        "ffn2_w_in": grads_big[6], "ffn2_w_out": grads_big[7], "final_norm": totals[:, off_fn:off_fn + d],
    }
    weights = {"c_ctx": (c_ctx, m_c_ctx, v_c_ctx), "w_ada": (w_ada, m_w_ada, v_w_ada),
               "b_ada": (b_ada, m_b_ada, v_b_ada), "ffn1_w_in": (ffn1_w_in, m_ffn1_w_in, v_ffn1_w_in),
               "ffn1_w_out": (ffn1_w_out, m_ffn1_w_out, v_ffn1_w_out), "mix_w_in": (mix_w_in, m_mix_w_in, v_mix_w_in),
               "attn_q_gain": (attn_q_gain, m_attn_q_gain, v_attn_q_gain),
               "attn_k_gain": (attn_k_gain, m_attn_k_gain, v_attn_k_gain),
               "ret_decay_logit": (ret_decay_logit, m_ret_decay_logit, v_ret_decay_logit),
               "w_proj_attn": (w_proj_attn, m_w_proj_attn, v_w_proj_attn),
               "w_proj_ret": (w_proj_ret, m_w_proj_ret, v_w_proj_ret), "mix_w_out": (mix_w_out, m_mix_w_out, v_mix_w_out),
               "ffn2_w_in": (ffn2_w_in, m_ffn2_w_in, v_ffn2_w_in), "ffn2_w_out": (ffn2_w_out, m_ffn2_w_out, v_ffn2_w_out),
               "final_norm": (final_norm, m_final_norm, v_final_norm)}
    out_g, out_d, out_m, out_v = [], [], [], []
    for name, (w, m, v) in weights.items():
        shape = w.shape
        if name == "ret_decay_logit":
            w2, m2, v2 = (a.reshape(1, -1) for a in (w, m, v))
        else:
            w2, m2, v2 = as2d(w), as2d(m), as2d(v)
        g2 = grads[name].reshape(w2.shape)
        delta, new_m, new_v = _adamw(w2, g2, m2, v2)
        out_g.append(g2.reshape(shape))
        out_d.append(delta.reshape(shape))
        out_m.append(new_m.reshape(shape))
        out_v.append(new_v.reshape(shape))
    return (loss, grad_x, *out_g, *out_d, *out_m, *out_v)
```

```python
import functools
import math

import jax
import jax.numpy as jnp
from jax import lax
from jax.experimental import pallas as pl
from jax.experimental.pallas import tpu as pltpu

F32 = jnp.float32
BF16 = jnp.bfloat16

HEAD_DIM = 128
GRID_W = 64
ROPE_THETA = 10000.0
NORM_EPS = 1e-6
N_MOD = 9
RET_CHUNK = 128
ADAM_LR = 0.001
ADAM_B1 = 0.9
ADAM_B2 = 0.999
ADAM_EPS = 1e-08
ADAM_WD = 0.01
ADAM_STEP = 10

N_DEV = 8
N_CHIP = 4
LANES_V7X = 128
MXU_WIDTH_V7X = 256
VMEM_LIMIT_V7X = 52 * 1024 * 1024

NT_DIMS = (((1,), (1,)), ((), ()))
TN_DIMS = (((0,), (0,)), ((), ()))
NN_DIMS = (((1,), (0,)), ((), ()))


def _tile(n, pref, mult=LANES_V7X):
    if n <= pref:
        return n
    t = (pref // mult) * mult
    while t >= mult:
        if n % t == 0:
            return t
        t -= mult
    return n


def _params(sem):
    return pltpu.CompilerParams(dimension_semantics=sem, vmem_limit_bytes=VMEM_LIMIT_V7X)


def _sigmoid(x):
    return 1.0 / (1.0 + jnp.exp(-x))


def _silu(x):
    return x * _sigmoid(x)


def _rmsn(x):
    return x * lax.rsqrt(jnp.mean(x * x, axis=-1, keepdims=True) + NORM_EPS)


MM_VMEM_BUDGET = 44 * 1024 * 1024


def _divisor_tiles(n, cap):
    ts = [t for t in range(LANES_V7X, min(n, cap) + 1, LANES_V7X) if n % t == 0]
    return ts or [n]


def _mm_tiles(m, n, tk, out_bytes, has_acc):
    best = None
    for tm in _divisor_tiles(m, 1536):
        for tn in _divisor_tiles(n, 2560):
            need = 4 * tk * (tm + tn) + 2 * tm * tn * out_bytes + 4 * tm * tn
            if need > MM_VMEM_BUDGET:
                continue
            score = tm * tn / (tm + tn)
            for tdim in (tm, tn):
                if tdim % MXU_WIDTH_V7X:
                    score *= 0.85
            if best is None or score > best[0]:
                best = (score, tm, tn)
    return best[1], best[2]


def _mm(name, a, b, mode, out_dtype, b_off=0, n=None, after=None):
    if mode == "nn":
        m, k = a.shape
        n = b.shape[1] if n is None else n
        dims = NN_DIMS
    elif mode == "nt":
        m, k = a.shape
        n = b.shape[0]
        dims = NT_DIMS
    else:
        k, m = a.shape
        n = b.shape[1]
        dims = TN_DIMS
    tk = _tile(k, 2816)
    nk = k // tk
    tm, tn = _mm_tiles(m, math.gcd(n, b_off) if b_off else n, tk, jnp.dtype(out_dtype).itemsize, nk > 1)
    joff = b_off // tn

    def body(a_ref, b_ref, *rest):
        o_ref = rest[0 if after is None else 1]
        if nk == 1:
            o_ref[...] = lax.dot_general(a_ref[...], b_ref[...], dims,
                                         preferred_element_type=F32).astype(o_ref.dtype)
            return
        acc_ref = rest[-1]
        kk = pl.program_id(2)

        @pl.when(kk == 0)
        def _():
            acc_ref[...] = jnp.zeros_like(acc_ref)

        acc_ref[...] += lax.dot_general(a_ref[...], b_ref[...], dims, preferred_element_type=F32)

        @pl.when(kk == nk - 1)
        def _():
            o_ref[...] = acc_ref[...].astype(o_ref.dtype)

    if mode == "nn":
        a_spec = pl.BlockSpec((tm, tk), lambda i, j, kk: (i, kk))
        b_spec = pl.BlockSpec((tk, tn), lambda i, j, kk: (kk, j + joff))
    elif mode == "nt":
        a_spec = pl.BlockSpec((tm, tk), lambda i, j, kk: (i, kk))
        b_spec = pl.BlockSpec((tn, tk), lambda i, j, kk: (j, kk))
    else:
        a_spec = pl.BlockSpec((tk, tm), lambda i, j, kk: (kk, i))
        b_spec = pl.BlockSpec((tk, tn), lambda i, j, kk: (kk, j))
    return pl.pallas_call(
        body, name=name, grid=(m // tm, n // tn, nk),
        in_specs=[a_spec, b_spec] + ([] if after is None else [pl.BlockSpec(memory_space=pl.ANY)]),
        out_specs=pl.BlockSpec((tm, tn), lambda i, j, kk: (i, j)),
        out_shape=jax.ShapeDtypeStruct((m, n), out_dtype),
        scratch_shapes=[pltpu.VMEM((tm, tn), F32)] if nk > 1 else [],
        compiler_params=_params(("parallel", "parallel", "arbitrary")),
    )(*((a, b) if after is None else (a, b, after)))


def _mm_swiglu(name, a, w):
    m, k = a.shape
    f = w.shape[1] // 2
    tm = _tile(m, 1024)
    tn = _tile(f, 512)
    tk = _tile(k, 2560)
    nk = k // tk
    jf = f // tn

    def body(a_ref, wa_ref, wb_ref, h_ref, ua_ref, ub_ref, acca, accb):
        kk = pl.program_id(2)

        @pl.when(kk == 0)
        def _():
            acca[...] = jnp.zeros_like(acca)
            accb[...] = jnp.zeros_like(accb)

        av = a_ref[...]
        acca[...] += jnp.dot(av, wa_ref[...], preferred_element_type=F32)
        accb[...] += jnp.dot(av, wb_ref[...], preferred_element_type=F32)

        @pl.when(kk == nk - 1)
        def _():
            ua = acca[...]
            ub = accb[...]
            h_ref[...] = (_silu(ua) * ub).astype(BF16)
            ua_ref[...] = ua.astype(BF16)
            ub_ref[...] = ub.astype(BF16)

    o_spec = pl.BlockSpec((tm, tn), lambda i, j, kk: (i, j))
    o_shape = jax.ShapeDtypeStruct((m, f), BF16)
    return pl.pallas_call(
        body, name=name, grid=(m // tm, jf, nk),
        in_specs=[pl.BlockSpec((tm, tk), lambda i, j, kk: (i, kk)),
                  pl.BlockSpec((tk, tn), lambda i, j, kk: (kk, j)),
                  pl.BlockSpec((tk, tn), lambda i, j, kk: (kk, j + jf))],
        out_specs=[o_spec, o_spec, o_spec],
        out_shape=[o_shape, o_shape, o_shape],
        scratch_shapes=[pltpu.VMEM((tm, tn), F32), pltpu.VMEM((tm, tn), F32)],
        compiler_params=_params(("parallel", "parallel", "arbitrary")),
    )(a, w, w)


def _rowwise(name, fn, *, n_tiles, tr, row_ins, row_outs, sel_in=None, sel_off=0, ctx_rows=0,
             full_ins=(), acc_shape=None):
    sr = 128 if tr % 128 == 0 else (32 if tr % 32 == 0 else tr)
    n_row, n_full, n_out = len(row_ins), len(full_ins), len(row_outs)
    has_sel = sel_in is not None
    has_acc = acc_shape is not None

    def sel_of(i):
        return jnp.where((i + sel_off) * tr < ctx_rows, 0, 1)

    def body(*refs):
        row_refs = refs[:n_row]
        pos = n_row
        sel_ref = None
        if has_sel:
            sel_ref = refs[pos]
            pos += 1
        full_refs = refs[pos:pos + n_full]
        pos += n_full
        out_refs = refs[pos:pos + n_out]
        pos += n_out
        acc_ref = refs[pos] if has_acc else None
        i = pl.program_id(0)
        if has_acc:
            first = (i == 0) | ((i + sel_off) * tr == ctx_rows)

            @pl.when(first)
            def _():
                acc_ref[...] = jnp.zeros_like(acc_ref)

        sel = (lambda kk: sel_ref[kk:kk + 1, :]) if has_sel else None
        fulls = [r[...] for r in full_refs] + [(i + sel_off) * tr < ctx_rows]

        def slab(r, carry):
            rs = pl.ds(pl.multiple_of(r * sr, sr), sr)
            rows = [ref[rs, :].astype(F32) for ref in row_refs]
            outs, accs = fn(rows, sel, fulls)
            for o_ref, o in zip(out_refs, outs):
                o_ref[rs, :] = o.astype(o_ref.dtype)
            for kk, a in enumerate(accs):
                acc_ref[kk:kk + 1, :a.shape[1]] += a
            return carry

        lax.fori_loop(0, tr // sr, slab, 0)

    def row_map(off, col=0):
        if off == "ctx":
            return lambda i: (jnp.minimum(i, ctx_rows // tr - 1), col)
        if off < 0:
            return lambda i: (jnp.maximum(i + off, 0), col)
        return lambda i: (i + off, col)

    in_specs, args = [], []
    for arr, off, blk in row_ins:
        if blk is None:
            in_specs.append(pl.BlockSpec((tr, arr.shape[1]), row_map(off)))
        else:
            in_specs.append(pl.BlockSpec((tr, blk[0]), row_map(off, blk[1])))
        args.append(arr)
    if has_sel:
        in_specs.append(pl.BlockSpec((None,) + sel_in.shape[1:], lambda i: (sel_of(i), 0, 0)))
        args.append(sel_in)
    for arr in full_ins:
        in_specs.append(pl.BlockSpec(arr.shape, lambda i: (0, 0)))
        args.append(arr)
    out_specs, out_shape = [], []
    for rows, cols, dt, off in row_outs:
        out_specs.append(pl.BlockSpec((tr, cols), row_map(off)))
        out_shape.append(jax.ShapeDtypeStruct((rows, cols), dt))
    if has_acc:
        out_specs.append(pl.BlockSpec((None,) + tuple(acc_shape), lambda i: (sel_of(i), 0, 0)))
        out_shape.append(jax.ShapeDtypeStruct((2,) + tuple(acc_shape), F32))
    return pl.pallas_call(
        body, name=name, grid=(n_tiles,), in_specs=in_specs, out_specs=out_specs, out_shape=out_shape,
        compiler_params=_params(("arbitrary",)),
    )(*args)


def _swap_pairs(x):
    lane = lax.broadcasted_iota(jnp.int32, x.shape, 1)
    nxt = pltpu.roll(x, x.shape[1] - 1, 1)
    prv = pltpu.roll(x, 1, 1)
    return jnp.where(lane % 2 == 0, nxt, prv)


def _heads_map(fn, arrs, width):
    outs = None
    for h in range(width // HEAD_DIM):
        sl = slice(h * HEAD_DIM, (h + 1) * HEAD_DIM)
        res = fn(*[a[:, sl] for a in arrs])
        if outs is None:
            outs = [[] for _ in res]
        for lst, r in zip(outs, res):
            lst.append(r)
    return [jnp.concatenate(lst, axis=1) if len(lst) > 1 else lst[0] for lst in outs]


QSCALE = HEAD_DIM ** -0.5 * math.log2(math.e)
LN2 = math.log(2.0)


def _lane_chunks(a):
    return [a[:, cc * LANES_V7X:(cc + 1) * LANES_V7X] for cc in range(a.shape[1] // LANES_V7X)]


def _row_bcast(col, like):
    return jnp.broadcast_to(col, like.shape)


def _flash_tiles(t, tk_all, key_pref):
    return _tile(t, 1024), _tile(tk_all, key_pref)


def _flash_fwd(q, k, vx, groups):
    t, aw = q.shape
    tk_all, kvw = k.shape
    kvh = kvw // HEAD_DIM
    gw = groups * HEAD_DIM
    tq, tk = _flash_tiles(t, tk_all, 1536)
    nk = tk_all // tk

    def body(q_ref, k_ref, v_ref, o_ref, lse_ref, m_sc, l_sc, acc_sc):
        j = pl.program_id(2)

        @pl.when(j == 0)
        def _():
            m_sc[...] = jnp.full_like(m_sc, -jnp.inf)
            l_sc[...] = jnp.zeros_like(l_sc)
            acc_sc[...] = jnp.zeros_like(acc_sc)

        kt = k_ref[...]
        vt = v_ref[...]
        for g in range(groups):
            sl = slice(g * HEAD_DIM, (g + 1) * HEAD_DIM)
            s = _lane_chunks(lax.dot_general(q_ref[:, sl], kt, NT_DIMS, preferred_element_type=F32))
            mx = functools.reduce(jnp.maximum, s)
            m_prev = m_sc[g]
            m_new = jnp.maximum(m_prev, _row_bcast(jnp.max(mx, axis=1, keepdims=True), mx))
            p = jnp.concatenate([jnp.exp2(sc - m_new).astype(BF16) for sc in s], axis=1)
            alpha = jnp.exp2(m_prev - m_new)
            pv = jnp.dot(p, vt, preferred_element_type=F32)
            acc_sc[g] = alpha * acc_sc[g] + pv[:, :HEAD_DIM]
            l_sc[g] = alpha * l_sc[g] + pv[:, HEAD_DIM:]
            m_sc[g] = m_new

        @pl.when(j == nk - 1)
        def _():
            for g in range(groups):
                sl = slice(g * HEAD_DIM, (g + 1) * HEAD_DIM)
                o_ref[:, sl] = (acc_sc[g] / l_sc[g]).astype(o_ref.dtype)
                lse_ref[:, sl] = m_sc[g] + jnp.log2(l_sc[g])

    qs = pl.BlockSpec((tq, gw), lambda kh, i, j: (i, kh))
    sc = pltpu.VMEM((groups, tq, HEAD_DIM), F32)
    return pl.pallas_call(
        body, name="flash_fwd", grid=(kvh, t // tq, nk),
        in_specs=[qs, pl.BlockSpec((tk, HEAD_DIM), lambda kh, i, j: (j, kh)),
                  pl.BlockSpec((tk, 2 * HEAD_DIM), lambda kh, i, j: (j, kh))],
        out_specs=[qs, qs],
        out_shape=[jax.ShapeDtypeStruct((t, aw), BF16), jax.ShapeDtypeStruct((t, aw), F32)],
        scratch_shapes=[sc, sc, sc],
        compiler_params=_params(("parallel", "parallel", "arbitrary")),
    )(q, k, vx)


def _flash_p_ds(q, kt, vt, do, lse, delta):
    s = _lane_chunks(lax.dot_general(q, kt, NT_DIMS, preferred_element_type=F32))
    dp = _lane_chunks(lax.dot_general(do, vt, NT_DIMS, preferred_element_type=F32))
    p = [jnp.exp2(sc - lse) for sc in s]
    ds = jnp.concatenate([(pc * (dc - delta)).astype(BF16) for pc, dc in zip(p, dp)], axis=1)
    return jnp.concatenate([pc.astype(BF16) for pc in p], axis=1), ds


def _flash_delta(do, o):
    prod = do.astype(F32) * o.astype(F32)
    return _row_bcast(jnp.sum(prod, axis=1, keepdims=True), prod)


def _flash_bwd(q, k, vx, o, do, lse, groups):
    t, aw = q.shape
    tk_all, kvw = k.shape
    kvh = kvw // HEAD_DIM
    gw = groups * HEAD_DIM
    tq, tk = _flash_tiles(t, tk_all, 1024)
    nq, nk = t // tq, tk_all // tk

    def body(q_ref, k_ref, v_ref, o_ref, do_ref, lse_ref, dq_ref, dk_ref, dv_ref, dq_sc, dk_acc, dv_acc):
        j = pl.program_id(1)
        i = pl.program_id(2)

        @pl.when(i == 0)
        def _():
            dk_acc[...] = jnp.zeros_like(dk_acc)
            dv_acc[...] = jnp.zeros_like(dv_acc)

        @pl.when(j == 0)
        def _():
            dq_sc[i] = jnp.zeros((groups, tq, HEAD_DIM), F32)

        kt = k_ref[...]
        vt = v_ref[:, :HEAD_DIM]
        for g in range(groups):
            sl = slice(g * HEAD_DIM, (g + 1) * HEAD_DIM)
            qv = q_ref[:, sl]
            dov = do_ref[:, sl]
            p, ds = _flash_p_ds(qv, kt, vt, dov, lse_ref[:, sl], _flash_delta(dov, o_ref[:, sl]))
            dv_acc[...] += lax.dot_general(p, dov, TN_DIMS, preferred_element_type=F32)
            dk_acc[...] += lax.dot_general(ds, qv, TN_DIMS, preferred_element_type=F32)
            dq_sc[i, g] += jnp.dot(ds, kt, preferred_element_type=F32)

        @pl.when(i == nq - 1)
        def _():
            dk_ref[...] = dk_acc[...] * LN2
            dv_ref[...] = dv_acc[...]

        @pl.when(j == nk - 1)
        def _():
            for g in range(groups):
                dq_ref[:, g * HEAD_DIM:(g + 1) * HEAD_DIM] = dq_sc[i, g]

    qs = pl.BlockSpec((tq, gw), lambda kh, j, i: (i, kh))
    ks = pl.BlockSpec((tk, HEAD_DIM), lambda kh, j, i: (j, kh))
    dq_spec = pl.BlockSpec((tq, gw), lambda kh, j, i: (jnp.where(j == nk - 1, i, 0), kh))
    return pl.pallas_call(
        body, name="flash_bwd", grid=(kvh, nk, nq),
        in_specs=[qs, ks, pl.BlockSpec((tk, 2 * HEAD_DIM), lambda kh, j, i: (j, kh)), qs, qs, qs],
        out_specs=[dq_spec, ks, ks],
        out_shape=[jax.ShapeDtypeStruct((t, aw), F32), jax.ShapeDtypeStruct((tk_all, kvw), F32),
                   jax.ShapeDtypeStruct((tk_all, kvw), F32)],
        scratch_shapes=[pltpu.VMEM((nq, groups, tq, HEAD_DIM), F32), pltpu.VMEM((tk, HEAD_DIM), F32),
                        pltpu.VMEM((tk, HEAD_DIM), F32)],
        compiler_params=_params(("parallel", "arbitrary", "arbitrary")),
    )(q, k, vx, o, do, lse)


def _bf_nn(a, b):
    return jnp.dot(a.astype(BF16), b.astype(BF16), preferred_element_type=F32)


def _bf_nt(a, b):
    return lax.dot_general(a.astype(BF16), b.astype(BF16), NT_DIMS, preferred_element_type=F32)


def _bf_tn(a, b):
    return lax.dot_general(a.astype(BF16), b.astype(BF16), TN_DIMS, preferred_element_type=F32)


@jax.custom_vjp
def _d_nn(a, b):
    return _bf_nn(a, b)


@jax.custom_vjp
def _d_nt(a, b):
    return _bf_nt(a, b)


@jax.custom_vjp
def _d_tn(a, b):
    return _bf_tn(a, b)


_d_nn.defvjp(lambda a, b: (_bf_nn(a, b), (a, b)), lambda r, g: (_d_nt(g, r[1]), _d_tn(r[0], g)))
_d_nt.defvjp(lambda a, b: (_bf_nt(a, b), (a, b)), lambda r, g: (_d_nn(g, r[1]), _d_tn(g, r[0])))
_d_tn.defvjp(lambda a, b: (_bf_tn(a, b), (a, b)), lambda r, g: (_d_nt(r[1], g), _d_nn(r[0], g)))


def _ret_chunk(q, k_raw, v, state, lg, rev, dots):
    nn, nt, tn = dots
    c = RET_CHUNK
    tcol = lax.broadcasted_iota(jnp.int32, (c, 1), 0).astype(F32)
    trow = lax.broadcasted_iota(jnp.int32, (1, c), 1).astype(F32)
    ucol = jnp.where(rev, c - 1.0 - tcol, tcol)
    urow = jnp.where(rev, c - 1.0 - trow, trow)
    e = ucol - urow
    low = e >= 0
    intra = jnp.where(low, jnp.exp(jnp.where(low, e, 0.0) * lg), 0.0)
    k = k_raw * (HEAD_DIM ** -0.5)
    inner = nt(q, k) * intra
    y = nn(inner, v) + nn(q, state) * jnp.exp((ucol + 1.0) * lg)
    new_state = state * jnp.exp(c * lg) + tn(k * jnp.exp((c - 1.0 - ucol) * lg), v)
    return y, new_state


def _ret_chunk_index(n_chunks, n_ctx_chunks):
    def idx(d, s):
        if d == 0:
            return s
        return jnp.where(s < n_ctx_chunks, n_ctx_chunks - 1 - s, n_chunks - 1 - s + n_ctx_chunks)
    return idx


def _ret_fwd(pr, lgv, ctx_rows):
    tk_all = pr.shape[0]
    rw = pr.shape[1] // 3
    nh = rw // HEAD_DIM
    nc = tk_all // RET_CHUNK
    cidx = _ret_chunk_index(nc, ctx_rows // RET_CHUNK)

    def body(pf_ref, pb_ref, lg_ref, yf_ref, yb_ref, st_ref, s_sc):
        s = pl.program_id(0)

        @pl.when(s == 0)
        def _():
            s_sc[...] = jnp.zeros_like(s_sc)

        for d, (p_ref, y_ref) in enumerate(((pf_ref, yf_ref), (pb_ref, yb_ref))):
            for h in range(nh):
                cols = [slice((part * nh + h) * HEAD_DIM, (part * nh + h + 1) * HEAD_DIM) for part in range(3)]
                state = s_sc[d, h]
                st_ref[d, h] = state
                y, new_state = _ret_chunk(p_ref[:, cols[0]], p_ref[:, cols[1]], p_ref[:, cols[2]], state,
                                          lg_ref[d, h][:, :1], d == 1, (_bf_nn, _bf_nt, _bf_tn))
                y_ref[:, h * HEAD_DIM:(h + 1) * HEAD_DIM] = y
                s_sc[d, h] = new_state

    y_shape = jax.ShapeDtypeStruct((tk_all, rw), F32)
    return pl.pallas_call(
        body, name="ret_fwd", grid=(nc,),
        in_specs=[pl.BlockSpec((RET_CHUNK, 3 * rw), lambda s: (cidx(0, s), 0)),
                  pl.BlockSpec((RET_CHUNK, 3 * rw), lambda s: (cidx(1, s), 0)),
                  pl.BlockSpec(lgv.shape, lambda s: (0, 0, 0, 0))],
        out_specs=[pl.BlockSpec((RET_CHUNK, rw), lambda s: (cidx(0, s), 0)),
                   pl.BlockSpec((RET_CHUNK, rw), lambda s: (cidx(1, s), 0)),
                   pl.BlockSpec((2, nh, None, HEAD_DIM, HEAD_DIM), lambda s: (0, 0, s, 0, 0))],
        out_shape=[y_shape, y_shape, jax.ShapeDtypeStruct((2, nh, nc, HEAD_DIM, HEAD_DIM), F32)],
        scratch_shapes=[pltpu.VMEM((2, nh, HEAD_DIM, HEAD_DIM), F32)],
        compiler_params=_params(("arbitrary",)),
    )(pr, pr, lgv)


def _ret_bwd(pr, states, dy, lgv, ctx_rows):
    tk_all = pr.shape[0]
    rw = pr.shape[1] // 3
    nh = rw // HEAD_DIM
    nc = tk_all // RET_CHUNK
    n_ctx = ctx_rows // RET_CHUNK
    cidx = _ret_chunk_index(nc, n_ctx)

    def body(pf_ref, pb_ref, st_ref, dyf_ref, dyb_ref, lg_ref, dpf_ref, dpb_ref, dlg_ref, ds_sc):
        sp = pl.program_id(0)
        on_ctx = [cidx(dd, nc - 1 - sp) < n_ctx for dd in (0, 1)]

        @pl.when(sp == 0)
        def _():
            ds_sc[...] = jnp.zeros_like(ds_sc)
            dlg_ref[...] = jnp.zeros_like(dlg_ref)

        for d, (p_ref, dy_ref, dp_ref) in enumerate(((pf_ref, dyf_ref, dpf_ref), (pb_ref, dyb_ref, dpb_ref))):
            for h in range(nh):
                cols = [slice((part * nh + h) * HEAD_DIM, (part * nh + h + 1) * HEAD_DIM) for part in range(3)]

                def step(q, k, v, state, lg, rev=(d == 1)):
                    return _ret_chunk(q, k, v, state, lg, rev, (_d_nn, _d_nt, _d_tn))

                _, vjp = jax.vjp(step, p_ref[:, cols[0]], p_ref[:, cols[1]], p_ref[:, cols[2]], st_ref[d, h],
                                 lg_ref[d, h][:, :1])
                dy_h = jnp.where(on_ctx[d], 0.0, dy_ref[:, h * HEAD_DIM:(h + 1) * HEAD_DIM])
                grads = vjp((dy_h, ds_sc[d, h]))
                for part in range(3):
                    dp_ref[:, cols[part]] = grads[part]
                ds_sc[d, h] = grads[3]
                dlg_ref[d, h] += jnp.broadcast_to(grads[4], (1, HEAD_DIM))

    def at(d):
        return lambda sp: (cidx(d, nc - 1 - sp), 0)

    def dy_at(d):
        return lambda sp: (jnp.maximum(cidx(d, nc - 1 - sp) - n_ctx, 0), 0)

    dp_shape = jax.ShapeDtypeStruct((tk_all, 3 * rw), F32)
    lg_spec = pl.BlockSpec(lgv.shape, lambda sp: (0, 0, 0, 0))
    return pl.pallas_call(
        body, name="ret_bwd", grid=(nc,),
        in_specs=[pl.BlockSpec((RET_CHUNK, 3 * rw), at(0)), pl.BlockSpec((RET_CHUNK, 3 * rw), at(1)),
                  pl.BlockSpec((2, nh, None, HEAD_DIM, HEAD_DIM), lambda sp: (0, 0, nc - 1 - sp, 0, 0)),
                  pl.BlockSpec((RET_CHUNK, rw), dy_at(0)), pl.BlockSpec((RET_CHUNK, rw), dy_at(1)), lg_spec],
        out_specs=[pl.BlockSpec((RET_CHUNK, 3 * rw), at(0)), pl.BlockSpec((RET_CHUNK, 3 * rw), at(1)), lg_spec],
        out_shape=[dp_shape, dp_shape, jax.ShapeDtypeStruct(lgv.shape, F32)],
        scratch_shapes=[pltpu.VMEM((2, nh, HEAD_DIM, HEAD_DIM), F32)],
        compiler_params=_params(("arbitrary",)),
    )(pr, pr, states, dy, dy, lgv)


FLIP_X, FLIP_Y, FLIP_XY, FLIP_C = (1, 0, 0), (0, 1, 0), (1, 1, 0), (0, 0, 1)
CHIP_FLIPS = ((FLIP_X, 2), (FLIP_Y, 1), (FLIP_XY, 3))


def _flip(me, mask):
    return tuple(1 - v if m else v for v, m in zip(me, mask))


def _comm(name, ins, out_shapes, plan, n_remote, n_local, aliases=None):
    n_in, n_out = len(ins), len(out_shapes)

    def body(*refs):
        in_refs = refs[:n_in]
        out_refs = refs[n_in:n_in + n_out]
        send_sems, recv_sems, local_sems = refs[n_in + n_out:]
        me = (lax.axis_index("x"), lax.axis_index("y"), lax.axis_index("c"))
        local, phases = plan(in_refs, out_refs, me)
        local_copies = [pltpu.make_async_copy(s, d, local_sems.at[i]) for i, (s, d) in enumerate(local)]
        for cp in local_copies:
            cp.start()
        sent = []
        kk = 0
        for phase in phases:
            arrivals = []
            for mask, src, dst, landing in phase:
                peer = _flip(me, mask)
                cp = pltpu.make_async_remote_copy(src_ref=src, dst_ref=dst, send_sem=send_sems.at[kk],
                                                  recv_sem=recv_sems.at[kk], device_id=peer,
                                                  device_id_type=pl.DeviceIdType.MESH)
                cp.start()
                sent.append(cp)
                arrivals.append(pltpu.make_async_remote_copy(
                    src_ref=landing, dst_ref=landing, send_sem=send_sems.at[kk], recv_sem=recv_sems.at[kk],
                    device_id=peer, device_id_type=pl.DeviceIdType.MESH))
                kk += 1
            for cp in arrivals:
                cp.wait_recv()
        for cp in sent:
            cp.wait_send()
        for cp in local_copies:
            cp.wait()

    any_spec = pl.BlockSpec(memory_space=pl.ANY)
    return pl.pallas_call(
        body, name=name,
        in_specs=[any_spec] * n_in, out_specs=[any_spec] * n_out, out_shape=list(out_shapes),
        scratch_shapes=[pltpu.SemaphoreType.DMA((n_remote,)), pltpu.SemaphoreType.DMA((n_remote,)),
                        pltpu.SemaphoreType.DMA((max(n_local, 1),))],
        input_output_aliases=aliases or {},
    )(*ins)


def _ds(start, size):
    return pl.ds(pl.multiple_of(start * size, 8), size)


def _all_gather8(name, v):
    masks = [(a, b, cc) for a in (0, 1) for b in (0, 1) for cc in (0, 1)][1:]

    def index(p):
        return 4 * p[0] + 2 * p[1] + p[2]

    def plan(in_refs, out_refs, me):
        (src,), (out,) = in_refs, out_refs
        local = [(src, out.at[index(me)])]
        phase = [(m, src, out.at[index(me)], out.at[index(_flip(me, m))]) for m in masks]
        return local, [phase]

    return _comm(name, [v], [jax.ShapeDtypeStruct((N_DEV,) + v.shape, v.dtype)], plan, len(masks), 1)[0]


def _gather_row(name, row):
    n = row.shape[1]
    n_pad = -(-n // (8 * LANES_V7X)) * (8 * LANES_V7X)
    v = jnp.pad(row, ((0, 0), (0, n_pad - n))).reshape(8, n_pad // 8)
    return _all_gather8(name, v).reshape(N_DEV, n_pad)[:, :n]


class _Sharded:
    def __init__(self, kind, rows, cols):
        self.kind, self.rows, self.cols = kind, rows, cols
        self.shard_shape = (rows, cols // N_CHIP) if kind == "col" else (rows // N_CHIP, cols)
        self.half_shape = (rows // 2, cols) if kind == "col" else (rows, cols // 2)
        self.piece_shape = (rows // 2, cols // N_CHIP) if kind == "col" else (rows // N_CHIP, cols // 2)

    def shard_of_full(self, ref, s):
        if self.kind == "col":
            return ref.at[:, _ds(s, self.cols // N_CHIP)]
        return ref.at[_ds(s, self.rows // N_CHIP), :]

    def half_of_full(self, ref, h):
        if self.kind == "col":
            return ref.at[_ds(h, self.rows // 2), :]
        return ref.at[:, _ds(h, self.cols // 2)]

    def piece_of_full(self, ref, s, h):
        if self.kind == "col":
            return ref.at[_ds(h, self.rows // 2), _ds(s, self.cols // N_CHIP)]
        return ref.at[_ds(s, self.rows // N_CHIP), _ds(h, self.cols // 2)]

    def half_of_shard(self, ref, h):
        if self.kind == "col":
            return ref.at[_ds(h, self.rows // 2), :]
        return ref.at[:, _ds(h, self.cols // 2)]

    def shard_of_half(self, ref, s):
        if self.kind == "col":
            return ref.at[:, _ds(s, self.cols // N_CHIP)]
        return ref.at[_ds(s, self.rows // N_CHIP), :]


def _place_shard(meta, w, s_arr, after=None):
    r, cols = w.shape
    tr = _tile(r, 256, 16)
    nr = r // tr

    def body(s_ref, w_ref, *rest):
        rest[-1][...] = w_ref[...].astype(BF16)

    if meta.kind == "col":
        o_map = lambda i, s_ref: (i, s_ref[0])
    else:
        o_map = lambda i, s_ref: (i + s_ref[0] * nr, 0)
    return pl.pallas_call(
        body, name="place_shard",
        grid_spec=pltpu.PrefetchScalarGridSpec(
            num_scalar_prefetch=1, grid=(nr,),
            in_specs=[pl.BlockSpec((tr, cols), lambda i, s_ref: (i, 0))]
            + ([] if after is None else [pl.BlockSpec(memory_space=pl.ANY)]),
            out_specs=pl.BlockSpec((tr, cols), o_map)),
        out_shape=jax.ShapeDtypeStruct((meta.rows, meta.cols), BF16),
        compiler_params=_params(("parallel",)),
    )(*((s_arr, w) if after is None else (s_arr, w, after)))


def _gather_copies(metas, over_ici):
    def copies(fulls, me):
        x, y, c = me
        s_me = 2 * x + y
        out = []
        for meta, full in zip(metas, fulls):
            for mask, bits in CHIP_FLIPS:
                s_peer = jnp.bitwise_xor(s_me, bits)
                if over_ici:
                    out.append((mask, meta.piece_of_full(full, s_me, c), meta.piece_of_full(full, s_me, c),
                                meta.piece_of_full(full, s_peer, c)))
                else:
                    out.append((FLIP_C, meta.piece_of_full(full, s_peer, c), meta.piece_of_full(full, s_peer, c),
                                meta.piece_of_full(full, s_peer, 1 - c)))
        return out
    return copies


def _gather_forward(name, metas, fulls):
    nt = len(metas)
    copies = _gather_copies(metas, False)
    outs = [jax.ShapeDtypeStruct((m.rows, m.cols), BF16) for m in metas]
    return _comm(name, list(fulls), outs, lambda ins, outs_, me: ([], [copies(outs_, me)]), 3 * nt, 0,
                 aliases={i: i for i in range(nt)})


HBM_SPEC = pl.BlockSpec(memory_space=pltpu.HBM)
SEM_SPEC = pl.BlockSpec(memory_space=pltpu.SEMAPHORE)
SPLIT_EFFECT = pltpu.SideEffectType.DATAFLOW_SIDE_EFFECTING


def _split_start(name, bufs, groups, after):
    nb, ng = len(bufs), len(groups)
    n_in = nb + (0 if after is None else 1)

    def body(*refs):
        buf_refs = refs[:nb]
        sem_refs = refs[n_in:n_in + 2 * ng]
        token = refs[-1]
        me = (lax.axis_index("x"), lax.axis_index("y"), lax.axis_index("c"))
        for gi, (lo, n_bufs, copies, _) in enumerate(groups):
            for kk, (mask, src, dst, _) in enumerate(copies(buf_refs[lo:lo + n_bufs], me)):
                pltpu.make_async_remote_copy(src_ref=src, dst_ref=dst, send_sem=sem_refs[2 * gi].at[kk],
                                             recv_sem=sem_refs[2 * gi + 1].at[kk], device_id=_flip(me, mask),
                                             device_id_type=pl.DeviceIdType.MESH).start()
        token[...] = jnp.zeros_like(token)

    out_shape = []
    for _, _, _, n in groups:
        out_shape += [pltpu.SemaphoreType.DMA((n,)), pltpu.SemaphoreType.DMA((n,))]
    out_shape += [pltpu.HBM(b.shape, b.dtype) for b in bufs] + [jax.ShapeDtypeStruct((8, LANES_V7X), F32)]
    res = pl.pallas_call(
        body, name=name, out_shape=tuple(out_shape),
        in_specs=(HBM_SPEC,) * nb + (pl.BlockSpec(memory_space=pl.ANY),) * (n_in - nb),
        out_specs=(SEM_SPEC,) * (2 * ng) + (HBM_SPEC,) * nb + (pl.BlockSpec(memory_space=pltpu.VMEM),),
        input_output_aliases={i: 2 * ng + i for i in range(nb)},
        compiler_params=pltpu.CompilerParams(has_side_effects=SPLIT_EFFECT),
    )(*[pltpu.with_memory_space_constraint(b, pltpu.HBM) for b in bufs], *([] if after is None else [after]))
    sems = [(res[2 * gi], res[2 * gi + 1]) for gi in range(ng)]
    return sems, list(res[2 * ng:2 * ng + nb]), res[-1]


def _split_wait(name, sems, bufs, copies, after):
    nb = len(bufs)

    def body(*refs):
        buf_refs = refs[:nb]
        send_sems, recv_sems = refs[nb], refs[nb + 1]
        me = (lax.axis_index("x"), lax.axis_index("y"), lax.axis_index("c"))
        for kk, (mask, _, _, landing) in enumerate(copies(buf_refs, me)):
            cp = pltpu.make_async_remote_copy(src_ref=landing, dst_ref=landing, send_sem=send_sems.at[kk],
                                              recv_sem=recv_sems.at[kk], device_id=_flip(me, mask),
                                              device_id_type=pl.DeviceIdType.MESH)
            cp.wait_send()
            cp.wait_recv()

    return list(pl.pallas_call(
        body, name=name, out_shape=tuple(pltpu.HBM(b.shape, b.dtype) for b in bufs),
        in_specs=(HBM_SPEC,) * nb + (SEM_SPEC, SEM_SPEC, pl.BlockSpec(memory_space=pl.ANY)),
        out_specs=(HBM_SPEC,) * nb,
        input_output_aliases={i: i for i in range(nb)},
        compiler_params=pltpu.CompilerParams(has_side_effects=SPLIT_EFFECT),
    )(*bufs, sems[0], sems[1], after))


N_REDUCE_PIECES = 7


def _reduce_copies(metas):
    def copies(refs, me):
        x, y, c = me
        s_me = 2 * x + y
        out = []
        for m, g, land in zip(metas, refs[:len(metas)], refs[len(metas):]):
            for kk, (mask, bits) in enumerate(CHIP_FLIPS):
                s_peer = jnp.bitwise_xor(s_me, bits)
                out.append((mask, m.piece_of_full(g, s_peer, c), land.at[kk], land.at[kk]))
                out.append((mask[:2] + (1,), m.piece_of_full(g, s_peer, 1 - c), land.at[3 + kk], land.at[3 + kk]))
            out.append((FLIP_C, m.piece_of_full(g, s_me, 1 - c), land.at[6], land.at[6]))
        return out
    return copies


def _reduce_start(name, metas, grads):
    lands = [lax.empty((N_REDUCE_PIECES,) + m.piece_shape, BF16) for m in metas]
    bufs = list(grads) + lands
    sems, thru, token = _split_start(name, bufs, [(0, len(bufs), _reduce_copies(metas),
                                                   N_REDUCE_PIECES * len(metas))], None)
    return sems[0], thru, token


def _reduce_wait(name, metas, sems, thru, after):
    done = _split_wait(name, sems, thru, _reduce_copies(metas), after)
    return done[:len(metas)], done[len(metas):]


def _share_halves(metas, shards):
    def plan(in_refs, out_refs, me):
        c = me[2]
        phase = [(FLIP_C, m.half_of_shard(g, c), m.half_of_shard(g, c), m.half_of_shard(g, 1 - c))
                 for m, g in zip(metas, out_refs)]
        return [], [phase]

    outs = [jax.ShapeDtypeStruct(m.shard_shape, F32) for m in metas]
    return _comm("share_halves", list(shards), outs, plan, len(metas), 0,
                 aliases={i: i for i in range(len(metas))})


def _sum_pieces(meta, grad, landed, s_arr, c_arr):
    pr, pc = meta.piece_shape
    tr = _tile(pr, 256, 16)
    tc = _tile(pc, 2048)
    nr, ncol = pr // tr, pc // tc

    def body(s_ref, c_ref, p_ref, l_ref, o_ref):
        acc = p_ref[...].astype(F32)
        for kk in range(N_REDUCE_PIECES):
            acc = acc + l_ref[kk].astype(F32)
        o_ref[...] = acc

    if meta.kind == "col":
        p_map = lambda i, j, s_ref, c_ref: (i + c_ref[0] * nr, j + s_ref[0] * ncol)
        o_map = lambda i, j, s_ref, c_ref: (i + c_ref[0] * nr, j)
    else:
        p_map = lambda i, j, s_ref, c_ref: (i + s_ref[0] * nr, j + c_ref[0] * ncol)
        o_map = lambda i, j, s_ref, c_ref: (i, j + c_ref[0] * ncol)
    blk = (tr, tc)
    return pl.pallas_call(
        body, name="sum_pieces",
        grid_spec=pltpu.PrefetchScalarGridSpec(
            num_scalar_prefetch=2, grid=(nr, ncol),
            in_specs=[pl.BlockSpec(blk, p_map),
                      pl.BlockSpec((N_REDUCE_PIECES,) + blk, lambda i, j, s_ref, c_ref: (0, i, j))],
            out_specs=pl.BlockSpec(blk, o_map)),
        out_shape=jax.ShapeDtypeStruct(meta.shard_shape, F32),
        compiler_params=_params(("parallel", "parallel")),
    )(s_arr, c_arr, grad, landed)


def _adam_rows(rows, sel, fulls):
    w, g, m, v = rows
    m2 = ADAM_B1 * m + (1.0 - ADAM_B1) * g
    v2 = ADAM_B2 * v + (1.0 - ADAM_B2) * jnp.square(g)
    m_hat = m2 / (1.0 - ADAM_B1 ** ADAM_STEP)
    v_hat = v2 / (1.0 - ADAM_B2 ** ADAM_STEP)
    delta = -ADAM_LR * (m_hat / (jnp.sqrt(v_hat) + ADAM_EPS) + ADAM_WD * w)
    return [delta, m2, v2], []


def _adamw(w, g, m, v):
    r, c = w.shape
    tr = _tile(r, 128, 8)
    outs = _rowwise("adamw", _adam_rows, n_tiles=r // tr, tr=tr,
                    row_ins=[(w, 0, None), (g, 0, None), (m, 0, None), (v, 0, None)],
                    row_outs=[(r, c, F32, 0)] * 3)
    return outs[0], outs[1], outs[2]


def _ada_fwd(cg, w, b):
    d, n = w.shape
    tn = _tile(n, 512)

    def body(c_ref, w_ref, b_ref, o_ref):
        a = _silu(c_ref[...]).astype(BF16)
        o_ref[...] = jnp.dot(a, w_ref[...].astype(BF16), preferred_element_type=F32) + b_ref[...]

    return pl.pallas_call(
        body, name="ada_fwd", grid=(n // tn,),
        in_specs=[pl.BlockSpec(cg.shape, lambda j: (0, 0)), pl.BlockSpec((d, tn), lambda j: (0, j)),
                  pl.BlockSpec((1, tn), lambda j: (0, j))],
        out_specs=pl.BlockSpec((cg.shape[0], tn), lambda j: (0, j)),
        out_shape=jax.ShapeDtypeStruct((cg.shape[0], n), F32),
        compiler_params=_params(("parallel",)),
    )(cg, w, b)


def _ada_bwd(cg, dm, w):
    d, n = w.shape
    tn = _tile(n, 512)
    nj = n // tn

    def body(c_ref, dm_ref, w_ref, gw_ref, da_ref, acc):
        j = pl.program_id(0)

        @pl.when(j == 0)
        def _():
            acc[...] = jnp.zeros_like(acc)

        a = _silu(c_ref[...]).astype(BF16)
        dmv = dm_ref[...].astype(BF16)
        gw_ref[...] = lax.dot_general(a, dmv, TN_DIMS, preferred_element_type=F32)
        acc[...] += lax.dot_general(dmv, w_ref[...].astype(BF16), NT_DIMS, preferred_element_type=F32)

        @pl.when(j == nj - 1)
        def _():
            da_ref[...] = acc[...]

    return pl.pallas_call(
        body, name="ada_bwd", grid=(nj,),
        in_specs=[pl.BlockSpec(cg.shape, lambda j: (0, 0)), pl.BlockSpec((dm.shape[0], tn), lambda j: (0, j)),
                  pl.BlockSpec((d, tn), lambda j: (0, j))],
        out_specs=[pl.BlockSpec((d, tn), lambda j: (0, j)), pl.BlockSpec(cg.shape, lambda j: (0, 0))],
        out_shape=[jax.ShapeDtypeStruct((d, n), F32), jax.ShapeDtypeStruct(cg.shape, F32)],
        scratch_shapes=[pltpu.VMEM(cg.shape, F32)],
        compiler_params=_params(("arbitrary",)),
    )(cg, dm, w)


def _small_reduce(gathered, logits, n_mod_cols, lg_off, loss_off, loss_cols):
    npk = gathered.shape[1]

    def body(g_ref, lo_ref, tot_ref, gb_ref, gl_ref, loss_ref):
        acc = g_ref[0:1, :]
        for dd in range(1, N_DEV):
            acc = acc + g_ref[dd:dd + 1, :]
        tot_ref[...] = acc
        gb_ref[...] = acc[:, :n_mod_cols] + acc[:, n_mod_cols:2 * n_mod_cols]
        gl_ref[...] = acc[:, lg_off:lg_off + LANES_V7X] * _sigmoid(-lo_ref[...])
        loss = jnp.sum(acc[:, loss_off:loss_off + loss_cols], axis=1, keepdims=True)
        loss_ref[...] = jnp.broadcast_to(loss, loss_ref.shape)

    lane = jax.ShapeDtypeStruct((1, LANES_V7X), F32)
    return pl.pallas_call(
        body, name="small_reduce",
        out_shape=[jax.ShapeDtypeStruct((1, npk), F32), jax.ShapeDtypeStruct((1, n_mod_cols), F32), lane, lane],
    )(gathered, logits)


def _c_ctx_grad(parts, c_ctx):
    def body(p_ref, c_ref, o_ref):
        tot = p_ref[0:1, :] + p_ref[2:3, :] + p_ref[4:5, :] + p_ref[6:7, :]
        _, vjp = jax.vjp(_silu, c_ref[...])
        o_ref[...] = vjp(tot)[0]

    return pl.pallas_call(body, name="c_ctx_grad", out_shape=jax.ShapeDtypeStruct(c_ctx.shape, F32))(parts, c_ctx)


def _rope_tables(seq, ctx_rows):
    rows = seq // GRID_W
    half = HEAD_DIM // 2
    inv_freq = ROPE_THETA ** (-jnp.arange(0, half, 2, dtype=F32) / half)
    ang_row = jnp.arange(rows, dtype=F32)[:, None] * inv_freq
    ang_col = jnp.arange(GRID_W, dtype=F32)[:, None] * inv_freq

    def spread(fn):
        return jnp.concatenate([jnp.repeat(fn(ang_row), GRID_W, axis=0), jnp.tile(fn(ang_col), (rows, 1))], axis=-1)

    cos, sin = spread(jnp.cos), spread(jnp.sin)
    cos_full = jnp.repeat(cos, 2, axis=1)
    sin_signed = jnp.stack([-sin, sin], axis=-1).reshape(seq, HEAD_DIM)
    cos_full = jnp.concatenate([jnp.ones((ctx_rows, HEAD_DIM), F32), cos_full], axis=0)
    sin_signed = jnp.concatenate([jnp.zeros((ctx_rows, HEAD_DIM), F32), sin_signed], axis=0)
    return cos_full, sin_signed


def _qk_rot(p, gain, cos_full, sin_signed):
    r = _rmsn(p) * gain
    return r * cos_full + _swap_pairs(r) * sin_signed


def _qk_rot_bwd(g, p, gain, cos_full, sin_signed):
    g1 = g * cos_full + _swap_pairs(g * sin_signed)
    _, vjp = jax.vjp(lambda pp, gn: _rmsn(pp) * gn, p, gain)
    return vjp(g1)


def kernel(x, c, ctx, c_ctx, w_ada, b_ada, ffn1_w_in, ffn1_w_out, mix_w_in, attn_q_gain, attn_k_gain, ret_decay_logit, w_proj_attn, w_proj_ret, mix_w_out, ffn2_w_in, ffn2_w_out, final_norm, loss_target, m_c_ctx, m_w_ada, m_b_ada, m_ffn1_w_in, m_ffn1_w_out, m_mix_w_in, m_attn_q_gain, m_attn_k_gain, m_ret_decay_logit, m_w_proj_attn, m_w_proj_ret, m_mix_w_out, m_ffn2_w_in, m_ffn2_w_out, m_final_norm, v_c_ctx, v_w_ada, v_b_ada, v_ffn1_w_in, v_ffn1_w_out, v_mix_w_in, v_attn_q_gain, v_attn_k_gain, v_ret_decay_logit, v_w_proj_attn, v_w_proj_ret, v_mix_w_out, v_ffn2_w_in, v_ffn2_w_out, v_final_norm):
    xi, yi, ci = lax.axis_index("x"), lax.axis_index("y"), lax.axis_index("c")
    dev = 4 * xi + 2 * yi + ci
    s_me = 2 * xi + yi
    c_arr = jnp.reshape(ci, (1,)).astype(jnp.int32)
    s_arr = jnp.reshape(s_me, (1,)).astype(jnp.int32)

    t, d = x.shape[1], x.shape[2]
    tc = ctx.shape[1]
    tk = tc + t
    ff = ffn1_w_out.shape[1] * N_CHIP
    aw = w_proj_attn.shape[1]
    rw = w_proj_ret.shape[1]
    pw = mix_w_in.shape[2] * N_CHIP
    kvw = (pw - aw - 4 * rw - 2 * d) // 2
    groups = aw // kvw
    n_ret_heads = rw // HEAD_DIM
    mod_cols = N_MOD * d
    tr = _tile(tc, 256, 32)
    nt_all, nt_x, ctx_tiles = tk // tr, t // tr, tc // tr

    c_rows = _gather_row("gather_c", c)
    cg = jnp.concatenate([c_rows, c_ctx[None, :], jnp.zeros((7, d), F32)], axis=0)
    w_ada_l = w_ada[0]
    ada_cols = w_ada_l.shape[1]
    b_ada_l = lax.dynamic_slice_in_dim(b_ada, s_me * ada_cols, ada_cols, axis=1)
    mod_shard = _ada_fwd(cg, w_ada_l, b_ada_l)
    mod_g = _all_gather8("gather_mod", mod_shard)
    mod_full = jnp.concatenate([mod_g[0], mod_g[2], mod_g[4], mod_g[6]], axis=1)
    mod_x = lax.dynamic_slice_in_dim(mod_full, dev, 1, axis=0).reshape(N_MOD, d)
    mod_c = mod_full[8].reshape(N_MOD, d)
    mods = jnp.stack([mod_c, mod_x])

    big = [("col", ffn1_w_in), ("row", ffn1_w_out), ("col", mix_w_in), ("col", w_proj_attn), ("col", w_proj_ret),
           ("row", mix_w_out), ("col", ffn2_w_in), ("row", ffn2_w_out)]
    metas = []
    for kind, w in big:
        r_l, c_l = w.shape[1:]
        metas.append(_Sharded(kind, r_l, c_l * N_CHIP) if kind == "col" else _Sharded(kind, r_l * N_CHIP, c_l))
    layer_groups = ((0, 1), (1, 2), (2, 6), (6, 8))
    sems_first, placed_first, token = _split_start(
        "gather_start_first", [_place_shard(metas[0], big[0][1][0], s_arr)],
        [(0, 1, _gather_copies(metas[0:1], True), 3)], mods)
    placed_rest = [_place_shard(m, w[0], s_arr, after=token) for m, (_, w) in zip(metas[1:], big[1:])]
    sems_rest, placed_rest, token = _split_start(
        "gather_start_rest", placed_rest,
        [(lo - 1, hi - lo, _gather_copies(metas[lo:hi], True), 3 * (hi - lo)) for lo, hi in layer_groups[1:]], None)
    gather_sems, placed = sems_first + sems_rest, placed_first + placed_rest
    mods = mods + token[0, 0]

    def weights_of(gi, after):
        lo, hi = layer_groups[gi]
        arrived = _split_wait("gather_wait_%d" % gi, gather_sems[gi], placed[lo:hi],
                              _gather_copies(metas[lo:hi], True), after)
        return _gather_forward("gather_forward_%d" % gi, metas[lo:hi], arrived)

    def weights_early(gi, after):
        lo, hi = layer_groups[gi]
        arrived = _split_wait("gather_wait_%d" % gi, gather_sems[gi], placed[lo:hi],
                              _gather_copies(metas[lo:hi], True), after)
        sems, thru, tok = _split_start("gather_forward_start_%d" % gi, arrived,
                                       [(0, hi - lo, _gather_copies(metas[lo:hi], False), 3 * (hi - lo))], None)
        return gi, sems[0], thru, tok

    def weights_late(early, after):
        gi, sems, thru, _ = early
        lo, hi = layer_groups[gi]
        return _split_wait("gather_forward_wait_%d" % gi, sems, thru, _gather_copies(metas[lo:hi], False), after)

    cos_full, sin_signed = _rope_tables(t, tc)
    q_gain, k_gain = attn_q_gain, attn_k_gain
    log_gamma = jax.nn.log_sigmoid(ret_decay_logit[0])
    lgv = jnp.broadcast_to(log_gamma[:, :, None, None], (2, n_ret_heads, 1, HEAD_DIM))

    def stream_rows(h, off):
        if isinstance(h, tuple):
            return [(h[0], "ctx", None), (h[1], -ctx_tiles, None)]
        return [(h, off, None)]

    def stream_value(h, rows, fulls):
        if isinstance(h, tuple):
            return jnp.where(fulls[-1], rows[0], rows[1]), rows[2:]
        return rows[0], rows[1:]

    def norm_mod(name, h, n_tiles, off, i_shift, i_scale):
        def fn(rows, sel, fulls):
            hv, _ = stream_value(h, rows, fulls)
            return [_rmsn(hv) * (1.0 + sel(i_scale)) + sel(i_shift)], []
        return _rowwise(name, fn, n_tiles=n_tiles, tr=tr, row_ins=stream_rows(h, 0),
                        row_outs=[(n_tiles * tr, d, BF16, 0)], sel_in=mods, sel_off=off, ctx_rows=tc)[0]

    def resid_norm(name, h, h_off, f, n_tiles, off, i_gate, coef, i_shift, i_scale):
        def fn(rows, sel, fulls):
            hv, rest = stream_value(h, rows, fulls)
            hn = hv + coef * sel(i_gate) * rest[0]
            return [hn, _rmsn(hn) * (1.0 + sel(i_scale)) + sel(i_shift)], []
        return _rowwise(name, fn, n_tiles=n_tiles, tr=tr, row_ins=stream_rows(h, h_off) + [(f, 0, None)],
                        row_outs=[(f.shape[0], d, F32, 0), (f.shape[0], d, BF16, 0)], sel_in=mods, sel_off=off,
                        ctx_rows=tc)

    h0 = (ctx[0], x[0])
    n1 = norm_mod("norm_mod1", h0, nt_all, 0, 0, 1)
    w1i, = weights_of(0, n1)
    hm1, ua1, ub1 = _mm_swiglu("ffn1_in", n1, w1i)
    w1o, = weights_of(1, hm1)
    f1 = _mm("ffn1_out", hm1, w1o, "nn", BF16)
    mixer_weights = weights_early(2, f1)
    mods = mods + mixer_weights[3][0, 0]
    h1, n2 = resid_norm("resid_norm1", h0, 0, f1, nt_all, 0, 2, 0.5, 3, 4)
    wmi, wpa, wpr, wmo = weights_late(mixer_weights, n2)
    p_q = _mm("mix_in_q", n2, wmi, "nn", F32, 0, aw)
    p_kv = _mm("mix_in_kv", n2, wmi, "nn", F32, aw, 2 * kvw)
    p_r = _mm("mix_in_ret", n2, wmi, "nn", F32, aw + 2 * kvw, 3 * rw)
    p_gr = _mm("mix_in_gr", n2, wmi, "nn", BF16, aw + 2 * kvw + 3 * rw, rw)
    p_gab = _mm("mix_in_gab", n2, wmi, "nn", BF16, aw + 2 * kvw + 4 * rw, 2 * d)
    ffn2_weights = weights_early(3, p_gab)
    q_gain = q_gain + ffn2_weights[3][0, 0]

    def q_prep(rows, sel, fulls):
        p, cf, ss = rows
        return _heads_map(lambda ph: [_qk_rot(ph, fulls[0], cf, ss) * QSCALE], [p], aw), []

    q_rot = _rowwise("q_prep", q_prep, n_tiles=nt_x, tr=tr,
                     row_ins=[(p_q, ctx_tiles, None), (cos_full, ctx_tiles, None), (sin_signed, ctx_tiles, None)],
                     row_outs=[(t, aw, BF16, 0)], full_ins=[q_gain])[0]

    def kv_prep(rows, sel, fulls):
        p, cf, ss = rows
        k_rot = _heads_map(lambda ph: [_qk_rot(ph, fulls[0], cf, ss)], [p[:, :kvw]], kvw)[0]
        v_ones = _heads_map(lambda vh: [jnp.concatenate([vh, jnp.ones_like(vh)], axis=1)], [p[:, kvw:]], kvw)[0]
        return [k_rot, v_ones], []

    k_rot, v_att = _rowwise("kv_prep", kv_prep, n_tiles=nt_all, tr=tr,
                            row_ins=[(p_kv, 0, None), (cos_full, 0, None), (sin_signed, 0, None)],
                            row_outs=[(tk, kvw, BF16, 0), (tk, 2 * kvw, BF16, 0)], full_ins=[k_gain])

    ya, lse = _flash_fwd(q_rot, k_rot, v_att, groups)
    y_fwd, y_bwd, states = _ret_fwd(p_r, lgv, tc)

    def ret_out_fn(yf, yb, gr):
        return [_silu(gr) * _rmsn(yf + yb)]

    def ret_out(rows, sel, fulls):
        return _heads_map(ret_out_fn, rows, rw), []

    y_rows = [(y_fwd, ctx_tiles, None), (y_bwd, ctx_tiles, None), (p_gr, ctx_tiles, None)]
    yr = _rowwise("ret_out", ret_out, n_tiles=nt_x, tr=tr, row_ins=y_rows, row_outs=[(t, rw, BF16, 0)])[0]

    pa = _mm("proj_attn", ya, wpa, "nn", BF16)
    prj = _mm("proj_ret", yr, wpr, "nn", BF16)

    def merge_fn(a, r, ga, gb):
        return _sigmoid(ga) * a + _sigmoid(gb) * r

    gate_rows = [(p_gab, ctx_tiles, (d, 0)), (p_gab, ctx_tiles, (d, 1))]
    z = _rowwise("merge", lambda rows, sel, fulls: ([merge_fn(*rows)], []), n_tiles=nt_x, tr=tr,
                 row_ins=[(pa, 0, None), (prj, 0, None)] + gate_rows, row_outs=[(t, d, BF16, 0)])[0]
    fo = _mm("mix_out", z, wmo, "nn", BF16)
    h2, n3 = resid_norm("resid_norm2", h1, ctx_tiles, fo, nt_x, ctx_tiles, 5, 1.0, 6, 7)
    w2i, w2o = weights_late(ffn2_weights, n3)
    hm2, ua2, ub2 = _mm_swiglu("ffn2_in", n3, w2i)
    f2 = _mm("ffn2_out", hm2, w2o, "nn", BF16)

    def loss_fn(rows, sel, fulls):
        h2v, f2v, tgt = rows
        g3 = 0.5 * sel(8)
        y, vjp = jax.vjp(lambda hh, ww: _rmsn(hh) * ww, h2v + g3 * f2v, fulls[0])
        err = y - tgt
        dh, dw = vjp(err / d)
        return [dh, g3 * dh], [0.5 / d * jnp.sum(err * err, axis=0, keepdims=True), dw,
                               jnp.sum(0.5 * dh * f2v, axis=0, keepdims=True)]

    dh3, df2, loss_acc = _rowwise("loss_head", loss_fn, n_tiles=nt_x, tr=tr,
                                  row_ins=[(h2, 0, None), (f2, 0, None), (loss_target[0], 0, None)],
                                  row_outs=[(t, d, F32, 0), (t, d, BF16, 0)], sel_in=mods, sel_off=ctx_tiles,
                                  ctx_rows=tc, full_ins=[final_norm[None, :]], acc_shape=(8, d))
    loss_cols, g_final, dg3 = loss_acc[1, 0:1], loss_acc[1, 1:2], loss_acc[1, 2:3]

    def swiglu_bwd(name, dhm, ua, ub):
        rows_n = dhm.shape[0]
        tr_w = _tile(tr, 128, 32)

        def fn(rows, sel, fulls):
            g, a, b = rows
            _, vjp = jax.vjp(lambda aa, bb: _silu(aa) * bb, a.astype(F32), b.astype(F32))
            da, db = vjp(g)
            return [jnp.concatenate([da, db], axis=1)], []
        return _rowwise(name, fn, n_tiles=rows_n // tr_w, tr=tr_w,
                        row_ins=[(dhm, 0, None), (ua, 0, None), (ub, 0, None)],
                        row_outs=[(rows_n, 2 * ff, BF16, 0)])[0]

    def norm_mod_bwd(name, dn, h, dres, dres_off, n_tiles, off, i_shift, i_scale, gate=None, out_off=0):
        def fn(rows, sel, fulls):
            hh, rows = stream_value(h, rows, fulls)
            g, dr = rows[:2]
            if dres_off < 0:
                dr = jnp.where(fulls[-1], 0.0, dr)
            _, vjp = jax.vjp(lambda a, sh, sc: _rmsn(a) * (1.0 + sc) + sh, hh,
                             sel(i_shift), sel(i_scale))
            dhh, dsh, dsc = vjp(g)
            dh = dr + dhh
            if gate is None:
                return [dh], [dsh, dsc]
            return [dh, gate[2] * sel(gate[1]) * dh], [dsh, dsc, jnp.sum(gate[2] * dh * rows[2], axis=0, keepdims=True)]
        n_rows = dn.shape[0] + out_off * tr
        row_ins = stream_rows(h, 0) + [(dn, 0, None), (dres, dres_off, None)]
        row_outs = [(n_rows, d, F32, out_off)]
        if gate is not None:
            row_ins.append((gate[0], 0, None))
            row_outs.append((n_rows, d, BF16, out_off))
        return _rowwise(name, fn, n_tiles=n_tiles, tr=tr, row_ins=row_ins, row_outs=row_outs, sel_in=mods,
                        sel_off=off, ctx_rows=tc, acc_shape=(8, d))

    g_w2o = _mm("ffn2_out_dw", hm2, df2, "tn", BF16)
    dhm2 = _mm("ffn2_out_dx", df2, w2o, "nt", F32)
    du2 = swiglu_bwd("swiglu_bwd2", dhm2, ua2, ub2)
    g_w2i = _mm("ffn2_in_dw", n3, du2, "tn", BF16)
    dn3 = _mm("ffn2_in_dx", du2, w2i, "nt", F32)
    dh2, dfo, acc_n3 = norm_mod_bwd("norm_mod_bwd3", dn3, h2, dh3, 0, nt_x, ctx_tiles, 6, 7, gate=(fo, 5, 1.0))

    sems_ffn2, thru_ffn2, token = _reduce_start("reduce_start_ffn2", metas[6:8], [g_w2i, g_w2o])

    g_wmo = _mm("mix_out_dw", z, dfo, "tn", BF16, after=token)
    dz = _mm("mix_out_dx", dfo, wmo, "nt", F32)

    def merge_bwd(rows, sel, fulls):
        g, a, r, ga, gb = rows
        _, vjp = jax.vjp(merge_fn, a, r, ga, gb)
        da, dr, dga, dgb = vjp(g)
        return [da, dr, jnp.concatenate([dga, dgb], axis=1)], []

    dpa, dpr, dgab = _rowwise("merge_bwd", merge_bwd, n_tiles=nt_x, tr=tr,
                              row_ins=[(dz, 0, None), (pa, 0, None), (prj, 0, None)] + gate_rows,
                              row_outs=[(t, d, BF16, 0), (t, d, BF16, 0), (t, 2 * d, BF16, 0)])
    g_wpa = _mm("proj_attn_dw", ya, dpa, "tn", BF16)
    dya = _mm("proj_attn_dx", dpa, wpa, "nt", BF16)
    g_wpr = _mm("proj_ret_dw", yr, dpr, "tn", BF16)
    dyr = _mm("proj_ret_dx", dpr, wpr, "nt", F32)

    def ret_out_bwd(rows, sel, fulls):
        def per_head(g, yf, yb, gr):
            _, vjp = jax.vjp(lambda yy, gg: ret_out_fn(yy, 0.0, gg)[0], yf + yb, gr)
            return list(vjp(g))
        dy, dgr = _heads_map(per_head, rows, rw)
        return [dy, dgr], []

    dy_ret, dgr = _rowwise("ret_out_bwd", ret_out_bwd, n_tiles=nt_x, tr=tr, row_ins=[(dyr, 0, None)] + y_rows,
                           row_outs=[(t, rw, F32, 0), (t, rw, BF16, 0)])
    dp_rf, dp_rb, dlg = _ret_bwd(p_r, states, dy_ret, lgv, tc)
    dp_r = _rowwise("ret_bwd_sum", lambda rows, sel, fulls: ([rows[0] + rows[1]], []), n_tiles=nt_all, tr=tr,
                    row_ins=[(dp_rf, 0, None), (dp_rb, 0, None)], row_outs=[(tk, 3 * rw, BF16, 0)])[0]

    dq_rot, dk_rot, dv_att = _flash_bwd(q_rot, k_rot, v_att, ya, dya, lse, groups)

    def q_prep_bwd(rows, sel, fulls):
        g, p, cf, ss = rows
        gain_acc = []

        def per_head(gh, ph):
            dp, dgain = _qk_rot_bwd(gh * HEAD_DIM ** -0.5, ph, fulls[0], cf, ss)
            gain_acc.append(dgain)
            return [dp]
        dp = _heads_map(per_head, [g, p], aw)[0]
        return [dp], [functools.reduce(lambda a, b: a + b, gain_acc)]

    dp_q, acc_gq = _rowwise("q_prep_bwd", q_prep_bwd, n_tiles=nt_x, tr=tr,
                            row_ins=[(dq_rot, 0, None), (p_q, ctx_tiles, None), (cos_full, ctx_tiles, None),
                                     (sin_signed, ctx_tiles, None)],
                            row_outs=[(t, aw, BF16, 0)], full_ins=[q_gain], acc_shape=(8, HEAD_DIM),
                            sel_off=ctx_tiles, ctx_rows=tc)

    def kv_prep_bwd(rows, sel, fulls):
        gk, gv, p, cf, ss = rows
        gain_acc = []

        def per_head(gh, ph):
            dp, dgain = _qk_rot_bwd(gh, ph, fulls[0], cf, ss)
            gain_acc.append(dgain)
            return [dp]
        dpk = _heads_map(per_head, [gk, p], kvw)[0]
        return [jnp.concatenate([dpk, gv], axis=1)], [functools.reduce(lambda a, b: a + b, gain_acc)]

    dp_kv, acc_gk = _rowwise("kv_prep_bwd", kv_prep_bwd, n_tiles=nt_all, tr=tr,
                             row_ins=[(dk_rot, 0, None), (dv_att, 0, None), (p_kv, 0, (kvw, 0)), (cos_full, 0, None),
                                      (sin_signed, 0, None)],
                             row_outs=[(tk, 2 * kvw, BF16, 0)], full_ins=[k_gain], acc_shape=(8, HEAD_DIM),
                             sel_off=0, ctx_rows=tc)

    def with_ctx_zeros(a):
        return jnp.concatenate([jnp.zeros((tc, a.shape[1]), a.dtype), a], axis=0)

    dp = jnp.concatenate([with_ctx_zeros(dp_q), dp_kv, dp_r, with_ctx_zeros(dgr), with_ctx_zeros(dgab)], axis=1)
    g_wmi = _mm("mix_in_dw", n2, dp, "tn", BF16)
    dn2 = _mm("mix_in_dx", dp, wmi, "nt", F32)
    dh1, df1, acc_n2 = norm_mod_bwd("norm_mod_bwd2", dn2, h1, dh2, -ctx_tiles, nt_all, 0, 3, 4, gate=(f1, 2, 0.5))
    sems_mix, thru_mix, token = _reduce_start("reduce_start_mix", metas[2:6], [g_wmi, g_wpa, g_wpr, g_wmo])

    g_w1o = _mm("ffn1_out_dw", hm1, df1, "tn", BF16, after=token)
    sems_w1o, thru_w1o, token = _reduce_start("reduce_start_ffn1_out", metas[1:2], [g_w1o])
    dhm1 = _mm("ffn1_out_dx", df1, w1o, "nt", F32, after=token)
    du1 = swiglu_bwd("swiglu_bwd1", dhm1, ua1, ub1)
    g_w1i = _mm("ffn1_in_dw", n1, du1, "tn", BF16)
    sems_w1i, thru_w1i, token = _reduce_start("reduce_start_ffn1_in", metas[0:1], [g_w1i])
    dn1 = _mm("ffn1_in_dx", du1, w1i, "nt", F32, after=token)
    dh0, acc_n1 = norm_mod_bwd("norm_mod_bwd1", dn1, h0, dh1, 0, nt_all, 0, 0, 1, out_off=-ctx_tiles)
    grad_x = dh0[None]

    grads_own, landed = [], []
    for name, lo, hi, sems_l, thru_l in (("reduce_wait_ffn1_in", 0, 1, sems_w1i, thru_w1i),
                                         ("reduce_wait_ffn1_out", 1, 2, sems_w1o, thru_w1o),
                                         ("reduce_wait_mix", 2, 6, sems_mix, thru_mix),
                                         ("reduce_wait_ffn2", 6, 8, sems_ffn2, thru_ffn2)):
        grads_l, landed_l = _reduce_wait(name, metas[lo:hi], sems_l, thru_l, dh0)
        grads_own += grads_l
        landed += landed_l
    pieces = [_sum_pieces(m, g, l, s_arr, c_arr) for m, g, l in zip(metas, grads_own, landed)]
    grads_big = _share_halves(metas, pieces)

    zero_row = jnp.zeros((1, d), F32)
    dmod_x = jnp.concatenate([acc_n1[1, 0:1], acc_n1[1, 1:2], acc_n2[1, 2:3], acc_n2[1, 0:1], acc_n2[1, 1:2],
                              acc_n3[1, 2:3], acc_n3[1, 0:1], acc_n3[1, 1:2], dg3], axis=1)
    dmod_c = jnp.concatenate([acc_n1[0, 0:1], acc_n1[0, 1:2], acc_n2[0, 2:3], acc_n2[0, 0:1], acc_n2[0, 1:2]]
                             + [zero_row] * 4, axis=1)
    dlg_row = jnp.pad(dlg[:, :, 0, 0].reshape(1, 2 * n_ret_heads), ((0, 0), (0, LANES_V7X - 2 * n_ret_heads)))
    packed = jnp.concatenate([dmod_x, dmod_c, acc_gq[1, 0:1], acc_gk[0, 0:1] + acc_gk[1, 0:1], dlg_row,
                              g_final, loss_cols], axis=1)
    off_gq = 2 * mod_cols
    off_gk = off_gq + LANES_V7X
    off_lg = off_gk + LANES_V7X
    off_fn = off_lg + LANES_V7X
    off_loss = off_fn + d
    gathered = _gather_row("gather_small", packed)
    logits_row = jnp.pad(ret_decay_logit.reshape(1, 2 * n_ret_heads), ((0, 0), (0, LANES_V7X - 2 * n_ret_heads)))
    totals, g_b_ada, g_decay, loss_row = _small_reduce(gathered, logits_row, mod_cols, off_lg, off_loss, d)
    loss = loss_row[0, 0]

    dm = jnp.concatenate([gathered[:, :mod_cols], totals[:, mod_cols:2 * mod_cols],
                          jnp.zeros((7, mod_cols), F32)], axis=0)
    dm_l = lax.dynamic_slice_in_dim(dm, s_me * ada_cols, ada_cols, axis=1)
    g_w_ada, da_part = _ada_bwd(cg, dm_l, w_ada_l)
    da_rows = _gather_row("gather_dc", da_part[8:9])
    g_c_ctx = _c_ctx_grad(da_rows, c_ctx[None, :])

    def as2d(a):
        return a.reshape(-1, a.shape[-1])

    grads = {
        "c_ctx": g_c_ctx, "w_ada": g_w_ada, "b_ada": g_b_ada,
        "ffn1_w_in": grads_big[0], "ffn1_w_out": grads_big[1], "mix_w_in": grads_big[2],
        "attn_q_gain": totals[:, off_gq:off_gq + HEAD_DIM], "attn_k_gain": totals[:, off_gk:off_gk + HEAD_DIM],
        "ret_decay_logit": g_decay[:, :2 * n_ret_heads],
        "w_proj_attn": grads_big[3], "w_proj_ret": grads_big[4], "mix_w_out": grads_big[5],
        "ffn2_w_in": grads_big[6], "ffn2_w_out": grads_big[7], "final_norm": totals[:, off_fn:off_fn + d],
    }
    weights = {"c_ctx": (c_ctx, m_c_ctx, v_c_ctx), "w_ada": (w_ada, m_w_ada, v_w_ada),
               "b_ada": (b_ada, m_b_ada, v_b_ada), "ffn1_w_in": (ffn1_w_in, m_ffn1_w_in, v_ffn1_w_in),
               "ffn1_w_out": (ffn1_w_out, m_ffn1_w_out, v_ffn1_w_out), "mix_w_in": (mix_w_in, m_mix_w_in, v_mix_w_in),
               "attn_q_gain": (attn_q_gain, m_attn_q_gain, v_attn_q_gain),
               "attn_k_gain": (attn_k_gain, m_attn_k_gain, v_attn_k_gain),
               "ret_decay_logit": (ret_decay_logit, m_ret_decay_logit, v_ret_decay_logit),
               "w_proj_attn": (w_proj_attn, m_w_proj_attn, v_w_proj_attn),
               "w_proj_ret": (w_proj_ret, m_w_proj_ret, v_w_proj_ret), "mix_w_out": (mix_w_out, m_mix_w_out, v_mix_w_out),
               "ffn2_w_in": (ffn2_w_in, m_ffn2_w_in, v_ffn2_w_in), "ffn2_w_out": (ffn2_w_out, m_ffn2_w_out, v_ffn2_w_out),
               "final_norm": (final_norm, m_final_norm, v_final_norm)}
    out_g, out_d, out_m, out_v = [], [], [], []
    for name, (w, m, v) in weights.items():
        shape = w.shape
        if name == "ret_decay_logit":
            w2, m2, v2 = (a.reshape(1, -1) for a in (w, m, v))
        else:
            w2, m2, v2 = as2d(w), as2d(m), as2d(v)
        g2 = grads[name].reshape(w2.shape)
        delta, new_m, new_v = _adamw(w2, g2, m2, v2)
        out_g.append(g2.reshape(shape))
        out_d.append(delta.reshape(shape))
        out_m.append(new_m.reshape(shape))
        out_v.append(new_v.reshape(shape))
    return (loss, grad_x, *out_g, *out_d, *out_m, *out_v)
```

```python
import functools
import math

import jax
import jax.numpy as jnp
from jax import lax
from jax.experimental import pallas as pl
from jax.experimental.pallas import tpu as pltpu

F32 = jnp.float32
BF16 = jnp.bfloat16

HEAD_DIM = 128
GRID_W = 64
ROPE_THETA = 10000.0
NORM_EPS = 1e-6
N_MOD = 9
RET_CHUNK = 128
ADAM_LR = 0.001
ADAM_B1 = 0.9
ADAM_B2 = 0.999
ADAM_EPS = 1e-08
ADAM_WD = 0.01
ADAM_STEP = 10

N_DEV = 8
N_CHIP = 4
LANES_V7X = 128
MXU_WIDTH_V7X = 256
VMEM_LIMIT_V7X = 52 * 1024 * 1024

NT_DIMS = (((1,), (1,)), ((), ()))
TN_DIMS = (((0,), (0,)), ((), ()))
NN_DIMS = (((1,), (0,)), ((), ()))


def _tile(n, pref, mult=LANES_V7X):
    if n <= pref:
        return n
    t = (pref // mult) * mult
    while t >= mult:
        if n % t == 0:
            return t
        t -= mult
    return n


def _params(sem):
    return pltpu.CompilerParams(dimension_semantics=sem, vmem_limit_bytes=VMEM_LIMIT_V7X)


def _sigmoid(x):
    return 1.0 / (1.0 + jnp.exp(-x))


def _silu(x):
    return x * _sigmoid(x)


def _rmsn(x):
    return x * lax.rsqrt(jnp.mean(x * x, axis=-1, keepdims=True) + NORM_EPS)


MM_VMEM_BUDGET = 44 * 1024 * 1024


def _divisor_tiles(n, cap):
    ts = [t for t in range(LANES_V7X, min(n, cap) + 1, LANES_V7X) if n % t == 0]
    return ts or [n]


def _mm_tiles(m, n, tk, out_bytes, has_acc):
    best = None
    for tm in _divisor_tiles(m, 1536):
        for tn in _divisor_tiles(n, 2560):
            need = 4 * tk * (tm + tn) + 2 * tm * tn * out_bytes + 4 * tm * tn
            if need > MM_VMEM_BUDGET:
                continue
            score = tm * tn / (tm + tn)
            for tdim in (tm, tn):
                if tdim % MXU_WIDTH_V7X:
                    score *= 0.85
            if best is None or score > best[0]:
                best = (score, tm, tn)
    return best[1], best[2]


def _mm(name, a, b, mode, out_dtype, b_off=0, n=None, after=None, stacked=False):
    half = 0
    if mode == "nn":
        m, k = a.shape
        n = b.shape[1] if n is None else n
        dims = NN_DIMS
    elif mode == "nt":
        if stacked:
            _, m, half = a.shape
            k = 2 * half
        else:
            m, k = a.shape
        n = b.shape[0]
        dims = NT_DIMS
    else:
        k, m = a.shape
        if stacked:
            half = b.shape[2]
            n = 2 * half
        else:
            n = b.shape[1]
        dims = TN_DIMS
    tk = _tile(half if (stacked and mode == "nt") else k, 2816)
    nk = k // tk
    n_tiled = half if (stacked and mode == "tn") else (math.gcd(n, b_off) if b_off else n)
    tm, tn = _mm_tiles(m, n_tiled, tk, jnp.dtype(out_dtype).itemsize, nk > 1)
    joff = b_off // tn
    per_half = (half // tk) if mode == "nt" else (half // tn)

    def body(a_ref, b_ref, *rest):
        o_ref = rest[0 if after is None else 1]
        if nk == 1:
            o_ref[...] = lax.dot_general(a_ref[...], b_ref[...], dims,
                                         preferred_element_type=F32).astype(o_ref.dtype)
            return
        acc_ref = rest[-1]
        kk = pl.program_id(2)

        @pl.when(kk == 0)
        def _():
            acc_ref[...] = jnp.zeros_like(acc_ref)

        acc_ref[...] += lax.dot_general(a_ref[...], b_ref[...], dims, preferred_element_type=F32)

        @pl.when(kk == nk - 1)
        def _():
            o_ref[...] = acc_ref[...].astype(o_ref.dtype)

    if mode == "nn":
        a_spec = pl.BlockSpec((tm, tk), lambda i, j, kk: (i, kk))
        b_spec = pl.BlockSpec((tk, tn), lambda i, j, kk: (kk, j + joff))
    elif mode == "nt":
        a_spec = pl.BlockSpec((tm, tk), lambda i, j, kk: (i, kk))
        if stacked:
            a_spec = pl.BlockSpec((None, tm, tk), lambda i, j, kk: (kk // per_half, i, kk % per_half))
        b_spec = pl.BlockSpec((tn, tk), lambda i, j, kk: (j, kk))
    else:
        a_spec = pl.BlockSpec((tk, tm), lambda i, j, kk: (kk, i))
        b_spec = pl.BlockSpec((tk, tn), lambda i, j, kk: (kk, j))
        if stacked:
            b_spec = pl.BlockSpec((None, tk, tn), lambda i, j, kk: (j // per_half, kk, j % per_half))
    return pl.pallas_call(
        body, name=name, grid=(m // tm, n // tn, nk),
        in_specs=[a_spec, b_spec] + ([] if after is None else [pl.BlockSpec(memory_space=pl.ANY)]),
        out_specs=pl.BlockSpec((tm, tn), lambda i, j, kk: (i, j)),
        out_shape=jax.ShapeDtypeStruct((m, n), out_dtype),
        scratch_shapes=[pltpu.VMEM((tm, tn), F32)] if nk > 1 else [],
        compiler_params=_params(("parallel", "parallel", "arbitrary")),
    )(*((a, b) if after is None else (a, b, after)))


def _mm_swiglu_bwd(name, dy, w_out, ua, ub, after=None):
    m, d_model = dy.shape
    f = w_out.shape[0]
    tm = _tile(m, 1024)
    tn = _tile(f, 512)

    def body(dy_ref, w_ref, ua_ref, ub_ref, *rest):
        du_ref = rest[-1]
        g = lax.dot_general(dy_ref[...], w_ref[...], NT_DIMS, preferred_element_type=F32)
        _, vjp = jax.vjp(lambda aa, bb: _silu(aa) * bb, ua_ref[...].astype(F32), ub_ref[...].astype(F32))
        da, db = vjp(g)
        du_ref[0] = da.astype(BF16)
        du_ref[1] = db.astype(BF16)

    tile = pl.BlockSpec((tm, tn), lambda i, j: (i, j))
    return pl.pallas_call(
        body, name=name, grid=(m // tm, f // tn),
        in_specs=[pl.BlockSpec((tm, d_model), lambda i, j: (i, 0)), pl.BlockSpec((tn, d_model), lambda i, j: (j, 0)),
                  tile, tile] + ([] if after is None else [pl.BlockSpec(memory_space=pl.ANY)]),
        out_specs=pl.BlockSpec((2, tm, tn), lambda i, j: (0, i, j)),
        out_shape=jax.ShapeDtypeStruct((2, m, f), BF16),
        compiler_params=_params(("parallel", "parallel")),
    )(*((dy, w_out, ua, ub) if after is None else (dy, w_out, ua, ub, after)))


def _mm_swiglu(name, a, w):
    m, k = a.shape
    f = w.shape[1] // 2
    tm = _tile(m, 1024)
    tn = _tile(f, 512)
    tk = _tile(k, 2560)
    nk = k // tk
    jf = f // tn

    def body(a_ref, wa_ref, wb_ref, h_ref, ua_ref, ub_ref, acca, accb):
        kk = pl.program_id(2)

        @pl.when(kk == 0)
        def _():
            acca[...] = jnp.zeros_like(acca)
            accb[...] = jnp.zeros_like(accb)

        av = a_ref[...]
        acca[...] += jnp.dot(av, wa_ref[...], preferred_element_type=F32)
        accb[...] += jnp.dot(av, wb_ref[...], preferred_element_type=F32)

        @pl.when(kk == nk - 1)
        def _():
            ua = acca[...]
            ub = accb[...]
            h_ref[...] = (_silu(ua) * ub).astype(BF16)
            ua_ref[...] = ua.astype(BF16)
            ub_ref[...] = ub.astype(BF16)

    o_spec = pl.BlockSpec((tm, tn), lambda i, j, kk: (i, j))
    o_shape = jax.ShapeDtypeStruct((m, f), BF16)
    return pl.pallas_call(
        body, name=name, grid=(m // tm, jf, nk),
        in_specs=[pl.BlockSpec((tm, tk), lambda i, j, kk: (i, kk)),
                  pl.BlockSpec((tk, tn), lambda i, j, kk: (kk, j)),
                  pl.BlockSpec((tk, tn), lambda i, j, kk: (kk, j + jf))],
        out_specs=[o_spec, o_spec, o_spec],
        out_shape=[o_shape, o_shape, o_shape],
        scratch_shapes=[pltpu.VMEM((tm, tn), F32), pltpu.VMEM((tm, tn), F32)],
        compiler_params=_params(("parallel", "parallel", "arbitrary")),
    )(a, w, w)


def _rowwise(name, fn, *, n_tiles, tr, row_ins, row_outs, sel_in=None, sel_off=0, ctx_rows=0,
             full_ins=(), acc_shape=None):
    sr = 128 if tr % 128 == 0 else (32 if tr % 32 == 0 else tr)
    n_row, n_full, n_out = len(row_ins), len(full_ins), len(row_outs)
    has_sel = sel_in is not None
    has_acc = acc_shape is not None

    def sel_of(i):
        return jnp.where((i + sel_off) * tr < ctx_rows, 0, 1)

    def body(*refs):
        row_refs = refs[:n_row]
        pos = n_row
        sel_ref = None
        if has_sel:
            sel_ref = refs[pos]
            pos += 1
        full_refs = refs[pos:pos + n_full]
        pos += n_full
        out_refs = refs[pos:pos + n_out]
        pos += n_out
        acc_ref = refs[pos] if has_acc else None
        i = pl.program_id(0)
        if has_acc:
            first = (i == 0) | ((i + sel_off) * tr == ctx_rows)

            @pl.when(first)
            def _():
                acc_ref[...] = jnp.zeros_like(acc_ref)

        sel = (lambda kk: sel_ref[kk:kk + 1, :]) if has_sel else None
        fulls = [r[...] for r in full_refs] + [(i + sel_off) * tr < ctx_rows]

        def slab(r, carry):
            rs = pl.ds(pl.multiple_of(r * sr, sr), sr)
            rows = [ref[rs, :].astype(F32) for ref in row_refs]
            outs, accs = fn(rows, sel, fulls)
            for o_ref, o in zip(out_refs, outs):
                o_ref[rs, :] = o.astype(o_ref.dtype)
            for kk, a in enumerate(accs):
                acc_ref[kk:kk + 1, :a.shape[1]] += a
            return carry

        lax.fori_loop(0, tr // sr, slab, 0)

    def row_map(off, col=0):
        if off == "ctx":
            return lambda i: (jnp.minimum(i, ctx_rows // tr - 1), col)
        if off < 0:
            return lambda i: (jnp.maximum(i + off, 0), col)
        return lambda i: (i + off, col)

    in_specs, args = [], []
    for arr, off, blk in row_ins:
        if blk is None:
            in_specs.append(pl.BlockSpec((tr, arr.shape[1]), row_map(off)))
        else:
            in_specs.append(pl.BlockSpec((tr, blk[0]), row_map(off, blk[1])))
        args.append(arr)
    if has_sel:
        in_specs.append(pl.BlockSpec((None,) + sel_in.shape[1:], lambda i: (sel_of(i), 0, 0)))
        args.append(sel_in)
    for arr in full_ins:
        in_specs.append(pl.BlockSpec(arr.shape, lambda i: (0, 0)))
        args.append(arr)
    out_specs, out_shape = [], []
    for rows, cols, dt, off in row_outs:
        out_specs.append(pl.BlockSpec((tr, cols), row_map(off)))
        out_shape.append(jax.ShapeDtypeStruct((rows, cols), dt))
    if has_acc:
        out_specs.append(pl.BlockSpec((None,) + tuple(acc_shape), lambda i: (sel_of(i), 0, 0)))
        out_shape.append(jax.ShapeDtypeStruct((2,) + tuple(acc_shape), F32))
    return pl.pallas_call(
        body, name=name, grid=(n_tiles,), in_specs=in_specs, out_specs=out_specs, out_shape=out_shape,
        compiler_params=_params(("arbitrary",)),
    )(*args)


def _swap_pairs(x):
    lane = lax.broadcasted_iota(jnp.int32, x.shape, 1)
    nxt = pltpu.roll(x, x.shape[1] - 1, 1)
    prv = pltpu.roll(x, 1, 1)
    return jnp.where(lane % 2 == 0, nxt, prv)


def _heads_map(fn, arrs, width):
    outs = None
    for h in range(width // HEAD_DIM):
        sl = slice(h * HEAD_DIM, (h + 1) * HEAD_DIM)
        res = fn(*[a[:, sl] for a in arrs])
        if outs is None:
            outs = [[] for _ in res]
        for lst, r in zip(outs, res):
            lst.append(r)
    return [jnp.concatenate(lst, axis=1) if len(lst) > 1 else lst[0] for lst in outs]


QSCALE = HEAD_DIM ** -0.5 * math.log2(math.e)
LN2 = math.log(2.0)


def _lane_chunks(a):
    return [a[:, cc * LANES_V7X:(cc + 1) * LANES_V7X] for cc in range(a.shape[1] // LANES_V7X)]


def _row_bcast(col, like):
    return jnp.broadcast_to(col, like.shape)


def _flash_tiles(t, tk_all, key_pref):
    return _tile(t, 1024), _tile(tk_all, key_pref)


def _flash_fwd(q, k, vx, groups):
    t, aw = q.shape
    tk_all, kvw = k.shape
    kvh = kvw // HEAD_DIM
    gw = groups * HEAD_DIM
    tq, tk = _flash_tiles(t, tk_all, 1536)
    nk = tk_all // tk

    def body(q_ref, k_ref, v_ref, o_ref, lse_ref, m_sc, l_sc, acc_sc):
        j = pl.program_id(2)

        @pl.when(j == 0)
        def _():
            m_sc[...] = jnp.full_like(m_sc, -jnp.inf)
            l_sc[...] = jnp.zeros_like(l_sc)
            acc_sc[...] = jnp.zeros_like(acc_sc)

        kt = k_ref[...]
        vt = v_ref[...]
        for g in range(groups):
            sl = slice(g * HEAD_DIM, (g + 1) * HEAD_DIM)
            s = _lane_chunks(lax.dot_general(q_ref[:, sl], kt, NT_DIMS, preferred_element_type=F32))
            mx = functools.reduce(jnp.maximum, s)
            m_prev = m_sc[g]
            m_new = jnp.maximum(m_prev, _row_bcast(jnp.max(mx, axis=1, keepdims=True), mx))
            p = jnp.concatenate([jnp.exp2(sc - m_new).astype(BF16) for sc in s], axis=1)
            alpha = jnp.exp2(m_prev - m_new)
            pv = jnp.dot(p, vt, preferred_element_type=F32)
            acc_sc[g] = alpha * acc_sc[g] + pv[:, :HEAD_DIM]
            l_sc[g] = alpha * l_sc[g] + pv[:, HEAD_DIM:]
            m_sc[g] = m_new

        @pl.when(j == nk - 1)
        def _():
            for g in range(groups):
                sl = slice(g * HEAD_DIM, (g + 1) * HEAD_DIM)
                o_ref[:, sl] = (acc_sc[g] / l_sc[g]).astype(o_ref.dtype)
                lse_ref[:, sl] = m_sc[g] + jnp.log2(l_sc[g])

    qs = pl.BlockSpec((tq, gw), lambda kh, i, j: (i, kh))
    sc = pltpu.VMEM((groups, tq, HEAD_DIM), F32)
    return pl.pallas_call(
        body, name="flash_fwd", grid=(kvh, t // tq, nk),
        in_specs=[qs, pl.BlockSpec((tk, HEAD_DIM), lambda kh, i, j: (j, kh)),
                  pl.BlockSpec((tk, 2 * HEAD_DIM), lambda kh, i, j: (j, kh))],
        out_specs=[qs, qs],
        out_shape=[jax.ShapeDtypeStruct((t, aw), BF16), jax.ShapeDtypeStruct((t, aw), F32)],
        scratch_shapes=[sc, sc, sc],
        compiler_params=_params(("parallel", "parallel", "arbitrary")),
    )(q, k, vx)


def _flash_p_ds(q, kt, vt, do, lse, delta):
    s = _lane_chunks(lax.dot_general(q, kt, NT_DIMS, preferred_element_type=F32))
    dp = _lane_chunks(lax.dot_general(do, vt, NT_DIMS, preferred_element_type=F32))
    p = [jnp.exp2(sc - lse) for sc in s]
    ds = jnp.concatenate([(pc * (dc - delta)).astype(BF16) for pc, dc in zip(p, dp)], axis=1)
    return jnp.concatenate([pc.astype(BF16) for pc in p], axis=1), ds


def _flash_delta(do, o):
    prod = do.astype(F32) * o.astype(F32)
    return _row_bcast(jnp.sum(prod, axis=1, keepdims=True), prod)


def _flash_bwd(q, k, vx, o, do, lse, groups):
    t, aw = q.shape
    tk_all, kvw = k.shape
    kvh = kvw // HEAD_DIM
    gw = groups * HEAD_DIM
    tq, tk = _flash_tiles(t, tk_all, 1024)
    nq, nk = t // tq, tk_all // tk

    def body(q_ref, k_ref, v_ref, o_ref, do_ref, lse_ref, dq_ref, dk_ref, dv_ref, dq_sc, dk_acc, dv_acc):
        j = pl.program_id(1)
        i = pl.program_id(2)

        @pl.when(i == 0)
        def _():
            dk_acc[...] = jnp.zeros_like(dk_acc)
            dv_acc[...] = jnp.zeros_like(dv_acc)

        @pl.when(j == 0)
        def _():
            dq_sc[i] = jnp.zeros((groups, tq, HEAD_DIM), F32)

        kt = k_ref[...]
        vt = v_ref[:, :HEAD_DIM]
        for g in range(groups):
            sl = slice(g * HEAD_DIM, (g + 1) * HEAD_DIM)
            qv = q_ref[:, sl]
            dov = do_ref[:, sl]
            p, ds = _flash_p_ds(qv, kt, vt, dov, lse_ref[:, sl], _flash_delta(dov, o_ref[:, sl]))
            dv_acc[...] += lax.dot_general(p, dov, TN_DIMS, preferred_element_type=F32)
            dk_acc[...] += lax.dot_general(ds, qv, TN_DIMS, preferred_element_type=F32)
            dq_sc[i, g] += jnp.dot(ds, kt, preferred_element_type=F32)

        @pl.when(i == nq - 1)
        def _():
            dk_ref[...] = dk_acc[...] * LN2
            dv_ref[...] = dv_acc[...]

        @pl.when(j == nk - 1)
        def _():
            for g in range(groups):
                dq_ref[:, g * HEAD_DIM:(g + 1) * HEAD_DIM] = dq_sc[i, g]

    qs = pl.BlockSpec((tq, gw), lambda kh, j, i: (i, kh))
    ks = pl.BlockSpec((tk, HEAD_DIM), lambda kh, j, i: (j, kh))
    dq_spec = pl.BlockSpec((tq, gw), lambda kh, j, i: (jnp.where(j == nk - 1, i, 0), kh))
    return pl.pallas_call(
        body, name="flash_bwd", grid=(kvh, nk, nq),
        in_specs=[qs, ks, pl.BlockSpec((tk, 2 * HEAD_DIM), lambda kh, j, i: (j, kh)), qs, qs, qs],
        out_specs=[dq_spec, ks, ks],
        out_shape=[jax.ShapeDtypeStruct((t, aw), F32), jax.ShapeDtypeStruct((tk_all, kvw), F32),
                   jax.ShapeDtypeStruct((tk_all, kvw), F32)],
        scratch_shapes=[pltpu.VMEM((nq, groups, tq, HEAD_DIM), F32), pltpu.VMEM((tk, HEAD_DIM), F32),
                        pltpu.VMEM((tk, HEAD_DIM), F32)],
        compiler_params=_params(("parallel", "arbitrary", "arbitrary")),
    )(q, k, vx, o, do, lse)


def _bf_nn(a, b):
    return jnp.dot(a.astype(BF16), b.astype(BF16), preferred_element_type=F32)


def _bf_nt(a, b):
    return lax.dot_general(a.astype(BF16), b.astype(BF16), NT_DIMS, preferred_element_type=F32)


def _bf_tn(a, b):
    return lax.dot_general(a.astype(BF16), b.astype(BF16), TN_DIMS, preferred_element_type=F32)


@jax.custom_vjp
def _d_nn(a, b):
    return _bf_nn(a, b)


@jax.custom_vjp
def _d_nt(a, b):
    return _bf_nt(a, b)


@jax.custom_vjp
def _d_tn(a, b):
    return _bf_tn(a, b)


_d_nn.defvjp(lambda a, b: (_bf_nn(a, b), (a, b)), lambda r, g: (_d_nt(g, r[1]), _d_tn(r[0], g)))
_d_nt.defvjp(lambda a, b: (_bf_nt(a, b), (a, b)), lambda r, g: (_d_nn(g, r[1]), _d_tn(g, r[0])))
_d_tn.defvjp(lambda a, b: (_bf_tn(a, b), (a, b)), lambda r, g: (_d_nt(r[1], g), _d_nn(r[0], g)))


def _ret_chunk(q, k_raw, v, state, lg, rev, dots):
    nn, nt, tn = dots
    c = RET_CHUNK
    tcol = lax.broadcasted_iota(jnp.int32, (c, 1), 0).astype(F32)
    trow = lax.broadcasted_iota(jnp.int32, (1, c), 1).astype(F32)
    ucol = jnp.where(rev, c - 1.0 - tcol, tcol)
    urow = jnp.where(rev, c - 1.0 - trow, trow)
    e = ucol - urow
    low = e >= 0
    intra = jnp.where(low, jnp.exp(jnp.where(low, e, 0.0) * lg), 0.0)
    k = k_raw * (HEAD_DIM ** -0.5)
    inner = nt(q, k) * intra
    y = nn(inner, v) + nn(q, state) * jnp.exp((ucol + 1.0) * lg)
    new_state = state * jnp.exp(c * lg) + tn(k * jnp.exp((c - 1.0 - ucol) * lg), v)
    return y, new_state


def _ret_chunk_index(n_chunks, n_ctx_chunks):
    def idx(d, s):
        if d == 0:
            return s
        return jnp.where(s < n_ctx_chunks, n_ctx_chunks - 1 - s, n_chunks - 1 - s + n_ctx_chunks)
    return idx


def _ret_fwd(pr, lgv, ctx_rows):
    tk_all = pr.shape[0]
    rw = pr.shape[1] // 3
    nh = rw // HEAD_DIM
    nc = tk_all // RET_CHUNK
    cidx = _ret_chunk_index(nc, ctx_rows // RET_CHUNK)

    def body(pf_ref, pb_ref, lg_ref, yf_ref, yb_ref, st_ref, s_sc):
        s = pl.program_id(0)

        @pl.when(s == 0)
        def _():
            s_sc[...] = jnp.zeros_like(s_sc)

        for d, (p_ref, y_ref) in enumerate(((pf_ref, yf_ref), (pb_ref, yb_ref))):
            for h in range(nh):
                cols = [slice((part * nh + h) * HEAD_DIM, (part * nh + h + 1) * HEAD_DIM) for part in range(3)]
                state = s_sc[d, h]
                st_ref[d, h] = state
                y, new_state = _ret_chunk(p_ref[:, cols[0]], p_ref[:, cols[1]], p_ref[:, cols[2]], state,
                                          lg_ref[d, h][:, :1], d == 1, (_bf_nn, _bf_nt, _bf_tn))
                y_ref[:, h * HEAD_DIM:(h + 1) * HEAD_DIM] = y
                s_sc[d, h] = new_state

    y_shape = jax.ShapeDtypeStruct((tk_all, rw), F32)
    return pl.pallas_call(
        body, name="ret_fwd", grid=(nc,),
        in_specs=[pl.BlockSpec((RET_CHUNK, 3 * rw), lambda s: (cidx(0, s), 0)),
                  pl.BlockSpec((RET_CHUNK, 3 * rw), lambda s: (cidx(1, s), 0)),
                  pl.BlockSpec(lgv.shape, lambda s: (0, 0, 0, 0))],
        out_specs=[pl.BlockSpec((RET_CHUNK, rw), lambda s: (cidx(0, s), 0)),
                   pl.BlockSpec((RET_CHUNK, rw), lambda s: (cidx(1, s), 0)),
                   pl.BlockSpec((2, nh, None, HEAD_DIM, HEAD_DIM), lambda s: (0, 0, s, 0, 0))],
        out_shape=[y_shape, y_shape, jax.ShapeDtypeStruct((2, nh, nc, HEAD_DIM, HEAD_DIM), F32)],
        scratch_shapes=[pltpu.VMEM((2, nh, HEAD_DIM, HEAD_DIM), F32)],
        compiler_params=_params(("arbitrary",)),
    )(pr, pr, lgv)


def _ret_bwd(pr, states, dy, lgv, ctx_rows):
    tk_all = pr.shape[0]
    rw = pr.shape[1] // 3
    nh = rw // HEAD_DIM
    nc = tk_all // RET_CHUNK
    n_ctx = ctx_rows // RET_CHUNK
    cidx = _ret_chunk_index(nc, n_ctx)

    def body(pf_ref, pb_ref, st_ref, dyf_ref, dyb_ref, lg_ref, dpf_ref, dpb_ref, dlg_ref, ds_sc):
        sp = pl.program_id(0)
        on_ctx = [cidx(dd, nc - 1 - sp) < n_ctx for dd in (0, 1)]

        @pl.when(sp == 0)
        def _():
            ds_sc[...] = jnp.zeros_like(ds_sc)
            dlg_ref[...] = jnp.zeros_like(dlg_ref)

        for d, (p_ref, dy_ref, dp_ref) in enumerate(((pf_ref, dyf_ref, dpf_ref), (pb_ref, dyb_ref, dpb_ref))):
            for h in range(nh):
                cols = [slice((part * nh + h) * HEAD_DIM, (part * nh + h + 1) * HEAD_DIM) for part in range(3)]

                def step(q, k, v, state, lg, rev=(d == 1)):
                    return _ret_chunk(q, k, v, state, lg, rev, (_d_nn, _d_nt, _d_tn))

                _, vjp = jax.vjp(step, p_ref[:, cols[0]], p_ref[:, cols[1]], p_ref[:, cols[2]], st_ref[d, h],
                                 lg_ref[d, h][:, :1])
                dy_h = jnp.where(on_ctx[d], 0.0, dy_ref[:, h * HEAD_DIM:(h + 1) * HEAD_DIM])
                grads = vjp((dy_h, ds_sc[d, h]))
                for part in range(3):
                    dp_ref[:, cols[part]] = grads[part]
                ds_sc[d, h] = grads[3]
                dlg_ref[d, h] += jnp.broadcast_to(grads[4], (1, HEAD_DIM))

    def at(d):
        return lambda sp: (cidx(d, nc - 1 - sp), 0)

    def dy_at(d):
        return lambda sp: (jnp.maximum(cidx(d, nc - 1 - sp) - n_ctx, 0), 0)

    dp_shape = jax.ShapeDtypeStruct((tk_all, 3 * rw), F32)
    lg_spec = pl.BlockSpec(lgv.shape, lambda sp: (0, 0, 0, 0))
    return pl.pallas_call(
        body, name="ret_bwd", grid=(nc,),
        in_specs=[pl.BlockSpec((RET_CHUNK, 3 * rw), at(0)), pl.BlockSpec((RET_CHUNK, 3 * rw), at(1)),
                  pl.BlockSpec((2, nh, None, HEAD_DIM, HEAD_DIM), lambda sp: (0, 0, nc - 1 - sp, 0, 0)),
                  pl.BlockSpec((RET_CHUNK, rw), dy_at(0)), pl.BlockSpec((RET_CHUNK, rw), dy_at(1)), lg_spec],
        out_specs=[pl.BlockSpec((RET_CHUNK, 3 * rw), at(0)), pl.BlockSpec((RET_CHUNK, 3 * rw), at(1)), lg_spec],
        out_shape=[dp_shape, dp_shape, jax.ShapeDtypeStruct(lgv.shape, F32)],
        scratch_shapes=[pltpu.VMEM((2, nh, HEAD_DIM, HEAD_DIM), F32)],
        compiler_params=_params(("arbitrary",)),
    )(pr, pr, states, dy, dy, lgv)


FLIP_X, FLIP_Y, FLIP_XY, FLIP_C = (1, 0, 0), (0, 1, 0), (1, 1, 0), (0, 0, 1)
CHIP_FLIPS = ((FLIP_X, 2), (FLIP_Y, 1), (FLIP_XY, 3))


def _flip(me, mask):
    return tuple(1 - v if m else v for v, m in zip(me, mask))


def _comm(name, ins, out_shapes, plan, n_remote, n_local, aliases=None):
    n_in, n_out = len(ins), len(out_shapes)

    def body(*refs):
        in_refs = refs[:n_in]
        out_refs = refs[n_in:n_in + n_out]
        send_sems, recv_sems, local_sems = refs[n_in + n_out:]
        me = (lax.axis_index("x"), lax.axis_index("y"), lax.axis_index("c"))
        local, phases = plan(in_refs, out_refs, me)
        local_copies = [pltpu.make_async_copy(s, d, local_sems.at[i]) for i, (s, d) in enumerate(local)]
        for cp in local_copies:
            cp.start()
        sent = []
        kk = 0
        for phase in phases:
            arrivals = []
            for mask, src, dst, landing in phase:
                peer = _flip(me, mask)
                cp = pltpu.make_async_remote_copy(src_ref=src, dst_ref=dst, send_sem=send_sems.at[kk],
                                                  recv_sem=recv_sems.at[kk], device_id=peer,
                                                  device_id_type=pl.DeviceIdType.MESH)
                cp.start()
                sent.append(cp)
                arrivals.append(pltpu.make_async_remote_copy(
                    src_ref=landing, dst_ref=landing, send_sem=send_sems.at[kk], recv_sem=recv_sems.at[kk],
                    device_id=peer, device_id_type=pl.DeviceIdType.MESH))
                kk += 1
            for cp in arrivals:
                cp.wait_recv()
        for cp in sent:
            cp.wait_send()
        for cp in local_copies:
            cp.wait()

    any_spec = pl.BlockSpec(memory_space=pl.ANY)
    return pl.pallas_call(
        body, name=name,
        in_specs=[any_spec] * n_in, out_specs=[any_spec] * n_out, out_shape=list(out_shapes),
        scratch_shapes=[pltpu.SemaphoreType.DMA((n_remote,)), pltpu.SemaphoreType.DMA((n_remote,)),
                        pltpu.SemaphoreType.DMA((max(n_local, 1),))],
        input_output_aliases=aliases or {},
    )(*ins)


def _ds(start, size):
    return pl.ds(pl.multiple_of(start * size, 8), size)


def _all_gather8(name, v):
    masks = [(a, b, cc) for a in (0, 1) for b in (0, 1) for cc in (0, 1)][1:]

    def index(p):
        return 4 * p[0] + 2 * p[1] + p[2]

    def plan(in_refs, out_refs, me):
        (src,), (out,) = in_refs, out_refs
        local = [(src, out.at[index(me)])]
        phase = [(m, src, out.at[index(me)], out.at[index(_flip(me, m))]) for m in masks]
        return local, [phase]

    return _comm(name, [v], [jax.ShapeDtypeStruct((N_DEV,) + v.shape, v.dtype)], plan, len(masks), 1)[0]


def _gather_row(name, row):
    n = row.shape[1]
    n_pad = -(-n // (8 * LANES_V7X)) * (8 * LANES_V7X)
    v = jnp.pad(row, ((0, 0), (0, n_pad - n))).reshape(8, n_pad // 8)
    return _all_gather8(name, v).reshape(N_DEV, n_pad)[:, :n]


class _Sharded:
    def __init__(self, kind, rows, cols):
        self.kind, self.rows, self.cols = kind, rows, cols
        self.shard_shape = (rows, cols // N_CHIP) if kind == "col" else (rows // N_CHIP, cols)
        self.piece_shape = (rows // 2, cols // N_CHIP) if kind == "col" else (rows // N_CHIP, cols // 2)

    def piece_of_full(self, ref, s, h):
        if self.kind == "col":
            return ref.at[_ds(h, self.rows // 2), _ds(s, self.cols // N_CHIP)]
        return ref.at[_ds(s, self.rows // N_CHIP), _ds(h, self.cols // 2)]

    def half_of_shard(self, ref, h):
        if self.kind == "col":
            return ref.at[_ds(h, self.rows // 2), :]
        return ref.at[:, _ds(h, self.cols // 2)]


def _place_shard(meta, w, s_arr, after=None):
    r, cols = w.shape
    tr = _tile(r, 256, 16)
    nr = r // tr

    def body(s_ref, w_ref, *rest):
        rest[-1][...] = w_ref[...].astype(BF16)

    if meta.kind == "col":
        o_map = lambda i, s_ref: (i, s_ref[0])
    else:
        o_map = lambda i, s_ref: (i + s_ref[0] * nr, 0)
    return pl.pallas_call(
        body, name="place_shard",
        grid_spec=pltpu.PrefetchScalarGridSpec(
            num_scalar_prefetch=1, grid=(nr,),
            in_specs=[pl.BlockSpec((tr, cols), lambda i, s_ref: (i, 0))]
            + ([] if after is None else [pl.BlockSpec(memory_space=pl.ANY)]),
            out_specs=pl.BlockSpec((tr, cols), o_map)),
        out_shape=jax.ShapeDtypeStruct((meta.rows, meta.cols), BF16),
        compiler_params=_params(("parallel",)),
    )(*((s_arr, w) if after is None else (s_arr, w, after)))


def _gather_copies(metas, over_ici):
    def copies(fulls, me):
        x, y, c = me
        s_me = 2 * x + y
        out = []
        for meta, full in zip(metas, fulls):
            for mask, bits in CHIP_FLIPS:
                s_peer = jnp.bitwise_xor(s_me, bits)
                if over_ici:
                    out.append((mask, meta.piece_of_full(full, s_me, c), meta.piece_of_full(full, s_me, c),
                                meta.piece_of_full(full, s_peer, c)))
                else:
                    out.append((FLIP_C, meta.piece_of_full(full, s_peer, c), meta.piece_of_full(full, s_peer, c),
                                meta.piece_of_full(full, s_peer, 1 - c)))
        return out
    return copies


def _gather_forward(name, metas, fulls):
    nt = len(metas)
    copies = _gather_copies(metas, False)
    outs = [jax.ShapeDtypeStruct((m.rows, m.cols), BF16) for m in metas]
    return _comm(name, list(fulls), outs, lambda ins, outs_, me: ([], [copies(outs_, me)]), 3 * nt, 0,
                 aliases={i: i for i in range(nt)})


HBM_SPEC = pl.BlockSpec(memory_space=pltpu.HBM)
SEM_SPEC = pl.BlockSpec(memory_space=pltpu.SEMAPHORE)
SPLIT_EFFECT = pltpu.SideEffectType.DATAFLOW_SIDE_EFFECTING


def _split_start(name, bufs, groups, after):
    nb, ng = len(bufs), len(groups)
    n_in = nb + (0 if after is None else 1)

    def body(*refs):
        buf_refs = refs[:nb]
        sem_refs = refs[n_in:n_in + 2 * ng]
        token = refs[-1]
        me = (lax.axis_index("x"), lax.axis_index("y"), lax.axis_index("c"))
        for gi, (lo, n_bufs, copies, _) in enumerate(groups):
            for kk, (mask, src, dst, _) in enumerate(copies(buf_refs[lo:lo + n_bufs], me)):
                pltpu.make_async_remote_copy(src_ref=src, dst_ref=dst, send_sem=sem_refs[2 * gi].at[kk],
                                             recv_sem=sem_refs[2 * gi + 1].at[kk], device_id=_flip(me, mask),
                                             device_id_type=pl.DeviceIdType.MESH).start()
        token[...] = jnp.zeros_like(token)

    out_shape = []
    for _, _, _, n in groups:
        out_shape += [pltpu.SemaphoreType.DMA((n,)), pltpu.SemaphoreType.DMA((n,))]
    out_shape += [pltpu.HBM(b.shape, b.dtype) for b in bufs] + [jax.ShapeDtypeStruct((8, LANES_V7X), F32)]
    res = pl.pallas_call(
        body, name=name, out_shape=tuple(out_shape),
        in_specs=(HBM_SPEC,) * nb + (pl.BlockSpec(memory_space=pl.ANY),) * (n_in - nb),
        out_specs=(SEM_SPEC,) * (2 * ng) + (HBM_SPEC,) * nb + (pl.BlockSpec(memory_space=pltpu.VMEM),),
        input_output_aliases={i: 2 * ng + i for i in range(nb)},
        compiler_params=pltpu.CompilerParams(has_side_effects=SPLIT_EFFECT),
    )(*[pltpu.with_memory_space_constraint(b, pltpu.HBM) for b in bufs], *([] if after is None else [after]))
    sems = [(res[2 * gi], res[2 * gi + 1]) for gi in range(ng)]
    return sems, list(res[2 * ng:2 * ng + nb]), res[-1]


def _split_wait(name, sems, bufs, copies, after):
    nb = len(bufs)

    def body(*refs):
        buf_refs = refs[:nb]
        send_sems, recv_sems = refs[nb], refs[nb + 1]
        me = (lax.axis_index("x"), lax.axis_index("y"), lax.axis_index("c"))
        for kk, (mask, _, _, landing) in enumerate(copies(buf_refs, me)):
            cp = pltpu.make_async_remote_copy(src_ref=landing, dst_ref=landing, send_sem=send_sems.at[kk],
                                              recv_sem=recv_sems.at[kk], device_id=_flip(me, mask),
                                              device_id_type=pl.DeviceIdType.MESH)
            cp.wait_send()
            cp.wait_recv()

    return list(pl.pallas_call(
        body, name=name, out_shape=tuple(pltpu.HBM(b.shape, b.dtype) for b in bufs),
        in_specs=(HBM_SPEC,) * nb + (SEM_SPEC, SEM_SPEC, pl.BlockSpec(memory_space=pl.ANY)),
        out_specs=(HBM_SPEC,) * nb,
        input_output_aliases={i: i for i in range(nb)},
        compiler_params=pltpu.CompilerParams(has_side_effects=SPLIT_EFFECT),
    )(*bufs, sems[0], sems[1], after))


N_REDUCE_PIECES = 7


def _reduce_copies(metas):
    def copies(refs, me):
        x, y, c = me
        s_me = 2 * x + y
        out = []
        for m, g, land in zip(metas, refs[:len(metas)], refs[len(metas):]):
            for kk, (mask, bits) in enumerate(CHIP_FLIPS):
                s_peer = jnp.bitwise_xor(s_me, bits)
                out.append((mask, m.piece_of_full(g, s_peer, c), land.at[kk], land.at[kk]))
                out.append((mask[:2] + (1,), m.piece_of_full(g, s_peer, 1 - c), land.at[3 + kk], land.at[3 + kk]))
            out.append((FLIP_C, m.piece_of_full(g, s_me, 1 - c), land.at[6], land.at[6]))
        return out
    return copies


def _reduce_start(name, metas, grads):
    lands = [lax.empty((N_REDUCE_PIECES,) + m.piece_shape, BF16) for m in metas]
    bufs = list(grads) + lands
    sems, thru, token = _split_start(name, bufs, [(0, len(bufs), _reduce_copies(metas),
                                                   N_REDUCE_PIECES * len(metas))], None)
    return sems[0], thru, token


def _reduce_wait(name, metas, sems, thru, after):
    done = _split_wait(name, sems, thru, _reduce_copies(metas), after)
    return done[:len(metas)], done[len(metas):]


def _share_halves(metas, shards):
    def plan(in_refs, out_refs, me):
        c = me[2]
        phase = [(FLIP_C, m.half_of_shard(g, c), m.half_of_shard(g, c), m.half_of_shard(g, 1 - c))
                 for m, g in zip(metas, out_refs)]
        return [], [phase]

    outs = [jax.ShapeDtypeStruct(m.shard_shape, F32) for m in metas]
    return _comm("share_halves", list(shards), outs, plan, len(metas), 0,
                 aliases={i: i for i in range(len(metas))})


def _sum_pieces(meta, grad, landed, s_arr, c_arr):
    pr, pc = meta.piece_shape
    tr = _tile(pr, 256, 16)
    tc = _tile(pc, 2048)
    nr, ncol = pr // tr, pc // tc

    def body(s_ref, c_ref, p_ref, l_ref, o_ref):
        acc = p_ref[...].astype(F32)
        for kk in range(N_REDUCE_PIECES):
            acc = acc + l_ref[kk].astype(F32)
        o_ref[...] = acc

    if meta.kind == "col":
        p_map = lambda i, j, s_ref, c_ref: (i + c_ref[0] * nr, j + s_ref[0] * ncol)
        o_map = lambda i, j, s_ref, c_ref: (i + c_ref[0] * nr, j)
    else:
        p_map = lambda i, j, s_ref, c_ref: (i + s_ref[0] * nr, j + c_ref[0] * ncol)
        o_map = lambda i, j, s_ref, c_ref: (i, j + c_ref[0] * ncol)
    blk = (tr, tc)
    return pl.pallas_call(
        body, name="sum_pieces",
        grid_spec=pltpu.PrefetchScalarGridSpec(
            num_scalar_prefetch=2, grid=(nr, ncol),
            in_specs=[pl.BlockSpec(blk, p_map),
                      pl.BlockSpec((N_REDUCE_PIECES,) + blk, lambda i, j, s_ref, c_ref: (0, i, j))],
            out_specs=pl.BlockSpec(blk, o_map)),
        out_shape=jax.ShapeDtypeStruct(meta.shard_shape, F32),
        compiler_params=_params(("parallel", "parallel")),
    )(s_arr, c_arr, grad, landed)


def _adam_rows(rows, sel, fulls):
    w, g, m, v = rows
    m2 = ADAM_B1 * m + (1.0 - ADAM_B1) * g
    v2 = ADAM_B2 * v + (1.0 - ADAM_B2) * jnp.square(g)
    m_hat = m2 / (1.0 - ADAM_B1 ** ADAM_STEP)
    v_hat = v2 / (1.0 - ADAM_B2 ** ADAM_STEP)
    delta = -ADAM_LR * (m_hat / (jnp.sqrt(v_hat) + ADAM_EPS) + ADAM_WD * w)
    return [delta, m2, v2], []


def _adamw(w, g, m, v):
    r, c = w.shape
    tr = _tile(r, 128, 8)
    outs = _rowwise("adamw", _adam_rows, n_tiles=r // tr, tr=tr,
                    row_ins=[(w, 0, None), (g, 0, None), (m, 0, None), (v, 0, None)],
                    row_outs=[(r, c, F32, 0)] * 3)
    return outs[0], outs[1], outs[2]


def _ada_fwd(cg, w, b):
    d, n = w.shape
    tn = _tile(n, 512)

    def body(c_ref, w_ref, b_ref, o_ref):
        a = _silu(c_ref[...]).astype(BF16)
        o_ref[...] = jnp.dot(a, w_ref[...].astype(BF16), preferred_element_type=F32) + b_ref[...]

    return pl.pallas_call(
        body, name="ada_fwd", grid=(n // tn,),
        in_specs=[pl.BlockSpec(cg.shape, lambda j: (0, 0)), pl.BlockSpec((d, tn), lambda j: (0, j)),
                  pl.BlockSpec((1, tn), lambda j: (0, j))],
        out_specs=pl.BlockSpec((cg.shape[0], tn), lambda j: (0, j)),
        out_shape=jax.ShapeDtypeStruct((cg.shape[0], n), F32),
        compiler_params=_params(("parallel",)),
    )(cg, w, b)


def _ada_bwd(cg, dm, w):
    d, n = w.shape
    tn = _tile(n, 512)
    nj = n // tn

    def body(c_ref, dm_ref, w_ref, gw_ref, da_ref, acc):
        j = pl.program_id(0)

        @pl.when(j == 0)
        def _():
            acc[...] = jnp.zeros_like(acc)

        a = _silu(c_ref[...]).astype(BF16)
        dmv = dm_ref[...].astype(BF16)
        gw_ref[...] = lax.dot_general(a, dmv, TN_DIMS, preferred_element_type=F32)
        acc[...] += lax.dot_general(dmv, w_ref[...].astype(BF16), NT_DIMS, preferred_element_type=F32)

        @pl.when(j == nj - 1)
        def _():
            da_ref[...] = acc[...]

    return pl.pallas_call(
        body, name="ada_bwd", grid=(nj,),
        in_specs=[pl.BlockSpec(cg.shape, lambda j: (0, 0)), pl.BlockSpec((dm.shape[0], tn), lambda j: (0, j)),
                  pl.BlockSpec((d, tn), lambda j: (0, j))],
        out_specs=[pl.BlockSpec((d, tn), lambda j: (0, j)), pl.BlockSpec(cg.shape, lambda j: (0, 0))],
        out_shape=[jax.ShapeDtypeStruct((d, n), F32), jax.ShapeDtypeStruct(cg.shape, F32)],
        scratch_shapes=[pltpu.VMEM(cg.shape, F32)],
        compiler_params=_params(("arbitrary",)),
    )(cg, dm, w)


def _small_reduce(gathered, logits, n_mod_cols, lg_off, loss_off, loss_cols):
    npk = gathered.shape[1]

    def body(g_ref, lo_ref, tot_ref, gb_ref, gl_ref, loss_ref):
        acc = g_ref[0:1, :]
        for dd in range(1, N_DEV):
            acc = acc + g_ref[dd:dd + 1, :]
        tot_ref[...] = acc
        gb_ref[...] = acc[:, :n_mod_cols] + acc[:, n_mod_cols:2 * n_mod_cols]
        gl_ref[...] = acc[:, lg_off:lg_off + LANES_V7X] * _sigmoid(-lo_ref[...])
        loss = jnp.sum(acc[:, loss_off:loss_off + loss_cols], axis=1, keepdims=True)
        loss_ref[...] = jnp.broadcast_to(loss, loss_ref.shape)

    lane = jax.ShapeDtypeStruct((1, LANES_V7X), F32)
    return pl.pallas_call(
        body, name="small_reduce",
        out_shape=[jax.ShapeDtypeStruct((1, npk), F32), jax.ShapeDtypeStruct((1, n_mod_cols), F32), lane, lane],
    )(gathered, logits)


def _c_ctx_grad(parts, c_ctx):
    def body(p_ref, c_ref, o_ref):
        tot = p_ref[0:1, :] + p_ref[2:3, :] + p_ref[4:5, :] + p_ref[6:7, :]
        _, vjp = jax.vjp(_silu, c_ref[...])
        o_ref[...] = vjp(tot)[0]

    return pl.pallas_call(body, name="c_ctx_grad", out_shape=jax.ShapeDtypeStruct(c_ctx.shape, F32))(parts, c_ctx)


def _rope_tables(seq, ctx_rows):
    rows = seq // GRID_W
    half = HEAD_DIM // 2
    inv_freq = ROPE_THETA ** (-jnp.arange(0, half, 2, dtype=F32) / half)
    ang_row = jnp.arange(rows, dtype=F32)[:, None] * inv_freq
    ang_col = jnp.arange(GRID_W, dtype=F32)[:, None] * inv_freq

    def spread(fn):
        return jnp.concatenate([jnp.repeat(fn(ang_row), GRID_W, axis=0), jnp.tile(fn(ang_col), (rows, 1))], axis=-1)

    cos, sin = spread(jnp.cos), spread(jnp.sin)
    cos_full = jnp.repeat(cos, 2, axis=1)
    sin_signed = jnp.stack([-sin, sin], axis=-1).reshape(seq, HEAD_DIM)
    cos_full = jnp.concatenate([jnp.ones((ctx_rows, HEAD_DIM), F32), cos_full], axis=0)
    sin_signed = jnp.concatenate([jnp.zeros((ctx_rows, HEAD_DIM), F32), sin_signed], axis=0)
    return cos_full, sin_signed


def _qk_rot(p, gain, cos_full, sin_signed):
    r = _rmsn(p) * gain
    return r * cos_full + _swap_pairs(r) * sin_signed


def _qk_rot_bwd(g, p, gain, cos_full, sin_signed):
    g1 = g * cos_full + _swap_pairs(g * sin_signed)
    _, vjp = jax.vjp(lambda pp, gn: _rmsn(pp) * gn, p, gain)
    return vjp(g1)


def kernel(x, c, ctx, c_ctx, w_ada, b_ada, ffn1_w_in, ffn1_w_out, mix_w_in, attn_q_gain, attn_k_gain, ret_decay_logit, w_proj_attn, w_proj_ret, mix_w_out, ffn2_w_in, ffn2_w_out, final_norm, loss_target, m_c_ctx, m_w_ada, m_b_ada, m_ffn1_w_in, m_ffn1_w_out, m_mix_w_in, m_attn_q_gain, m_attn_k_gain, m_ret_decay_logit, m_w_proj_attn, m_w_proj_ret, m_mix_w_out, m_ffn2_w_in, m_ffn2_w_out, m_final_norm, v_c_ctx, v_w_ada, v_b_ada, v_ffn1_w_in, v_ffn1_w_out, v_mix_w_in, v_attn_q_gain, v_attn_k_gain, v_ret_decay_logit, v_w_proj_attn, v_w_proj_ret, v_mix_w_out, v_ffn2_w_in, v_ffn2_w_out, v_final_norm):
    xi, yi, ci = lax.axis_index("x"), lax.axis_index("y"), lax.axis_index("c")
    dev = 4 * xi + 2 * yi + ci
    s_me = 2 * xi + yi
    c_arr = jnp.reshape(ci, (1,)).astype(jnp.int32)
    s_arr = jnp.reshape(s_me, (1,)).astype(jnp.int32)

    t, d = x.shape[1], x.shape[2]
    tc = ctx.shape[1]
    tk = tc + t
    aw = w_proj_attn.shape[1]
    rw = w_proj_ret.shape[1]
    pw = mix_w_in.shape[2] * N_CHIP
    kvw = (pw - aw - 4 * rw - 2 * d) // 2
    groups = aw // kvw
    n_ret_heads = rw // HEAD_DIM
    mod_cols = N_MOD * d
    tr = _tile(tc, 256, 32)
    nt_all, nt_x, ctx_tiles = tk // tr, t // tr, tc // tr

    c_rows = _gather_row("gather_c", c)
    cg = jnp.concatenate([c_rows, c_ctx[None, :], jnp.zeros((7, d), F32)], axis=0)
    w_ada_l = w_ada[0]
    ada_cols = w_ada_l.shape[1]
    b_ada_l = lax.dynamic_slice_in_dim(b_ada, s_me * ada_cols, ada_cols, axis=1)
    mod_shard = _ada_fwd(cg, w_ada_l, b_ada_l)
    mod_g = _all_gather8("gather_mod", mod_shard)
    mod_full = jnp.concatenate([mod_g[0], mod_g[2], mod_g[4], mod_g[6]], axis=1)
    mod_x = lax.dynamic_slice_in_dim(mod_full, dev, 1, axis=0).reshape(N_MOD, d)
    mod_c = mod_full[8].reshape(N_MOD, d)
    mods = jnp.stack([mod_c, mod_x])

    big = [("col", ffn1_w_in), ("row", ffn1_w_out), ("col", mix_w_in), ("col", w_proj_attn), ("col", w_proj_ret),
           ("row", mix_w_out), ("col", ffn2_w_in), ("row", ffn2_w_out)]
    metas = []
    for kind, w in big:
        r_l, c_l = w.shape[1:]
        metas.append(_Sharded(kind, r_l, c_l * N_CHIP) if kind == "col" else _Sharded(kind, r_l * N_CHIP, c_l))
    layer_groups = ((0, 1), (1, 2), (2, 6), (6, 8))
    sems_first, placed_first, token = _split_start(
        "gather_start_first", [_place_shard(metas[0], big[0][1][0], s_arr)],
        [(0, 1, _gather_copies(metas[0:1], True), 3)], mods)
    placed_rest = [_place_shard(m, w[0], s_arr, after=token) for m, (_, w) in zip(metas[1:], big[1:])]
    sems_rest, placed_rest, token = _split_start(
        "gather_start_rest", placed_rest,
        [(lo - 1, hi - lo, _gather_copies(metas[lo:hi], True), 3 * (hi - lo)) for lo, hi in layer_groups[1:]], None)
    gather_sems, placed = sems_first + sems_rest, placed_first + placed_rest
    mods = mods + token[0, 0]

    def weights_of(gi, after):
        lo, hi = layer_groups[gi]
        arrived = _split_wait("gather_wait_%d" % gi, gather_sems[gi], placed[lo:hi],
                              _gather_copies(metas[lo:hi], True), after)
        return _gather_forward("gather_forward_%d" % gi, metas[lo:hi], arrived)

    def weights_early(gi, after):
        lo, hi = layer_groups[gi]
        arrived = _split_wait("gather_wait_%d" % gi, gather_sems[gi], placed[lo:hi],
                              _gather_copies(metas[lo:hi], True), after)
        sems, thru, tok = _split_start("gather_forward_start_%d" % gi, arrived,
                                       [(0, hi - lo, _gather_copies(metas[lo:hi], False), 3 * (hi - lo))], None)
        return gi, sems[0], thru, tok

    def weights_late(early, after):
        gi, sems, thru, _ = early
        lo, hi = layer_groups[gi]
        return _split_wait("gather_forward_wait_%d" % gi, sems, thru, _gather_copies(metas[lo:hi], False), after)

    cos_full, sin_signed = _rope_tables(t, tc)
    q_gain, k_gain = attn_q_gain, attn_k_gain
    log_gamma = jax.nn.log_sigmoid(ret_decay_logit[0])
    lgv = jnp.broadcast_to(log_gamma[:, :, None, None], (2, n_ret_heads, 1, HEAD_DIM))

    def stream_rows(h, off):
        if isinstance(h, tuple):
            return [(h[0], "ctx", None), (h[1], -ctx_tiles, None)]
        return [(h, off, None)]

    def stream_value(h, rows, fulls):
        if isinstance(h, tuple):
            return jnp.where(fulls[-1], rows[0], rows[1]), rows[2:]
        return rows[0], rows[1:]

    def norm_mod(name, h, n_tiles, off, i_shift, i_scale):
        def fn(rows, sel, fulls):
            hv, _ = stream_value(h, rows, fulls)
            return [_rmsn(hv) * (1.0 + sel(i_scale)) + sel(i_shift)], []
        return _rowwise(name, fn, n_tiles=n_tiles, tr=tr, row_ins=stream_rows(h, 0),
                        row_outs=[(n_tiles * tr, d, BF16, 0)], sel_in=mods, sel_off=off, ctx_rows=tc)[0]

    def resid_norm(name, h, h_off, f, n_tiles, off, i_gate, coef, i_shift, i_scale):
        def fn(rows, sel, fulls):
            hv, rest = stream_value(h, rows, fulls)
            hn = hv + coef * sel(i_gate) * rest[0]
            return [hn, _rmsn(hn) * (1.0 + sel(i_scale)) + sel(i_shift)], []
        return _rowwise(name, fn, n_tiles=n_tiles, tr=tr, row_ins=stream_rows(h, h_off) + [(f, 0, None)],
                        row_outs=[(f.shape[0], d, F32, 0), (f.shape[0], d, BF16, 0)], sel_in=mods, sel_off=off,
                        ctx_rows=tc)

    h0 = (ctx[0], x[0])
    n1 = norm_mod("norm_mod1", h0, nt_all, 0, 0, 1)
    w1i, = weights_of(0, n1)
    hm1, ua1, ub1 = _mm_swiglu("ffn1_in", n1, w1i)
    w1o, = weights_of(1, hm1)
    f1 = _mm("ffn1_out", hm1, w1o, "nn", BF16)
    mixer_weights = weights_early(2, f1)
    mods = mods + mixer_weights[3][0, 0]
    h1, n2 = resid_norm("resid_norm1", h0, 0, f1, nt_all, 0, 2, 0.5, 3, 4)
    wmi, wpa, wpr, wmo = weights_late(mixer_weights, n2)
    p_q = _mm("mix_in_q", n2, wmi, "nn", F32, 0, aw)
    p_kv = _mm("mix_in_kv", n2, wmi, "nn", F32, aw, 2 * kvw)
    p_r = _mm("mix_in_ret", n2, wmi, "nn", F32, aw + 2 * kvw, 3 * rw)
    p_gr = _mm("mix_in_gr", n2, wmi, "nn", BF16, aw + 2 * kvw + 3 * rw, rw)
    p_gab = _mm("mix_in_gab", n2, wmi, "nn", BF16, aw + 2 * kvw + 4 * rw, 2 * d)
    ffn2_weights = weights_early(3, p_gab)
    q_gain = q_gain + ffn2_weights[3][0, 0]

    def q_prep(rows, sel, fulls):
        p, cf, ss = rows
        return _heads_map(lambda ph: [_qk_rot(ph, fulls[0], cf, ss) * QSCALE], [p], aw), []

    q_rot = _rowwise("q_prep", q_prep, n_tiles=nt_x, tr=tr,
                     row_ins=[(p_q, ctx_tiles, None), (cos_full, ctx_tiles, None), (sin_signed, ctx_tiles, None)],
                     row_outs=[(t, aw, BF16, 0)], full_ins=[q_gain])[0]

    def kv_prep(rows, sel, fulls):
        p, cf, ss = rows
        k_rot = _heads_map(lambda ph: [_qk_rot(ph, fulls[0], cf, ss)], [p[:, :kvw]], kvw)[0]
        v_ones = _heads_map(lambda vh: [jnp.concatenate([vh, jnp.ones_like(vh)], axis=1)], [p[:, kvw:]], kvw)[0]
        return [k_rot, v_ones], []

    k_rot, v_att = _rowwise("kv_prep", kv_prep, n_tiles=nt_all, tr=tr,
                            row_ins=[(p_kv, 0, None), (cos_full, 0, None), (sin_signed, 0, None)],
                            row_outs=[(tk, kvw, BF16, 0), (tk, 2 * kvw, BF16, 0)], full_ins=[k_gain])

    ya, lse = _flash_fwd(q_rot, k_rot, v_att, groups)
    y_fwd, y_bwd, states = _ret_fwd(p_r, lgv, tc)

    def ret_out_fn(yf, yb, gr):
        return [_silu(gr) * _rmsn(yf + yb)]

    def ret_out(rows, sel, fulls):
        return _heads_map(ret_out_fn, rows, rw), []

    y_rows = [(y_fwd, ctx_tiles, None), (y_bwd, ctx_tiles, None), (p_gr, ctx_tiles, None)]
    yr = _rowwise("ret_out", ret_out, n_tiles=nt_x, tr=tr, row_ins=y_rows, row_outs=[(t, rw, BF16, 0)])[0]

    pa = _mm("proj_attn", ya, wpa, "nn", BF16)
    prj = _mm("proj_ret", yr, wpr, "nn", BF16)

    def merge_fn(a, r, ga, gb):
        return _sigmoid(ga) * a + _sigmoid(gb) * r

    gate_rows = [(p_gab, ctx_tiles, (d, 0)), (p_gab, ctx_tiles, (d, 1))]
    z = _rowwise("merge", lambda rows, sel, fulls: ([merge_fn(*rows)], []), n_tiles=nt_x, tr=tr,
                 row_ins=[(pa, 0, None), (prj, 0, None)] + gate_rows, row_outs=[(t, d, BF16, 0)])[0]
    fo = _mm("mix_out", z, wmo, "nn", BF16)
    h2, n3 = resid_norm("resid_norm2", h1, ctx_tiles, fo, nt_x, ctx_tiles, 5, 1.0, 6, 7)
    w2i, w2o = weights_late(ffn2_weights, n3)
    hm2, ua2, ub2 = _mm_swiglu("ffn2_in", n3, w2i)
    f2 = _mm("ffn2_out", hm2, w2o, "nn", BF16)

    def loss_fn(rows, sel, fulls):
        h2v, f2v, tgt = rows
        g3 = 0.5 * sel(8)
        y, vjp = jax.vjp(lambda hh, ww: _rmsn(hh) * ww, h2v + g3 * f2v, fulls[0])
        err = y - tgt
        dh, dw = vjp(err / d)
        return [dh, g3 * dh], [0.5 / d * jnp.sum(err * err, axis=0, keepdims=True), dw,
                               jnp.sum(0.5 * dh * f2v, axis=0, keepdims=True)]

    dh3, df2, loss_acc = _rowwise("loss_head", loss_fn, n_tiles=nt_x, tr=tr,
                                  row_ins=[(h2, 0, None), (f2, 0, None), (loss_target[0], 0, None)],
                                  row_outs=[(t, d, F32, 0), (t, d, BF16, 0)], sel_in=mods, sel_off=ctx_tiles,
                                  ctx_rows=tc, full_ins=[final_norm[None, :]], acc_shape=(8, d))
    loss_cols, g_final, dg3 = loss_acc[1, 0:1], loss_acc[1, 1:2], loss_acc[1, 2:3]

    def norm_mod_bwd(name, dn, h, dres, dres_off, n_tiles, off, i_shift, i_scale, gate=None, out_off=0):
        def fn(rows, sel, fulls):
            hh, rows = stream_value(h, rows, fulls)
            g, dr = rows[:2]
            if dres_off < 0:
                dr = jnp.where(fulls[-1], 0.0, dr)
            _, vjp = jax.vjp(lambda a, sh, sc: _rmsn(a) * (1.0 + sc) + sh, hh,
                             sel(i_shift), sel(i_scale))
            dhh, dsh, dsc = vjp(g)
            dh = dr + dhh
            if gate is None:
                return [dh], [dsh, dsc]
            return [dh, gate[2] * sel(gate[1]) * dh], [dsh, dsc, jnp.sum(gate[2] * dh * rows[2], axis=0, keepdims=True)]
        n_rows = dn.shape[0] + out_off * tr
        row_ins = stream_rows(h, 0) + [(dn, 0, None), (dres, dres_off, None)]
        row_outs = [(n_rows, d, F32, out_off)]
        if gate is not None:
            row_ins.append((gate[0], 0, None))
            row_outs.append((n_rows, d, BF16, out_off))
        return _rowwise(name, fn, n_tiles=n_tiles, tr=tr, row_ins=row_ins, row_outs=row_outs, sel_in=mods,
                        sel_off=off, ctx_rows=tc, acc_shape=(8, d))

    g_w2o = _mm("ffn2_out_dw", hm2, df2, "tn", BF16)
    du2 = _mm_swiglu_bwd("ffn2_out_dx", df2, w2o, ua2, ub2)
    g_w2i = _mm("ffn2_in_dw", n3, du2, "tn", BF16, stacked=True)
    dn3 = _mm("ffn2_in_dx", du2, w2i, "nt", F32, stacked=True)
    dh2, dfo, acc_n3 = norm_mod_bwd("norm_mod_bwd3", dn3, h2, dh3, 0, nt_x, ctx_tiles, 6, 7, gate=(fo, 5, 1.0))

    sems_ffn2, thru_ffn2, token = _reduce_start("reduce_start_ffn2", metas[6:8], [g_w2i, g_w2o])

    g_wmo = _mm("mix_out_dw", z, dfo, "tn", BF16, after=token)
    dz = _mm("mix_out_dx", dfo, wmo, "nt", F32)

    def merge_bwd(rows, sel, fulls):
        g, a, r, ga, gb = rows
        _, vjp = jax.vjp(merge_fn, a, r, ga, gb)
        da, dr, dga, dgb = vjp(g)
        return [da, dr, jnp.concatenate([dga, dgb], axis=1)], []

    dpa, dpr, dgab = _rowwise("merge_bwd", merge_bwd, n_tiles=nt_x, tr=tr,
                              row_ins=[(dz, 0, None), (pa, 0, None), (prj, 0, None)] + gate_rows,
                              row_outs=[(t, d, BF16, 0), (t, d, BF16, 0), (t, 2 * d, BF16, 0)])
    g_wpa = _mm("proj_attn_dw", ya, dpa, "tn", BF16)
    dya = _mm("proj_attn_dx", dpa, wpa, "nt", BF16)
    g_wpr = _mm("proj_ret_dw", yr, dpr, "tn", BF16)
    dyr = _mm("proj_ret_dx", dpr, wpr, "nt", F32)

    def ret_out_bwd(rows, sel, fulls):
        def per_head(g, yf, yb, gr):
            _, vjp = jax.vjp(lambda yy, gg: ret_out_fn(yy, 0.0, gg)[0], yf + yb, gr)
            return list(vjp(g))
        dy, dgr = _heads_map(per_head, rows, rw)
        return [dy, dgr], []

    dy_ret, dgr = _rowwise("ret_out_bwd", ret_out_bwd, n_tiles=nt_x, tr=tr, row_ins=[(dyr, 0, None)] + y_rows,
                           row_outs=[(t, rw, F32, 0), (t, rw, BF16, 0)])
    dp_rf, dp_rb, dlg = _ret_bwd(p_r, states, dy_ret, lgv, tc)
    dp_r = _rowwise("ret_bwd_sum", lambda rows, sel, fulls: ([rows[0] + rows[1]], []), n_tiles=nt_all, tr=tr,
                    row_ins=[(dp_rf, 0, None), (dp_rb, 0, None)], row_outs=[(tk, 3 * rw, BF16, 0)])[0]

    dq_rot, dk_rot, dv_att = _flash_bwd(q_rot, k_rot, v_att, ya, dya, lse, groups)

    def q_prep_bwd(rows, sel, fulls):
        g, p, cf, ss = rows
        gain_acc = []

        def per_head(gh, ph):
            dp, dgain = _qk_rot_bwd(gh * HEAD_DIM ** -0.5, ph, fulls[0], cf, ss)
            gain_acc.append(dgain)
            return [dp]
        dp = _heads_map(per_head, [g, p], aw)[0]
        return [dp], [functools.reduce(lambda a, b: a + b, gain_acc)]

    dp_q, acc_gq = _rowwise("q_prep_bwd", q_prep_bwd, n_tiles=nt_x, tr=tr,
                            row_ins=[(dq_rot, 0, None), (p_q, ctx_tiles, None), (cos_full, ctx_tiles, None),
                                     (sin_signed, ctx_tiles, None)],
                            row_outs=[(t, aw, BF16, 0)], full_ins=[q_gain], acc_shape=(8, HEAD_DIM),
                            sel_off=ctx_tiles, ctx_rows=tc)

    def kv_prep_bwd(rows, sel, fulls):
        gk, gv, p, cf, ss = rows
        gain_acc = []

        def per_head(gh, ph):
            dp, dgain = _qk_rot_bwd(gh, ph, fulls[0], cf, ss)
            gain_acc.append(dgain)
            return [dp]
        dpk = _heads_map(per_head, [gk, p], kvw)[0]
        return [jnp.concatenate([dpk, gv], axis=1)], [functools.reduce(lambda a, b: a + b, gain_acc)]

    dp_kv, acc_gk = _rowwise("kv_prep_bwd", kv_prep_bwd, n_tiles=nt_all, tr=tr,
                             row_ins=[(dk_rot, 0, None), (dv_att, 0, None), (p_kv, 0, (kvw, 0)), (cos_full, 0, None),
                                      (sin_signed, 0, None)],
                             row_outs=[(tk, 2 * kvw, BF16, 0)], full_ins=[k_gain], acc_shape=(8, HEAD_DIM),
                             sel_off=0, ctx_rows=tc)

    def with_ctx_zeros(a):
        return jnp.concatenate([jnp.zeros((tc, a.shape[1]), a.dtype), a], axis=0)

    dp = jnp.concatenate([with_ctx_zeros(dp_q), dp_kv, dp_r, with_ctx_zeros(dgr), with_ctx_zeros(dgab)], axis=1)
    g_wmi = _mm("mix_in_dw", n2, dp, "tn", BF16)
    dn2 = _mm("mix_in_dx", dp, wmi, "nt", F32)
    dh1, df1, acc_n2 = norm_mod_bwd("norm_mod_bwd2", dn2, h1, dh2, -ctx_tiles, nt_all, 0, 3, 4, gate=(f1, 2, 0.5))
    sems_mix, thru_mix, token = _reduce_start("reduce_start_mix", metas[2:6], [g_wmi, g_wpa, g_wpr, g_wmo])

    g_w1o = _mm("ffn1_out_dw", hm1, df1, "tn", BF16, after=token)
    sems_w1o, thru_w1o, token = _reduce_start("reduce_start_ffn1_out", metas[1:2], [g_w1o])
    du1 = _mm_swiglu_bwd("ffn1_out_dx", df1, w1o, ua1, ub1, after=token)
    g_w1i = _mm("ffn1_in_dw", n1, du1, "tn", BF16, stacked=True)
    sems_w1i, thru_w1i, token = _reduce_start("reduce_start_ffn1_in", metas[0:1], [g_w1i])
    dn1 = _mm("ffn1_in_dx", du1, w1i, "nt", F32, after=token, stacked=True)
    dh0, acc_n1 = norm_mod_bwd("norm_mod_bwd1", dn1, h0, dh1, 0, nt_all, 0, 0, 1, out_off=-ctx_tiles)
    grad_x = dh0[None]

    grads_own, landed = [], []
    for name, lo, hi, sems_l, thru_l in (("reduce_wait_ffn1_in", 0, 1, sems_w1i, thru_w1i),
                                         ("reduce_wait_ffn1_out", 1, 2, sems_w1o, thru_w1o),
                                         ("reduce_wait_mix", 2, 6, sems_mix, thru_mix),
                                         ("reduce_wait_ffn2", 6, 8, sems_ffn2, thru_ffn2)):
        grads_l, landed_l = _reduce_wait(name, metas[lo:hi], sems_l, thru_l, dh0)
        grads_own += grads_l
        landed += landed_l
    pieces = [_sum_pieces(m, g, l, s_arr, c_arr) for m, g, l in zip(metas, grads_own, landed)]
    grads_big = _share_halves(metas, pieces)

    zero_row = jnp.zeros((1, d), F32)
    dmod_x = jnp.concatenate([acc_n1[1, 0:1], acc_n1[1, 1:2], acc_n2[1, 2:3], acc_n2[1, 0:1], acc_n2[1, 1:2],
                              acc_n3[1, 2:3], acc_n3[1, 0:1], acc_n3[1, 1:2], dg3], axis=1)
    dmod_c = jnp.concatenate([acc_n1[0, 0:1], acc_n1[0, 1:2], acc_n2[0, 2:3], acc_n2[0, 0:1], acc_n2[0, 1:2]]
                             + [zero_row] * 4, axis=1)
    dlg_row = jnp.pad(dlg[:, :, 0, 0].reshape(1, 2 * n_ret_heads), ((0, 0), (0, LANES_V7X - 2 * n_ret_heads)))
    packed = jnp.concatenate([dmod_x, dmod_c, acc_gq[1, 0:1], acc_gk[0, 0:1] + acc_gk[1, 0:1], dlg_row,
                              g_final, loss_cols], axis=1)
    off_gq = 2 * mod_cols
    off_gk = off_gq + LANES_V7X
    off_lg = off_gk + LANES_V7X
    off_fn = off_lg + LANES_V7X
    off_loss = off_fn + d
    gathered = _gather_row("gather_small", packed)
    logits_row = jnp.pad(ret_decay_logit.reshape(1, 2 * n_ret_heads), ((0, 0), (0, LANES_V7X - 2 * n_ret_heads)))
    totals, g_b_ada, g_decay, loss_row = _small_reduce(gathered, logits_row, mod_cols, off_lg, off_loss, d)
    loss = loss_row[0, 0]

    dm = jnp.concatenate([gathered[:, :mod_cols], totals[:, mod_cols:2 * mod_cols],
                          jnp.zeros((7, mod_cols), F32)], axis=0)
    dm_l = lax.dynamic_slice_in_dim(dm, s_me * ada_cols, ada_cols, axis=1)
    g_w_ada, da_part = _ada_bwd(cg, dm_l, w_ada_l)
    da_rows = _gather_row("gather_dc", da_part[8:9])
    g_c_ctx = _c_ctx_grad(da_rows, c_ctx[None, :])

    def as2d(a):
        return a.reshape(-1, a.shape[-1])

    grads = {
        "c_ctx": g_c_ctx, "w_ada": g_w_ada, "b_ada": g_b_ada,
        "ffn1_w_in": grads_big[0], "ffn1_w_out": grads_big[1], "mix_w_in": grads_big[2],
        "attn_q_gain": totals[:, off_gq:off_gq + HEAD_DIM], "attn_k_gain": totals[:, off_gk:off_gk + HEAD_DIM],
        "ret_decay_logit": g_decay[:, :2 * n_ret_heads],
        "w_proj_attn": grads_big[3], "w_proj_ret": grads_big[4], "mix_w_out": grads_big[5],
        "ffn2_w_in": grads_big[6], "ffn2_w_out": grads_big[7], "final_norm": totals[:, off_fn:off_fn + d],
    }
    weights = {"c_ctx": (c_ctx, m_c_ctx, v_c_ctx), "w_ada": (w_ada, m_w_ada, v_w_ada),
               "b_ada": (b_ada, m_b_ada, v_b_ada), "ffn1_w_in": (ffn1_w_in, m_ffn1_w_in, v_ffn1_w_in),
               "ffn1_w_out": (ffn1_w_out, m_ffn1_w_out, v_ffn1_w_out), "mix_w_in": (mix_w_in, m_mix_w_in, v_mix_w_in),
               "attn_q_gain": (attn_q_gain, m_attn_q_gain, v_attn_q_gain),
               "attn_k_gain": (attn_k_gain, m_attn_k_gain, v_attn_k_gain),
               "ret_decay_logit": (ret_decay_logit, m_ret_decay_logit, v_ret_decay_logit),
               "w_proj_attn": (w_proj_attn, m_w_proj_attn, v_w_proj_attn),
               "w_proj_ret": (w_proj_ret, m_w_proj_ret, v_w_proj_ret), "mix_w_out": (mix_w_out, m_mix_w_out, v_mix_w_out),
               "ffn2_w_in": (ffn2_w_in, m_ffn2_w_in, v_ffn2_w_in), "ffn2_w_out": (ffn2_w_out, m_ffn2_w_out, v_ffn2_w_out),
               "final_norm": (final_norm, m_final_norm, v_final_norm)}
    out_g, out_d, out_m, out_v = [], [], [], []
    for name, (w, m, v) in weights.items():
        shape = w.shape
        if name == "ret_decay_logit":
            w2, m2, v2 = (a.reshape(1, -1) for a in (w, m, v))
        else:
            w2, m2, v2 = as2d(w), as2d(m), as2d(v)
        g2 = grads[name].reshape(w2.shape)
        delta, new_m, new_v = _adamw(w2, g2, m2, v2)
        out_g.append(g2.reshape(shape))
        out_d.append(delta.reshape(shape))
        out_m.append(new_m.reshape(shape))
        out_v.append(new_v.reshape(shape))
    return (loss, grad_x, *out_g, *out_d, *out_m, *out_v)
```

```python
import functools
import math

import jax
import jax.numpy as jnp
from jax import lax
from jax.experimental import pallas as pl
from jax.experimental.pallas import tpu as pltpu

F32 = jnp.float32
BF16 = jnp.bfloat16

HEAD_DIM = 128
GRID_W = 64
ROPE_THETA = 10000.0
NORM_EPS = 1e-6
N_MOD = 9
RET_CHUNK = 128
ADAM_LR = 0.001
ADAM_B1 = 0.9
ADAM_B2 = 0.999
ADAM_EPS = 1e-08
ADAM_WD = 0.01
ADAM_STEP = 10

N_DEV = 8
N_CHIP = 4
LANES_V7X = 128
MXU_WIDTH_V7X = 256
VMEM_LIMIT_V7X = 52 * 1024 * 1024

NT_DIMS = (((1,), (1,)), ((), ()))
TN_DIMS = (((0,), (0,)), ((), ()))
NN_DIMS = (((1,), (0,)), ((), ()))


def _tile(n, pref, mult=LANES_V7X):
    if n <= pref:
        return n
    t = (pref // mult) * mult
    while t >= mult:
        if n % t == 0:
            return t
        t -= mult
    return n


def _params(sem):
    return pltpu.CompilerParams(dimension_semantics=sem, vmem_limit_bytes=VMEM_LIMIT_V7X)


def _sigmoid(x):
    return 1.0 / (1.0 + jnp.exp(-x))


def _silu(x):
    return x * _sigmoid(x)


def _rmsn(x):
    return x * lax.rsqrt(jnp.mean(x * x, axis=-1, keepdims=True) + NORM_EPS)


MM_VMEM_BUDGET = 44 * 1024 * 1024


def _divisor_tiles(n, cap):
    ts = [t for t in range(LANES_V7X, min(n, cap) + 1, LANES_V7X) if n % t == 0]
    return ts or [n]


def _mm_tiles(m, n, tk, out_bytes, has_acc):
    best = None
    for tm in _divisor_tiles(m, 1536):
        for tn in _divisor_tiles(n, 2560):
            need = 4 * tk * (tm + tn) + 2 * tm * tn * out_bytes + 4 * tm * tn
            if need > MM_VMEM_BUDGET:
                continue
            score = tm * tn / (tm + tn)
            for tdim in (tm, tn):
                if tdim % MXU_WIDTH_V7X:
                    score *= 0.85
            if best is None or score > best[0]:
                best = (score, tm, tn)
    return best[1], best[2]


def _mm(name, a, b, mode, out_dtype, b_off=0, n=None, after=None, stacked=False):
    half = 0
    if mode == "nn":
        m, k = a.shape
        n = b.shape[1] if n is None else n
        dims = NN_DIMS
    elif mode == "nt":
        if stacked:
            _, m, half = a.shape
            k = 2 * half
        else:
            m, k = a.shape
        n = b.shape[0]
        dims = NT_DIMS
    else:
        k, m = a.shape
        if stacked:
            half = b.shape[2]
            n = 2 * half
        else:
            n = b.shape[1]
        dims = TN_DIMS
    tk = _tile(half if (stacked and mode == "nt") else k, 2816)
    nk = k // tk
    n_tiled = half if (stacked and mode == "tn") else (math.gcd(n, b_off) if b_off else n)
    tm, tn = _mm_tiles(m, n_tiled, tk, jnp.dtype(out_dtype).itemsize, nk > 1)
    joff = b_off // tn
    per_half = (half // tk) if mode == "nt" else (half // tn)

    def body(a_ref, b_ref, *rest):
        o_ref = rest[0 if after is None else 1]
        if nk == 1:
            o_ref[...] = lax.dot_general(a_ref[...], b_ref[...], dims,
                                         preferred_element_type=F32).astype(o_ref.dtype)
            return
        acc_ref = rest[-1]
        kk = pl.program_id(2)

        @pl.when(kk == 0)
        def _():
            acc_ref[...] = jnp.zeros_like(acc_ref)

        acc_ref[...] += lax.dot_general(a_ref[...], b_ref[...], dims, preferred_element_type=F32)

        @pl.when(kk == nk - 1)
        def _():
            o_ref[...] = acc_ref[...].astype(o_ref.dtype)

    if mode == "nn":
        a_spec = pl.BlockSpec((tm, tk), lambda i, j, kk: (i, kk))
        b_spec = pl.BlockSpec((tk, tn), lambda i, j, kk: (kk, j + joff))
    elif mode == "nt":
        a_spec = pl.BlockSpec((tm, tk), lambda i, j, kk: (i, kk))
        if stacked:
            a_spec = pl.BlockSpec((None, tm, tk), lambda i, j, kk: (kk // per_half, i, kk % per_half))
        b_spec = pl.BlockSpec((tn, tk), lambda i, j, kk: (j, kk))
    else:
        a_spec = pl.BlockSpec((tk, tm), lambda i, j, kk: (kk, i))
        b_spec = pl.BlockSpec((tk, tn), lambda i, j, kk: (kk, j))
        if stacked:
            b_spec = pl.BlockSpec((None, tk, tn), lambda i, j, kk: (j // per_half, kk, j % per_half))
    return pl.pallas_call(
        body, name=name, grid=(m // tm, n // tn, nk),
        in_specs=[a_spec, b_spec] + ([] if after is None else [pl.BlockSpec(memory_space=pl.ANY)]),
        out_specs=pl.BlockSpec((tm, tn), lambda i, j, kk: (i, j)),
        out_shape=jax.ShapeDtypeStruct((m, n), out_dtype),
        scratch_shapes=[pltpu.VMEM((tm, tn), F32)] if nk > 1 else [],
        compiler_params=_params(("parallel", "parallel", "arbitrary")),
    )(*((a, b) if after is None else (a, b, after)))


def _mm_swiglu_bwd(name, dy, w_out, ua, ub, after=None):
    m, d_model = dy.shape
    f = w_out.shape[0]
    tm = _tile(m, 1024)
    tn = _tile(f, 512)

    def body(dy_ref, w_ref, ua_ref, ub_ref, *rest):
        du_ref = rest[-1]
        g = lax.dot_general(dy_ref[...], w_ref[...], NT_DIMS, preferred_element_type=F32)
        _, vjp = jax.vjp(lambda aa, bb: _silu(aa) * bb, ua_ref[...].astype(F32), ub_ref[...].astype(F32))
        da, db = vjp(g)
        du_ref[0] = da.astype(BF16)
        du_ref[1] = db.astype(BF16)

    tile = pl.BlockSpec((tm, tn), lambda i, j: (i, j))
    return pl.pallas_call(
        body, name=name, grid=(m // tm, f // tn),
        in_specs=[pl.BlockSpec((tm, d_model), lambda i, j: (i, 0)), pl.BlockSpec((tn, d_model), lambda i, j: (j, 0)),
                  tile, tile] + ([] if after is None else [pl.BlockSpec(memory_space=pl.ANY)]),
        out_specs=pl.BlockSpec((2, tm, tn), lambda i, j: (0, i, j)),
        out_shape=jax.ShapeDtypeStruct((2, m, f), BF16),
        compiler_params=_params(("parallel", "parallel")),
    )(*((dy, w_out, ua, ub) if after is None else (dy, w_out, ua, ub, after)))


def _mm_swiglu(name, a, w):
    m, k = a.shape
    f = w.shape[1] // 2
    tm = _tile(m, 1024)
    tn = _tile(f, 512)
    tk = _tile(k, 2560)
    nk = k // tk
    jf = f // tn

    def body(a_ref, wa_ref, wb_ref, h_ref, ua_ref, ub_ref, acca, accb):
        kk = pl.program_id(2)

        @pl.when(kk == 0)
        def _():
            acca[...] = jnp.zeros_like(acca)
            accb[...] = jnp.zeros_like(accb)

        av = a_ref[...]
        acca[...] += jnp.dot(av, wa_ref[...], preferred_element_type=F32)
        accb[...] += jnp.dot(av, wb_ref[...], preferred_element_type=F32)

        @pl.when(kk == nk - 1)
        def _():
            ua = acca[...]
            ub = accb[...]
            h_ref[...] = (_silu(ua) * ub).astype(BF16)
            ua_ref[...] = ua.astype(BF16)
            ub_ref[...] = ub.astype(BF16)

    o_spec = pl.BlockSpec((tm, tn), lambda i, j, kk: (i, j))
    o_shape = jax.ShapeDtypeStruct((m, f), BF16)
    return pl.pallas_call(
        body, name=name, grid=(m // tm, jf, nk),
        in_specs=[pl.BlockSpec((tm, tk), lambda i, j, kk: (i, kk)),
                  pl.BlockSpec((tk, tn), lambda i, j, kk: (kk, j)),
                  pl.BlockSpec((tk, tn), lambda i, j, kk: (kk, j + jf))],
        out_specs=[o_spec, o_spec, o_spec],
        out_shape=[o_shape, o_shape, o_shape],
        scratch_shapes=[pltpu.VMEM((tm, tn), F32), pltpu.VMEM((tm, tn), F32)],
        compiler_params=_params(("parallel", "parallel", "arbitrary")),
    )(a, w, w)


def _rowwise(name, fn, *, n_tiles, tr, row_ins, row_outs, sel_in=None, sel_off=0, ctx_rows=0,
             full_ins=(), acc_shape=None):
    sr = 256 if tr % 256 == 0 else (128 if tr % 128 == 0 else (32 if tr % 32 == 0 else tr))
    n_row, n_full, n_out = len(row_ins), len(full_ins), len(row_outs)
    has_sel = sel_in is not None
    has_acc = acc_shape is not None

    def sel_of(i):
        return jnp.where((i + sel_off) * tr < ctx_rows, 0, 1)

    def body(*refs):
        row_refs = refs[:n_row]
        pos = n_row
        sel_ref = None
        if has_sel:
            sel_ref = refs[pos]
            pos += 1
        full_refs = refs[pos:pos + n_full]
        pos += n_full
        out_refs = refs[pos:pos + n_out]
        pos += n_out
        acc_ref = refs[pos] if has_acc else None
        i = pl.program_id(0)
        if has_acc:
            first = (i == 0) | ((i + sel_off) * tr == ctx_rows)

            @pl.when(first)
            def _():
                acc_ref[...] = jnp.zeros_like(acc_ref)

        sel = (lambda kk: sel_ref[kk:kk + 1, :]) if has_sel else None
        fulls = [r[...] for r in full_refs] + [(i + sel_off) * tr < ctx_rows]

        def slab(r, carry):
            rs = pl.ds(pl.multiple_of(r * sr, sr), sr)
            rows = [ref[rs, :].astype(F32) for ref in row_refs]
            outs, accs = fn(rows, sel, fulls)
            for o_ref, o in zip(out_refs, outs):
                o_ref[rs, :] = o.astype(o_ref.dtype)
            for kk, a in enumerate(accs):
                acc_ref[kk:kk + 1, :a.shape[1]] += a
            return carry

        lax.fori_loop(0, tr // sr, slab, 0)

    def row_map(off, col=0):
        if off == "ctx":
            return lambda i: (jnp.minimum(i, ctx_rows // tr - 1), col)
        if off < 0:
            return lambda i: (jnp.maximum(i + off, 0), col)
        return lambda i: (i + off, col)

    in_specs, args = [], []
    for arr, off, blk in row_ins:
        if blk is None:
            in_specs.append(pl.BlockSpec((tr, arr.shape[1]), row_map(off)))
        else:
            in_specs.append(pl.BlockSpec((tr, blk[0]), row_map(off, blk[1])))
        args.append(arr)
    if has_sel:
        in_specs.append(pl.BlockSpec((None,) + sel_in.shape[1:], lambda i: (sel_of(i), 0, 0)))
        args.append(sel_in)
    for arr in full_ins:
        in_specs.append(pl.BlockSpec(arr.shape, lambda i: (0, 0)))
        args.append(arr)
    out_specs, out_shape = [], []
    for rows, cols, dt, off in row_outs:
        out_specs.append(pl.BlockSpec((tr, cols), row_map(off)))
        out_shape.append(jax.ShapeDtypeStruct((rows, cols), dt))
    if has_acc:
        out_specs.append(pl.BlockSpec((None,) + tuple(acc_shape), lambda i: (sel_of(i), 0, 0)))
        out_shape.append(jax.ShapeDtypeStruct((2,) + tuple(acc_shape), F32))
    return pl.pallas_call(
        body, name=name, grid=(n_tiles,), in_specs=in_specs, out_specs=out_specs, out_shape=out_shape,
        compiler_params=_params(("arbitrary",)),
    )(*args)


def _swap_pairs(x):
    lane = lax.broadcasted_iota(jnp.int32, x.shape, 1)
    nxt = pltpu.roll(x, x.shape[1] - 1, 1)
    prv = pltpu.roll(x, 1, 1)
    return jnp.where(lane % 2 == 0, nxt, prv)


def _heads_map(fn, arrs, width):
    outs = None
    for h in range(width // HEAD_DIM):
        sl = slice(h * HEAD_DIM, (h + 1) * HEAD_DIM)
        res = fn(*[a[:, sl] for a in arrs])
        if outs is None:
            outs = [[] for _ in res]
        for lst, r in zip(outs, res):
            lst.append(r)
    return [jnp.concatenate(lst, axis=1) if len(lst) > 1 else lst[0] for lst in outs]


QSCALE = HEAD_DIM ** -0.5 * math.log2(math.e)
LN2 = math.log(2.0)


def _lane_chunks(a):
    return [a[:, cc * LANES_V7X:(cc + 1) * LANES_V7X] for cc in range(a.shape[1] // LANES_V7X)]


def _row_bcast(col, like):
    return jnp.broadcast_to(col, like.shape)


def _flash_tiles(t, tk_all, key_pref):
    return _tile(t, 1024), _tile(tk_all, key_pref)


def _flash_fwd(q, k, vx, groups):
    t, aw = q.shape
    tk_all, kvw = k.shape
    kvh = kvw // HEAD_DIM
    gw = groups * HEAD_DIM
    tq, tk = _flash_tiles(t, tk_all, 1536)
    nk = tk_all // tk

    def body(q_ref, k_ref, v_ref, o_ref, lse_ref, m_sc, l_sc, acc_sc):
        j = pl.program_id(2)

        @pl.when(j == 0)
        def _():
            m_sc[...] = jnp.full_like(m_sc, -jnp.inf)
            l_sc[...] = jnp.zeros_like(l_sc)
            acc_sc[...] = jnp.zeros_like(acc_sc)

        kt = k_ref[...]
        vt = v_ref[...]
        for g in range(groups):
            sl = slice(g * HEAD_DIM, (g + 1) * HEAD_DIM)
            s = _lane_chunks(lax.dot_general(q_ref[:, sl], kt, NT_DIMS, preferred_element_type=F32))
            mx = functools.reduce(jnp.maximum, s)
            m_prev = m_sc[g]
            m_new = jnp.maximum(m_prev, _row_bcast(jnp.max(mx, axis=1, keepdims=True), mx))
            p = jnp.concatenate([jnp.exp2(sc - m_new).astype(BF16) for sc in s], axis=1)
            alpha = jnp.exp2(m_prev - m_new)
            pv = jnp.dot(p, vt, preferred_element_type=F32)
            acc_sc[g] = alpha * acc_sc[g] + pv[:, :HEAD_DIM]
            l_sc[g] = alpha * l_sc[g] + pv[:, HEAD_DIM:]
            m_sc[g] = m_new

        @pl.when(j == nk - 1)
        def _():
            for g in range(groups):
                sl = slice(g * HEAD_DIM, (g + 1) * HEAD_DIM)
                o_ref[:, sl] = (acc_sc[g] / l_sc[g]).astype(o_ref.dtype)
                lse_ref[:, sl] = m_sc[g] + jnp.log2(l_sc[g])

    qs = pl.BlockSpec((tq, gw), lambda kh, i, j: (i, kh))
    sc = pltpu.VMEM((groups, tq, HEAD_DIM), F32)
    return pl.pallas_call(
        body, name="flash_fwd", grid=(kvh, t // tq, nk),
        in_specs=[qs, pl.BlockSpec((tk, HEAD_DIM), lambda kh, i, j: (j, kh)),
                  pl.BlockSpec((tk, 2 * HEAD_DIM), lambda kh, i, j: (j, kh))],
        out_specs=[qs, qs],
        out_shape=[jax.ShapeDtypeStruct((t, aw), BF16), jax.ShapeDtypeStruct((t, aw), F32)],
        scratch_shapes=[sc, sc, sc],
        compiler_params=_params(("parallel", "parallel", "arbitrary")),
    )(q, k, vx)


def _flash_p_ds(q, kt, vt, do, lse, delta):
    s = _lane_chunks(lax.dot_general(q, kt, NT_DIMS, preferred_element_type=F32))
    dp = _lane_chunks(lax.dot_general(do, vt, NT_DIMS, preferred_element_type=F32))
    p = [jnp.exp2(sc - lse) for sc in s]
    ds = jnp.concatenate([(pc * (dc - delta)).astype(BF16) for pc, dc in zip(p, dp)], axis=1)
    return jnp.concatenate([pc.astype(BF16) for pc in p], axis=1), ds


def _flash_delta(do, o):
    prod = do.astype(F32) * o.astype(F32)
    return _row_bcast(jnp.sum(prod, axis=1, keepdims=True), prod)


def _flash_bwd(q, k, vx, o, do, lse, groups):
    t, aw = q.shape
    tk_all, kvw = k.shape
    kvh = kvw // HEAD_DIM
    gw = groups * HEAD_DIM
    tq, tk = _flash_tiles(t, tk_all, 1024)
    nq, nk = t // tq, tk_all // tk

    def body(q_ref, k_ref, v_ref, o_ref, do_ref, lse_ref, dq_ref, dk_ref, dv_ref, dq_sc, dk_acc, dv_acc):
        j = pl.program_id(1)
        i = pl.program_id(2)

        @pl.when(i == 0)
        def _():
            dk_acc[...] = jnp.zeros_like(dk_acc)
            dv_acc[...] = jnp.zeros_like(dv_acc)

        @pl.when(j == 0)
        def _():
            dq_sc[i] = jnp.zeros((groups, tq, HEAD_DIM), F32)

        kt = k_ref[...]
        vt = v_ref[:, :HEAD_DIM]
        for g in range(groups):
            sl = slice(g * HEAD_DIM, (g + 1) * HEAD_DIM)
            qv = q_ref[:, sl]
            dov = do_ref[:, sl]
            p, ds = _flash_p_ds(qv, kt, vt, dov, lse_ref[:, sl], _flash_delta(dov, o_ref[:, sl]))
            dv_acc[...] += lax.dot_general(p, dov, TN_DIMS, preferred_element_type=F32)
            dk_acc[...] += lax.dot_general(ds, qv, TN_DIMS, preferred_element_type=F32)
            dq_sc[i, g] += jnp.dot(ds, kt, preferred_element_type=F32)

        @pl.when(i == nq - 1)
        def _():
            dk_ref[...] = dk_acc[...] * LN2
            dv_ref[...] = dv_acc[...]

        @pl.when(j == nk - 1)
        def _():
            for g in range(groups):
                dq_ref[:, g * HEAD_DIM:(g + 1) * HEAD_DIM] = dq_sc[i, g]

    qs = pl.BlockSpec((tq, gw), lambda kh, j, i: (i, kh))
    ks = pl.BlockSpec((tk, HEAD_DIM), lambda kh, j, i: (j, kh))
    dq_spec = pl.BlockSpec((tq, gw), lambda kh, j, i: (jnp.where(j == nk - 1, i, 0), kh))
    return pl.pallas_call(
        body, name="flash_bwd", grid=(kvh, nk, nq),
        in_specs=[qs, ks, pl.BlockSpec((tk, 2 * HEAD_DIM), lambda kh, j, i: (j, kh)), qs, qs, qs],
        out_specs=[dq_spec, ks, ks],
        out_shape=[jax.ShapeDtypeStruct((t, aw), F32), jax.ShapeDtypeStruct((tk_all, kvw), F32),
                   jax.ShapeDtypeStruct((tk_all, kvw), F32)],
        scratch_shapes=[pltpu.VMEM((nq, groups, tq, HEAD_DIM), F32), pltpu.VMEM((tk, HEAD_DIM), F32),
                        pltpu.VMEM((tk, HEAD_DIM), F32)],
        compiler_params=_params(("parallel", "arbitrary", "arbitrary")),
    )(q, k, vx, o, do, lse)


def _bf_nn(a, b):
    return jnp.dot(a.astype(BF16), b.astype(BF16), preferred_element_type=F32)


def _bf_nt(a, b):
    return lax.dot_general(a.astype(BF16), b.astype(BF16), NT_DIMS, preferred_element_type=F32)


def _bf_tn(a, b):
    return lax.dot_general(a.astype(BF16), b.astype(BF16), TN_DIMS, preferred_element_type=F32)


@jax.custom_vjp
def _d_nn(a, b):
    return _bf_nn(a, b)


@jax.custom_vjp
def _d_nt(a, b):
    return _bf_nt(a, b)


@jax.custom_vjp
def _d_tn(a, b):
    return _bf_tn(a, b)


_d_nn.defvjp(lambda a, b: (_bf_nn(a, b), (a, b)), lambda r, g: (_d_nt(g, r[1]), _d_tn(r[0], g)))
_d_nt.defvjp(lambda a, b: (_bf_nt(a, b), (a, b)), lambda r, g: (_d_nn(g, r[1]), _d_tn(g, r[0])))
_d_tn.defvjp(lambda a, b: (_bf_tn(a, b), (a, b)), lambda r, g: (_d_nt(r[1], g), _d_nn(r[0], g)))


def _ret_chunk(q, k_raw, v, state, lg, rev, dots):
    nn, nt, tn = dots
    c = RET_CHUNK
    tcol = lax.broadcasted_iota(jnp.int32, (c, 1), 0).astype(F32)
    trow = lax.broadcasted_iota(jnp.int32, (1, c), 1).astype(F32)
    ucol = jnp.where(rev, c - 1.0 - tcol, tcol)
    urow = jnp.where(rev, c - 1.0 - trow, trow)
    e = ucol - urow
    low = e >= 0
    intra = jnp.where(low, jnp.exp(jnp.where(low, e, 0.0) * lg), 0.0)
    k = k_raw * (HEAD_DIM ** -0.5)
    inner = nt(q, k) * intra
    y = nn(inner, v) + nn(q, state) * jnp.exp((ucol + 1.0) * lg)
    new_state = state * jnp.exp(c * lg) + tn(k * jnp.exp((c - 1.0 - ucol) * lg), v)
    return y, new_state


def _ret_chunk_index(n_chunks, n_ctx_chunks):
    def idx(d, s):
        if d == 0:
            return s
        return jnp.where(s < n_ctx_chunks, n_ctx_chunks - 1 - s, n_chunks - 1 - s + n_ctx_chunks)
    return idx


def _ret_fwd(pr, lgv, ctx_rows):
    tk_all = pr.shape[0]
    rw = pr.shape[1] // 3
    nh = rw // HEAD_DIM
    nc = tk_all // RET_CHUNK
    cidx = _ret_chunk_index(nc, ctx_rows // RET_CHUNK)

    def body(pf_ref, pb_ref, lg_ref, yf_ref, yb_ref, st_ref, s_sc):
        s = pl.program_id(0)

        @pl.when(s == 0)
        def _():
            s_sc[...] = jnp.zeros_like(s_sc)

        for d, (p_ref, y_ref) in enumerate(((pf_ref, yf_ref), (pb_ref, yb_ref))):
            for h in range(nh):
                cols = [slice((part * nh + h) * HEAD_DIM, (part * nh + h + 1) * HEAD_DIM) for part in range(3)]
                state = s_sc[d, h]
                st_ref[d, h] = state
                y, new_state = _ret_chunk(p_ref[:, cols[0]], p_ref[:, cols[1]], p_ref[:, cols[2]], state,
                                          lg_ref[d, h][:, :1], d == 1, (_bf_nn, _bf_nt, _bf_tn))
                y_ref[:, h * HEAD_DIM:(h + 1) * HEAD_DIM] = y
                s_sc[d, h] = new_state

    y_shape = jax.ShapeDtypeStruct((tk_all, rw), F32)
    return pl.pallas_call(
        body, name="ret_fwd", grid=(nc,),
        in_specs=[pl.BlockSpec((RET_CHUNK, 3 * rw), lambda s: (cidx(0, s), 0)),
                  pl.BlockSpec((RET_CHUNK, 3 * rw), lambda s: (cidx(1, s), 0)),
                  pl.BlockSpec(lgv.shape, lambda s: (0, 0, 0, 0))],
        out_specs=[pl.BlockSpec((RET_CHUNK, rw), lambda s: (cidx(0, s), 0)),
                   pl.BlockSpec((RET_CHUNK, rw), lambda s: (cidx(1, s), 0)),
                   pl.BlockSpec((2, nh, None, HEAD_DIM, HEAD_DIM), lambda s: (0, 0, s, 0, 0))],
        out_shape=[y_shape, y_shape, jax.ShapeDtypeStruct((2, nh, nc, HEAD_DIM, HEAD_DIM), F32)],
        scratch_shapes=[pltpu.VMEM((2, nh, HEAD_DIM, HEAD_DIM), F32)],
        compiler_params=_params(("arbitrary",)),
    )(pr, pr, lgv)


def _ret_bwd(pr, states, dy, lgv, ctx_rows):
    tk_all = pr.shape[0]
    rw = pr.shape[1] // 3
    nh = rw // HEAD_DIM
    nc = tk_all // RET_CHUNK
    n_ctx = ctx_rows // RET_CHUNK
    cidx = _ret_chunk_index(nc, n_ctx)

    def body(pf_ref, pb_ref, st_ref, dyf_ref, dyb_ref, lg_ref, dpf_ref, dpb_ref, dlg_ref, ds_sc):
        sp = pl.program_id(0)
        on_ctx = [cidx(dd, nc - 1 - sp) < n_ctx for dd in (0, 1)]

        @pl.when(sp == 0)
        def _():
            ds_sc[...] = jnp.zeros_like(ds_sc)
            dlg_ref[...] = jnp.zeros_like(dlg_ref)

        for d, (p_ref, dy_ref, dp_ref) in enumerate(((pf_ref, dyf_ref, dpf_ref), (pb_ref, dyb_ref, dpb_ref))):
            for h in range(nh):
                cols = [slice((part * nh + h) * HEAD_DIM, (part * nh + h + 1) * HEAD_DIM) for part in range(3)]

                def step(q, k, v, state, lg, rev=(d == 1)):
                    return _ret_chunk(q, k, v, state, lg, rev, (_d_nn, _d_nt, _d_tn))

                _, vjp = jax.vjp(step, p_ref[:, cols[0]], p_ref[:, cols[1]], p_ref[:, cols[2]], st_ref[d, h],
                                 lg_ref[d, h][:, :1])
                dy_h = jnp.where(on_ctx[d], 0.0, dy_ref[:, h * HEAD_DIM:(h + 1) * HEAD_DIM])
                grads = vjp((dy_h, ds_sc[d, h]))
                for part in range(3):
                    dp_ref[:, cols[part]] = grads[part]
                ds_sc[d, h] = grads[3]
                dlg_ref[d, h] += jnp.broadcast_to(grads[4], (1, HEAD_DIM))

    def at(d):
        return lambda sp: (cidx(d, nc - 1 - sp), 0)

    def dy_at(d):
        return lambda sp: (jnp.maximum(cidx(d, nc - 1 - sp) - n_ctx, 0), 0)

    dp_shape = jax.ShapeDtypeStruct((tk_all, 3 * rw), F32)
    lg_spec = pl.BlockSpec(lgv.shape, lambda sp: (0, 0, 0, 0))
    return pl.pallas_call(
        body, name="ret_bwd", grid=(nc,),
        in_specs=[pl.BlockSpec((RET_CHUNK, 3 * rw), at(0)), pl.BlockSpec((RET_CHUNK, 3 * rw), at(1)),
                  pl.BlockSpec((2, nh, None, HEAD_DIM, HEAD_DIM), lambda sp: (0, 0, nc - 1 - sp, 0, 0)),
                  pl.BlockSpec((RET_CHUNK, rw), dy_at(0)), pl.BlockSpec((RET_CHUNK, rw), dy_at(1)), lg_spec],
        out_specs=[pl.BlockSpec((RET_CHUNK, 3 * rw), at(0)), pl.BlockSpec((RET_CHUNK, 3 * rw), at(1)), lg_spec],
        out_shape=[dp_shape, dp_shape, jax.ShapeDtypeStruct(lgv.shape, F32)],
        scratch_shapes=[pltpu.VMEM((2, nh, HEAD_DIM, HEAD_DIM), F32)],
        compiler_params=_params(("arbitrary",)),
    )(pr, pr, states, dy, dy, lgv)


FLIP_X, FLIP_Y, FLIP_XY, FLIP_C = (1, 0, 0), (0, 1, 0), (1, 1, 0), (0, 0, 1)
CHIP_FLIPS = ((FLIP_X, 2), (FLIP_Y, 1), (FLIP_XY, 3))


def _flip(me, mask):
    return tuple(1 - v if m else v for v, m in zip(me, mask))


def _comm(name, ins, out_shapes, plan, n_remote, n_local, aliases=None):
    n_in, n_out = len(ins), len(out_shapes)

    def body(*refs):
        in_refs = refs[:n_in]
        out_refs = refs[n_in:n_in + n_out]
        send_sems, recv_sems, local_sems = refs[n_in + n_out:]
        me = (lax.axis_index("x"), lax.axis_index("y"), lax.axis_index("c"))
        local, phases = plan(in_refs, out_refs, me)
        local_copies = [pltpu.make_async_copy(s, d, local_sems.at[i]) for i, (s, d) in enumerate(local)]
        for cp in local_copies:
            cp.start()
        sent = []
        kk = 0
        for phase in phases:
            arrivals = []
            for mask, src, dst, landing in phase:
                peer = _flip(me, mask)
                cp = pltpu.make_async_remote_copy(src_ref=src, dst_ref=dst, send_sem=send_sems.at[kk],
                                                  recv_sem=recv_sems.at[kk], device_id=peer,
                                                  device_id_type=pl.DeviceIdType.MESH)
                cp.start()
                sent.append(cp)
                arrivals.append(pltpu.make_async_remote_copy(
                    src_ref=landing, dst_ref=landing, send_sem=send_sems.at[kk], recv_sem=recv_sems.at[kk],
                    device_id=peer, device_id_type=pl.DeviceIdType.MESH))
                kk += 1
            for cp in arrivals:
                cp.wait_recv()
        for cp in sent:
            cp.wait_send()
        for cp in local_copies:
            cp.wait()

    any_spec = pl.BlockSpec(memory_space=pl.ANY)
    return pl.pallas_call(
        body, name=name,
        in_specs=[any_spec] * n_in, out_specs=[any_spec] * n_out, out_shape=list(out_shapes),
        scratch_shapes=[pltpu.SemaphoreType.DMA((n_remote,)), pltpu.SemaphoreType.DMA((n_remote,)),
                        pltpu.SemaphoreType.DMA((max(n_local, 1),))],
        input_output_aliases=aliases or {},
    )(*ins)


def _ds(start, size):
    return pl.ds(pl.multiple_of(start * size, 8), size)


def _all_gather8(name, v):
    masks = [(a, b, cc) for a in (0, 1) for b in (0, 1) for cc in (0, 1)][1:]

    def index(p):
        return 4 * p[0] + 2 * p[1] + p[2]

    def plan(in_refs, out_refs, me):
        (src,), (out,) = in_refs, out_refs
        local = [(src, out.at[index(me)])]
        phase = [(m, src, out.at[index(me)], out.at[index(_flip(me, m))]) for m in masks]
        return local, [phase]

    return _comm(name, [v], [jax.ShapeDtypeStruct((N_DEV,) + v.shape, v.dtype)], plan, len(masks), 1)[0]


def _gather_row(name, row):
    n = row.shape[1]
    n_pad = -(-n // (8 * LANES_V7X)) * (8 * LANES_V7X)
    v = jnp.pad(row, ((0, 0), (0, n_pad - n))).reshape(8, n_pad // 8)
    return _all_gather8(name, v).reshape(N_DEV, n_pad)[:, :n]


class _Sharded:
    def __init__(self, kind, rows, cols):
        self.kind, self.rows, self.cols = kind, rows, cols
        self.shard_shape = (rows, cols // N_CHIP) if kind == "col" else (rows // N_CHIP, cols)
        self.piece_shape = (rows // 2, cols // N_CHIP) if kind == "col" else (rows // N_CHIP, cols // 2)

    def piece_of_full(self, ref, s, h):
        if self.kind == "col":
            return ref.at[_ds(h, self.rows // 2), _ds(s, self.cols // N_CHIP)]
        return ref.at[_ds(s, self.rows // N_CHIP), _ds(h, self.cols // 2)]

    def half_of_shard(self, ref, h):
        if self.kind == "col":
            return ref.at[_ds(h, self.rows // 2), :]
        return ref.at[:, _ds(h, self.cols // 2)]


def _place_shard(meta, w, s_arr, after=None):
    r, cols = w.shape
    tr = _tile(r, 256, 16)
    nr = r // tr

    def body(s_ref, w_ref, *rest):
        rest[-1][...] = w_ref[...].astype(BF16)

    if meta.kind == "col":
        o_map = lambda i, s_ref: (i, s_ref[0])
    else:
        o_map = lambda i, s_ref: (i + s_ref[0] * nr, 0)
    return pl.pallas_call(
        body, name="place_shard",
        grid_spec=pltpu.PrefetchScalarGridSpec(
            num_scalar_prefetch=1, grid=(nr,),
            in_specs=[pl.BlockSpec((tr, cols), lambda i, s_ref: (i, 0))]
            + ([] if after is None else [pl.BlockSpec(memory_space=pl.ANY)]),
            out_specs=pl.BlockSpec((tr, cols), o_map)),
        out_shape=jax.ShapeDtypeStruct((meta.rows, meta.cols), BF16),
        compiler_params=_params(("parallel",)),
    )(*((s_arr, w) if after is None else (s_arr, w, after)))


def _gather_copies(metas, over_ici):
    def copies(fulls, me):
        x, y, c = me
        s_me = 2 * x + y
        out = []
        for meta, full in zip(metas, fulls):
            for mask, bits in CHIP_FLIPS:
                s_peer = jnp.bitwise_xor(s_me, bits)
                if over_ici:
                    out.append((mask, meta.piece_of_full(full, s_me, c), meta.piece_of_full(full, s_me, c),
                                meta.piece_of_full(full, s_peer, c)))
                else:
                    out.append((FLIP_C, meta.piece_of_full(full, s_peer, c), meta.piece_of_full(full, s_peer, c),
                                meta.piece_of_full(full, s_peer, 1 - c)))
        return out
    return copies


def _gather_forward(name, metas, fulls):
    nt = len(metas)
    copies = _gather_copies(metas, False)
    outs = [jax.ShapeDtypeStruct((m.rows, m.cols), BF16) for m in metas]
    return _comm(name, list(fulls), outs, lambda ins, outs_, me: ([], [copies(outs_, me)]), 3 * nt, 0,
                 aliases={i: i for i in range(nt)})


HBM_SPEC = pl.BlockSpec(memory_space=pltpu.HBM)
SEM_SPEC = pl.BlockSpec(memory_space=pltpu.SEMAPHORE)
SPLIT_EFFECT = pltpu.SideEffectType.DATAFLOW_SIDE_EFFECTING


def _split_start(name, bufs, groups, after):
    nb, ng = len(bufs), len(groups)
    n_in = nb + (0 if after is None else 1)

    def body(*refs):
        buf_refs = refs[:nb]
        sem_refs = refs[n_in:n_in + 2 * ng]
        token = refs[-1]
        me = (lax.axis_index("x"), lax.axis_index("y"), lax.axis_index("c"))
        for gi, (lo, n_bufs, copies, _) in enumerate(groups):
            for kk, (mask, src, dst, _) in enumerate(copies(buf_refs[lo:lo + n_bufs], me)):
                pltpu.make_async_remote_copy(src_ref=src, dst_ref=dst, send_sem=sem_refs[2 * gi].at[kk],
                                             recv_sem=sem_refs[2 * gi + 1].at[kk], device_id=_flip(me, mask),
                                             device_id_type=pl.DeviceIdType.MESH).start()
        token[...] = jnp.zeros_like(token)

    out_shape = []
    for _, _, _, n in groups:
        out_shape += [pltpu.SemaphoreType.DMA((n,)), pltpu.SemaphoreType.DMA((n,))]
    out_shape += [pltpu.HBM(b.shape, b.dtype) for b in bufs] + [jax.ShapeDtypeStruct((8, LANES_V7X), F32)]
    res = pl.pallas_call(
        body, name=name, out_shape=tuple(out_shape),
        in_specs=(HBM_SPEC,) * nb + (pl.BlockSpec(memory_space=pl.ANY),) * (n_in - nb),
        out_specs=(SEM_SPEC,) * (2 * ng) + (HBM_SPEC,) * nb + (pl.BlockSpec(memory_space=pltpu.VMEM),),
        input_output_aliases={i: 2 * ng + i for i in range(nb)},
        compiler_params=pltpu.CompilerParams(has_side_effects=SPLIT_EFFECT),
    )(*[pltpu.with_memory_space_constraint(b, pltpu.HBM) for b in bufs], *([] if after is None else [after]))
    sems = [(res[2 * gi], res[2 * gi + 1]) for gi in range(ng)]
    return sems, list(res[2 * ng:2 * ng + nb]), res[-1]


def _split_wait(name, sems, bufs, copies, after):
    nb = len(bufs)

    def body(*refs):
        buf_refs = refs[:nb]
        send_sems, recv_sems = refs[nb], refs[nb + 1]
        me = (lax.axis_index("x"), lax.axis_index("y"), lax.axis_index("c"))
        for kk, (mask, _, _, landing) in enumerate(copies(buf_refs, me)):
            cp = pltpu.make_async_remote_copy(src_ref=landing, dst_ref=landing, send_sem=send_sems.at[kk],
                                              recv_sem=recv_sems.at[kk], device_id=_flip(me, mask),
                                              device_id_type=pl.DeviceIdType.MESH)
            cp.wait_send()
            cp.wait_recv()

    return list(pl.pallas_call(
        body, name=name, out_shape=tuple(pltpu.HBM(b.shape, b.dtype) for b in bufs),
        in_specs=(HBM_SPEC,) * nb + (SEM_SPEC, SEM_SPEC, pl.BlockSpec(memory_space=pl.ANY)),
        out_specs=(HBM_SPEC,) * nb,
        input_output_aliases={i: i for i in range(nb)},
        compiler_params=pltpu.CompilerParams(has_side_effects=SPLIT_EFFECT),
    )(*bufs, sems[0], sems[1], after))


N_REDUCE_PIECES = 7


def _reduce_copies(metas):
    def copies(refs, me):
        x, y, c = me
        s_me = 2 * x + y
        out = []
        for m, g, land in zip(metas, refs[:len(metas)], refs[len(metas):]):
            for kk, (mask, bits) in enumerate(CHIP_FLIPS):
                s_peer = jnp.bitwise_xor(s_me, bits)
                out.append((mask, m.piece_of_full(g, s_peer, c), land.at[kk], land.at[kk]))
                out.append((mask[:2] + (1,), m.piece_of_full(g, s_peer, 1 - c), land.at[3 + kk], land.at[3 + kk]))
            out.append((FLIP_C, m.piece_of_full(g, s_me, 1 - c), land.at[6], land.at[6]))
        return out
    return copies


def _reduce_start(name, metas, grads):
    lands = [lax.empty((N_REDUCE_PIECES,) + m.piece_shape, BF16) for m in metas]
    bufs = list(grads) + lands
    sems, thru, token = _split_start(name, bufs, [(0, len(bufs), _reduce_copies(metas),
                                                   N_REDUCE_PIECES * len(metas))], None)
    return sems[0], thru, token


def _reduce_wait(name, metas, sems, thru, after):
    done = _split_wait(name, sems, thru, _reduce_copies(metas), after)
    return done[:len(metas)], done[len(metas):]


def _share_halves(metas, shards):
    def plan(in_refs, out_refs, me):
        c = me[2]
        phase = [(FLIP_C, m.half_of_shard(g, c), m.half_of_shard(g, c), m.half_of_shard(g, 1 - c))
                 for m, g in zip(metas, out_refs)]
        return [], [phase]

    outs = [jax.ShapeDtypeStruct(m.shard_shape, F32) for m in metas]
    return _comm("share_halves", list(shards), outs, plan, len(metas), 0,
                 aliases={i: i for i in range(len(metas))})


def _sum_pieces(meta, grad, landed, s_arr, c_arr):
    pr, pc = meta.piece_shape
    tr = _tile(pr, 256, 16)
    tc = _tile(pc, 2048)
    nr, ncol = pr // tr, pc // tc

    def body(s_ref, c_ref, p_ref, l_ref, o_ref):
        acc = p_ref[...].astype(F32)
        for kk in range(N_REDUCE_PIECES):
            acc = acc + l_ref[kk].astype(F32)
        o_ref[...] = acc

    if meta.kind == "col":
        p_map = lambda i, j, s_ref, c_ref: (i + c_ref[0] * nr, j + s_ref[0] * ncol)
        o_map = lambda i, j, s_ref, c_ref: (i + c_ref[0] * nr, j)
    else:
        p_map = lambda i, j, s_ref, c_ref: (i + s_ref[0] * nr, j + c_ref[0] * ncol)
        o_map = lambda i, j, s_ref, c_ref: (i, j + c_ref[0] * ncol)
    blk = (tr, tc)
    return pl.pallas_call(
        body, name="sum_pieces",
        grid_spec=pltpu.PrefetchScalarGridSpec(
            num_scalar_prefetch=2, grid=(nr, ncol),
            in_specs=[pl.BlockSpec(blk, p_map),
                      pl.BlockSpec((N_REDUCE_PIECES,) + blk, lambda i, j, s_ref, c_ref: (0, i, j))],
            out_specs=pl.BlockSpec(blk, o_map)),
        out_shape=jax.ShapeDtypeStruct(meta.shard_shape, F32),
        compiler_params=_params(("parallel", "parallel")),
    )(s_arr, c_arr, grad, landed)


def _adam_rows(rows, sel, fulls):
    w, g, m, v = rows
    m2 = ADAM_B1 * m + (1.0 - ADAM_B1) * g
    v2 = ADAM_B2 * v + (1.0 - ADAM_B2) * jnp.square(g)
    m_hat = m2 / (1.0 - ADAM_B1 ** ADAM_STEP)
    v_hat = v2 / (1.0 - ADAM_B2 ** ADAM_STEP)
    delta = -ADAM_LR * (m_hat / (jnp.sqrt(v_hat) + ADAM_EPS) + ADAM_WD * w)
    return [delta, m2, v2], []


def _adamw(w, g, m, v):
    r, c = w.shape
    tr = _tile(r, 128, 8)
    outs = _rowwise("adamw", _adam_rows, n_tiles=r // tr, tr=tr,
                    row_ins=[(w, 0, None), (g, 0, None), (m, 0, None), (v, 0, None)],
                    row_outs=[(r, c, F32, 0)] * 3)
    return outs[0], outs[1], outs[2]


def _ada_fwd(cg, w, b):
    d, n = w.shape
    tn = _tile(n, 512)

    def body(c_ref, w_ref, b_ref, o_ref):
        a = _silu(c_ref[...]).astype(BF16)
        o_ref[...] = jnp.dot(a, w_ref[...].astype(BF16), preferred_element_type=F32) + b_ref[...]

    return pl.pallas_call(
        body, name="ada_fwd", grid=(n // tn,),
        in_specs=[pl.BlockSpec(cg.shape, lambda j: (0, 0)), pl.BlockSpec((d, tn), lambda j: (0, j)),
                  pl.BlockSpec((1, tn), lambda j: (0, j))],
        out_specs=pl.BlockSpec((cg.shape[0], tn), lambda j: (0, j)),
        out_shape=jax.ShapeDtypeStruct((cg.shape[0], n), F32),
        compiler_params=_params(("parallel",)),
    )(cg, w, b)


def _ada_bwd(cg, dm, w):
    d, n = w.shape
    tn = _tile(n, 512)
    nj = n // tn

    def body(c_ref, dm_ref, w_ref, gw_ref, da_ref, acc):
        j = pl.program_id(0)

        @pl.when(j == 0)
        def _():
            acc[...] = jnp.zeros_like(acc)

        a = _silu(c_ref[...]).astype(BF16)
        dmv = dm_ref[...].astype(BF16)
        gw_ref[...] = lax.dot_general(a, dmv, TN_DIMS, preferred_element_type=F32)
        acc[...] += lax.dot_general(dmv, w_ref[...].astype(BF16), NT_DIMS, preferred_element_type=F32)

        @pl.when(j == nj - 1)
        def _():
            da_ref[...] = acc[...]

    return pl.pallas_call(
        body, name="ada_bwd", grid=(nj,),
        in_specs=[pl.BlockSpec(cg.shape, lambda j: (0, 0)), pl.BlockSpec((dm.shape[0], tn), lambda j: (0, j)),
                  pl.BlockSpec((d, tn), lambda j: (0, j))],
        out_specs=[pl.BlockSpec((d, tn), lambda j: (0, j)), pl.BlockSpec(cg.shape, lambda j: (0, 0))],
        out_shape=[jax.ShapeDtypeStruct((d, n), F32), jax.ShapeDtypeStruct(cg.shape, F32)],
        scratch_shapes=[pltpu.VMEM(cg.shape, F32)],
        compiler_params=_params(("arbitrary",)),
    )(cg, dm, w)


def _small_reduce(gathered, logits, n_mod_cols, lg_off, loss_off, loss_cols):
    npk = gathered.shape[1]

    def body(g_ref, lo_ref, tot_ref, gb_ref, gl_ref, loss_ref):
        acc = g_ref[0:1, :]
        for dd in range(1, N_DEV):
            acc = acc + g_ref[dd:dd + 1, :]
        tot_ref[...] = acc
        gb_ref[...] = acc[:, :n_mod_cols] + acc[:, n_mod_cols:2 * n_mod_cols]
        gl_ref[...] = acc[:, lg_off:lg_off + LANES_V7X] * _sigmoid(-lo_ref[...])
        loss = jnp.sum(acc[:, loss_off:loss_off + loss_cols], axis=1, keepdims=True)
        loss_ref[...] = jnp.broadcast_to(loss, loss_ref.shape)

    lane = jax.ShapeDtypeStruct((1, LANES_V7X), F32)
    return pl.pallas_call(
        body, name="small_reduce",
        out_shape=[jax.ShapeDtypeStruct((1, npk), F32), jax.ShapeDtypeStruct((1, n_mod_cols), F32), lane, lane],
    )(gathered, logits)


def _c_ctx_grad(parts, c_ctx):
    def body(p_ref, c_ref, o_ref):
        tot = p_ref[0:1, :] + p_ref[2:3, :] + p_ref[4:5, :] + p_ref[6:7, :]
        _, vjp = jax.vjp(_silu, c_ref[...])
        o_ref[...] = vjp(tot)[0]

    return pl.pallas_call(body, name="c_ctx_grad", out_shape=jax.ShapeDtypeStruct(c_ctx.shape, F32))(parts, c_ctx)


def _rope_tables(seq, ctx_rows):
    rows = seq // GRID_W
    half = HEAD_DIM // 2
    inv_freq = ROPE_THETA ** (-jnp.arange(0, half, 2, dtype=F32) / half)
    ang_row = jnp.arange(rows, dtype=F32)[:, None] * inv_freq
    ang_col = jnp.arange(GRID_W, dtype=F32)[:, None] * inv_freq

    def spread(fn):
        return jnp.concatenate([jnp.repeat(fn(ang_row), GRID_W, axis=0), jnp.tile(fn(ang_col), (rows, 1))], axis=-1)

    cos, sin = spread(jnp.cos), spread(jnp.sin)
    cos_full = jnp.repeat(cos, 2, axis=1)
    sin_signed = jnp.stack([-sin, sin], axis=-1).reshape(seq, HEAD_DIM)
    cos_full = jnp.concatenate([jnp.ones((ctx_rows, HEAD_DIM), F32), cos_full], axis=0)
    sin_signed = jnp.concatenate([jnp.zeros((ctx_rows, HEAD_DIM), F32), sin_signed], axis=0)
    return cos_full, sin_signed


def _qk_rot(p, gain, cos_full, sin_signed):
    r = _rmsn(p) * gain
    return r * cos_full + _swap_pairs(r) * sin_signed


def _qk_rot_bwd(g, p, gain, cos_full, sin_signed):
    g1 = g * cos_full + _swap_pairs(g * sin_signed)
    _, vjp = jax.vjp(lambda pp, gn: _rmsn(pp) * gn, p, gain)
    return vjp(g1)


def kernel(x, c, ctx, c_ctx, w_ada, b_ada, ffn1_w_in, ffn1_w_out, mix_w_in, attn_q_gain, attn_k_gain, ret_decay_logit, w_proj_attn, w_proj_ret, mix_w_out, ffn2_w_in, ffn2_w_out, final_norm, loss_target, m_c_ctx, m_w_ada, m_b_ada, m_ffn1_w_in, m_ffn1_w_out, m_mix_w_in, m_attn_q_gain, m_attn_k_gain, m_ret_decay_logit, m_w_proj_attn, m_w_proj_ret, m_mix_w_out, m_ffn2_w_in, m_ffn2_w_out, m_final_norm, v_c_ctx, v_w_ada, v_b_ada, v_ffn1_w_in, v_ffn1_w_out, v_mix_w_in, v_attn_q_gain, v_attn_k_gain, v_ret_decay_logit, v_w_proj_attn, v_w_proj_ret, v_mix_w_out, v_ffn2_w_in, v_ffn2_w_out, v_final_norm):
    xi, yi, ci = lax.axis_index("x"), lax.axis_index("y"), lax.axis_index("c")
    dev = 4 * xi + 2 * yi + ci
    s_me = 2 * xi + yi
    c_arr = jnp.reshape(ci, (1,)).astype(jnp.int32)
    s_arr = jnp.reshape(s_me, (1,)).astype(jnp.int32)

    t, d = x.shape[1], x.shape[2]
    tc = ctx.shape[1]
    tk = tc + t
    aw = w_proj_attn.shape[1]
    rw = w_proj_ret.shape[1]
    pw = mix_w_in.shape[2] * N_CHIP
    kvw = (pw - aw - 4 * rw - 2 * d) // 2
    groups = aw // kvw
    n_ret_heads = rw // HEAD_DIM
    mod_cols = N_MOD * d
    tr = _tile(tc, 256, 32)
    nt_all, nt_x, ctx_tiles = tk // tr, t // tr, tc // tr

    c_rows = _gather_row("gather_c", c)
    cg = jnp.concatenate([c_rows, c_ctx[None, :], jnp.zeros((7, d), F32)], axis=0)
    w_ada_l = w_ada[0]
    ada_cols = w_ada_l.shape[1]
    b_ada_l = lax.dynamic_slice_in_dim(b_ada, s_me * ada_cols, ada_cols, axis=1)
    mod_shard = _ada_fwd(cg, w_ada_l, b_ada_l)
    mod_g = _all_gather8("gather_mod", mod_shard)
    mod_full = jnp.concatenate([mod_g[0], mod_g[2], mod_g[4], mod_g[6]], axis=1)
    mod_x = lax.dynamic_slice_in_dim(mod_full, dev, 1, axis=0).reshape(N_MOD, d)
    mod_c = mod_full[8].reshape(N_MOD, d)
    mods = jnp.stack([mod_c, mod_x])

    big = [("col", ffn1_w_in), ("row", ffn1_w_out), ("col", mix_w_in), ("col", w_proj_attn), ("col", w_proj_ret),
           ("row", mix_w_out), ("col", ffn2_w_in), ("row", ffn2_w_out)]
    metas = []
    for kind, w in big:
        r_l, c_l = w.shape[1:]
        metas.append(_Sharded(kind, r_l, c_l * N_CHIP) if kind == "col" else _Sharded(kind, r_l * N_CHIP, c_l))
    layer_groups = ((0, 1), (1, 2), (2, 6), (6, 8))
    sems_first, placed_first, token = _split_start(
        "gather_start_first", [_place_shard(metas[0], big[0][1][0], s_arr)],
        [(0, 1, _gather_copies(metas[0:1], True), 3)], mods)
    placed_rest = [_place_shard(m, w[0], s_arr, after=token) for m, (_, w) in zip(metas[1:], big[1:])]
    sems_rest, placed_rest, token = _split_start(
        "gather_start_rest", placed_rest,
        [(lo - 1, hi - lo, _gather_copies(metas[lo:hi], True), 3 * (hi - lo)) for lo, hi in layer_groups[1:]], None)
    gather_sems, placed = sems_first + sems_rest, placed_first + placed_rest
    mods = mods + token[0, 0]

    def weights_of(gi, after):
        lo, hi = layer_groups[gi]
        arrived = _split_wait("gather_wait_%d" % gi, gather_sems[gi], placed[lo:hi],
                              _gather_copies(metas[lo:hi], True), after)
        return _gather_forward("gather_forward_%d" % gi, metas[lo:hi], arrived)

    def weights_early(gi, after):
        lo, hi = layer_groups[gi]
        arrived = _split_wait("gather_wait_%d" % gi, gather_sems[gi], placed[lo:hi],
                              _gather_copies(metas[lo:hi], True), after)
        sems, thru, tok = _split_start("gather_forward_start_%d" % gi, arrived,
                                       [(0, hi - lo, _gather_copies(metas[lo:hi], False), 3 * (hi - lo))], None)
        return gi, sems[0], thru, tok

    def weights_late(early, after):
        gi, sems, thru, _ = early
        lo, hi = layer_groups[gi]
        return _split_wait("gather_forward_wait_%d" % gi, sems, thru, _gather_copies(metas[lo:hi], False), after)

    cos_full, sin_signed = _rope_tables(t, tc)
    q_gain, k_gain = attn_q_gain, attn_k_gain
    log_gamma = jax.nn.log_sigmoid(ret_decay_logit[0])
    lgv = jnp.broadcast_to(log_gamma[:, :, None, None], (2, n_ret_heads, 1, HEAD_DIM))

    def stream_rows(h, off):
        if isinstance(h, tuple):
            return [(h[0], "ctx", None), (h[1], -ctx_tiles, None)]
        return [(h, off, None)]

    def stream_value(h, rows, fulls):
        if isinstance(h, tuple):
            return jnp.where(fulls[-1], rows[0], rows[1]), rows[2:]
        return rows[0], rows[1:]

    def norm_mod(name, h, n_tiles, off, i_shift, i_scale):
        def fn(rows, sel, fulls):
            hv, _ = stream_value(h, rows, fulls)
            return [_rmsn(hv) * (1.0 + sel(i_scale)) + sel(i_shift)], []
        return _rowwise(name, fn, n_tiles=n_tiles, tr=tr, row_ins=stream_rows(h, 0),
                        row_outs=[(n_tiles * tr, d, BF16, 0)], sel_in=mods, sel_off=off, ctx_rows=tc)[0]

    def resid_norm(name, h, h_off, f, n_tiles, off, i_gate, coef, i_shift, i_scale):
        def fn(rows, sel, fulls):
            hv, rest = stream_value(h, rows, fulls)
            hn = hv + coef * sel(i_gate) * rest[0]
            return [hn, _rmsn(hn) * (1.0 + sel(i_scale)) + sel(i_shift)], []
        return _rowwise(name, fn, n_tiles=n_tiles, tr=tr, row_ins=stream_rows(h, h_off) + [(f, 0, None)],
                        row_outs=[(f.shape[0], d, F32, 0), (f.shape[0], d, BF16, 0)], sel_in=mods, sel_off=off,
                        ctx_rows=tc)

    h0 = (ctx[0], x[0])
    n1 = norm_mod("norm_mod1", h0, nt_all, 0, 0, 1)
    w1i, = weights_of(0, n1)
    hm1, ua1, ub1 = _mm_swiglu("ffn1_in", n1, w1i)
    w1o, = weights_of(1, hm1)
    f1 = _mm("ffn1_out", hm1, w1o, "nn", BF16)
    mixer_weights = weights_early(2, f1)
    mods = mods + mixer_weights[3][0, 0]
    h1, n2 = resid_norm("resid_norm1", h0, 0, f1, nt_all, 0, 2, 0.5, 3, 4)
    wmi, wpa, wpr, wmo = weights_late(mixer_weights, n2)
    p_q = _mm("mix_in_q", n2, wmi, "nn", F32, 0, aw)
    p_kv = _mm("mix_in_kv", n2, wmi, "nn", F32, aw, 2 * kvw)
    p_r = _mm("mix_in_ret", n2, wmi, "nn", F32, aw + 2 * kvw, 3 * rw)
    p_gr = _mm("mix_in_gr", n2, wmi, "nn", BF16, aw + 2 * kvw + 3 * rw, rw)
    p_gab = _mm("mix_in_gab", n2, wmi, "nn", BF16, aw + 2 * kvw + 4 * rw, 2 * d)
    ffn2_weights = weights_early(3, p_gab)
    q_gain = q_gain + ffn2_weights[3][0, 0]

    def q_prep(rows, sel, fulls):
        p, cf, ss = rows
        return _heads_map(lambda ph: [_qk_rot(ph, fulls[0], cf, ss) * QSCALE], [p], aw), []

    q_rot = _rowwise("q_prep", q_prep, n_tiles=nt_x, tr=tr,
                     row_ins=[(p_q, ctx_tiles, None), (cos_full, ctx_tiles, None), (sin_signed, ctx_tiles, None)],
                     row_outs=[(t, aw, BF16, 0)], full_ins=[q_gain])[0]

    def kv_prep(rows, sel, fulls):
        p, cf, ss = rows
        k_rot = _heads_map(lambda ph: [_qk_rot(ph, fulls[0], cf, ss)], [p[:, :kvw]], kvw)[0]
        v_ones = _heads_map(lambda vh: [jnp.concatenate([vh, jnp.ones_like(vh)], axis=1)], [p[:, kvw:]], kvw)[0]
        return [k_rot, v_ones], []

    k_rot, v_att = _rowwise("kv_prep", kv_prep, n_tiles=nt_all, tr=tr,
                            row_ins=[(p_kv, 0, None), (cos_full, 0, None), (sin_signed, 0, None)],
                            row_outs=[(tk, kvw, BF16, 0), (tk, 2 * kvw, BF16, 0)], full_ins=[k_gain])

    ya, lse = _flash_fwd(q_rot, k_rot, v_att, groups)
    y_fwd, y_bwd, states = _ret_fwd(p_r, lgv, tc)

    def ret_out_fn(yf, yb, gr):
        return [_silu(gr) * _rmsn(yf + yb)]

    def ret_out(rows, sel, fulls):
        return _heads_map(ret_out_fn, rows, rw), []

    y_rows = [(y_fwd, ctx_tiles, None), (y_bwd, ctx_tiles, None), (p_gr, ctx_tiles, None)]
    yr = _rowwise("ret_out", ret_out, n_tiles=nt_x, tr=tr, row_ins=y_rows, row_outs=[(t, rw, BF16, 0)])[0]

    pa = _mm("proj_attn", ya, wpa, "nn", BF16)
    prj = _mm("proj_ret", yr, wpr, "nn", BF16)

    def merge_fn(a, r, ga, gb):
        return _sigmoid(ga) * a + _sigmoid(gb) * r

    gate_rows = [(p_gab, ctx_tiles, (d, 0)), (p_gab, ctx_tiles, (d, 1))]
    z = _rowwise("merge", lambda rows, sel, fulls: ([merge_fn(*rows)], []), n_tiles=nt_x, tr=tr,
                 row_ins=[(pa, 0, None), (prj, 0, None)] + gate_rows, row_outs=[(t, d, BF16, 0)])[0]
    fo = _mm("mix_out", z, wmo, "nn", BF16)
    h2, n3 = resid_norm("resid_norm2", h1, ctx_tiles, fo, nt_x, ctx_tiles, 5, 1.0, 6, 7)
    w2i, w2o = weights_late(ffn2_weights, n3)
    hm2, ua2, ub2 = _mm_swiglu("ffn2_in", n3, w2i)
    f2 = _mm("ffn2_out", hm2, w2o, "nn", BF16)

    def loss_fn(rows, sel, fulls):
        h2v, f2v, tgt = rows
        g3 = 0.5 * sel(8)
        y, vjp = jax.vjp(lambda hh, ww: _rmsn(hh) * ww, h2v + g3 * f2v, fulls[0])
        err = y - tgt
        dh, dw = vjp(err / d)
        return [dh, g3 * dh], [0.5 / d * jnp.sum(err * err, axis=0, keepdims=True), dw,
                               jnp.sum(0.5 * dh * f2v, axis=0, keepdims=True)]

    dh3, df2, loss_acc = _rowwise("loss_head", loss_fn, n_tiles=nt_x, tr=tr,
                                  row_ins=[(h2, 0, None), (f2, 0, None), (loss_target[0], 0, None)],
                                  row_outs=[(t, d, F32, 0), (t, d, BF16, 0)], sel_in=mods, sel_off=ctx_tiles,
                                  ctx_rows=tc, full_ins=[final_norm[None, :]], acc_shape=(8, d))
    loss_cols, g_final, dg3 = loss_acc[1, 0:1], loss_acc[1, 1:2], loss_acc[1, 2:3]

    def norm_mod_bwd(name, dn, h, dres, dres_off, n_tiles, off, i_shift, i_scale, gate=None, out_off=0):
        def fn(rows, sel, fulls):
            hh, rows = stream_value(h, rows, fulls)
            g, dr = rows[:2]
            if dres_off < 0:
                dr = jnp.where(fulls[-1], 0.0, dr)
            _, vjp = jax.vjp(lambda a, sh, sc: _rmsn(a) * (1.0 + sc) + sh, hh,
                             sel(i_shift), sel(i_scale))
            dhh, dsh, dsc = vjp(g)
            dh = dr + dhh
            if gate is None:
                return [dh], [dsh, dsc]
            return [dh, gate[2] * sel(gate[1]) * dh], [dsh, dsc, jnp.sum(gate[2] * dh * rows[2], axis=0, keepdims=True)]
        n_rows = dn.shape[0] + out_off * tr
        row_ins = stream_rows(h, 0) + [(dn, 0, None), (dres, dres_off, None)]
        row_outs = [(n_rows, d, F32, out_off)]
        if gate is not None:
            row_ins.append((gate[0], 0, None))
            row_outs.append((n_rows, d, BF16, out_off))
        return _rowwise(name, fn, n_tiles=n_tiles, tr=tr, row_ins=row_ins, row_outs=row_outs, sel_in=mods,
                        sel_off=off, ctx_rows=tc, acc_shape=(8, d))

    g_w2o = _mm("ffn2_out_dw", hm2, df2, "tn", BF16)
    du2 = _mm_swiglu_bwd("ffn2_out_dx", df2, w2o, ua2, ub2)
    g_w2i = _mm("ffn2_in_dw", n3, du2, "tn", BF16, stacked=True)
    dn3 = _mm("ffn2_in_dx", du2, w2i, "nt", F32, stacked=True)
    dh2, dfo, acc_n3 = norm_mod_bwd("norm_mod_bwd3", dn3, h2, dh3, 0, nt_x, ctx_tiles, 6, 7, gate=(fo, 5, 1.0))

    sems_ffn2, thru_ffn2, token = _reduce_start("reduce_start_ffn2", metas[6:8], [g_w2i, g_w2o])

    g_wmo = _mm("mix_out_dw", z, dfo, "tn", BF16, after=token)
    dz = _mm("mix_out_dx", dfo, wmo, "nt", F32)

    def merge_bwd(rows, sel, fulls):
        g, a, r, ga, gb = rows
        _, vjp = jax.vjp(merge_fn, a, r, ga, gb)
        da, dr, dga, dgb = vjp(g)
        return [da, dr, jnp.concatenate([dga, dgb], axis=1)], []

    dpa, dpr, dgab = _rowwise("merge_bwd", merge_bwd, n_tiles=nt_x, tr=tr,
                              row_ins=[(dz, 0, None), (pa, 0, None), (prj, 0, None)] + gate_rows,
                              row_outs=[(t, d, BF16, 0), (t, d, BF16, 0), (t, 2 * d, BF16, 0)])
    g_wpa = _mm("proj_attn_dw", ya, dpa, "tn", BF16)
    dya = _mm("proj_attn_dx", dpa, wpa, "nt", BF16)
    g_wpr = _mm("proj_ret_dw", yr, dpr, "tn", BF16)
    dyr = _mm("proj_ret_dx", dpr, wpr, "nt", F32)

    def ret_out_bwd(rows, sel, fulls):
        def per_head(g, yf, yb, gr):
            _, vjp = jax.vjp(lambda yy, gg: ret_out_fn(yy, 0.0, gg)[0], yf + yb, gr)
            return list(vjp(g))
        dy, dgr = _heads_map(per_head, rows, rw)
        return [dy, dgr], []

    dy_ret, dgr = _rowwise("ret_out_bwd", ret_out_bwd, n_tiles=nt_x, tr=tr, row_ins=[(dyr, 0, None)] + y_rows,
                           row_outs=[(t, rw, F32, 0), (t, rw, BF16, 0)])
    dp_rf, dp_rb, dlg = _ret_bwd(p_r, states, dy_ret, lgv, tc)
    dp_r = _rowwise("ret_bwd_sum", lambda rows, sel, fulls: ([rows[0] + rows[1]], []), n_tiles=nt_all, tr=tr,
                    row_ins=[(dp_rf, 0, None), (dp_rb, 0, None)], row_outs=[(tk, 3 * rw, BF16, 0)])[0]

    dq_rot, dk_rot, dv_att = _flash_bwd(q_rot, k_rot, v_att, ya, dya, lse, groups)

    def q_prep_bwd(rows, sel, fulls):
        g, p, cf, ss = rows
        gain_acc = []

        def per_head(gh, ph):
            dp, dgain = _qk_rot_bwd(gh * HEAD_DIM ** -0.5, ph, fulls[0], cf, ss)
            gain_acc.append(dgain)
            return [dp]
        dp = _heads_map(per_head, [g, p], aw)[0]
        return [dp], [functools.reduce(lambda a, b: a + b, gain_acc)]

    dp_q, acc_gq = _rowwise("q_prep_bwd", q_prep_bwd, n_tiles=nt_x, tr=tr,
                            row_ins=[(dq_rot, 0, None), (p_q, ctx_tiles, None), (cos_full, ctx_tiles, None),
                                     (sin_signed, ctx_tiles, None)],
                            row_outs=[(t, aw, BF16, 0)], full_ins=[q_gain], acc_shape=(8, HEAD_DIM),
                            sel_off=ctx_tiles, ctx_rows=tc)

    def kv_prep_bwd(rows, sel, fulls):
        gk, gv, p, cf, ss = rows
        gain_acc = []

        def per_head(gh, ph):
            dp, dgain = _qk_rot_bwd(gh, ph, fulls[0], cf, ss)
            gain_acc.append(dgain)
            return [dp]
        dpk = _heads_map(per_head, [gk, p], kvw)[0]
        return [jnp.concatenate([dpk, gv], axis=1)], [functools.reduce(lambda a, b: a + b, gain_acc)]

    dp_kv, acc_gk = _rowwise("kv_prep_bwd", kv_prep_bwd, n_tiles=nt_all, tr=tr,
                             row_ins=[(dk_rot, 0, None), (dv_att, 0, None), (p_kv, 0, (kvw, 0)), (cos_full, 0, None),
                                      (sin_signed, 0, None)],
                             row_outs=[(tk, 2 * kvw, BF16, 0)], full_ins=[k_gain], acc_shape=(8, HEAD_DIM),
                             sel_off=0, ctx_rows=tc)

    def with_ctx_zeros(a):
        return jnp.concatenate([jnp.zeros((tc, a.shape[1]), a.dtype), a], axis=0)

    dp = jnp.concatenate([with_ctx_zeros(dp_q), dp_kv, dp_r, with_ctx_zeros(dgr), with_ctx_zeros(dgab)], axis=1)
    g_wmi = _mm("mix_in_dw", n2, dp, "tn", BF16)
    dn2 = _mm("mix_in_dx", dp, wmi, "nt", F32)
    dh1, df1, acc_n2 = norm_mod_bwd("norm_mod_bwd2", dn2, h1, dh2, -ctx_tiles, nt_all, 0, 3, 4, gate=(f1, 2, 0.5))
    sems_mix, thru_mix, token = _reduce_start("reduce_start_mix", metas[2:6], [g_wmi, g_wpa, g_wpr, g_wmo])

    g_w1o = _mm("ffn1_out_dw", hm1, df1, "tn", BF16, after=token)
    sems_w1o, thru_w1o, token = _reduce_start("reduce_start_ffn1_out", metas[1:2], [g_w1o])
    du1 = _mm_swiglu_bwd("ffn1_out_dx", df1, w1o, ua1, ub1, after=token)
    g_w1i = _mm("ffn1_in_dw", n1, du1, "tn", BF16, stacked=True)
    sems_w1i, thru_w1i, token = _reduce_start("reduce_start_ffn1_in", metas[0:1], [g_w1i])
    dn1 = _mm("ffn1_in_dx", du1, w1i, "nt", F32, after=token, stacked=True)
    dh0, acc_n1 = norm_mod_bwd("norm_mod_bwd1", dn1, h0, dh1, 0, nt_all, 0, 0, 1, out_off=-ctx_tiles)
    grad_x = dh0[None]

    grads_own, landed = [], []
    for name, lo, hi, sems_l, thru_l in (("reduce_wait_ffn1_in", 0, 1, sems_w1i, thru_w1i),
                                         ("reduce_wait_ffn1_out", 1, 2, sems_w1o, thru_w1o),
                                         ("reduce_wait_mix", 2, 6, sems_mix, thru_mix),
                                         ("reduce_wait_ffn2", 6, 8, sems_ffn2, thru_ffn2)):
        grads_l, landed_l = _reduce_wait(name, metas[lo:hi], sems_l, thru_l, dh0)
        grads_own += grads_l
        landed += landed_l
    pieces = [_sum_pieces(m, g, l, s_arr, c_arr) for m, g, l in zip(metas, grads_own, landed)]
    grads_big = _share_halves(metas, pieces)

    zero_row = jnp.zeros((1, d), F32)
    dmod_x = jnp.concatenate([acc_n1[1, 0:1], acc_n1[1, 1:2], acc_n2[1, 2:3], acc_n2[1, 0:1], acc_n2[1, 1:2],
                              acc_n3[1, 2:3], acc_n3[1, 0:1], acc_n3[1, 1:2], dg3], axis=1)
    dmod_c = jnp.concatenate([acc_n1[0, 0:1], acc_n1[0, 1:2], acc_n2[0, 2:3], acc_n2[0, 0:1], acc_n2[0, 1:2]]
                             + [zero_row] * 4, axis=1)
    dlg_row = jnp.pad(dlg[:, :, 0, 0].reshape(1, 2 * n_ret_heads), ((0, 0), (0, LANES_V7X - 2 * n_ret_heads)))
    packed = jnp.concatenate([dmod_x, dmod_c, acc_gq[1, 0:1], acc_gk[0, 0:1] + acc_gk[1, 0:1], dlg_row,
                              g_final, loss_cols], axis=1)
    off_gq = 2 * mod_cols
    off_gk = off_gq + LANES_V7X
    off_lg = off_gk + LANES_V7X
    off_fn = off_lg + LANES_V7X
    off_loss = off_fn + d
    gathered = _gather_row("gather_small", packed)
    logits_row = jnp.pad(ret_decay_logit.reshape(1, 2 * n_ret_heads), ((0, 0), (0, LANES_V7X - 2 * n_ret_heads)))
    totals, g_b_ada, g_decay, loss_row = _small_reduce(gathered, logits_row, mod_cols, off_lg, off_loss, d)
    loss = loss_row[0, 0]

    dm = jnp.concatenate([gathered[:, :mod_cols], totals[:, mod_cols:2 * mod_cols],
                          jnp.zeros((7, mod_cols), F32)], axis=0)
    dm_l = lax.dynamic_slice_in_dim(dm, s_me * ada_cols, ada_cols, axis=1)
    g_w_ada, da_part = _ada_bwd(cg, dm_l, w_ada_l)
    da_rows = _gather_row("gather_dc", da_part[8:9])
    g_c_ctx = _c_ctx_grad(da_rows, c_ctx[None, :])

    def as2d(a):
        return a.reshape(-1, a.shape[-1])

    grads = {
        "c_ctx": g_c_ctx, "w_ada": g_w_ada, "b_ada": g_b_ada,
        "ffn1_w_in": grads_big[0], "ffn1_w_out": grads_big[1], "mix_w_in": grads_big[2],
        "attn_q_gain": totals[:, off_gq:off_gq + HEAD_DIM], "attn_k_gain": totals[:, off_gk:off_gk + HEAD_DIM],
        "ret_decay_logit": g_decay[:, :2 * n_ret_heads],
        "w_proj_attn": grads_big[3], "w_proj_ret": grads_big[4], "mix_w_out": grads_big[5],
        "ffn2_w_in": grads_big[6], "ffn2_w_out": grads_big[7], "final_norm": totals[:, off_fn:off_fn + d],
    }
    weights = {"c_ctx": (c_ctx, m_c_ctx, v_c_ctx), "w_ada": (w_ada, m_w_ada, v_w_ada),
               "b_ada": (b_ada, m_b_ada, v_b_ada), "ffn1_w_in": (ffn1_w_in, m_ffn1_w_in, v_ffn1_w_in),
               "ffn1_w_out": (ffn1_w_out, m_ffn1_w_out, v_ffn1_w_out), "mix_w_in": (mix_w_in, m_mix_w_in, v_mix_w_in),
               "attn_q_gain": (attn_q_gain, m_attn_q_gain, v_attn_q_gain),
               "attn_k_gain": (attn_k_gain, m_attn_k_gain, v_attn_k_gain),
               "ret_decay_logit": (ret_decay_logit, m_ret_decay_logit, v_ret_decay_logit),
               "w_proj_attn": (w_proj_attn, m_w_proj_attn, v_w_proj_attn),
               "w_proj_ret": (w_proj_ret, m_w_proj_ret, v_w_proj_ret), "mix_w_out": (mix_w_out, m_mix_w_out, v_mix_w_out),
               "ffn2_w_in": (ffn2_w_in, m_ffn2_w_in, v_ffn2_w_in), "ffn2_w_out": (ffn2_w_out, m_ffn2_w_out, v_ffn2_w_out),
               "final_norm": (final_norm, m_final_norm, v_final_norm)}
    out_g, out_d, out_m, out_v = [], [], [], []
    for name, (w, m, v) in weights.items():
        shape = w.shape
        if name == "ret_decay_logit":
            w2, m2, v2 = (a.reshape(1, -1) for a in (w, m, v))
        else:
            w2, m2, v2 = as2d(w), as2d(m), as2d(v)
        g2 = grads[name].reshape(w2.shape)
        delta, new_m, new_v = _adamw(w2, g2, m2, v2)
        out_g.append(g2.reshape(shape))
        out_d.append(delta.reshape(shape))
        out_m.append(new_m.reshape(shape))
        out_v.append(new_v.reshape(shape))
    return (loss, grad_x, *out_g, *out_d, *out_m, *out_v)
```

```python
import functools
import math

import jax
import jax.numpy as jnp
from jax import lax
from jax.experimental import pallas as pl
from jax.experimental.pallas import tpu as pltpu

F32 = jnp.float32
BF16 = jnp.bfloat16

HEAD_DIM = 128
GRID_W = 64
ROPE_THETA = 10000.0
NORM_EPS = 1e-6
N_MOD = 9
RET_CHUNK = 128
ADAM_LR = 0.001
ADAM_B1 = 0.9
ADAM_B2 = 0.999
ADAM_EPS = 1e-08
ADAM_WD = 0.01
ADAM_STEP = 10

N_DEV = 8
N_CHIP = 4
LANES_V7X = 128
MXU_WIDTH_V7X = 256
VMEM_LIMIT_V7X = 52 * 1024 * 1024

NT_DIMS = (((1,), (1,)), ((), ()))
TN_DIMS = (((0,), (0,)), ((), ()))
NN_DIMS = (((1,), (0,)), ((), ()))


def _tile(n, pref, mult=LANES_V7X):
    if n <= pref:
        return n
    t = (pref // mult) * mult
    while t >= mult:
        if n % t == 0:
            return t
        t -= mult
    return n


def _params(sem):
    return pltpu.CompilerParams(dimension_semantics=sem, vmem_limit_bytes=VMEM_LIMIT_V7X)


def _sigmoid(x):
    return 1.0 / (1.0 + jnp.exp(-x))


def _silu(x):
    return x * _sigmoid(x)


def _rmsn(x):
    return x * lax.rsqrt(jnp.mean(x * x, axis=-1, keepdims=True) + NORM_EPS)


MM_VMEM_BUDGET = 44 * 1024 * 1024


def _divisor_tiles(n, cap):
    ts = [t for t in range(LANES_V7X, min(n, cap) + 1, LANES_V7X) if n % t == 0]
    return ts or [n]


def _mm_tiles(m, n, tk, out_bytes, has_acc):
    best = None
    for tm in _divisor_tiles(m, 1536):
        for tn in _divisor_tiles(n, 2560):
            need = 4 * tk * (tm + tn) + 2 * tm * tn * out_bytes + 4 * tm * tn
            if need > MM_VMEM_BUDGET:
                continue
            score = tm * tn / (tm + tn)
            for tdim in (tm, tn):
                if tdim % MXU_WIDTH_V7X:
                    score *= 0.85
            if best is None or score > best[0]:
                best = (score, tm, tn)
    return best[1], best[2]


def _mm(name, a, b, mode, out_dtype, b_off=0, n=None, after=None, stacked=False):
    half = 0
    if mode == "nn":
        m, k = a.shape
        n = b.shape[1] if n is None else n
        dims = NN_DIMS
    elif mode == "nt":
        if stacked:
            _, m, half = a.shape
            k = 2 * half
        else:
            m, k = a.shape
        n = b.shape[0]
        dims = NT_DIMS
    else:
        k, m = a.shape
        if stacked:
            half = b.shape[2]
            n = 2 * half
        else:
            n = b.shape[1]
        dims = TN_DIMS
    tk = _tile(half if (stacked and mode == "nt") else k, 2816)
    nk = k // tk
    n_tiled = half if (stacked and mode == "tn") else (math.gcd(n, b_off) if b_off else n)
    tm, tn = _mm_tiles(m, n_tiled, tk, jnp.dtype(out_dtype).itemsize, nk > 1)
    joff = b_off // tn
    per_half = (half // tk) if mode == "nt" else (half // tn)

    def body(a_ref, b_ref, *rest):
        o_ref = rest[0 if after is None else 1]
        if nk == 1:
            o_ref[...] = lax.dot_general(a_ref[...], b_ref[...], dims,
                                         preferred_element_type=F32).astype(o_ref.dtype)
            return
        acc_ref = rest[-1]
        kk = pl.program_id(2)

        @pl.when(kk == 0)
        def _():
            acc_ref[...] = jnp.zeros_like(acc_ref)

        acc_ref[...] += lax.dot_general(a_ref[...], b_ref[...], dims, preferred_element_type=F32)

        @pl.when(kk == nk - 1)
        def _():
            o_ref[...] = acc_ref[...].astype(o_ref.dtype)

    if mode == "nn":
        a_spec = pl.BlockSpec((tm, tk), lambda i, j, kk: (i, kk))
        b_spec = pl.BlockSpec((tk, tn), lambda i, j, kk: (kk, j + joff))
    elif mode == "nt":
        a_spec = pl.BlockSpec((tm, tk), lambda i, j, kk: (i, kk))
        if stacked:
            a_spec = pl.BlockSpec((None, tm, tk), lambda i, j, kk: (kk // per_half, i, kk % per_half))
        b_spec = pl.BlockSpec((tn, tk), lambda i, j, kk: (j, kk))
    else:
        a_spec = pl.BlockSpec((tk, tm), lambda i, j, kk: (kk, i))
        b_spec = pl.BlockSpec((tk, tn), lambda i, j, kk: (kk, j))
        if stacked:
            b_spec = pl.BlockSpec((None, tk, tn), lambda i, j, kk: (j // per_half, kk, j % per_half))
    return pl.pallas_call(
        body, name=name, grid=(m // tm, n // tn, nk),
        in_specs=[a_spec, b_spec] + ([] if after is None else [pl.BlockSpec(memory_space=pl.ANY)]),
        out_specs=pl.BlockSpec((tm, tn), lambda i, j, kk: (i, j)),
        out_shape=jax.ShapeDtypeStruct((m, n), out_dtype),
        scratch_shapes=[pltpu.VMEM((tm, tn), F32)] if nk > 1 else [],
        compiler_params=_params(("parallel", "parallel", "arbitrary")),
    )(*((a, b) if after is None else (a, b, after)))


def _mm_swiglu_bwd(name, dy, w_out, ua, ub, after=None):
    m, d_model = dy.shape
    f = w_out.shape[0]
    tm = _tile(m, 1536)
    tn = _tile(f, 512)

    def body(dy_ref, w_ref, ua_ref, ub_ref, *rest):
        du_ref = rest[-1]
        g = lax.dot_general(dy_ref[...], w_ref[...], NT_DIMS, preferred_element_type=F32)
        _, vjp = jax.vjp(lambda aa, bb: _silu(aa) * bb, ua_ref[...].astype(F32), ub_ref[...].astype(F32))
        da, db = vjp(g)
        du_ref[0] = da.astype(BF16)
        du_ref[1] = db.astype(BF16)

    tile = pl.BlockSpec((tm, tn), lambda i, j: (i, j))
    return pl.pallas_call(
        body, name=name, grid=(m // tm, f // tn),
        in_specs=[pl.BlockSpec((tm, d_model), lambda i, j: (i, 0)), pl.BlockSpec((tn, d_model), lambda i, j: (j, 0)),
                  tile, tile] + ([] if after is None else [pl.BlockSpec(memory_space=pl.ANY)]),
        out_specs=pl.BlockSpec((2, tm, tn), lambda i, j: (0, i, j)),
        out_shape=jax.ShapeDtypeStruct((2, m, f), BF16),
        compiler_params=_params(("parallel", "parallel")),
    )(*((dy, w_out, ua, ub) if after is None else (dy, w_out, ua, ub, after)))


def _mm_swiglu(name, a, w):
    m, k = a.shape
    f = w.shape[1] // 2
    tm = _tile(m, 1536)
    tn = _tile(f, 512)
    tk = _tile(k, 2560)
    nk = k // tk
    jf = f // tn

    def body(a_ref, wa_ref, wb_ref, h_ref, ua_ref, ub_ref, acca, accb):
        kk = pl.program_id(2)

        @pl.when(kk == 0)
        def _():
            acca[...] = jnp.zeros_like(acca)
            accb[...] = jnp.zeros_like(accb)

        av = a_ref[...]
        acca[...] += jnp.dot(av, wa_ref[...], preferred_element_type=F32)
        accb[...] += jnp.dot(av, wb_ref[...], preferred_element_type=F32)

        @pl.when(kk == nk - 1)
        def _():
            ua = acca[...]
            ub = accb[...]
            h_ref[...] = (_silu(ua) * ub).astype(BF16)
            ua_ref[...] = ua.astype(BF16)
            ub_ref[...] = ub.astype(BF16)

    o_spec = pl.BlockSpec((tm, tn), lambda i, j, kk: (i, j))
    o_shape = jax.ShapeDtypeStruct((m, f), BF16)
    return pl.pallas_call(
        body, name=name, grid=(m // tm, jf, nk),
        in_specs=[pl.BlockSpec((tm, tk), lambda i, j, kk: (i, kk)),
                  pl.BlockSpec((tk, tn), lambda i, j, kk: (kk, j)),
                  pl.BlockSpec((tk, tn), lambda i, j, kk: (kk, j + jf))],
        out_specs=[o_spec, o_spec, o_spec],
        out_shape=[o_shape, o_shape, o_shape],
        scratch_shapes=[pltpu.VMEM((tm, tn), F32), pltpu.VMEM((tm, tn), F32)],
        compiler_params=_params(("parallel", "parallel", "arbitrary")),
    )(a, w, w)


def _rowwise(name, fn, *, n_tiles, tr, row_ins, row_outs, sel_in=None, sel_off=0, ctx_rows=0,
             full_ins=(), acc_shape=None):
    sr = 256 if tr % 256 == 0 else (128 if tr % 128 == 0 else (32 if tr % 32 == 0 else tr))
    n_row, n_full, n_out = len(row_ins), len(full_ins), len(row_outs)
    has_sel = sel_in is not None
    has_acc = acc_shape is not None

    def sel_of(i):
        return jnp.where((i + sel_off) * tr < ctx_rows, 0, 1)

    def body(*refs):
        row_refs = refs[:n_row]
        pos = n_row
        sel_ref = None
        if has_sel:
            sel_ref = refs[pos]
            pos += 1
        full_refs = refs[pos:pos + n_full]
        pos += n_full
        out_refs = refs[pos:pos + n_out]
        pos += n_out
        acc_ref = refs[pos] if has_acc else None
        i = pl.program_id(0)
        if has_acc:
            first = (i == 0) | ((i + sel_off) * tr == ctx_rows)

            @pl.when(first)
            def _():
                acc_ref[...] = jnp.zeros_like(acc_ref)

        sel = (lambda kk: sel_ref[kk:kk + 1, :]) if has_sel else None
        fulls = [r[...] for r in full_refs] + [(i + sel_off) * tr < ctx_rows]

        def slab(r, carry):
            rs = pl.ds(pl.multiple_of(r * sr, sr), sr)
            rows = [ref[rs, :].astype(F32) for ref in row_refs]
            outs, accs = fn(rows, sel, fulls)
            for o_ref, o in zip(out_refs, outs):
                o_ref[rs, :] = o.astype(o_ref.dtype)
            for kk, a in enumerate(accs):
                acc_ref[kk:kk + 1, :a.shape[1]] += a
            return carry

        lax.fori_loop(0, tr // sr, slab, 0)

    def row_map(off, col=0):
        if off == "ctx":
            return lambda i: (jnp.minimum(i, ctx_rows // tr - 1), col)
        if off < 0:
            return lambda i: (jnp.maximum(i + off, 0), col)
        return lambda i: (i + off, col)

    in_specs, args = [], []
    for arr, off, blk in row_ins:
        if blk is None:
            in_specs.append(pl.BlockSpec((tr, arr.shape[1]), row_map(off)))
        else:
            in_specs.append(pl.BlockSpec((tr, blk[0]), row_map(off, blk[1])))
        args.append(arr)
    if has_sel:
        in_specs.append(pl.BlockSpec((None,) + sel_in.shape[1:], lambda i: (sel_of(i), 0, 0)))
        args.append(sel_in)
    for arr in full_ins:
        in_specs.append(pl.BlockSpec(arr.shape, lambda i: (0, 0)))
        args.append(arr)
    out_specs, out_shape = [], []
    for rows, cols, dt, off in row_outs:
        out_specs.append(pl.BlockSpec((tr, cols), row_map(off)))
        out_shape.append(jax.ShapeDtypeStruct((rows, cols), dt))
    if has_acc:
        out_specs.append(pl.BlockSpec((None,) + tuple(acc_shape), lambda i: (sel_of(i), 0, 0)))
        out_shape.append(jax.ShapeDtypeStruct((2,) + tuple(acc_shape), F32))
    return pl.pallas_call(
        body, name=name, grid=(n_tiles,), in_specs=in_specs, out_specs=out_specs, out_shape=out_shape,
        compiler_params=_params(("arbitrary",)),
    )(*args)


def _swap_pairs(x):
    lane = lax.broadcasted_iota(jnp.int32, x.shape, 1)
    nxt = pltpu.roll(x, x.shape[1] - 1, 1)
    prv = pltpu.roll(x, 1, 1)
    return jnp.where(lane % 2 == 0, nxt, prv)


def _heads_map(fn, arrs, width):
    outs = None
    for h in range(width // HEAD_DIM):
        sl = slice(h * HEAD_DIM, (h + 1) * HEAD_DIM)
        res = fn(*[a[:, sl] for a in arrs])
        if outs is None:
            outs = [[] for _ in res]
        for lst, r in zip(outs, res):
            lst.append(r)
    return [jnp.concatenate(lst, axis=1) if len(lst) > 1 else lst[0] for lst in outs]


QSCALE = HEAD_DIM ** -0.5 * math.log2(math.e)
LN2 = math.log(2.0)


def _lane_chunks(a):
    return [a[:, cc * LANES_V7X:(cc + 1) * LANES_V7X] for cc in range(a.shape[1] // LANES_V7X)]


def _row_bcast(col, like):
    return jnp.broadcast_to(col, like.shape)


def _flash_tiles(t, tk_all, key_pref):
    return _tile(t, 1024), _tile(tk_all, key_pref)


def _flash_fwd(q, k, vx, groups):
    t, aw = q.shape
    tk_all, kvw = k.shape
    kvh = kvw // HEAD_DIM
    gw = groups * HEAD_DIM
    tq, tk = _flash_tiles(t, tk_all, 1536)
    nk = tk_all // tk

    def body(q_ref, k_ref, v_ref, o_ref, lse_ref, m_sc, l_sc, acc_sc):
        j = pl.program_id(2)

        @pl.when(j == 0)
        def _():
            m_sc[...] = jnp.full_like(m_sc, -jnp.inf)
            l_sc[...] = jnp.zeros_like(l_sc)
            acc_sc[...] = jnp.zeros_like(acc_sc)

        kt = k_ref[...]
        vt = v_ref[...]
        for g in range(groups):
            sl = slice(g * HEAD_DIM, (g + 1) * HEAD_DIM)
            s = _lane_chunks(lax.dot_general(q_ref[:, sl], kt, NT_DIMS, preferred_element_type=F32))
            mx = functools.reduce(jnp.maximum, s)
            m_prev = m_sc[g]
            m_new = jnp.maximum(m_prev, _row_bcast(jnp.max(mx, axis=1, keepdims=True), mx))
            p = jnp.concatenate([jnp.exp2(sc - m_new).astype(BF16) for sc in s], axis=1)
            alpha = jnp.exp2(m_prev - m_new)
            pv = jnp.dot(p, vt, preferred_element_type=F32)
            acc_sc[g] = alpha * acc_sc[g] + pv[:, :HEAD_DIM]
            l_sc[g] = alpha * l_sc[g] + pv[:, HEAD_DIM:]
            m_sc[g] = m_new

        @pl.when(j == nk - 1)
        def _():
            for g in range(groups):
                sl = slice(g * HEAD_DIM, (g + 1) * HEAD_DIM)
                o_ref[:, sl] = (acc_sc[g] / l_sc[g]).astype(o_ref.dtype)
                lse_ref[:, sl] = m_sc[g] + jnp.log2(l_sc[g])

    qs = pl.BlockSpec((tq, gw), lambda kh, i, j: (i, kh))
    sc = pltpu.VMEM((groups, tq, HEAD_DIM), F32)
    return pl.pallas_call(
        body, name="flash_fwd", grid=(kvh, t // tq, nk),
        in_specs=[qs, pl.BlockSpec((tk, HEAD_DIM), lambda kh, i, j: (j, kh)),
                  pl.BlockSpec((tk, 2 * HEAD_DIM), lambda kh, i, j: (j, kh))],
        out_specs=[qs, qs],
        out_shape=[jax.ShapeDtypeStruct((t, aw), BF16), jax.ShapeDtypeStruct((t, aw), F32)],
        scratch_shapes=[sc, sc, sc],
        compiler_params=_params(("parallel", "parallel", "arbitrary")),
    )(q, k, vx)


def _flash_p_ds(q, kt, vt, do, lse, delta):
    s = _lane_chunks(lax.dot_general(q, kt, NT_DIMS, preferred_element_type=F32))
    dp = _lane_chunks(lax.dot_general(do, vt, NT_DIMS, preferred_element_type=F32))
    p = [jnp.exp2(sc - lse) for sc in s]
    ds = jnp.concatenate([(pc * (dc - delta)).astype(BF16) for pc, dc in zip(p, dp)], axis=1)
    return jnp.concatenate([pc.astype(BF16) for pc in p], axis=1), ds


def _flash_delta(do, o):
    prod = do.astype(F32) * o.astype(F32)
    return _row_bcast(jnp.sum(prod, axis=1, keepdims=True), prod)


def _flash_bwd(q, k, vx, o, do, lse, groups):
    t, aw = q.shape
    tk_all, kvw = k.shape
    kvh = kvw // HEAD_DIM
    gw = groups * HEAD_DIM
    tq, tk = _flash_tiles(t, tk_all, 1024)
    nq, nk = t // tq, tk_all // tk

    def body(q_ref, k_ref, v_ref, o_ref, do_ref, lse_ref, dq_ref, dk_ref, dv_ref, dq_sc, dk_acc, dv_acc):
        j = pl.program_id(1)
        i = pl.program_id(2)

        @pl.when(i == 0)
        def _():
            dk_acc[...] = jnp.zeros_like(dk_acc)
            dv_acc[...] = jnp.zeros_like(dv_acc)

        @pl.when(j == 0)
        def _():
            dq_sc[i] = jnp.zeros((groups, tq, HEAD_DIM), F32)

        kt = k_ref[...]
        vt = v_ref[:, :HEAD_DIM]
        for g in range(groups):
            sl = slice(g * HEAD_DIM, (g + 1) * HEAD_DIM)
            qv = q_ref[:, sl]
            dov = do_ref[:, sl]
            p, ds = _flash_p_ds(qv, kt, vt, dov, lse_ref[:, sl], _flash_delta(dov, o_ref[:, sl]))
            dv_acc[...] += lax.dot_general(p, dov, TN_DIMS, preferred_element_type=F32)
            dk_acc[...] += lax.dot_general(ds, qv, TN_DIMS, preferred_element_type=F32)
            dq_sc[i, g] += jnp.dot(ds, kt, preferred_element_type=F32)

        @pl.when(i == nq - 1)
        def _():
            dk_ref[...] = dk_acc[...] * LN2
            dv_ref[...] = dv_acc[...]

        @pl.when(j == nk - 1)
        def _():
            for g in range(groups):
                dq_ref[:, g * HEAD_DIM:(g + 1) * HEAD_DIM] = dq_sc[i, g]

    qs = pl.BlockSpec((tq, gw), lambda kh, j, i: (i, kh))
    ks = pl.BlockSpec((tk, HEAD_DIM), lambda kh, j, i: (j, kh))
    dq_spec = pl.BlockSpec((tq, gw), lambda kh, j, i: (jnp.where(j == nk - 1, i, 0), kh))
    return pl.pallas_call(
        body, name="flash_bwd", grid=(kvh, nk, nq),
        in_specs=[qs, ks, pl.BlockSpec((tk, 2 * HEAD_DIM), lambda kh, j, i: (j, kh)), qs, qs, qs],
        out_specs=[dq_spec, ks, ks],
        out_shape=[jax.ShapeDtypeStruct((t, aw), F32), jax.ShapeDtypeStruct((tk_all, kvw), F32),
                   jax.ShapeDtypeStruct((tk_all, kvw), F32)],
        scratch_shapes=[pltpu.VMEM((nq, groups, tq, HEAD_DIM), F32), pltpu.VMEM((tk, HEAD_DIM), F32),
                        pltpu.VMEM((tk, HEAD_DIM), F32)],
        compiler_params=_params(("parallel", "arbitrary", "arbitrary")),
    )(q, k, vx, o, do, lse)


def _bf_nn(a, b):
    return jnp.dot(a.astype(BF16), b.astype(BF16), preferred_element_type=F32)


def _bf_nt(a, b):
    return lax.dot_general(a.astype(BF16), b.astype(BF16), NT_DIMS, preferred_element_type=F32)


def _bf_tn(a, b):
    return lax.dot_general(a.astype(BF16), b.astype(BF16), TN_DIMS, preferred_element_type=F32)


@jax.custom_vjp
def _d_nn(a, b):
    return _bf_nn(a, b)


@jax.custom_vjp
def _d_nt(a, b):
    return _bf_nt(a, b)


@jax.custom_vjp
def _d_tn(a, b):
    return _bf_tn(a, b)


_d_nn.defvjp(lambda a, b: (_bf_nn(a, b), (a, b)), lambda r, g: (_d_nt(g, r[1]), _d_tn(r[0], g)))
_d_nt.defvjp(lambda a, b: (_bf_nt(a, b), (a, b)), lambda r, g: (_d_nn(g, r[1]), _d_tn(g, r[0])))
_d_tn.defvjp(lambda a, b: (_bf_tn(a, b), (a, b)), lambda r, g: (_d_nt(r[1], g), _d_nn(r[0], g)))


def _ret_chunk(q, k_raw, v, state, lg, rev, dots):
    nn, nt, tn = dots
    c = RET_CHUNK
    tcol = lax.broadcasted_iota(jnp.int32, (c, 1), 0).astype(F32)
    trow = lax.broadcasted_iota(jnp.int32, (1, c), 1).astype(F32)
    ucol = jnp.where(rev, c - 1.0 - tcol, tcol)
    urow = jnp.where(rev, c - 1.0 - trow, trow)
    e = ucol - urow
    low = e >= 0
    intra = jnp.where(low, jnp.exp(jnp.where(low, e, 0.0) * lg), 0.0)
    k = k_raw * (HEAD_DIM ** -0.5)
    inner = nt(q, k) * intra
    y = nn(inner, v) + nn(q, state) * jnp.exp((ucol + 1.0) * lg)
    new_state = state * jnp.exp(c * lg) + tn(k * jnp.exp((c - 1.0 - ucol) * lg), v)
    return y, new_state


def _ret_chunk_index(n_chunks, n_ctx_chunks):
    def idx(d, s):
        if d == 0:
            return s
        return jnp.where(s < n_ctx_chunks, n_ctx_chunks - 1 - s, n_chunks - 1 - s + n_ctx_chunks)
    return idx


def _ret_fwd(pr, lgv, ctx_rows):
    tk_all = pr.shape[0]
    rw = pr.shape[1] // 3
    nh = rw // HEAD_DIM
    nc = tk_all // RET_CHUNK
    cidx = _ret_chunk_index(nc, ctx_rows // RET_CHUNK)

    def body(pf_ref, pb_ref, lg_ref, yf_ref, yb_ref, st_ref, s_sc):
        s = pl.program_id(0)

        @pl.when(s == 0)
        def _():
            s_sc[...] = jnp.zeros_like(s_sc)

        for d, (p_ref, y_ref) in enumerate(((pf_ref, yf_ref), (pb_ref, yb_ref))):
            for h in range(nh):
                cols = [slice((part * nh + h) * HEAD_DIM, (part * nh + h + 1) * HEAD_DIM) for part in range(3)]
                state = s_sc[d, h]
                st_ref[d, h] = state
                y, new_state = _ret_chunk(p_ref[:, cols[0]], p_ref[:, cols[1]], p_ref[:, cols[2]], state,
                                          lg_ref[d, h][:, :1], d == 1, (_bf_nn, _bf_nt, _bf_tn))
                y_ref[:, h * HEAD_DIM:(h + 1) * HEAD_DIM] = y
                s_sc[d, h] = new_state

    y_shape = jax.ShapeDtypeStruct((tk_all, rw), F32)
    return pl.pallas_call(
        body, name="ret_fwd", grid=(nc,),
        in_specs=[pl.BlockSpec((RET_CHUNK, 3 * rw), lambda s: (cidx(0, s), 0)),
                  pl.BlockSpec((RET_CHUNK, 3 * rw), lambda s: (cidx(1, s), 0)),
                  pl.BlockSpec(lgv.shape, lambda s: (0, 0, 0, 0))],
        out_specs=[pl.BlockSpec((RET_CHUNK, rw), lambda s: (cidx(0, s), 0)),
                   pl.BlockSpec((RET_CHUNK, rw), lambda s: (cidx(1, s), 0)),
                   pl.BlockSpec((2, nh, None, HEAD_DIM, HEAD_DIM), lambda s: (0, 0, s, 0, 0))],
        out_shape=[y_shape, y_shape, jax.ShapeDtypeStruct((2, nh, nc, HEAD_DIM, HEAD_DIM), F32)],
        scratch_shapes=[pltpu.VMEM((2, nh, HEAD_DIM, HEAD_DIM), F32)],
        compiler_params=_params(("arbitrary",)),
    )(pr, pr, lgv)


def _ret_bwd(pr, states, dy, lgv, ctx_rows):
    tk_all = pr.shape[0]
    rw = pr.shape[1] // 3
    nh = rw // HEAD_DIM
    nc = tk_all // RET_CHUNK
    n_ctx = ctx_rows // RET_CHUNK
    cidx = _ret_chunk_index(nc, n_ctx)

    def body(pf_ref, pb_ref, st_ref, dyf_ref, dyb_ref, lg_ref, dpf_ref, dpb_ref, dlg_ref, ds_sc):
        sp = pl.program_id(0)
        on_ctx = [cidx(dd, nc - 1 - sp) < n_ctx for dd in (0, 1)]

        @pl.when(sp == 0)
        def _():
            ds_sc[...] = jnp.zeros_like(ds_sc)
            dlg_ref[...] = jnp.zeros_like(dlg_ref)

        for d, (p_ref, dy_ref, dp_ref) in enumerate(((pf_ref, dyf_ref, dpf_ref), (pb_ref, dyb_ref, dpb_ref))):
            for h in range(nh):
                cols = [slice((part * nh + h) * HEAD_DIM, (part * nh + h + 1) * HEAD_DIM) for part in range(3)]

                def step(q, k, v, state, lg, rev=(d == 1)):
                    return _ret_chunk(q, k, v, state, lg, rev, (_d_nn, _d_nt, _d_tn))

                _, vjp = jax.vjp(step, p_ref[:, cols[0]], p_ref[:, cols[1]], p_ref[:, cols[2]], st_ref[d, h],
                                 lg_ref[d, h][:, :1])
                dy_h = jnp.where(on_ctx[d], 0.0, dy_ref[:, h * HEAD_DIM:(h + 1) * HEAD_DIM])
                grads = vjp((dy_h, ds_sc[d, h]))
                for part in range(3):
                    dp_ref[:, cols[part]] = grads[part]
                ds_sc[d, h] = grads[3]
                dlg_ref[d, h] += jnp.broadcast_to(grads[4], (1, HEAD_DIM))

    def at(d):
        return lambda sp: (cidx(d, nc - 1 - sp), 0)

    def dy_at(d):
        return lambda sp: (jnp.maximum(cidx(d, nc - 1 - sp) - n_ctx, 0), 0)

    dp_shape = jax.ShapeDtypeStruct((tk_all, 3 * rw), F32)
    lg_spec = pl.BlockSpec(lgv.shape, lambda sp: (0, 0, 0, 0))
    return pl.pallas_call(
        body, name="ret_bwd", grid=(nc,),
        in_specs=[pl.BlockSpec((RET_CHUNK, 3 * rw), at(0)), pl.BlockSpec((RET_CHUNK, 3 * rw), at(1)),
                  pl.BlockSpec((2, nh, None, HEAD_DIM, HEAD_DIM), lambda sp: (0, 0, nc - 1 - sp, 0, 0)),
                  pl.BlockSpec((RET_CHUNK, rw), dy_at(0)), pl.BlockSpec((RET_CHUNK, rw), dy_at(1)), lg_spec],
        out_specs=[pl.BlockSpec((RET_CHUNK, 3 * rw), at(0)), pl.BlockSpec((RET_CHUNK, 3 * rw), at(1)), lg_spec],
        out_shape=[dp_shape, dp_shape, jax.ShapeDtypeStruct(lgv.shape, F32)],
        scratch_shapes=[pltpu.VMEM((2, nh, HEAD_DIM, HEAD_DIM), F32)],
        compiler_params=_params(("arbitrary",)),
    )(pr, pr, states, dy, dy, lgv)


FLIP_X, FLIP_Y, FLIP_XY, FLIP_C = (1, 0, 0), (0, 1, 0), (1, 1, 0), (0, 0, 1)
CHIP_FLIPS = ((FLIP_X, 2), (FLIP_Y, 1), (FLIP_XY, 3))


def _flip(me, mask):
    return tuple(1 - v if m else v for v, m in zip(me, mask))


def _comm(name, ins, out_shapes, plan, n_remote, n_local, aliases=None):
    n_in, n_out = len(ins), len(out_shapes)

    def body(*refs):
        in_refs = refs[:n_in]
        out_refs = refs[n_in:n_in + n_out]
        send_sems, recv_sems, local_sems = refs[n_in + n_out:]
        me = (lax.axis_index("x"), lax.axis_index("y"), lax.axis_index("c"))
        local, phases = plan(in_refs, out_refs, me)
        local_copies = [pltpu.make_async_copy(s, d, local_sems.at[i]) for i, (s, d) in enumerate(local)]
        for cp in local_copies:
            cp.start()
        sent = []
        kk = 0
        for phase in phases:
            arrivals = []
            for mask, src, dst, landing in phase:
                peer = _flip(me, mask)
                cp = pltpu.make_async_remote_copy(src_ref=src, dst_ref=dst, send_sem=send_sems.at[kk],
                                                  recv_sem=recv_sems.at[kk], device_id=peer,
                                                  device_id_type=pl.DeviceIdType.MESH)
                cp.start()
                sent.append(cp)
                arrivals.append(pltpu.make_async_remote_copy(
                    src_ref=landing, dst_ref=landing, send_sem=send_sems.at[kk], recv_sem=recv_sems.at[kk],
                    device_id=peer, device_id_type=pl.DeviceIdType.MESH))
                kk += 1
            for cp in arrivals:
                cp.wait_recv()
        for cp in sent:
            cp.wait_send()
        for cp in local_copies:
            cp.wait()

    any_spec = pl.BlockSpec(memory_space=pl.ANY)
    return pl.pallas_call(
        body, name=name,
        in_specs=[any_spec] * n_in, out_specs=[any_spec] * n_out, out_shape=list(out_shapes),
        scratch_shapes=[pltpu.SemaphoreType.DMA((n_remote,)), pltpu.SemaphoreType.DMA((n_remote,)),
                        pltpu.SemaphoreType.DMA((max(n_local, 1),))],
        input_output_aliases=aliases or {},
    )(*ins)


def _ds(start, size):
    return pl.ds(pl.multiple_of(start * size, 8), size)


def _all_gather8(name, v):
    masks = [(a, b, cc) for a in (0, 1) for b in (0, 1) for cc in (0, 1)][1:]

    def index(p):
        return 4 * p[0] + 2 * p[1] + p[2]

    def plan(in_refs, out_refs, me):
        (src,), (out,) = in_refs, out_refs
        local = [(src, out.at[index(me)])]
        phase = [(m, src, out.at[index(me)], out.at[index(_flip(me, m))]) for m in masks]
        return local, [phase]

    return _comm(name, [v], [jax.ShapeDtypeStruct((N_DEV,) + v.shape, v.dtype)], plan, len(masks), 1)[0]


def _gather_row(name, row):
    n = row.shape[1]
    n_pad = -(-n // (8 * LANES_V7X)) * (8 * LANES_V7X)
    v = jnp.pad(row, ((0, 0), (0, n_pad - n))).reshape(8, n_pad // 8)
    return _all_gather8(name, v).reshape(N_DEV, n_pad)[:, :n]


class _Sharded:
    def __init__(self, kind, rows, cols):
        self.kind, self.rows, self.cols = kind, rows, cols
        self.shard_shape = (rows, cols // N_CHIP) if kind == "col" else (rows // N_CHIP, cols)
        self.piece_shape = (rows // 2, cols // N_CHIP) if kind == "col" else (rows // N_CHIP, cols // 2)

    def piece_of_full(self, ref, s, h):
        if self.kind == "col":
            return ref.at[_ds(h, self.rows // 2), _ds(s, self.cols // N_CHIP)]
        return ref.at[_ds(s, self.rows // N_CHIP), _ds(h, self.cols // 2)]

    def half_of_shard(self, ref, h):
        if self.kind == "col":
            return ref.at[_ds(h, self.rows // 2), :]
        return ref.at[:, _ds(h, self.cols // 2)]


def _place_shard(meta, w, s_arr, after=None):
    r, cols = w.shape
    tr = _tile(r, 256, 16)
    nr = r // tr

    def body(s_ref, w_ref, *rest):
        rest[-1][...] = w_ref[...].astype(BF16)

    if meta.kind == "col":
        o_map = lambda i, s_ref: (i, s_ref[0])
    else:
        o_map = lambda i, s_ref: (i + s_ref[0] * nr, 0)
    return pl.pallas_call(
        body, name="place_shard",
        grid_spec=pltpu.PrefetchScalarGridSpec(
            num_scalar_prefetch=1, grid=(nr,),
            in_specs=[pl.BlockSpec((tr, cols), lambda i, s_ref: (i, 0))]
            + ([] if after is None else [pl.BlockSpec(memory_space=pl.ANY)]),
            out_specs=pl.BlockSpec((tr, cols), o_map)),
        out_shape=jax.ShapeDtypeStruct((meta.rows, meta.cols), BF16),
        compiler_params=_params(("parallel",)),
    )(*((s_arr, w) if after is None else (s_arr, w, after)))


def _gather_copies(metas, over_ici):
    def copies(fulls, me):
        x, y, c = me
        s_me = 2 * x + y
        out = []
        for meta, full in zip(metas, fulls):
            for mask, bits in CHIP_FLIPS:
                s_peer = jnp.bitwise_xor(s_me, bits)
                if over_ici:
                    out.append((mask, meta.piece_of_full(full, s_me, c), meta.piece_of_full(full, s_me, c),
                                meta.piece_of_full(full, s_peer, c)))
                else:
                    out.append((FLIP_C, meta.piece_of_full(full, s_peer, c), meta.piece_of_full(full, s_peer, c),
                                meta.piece_of_full(full, s_peer, 1 - c)))
        return out
    return copies


def _gather_forward(name, metas, fulls):
    nt = len(metas)
    copies = _gather_copies(metas, False)
    outs = [jax.ShapeDtypeStruct((m.rows, m.cols), BF16) for m in metas]
    return _comm(name, list(fulls), outs, lambda ins, outs_, me: ([], [copies(outs_, me)]), 3 * nt, 0,
                 aliases={i: i for i in range(nt)})


HBM_SPEC = pl.BlockSpec(memory_space=pltpu.HBM)
SEM_SPEC = pl.BlockSpec(memory_space=pltpu.SEMAPHORE)
SPLIT_EFFECT = pltpu.SideEffectType.DATAFLOW_SIDE_EFFECTING


def _split_start(name, bufs, groups, after):
    nb, ng = len(bufs), len(groups)
    n_in = nb + (0 if after is None else 1)

    def body(*refs):
        buf_refs = refs[:nb]
        sem_refs = refs[n_in:n_in + 2 * ng]
        token = refs[-1]
        me = (lax.axis_index("x"), lax.axis_index("y"), lax.axis_index("c"))
        for gi, (lo, n_bufs, copies, _) in enumerate(groups):
            for kk, (mask, src, dst, _) in enumerate(copies(buf_refs[lo:lo + n_bufs], me)):
                pltpu.make_async_remote_copy(src_ref=src, dst_ref=dst, send_sem=sem_refs[2 * gi].at[kk],
                                             recv_sem=sem_refs[2 * gi + 1].at[kk], device_id=_flip(me, mask),
                                             device_id_type=pl.DeviceIdType.MESH).start()
        token[...] = jnp.zeros_like(token)

    out_shape = []
    for _, _, _, n in groups:
        out_shape += [pltpu.SemaphoreType.DMA((n,)), pltpu.SemaphoreType.DMA((n,))]
    out_shape += [pltpu.HBM(b.shape, b.dtype) for b in bufs] + [jax.ShapeDtypeStruct((8, LANES_V7X), F32)]
    res = pl.pallas_call(
        body, name=name, out_shape=tuple(out_shape),
        in_specs=(HBM_SPEC,) * nb + (pl.BlockSpec(memory_space=pl.ANY),) * (n_in - nb),
        out_specs=(SEM_SPEC,) * (2 * ng) + (HBM_SPEC,) * nb + (pl.BlockSpec(memory_space=pltpu.VMEM),),
        input_output_aliases={i: 2 * ng + i for i in range(nb)},
        compiler_params=pltpu.CompilerParams(has_side_effects=SPLIT_EFFECT),
    )(*[pltpu.with_memory_space_constraint(b, pltpu.HBM) for b in bufs], *([] if after is None else [after]))
    sems = [(res[2 * gi], res[2 * gi + 1]) for gi in range(ng)]
    return sems, list(res[2 * ng:2 * ng + nb]), res[-1]


def _split_wait(name, sems, bufs, copies, after):
    nb = len(bufs)

    def body(*refs):
        buf_refs = refs[:nb]
        send_sems, recv_sems = refs[nb], refs[nb + 1]
        me = (lax.axis_index("x"), lax.axis_index("y"), lax.axis_index("c"))
        for kk, (mask, _, _, landing) in enumerate(copies(buf_refs, me)):
            cp = pltpu.make_async_remote_copy(src_ref=landing, dst_ref=landing, send_sem=send_sems.at[kk],
                                              recv_sem=recv_sems.at[kk], device_id=_flip(me, mask),
                                              device_id_type=pl.DeviceIdType.MESH)
            cp.wait_send()
            cp.wait_recv()

    return list(pl.pallas_call(
        body, name=name, out_shape=tuple(pltpu.HBM(b.shape, b.dtype) for b in bufs),
        in_specs=(HBM_SPEC,) * nb + (SEM_SPEC, SEM_SPEC, pl.BlockSpec(memory_space=pl.ANY)),
        out_specs=(HBM_SPEC,) * nb,
        input_output_aliases={i: i for i in range(nb)},
        compiler_params=pltpu.CompilerParams(has_side_effects=SPLIT_EFFECT),
    )(*bufs, sems[0], sems[1], after))


N_REDUCE_PIECES = 7


def _reduce_copies(metas):
    def copies(refs, me):
        x, y, c = me
        s_me = 2 * x + y
        out = []
        for m, g, land in zip(metas, refs[:len(metas)], refs[len(metas):]):
            for kk, (mask, bits) in enumerate(CHIP_FLIPS):
                s_peer = jnp.bitwise_xor(s_me, bits)
                out.append((mask, m.piece_of_full(g, s_peer, c), land.at[kk], land.at[kk]))
                out.append((mask[:2] + (1,), m.piece_of_full(g, s_peer, 1 - c), land.at[3 + kk], land.at[3 + kk]))
            out.append((FLIP_C, m.piece_of_full(g, s_me, 1 - c), land.at[6], land.at[6]))
        return out
    return copies


def _reduce_start(name, metas, grads):
    lands = [lax.empty((N_REDUCE_PIECES,) + m.piece_shape, BF16) for m in metas]
    bufs = list(grads) + lands
    sems, thru, token = _split_start(name, bufs, [(0, len(bufs), _reduce_copies(metas),
                                                   N_REDUCE_PIECES * len(metas))], None)
    return sems[0], thru, token


def _reduce_wait(name, metas, sems, thru, after):
    done = _split_wait(name, sems, thru, _reduce_copies(metas), after)
    return done[:len(metas)], done[len(metas):]


def _share_halves(metas, shards):
    def plan(in_refs, out_refs, me):
        c = me[2]
        phase = [(FLIP_C, m.half_of_shard(g, c), m.half_of_shard(g, c), m.half_of_shard(g, 1 - c))
                 for m, g in zip(metas, out_refs)]
        return [], [phase]

    outs = [jax.ShapeDtypeStruct(m.shard_shape, F32) for m in metas]
    return _comm("share_halves", list(shards), outs, plan, len(metas), 0,
                 aliases={i: i for i in range(len(metas))})


def _sum_pieces(meta, grad, landed, s_arr, c_arr):
    pr, pc = meta.piece_shape
    tr = _tile(pr, 256, 16)
    tc = _tile(pc, 2048)
    nr, ncol = pr // tr, pc // tc

    def body(s_ref, c_ref, p_ref, l_ref, o_ref):
        acc = p_ref[...].astype(F32)
        for kk in range(N_REDUCE_PIECES):
            acc = acc + l_ref[kk].astype(F32)
        o_ref[...] = acc

    if meta.kind == "col":
        p_map = lambda i, j, s_ref, c_ref: (i + c_ref[0] * nr, j + s_ref[0] * ncol)
        o_map = lambda i, j, s_ref, c_ref: (i + c_ref[0] * nr, j)
    else:
        p_map = lambda i, j, s_ref, c_ref: (i + s_ref[0] * nr, j + c_ref[0] * ncol)
        o_map = lambda i, j, s_ref, c_ref: (i, j + c_ref[0] * ncol)
    blk = (tr, tc)
    return pl.pallas_call(
        body, name="sum_pieces",
        grid_spec=pltpu.PrefetchScalarGridSpec(
            num_scalar_prefetch=2, grid=(nr, ncol),
            in_specs=[pl.BlockSpec(blk, p_map),
                      pl.BlockSpec((N_REDUCE_PIECES,) + blk, lambda i, j, s_ref, c_ref: (0, i, j))],
            out_specs=pl.BlockSpec(blk, o_map)),
        out_shape=jax.ShapeDtypeStruct(meta.shard_shape, F32),
        compiler_params=_params(("parallel", "parallel")),
    )(s_arr, c_arr, grad, landed)


def _adam_rows(rows, sel, fulls):
    w, g, m, v = rows
    m2 = ADAM_B1 * m + (1.0 - ADAM_B1) * g
    v2 = ADAM_B2 * v + (1.0 - ADAM_B2) * jnp.square(g)
    m_hat = m2 / (1.0 - ADAM_B1 ** ADAM_STEP)
    v_hat = v2 / (1.0 - ADAM_B2 ** ADAM_STEP)
    delta = -ADAM_LR * (m_hat / (jnp.sqrt(v_hat) + ADAM_EPS) + ADAM_WD * w)
    return [delta, m2, v2], []


def _adamw(w, g, m, v):
    r, c = w.shape
    tr = _tile(r, 128, 8)
    outs = _rowwise("adamw", _adam_rows, n_tiles=r // tr, tr=tr,
                    row_ins=[(w, 0, None), (g, 0, None), (m, 0, None), (v, 0, None)],
                    row_outs=[(r, c, F32, 0)] * 3)
    return outs[0], outs[1], outs[2]


def _ada_fwd(cg, w, b):
    d, n = w.shape
    tn = _tile(n, 512)

    def body(c_ref, w_ref, b_ref, o_ref):
        a = _silu(c_ref[...]).astype(BF16)
        o_ref[...] = jnp.dot(a, w_ref[...].astype(BF16), preferred_element_type=F32) + b_ref[...]

    return pl.pallas_call(
        body, name="ada_fwd", grid=(n // tn,),
        in_specs=[pl.BlockSpec(cg.shape, lambda j: (0, 0)), pl.BlockSpec((d, tn), lambda j: (0, j)),
                  pl.BlockSpec((1, tn), lambda j: (0, j))],
        out_specs=pl.BlockSpec((cg.shape[0], tn), lambda j: (0, j)),
        out_shape=jax.ShapeDtypeStruct((cg.shape[0], n), F32),
        compiler_params=_params(("parallel",)),
    )(cg, w, b)


def _ada_bwd(cg, dm, w):
    d, n = w.shape
    tn = _tile(n, 512)
    nj = n // tn

    def body(c_ref, dm_ref, w_ref, gw_ref, da_ref, acc):
        j = pl.program_id(0)

        @pl.when(j == 0)
        def _():
            acc[...] = jnp.zeros_like(acc)

        a = _silu(c_ref[...]).astype(BF16)
        dmv = dm_ref[...].astype(BF16)
        gw_ref[...] = lax.dot_general(a, dmv, TN_DIMS, preferred_element_type=F32)
        acc[...] += lax.dot_general(dmv, w_ref[...].astype(BF16), NT_DIMS, preferred_element_type=F32)

        @pl.when(j == nj - 1)
        def _():
            da_ref[...] = acc[...]

    return pl.pallas_call(
        body, name="ada_bwd", grid=(nj,),
        in_specs=[pl.BlockSpec(cg.shape, lambda j: (0, 0)), pl.BlockSpec((dm.shape[0], tn), lambda j: (0, j)),
                  pl.BlockSpec((d, tn), lambda j: (0, j))],
        out_specs=[pl.BlockSpec((d, tn), lambda j: (0, j)), pl.BlockSpec(cg.shape, lambda j: (0, 0))],
        out_shape=[jax.ShapeDtypeStruct((d, n), F32), jax.ShapeDtypeStruct(cg.shape, F32)],
        scratch_shapes=[pltpu.VMEM(cg.shape, F32)],
        compiler_params=_params(("arbitrary",)),
    )(cg, dm, w)


def _small_reduce(gathered, logits, n_mod_cols, lg_off, loss_off, loss_cols):
    npk = gathered.shape[1]

    def body(g_ref, lo_ref, tot_ref, gb_ref, gl_ref, loss_ref):
        acc = g_ref[0:1, :]
        for dd in range(1, N_DEV):
            acc = acc + g_ref[dd:dd + 1, :]
        tot_ref[...] = acc
        gb_ref[...] = acc[:, :n_mod_cols] + acc[:, n_mod_cols:2 * n_mod_cols]
        gl_ref[...] = acc[:, lg_off:lg_off + LANES_V7X] * _sigmoid(-lo_ref[...])
        loss = jnp.sum(acc[:, loss_off:loss_off + loss_cols], axis=1, keepdims=True)
        loss_ref[...] = jnp.broadcast_to(loss, loss_ref.shape)

    lane = jax.ShapeDtypeStruct((1, LANES_V7X), F32)
    return pl.pallas_call(
        body, name="small_reduce",
        out_shape=[jax.ShapeDtypeStruct((1, npk), F32), jax.ShapeDtypeStruct((1, n_mod_cols), F32), lane, lane],
    )(gathered, logits)


def _c_ctx_grad(parts, c_ctx):
    def body(p_ref, c_ref, o_ref):
        tot = p_ref[0:1, :] + p_ref[2:3, :] + p_ref[4:5, :] + p_ref[6:7, :]
        _, vjp = jax.vjp(_silu, c_ref[...])
        o_ref[...] = vjp(tot)[0]

    return pl.pallas_call(body, name="c_ctx_grad", out_shape=jax.ShapeDtypeStruct(c_ctx.shape, F32))(parts, c_ctx)


def _rope_tables(seq, ctx_rows):
    rows = seq // GRID_W
    half = HEAD_DIM // 2
    inv_freq = ROPE_THETA ** (-jnp.arange(0, half, 2, dtype=F32) / half)
    ang_row = jnp.arange(rows, dtype=F32)[:, None] * inv_freq
    ang_col = jnp.arange(GRID_W, dtype=F32)[:, None] * inv_freq

    def spread(fn):
        return jnp.concatenate([jnp.repeat(fn(ang_row), GRID_W, axis=0), jnp.tile(fn(ang_col), (rows, 1))], axis=-1)

    cos, sin = spread(jnp.cos), spread(jnp.sin)
    cos_full = jnp.repeat(cos, 2, axis=1)
    sin_signed = jnp.stack([-sin, sin], axis=-1).reshape(seq, HEAD_DIM)
    cos_full = jnp.concatenate([jnp.ones((ctx_rows, HEAD_DIM), F32), cos_full], axis=0)
    sin_signed = jnp.concatenate([jnp.zeros((ctx_rows, HEAD_DIM), F32), sin_signed], axis=0)
    return cos_full, sin_signed


def _qk_rot(p, gain, cos_full, sin_signed):
    r = _rmsn(p) * gain
    return r * cos_full + _swap_pairs(r) * sin_signed


def _qk_rot_bwd(g, p, gain, cos_full, sin_signed):
    g1 = g * cos_full + _swap_pairs(g * sin_signed)
    _, vjp = jax.vjp(lambda pp, gn: _rmsn(pp) * gn, p, gain)
    return vjp(g1)


def kernel(x, c, ctx, c_ctx, w_ada, b_ada, ffn1_w_in, ffn1_w_out, mix_w_in, attn_q_gain, attn_k_gain, ret_decay_logit, w_proj_attn, w_proj_ret, mix_w_out, ffn2_w_in, ffn2_w_out, final_norm, loss_target, m_c_ctx, m_w_ada, m_b_ada, m_ffn1_w_in, m_ffn1_w_out, m_mix_w_in, m_attn_q_gain, m_attn_k_gain, m_ret_decay_logit, m_w_proj_attn, m_w_proj_ret, m_mix_w_out, m_ffn2_w_in, m_ffn2_w_out, m_final_norm, v_c_ctx, v_w_ada, v_b_ada, v_ffn1_w_in, v_ffn1_w_out, v_mix_w_in, v_attn_q_gain, v_attn_k_gain, v_ret_decay_logit, v_w_proj_attn, v_w_proj_ret, v_mix_w_out, v_ffn2_w_in, v_ffn2_w_out, v_final_norm):
    xi, yi, ci = lax.axis_index("x"), lax.axis_index("y"), lax.axis_index("c")
    dev = 4 * xi + 2 * yi + ci
    s_me = 2 * xi + yi
    c_arr = jnp.reshape(ci, (1,)).astype(jnp.int32)
    s_arr = jnp.reshape(s_me, (1,)).astype(jnp.int32)

    t, d = x.shape[1], x.shape[2]
    tc = ctx.shape[1]
    tk = tc + t
    aw = w_proj_attn.shape[1]
    rw = w_proj_ret.shape[1]
    pw = mix_w_in.shape[2] * N_CHIP
    kvw = (pw - aw - 4 * rw - 2 * d) // 2
    groups = aw // kvw
    n_ret_heads = rw // HEAD_DIM
    mod_cols = N_MOD * d
    tr = _tile(tc, 256, 32)
    nt_all, nt_x, ctx_tiles = tk // tr, t // tr, tc // tr

    c_rows = _gather_row("gather_c", c)
    cg = jnp.concatenate([c_rows, c_ctx[None, :], jnp.zeros((7, d), F32)], axis=0)
    w_ada_l = w_ada[0]
    ada_cols = w_ada_l.shape[1]
    b_ada_l = lax.dynamic_slice_in_dim(b_ada, s_me * ada_cols, ada_cols, axis=1)
    mod_shard = _ada_fwd(cg, w_ada_l, b_ada_l)
    mod_g = _all_gather8("gather_mod", mod_shard)
    mod_full = jnp.concatenate([mod_g[0], mod_g[2], mod_g[4], mod_g[6]], axis=1)
    mod_x = lax.dynamic_slice_in_dim(mod_full, dev, 1, axis=0).reshape(N_MOD, d)
    mod_c = mod_full[8].reshape(N_MOD, d)
    mods = jnp.stack([mod_c, mod_x])

    big = [("col", ffn1_w_in), ("row", ffn1_w_out), ("col", mix_w_in), ("col", w_proj_attn), ("col", w_proj_ret),
           ("row", mix_w_out), ("col", ffn2_w_in), ("row", ffn2_w_out)]
    metas = []
    for kind, w in big:
        r_l, c_l = w.shape[1:]
        metas.append(_Sharded(kind, r_l, c_l * N_CHIP) if kind == "col" else _Sharded(kind, r_l * N_CHIP, c_l))
    layer_groups = ((0, 1), (1, 2), (2, 6), (6, 8))
    sems_first, placed_first, token = _split_start(
        "gather_start_first", [_place_shard(metas[0], big[0][1][0], s_arr)],
        [(0, 1, _gather_copies(metas[0:1], True), 3)], mods)
    placed_rest = [_place_shard(m, w[0], s_arr, after=token) for m, (_, w) in zip(metas[1:], big[1:])]
    sems_rest, placed_rest, token = _split_start(
        "gather_start_rest", placed_rest,
        [(lo - 1, hi - lo, _gather_copies(metas[lo:hi], True), 3 * (hi - lo)) for lo, hi in layer_groups[1:]], None)
    gather_sems, placed = sems_first + sems_rest, placed_first + placed_rest
    mods = mods + token[0, 0]

    def weights_of(gi, after):
        lo, hi = layer_groups[gi]
        arrived = _split_wait("gather_wait_%d" % gi, gather_sems[gi], placed[lo:hi],
                              _gather_copies(metas[lo:hi], True), after)
        return _gather_forward("gather_forward_%d" % gi, metas[lo:hi], arrived)

    def weights_early(gi, after):
        lo, hi = layer_groups[gi]
        arrived = _split_wait("gather_wait_%d" % gi, gather_sems[gi], placed[lo:hi],
                              _gather_copies(metas[lo:hi], True), after)
        sems, thru, tok = _split_start("gather_forward_start_%d" % gi, arrived,
                                       [(0, hi - lo, _gather_copies(metas[lo:hi], False), 3 * (hi - lo))], None)
        return gi, sems[0], thru, tok

    def weights_late(early, after):
        gi, sems, thru, _ = early
        lo, hi = layer_groups[gi]
        return _split_wait("gather_forward_wait_%d" % gi, sems, thru, _gather_copies(metas[lo:hi], False), after)

    cos_full, sin_signed = _rope_tables(t, tc)
    q_gain, k_gain = attn_q_gain, attn_k_gain
    log_gamma = jax.nn.log_sigmoid(ret_decay_logit[0])
    lgv = jnp.broadcast_to(log_gamma[:, :, None, None], (2, n_ret_heads, 1, HEAD_DIM))

    def stream_rows(h, off):
        if isinstance(h, tuple):
            return [(h[0], "ctx", None), (h[1], -ctx_tiles, None)]
        return [(h, off, None)]

    def stream_value(h, rows, fulls):
        if isinstance(h, tuple):
            return jnp.where(fulls[-1], rows[0], rows[1]), rows[2:]
        return rows[0], rows[1:]

    def norm_mod(name, h, n_tiles, off, i_shift, i_scale):
        def fn(rows, sel, fulls):
            hv, _ = stream_value(h, rows, fulls)
            return [_rmsn(hv) * (1.0 + sel(i_scale)) + sel(i_shift)], []
        return _rowwise(name, fn, n_tiles=n_tiles, tr=tr, row_ins=stream_rows(h, 0),
                        row_outs=[(n_tiles * tr, d, BF16, 0)], sel_in=mods, sel_off=off, ctx_rows=tc)[0]

    def resid_norm(name, h, h_off, f, n_tiles, off, i_gate, coef, i_shift, i_scale):
        def fn(rows, sel, fulls):
            hv, rest = stream_value(h, rows, fulls)
            hn = hv + coef * sel(i_gate) * rest[0]
            return [hn, _rmsn(hn) * (1.0 + sel(i_scale)) + sel(i_shift)], []
        return _rowwise(name, fn, n_tiles=n_tiles, tr=tr, row_ins=stream_rows(h, h_off) + [(f, 0, None)],
                        row_outs=[(f.shape[0], d, F32, 0), (f.shape[0], d, BF16, 0)], sel_in=mods, sel_off=off,
                        ctx_rows=tc)

    h0 = (ctx[0], x[0])
    n1 = norm_mod("norm_mod1", h0, nt_all, 0, 0, 1)
    w1i, = weights_of(0, n1)
    hm1, ua1, ub1 = _mm_swiglu("ffn1_in", n1, w1i)
    w1o, = weights_of(1, hm1)
    f1 = _mm("ffn1_out", hm1, w1o, "nn", BF16)
    mixer_weights = weights_early(2, f1)
    mods = mods + mixer_weights[3][0, 0]
    h1, n2 = resid_norm("resid_norm1", h0, 0, f1, nt_all, 0, 2, 0.5, 3, 4)
    wmi, wpa, wpr, wmo = weights_late(mixer_weights, n2)
    p_q = _mm("mix_in_q", n2, wmi, "nn", F32, 0, aw)
    p_kv = _mm("mix_in_kv", n2, wmi, "nn", F32, aw, 2 * kvw)
    p_r = _mm("mix_in_ret", n2, wmi, "nn", F32, aw + 2 * kvw, 3 * rw)
    p_gr = _mm("mix_in_gr", n2, wmi, "nn", BF16, aw + 2 * kvw + 3 * rw, rw)
    p_gab = _mm("mix_in_gab", n2, wmi, "nn", BF16, aw + 2 * kvw + 4 * rw, 2 * d)
    ffn2_weights = weights_early(3, p_gab)
    q_gain = q_gain + ffn2_weights[3][0, 0]

    def q_prep(rows, sel, fulls):
        p, cf, ss = rows
        return _heads_map(lambda ph: [_qk_rot(ph, fulls[0], cf, ss) * QSCALE], [p], aw), []

    q_rot = _rowwise("q_prep", q_prep, n_tiles=nt_x, tr=tr,
                     row_ins=[(p_q, ctx_tiles, None), (cos_full, ctx_tiles, None), (sin_signed, ctx_tiles, None)],
                     row_outs=[(t, aw, BF16, 0)], full_ins=[q_gain])[0]

    def kv_prep(rows, sel, fulls):
        p, cf, ss = rows
        k_rot = _heads_map(lambda ph: [_qk_rot(ph, fulls[0], cf, ss)], [p[:, :kvw]], kvw)[0]
        v_ones = _heads_map(lambda vh: [jnp.concatenate([vh, jnp.ones_like(vh)], axis=1)], [p[:, kvw:]], kvw)[0]
        return [k_rot, v_ones], []

    k_rot, v_att = _rowwise("kv_prep", kv_prep, n_tiles=nt_all, tr=tr,
                            row_ins=[(p_kv, 0, None), (cos_full, 0, None), (sin_signed, 0, None)],
                            row_outs=[(tk, kvw, BF16, 0), (tk, 2 * kvw, BF16, 0)], full_ins=[k_gain])

    ya, lse = _flash_fwd(q_rot, k_rot, v_att, groups)
    y_fwd, y_bwd, states = _ret_fwd(p_r, lgv, tc)

    def ret_out_fn(yf, yb, gr):
        return [_silu(gr) * _rmsn(yf + yb)]

    def ret_out(rows, sel, fulls):
        return _heads_map(ret_out_fn, rows, rw), []

    y_rows = [(y_fwd, ctx_tiles, None), (y_bwd, ctx_tiles, None), (p_gr, ctx_tiles, None)]
    yr = _rowwise("ret_out", ret_out, n_tiles=nt_x, tr=tr, row_ins=y_rows, row_outs=[(t, rw, BF16, 0)])[0]

    pa = _mm("proj_attn", ya, wpa, "nn", BF16)
    prj = _mm("proj_ret", yr, wpr, "nn", BF16)

    def merge_fn(a, r, ga, gb):
        return _sigmoid(ga) * a + _sigmoid(gb) * r

    gate_rows = [(p_gab, ctx_tiles, (d, 0)), (p_gab, ctx_tiles, (d, 1))]
    z = _rowwise("merge", lambda rows, sel, fulls: ([merge_fn(*rows)], []), n_tiles=nt_x, tr=tr,
                 row_ins=[(pa, 0, None), (prj, 0, None)] + gate_rows, row_outs=[(t, d, BF16, 0)])[0]
    fo = _mm("mix_out", z, wmo, "nn", BF16)
    h2, n3 = resid_norm("resid_norm2", h1, ctx_tiles, fo, nt_x, ctx_tiles, 5, 1.0, 6, 7)
    w2i, w2o = weights_late(ffn2_weights, n3)
    hm2, ua2, ub2 = _mm_swiglu("ffn2_in", n3, w2i)
    f2 = _mm("ffn2_out", hm2, w2o, "nn", BF16)

    def loss_fn(rows, sel, fulls):
        h2v, f2v, tgt = rows
        g3 = 0.5 * sel(8)
        y, vjp = jax.vjp(lambda hh, ww: _rmsn(hh) * ww, h2v + g3 * f2v, fulls[0])
        err = y - tgt
        dh, dw = vjp(err / d)
        return [dh, g3 * dh], [0.5 / d * jnp.sum(err * err, axis=0, keepdims=True), dw,
                               jnp.sum(0.5 * dh * f2v, axis=0, keepdims=True)]

    dh3, df2, loss_acc = _rowwise("loss_head", loss_fn, n_tiles=nt_x, tr=tr,
                                  row_ins=[(h2, 0, None), (f2, 0, None), (loss_target[0], 0, None)],
                                  row_outs=[(t, d, F32, 0), (t, d, BF16, 0)], sel_in=mods, sel_off=ctx_tiles,
                                  ctx_rows=tc, full_ins=[final_norm[None, :]], acc_shape=(8, d))
    loss_cols, g_final, dg3 = loss_acc[1, 0:1], loss_acc[1, 1:2], loss_acc[1, 2:3]

    def norm_mod_bwd(name, dn, h, dres, dres_off, n_tiles, off, i_shift, i_scale, gate=None, out_off=0):
        def fn(rows, sel, fulls):
            hh, rows = stream_value(h, rows, fulls)
            g, dr = rows[:2]
            if dres_off < 0:
                dr = jnp.where(fulls[-1], 0.0, dr)
            _, vjp = jax.vjp(lambda a, sh, sc: _rmsn(a) * (1.0 + sc) + sh, hh,
                             sel(i_shift), sel(i_scale))
            dhh, dsh, dsc = vjp(g)
            dh = dr + dhh
            if gate is None:
                return [dh], [dsh, dsc]
            return [dh, gate[2] * sel(gate[1]) * dh], [dsh, dsc, jnp.sum(gate[2] * dh * rows[2], axis=0, keepdims=True)]
        n_rows = dn.shape[0] + out_off * tr
        row_ins = stream_rows(h, 0) + [(dn, 0, None), (dres, dres_off, None)]
        row_outs = [(n_rows, d, F32, out_off)]
        if gate is not None:
            row_ins.append((gate[0], 0, None))
            row_outs.append((n_rows, d, BF16, out_off))
        return _rowwise(name, fn, n_tiles=n_tiles, tr=tr, row_ins=row_ins, row_outs=row_outs, sel_in=mods,
                        sel_off=off, ctx_rows=tc, acc_shape=(8, d))

    g_w2o = _mm("ffn2_out_dw", hm2, df2, "tn", BF16)
    du2 = _mm_swiglu_bwd("ffn2_out_dx", df2, w2o, ua2, ub2)
    g_w2i = _mm("ffn2_in_dw", n3, du2, "tn", BF16, stacked=True)
    dn3 = _mm("ffn2_in_dx", du2, w2i, "nt", F32, stacked=True)
    dh2, dfo, acc_n3 = norm_mod_bwd("norm_mod_bwd3", dn3, h2, dh3, 0, nt_x, ctx_tiles, 6, 7, gate=(fo, 5, 1.0))

    sems_ffn2, thru_ffn2, token = _reduce_start("reduce_start_ffn2", metas[6:8], [g_w2i, g_w2o])

    g_wmo = _mm("mix_out_dw", z, dfo, "tn", BF16, after=token)
    dz = _mm("mix_out_dx", dfo, wmo, "nt", F32)

    def merge_bwd(rows, sel, fulls):
        g, a, r, ga, gb = rows
        _, vjp = jax.vjp(merge_fn, a, r, ga, gb)
        da, dr, dga, dgb = vjp(g)
        return [da, dr, jnp.concatenate([dga, dgb], axis=1)], []

    dpa, dpr, dgab = _rowwise("merge_bwd", merge_bwd, n_tiles=nt_x, tr=tr,
                              row_ins=[(dz, 0, None), (pa, 0, None), (prj, 0, None)] + gate_rows,
                              row_outs=[(t, d, BF16, 0), (t, d, BF16, 0), (t, 2 * d, BF16, 0)])
    g_wpa = _mm("proj_attn_dw", ya, dpa, "tn", BF16)
    dya = _mm("proj_attn_dx", dpa, wpa, "nt", BF16)
    g_wpr = _mm("proj_ret_dw", yr, dpr, "tn", BF16)
    dyr = _mm("proj_ret_dx", dpr, wpr, "nt", F32)

    def ret_out_bwd(rows, sel, fulls):
        def per_head(g, yf, yb, gr):
            _, vjp = jax.vjp(lambda yy, gg: ret_out_fn(yy, 0.0, gg)[0], yf + yb, gr)
            return list(vjp(g))
        dy, dgr = _heads_map(per_head, rows, rw)
        return [dy, dgr], []

    dy_ret, dgr = _rowwise("ret_out_bwd", ret_out_bwd, n_tiles=nt_x, tr=tr, row_ins=[(dyr, 0, None)] + y_rows,
                           row_outs=[(t, rw, F32, 0), (t, rw, BF16, 0)])
    dp_rf, dp_rb, dlg = _ret_bwd(p_r, states, dy_ret, lgv, tc)
    dp_r = _rowwise("ret_bwd_sum", lambda rows, sel, fulls: ([rows[0] + rows[1]], []), n_tiles=nt_all, tr=tr,
                    row_ins=[(dp_rf, 0, None), (dp_rb, 0, None)], row_outs=[(tk, 3 * rw, BF16, 0)])[0]

    dq_rot, dk_rot, dv_att = _flash_bwd(q_rot, k_rot, v_att, ya, dya, lse, groups)

    def q_prep_bwd(rows, sel, fulls):
        g, p, cf, ss = rows
        gain_acc = []

        def per_head(gh, ph):
            dp, dgain = _qk_rot_bwd(gh * HEAD_DIM ** -0.5, ph, fulls[0], cf, ss)
            gain_acc.append(dgain)
            return [dp]
        dp = _heads_map(per_head, [g, p], aw)[0]
        return [dp], [functools.reduce(lambda a, b: a + b, gain_acc)]

    dp_q, acc_gq = _rowwise("q_prep_bwd", q_prep_bwd, n_tiles=nt_x, tr=tr,
                            row_ins=[(dq_rot, 0, None), (p_q, ctx_tiles, None), (cos_full, ctx_tiles, None),
                                     (sin_signed, ctx_tiles, None)],
                            row_outs=[(t, aw, BF16, 0)], full_ins=[q_gain], acc_shape=(8, HEAD_DIM),
                            sel_off=ctx_tiles, ctx_rows=tc)

    def kv_prep_bwd(rows, sel, fulls):
        gk, gv, p, cf, ss = rows
        gain_acc = []

        def per_head(gh, ph):
            dp, dgain = _qk_rot_bwd(gh, ph, fulls[0], cf, ss)
            gain_acc.append(dgain)
            return [dp]
        dpk = _heads_map(per_head, [gk, p], kvw)[0]
        return [jnp.concatenate([dpk, gv], axis=1)], [functools.reduce(lambda a, b: a + b, gain_acc)]

    dp_kv, acc_gk = _rowwise("kv_prep_bwd", kv_prep_bwd, n_tiles=nt_all, tr=tr,
                             row_ins=[(dk_rot, 0, None), (dv_att, 0, None), (p_kv, 0, (kvw, 0)), (cos_full, 0, None),
                                      (sin_signed, 0, None)],
                             row_outs=[(tk, 2 * kvw, BF16, 0)], full_ins=[k_gain], acc_shape=(8, HEAD_DIM),
                             sel_off=0, ctx_rows=tc)

    def with_ctx_zeros(a):
        return jnp.concatenate([jnp.zeros((tc, a.shape[1]), a.dtype), a], axis=0)

    dp = jnp.concatenate([with_ctx_zeros(dp_q), dp_kv, dp_r, with_ctx_zeros(dgr), with_ctx_zeros(dgab)], axis=1)
    g_wmi = _mm("mix_in_dw", n2, dp, "tn", BF16)
    dn2 = _mm("mix_in_dx", dp, wmi, "nt", F32)
    dh1, df1, acc_n2 = norm_mod_bwd("norm_mod_bwd2", dn2, h1, dh2, -ctx_tiles, nt_all, 0, 3, 4, gate=(f1, 2, 0.5))
    sems_mix, thru_mix, token = _reduce_start("reduce_start_mix", metas[2:6], [g_wmi, g_wpa, g_wpr, g_wmo])

    g_w1o = _mm("ffn1_out_dw", hm1, df1, "tn", BF16, after=token)
    sems_w1o, thru_w1o, token = _reduce_start("reduce_start_ffn1_out", metas[1:2], [g_w1o])
    du1 = _mm_swiglu_bwd("ffn1_out_dx", df1, w1o, ua1, ub1, after=token)
    g_w1i = _mm("ffn1_in_dw", n1, du1, "tn", BF16, stacked=True)
    sems_w1i, thru_w1i, token = _reduce_start("reduce_start_ffn1_in", metas[0:1], [g_w1i])
    dn1 = _mm("ffn1_in_dx", du1, w1i, "nt", F32, after=token, stacked=True)
    dh0, acc_n1 = norm_mod_bwd("norm_mod_bwd1", dn1, h0, dh1, 0, nt_all, 0, 0, 1, out_off=-ctx_tiles)
    grad_x = dh0[None]

    grads_own, landed = [], []
    for name, lo, hi, sems_l, thru_l in (("reduce_wait_ffn1_in", 0, 1, sems_w1i, thru_w1i),
                                         ("reduce_wait_ffn1_out", 1, 2, sems_w1o, thru_w1o),
                                         ("reduce_wait_mix", 2, 6, sems_mix, thru_mix),
                                         ("reduce_wait_ffn2", 6, 8, sems_ffn2, thru_ffn2)):
        grads_l, landed_l = _reduce_wait(name, metas[lo:hi], sems_l, thru_l, dh0)
        grads_own += grads_l
        landed += landed_l
    pieces = [_sum_pieces(m, g, l, s_arr, c_arr) for m, g, l in zip(metas, grads_own, landed)]
    grads_big = _share_halves(metas, pieces)

    zero_row = jnp.zeros((1, d), F32)
    dmod_x = jnp.concatenate([acc_n1[1, 0:1], acc_n1[1, 1:2], acc_n2[1, 2:3], acc_n2[1, 0:1], acc_n2[1, 1:2],
                              acc_n3[1, 2:3], acc_n3[1, 0:1], acc_n3[1, 1:2], dg3], axis=1)
    dmod_c = jnp.concatenate([acc_n1[0, 0:1], acc_n1[0, 1:2], acc_n2[0, 2:3], acc_n2[0, 0:1], acc_n2[0, 1:2]]
                             + [zero_row] * 4, axis=1)
    dlg_row = jnp.pad(dlg[:, :, 0, 0].reshape(1, 2 * n_ret_heads), ((0, 0), (0, LANES_V7X - 2 * n_ret_heads)))
    packed = jnp.concatenate([dmod_x, dmod_c, acc_gq[1, 0:1], acc_gk[0, 0:1] + acc_gk[1, 0:1], dlg_row,
                              g_final, loss_cols], axis=1)
    off_gq = 2 * mod_cols
    off_gk = off_gq + LANES_V7X
    off_lg = off_gk + LANES_V7X
    off_fn = off_lg + LANES_V7X
    off_loss = off_fn + d
    gathered = _gather_row("gather_small", packed)
    logits_row = jnp.pad(ret_decay_logit.reshape(1, 2 * n_ret_heads), ((0, 0), (0, LANES_V7X - 2 * n_ret_heads)))
    totals, g_b_ada, g_decay, loss_row = _small_reduce(gathered, logits_row, mod_cols, off_lg, off_loss, d)
    loss = loss_row[0, 0]

    dm = jnp.concatenate([gathered[:, :mod_cols], totals[:, mod_cols:2 * mod_cols],
                          jnp.zeros((7, mod_cols), F32)], axis=0)
    dm_l = lax.dynamic_slice_in_dim(dm, s_me * ada_cols, ada_cols, axis=1)
    g_w_ada, da_part = _ada_bwd(cg, dm_l, w_ada_l)
    da_rows = _gather_row("gather_dc", da_part[8:9])
    g_c_ctx = _c_ctx_grad(da_rows, c_ctx[None, :])

    def as2d(a):
        return a.reshape(-1, a.shape[-1])

    grads = {
        "c_ctx": g_c_ctx, "w_ada": g_w_ada, "b_ada": g_b_ada,
        "ffn1_w_in": grads_big[0], "ffn1_w_out": grads_big[1], "mix_w_in": grads_big[2],
        "attn_q_gain": totals[:, off_gq:off_gq + HEAD_DIM], "attn_k_gain": totals[:, off_gk:off_gk + HEAD_DIM],
        "ret_decay_logit": g_decay[:, :2 * n_ret_heads],
        "w_proj_attn": grads_big[3], "w_proj_ret": grads_big[4], "mix_w_out": grads_big[5],
        "ffn2_w_in": grads_big[6], "ffn2_w_out": grads_big[7], "final_norm": totals[:, off_fn:off_fn + d],
    }
    weights = {"c_ctx": (c_ctx, m_c_ctx, v_c_ctx), "w_ada": (w_ada, m_w_ada, v_w_ada),
               "b_ada": (b_ada, m_b_ada, v_b_ada), "ffn1_w_in": (ffn1_w_in, m_ffn1_w_in, v_ffn1_w_in),
               "ffn1_w_out": (ffn1_w_out, m_ffn1_w_out, v_ffn1_w_out), "mix_w_in": (mix_w_in, m_mix_w_in, v_mix_w_in),
               "attn_q_gain": (attn_q_gain, m_attn_q_gain, v_attn_q_gain),
               "attn_k_gain": (attn_k_gain, m_attn_k_gain, v_attn_k_gain),
               "ret_decay_logit": (ret_decay_logit, m_ret_decay_logit, v_ret_decay_logit),
               "w_proj_attn": (w_proj_attn, m_w_proj_attn, v_w_proj_attn),
               "w_proj_ret": (w_proj_ret, m_w_proj_ret, v_w_proj_ret), "mix_w_out": (mix_w_out, m_mix_w_out, v_mix_w_out),
               "ffn2_w_in": (ffn2_w_in, m_ffn2_w_in, v_ffn2_w_in), "ffn2_w_out": (ffn2_w_out, m_ffn2_w_out, v_ffn2_w_out),
               "final_norm": (final_norm, m_final_norm, v_final_norm)}
    out_g, out_d, out_m, out_v = [], [], [], []
    for name, (w, m, v) in weights.items():
        shape = w.shape
        if name == "ret_decay_logit":
            w2, m2, v2 = (a.reshape(1, -1) for a in (w, m, v))
        else:
            w2, m2, v2 = as2d(w), as2d(m), as2d(v)
        g2 = grads[name].reshape(w2.shape)
        delta, new_m, new_v = _adamw(w2, g2, m2, v2)
        out_g.append(g2.reshape(shape))
        out_d.append(delta.reshape(shape))
        out_m.append(new_m.reshape(shape))
        out_v.append(new_v.reshape(shape))
    return (loss, grad_x, *out_g, *out_d, *out_m, *out_v)
```

```python
import functools
import math

import jax
import jax.numpy as jnp
from jax import lax
from jax.experimental import pallas as pl
from jax.experimental.pallas import tpu as pltpu

F32 = jnp.float32
BF16 = jnp.bfloat16

HEAD_DIM = 128
GRID_W = 64
ROPE_THETA = 10000.0
NORM_EPS = 1e-6
N_MOD = 9
RET_CHUNK = 128
ADAM_LR = 0.001
ADAM_B1 = 0.9
ADAM_B2 = 0.999
ADAM_EPS = 1e-08
ADAM_WD = 0.01
ADAM_STEP = 10

N_DEV = 8
N_CHIP = 4
LANES_V7X = 128
MXU_WIDTH_V7X = 256
VMEM_LIMIT_V7X = 52 * 1024 * 1024

NT_DIMS = (((1,), (1,)), ((), ()))
TN_DIMS = (((0,), (0,)), ((), ()))
NN_DIMS = (((1,), (0,)), ((), ()))


def _tile(n, pref, mult=LANES_V7X):
    if n <= pref:
        return n
    t = (pref // mult) * mult
    while t >= mult:
        if n % t == 0:
            return t
        t -= mult
    return n


def _params(sem):
    return pltpu.CompilerParams(dimension_semantics=sem, vmem_limit_bytes=VMEM_LIMIT_V7X)


def _sigmoid(x):
    return 1.0 / (1.0 + jnp.exp(-x))


def _silu(x):
    return x * _sigmoid(x)


def _rmsn(x):
    return x * lax.rsqrt(jnp.mean(x * x, axis=-1, keepdims=True) + NORM_EPS)


MM_VMEM_BUDGET = 44 * 1024 * 1024


def _divisor_tiles(n, cap):
    ts = [t for t in range(LANES_V7X, min(n, cap) + 1, LANES_V7X) if n % t == 0]
    return ts or [n]


def _mm_tiles(m, n, tk, out_bytes, has_acc):
    best = None
    for tm in _divisor_tiles(m, 1536):
        for tn in _divisor_tiles(n, 2560):
            need = 4 * tk * (tm + tn) + 2 * tm * tn * out_bytes + 4 * tm * tn
            if need > MM_VMEM_BUDGET:
                continue
            score = tm * tn / (tm + tn)
            for tdim in (tm, tn):
                if tdim % MXU_WIDTH_V7X:
                    score *= 0.85
            if best is None or score > best[0]:
                best = (score, tm, tn)
    return best[1], best[2]


def _mm(name, a, b, mode, out_dtype, b_off=0, n=None, after=None, stacked=False):
    half = 0
    if mode == "nn":
        m, k = a.shape
        n = b.shape[1] if n is None else n
        dims = NN_DIMS
    elif mode == "nt":
        if stacked:
            _, m, half = a.shape
            k = 2 * half
        else:
            m, k = a.shape
        n = b.shape[0]
        dims = NT_DIMS
    else:
        k, m = a.shape
        if stacked:
            half = b.shape[2]
            n = 2 * half
        else:
            n = b.shape[1]
        dims = TN_DIMS
    tk = _tile(half if (stacked and mode == "nt") else k, 2816)
    nk = k // tk
    n_tiled = half if (stacked and mode == "tn") else n
    tm, tn = _mm_tiles(m, n_tiled, tk, jnp.dtype(out_dtype).itemsize, nk > 1)
    per_half = (half // tk) if mode == "nt" else (half // tn)

    def body(a_ref, b_ref, *rest):
        o_ref = rest[0 if after is None else 1]
        if nk == 1:
            o_ref[...] = lax.dot_general(a_ref[...], b_ref[...], dims,
                                         preferred_element_type=F32).astype(o_ref.dtype)
            return
        acc_ref = rest[-1]
        kk = pl.program_id(2)

        @pl.when(kk == 0)
        def _():
            acc_ref[...] = jnp.zeros_like(acc_ref)

        acc_ref[...] += lax.dot_general(a_ref[...], b_ref[...], dims, preferred_element_type=F32)

        @pl.when(kk == nk - 1)
        def _():
            o_ref[...] = acc_ref[...].astype(o_ref.dtype)

    if mode == "nn":
        a_spec = pl.BlockSpec((tm, tk), lambda i, j, kk: (i, kk))
        b_spec = pl.BlockSpec((tk, tn), lambda i, j, kk: (kk, j))
        if b_off:
            b_spec = pl.BlockSpec((pl.Element(tk), pl.Element(tn)),
                                  lambda i, j, kk: (pl.multiple_of(kk * tk, LANES_V7X),
                                                    pl.multiple_of(b_off + j * tn, LANES_V7X)))
    elif mode == "nt":
        a_spec = pl.BlockSpec((tm, tk), lambda i, j, kk: (i, kk))
        if stacked:
            a_spec = pl.BlockSpec((None, tm, tk), lambda i, j, kk: (kk // per_half, i, kk % per_half))
        b_spec = pl.BlockSpec((tn, tk), lambda i, j, kk: (j, kk))
    else:
        a_spec = pl.BlockSpec((tk, tm), lambda i, j, kk: (kk, i))
        b_spec = pl.BlockSpec((tk, tn), lambda i, j, kk: (kk, j))
        if stacked:
            b_spec = pl.BlockSpec((None, tk, tn), lambda i, j, kk: (j // per_half, kk, j % per_half))
    return pl.pallas_call(
        body, name=name, grid=(m // tm, n // tn, nk),
        in_specs=[a_spec, b_spec] + ([] if after is None else [pl.BlockSpec(memory_space=pl.ANY)]),
        out_specs=pl.BlockSpec((tm, tn), lambda i, j, kk: (i, j)),
        out_shape=jax.ShapeDtypeStruct((m, n), out_dtype),
        scratch_shapes=[pltpu.VMEM((tm, tn), F32)] if nk > 1 else [],
        compiler_params=_params(("parallel", "parallel", "arbitrary")),
    )(*((a, b) if after is None else (a, b, after)))


def _mm_swiglu_bwd(name, dy, w_out, ua, ub, after=None):
    m, d_model = dy.shape
    f = w_out.shape[0]
    tm = _tile(m, 1024)
    tn = _tile(f, 512)

    def body(dy_ref, w_ref, ua_ref, ub_ref, *rest):
        du_ref = rest[-1]
        g = lax.dot_general(dy_ref[...], w_ref[...], NT_DIMS, preferred_element_type=F32)
        _, vjp = jax.vjp(lambda aa, bb: _silu(aa) * bb, ua_ref[...].astype(F32), ub_ref[...].astype(F32))
        da, db = vjp(g)
        du_ref[0] = da.astype(BF16)
        du_ref[1] = db.astype(BF16)

    tile = pl.BlockSpec((tm, tn), lambda i, j: (i, j))
    return pl.pallas_call(
        body, name=name, grid=(m // tm, f // tn),
        in_specs=[pl.BlockSpec((tm, d_model), lambda i, j: (i, 0)), pl.BlockSpec((tn, d_model), lambda i, j: (j, 0)),
                  tile, tile] + ([] if after is None else [pl.BlockSpec(memory_space=pl.ANY)]),
        out_specs=pl.BlockSpec((2, tm, tn), lambda i, j: (0, i, j)),
        out_shape=jax.ShapeDtypeStruct((2, m, f), BF16),
        compiler_params=_params(("parallel", "parallel")),
    )(*((dy, w_out, ua, ub) if after is None else (dy, w_out, ua, ub, after)))


def _mm_swiglu(name, a, w):
    m, k = a.shape
    f = w.shape[1] // 2
    tm = _tile(m, 1024)
    tn = _tile(f, 512)
    tk = _tile(k, 2560)
    nk = k // tk
    jf = f // tn

    def body(a_ref, wa_ref, wb_ref, h_ref, ua_ref, ub_ref, acca, accb):
        kk = pl.program_id(2)

        @pl.when(kk == 0)
        def _():
            acca[...] = jnp.zeros_like(acca)
            accb[...] = jnp.zeros_like(accb)

        av = a_ref[...]
        acca[...] += jnp.dot(av, wa_ref[...], preferred_element_type=F32)
        accb[...] += jnp.dot(av, wb_ref[...], preferred_element_type=F32)

        @pl.when(kk == nk - 1)
        def _():
            ua = acca[...]
            ub = accb[...]
            h_ref[...] = (_silu(ua) * ub).astype(BF16)
            ua_ref[...] = ua.astype(BF16)
            ub_ref[...] = ub.astype(BF16)

    o_spec = pl.BlockSpec((tm, tn), lambda i, j, kk: (i, j))
    o_shape = jax.ShapeDtypeStruct((m, f), BF16)
    return pl.pallas_call(
        body, name=name, grid=(m // tm, jf, nk),
        in_specs=[pl.BlockSpec((tm, tk), lambda i, j, kk: (i, kk)),
                  pl.BlockSpec((tk, tn), lambda i, j, kk: (kk, j)),
                  pl.BlockSpec((tk, tn), lambda i, j, kk: (kk, j + jf))],
        out_specs=[o_spec, o_spec, o_spec],
        out_shape=[o_shape, o_shape, o_shape],
        scratch_shapes=[pltpu.VMEM((tm, tn), F32), pltpu.VMEM((tm, tn), F32)],
        compiler_params=_params(("parallel", "parallel", "arbitrary")),
    )(a, w, w)


def _rowwise(name, fn, *, n_tiles, tr, row_ins, row_outs, sel_in=None, sel_off=0, ctx_rows=0,
             full_ins=(), acc_shape=None):
    sr = 256 if tr % 256 == 0 else (128 if tr % 128 == 0 else (32 if tr % 32 == 0 else tr))
    n_row, n_full, n_out = len(row_ins), len(full_ins), len(row_outs)
    has_sel = sel_in is not None
    has_acc = acc_shape is not None

    def sel_of(i):
        return jnp.where((i + sel_off) * tr < ctx_rows, 0, 1)

    def body(*refs):
        row_refs = refs[:n_row]
        pos = n_row
        sel_ref = None
        if has_sel:
            sel_ref = refs[pos]
            pos += 1
        full_refs = refs[pos:pos + n_full]
        pos += n_full
        out_refs = refs[pos:pos + n_out]
        pos += n_out
        acc_ref = refs[pos] if has_acc else None
        i = pl.program_id(0)
        if has_acc:
            first = (i == 0) | ((i + sel_off) * tr == ctx_rows)

            @pl.when(first)
            def _():
                acc_ref[...] = jnp.zeros_like(acc_ref)

        sel = (lambda kk: sel_ref[kk:kk + 1, :]) if has_sel else None
        fulls = [r[...] for r in full_refs] + [(i + sel_off) * tr < ctx_rows]

        def slab(r, carry):
            rs = pl.ds(pl.multiple_of(r * sr, sr), sr)
            rows = [ref[rs, :].astype(F32) for ref in row_refs]
            outs, accs = fn(rows, sel, fulls)
            for o_ref, o in zip(out_refs, outs):
                o_ref[rs, :] = o.astype(o_ref.dtype)
            for kk, a in enumerate(accs):
                acc_ref[kk:kk + 1, :a.shape[1]] += a
            return carry

        lax.fori_loop(0, tr // sr, slab, 0)

    def row_map(off, col=0):
        if off == "ctx":
            return lambda i: (jnp.minimum(i, ctx_rows // tr - 1), col)
        if off < 0:
            return lambda i: (jnp.maximum(i + off, 0), col)
        return lambda i: (i + off, col)

    in_specs, args = [], []
    for arr, off, blk in row_ins:
        if blk is None:
            in_specs.append(pl.BlockSpec((tr, arr.shape[1]), row_map(off)))
        else:
            in_specs.append(pl.BlockSpec((tr, blk[0]), row_map(off, blk[1])))
        args.append(arr)
    if has_sel:
        in_specs.append(pl.BlockSpec((None,) + sel_in.shape[1:], lambda i: (sel_of(i), 0, 0)))
        args.append(sel_in)
    for arr in full_ins:
        in_specs.append(pl.BlockSpec(arr.shape, lambda i: (0, 0)))
        args.append(arr)
    out_specs, out_shape = [], []
    for rows, cols, dt, off in row_outs:
        out_specs.append(pl.BlockSpec((tr, cols), row_map(off)))
        out_shape.append(jax.ShapeDtypeStruct((rows, cols), dt))
    if has_acc:
        out_specs.append(pl.BlockSpec((None,) + tuple(acc_shape), lambda i: (sel_of(i), 0, 0)))
        out_shape.append(jax.ShapeDtypeStruct((2,) + tuple(acc_shape), F32))
    return pl.pallas_call(
        body, name=name, grid=(n_tiles,), in_specs=in_specs, out_specs=out_specs, out_shape=out_shape,
        compiler_params=_params(("arbitrary",)),
    )(*args)


def _swap_pairs(x):
    lane = lax.broadcasted_iota(jnp.int32, x.shape, 1)
    nxt = pltpu.roll(x, x.shape[1] - 1, 1)
    prv = pltpu.roll(x, 1, 1)
    return jnp.where(lane % 2 == 0, nxt, prv)


def _heads_map(fn, arrs, width):
    outs = None
    for h in range(width // HEAD_DIM):
        sl = slice(h * HEAD_DIM, (h + 1) * HEAD_DIM)
        res = fn(*[a[:, sl] for a in arrs])
        if outs is None:
            outs = [[] for _ in res]
        for lst, r in zip(outs, res):
            lst.append(r)
    return [jnp.concatenate(lst, axis=1) if len(lst) > 1 else lst[0] for lst in outs]


QSCALE = HEAD_DIM ** -0.5 * math.log2(math.e)
LN2 = math.log(2.0)


def _lane_chunks(a):
    return [a[:, cc * LANES_V7X:(cc + 1) * LANES_V7X] for cc in range(a.shape[1] // LANES_V7X)]


def _row_bcast(col, like):
    return jnp.broadcast_to(col, like.shape)


def _flash_tiles(t, tk_all, key_pref):
    return _tile(t, 1024), _tile(tk_all, key_pref)


def _flash_fwd(q, k, vx, groups):
    t, aw = q.shape
    tk_all, kvw = k.shape
    kvh = kvw // HEAD_DIM
    gw = groups * HEAD_DIM
    tq, tk = _flash_tiles(t, tk_all, 1536)
    nk = tk_all // tk

    def body(q_ref, k_ref, v_ref, o_ref, lse_ref, m_sc, l_sc, acc_sc):
        j = pl.program_id(2)

        @pl.when(j == 0)
        def _():
            m_sc[...] = jnp.full_like(m_sc, -jnp.inf)
            l_sc[...] = jnp.zeros_like(l_sc)
            acc_sc[...] = jnp.zeros_like(acc_sc)

        kt = k_ref[...]
        vt = v_ref[...]
        for g in range(groups):
            sl = slice(g * HEAD_DIM, (g + 1) * HEAD_DIM)
            s = _lane_chunks(lax.dot_general(q_ref[:, sl], kt, NT_DIMS, preferred_element_type=F32))
            mx = functools.reduce(jnp.maximum, s)
            m_prev = m_sc[g]
            m_new = jnp.maximum(m_prev, _row_bcast(jnp.max(mx, axis=1, keepdims=True), mx))
            p = jnp.concatenate([jnp.exp2(sc - m_new).astype(BF16) for sc in s], axis=1)
            alpha = jnp.exp2(m_prev - m_new)
            pv = jnp.dot(p, vt, preferred_element_type=F32)
            acc_sc[g] = alpha * acc_sc[g] + pv[:, :HEAD_DIM]
            l_sc[g] = alpha * l_sc[g] + pv[:, HEAD_DIM:]
            m_sc[g] = m_new

        @pl.when(j == nk - 1)
        def _():
            for g in range(groups):
                sl = slice(g * HEAD_DIM, (g + 1) * HEAD_DIM)
                o_ref[:, sl] = (acc_sc[g] / l_sc[g]).astype(o_ref.dtype)
                lse_ref[:, sl] = m_sc[g] + jnp.log2(l_sc[g])

    qs = pl.BlockSpec((tq, gw), lambda kh, i, j: (i, kh))
    sc = pltpu.VMEM((groups, tq, HEAD_DIM), F32)
    return pl.pallas_call(
        body, name="flash_fwd", grid=(kvh, t // tq, nk),
        in_specs=[qs, pl.BlockSpec((tk, HEAD_DIM), lambda kh, i, j: (j, kh)),
                  pl.BlockSpec((tk, 2 * HEAD_DIM), lambda kh, i, j: (j, kh))],
        out_specs=[qs, qs],
        out_shape=[jax.ShapeDtypeStruct((t, aw), BF16), jax.ShapeDtypeStruct((t, aw), F32)],
        scratch_shapes=[sc, sc, sc],
        compiler_params=_params(("parallel", "parallel", "arbitrary")),
    )(q, k, vx)


def _flash_p_ds(q, kt, vt, do, lse, delta):
    s = _lane_chunks(lax.dot_general(q, kt, NT_DIMS, preferred_element_type=F32))
    dp = _lane_chunks(lax.dot_general(do, vt, NT_DIMS, preferred_element_type=F32))
    p = [jnp.exp2(sc - lse) for sc in s]
    ds = jnp.concatenate([(pc * (dc - delta)).astype(BF16) for pc, dc in zip(p, dp)], axis=1)
    return jnp.concatenate([pc.astype(BF16) for pc in p], axis=1), ds


def _flash_delta(do, o):
    prod = do.astype(F32) * o.astype(F32)
    return _row_bcast(jnp.sum(prod, axis=1, keepdims=True), prod)


def _flash_bwd(q, k, vx, o, do, lse, groups):
    t, aw = q.shape
    tk_all, kvw = k.shape
    kvh = kvw // HEAD_DIM
    gw = groups * HEAD_DIM
    tq, tk = _flash_tiles(t, tk_all, 1024)
    nq, nk = t // tq, tk_all // tk

    def body(q_ref, k_ref, v_ref, o_ref, do_ref, lse_ref, dq_ref, dk_ref, dv_ref, dq_sc, dk_acc, dv_acc):
        j = pl.program_id(1)
        i = pl.program_id(2)

        @pl.when(i == 0)
        def _():
            dk_acc[...] = jnp.zeros_like(dk_acc)
            dv_acc[...] = jnp.zeros_like(dv_acc)

        @pl.when(j == 0)
        def _():
            dq_sc[i] = jnp.zeros((groups, tq, HEAD_DIM), F32)

        kt = k_ref[...]
        vt = v_ref[:, :HEAD_DIM]
        for g in range(groups):
            sl = slice(g * HEAD_DIM, (g + 1) * HEAD_DIM)
            qv = q_ref[:, sl]
            dov = do_ref[:, sl]
            p, ds = _flash_p_ds(qv, kt, vt, dov, lse_ref[:, sl], _flash_delta(dov, o_ref[:, sl]))
            dv_acc[...] += lax.dot_general(p, dov, TN_DIMS, preferred_element_type=F32)
            dk_acc[...] += lax.dot_general(ds, qv, TN_DIMS, preferred_element_type=F32)
            dq_sc[i, g] += jnp.dot(ds, kt, preferred_element_type=F32)

        @pl.when(i == nq - 1)
        def _():
            dk_ref[...] = dk_acc[...] * LN2
            dv_ref[...] = dv_acc[...]

        @pl.when(j == nk - 1)
        def _():
            for g in range(groups):
                dq_ref[:, g * HEAD_DIM:(g + 1) * HEAD_DIM] = dq_sc[i, g]

    qs = pl.BlockSpec((tq, gw), lambda kh, j, i: (i, kh))
    ks = pl.BlockSpec((tk, HEAD_DIM), lambda kh, j, i: (j, kh))
    dq_spec = pl.BlockSpec((tq, gw), lambda kh, j, i: (jnp.where(j == nk - 1, i, 0), kh))
    return pl.pallas_call(
        body, name="flash_bwd", grid=(kvh, nk, nq),
        in_specs=[qs, ks, pl.BlockSpec((tk, 2 * HEAD_DIM), lambda kh, j, i: (j, kh)), qs, qs, qs],
        out_specs=[dq_spec, ks, ks],
        out_shape=[jax.ShapeDtypeStruct((t, aw), F32), jax.ShapeDtypeStruct((tk_all, kvw), F32),
                   jax.ShapeDtypeStruct((tk_all, kvw), F32)],
        scratch_shapes=[pltpu.VMEM((nq, groups, tq, HEAD_DIM), F32), pltpu.VMEM((tk, HEAD_DIM), F32),
                        pltpu.VMEM((tk, HEAD_DIM), F32)],
        compiler_params=_params(("parallel", "arbitrary", "arbitrary")),
    )(q, k, vx, o, do, lse)


def _bf_nn(a, b):
    return jnp.dot(a.astype(BF16), b.astype(BF16), preferred_element_type=F32)


def _bf_nt(a, b):
    return lax.dot_general(a.astype(BF16), b.astype(BF16), NT_DIMS, preferred_element_type=F32)


def _bf_tn(a, b):
    return lax.dot_general(a.astype(BF16), b.astype(BF16), TN_DIMS, preferred_element_type=F32)


@jax.custom_vjp
def _d_nn(a, b):
    return _bf_nn(a, b)


@jax.custom_vjp
def _d_nt(a, b):
    return _bf_nt(a, b)


@jax.custom_vjp
def _d_tn(a, b):
    return _bf_tn(a, b)


_d_nn.defvjp(lambda a, b: (_bf_nn(a, b), (a, b)), lambda r, g: (_d_nt(g, r[1]), _d_tn(r[0], g)))
_d_nt.defvjp(lambda a, b: (_bf_nt(a, b), (a, b)), lambda r, g: (_d_nn(g, r[1]), _d_tn(g, r[0])))
_d_tn.defvjp(lambda a, b: (_bf_tn(a, b), (a, b)), lambda r, g: (_d_nt(r[1], g), _d_nn(r[0], g)))


def _ret_chunk(q, k_raw, v, state, lg, rev, dots):
    nn, nt, tn = dots
    c = RET_CHUNK
    tcol = lax.broadcasted_iota(jnp.int32, (c, 1), 0).astype(F32)
    trow = lax.broadcasted_iota(jnp.int32, (1, c), 1).astype(F32)
    ucol = jnp.where(rev, c - 1.0 - tcol, tcol)
    urow = jnp.where(rev, c - 1.0 - trow, trow)
    e = ucol - urow
    low = e >= 0
    intra = jnp.where(low, jnp.exp(jnp.where(low, e, 0.0) * lg), 0.0)
    k = k_raw * (HEAD_DIM ** -0.5)
    inner = nt(q, k) * intra
    y = nn(inner, v) + nn(q, state) * jnp.exp((ucol + 1.0) * lg)
    new_state = state * jnp.exp(c * lg) + tn(k * jnp.exp((c - 1.0 - ucol) * lg), v)
    return y, new_state


def _ret_chunk_index(n_chunks, n_ctx_chunks):
    def idx(d, s):
        if d == 0:
            return s
        return jnp.where(s < n_ctx_chunks, n_ctx_chunks - 1 - s, n_chunks - 1 - s + n_ctx_chunks)
    return idx


def _ret_fwd(pr, lgv, ctx_rows):
    tk_all = pr.shape[0]
    rw = pr.shape[1] // 3
    nh = rw // HEAD_DIM
    nc = tk_all // RET_CHUNK
    cidx = _ret_chunk_index(nc, ctx_rows // RET_CHUNK)

    def body(pf_ref, pb_ref, lg_ref, yf_ref, yb_ref, st_ref, s_sc):
        s = pl.program_id(0)

        @pl.when(s == 0)
        def _():
            s_sc[...] = jnp.zeros_like(s_sc)

        for d, (p_ref, y_ref) in enumerate(((pf_ref, yf_ref), (pb_ref, yb_ref))):
            for h in range(nh):
                cols = [slice((part * nh + h) * HEAD_DIM, (part * nh + h + 1) * HEAD_DIM) for part in range(3)]
                state = s_sc[d, h]
                st_ref[d, h] = state
                y, new_state = _ret_chunk(p_ref[:, cols[0]], p_ref[:, cols[1]], p_ref[:, cols[2]], state,
                                          lg_ref[d, h][:, :1], d == 1, (_bf_nn, _bf_nt, _bf_tn))
                y_ref[:, h * HEAD_DIM:(h + 1) * HEAD_DIM] = y
                s_sc[d, h] = new_state

    y_shape = jax.ShapeDtypeStruct((tk_all, rw), F32)
    return pl.pallas_call(
        body, name="ret_fwd", grid=(nc,),
        in_specs=[pl.BlockSpec((RET_CHUNK, 3 * rw), lambda s: (cidx(0, s), 0)),
                  pl.BlockSpec((RET_CHUNK, 3 * rw), lambda s: (cidx(1, s), 0)),
                  pl.BlockSpec(lgv.shape, lambda s: (0, 0, 0, 0))],
        out_specs=[pl.BlockSpec((RET_CHUNK, rw), lambda s: (cidx(0, s), 0)),
                   pl.BlockSpec((RET_CHUNK, rw), lambda s: (cidx(1, s), 0)),
                   pl.BlockSpec((2, nh, None, HEAD_DIM, HEAD_DIM), lambda s: (0, 0, s, 0, 0))],
        out_shape=[y_shape, y_shape, jax.ShapeDtypeStruct((2, nh, nc, HEAD_DIM, HEAD_DIM), F32)],
        scratch_shapes=[pltpu.VMEM((2, nh, HEAD_DIM, HEAD_DIM), F32)],
        compiler_params=_params(("arbitrary",)),
    )(pr, pr, lgv)


def _ret_bwd(pr, states, dy, lgv, ctx_rows):
    tk_all = pr.shape[0]
    rw = pr.shape[1] // 3
    nh = rw // HEAD_DIM
    nc = tk_all // RET_CHUNK
    n_ctx = ctx_rows // RET_CHUNK
    cidx = _ret_chunk_index(nc, n_ctx)

    def body(pf_ref, pb_ref, st_ref, dyf_ref, dyb_ref, lg_ref, dpf_ref, dpb_ref, dlg_ref, ds_sc):
        sp = pl.program_id(0)
        on_ctx = [cidx(dd, nc - 1 - sp) < n_ctx for dd in (0, 1)]

        @pl.when(sp == 0)
        def _():
            ds_sc[...] = jnp.zeros_like(ds_sc)
            dlg_ref[...] = jnp.zeros_like(dlg_ref)

        for d, (p_ref, dy_ref, dp_ref) in enumerate(((pf_ref, dyf_ref, dpf_ref), (pb_ref, dyb_ref, dpb_ref))):
            for h in range(nh):
                cols = [slice((part * nh + h) * HEAD_DIM, (part * nh + h + 1) * HEAD_DIM) for part in range(3)]

                def step(q, k, v, state, lg, rev=(d == 1)):
                    return _ret_chunk(q, k, v, state, lg, rev, (_d_nn, _d_nt, _d_tn))

                _, vjp = jax.vjp(step, p_ref[:, cols[0]], p_ref[:, cols[1]], p_ref[:, cols[2]], st_ref[d, h],
                                 lg_ref[d, h][:, :1])
                dy_h = jnp.where(on_ctx[d], 0.0, dy_ref[:, h * HEAD_DIM:(h + 1) * HEAD_DIM])
                grads = vjp((dy_h, ds_sc[d, h]))
                for part in range(3):
                    dp_ref[:, cols[part]] = grads[part]
                ds_sc[d, h] = grads[3]
                dlg_ref[d, h] += jnp.broadcast_to(grads[4], (1, HEAD_DIM))

    def at(d):
        return lambda sp: (cidx(d, nc - 1 - sp), 0)

    def dy_at(d):
        return lambda sp: (jnp.maximum(cidx(d, nc - 1 - sp) - n_ctx, 0), 0)

    dp_shape = jax.ShapeDtypeStruct((tk_all, 3 * rw), F32)
    lg_spec = pl.BlockSpec(lgv.shape, lambda sp: (0, 0, 0, 0))
    return pl.pallas_call(
        body, name="ret_bwd", grid=(nc,),
        in_specs=[pl.BlockSpec((RET_CHUNK, 3 * rw), at(0)), pl.BlockSpec((RET_CHUNK, 3 * rw), at(1)),
                  pl.BlockSpec((2, nh, None, HEAD_DIM, HEAD_DIM), lambda sp: (0, 0, nc - 1 - sp, 0, 0)),
                  pl.BlockSpec((RET_CHUNK, rw), dy_at(0)), pl.BlockSpec((RET_CHUNK, rw), dy_at(1)), lg_spec],
        out_specs=[pl.BlockSpec((RET_CHUNK, 3 * rw), at(0)), pl.BlockSpec((RET_CHUNK, 3 * rw), at(1)), lg_spec],
        out_shape=[dp_shape, dp_shape, jax.ShapeDtypeStruct(lgv.shape, F32)],
        scratch_shapes=[pltpu.VMEM((2, nh, HEAD_DIM, HEAD_DIM), F32)],
        compiler_params=_params(("arbitrary",)),
    )(pr, pr, states, dy, dy, lgv)


FLIP_X, FLIP_Y, FLIP_XY, FLIP_C = (1, 0, 0), (0, 1, 0), (1, 1, 0), (0, 0, 1)
CHIP_FLIPS = ((FLIP_X, 2), (FLIP_Y, 1), (FLIP_XY, 3))


def _flip(me, mask):
    return tuple(1 - v if m else v for v, m in zip(me, mask))


def _comm(name, ins, out_shapes, plan, n_remote, n_local, aliases=None):
    n_in, n_out = len(ins), len(out_shapes)

    def body(*refs):
        in_refs = refs[:n_in]
        out_refs = refs[n_in:n_in + n_out]
        send_sems, recv_sems, local_sems = refs[n_in + n_out:]
        me = (lax.axis_index("x"), lax.axis_index("y"), lax.axis_index("c"))
        local, phases = plan(in_refs, out_refs, me)
        local_copies = [pltpu.make_async_copy(s, d, local_sems.at[i]) for i, (s, d) in enumerate(local)]
        for cp in local_copies:
            cp.start()
        sent = []
        kk = 0
        for phase in phases:
            arrivals = []
            for mask, src, dst, landing in phase:
                peer = _flip(me, mask)
                cp = pltpu.make_async_remote_copy(src_ref=src, dst_ref=dst, send_sem=send_sems.at[kk],
                                                  recv_sem=recv_sems.at[kk], device_id=peer,
                                                  device_id_type=pl.DeviceIdType.MESH)
                cp.start()
                sent.append(cp)
                arrivals.append(pltpu.make_async_remote_copy(
                    src_ref=landing, dst_ref=landing, send_sem=send_sems.at[kk], recv_sem=recv_sems.at[kk],
                    device_id=peer, device_id_type=pl.DeviceIdType.MESH))
                kk += 1
            for cp in arrivals:
                cp.wait_recv()
        for cp in sent:
            cp.wait_send()
        for cp in local_copies:
            cp.wait()

    any_spec = pl.BlockSpec(memory_space=pl.ANY)
    return pl.pallas_call(
        body, name=name,
        in_specs=[any_spec] * n_in, out_specs=[any_spec] * n_out, out_shape=list(out_shapes),
        scratch_shapes=[pltpu.SemaphoreType.DMA((n_remote,)), pltpu.SemaphoreType.DMA((n_remote,)),
                        pltpu.SemaphoreType.DMA((max(n_local, 1),))],
        input_output_aliases=aliases or {},
    )(*ins)


def _ds(start, size):
    return pl.ds(pl.multiple_of(start * size, 8), size)


def _all_gather8(name, v):
    masks = [(a, b, cc) for a in (0, 1) for b in (0, 1) for cc in (0, 1)][1:]

    def index(p):
        return 4 * p[0] + 2 * p[1] + p[2]

    def plan(in_refs, out_refs, me):
        (src,), (out,) = in_refs, out_refs
        local = [(src, out.at[index(me)])]
        phase = [(m, src, out.at[index(me)], out.at[index(_flip(me, m))]) for m in masks]
        return local, [phase]

    return _comm(name, [v], [jax.ShapeDtypeStruct((N_DEV,) + v.shape, v.dtype)], plan, len(masks), 1)[0]


def _gather_row(name, row):
    n = row.shape[1]
    n_pad = -(-n // (8 * LANES_V7X)) * (8 * LANES_V7X)
    v = jnp.pad(row, ((0, 0), (0, n_pad - n))).reshape(8, n_pad // 8)
    return _all_gather8(name, v).reshape(N_DEV, n_pad)[:, :n]


class _Sharded:
    def __init__(self, kind, rows, cols):
        self.kind, self.rows, self.cols = kind, rows, cols
        self.shard_shape = (rows, cols // N_CHIP) if kind == "col" else (rows // N_CHIP, cols)
        self.piece_shape = (rows // 2, cols // N_CHIP) if kind == "col" else (rows // N_CHIP, cols // 2)

    def piece_of_full(self, ref, s, h):
        if self.kind == "col":
            return ref.at[_ds(h, self.rows // 2), _ds(s, self.cols // N_CHIP)]
        return ref.at[_ds(s, self.rows // N_CHIP), _ds(h, self.cols // 2)]

    def half_of_shard(self, ref, h):
        if self.kind == "col":
            return ref.at[_ds(h, self.rows // 2), :]
        return ref.at[:, _ds(h, self.cols // 2)]


def _place_shard(meta, w, s_arr, after=None):
    r, cols = w.shape
    tr = _tile(r, 256, 16)
    nr = r // tr

    def body(s_ref, w_ref, *rest):
        rest[-1][...] = w_ref[...].astype(BF16)

    if meta.kind == "col":
        o_map = lambda i, s_ref: (i, s_ref[0])
    else:
        o_map = lambda i, s_ref: (i + s_ref[0] * nr, 0)
    return pl.pallas_call(
        body, name="place_shard",
        grid_spec=pltpu.PrefetchScalarGridSpec(
            num_scalar_prefetch=1, grid=(nr,),
            in_specs=[pl.BlockSpec((tr, cols), lambda i, s_ref: (i, 0))]
            + ([] if after is None else [pl.BlockSpec(memory_space=pl.ANY)]),
            out_specs=pl.BlockSpec((tr, cols), o_map)),
        out_shape=jax.ShapeDtypeStruct((meta.rows, meta.cols), BF16),
        compiler_params=_params(("parallel",)),
    )(*((s_arr, w) if after is None else (s_arr, w, after)))


def _gather_copies(metas, over_ici):
    def copies(fulls, me):
        x, y, c = me
        s_me = 2 * x + y
        out = []
        for meta, full in zip(metas, fulls):
            for mask, bits in CHIP_FLIPS:
                s_peer = jnp.bitwise_xor(s_me, bits)
                if over_ici:
                    out.append((mask, meta.piece_of_full(full, s_me, c), meta.piece_of_full(full, s_me, c),
                                meta.piece_of_full(full, s_peer, c)))
                else:
                    out.append((FLIP_C, meta.piece_of_full(full, s_peer, c), meta.piece_of_full(full, s_peer, c),
                                meta.piece_of_full(full, s_peer, 1 - c)))
        return out
    return copies


def _gather_forward(name, metas, fulls):
    nt = len(metas)
    copies = _gather_copies(metas, False)
    outs = [jax.ShapeDtypeStruct((m.rows, m.cols), BF16) for m in metas]
    return _comm(name, list(fulls), outs, lambda ins, outs_, me: ([], [copies(outs_, me)]), 3 * nt, 0,
                 aliases={i: i for i in range(nt)})


HBM_SPEC = pl.BlockSpec(memory_space=pltpu.HBM)
SEM_SPEC = pl.BlockSpec(memory_space=pltpu.SEMAPHORE)
SPLIT_EFFECT = pltpu.SideEffectType.DATAFLOW_SIDE_EFFECTING


def _split_start(name, bufs, groups, after):
    nb, ng = len(bufs), len(groups)
    n_in = nb + (0 if after is None else 1)

    def body(*refs):
        buf_refs = refs[:nb]
        sem_refs = refs[n_in:n_in + 2 * ng]
        token = refs[-1]
        me = (lax.axis_index("x"), lax.axis_index("y"), lax.axis_index("c"))
        for gi, (lo, n_bufs, copies, _) in enumerate(groups):
            for kk, (mask, src, dst, _) in enumerate(copies(buf_refs[lo:lo + n_bufs], me)):
                pltpu.make_async_remote_copy(src_ref=src, dst_ref=dst, send_sem=sem_refs[2 * gi].at[kk],
                                             recv_sem=sem_refs[2 * gi + 1].at[kk], device_id=_flip(me, mask),
                                             device_id_type=pl.DeviceIdType.MESH).start()
        token[...] = jnp.zeros_like(token)

    out_shape = []
    for _, _, _, n in groups:
        out_shape += [pltpu.SemaphoreType.DMA((n,)), pltpu.SemaphoreType.DMA((n,))]
    out_shape += [pltpu.HBM(b.shape, b.dtype) for b in bufs] + [jax.ShapeDtypeStruct((8, LANES_V7X), F32)]
    res = pl.pallas_call(
        body, name=name, out_shape=tuple(out_shape),
        in_specs=(HBM_SPEC,) * nb + (pl.BlockSpec(memory_space=pl.ANY),) * (n_in - nb),
        out_specs=(SEM_SPEC,) * (2 * ng) + (HBM_SPEC,) * nb + (pl.BlockSpec(memory_space=pltpu.VMEM),),
        input_output_aliases={i: 2 * ng + i for i in range(nb)},
        compiler_params=pltpu.CompilerParams(has_side_effects=SPLIT_EFFECT),
    )(*[pltpu.with_memory_space_constraint(b, pltpu.HBM) for b in bufs], *([] if after is None else [after]))
    sems = [(res[2 * gi], res[2 * gi + 1]) for gi in range(ng)]
    return sems, list(res[2 * ng:2 * ng + nb]), res[-1]


def _split_wait(name, sems, bufs, copies, after):
    nb = len(bufs)

    def body(*refs):
        buf_refs = refs[:nb]
        send_sems, recv_sems = refs[nb], refs[nb + 1]
        me = (lax.axis_index("x"), lax.axis_index("y"), lax.axis_index("c"))
        for kk, (mask, _, _, landing) in enumerate(copies(buf_refs, me)):
            cp = pltpu.make_async_remote_copy(src_ref=landing, dst_ref=landing, send_sem=send_sems.at[kk],
                                              recv_sem=recv_sems.at[kk], device_id=_flip(me, mask),
                                              device_id_type=pl.DeviceIdType.MESH)
            cp.wait_send()
            cp.wait_recv()

    return list(pl.pallas_call(
        body, name=name, out_shape=tuple(pltpu.HBM(b.shape, b.dtype) for b in bufs),
        in_specs=(HBM_SPEC,) * nb + (SEM_SPEC, SEM_SPEC, pl.BlockSpec(memory_space=pl.ANY)),
        out_specs=(HBM_SPEC,) * nb,
        input_output_aliases={i: i for i in range(nb)},
        compiler_params=pltpu.CompilerParams(has_side_effects=SPLIT_EFFECT),
    )(*bufs, sems[0], sems[1], after))


N_REDUCE_PIECES = 7


def _reduce_copies(metas):
    def copies(refs, me):
        x, y, c = me
        s_me = 2 * x + y
        out = []
        for m, g, land in zip(metas, refs[:len(metas)], refs[len(metas):]):
            for kk, (mask, bits) in enumerate(CHIP_FLIPS):
                s_peer = jnp.bitwise_xor(s_me, bits)
                out.append((mask, m.piece_of_full(g, s_peer, c), land.at[kk], land.at[kk]))
                out.append((mask[:2] + (1,), m.piece_of_full(g, s_peer, 1 - c), land.at[3 + kk], land.at[3 + kk]))
            out.append((FLIP_C, m.piece_of_full(g, s_me, 1 - c), land.at[6], land.at[6]))
        return out
    return copies


def _reduce_start(name, metas, grads):
    lands = [lax.empty((N_REDUCE_PIECES,) + m.piece_shape, BF16) for m in metas]
    bufs = list(grads) + lands
    sems, thru, token = _split_start(name, bufs, [(0, len(bufs), _reduce_copies(metas),
                                                   N_REDUCE_PIECES * len(metas))], None)
    return sems[0], thru, token


def _reduce_wait(name, metas, sems, thru, after):
    done = _split_wait(name, sems, thru, _reduce_copies(metas), after)
    return done[:len(metas)], done[len(metas):]


def _share_halves(metas, shards):
    def plan(in_refs, out_refs, me):
        c = me[2]
        phase = [(FLIP_C, m.half_of_shard(g, c), m.half_of_shard(g, c), m.half_of_shard(g, 1 - c))
                 for m, g in zip(metas, out_refs)]
        return [], [phase]

    outs = [jax.ShapeDtypeStruct(m.shard_shape, F32) for m in metas]
    return _comm("share_halves", list(shards), outs, plan, len(metas), 0,
                 aliases={i: i for i in range(len(metas))})


def _sum_pieces(meta, grad, landed, s_arr, c_arr):
    pr, pc = meta.piece_shape
    tr = _tile(pr, 256, 16)
    tc = _tile(pc, 2048)
    nr, ncol = pr // tr, pc // tc

    def body(s_ref, c_ref, p_ref, l_ref, o_ref):
        acc = p_ref[...].astype(F32)
        for kk in range(N_REDUCE_PIECES):
            acc = acc + l_ref[kk].astype(F32)
        o_ref[...] = acc

    if meta.kind == "col":
        p_map = lambda i, j, s_ref, c_ref: (i + c_ref[0] * nr, j + s_ref[0] * ncol)
        o_map = lambda i, j, s_ref, c_ref: (i + c_ref[0] * nr, j)
    else:
        p_map = lambda i, j, s_ref, c_ref: (i + s_ref[0] * nr, j + c_ref[0] * ncol)
        o_map = lambda i, j, s_ref, c_ref: (i, j + c_ref[0] * ncol)
    blk = (tr, tc)
    return pl.pallas_call(
        body, name="sum_pieces",
        grid_spec=pltpu.PrefetchScalarGridSpec(
            num_scalar_prefetch=2, grid=(nr, ncol),
            in_specs=[pl.BlockSpec(blk, p_map),
                      pl.BlockSpec((N_REDUCE_PIECES,) + blk, lambda i, j, s_ref, c_ref: (0, i, j))],
            out_specs=pl.BlockSpec(blk, o_map)),
        out_shape=jax.ShapeDtypeStruct(meta.shard_shape, F32),
        compiler_params=_params(("parallel", "parallel")),
    )(s_arr, c_arr, grad, landed)


def _adam_rows(rows, sel, fulls):
    w, g, m, v = rows
    m2 = ADAM_B1 * m + (1.0 - ADAM_B1) * g
    v2 = ADAM_B2 * v + (1.0 - ADAM_B2) * jnp.square(g)
    m_hat = m2 / (1.0 - ADAM_B1 ** ADAM_STEP)
    v_hat = v2 / (1.0 - ADAM_B2 ** ADAM_STEP)
    delta = -ADAM_LR * (m_hat / (jnp.sqrt(v_hat) + ADAM_EPS) + ADAM_WD * w)
    return [delta, m2, v2], []


def _adamw(w, g, m, v):
    r, c = w.shape
    tr = _tile(r, 128, 8)
    outs = _rowwise("adamw", _adam_rows, n_tiles=r // tr, tr=tr,
                    row_ins=[(w, 0, None), (g, 0, None), (m, 0, None), (v, 0, None)],
                    row_outs=[(r, c, F32, 0)] * 3)
    return outs[0], outs[1], outs[2]


def _ada_fwd(cg, w, b):
    d, n = w.shape
    tn = _tile(n, 512)

    def body(c_ref, w_ref, b_ref, o_ref):
        a = _silu(c_ref[...]).astype(BF16)
        o_ref[...] = jnp.dot(a, w_ref[...].astype(BF16), preferred_element_type=F32) + b_ref[...]

    return pl.pallas_call(
        body, name="ada_fwd", grid=(n // tn,),
        in_specs=[pl.BlockSpec(cg.shape, lambda j: (0, 0)), pl.BlockSpec((d, tn), lambda j: (0, j)),
                  pl.BlockSpec((1, tn), lambda j: (0, j))],
        out_specs=pl.BlockSpec((cg.shape[0], tn), lambda j: (0, j)),
        out_shape=jax.ShapeDtypeStruct((cg.shape[0], n), F32),
        compiler_params=_params(("parallel",)),
    )(cg, w, b)


def _ada_bwd(cg, dm, w):
    d, n = w.shape
    tn = _tile(n, 512)
    nj = n // tn

    def body(c_ref, dm_ref, w_ref, gw_ref, da_ref, acc):
        j = pl.program_id(0)

        @pl.when(j == 0)
        def _():
            acc[...] = jnp.zeros_like(acc)

        a = _silu(c_ref[...]).astype(BF16)
        dmv = dm_ref[...].astype(BF16)
        gw_ref[...] = lax.dot_general(a, dmv, TN_DIMS, preferred_element_type=F32)
        acc[...] += lax.dot_general(dmv, w_ref[...].astype(BF16), NT_DIMS, preferred_element_type=F32)

        @pl.when(j == nj - 1)
        def _():
            da_ref[...] = acc[...]

    return pl.pallas_call(
        body, name="ada_bwd", grid=(nj,),
        in_specs=[pl.BlockSpec(cg.shape, lambda j: (0, 0)), pl.BlockSpec((dm.shape[0], tn), lambda j: (0, j)),
                  pl.BlockSpec((d, tn), lambda j: (0, j))],
        out_specs=[pl.BlockSpec((d, tn), lambda j: (0, j)), pl.BlockSpec(cg.shape, lambda j: (0, 0))],
        out_shape=[jax.ShapeDtypeStruct((d, n), F32), jax.ShapeDtypeStruct(cg.shape, F32)],
        scratch_shapes=[pltpu.VMEM(cg.shape, F32)],
        compiler_params=_params(("arbitrary",)),
    )(cg, dm, w)


def _small_reduce(gathered, logits, n_mod_cols, lg_off, loss_off, loss_cols):
    npk = gathered.shape[1]

    def body(g_ref, lo_ref, tot_ref, gb_ref, gl_ref, loss_ref):
        acc = g_ref[0:1, :]
        for dd in range(1, N_DEV):
            acc = acc + g_ref[dd:dd + 1, :]
        tot_ref[...] = acc
        gb_ref[...] = acc[:, :n_mod_cols] + acc[:, n_mod_cols:2 * n_mod_cols]
        gl_ref[...] = acc[:, lg_off:lg_off + LANES_V7X] * _sigmoid(-lo_ref[...])
        loss = jnp.sum(acc[:, loss_off:loss_off + loss_cols], axis=1, keepdims=True)
        loss_ref[...] = jnp.broadcast_to(loss, loss_ref.shape)

    lane = jax.ShapeDtypeStruct((1, LANES_V7X), F32)
    return pl.pallas_call(
        body, name="small_reduce",
        out_shape=[jax.ShapeDtypeStruct((1, npk), F32), jax.ShapeDtypeStruct((1, n_mod_cols), F32), lane, lane],
    )(gathered, logits)


def _c_ctx_grad(parts, c_ctx):
    def body(p_ref, c_ref, o_ref):
        tot = p_ref[0:1, :] + p_ref[2:3, :] + p_ref[4:5, :] + p_ref[6:7, :]
        _, vjp = jax.vjp(_silu, c_ref[...])
        o_ref[...] = vjp(tot)[0]

    return pl.pallas_call(body, name="c_ctx_grad", out_shape=jax.ShapeDtypeStruct(c_ctx.shape, F32))(parts, c_ctx)


def _rope_tables(seq, ctx_rows):
    rows = seq // GRID_W
    half = HEAD_DIM // 2
    inv_freq = ROPE_THETA ** (-jnp.arange(0, half, 2, dtype=F32) / half)
    ang_row = jnp.arange(rows, dtype=F32)[:, None] * inv_freq
    ang_col = jnp.arange(GRID_W, dtype=F32)[:, None] * inv_freq

    def spread(fn):
        return jnp.concatenate([jnp.repeat(fn(ang_row), GRID_W, axis=0), jnp.tile(fn(ang_col), (rows, 1))], axis=-1)

    cos, sin = spread(jnp.cos), spread(jnp.sin)
    cos_full = jnp.repeat(cos, 2, axis=1)
    sin_signed = jnp.stack([-sin, sin], axis=-1).reshape(seq, HEAD_DIM)
    cos_full = jnp.concatenate([jnp.ones((ctx_rows, HEAD_DIM), F32), cos_full], axis=0)
    sin_signed = jnp.concatenate([jnp.zeros((ctx_rows, HEAD_DIM), F32), sin_signed], axis=0)
    return cos_full, sin_signed


def _qk_rot(p, gain, cos_full, sin_signed):
    r = _rmsn(p) * gain
    return r * cos_full + _swap_pairs(r) * sin_signed


def _qk_rot_bwd(g, p, gain, cos_full, sin_signed):
    g1 = g * cos_full + _swap_pairs(g * sin_signed)
    _, vjp = jax.vjp(lambda pp, gn: _rmsn(pp) * gn, p, gain)
    return vjp(g1)


def kernel(x, c, ctx, c_ctx, w_ada, b_ada, ffn1_w_in, ffn1_w_out, mix_w_in, attn_q_gain, attn_k_gain, ret_decay_logit, w_proj_attn, w_proj_ret, mix_w_out, ffn2_w_in, ffn2_w_out, final_norm, loss_target, m_c_ctx, m_w_ada, m_b_ada, m_ffn1_w_in, m_ffn1_w_out, m_mix_w_in, m_attn_q_gain, m_attn_k_gain, m_ret_decay_logit, m_w_proj_attn, m_w_proj_ret, m_mix_w_out, m_ffn2_w_in, m_ffn2_w_out, m_final_norm, v_c_ctx, v_w_ada, v_b_ada, v_ffn1_w_in, v_ffn1_w_out, v_mix_w_in, v_attn_q_gain, v_attn_k_gain, v_ret_decay_logit, v_w_proj_attn, v_w_proj_ret, v_mix_w_out, v_ffn2_w_in, v_ffn2_w_out, v_final_norm):
    xi, yi, ci = lax.axis_index("x"), lax.axis_index("y"), lax.axis_index("c")
    dev = 4 * xi + 2 * yi + ci
    s_me = 2 * xi + yi
    c_arr = jnp.reshape(ci, (1,)).astype(jnp.int32)
    s_arr = jnp.reshape(s_me, (1,)).astype(jnp.int32)

    t, d = x.shape[1], x.shape[2]
    tc = ctx.shape[1]
    tk = tc + t
    aw = w_proj_attn.shape[1]
    rw = w_proj_ret.shape[1]
    pw = mix_w_in.shape[2] * N_CHIP
    kvw = (pw - aw - 4 * rw - 2 * d) // 2
    groups = aw // kvw
    n_ret_heads = rw // HEAD_DIM
    mod_cols = N_MOD * d
    tr = _tile(tc, 256, 32)
    nt_all, nt_x, ctx_tiles = tk // tr, t // tr, tc // tr

    c_rows = _gather_row("gather_c", c)
    cg = jnp.concatenate([c_rows, c_ctx[None, :], jnp.zeros((7, d), F32)], axis=0)
    w_ada_l = w_ada[0]
    ada_cols = w_ada_l.shape[1]
    b_ada_l = lax.dynamic_slice_in_dim(b_ada, s_me * ada_cols, ada_cols, axis=1)
    mod_shard = _ada_fwd(cg, w_ada_l, b_ada_l)
    mod_g = _all_gather8("gather_mod", mod_shard)
    mod_full = jnp.concatenate([mod_g[0], mod_g[2], mod_g[4], mod_g[6]], axis=1)
    mod_x = lax.dynamic_slice_in_dim(mod_full, dev, 1, axis=0).reshape(N_MOD, d)
    mod_c = mod_full[8].reshape(N_MOD, d)
    mods = jnp.stack([mod_c, mod_x])

    big = [("col", ffn1_w_in), ("row", ffn1_w_out), ("col", mix_w_in), ("col", w_proj_attn), ("col", w_proj_ret),
           ("row", mix_w_out), ("col", ffn2_w_in), ("row", ffn2_w_out)]
    metas = []
    for kind, w in big:
        r_l, c_l = w.shape[1:]
        metas.append(_Sharded(kind, r_l, c_l * N_CHIP) if kind == "col" else _Sharded(kind, r_l * N_CHIP, c_l))
    layer_groups = ((0, 1), (1, 2), (2, 6), (6, 8))
    sems_first, placed_first, token = _split_start(
        "gather_start_first", [_place_shard(metas[0], big[0][1][0], s_arr)],
        [(0, 1, _gather_copies(metas[0:1], True), 3)], mods)
    placed_rest = [_place_shard(m, w[0], s_arr, after=token) for m, (_, w) in zip(metas[1:], big[1:])]
    sems_rest, placed_rest, token = _split_start(
        "gather_start_rest", placed_rest,
        [(lo - 1, hi - lo, _gather_copies(metas[lo:hi], True), 3 * (hi - lo)) for lo, hi in layer_groups[1:]], None)
    gather_sems, placed = sems_first + sems_rest, placed_first + placed_rest
    mods = mods + token[0, 0]

    def weights_of(gi, after):
        lo, hi = layer_groups[gi]
        arrived = _split_wait("gather_wait_%d" % gi, gather_sems[gi], placed[lo:hi],
                              _gather_copies(metas[lo:hi], True), after)
        return _gather_forward("gather_forward_%d" % gi, metas[lo:hi], arrived)

    def weights_early(gi, after):
        lo, hi = layer_groups[gi]
        arrived = _split_wait("gather_wait_%d" % gi, gather_sems[gi], placed[lo:hi],
                              _gather_copies(metas[lo:hi], True), after)
        sems, thru, tok = _split_start("gather_forward_start_%d" % gi, arrived,
                                       [(0, hi - lo, _gather_copies(metas[lo:hi], False), 3 * (hi - lo))], None)
        return gi, sems[0], thru, tok

    def weights_late(early, after):
        gi, sems, thru, _ = early
        lo, hi = layer_groups[gi]
        return _split_wait("gather_forward_wait_%d" % gi, sems, thru, _gather_copies(metas[lo:hi], False), after)

    cos_full, sin_signed = _rope_tables(t, tc)
    q_gain, k_gain = attn_q_gain, attn_k_gain
    log_gamma = jax.nn.log_sigmoid(ret_decay_logit[0])
    lgv = jnp.broadcast_to(log_gamma[:, :, None, None], (2, n_ret_heads, 1, HEAD_DIM))

    def stream_rows(h, off):
        if isinstance(h, tuple):
            return [(h[0], "ctx", None), (h[1], -ctx_tiles, None)]
        return [(h, off, None)]

    def stream_value(h, rows, fulls):
        if isinstance(h, tuple):
            return jnp.where(fulls[-1], rows[0], rows[1]), rows[2:]
        return rows[0], rows[1:]

    def norm_mod(name, h, n_tiles, off, i_shift, i_scale):
        def fn(rows, sel, fulls):
            hv, _ = stream_value(h, rows, fulls)
            return [_rmsn(hv) * (1.0 + sel(i_scale)) + sel(i_shift)], []
        return _rowwise(name, fn, n_tiles=n_tiles, tr=tr, row_ins=stream_rows(h, 0),
                        row_outs=[(n_tiles * tr, d, BF16, 0)], sel_in=mods, sel_off=off, ctx_rows=tc)[0]

    def resid_norm(name, h, h_off, f, n_tiles, off, i_gate, coef, i_shift, i_scale):
        def fn(rows, sel, fulls):
            hv, rest = stream_value(h, rows, fulls)
            hn = hv + coef * sel(i_gate) * rest[0]
            return [hn, _rmsn(hn) * (1.0 + sel(i_scale)) + sel(i_shift)], []
        return _rowwise(name, fn, n_tiles=n_tiles, tr=tr, row_ins=stream_rows(h, h_off) + [(f, 0, None)],
                        row_outs=[(f.shape[0], d, F32, 0), (f.shape[0], d, BF16, 0)], sel_in=mods, sel_off=off,
                        ctx_rows=tc)

    h0 = (ctx[0], x[0])
    n1 = norm_mod("norm_mod1", h0, nt_all, 0, 0, 1)
    w1i, = weights_of(0, n1)
    hm1, ua1, ub1 = _mm_swiglu("ffn1_in", n1, w1i)
    w1o, = weights_of(1, hm1)
    f1 = _mm("ffn1_out", hm1, w1o, "nn", BF16)
    mixer_weights = weights_early(2, f1)
    mods = mods + mixer_weights[3][0, 0]
    h1, n2 = resid_norm("resid_norm1", h0, 0, f1, nt_all, 0, 2, 0.5, 3, 4)
    wmi, wpa, wpr, wmo = weights_late(mixer_weights, n2)
    p_q = _mm("mix_in_q", n2, wmi, "nn", F32, 0, aw)
    p_kv = _mm("mix_in_kv", n2, wmi, "nn", F32, aw, 2 * kvw)
    p_r = _mm("mix_in_ret", n2, wmi, "nn", F32, aw + 2 * kvw, 3 * rw)
    p_gr = _mm("mix_in_gr", n2, wmi, "nn", BF16, aw + 2 * kvw + 3 * rw, rw)
    p_gab = _mm("mix_in_gab", n2, wmi, "nn", BF16, aw + 2 * kvw + 4 * rw, 2 * d)
    ffn2_weights = weights_early(3, p_gab)
    q_gain = q_gain + ffn2_weights[3][0, 0]

    def q_prep(rows, sel, fulls):
        p, cf, ss = rows
        return _heads_map(lambda ph: [_qk_rot(ph, fulls[0], cf, ss) * QSCALE], [p], aw), []

    q_rot = _rowwise("q_prep", q_prep, n_tiles=nt_x, tr=tr,
                     row_ins=[(p_q, ctx_tiles, None), (cos_full, ctx_tiles, None), (sin_signed, ctx_tiles, None)],
                     row_outs=[(t, aw, BF16, 0)], full_ins=[q_gain])[0]

    def kv_prep(rows, sel, fulls):
        p, cf, ss = rows
        k_rot = _heads_map(lambda ph: [_qk_rot(ph, fulls[0], cf, ss)], [p[:, :kvw]], kvw)[0]
        v_ones = _heads_map(lambda vh: [jnp.concatenate([vh, jnp.ones_like(vh)], axis=1)], [p[:, kvw:]], kvw)[0]
        return [k_rot, v_ones], []

    k_rot, v_att = _rowwise("kv_prep", kv_prep, n_tiles=nt_all, tr=tr,
                            row_ins=[(p_kv, 0, None), (cos_full, 0, None), (sin_signed, 0, None)],
                            row_outs=[(tk, kvw, BF16, 0), (tk, 2 * kvw, BF16, 0)], full_ins=[k_gain])

    ya, lse = _flash_fwd(q_rot, k_rot, v_att, groups)
    y_fwd, y_bwd, states = _ret_fwd(p_r, lgv, tc)

    def ret_out_fn(yf, yb, gr):
        return [_silu(gr) * _rmsn(yf + yb)]

    def ret_out(rows, sel, fulls):
        return _heads_map(ret_out_fn, rows, rw), []

    y_rows = [(y_fwd, ctx_tiles, None), (y_bwd, ctx_tiles, None), (p_gr, ctx_tiles, None)]
    yr = _rowwise("ret_out", ret_out, n_tiles=nt_x, tr=tr, row_ins=y_rows, row_outs=[(t, rw, BF16, 0)])[0]

    pa = _mm("proj_attn", ya, wpa, "nn", BF16)
    prj = _mm("proj_ret", yr, wpr, "nn", BF16)

    def merge_fn(a, r, ga, gb):
        return _sigmoid(ga) * a + _sigmoid(gb) * r

    gate_rows = [(p_gab, ctx_tiles, (d, 0)), (p_gab, ctx_tiles, (d, 1))]
    z = _rowwise("merge", lambda rows, sel, fulls: ([merge_fn(*rows)], []), n_tiles=nt_x, tr=tr,
                 row_ins=[(pa, 0, None), (prj, 0, None)] + gate_rows, row_outs=[(t, d, BF16, 0)])[0]
    fo = _mm("mix_out", z, wmo, "nn", BF16)
    h2, n3 = resid_norm("resid_norm2", h1, ctx_tiles, fo, nt_x, ctx_tiles, 5, 1.0, 6, 7)
    w2i, w2o = weights_late(ffn2_weights, n3)
    hm2, ua2, ub2 = _mm_swiglu("ffn2_in", n3, w2i)
    f2 = _mm("ffn2_out", hm2, w2o, "nn", BF16)

    def loss_fn(rows, sel, fulls):
        h2v, f2v, tgt = rows
        g3 = 0.5 * sel(8)
        y, vjp = jax.vjp(lambda hh, ww: _rmsn(hh) * ww, h2v + g3 * f2v, fulls[0])
        err = y - tgt
        dh, dw = vjp(err / d)
        return [dh, g3 * dh], [0.5 / d * jnp.sum(err * err, axis=0, keepdims=True), dw,
                               jnp.sum(0.5 * dh * f2v, axis=0, keepdims=True)]

    dh3, df2, loss_acc = _rowwise("loss_head", loss_fn, n_tiles=nt_x, tr=tr,
                                  row_ins=[(h2, 0, None), (f2, 0, None), (loss_target[0], 0, None)],
                                  row_outs=[(t, d, F32, 0), (t, d, BF16, 0)], sel_in=mods, sel_off=ctx_tiles,
                                  ctx_rows=tc, full_ins=[final_norm[None, :]], acc_shape=(8, d))
    loss_cols, g_final, dg3 = loss_acc[1, 0:1], loss_acc[1, 1:2], loss_acc[1, 2:3]

    def norm_mod_bwd(name, dn, h, dres, dres_off, n_tiles, off, i_shift, i_scale, gate=None, out_off=0):
        def fn(rows, sel, fulls):
            hh, rows = stream_value(h, rows, fulls)
            g, dr = rows[:2]
            if dres_off < 0:
                dr = jnp.where(fulls[-1], 0.0, dr)
            _, vjp = jax.vjp(lambda a, sh, sc: _rmsn(a) * (1.0 + sc) + sh, hh,
                             sel(i_shift), sel(i_scale))
            dhh, dsh, dsc = vjp(g)
            dh = dr + dhh
            if gate is None:
                return [dh], [dsh, dsc]
            return [dh, gate[2] * sel(gate[1]) * dh], [dsh, dsc, jnp.sum(gate[2] * dh * rows[2], axis=0, keepdims=True)]
        n_rows = dn.shape[0] + out_off * tr
        row_ins = stream_rows(h, 0) + [(dn, 0, None), (dres, dres_off, None)]
        row_outs = [(n_rows, d, F32, out_off)]
        if gate is not None:
            row_ins.append((gate[0], 0, None))
            row_outs.append((n_rows, d, BF16, out_off))
        return _rowwise(name, fn, n_tiles=n_tiles, tr=tr, row_ins=row_ins, row_outs=row_outs, sel_in=mods,
                        sel_off=off, ctx_rows=tc, acc_shape=(8, d))

    g_w2o = _mm("ffn2_out_dw", hm2, df2, "tn", BF16)
    du2 = _mm_swiglu_bwd("ffn2_out_dx", df2, w2o, ua2, ub2)
    g_w2i = _mm("ffn2_in_dw", n3, du2, "tn", BF16, stacked=True)
    dn3 = _mm("ffn2_in_dx", du2, w2i, "nt", F32, stacked=True)
    dh2, dfo, acc_n3 = norm_mod_bwd("norm_mod_bwd3", dn3, h2, dh3, 0, nt_x, ctx_tiles, 6, 7, gate=(fo, 5, 1.0))

    sems_ffn2, thru_ffn2, token = _reduce_start("reduce_start_ffn2", metas[6:8], [g_w2i, g_w2o])

    g_wmo = _mm("mix_out_dw", z, dfo, "tn", BF16, after=token)
    dz = _mm("mix_out_dx", dfo, wmo, "nt", F32)

    def merge_bwd(rows, sel, fulls):
        g, a, r, ga, gb = rows
        _, vjp = jax.vjp(merge_fn, a, r, ga, gb)
        da, dr, dga, dgb = vjp(g)
        return [da, dr, jnp.concatenate([dga, dgb], axis=1)], []

    dpa, dpr, dgab = _rowwise("merge_bwd", merge_bwd, n_tiles=nt_x, tr=tr,
                              row_ins=[(dz, 0, None), (pa, 0, None), (prj, 0, None)] + gate_rows,
                              row_outs=[(t, d, BF16, 0), (t, d, BF16, 0), (t, 2 * d, BF16, 0)])
    g_wpa = _mm("proj_attn_dw", ya, dpa, "tn", BF16)
    dya = _mm("proj_attn_dx", dpa, wpa, "nt", BF16)
    g_wpr = _mm("proj_ret_dw", yr, dpr, "tn", BF16)
    dyr = _mm("proj_ret_dx", dpr, wpr, "nt", F32)

    def ret_out_bwd(rows, sel, fulls):
        def per_head(g, yf, yb, gr):
            _, vjp = jax.vjp(lambda yy, gg: ret_out_fn(yy, 0.0, gg)[0], yf + yb, gr)
            return list(vjp(g))
        dy, dgr = _heads_map(per_head, rows, rw)
        return [dy, dgr], []

    dy_ret, dgr = _rowwise("ret_out_bwd", ret_out_bwd, n_tiles=nt_x, tr=tr, row_ins=[(dyr, 0, None)] + y_rows,
                           row_outs=[(t, rw, F32, 0), (t, rw, BF16, 0)])
    dp_rf, dp_rb, dlg = _ret_bwd(p_r, states, dy_ret, lgv, tc)
    dp_r = _rowwise("ret_bwd_sum", lambda rows, sel, fulls: ([rows[0] + rows[1]], []), n_tiles=nt_all, tr=tr,
                    row_ins=[(dp_rf, 0, None), (dp_rb, 0, None)], row_outs=[(tk, 3 * rw, BF16, 0)])[0]

    dq_rot, dk_rot, dv_att = _flash_bwd(q_rot, k_rot, v_att, ya, dya, lse, groups)

    def q_prep_bwd(rows, sel, fulls):
        g, p, cf, ss = rows
        gain_acc = []

        def per_head(gh, ph):
            dp, dgain = _qk_rot_bwd(gh * HEAD_DIM ** -0.5, ph, fulls[0], cf, ss)
            gain_acc.append(dgain)
            return [dp]
        dp = _heads_map(per_head, [g, p], aw)[0]
        return [dp], [functools.reduce(lambda a, b: a + b, gain_acc)]

    dp_q, acc_gq = _rowwise("q_prep_bwd", q_prep_bwd, n_tiles=nt_x, tr=tr,
                            row_ins=[(dq_rot, 0, None), (p_q, ctx_tiles, None), (cos_full, ctx_tiles, None),
                                     (sin_signed, ctx_tiles, None)],
                            row_outs=[(t, aw, BF16, 0)], full_ins=[q_gain], acc_shape=(8, HEAD_DIM),
                            sel_off=ctx_tiles, ctx_rows=tc)

    def kv_prep_bwd(rows, sel, fulls):
        gk, gv, p, cf, ss = rows
        gain_acc = []

        def per_head(gh, ph):
            dp, dgain = _qk_rot_bwd(gh, ph, fulls[0], cf, ss)
            gain_acc.append(dgain)
            return [dp]
        dpk = _heads_map(per_head, [gk, p], kvw)[0]
        return [jnp.concatenate([dpk, gv], axis=1)], [functools.reduce(lambda a, b: a + b, gain_acc)]

    dp_kv, acc_gk = _rowwise("kv_prep_bwd", kv_prep_bwd, n_tiles=nt_all, tr=tr,
                             row_ins=[(dk_rot, 0, None), (dv_att, 0, None), (p_kv, 0, (kvw, 0)), (cos_full, 0, None),
                                      (sin_signed, 0, None)],
                             row_outs=[(tk, 2 * kvw, BF16, 0)], full_ins=[k_gain], acc_shape=(8, HEAD_DIM),
                             sel_off=0, ctx_rows=tc)

    def with_ctx_zeros(a):
        return jnp.concatenate([jnp.zeros((tc, a.shape[1]), a.dtype), a], axis=0)

    dp = jnp.concatenate([with_ctx_zeros(dp_q), dp_kv, dp_r, with_ctx_zeros(dgr), with_ctx_zeros(dgab)], axis=1)
    g_wmi = _mm("mix_in_dw", n2, dp, "tn", BF16)
    dn2 = _mm("mix_in_dx", dp, wmi, "nt", F32)
    dh1, df1, acc_n2 = norm_mod_bwd("norm_mod_bwd2", dn2, h1, dh2, -ctx_tiles, nt_all, 0, 3, 4, gate=(f1, 2, 0.5))
    sems_mix, thru_mix, token = _reduce_start("reduce_start_mix", metas[2:6], [g_wmi, g_wpa, g_wpr, g_wmo])

    g_w1o = _mm("ffn1_out_dw", hm1, df1, "tn", BF16, after=token)
    sems_w1o, thru_w1o, token = _reduce_start("reduce_start_ffn1_out", metas[1:2], [g_w1o])
    du1 = _mm_swiglu_bwd("ffn1_out_dx", df1, w1o, ua1, ub1, after=token)
    g_w1i = _mm("ffn1_in_dw", n1, du1, "tn", BF16, stacked=True)
    sems_w1i, thru_w1i, token = _reduce_start("reduce_start_ffn1_in", metas[0:1], [g_w1i])
    dn1 = _mm("ffn1_in_dx", du1, w1i, "nt", F32, after=token, stacked=True)
    dh0, acc_n1 = norm_mod_bwd("norm_mod_bwd1", dn1, h0, dh1, 0, nt_all, 0, 0, 1, out_off=-ctx_tiles)
    grad_x = dh0[None]

    grads_own, landed = [], []
    for name, lo, hi, sems_l, thru_l in (("reduce_wait_ffn1_in", 0, 1, sems_w1i, thru_w1i),
                                         ("reduce_wait_ffn1_out", 1, 2, sems_w1o, thru_w1o),
                                         ("reduce_wait_mix", 2, 6, sems_mix, thru_mix),
                                         ("reduce_wait_ffn2", 6, 8, sems_ffn2, thru_ffn2)):
        grads_l, landed_l = _reduce_wait(name, metas[lo:hi], sems_l, thru_l, dh0)
        grads_own += grads_l
        landed += landed_l
    pieces = [_sum_pieces(m, g, l, s_arr, c_arr) for m, g, l in zip(metas, grads_own, landed)]
    grads_big = _share_halves(metas, pieces)

    zero_row = jnp.zeros((1, d), F32)
    dmod_x = jnp.concatenate([acc_n1[1, 0:1], acc_n1[1, 1:2], acc_n2[1, 2:3], acc_n2[1, 0:1], acc_n2[1, 1:2],
                              acc_n3[1, 2:3], acc_n3[1, 0:1], acc_n3[1, 1:2], dg3], axis=1)
    dmod_c = jnp.concatenate([acc_n1[0, 0:1], acc_n1[0, 1:2], acc_n2[0, 2:3], acc_n2[0, 0:1], acc_n2[0, 1:2]]
                             + [zero_row] * 4, axis=1)
    dlg_row = jnp.pad(dlg[:, :, 0, 0].reshape(1, 2 * n_ret_heads), ((0, 0), (0, LANES_V7X - 2 * n_ret_heads)))
    packed = jnp.concatenate([dmod_x, dmod_c, acc_gq[1, 0:1], acc_gk[0, 0:1] + acc_gk[1, 0:1], dlg_row,
                              g_final, loss_cols], axis=1)
    off_gq = 2 * mod_cols
    off_gk = off_gq + LANES_V7X
    off_lg = off_gk + LANES_V7X
    off_fn = off_lg + LANES_V7X
    off_loss = off_fn + d
    gathered = _gather_row("gather_small", packed)
    logits_row = jnp.pad(ret_decay_logit.reshape(1, 2 * n_ret_heads), ((0, 0), (0, LANES_V7X - 2 * n_ret_heads)))
    totals, g_b_ada, g_decay, loss_row = _small_reduce(gathered, logits_row, mod_cols, off_lg, off_loss, d)
    loss = loss_row[0, 0]

    dm = jnp.concatenate([gathered[:, :mod_cols], totals[:, mod_cols:2 * mod_cols],
                          jnp.zeros((7, mod_cols), F32)], axis=0)
    dm_l = lax.dynamic_slice_in_dim(dm, s_me * ada_cols, ada_cols, axis=1)
    g_w_ada, da_part = _ada_bwd(cg, dm_l, w_ada_l)
    da_rows = _gather_row("gather_dc", da_part[8:9])
    g_c_ctx = _c_ctx_grad(da_rows, c_ctx[None, :])

    def as2d(a):
        return a.reshape(-1, a.shape[-1])

    grads = {
        "c_ctx": g_c_ctx, "w_ada": g_w_ada, "b_ada": g_b_ada,
        "ffn1_w_in": grads_big[0], "ffn1_w_out": grads_big[1], "mix_w_in": grads_big[2],
        "attn_q_gain": totals[:, off_gq:off_gq + HEAD_DIM], "attn_k_gain": totals[:, off_gk:off_gk + HEAD_DIM],
        "ret_decay_logit": g_decay[:, :2 * n_ret_heads],
        "w_proj_attn": grads_big[3], "w_proj_ret": grads_big[4], "mix_w_out": grads_big[5],
        "ffn2_w_in": grads_big[6], "ffn2_w_out": grads_big[7], "final_norm": totals[:, off_fn:off_fn + d],
    }
    weights = {"c_ctx": (c_ctx, m_c_ctx, v_c_ctx), "w_ada": (w_ada, m_w_ada, v_w_ada),
               "b_ada": (b_ada, m_b_ada, v_b_ada), "ffn1_w_in": (ffn1_w_in, m_ffn1_w_in, v_ffn1_w_in),
               "ffn1_w_out": (ffn1_w_out, m_ffn1_w_out, v_ffn1_w_out), "mix_w_in": (mix_w_in, m_mix_w_in, v_mix_w_in),
               "attn_q_gain": (attn_q_gain, m_attn_q_gain, v_attn_q_gain),
               "attn_k_gain": (attn_k_gain, m_attn_k_gain, v_attn_k_gain),
               "ret_decay_logit": (ret_decay_logit, m_ret_decay_logit, v_ret_decay_logit),
               "w_proj_attn": (w_proj_attn, m_w_proj_attn, v_w_proj_attn),
               "w_proj_ret": (w_proj_ret, m_w_proj_ret, v_w_proj_ret), "mix_w_out": (mix_w_out, m_mix_w_out, v_mix_w_out),
               "ffn2_w_in": (ffn2_w_in, m_ffn2_w_in, v_ffn2_w_in), "ffn2_w_out": (ffn2_w_out, m_ffn2_w_out, v_ffn2_w_out),
               "final_norm": (final_norm, m_final_norm, v_final_norm)}
    out_g, out_d, out_m, out_v = [], [], [], []
    for name, (w, m, v) in weights.items():
        shape = w.shape
        if name == "ret_decay_logit":
            w2, m2, v2 = (a.reshape(1, -1) for a in (w, m, v))
        else:
            w2, m2, v2 = as2d(w), as2d(m), as2d(v)
        g2 = grads[name].reshape(w2.shape)
        delta, new_m, new_v = _adamw(w2, g2, m2, v2)
        out_g.append(g2.reshape(shape))
        out_d.append(delta.reshape(shape))
        out_m.append(new_m.reshape(shape))
        out_v.append(new_v.reshape(shape))
    return (loss, grad_x, *out_g, *out_d, *out_m, *out_v)
```
